```python
import jax, jax.numpy as jnp
from jax import lax
import numpy as np

D_MODEL = 1024
BATCH = 8
SEQ = 8192
DEPTH = 1

HG_HEADS = 8
HG_EXPAND = 128
HG_KEY_DIM = HG_HEADS * HG_EXPAND
HG_VAL_DIM = D_MODEL
HG_HEAD_V = HG_VAL_DIM // HG_HEADS
HG_CHUNK = 64
ATT_Q_HEADS = 16
ATT_KV_HEADS = 4
ATT_HEAD_DIM = 64
ATT_GROUP = ATT_Q_HEADS // ATT_KV_HEADS
ATT_WIDTH = ATT_Q_HEADS * ATT_HEAD_DIM
KV_WIDTH = ATT_KV_HEADS * ATT_HEAD_DIM
WINDOW = 128
ATT_BLOCK = WINDOW
EPS = 1e-6

IN_SIZES = [
    HG_KEY_DIM,
    HG_KEY_DIM,
    HG_VAL_DIM,
    HG_VAL_DIM,
    ATT_WIDTH,
    KV_WIDTH,
    KV_WIDTH,
    ATT_WIDTH,
    D_MODEL,
    D_MODEL,
]
D_IN = sum(IN_SIZES)
SPLIT_POINTS = [int(s) for s in np.cumsum(IN_SIZES)[:-1]]

kernel_name = "hybrid_hgrn2_swa_sink_gated_block"


def rms_norm(x, w):
    xf = x.astype(jnp.float32)
    xf = xf * lax.rsqrt(jnp.mean(xf * xf, axis=-1, keepdims=True) + EPS)
    return xf.astype(x.dtype) * w


def hgrn2_chunkwise(q, k, v, log_f):
    B, T, H, dk = q.shape
    dv = v.shape[-1]
    n = T // HG_CHUNK

    def to_chunks(a):
        return a.reshape(B, n, HG_CHUNK, H, a.shape[-1]).transpose(1, 0, 3, 2, 4)

    qc, kc, vc, gc = to_chunks(q), to_chunks(k), to_chunks(v), to_chunks(log_f)
    causal = jnp.tril(jnp.ones((HG_CHUNK, HG_CHUNK), dtype=bool))

    def step(S, inp):
        q_, k_, v_, g_ = inp
        b = jnp.cumsum(g_, axis=2)
        o_inter = jnp.einsum('bhtd,bhdv->bhtv', q_ * jnp.exp(b), S)
        diff = b[:, :, :, None, :] - b[:, :, None, :, :]
        decay = jnp.exp(jnp.where(causal[:, :, None], diff, -jnp.inf))
        scores = jnp.einsum('bhtd,bhsd,bhtsd->bhts', q_, k_, decay)
        o_intra = jnp.einsum('bhts,bhsv->bhtv', scores, v_)
        b_last = b[:, :, -1:, :]
        S_new = jnp.exp(b_last[:, :, 0, :])[..., None] * S + jnp.einsum(
            'bhsd,bhsv->bhdv', k_ * jnp.exp(b_last - b), v_)
        return S_new, o_inter + o_intra

    S0 = jnp.zeros((B, H, dk, dv), jnp.float32)
    _, o = lax.scan(step, S0, (qc, kc, vc, gc))
    return o.transpose(1, 0, 3, 2, 4).reshape(B, T, H, dv)


def sliding_window_attention_with_sinks(q, k, v, sinks):
    B, T = q.shape[0], q.shape[1]
    n = T // ATT_BLOCK
    qb = q.reshape(B, n, ATT_BLOCK, ATT_KV_HEADS, ATT_GROUP, ATT_HEAD_DIM)

    def banded(a):
        ab = a.reshape(B, n, ATT_BLOCK, ATT_KV_HEADS, ATT_HEAD_DIM)
        prev = jnp.pad(ab[:, :-1], ((0, 0), (1, 0), (0, 0), (0, 0), (0, 0)))
        return jnp.concatenate([prev, ab], axis=2)

    keys, vals = banded(k), banded(v)
    scores = jnp.einsum('bnqhgd,bnkhd->bnhgqk', qb, keys).astype(jnp.float32) * (ATT_HEAD_DIM ** -0.5)
    qi = jnp.arange(ATT_BLOCK)[:, None]
    kj = jnp.arange(2 * ATT_BLOCK)[None, :]
    rel = qi + ATT_BLOCK - kj
    band = (rel >= 0) & (rel < WINDOW)
    pad_keys = (jnp.arange(n) == 0)[:, None, None] & (kj < ATT_BLOCK)[None]
    valid = band[None] & ~pad_keys
    scores = jnp.where(valid[None, :, None, None], scores, -jnp.inf)
    sink = sinks.astype(jnp.float32).reshape(ATT_KV_HEADS, ATT_GROUP)[None, None, :, :, None, None]
    m = jnp.maximum(jnp.max(scores, axis=-1, keepdims=True), sink)
    p = jnp.exp(scores - m)
    probs = p / (jnp.sum(p, axis=-1, keepdims=True) + jnp.exp(sink - m))
    out = jnp.einsum('bnhgqk,bnkhd->bnqhgd', probs.astype(v.dtype), vals)
    return out.reshape(B, T, ATT_WIDTH)


def _fwd_setup_inputs(seed: int = 0) -> dict:
    key = jax.random.key(seed)
    ks = jax.random.split(key, 12)
    f32 = jnp.float32
    return {
        "x": jax.random.normal(ks[0], (BATCH, SEQ, D_MODEL), f32),
        "norm_w": 1.0 + 0.02 * jax.random.normal(ks[1], (DEPTH, D_MODEL), f32),
        "w_in": jax.random.normal(ks[2], (DEPTH, D_MODEL, D_IN), f32) * D_MODEL ** -0.5,
        "hgrn_lower_bound": 0.1 * jax.random.normal(ks[3], (DEPTH + 1, HG_KEY_DIM), f32),
        "hgrn_norm_w": 1.0 + 0.02 * jax.random.normal(ks[4], (DEPTH, HG_VAL_DIM), f32),
        "w_branch_hgrn": jax.random.normal(ks[5], (DEPTH, HG_VAL_DIM, D_MODEL), f32) * HG_VAL_DIM ** -0.5,
        "attn_sinks": 0.5 * jax.random.normal(ks[6], (DEPTH, ATT_Q_HEADS), f32),
        "w_branch_attn": jax.random.normal(ks[7], (DEPTH, ATT_WIDTH, D_MODEL), f32) * ATT_WIDTH ** -0.5,
        "w_out": jax.random.normal(ks[8], (DEPTH, D_MODEL, D_MODEL), f32) * D_MODEL ** -0.5,
        "final_norm_w": 1.0 + 0.02 * jax.random.normal(ks[9], (D_MODEL,), f32),
    }


def _fwd_reference(x, norm_w, w_in, hgrn_lower_bound, hgrn_norm_w, w_branch_hgrn, attn_sinks,
              w_branch_attn, w_out, final_norm_w):
    B, T, _ = x.shape
    lb_all = jnp.cumsum(jax.nn.softmax(hgrn_lower_bound.astype(jnp.float32), axis=0), axis=0)
    for l in range(DEPTH):
        xn = rms_norm(x, norm_w[l])
        proj = xn @ w_in[l]
        hq, hf, hi, hg, aq, ak, av, ag, mh, ma = jnp.split(proj, SPLIT_POINTS, axis=-1)

        lb = lb_all[l]
        f = lb + (1.0 - lb) * jax.nn.sigmoid(hf.astype(jnp.float32))
        log_f = jnp.log(f).reshape(B, T, HG_HEADS, HG_EXPAND)
        k_h = (1.0 - f).reshape(B, T, HG_HEADS, HG_EXPAND)
        q_h = jax.nn.silu(hq.astype(jnp.float32)).reshape(B, T, HG_HEADS, HG_EXPAND)
        v_h = hi.astype(jnp.float32).reshape(B, T, HG_HEADS, HG_HEAD_V)
        o_h = hgrn2_chunkwise(q_h, k_h, v_h, log_f)
        o_h = rms_norm(o_h, hgrn_norm_w[l].reshape(HG_HEADS, HG_HEAD_V)).reshape(B, T, HG_VAL_DIM)
        y_h = (o_h.astype(x.dtype) * jax.nn.silu(hg)) @ w_branch_hgrn[l]

        o_a = sliding_window_attention_with_sinks(
            aq.reshape(B, T, ATT_Q_HEADS, ATT_HEAD_DIM),
            ak.reshape(B, T, ATT_KV_HEADS, ATT_HEAD_DIM),
            av.reshape(B, T, ATT_KV_HEADS, ATT_HEAD_DIM),
            attn_sinks[l])
        y_a = (o_a * jax.nn.silu(ag)) @ w_branch_attn[l]

        merged = jax.nn.sigmoid(mh) * y_h + jax.nn.sigmoid(ma) * y_a
        x = x + merged @ w_out[l]
    return rms_norm(x, final_norm_w)


import jax as _jax
import jax.numpy as _jnp

TWIN_FORMAT = 'train_step'
FWD_PARAMS = ['x', 'norm_w', 'w_in', 'hgrn_lower_bound', 'hgrn_norm_w', 'w_branch_hgrn', 'attn_sinks', 'w_branch_attn', 'w_out', 'final_norm_w']
TWIN_WEIGHTS = ['norm_w', 'w_in', 'hgrn_lower_bound', 'hgrn_norm_w', 'w_branch_hgrn', 'attn_sinks', 'w_branch_attn', 'w_out', 'final_norm_w']
TWIN_DIFF_INPUT = 'x'
TWIN_INPUTS = ['x', 'norm_w', 'w_in', 'hgrn_lower_bound', 'hgrn_norm_w', 'w_branch_hgrn', 'attn_sinks', 'w_branch_attn', 'w_out', 'final_norm_w', 'loss_target', 'm_norm_w', 'm_w_in', 'm_hgrn_lower_bound', 'm_hgrn_norm_w', 'm_w_branch_hgrn', 'm_attn_sinks', 'm_w_branch_attn', 'm_w_out', 'm_final_norm_w', 'v_norm_w', 'v_w_in', 'v_hgrn_lower_bound', 'v_hgrn_norm_w', 'v_w_branch_hgrn', 'v_attn_sinks', 'v_w_branch_attn', 'v_w_out', 'v_final_norm_w']
TWIN_OUTPUTS = ['loss', 'grad_x', 'grad_norm_w', 'grad_w_in', 'grad_hgrn_lower_bound', 'grad_hgrn_norm_w', 'grad_w_branch_hgrn', 'grad_attn_sinks', 'grad_w_branch_attn', 'grad_w_out', 'grad_final_norm_w', 'delta_norm_w', 'delta_w_in', 'delta_hgrn_lower_bound', 'delta_hgrn_norm_w', 'delta_w_branch_hgrn', 'delta_attn_sinks', 'delta_w_branch_attn', 'delta_w_out', 'delta_final_norm_w', 'new_m_norm_w', 'new_m_w_in', 'new_m_hgrn_lower_bound', 'new_m_hgrn_norm_w', 'new_m_w_branch_hgrn', 'new_m_attn_sinks', 'new_m_w_branch_attn', 'new_m_w_out', 'new_m_final_norm_w', 'new_v_norm_w', 'new_v_w_in', 'new_v_hgrn_lower_bound', 'new_v_hgrn_norm_w', 'new_v_w_branch_hgrn', 'new_v_attn_sinks', 'new_v_w_branch_attn', 'new_v_w_out', 'new_v_final_norm_w']
TWIN_LEAF_KINDS = {'loss': 'loss', 'grad_x': 'grad_x', 'grad_norm_w': 'grad_w', 'grad_w_in': 'grad_w', 'grad_hgrn_lower_bound': 'grad_w', 'grad_hgrn_norm_w': 'grad_w', 'grad_w_branch_hgrn': 'grad_w', 'grad_attn_sinks': 'grad_w', 'grad_w_branch_attn': 'grad_w', 'grad_w_out': 'grad_w', 'grad_final_norm_w': 'grad_w', 'delta_norm_w': 'delta_w', 'delta_w_in': 'delta_w', 'delta_hgrn_lower_bound': 'delta_w', 'delta_hgrn_norm_w': 'delta_w', 'delta_w_branch_hgrn': 'delta_w', 'delta_attn_sinks': 'delta_w', 'delta_w_branch_attn': 'delta_w', 'delta_w_out': 'delta_w', 'delta_final_norm_w': 'delta_w', 'new_m_norm_w': 'new_m', 'new_m_w_in': 'new_m', 'new_m_hgrn_lower_bound': 'new_m', 'new_m_hgrn_norm_w': 'new_m', 'new_m_w_branch_hgrn': 'new_m', 'new_m_attn_sinks': 'new_m', 'new_m_w_branch_attn': 'new_m', 'new_m_w_out': 'new_m', 'new_m_final_norm_w': 'new_m', 'new_v_norm_w': 'new_v', 'new_v_w_in': 'new_v', 'new_v_hgrn_lower_bound': 'new_v', 'new_v_hgrn_norm_w': 'new_v', 'new_v_w_branch_hgrn': 'new_v', 'new_v_attn_sinks': 'new_v', 'new_v_w_branch_attn': 'new_v', 'new_v_w_out': 'new_v', 'new_v_final_norm_w': 'new_v'}


def _forward(args):
    return _fwd_reference(*[args[k] for k in FWD_PARAMS])


def _output_shape():
    def fwd():
        inp = _fwd_setup_inputs(0)
        return _fwd_reference(*[inp[k] for k in FWD_PARAMS])
    out = _jax.eval_shape(fwd)
    return out.shape, out.dtype

N_MICROBATCH = 1
ADAM_LR = 0.001
ADAM_B1 = 0.9
ADAM_B2 = 0.999
ADAM_EPS = 1e-08
ADAM_WD = 0.01
ADAM_STEP = 10
PER_EXAMPLE_BATCH_AXIS = {'x': 0, 'loss_target': 0}
SHARED_INPUTS = []
_WEIGHT_DTYPES = {'norm_w': _jnp.float32, 'w_in': _jnp.float32, 'hgrn_lower_bound': _jnp.float32, 'hgrn_norm_w': _jnp.float32, 'w_branch_hgrn': _jnp.float32, 'attn_sinks': _jnp.float32, 'w_branch_attn': _jnp.float32, 'w_out': _jnp.float32, 'final_norm_w': _jnp.float32}
MOMENT_SCALE = {'norm_w': 1.208803e-01, 'w_in': 4.079551e-02, 'hgrn_lower_bound': 7.614466e-03, 'hgrn_norm_w': 8.195984e-02, 'w_branch_hgrn': 7.687394e-02, 'attn_sinks': 1.470779e-02, 'w_branch_attn': 1.553212e-02, 'w_out': 7.806072e-02, 'final_norm_w': 6.403149e+01}


def _to_microbatches(a, axis):
    t = _jnp.moveaxis(a, axis, 0)
    t = t.reshape((N_MICROBATCH, t.shape[0] // N_MICROBATCH) + t.shape[1:])
    return _jnp.moveaxis(t, 1, axis + 1)


def setup_inputs(seed: int = 0) -> dict:
    inp = _fwd_setup_inputs(seed)
    key = _jax.random.fold_in(_jax.random.key(seed), 7919)
    shape, _ = _output_shape()
    out = dict(inp)
    out["loss_target"] = _jax.random.normal(_jax.random.fold_in(key, 0), shape, _jnp.float32)
    for i, name in enumerate(TWIN_WEIGHTS):
        w = inp[name].astype(_jnp.float32)
        if MOMENT_SCALE is None:
            s = _jnp.sqrt(_jnp.mean(_jnp.square(w)) + 1e-30)
        else:
            s = MOMENT_SCALE[name]
        km, kv = _jax.random.split(_jax.random.fold_in(key, i + 1))
        out[name] = w
        out["m_" + name] = s * _jax.random.normal(km, w.shape, _jnp.float32)
        out["v_" + name] = (s * s) * _jax.random.uniform(kv, w.shape, _jnp.float32, 0.5, 1.5)
    if N_MICROBATCH > 1:
        for name, axis in PER_EXAMPLE_BATCH_AXIS.items():
            out[name] = _to_microbatches(out[name], axis)
    return {'x': out['x'], 'norm_w': out['norm_w'], 'w_in': out['w_in'], 'hgrn_lower_bound': out['hgrn_lower_bound'], 'hgrn_norm_w': out['hgrn_norm_w'], 'w_branch_hgrn': out['w_branch_hgrn'], 'attn_sinks': out['attn_sinks'], 'w_branch_attn': out['w_branch_attn'], 'w_out': out['w_out'], 'final_norm_w': out['final_norm_w'], 'loss_target': out['loss_target'], 'm_norm_w': out['m_norm_w'], 'm_w_in': out['m_w_in'], 'm_hgrn_lower_bound': out['m_hgrn_lower_bound'], 'm_hgrn_norm_w': out['m_hgrn_norm_w'], 'm_w_branch_hgrn': out['m_w_branch_hgrn'], 'm_attn_sinks': out['m_attn_sinks'], 'm_w_branch_attn': out['m_w_branch_attn'], 'm_w_out': out['m_w_out'], 'm_final_norm_w': out['m_final_norm_w'], 'v_norm_w': out['v_norm_w'], 'v_w_in': out['v_w_in'], 'v_hgrn_lower_bound': out['v_hgrn_lower_bound'], 'v_hgrn_norm_w': out['v_hgrn_norm_w'], 'v_w_branch_hgrn': out['v_w_branch_hgrn'], 'v_attn_sinks': out['v_attn_sinks'], 'v_w_branch_attn': out['v_w_branch_attn'], 'v_w_out': out['v_w_out'], 'v_final_norm_w': out['v_final_norm_w']}


def _loss(weights, diff, rest, loss_target):
    with _jax.named_scope("forward"):
        args = {**rest, TWIN_DIFF_INPUT: diff, **{k: w.astype(_WEIGHT_DTYPES[k]) for k, w in weights.items()}}
        y = _forward(args)
    with _jax.named_scope("loss_head"):
        err = _jnp.square(y.astype(_jnp.float32) - loss_target)
        return 0.5 * _jnp.sum(_jnp.mean(err, axis=-1)) if err.ndim else 0.5 * err


def _adamw(w, g, m, v):
    m = ADAM_B1 * m + (1.0 - ADAM_B1) * g
    v = ADAM_B2 * v + (1.0 - ADAM_B2) * _jnp.square(g)
    m_hat = m / (1.0 - ADAM_B1 ** ADAM_STEP)
    v_hat = v / (1.0 - ADAM_B2 ** ADAM_STEP)
    delta = -ADAM_LR * (m_hat / (_jnp.sqrt(v_hat) + ADAM_EPS) + ADAM_WD * w)
    return delta, m, v


def reference(x, norm_w, w_in, hgrn_lower_bound, hgrn_norm_w, w_branch_hgrn, attn_sinks, w_branch_attn, w_out, final_norm_w, loss_target, m_norm_w, m_w_in, m_hgrn_lower_bound, m_hgrn_norm_w, m_w_branch_hgrn, m_attn_sinks, m_w_branch_attn, m_w_out, m_final_norm_w, v_norm_w, v_w_in, v_hgrn_lower_bound, v_hgrn_norm_w, v_w_branch_hgrn, v_attn_sinks, v_w_branch_attn, v_w_out, v_final_norm_w):
    given = dict(x=x, norm_w=norm_w, w_in=w_in, hgrn_lower_bound=hgrn_lower_bound, hgrn_norm_w=hgrn_norm_w, w_branch_hgrn=w_branch_hgrn, attn_sinks=attn_sinks, w_branch_attn=w_branch_attn, w_out=w_out, final_norm_w=final_norm_w, loss_target=loss_target, m_norm_w=m_norm_w, m_w_in=m_w_in, m_hgrn_lower_bound=m_hgrn_lower_bound, m_hgrn_norm_w=m_hgrn_norm_w, m_w_branch_hgrn=m_w_branch_hgrn, m_attn_sinks=m_attn_sinks, m_w_branch_attn=m_w_branch_attn, m_w_out=m_w_out, m_final_norm_w=m_final_norm_w, v_norm_w=v_norm_w, v_w_in=v_w_in, v_hgrn_lower_bound=v_hgrn_lower_bound, v_hgrn_norm_w=v_hgrn_norm_w, v_w_branch_hgrn=v_w_branch_hgrn, v_attn_sinks=v_attn_sinks, v_w_branch_attn=v_w_branch_attn, v_w_out=v_w_out, v_final_norm_w=v_final_norm_w)
    weights = {n: given[n] for n in TWIN_WEIGHTS}
    shared = {n: given[n] for n in SHARED_INPUTS}
    per_example = {n: given[n] for n in ['x']}
    grad_fn = _jax.value_and_grad(_loss, argnums=(0, 1))

    def one_microbatch(ex, loss_target):
        ex = dict(ex)
        diff = ex.pop(TWIN_DIFF_INPUT)
        return grad_fn(weights, diff, {**shared, **ex}, loss_target)

    if N_MICROBATCH == 1:
        loss, (grad_w, grad_x) = one_microbatch(per_example, given["loss_target"])
    else:
        def body(carry, xs):
            loss_sum, grad_sum = carry
            l_k, (gw_k, gx_k) = one_microbatch(xs[0], xs[1])
            with _jax.named_scope("update"):
                return (loss_sum + l_k, _jax.tree.map(_jnp.add, grad_sum, gw_k)), gx_k

        init = (_jnp.zeros((), _jnp.float32), _jax.tree.map(_jnp.zeros_like, weights))
        (loss, grad_w), grad_x = _jax.lax.scan(body, init, (per_example, given["loss_target"]))
    with _jax.named_scope("update"):
        delta_w, new_m, new_v = {}, {}, {}
        for n in TWIN_WEIGHTS:
            delta_w[n], new_m[n], new_v[n] = _adamw(weights[n], grad_w[n], given["m_" + n], given["v_" + n])
    return (loss, grad_x, *[grad_w[n] for n in TWIN_WEIGHTS], *[delta_w[n] for n in TWIN_WEIGHTS],
            *[new_m[n] for n in TWIN_WEIGHTS], *[new_v[n] for n in TWIN_WEIGHTS])
```

```python
import functools

import jax
import jax.numpy as jnp
from jax import lax
from jax.experimental import pallas as pl
from jax.experimental.pallas import tpu as pltpu

F32 = jnp.float32
BF16 = jnp.bfloat16

D = 1024
D_IN = 8704
SHARDS = 4
SHARD_W = D_IN // SHARDS
SQ_ROWS = D // SHARDS
HEADS = 8
HEAD_W = 128
CHUNK = 64
ATT_BLOCK = 128
KV_HEADS = 4
HEAD_DIM = 64
EPS = 1e-6
NEG = -1e30
SCALE = HEAD_DIM ** -0.5
COL_HG, COL_AQ, COL_AK, COL_AV, COL_AG, COL_MH, COL_MA = 3072, 4096, 5120, 5376, 5632, 6656, 7680

ADAM_LR, ADAM_B1, ADAM_B2, ADAM_EPS, ADAM_WD, ADAM_STEP = 0.001, 0.9, 0.999, 1e-08, 0.01, 10

VMEM_LIMIT = 56 * 1024 * 1024
MESH = pl.DeviceIdType.MESH
HBM_SPEC = pl.BlockSpec(memory_space=pltpu.HBM)
CHIP_FLIPS = ((1, 0), (0, 1), (1, 1))


def _dot(a, b):
    return jnp.dot(a, b, preferred_element_type=F32)


def _dot_nt(a, b):
    return lax.dot_general(a, b, (((1,), (1,)), ((), ())), preferred_element_type=F32)


def _dot_tn(a, b):
    return lax.dot_general(a, b, (((0,), (0,)), ((), ())), preferred_element_type=F32)


def _sigmoid(v):
    return 1.0 / (1.0 + jnp.exp(-v))


def _bf(v):
    return v.astype(BF16)


def _split3(v):
    a = _bf(v)
    r = v - a.astype(F32)
    b = _bf(r)
    c = _bf(r - b.astype(F32))
    return a, b, c


def _tri_dot(tri, v):
    a, b, c = _split3(v)
    return _dot(tri, a) + _dot(tri, b) + _dot(tri, c)


def _params(sem=None):
    return pltpu.CompilerParams(dimension_semantics=sem, vmem_limit_bytes=VMEM_LIMIT)


def _cast_shards(win_s, wbh_s, wba_s, wout_s):
    def body(win_ref, a_ref, b_ref, c_ref, win_o, sq_o):
        win_o[...] = _bf(win_ref[...])
        sq_o[0:SQ_ROWS, :] = _bf(a_ref[...])
        sq_o[SQ_ROWS:2 * SQ_ROWS, :] = _bf(b_ref[...])
        sq_o[2 * SQ_ROWS:3 * SQ_ROWS, :] = _bf(c_ref[...])

    return pl.pallas_call(
        body, name="cast_shards",
        out_shape=[jax.ShapeDtypeStruct((D, SHARD_W), BF16), jax.ShapeDtypeStruct((3 * SQ_ROWS, D), BF16)],
        compiler_params=_params(),
    )(win_s, wbh_s, wba_s, wout_s)


def _fwd_proj(x, norm_w, win_bf):
    T = x.shape[0]
    tm = min(512, T)

    def body(x_ref, nw_ref, w_ref, proj_ref, xn_ref):
        @pl.when(pl.program_id(1) == 0)
        def _():
            xf = x_ref[...]
            rs = lax.rsqrt(jnp.mean(xf * xf, axis=1, keepdims=True) + EPS)
            xn_ref[...] = _bf((xf * rs) * nw_ref[...])
        proj_ref[...] = _dot(xn_ref[...], w_ref[...])

    return pl.pallas_call(
        body, name="fwd_proj", grid=(T // tm, SHARDS),
        in_specs=[pl.BlockSpec((tm, D), lambda i, j: (i, 0)),
                  pl.BlockSpec((1, D), lambda i, j: (0, 0)),
                  pl.BlockSpec((None, D, SHARD_W), lambda i, j: (j, 0, 0))],
        out_specs=[pl.BlockSpec((tm, SHARD_W), lambda i, j: (i, j)),
                   pl.BlockSpec((tm, D), lambda i, j: (i, 0))],
        out_shape=[jax.ShapeDtypeStruct((T, D_IN), F32), jax.ShapeDtypeStruct((T, D), BF16)],
        compiler_params=_params(("parallel", "arbitrary")),
    )(x, norm_w, win_bf)


def _hgrn_gates(hq_ref, hf_ref, lbw_ref, b_scr):
    lb = 1.0 / (1.0 + jnp.exp(lbw_ref[1:2, :] - lbw_ref[0:1, :]))
    hf = hf_ref[...]
    sig = _sigmoid(hf)
    f = lb + (1.0 - lb) * sig
    g = jnp.log(f)
    hq = hq_ref[...]
    sq = _sigmoid(hq)
    q = hq * sq
    row = lax.broadcasted_iota(jnp.int32, (CHUNK, CHUNK), 0)
    col = lax.broadcasted_iota(jnp.int32, (CHUNK, CHUNK), 1)
    causal = row >= col
    b = _tri_dot(jnp.where(causal, 1.0, 0.0).astype(BF16), g)
    b_scr[...] = b
    bc = b_scr[CHUNK - 1:CHUNK, :]
    r = b_scr[CHUNK // 2 - 1:CHUNK // 2, :]
    return dict(lb=lb, sig=sig, f=f, k=1.0 - f, hq=hq, sq=sq, q=q, b=b, bc=bc, r=r, causal=causal)


def _hgrn_fwd(proj, lbw):
    T = proj.shape[0]
    n = T // CHUNK

    def body(hq_ref, hf_ref, hi_ref, lbw_ref, o_ref, st_ref, s_scr, b_scr):
        @pl.when(pl.program_id(0) == 0)
        def _():
            s_scr[...] = jnp.zeros_like(s_scr)

        gt = _hgrn_gates(hq_ref, hf_ref, lbw_ref, b_scr)
        b, bc, r, q, k = gt["b"], gt["bc"], gt["r"], gt["q"], gt["k"]
        qe = _bf(q * jnp.exp(b))
        qr = _bf(q * jnp.exp(b - r))
        kr = _bf(k * jnp.exp(r - b))
        kl = _bf(k * jnp.exp(bc - b))
        ebc = jnp.exp(bc)
        v = _bf(hi_ref[...])
        for h in range(HEADS):
            sl = slice(h * HEAD_W, (h + 1) * HEAD_W)
            st = s_scr[h]
            st_ref[0, h] = st
            a = jnp.where(gt["causal"], _dot_nt(qr[:, sl], kr[:, sl]), 0.0)
            o_ref[:, sl] = _dot(_bf(a), v[:, sl]) + _dot_nt(qe[:, sl], _bf(st))
            s_scr[h] = ebc[:, sl] * st + _dot_tn(v[:, sl], kl[:, sl])

    col = lambda j: pl.BlockSpec((CHUNK, D), lambda i: (i, j))
    return pl.pallas_call(
        body, name="hgrn_fwd", grid=(n,),
        in_specs=[col(0), col(1), col(2), pl.BlockSpec((2, D), lambda i: (0, 0))],
        out_specs=[pl.BlockSpec((CHUNK, D), lambda i: (i, 0)),
                   pl.BlockSpec((1, HEADS, HEAD_W, HEAD_W), lambda i: (i, 0, 0, 0))],
        out_shape=[jax.ShapeDtypeStruct((T, D), F32), jax.ShapeDtypeStruct((n, HEADS, HEAD_W, HEAD_W), F32)],
        scratch_shapes=[pltpu.VMEM((HEADS, HEAD_W, HEAD_W), F32), pltpu.VMEM((CHUNK, D), F32)],
        compiler_params=_params(("arbitrary",)),
    )(proj, proj, proj, lbw)


def _hgrn_bwd(proj, lbw, states, do):
    T = proj.shape[0]
    n = T // CHUNK

    def body(hq_ref, hf_ref, hi_ref, lbw_ref, st_ref, do_ref, dp_ref, dlb_ref,
             ds_scr, b_scr, dq_scr, dk_scr, dv_scr, late_scr, early_scr, ex_scr):
        @pl.when(pl.program_id(0) == 0)
        def _():
            ds_scr[...] = jnp.zeros_like(ds_scr)
            dlb_ref[...] = jnp.zeros_like(dlb_ref)

        gt = _hgrn_gates(hq_ref, hf_ref, lbw_ref, b_scr)
        b, bc, r, q, k = gt["b"], gt["bc"], gt["r"], gt["q"], gt["k"]
        eb = jnp.exp(b)
        er = jnp.exp(b - r)
        erk = jnp.exp(r - b)
        el = jnp.exp(bc - b)
        ebc = jnp.exp(bc)
        qe, qr, kr, kl = _bf(q * eb), _bf(q * er), _bf(k * erk), _bf(k * el)
        v = _bf(hi_ref[...])
        do_b = _bf(do_ref[...])
        for h in range(HEADS):
            sl = slice(h * HEAD_W, (h + 1) * HEAD_W)
            st0 = st_ref[0, h]
            dst = ds_scr[h]
            dst_b = _bf(dst)
            a = _bf(jnp.where(gt["causal"], _dot_nt(qr[:, sl], kr[:, sl]), 0.0))
            da = _bf(jnp.where(gt["causal"], _dot_nt(do_b[:, sl], v[:, sl]), 0.0))
            mq = _dot(da, kr[:, sl])
            mk = _dot_tn(da, qr[:, sl])
            dq_in = eb[:, sl] * _dot(do_b[:, sl], _bf(st0))
            dk_in = el[:, sl] * _dot(v[:, sl], dst_b)
            dq_scr[:, sl] = er[:, sl] * mq + dq_in
            dk_scr[:, sl] = erk[:, sl] * mk + dk_in
            dv_scr[:, sl] = _dot_tn(a, do_b[:, sl]) + _dot_nt(kl[:, sl], dst_b)
            late_scr[:, sl] = q[:, sl] * dq_in + qr[:, sl].astype(F32) * mq - kr[:, sl].astype(F32) * mk
            early_scr[:, sl] = k[:, sl] * dk_in
            ex_scr[:, sl] = jnp.sum(dst * st0, axis=0, keepdims=True)
            ds_scr[h] = ebc[:, sl] * dst + _dot_tn(do_b[:, sl], qe[:, sl])

        dq, dk = dq_scr[...], dk_scr[...]
        row = lax.broadcasted_iota(jnp.int32, (CHUNK, CHUNK), 0)
        col = lax.broadcasted_iota(jnp.int32, (CHUNK, CHUNK), 1)
        at_or_after = jnp.where(col >= row, 1.0, 0.0).astype(BF16)
        before = jnp.where(col < row, 1.0, 0.0).astype(BF16)
        dg = _tri_dot(at_or_after, late_scr[...]) + _tri_dot(before, early_scr[...]) + ebc * ex_scr[...]
        df = dg / gt["f"] - dk
        sig, sq, hq, lb = gt["sig"], gt["sq"], gt["hq"], gt["lb"]
        dp_ref[:, 0:D] = _bf(dq * (sq * (1.0 + hq * (1.0 - sq))))
        dp_ref[:, D:2 * D] = _bf(df * (1.0 - lb) * sig * (1.0 - sig))
        dp_ref[:, 2 * D:3 * D] = _bf(dv_scr[...])
        dlb_ref[...] += jnp.sum(df * (1.0 - sig), axis=0, keepdims=True)

    col = lambda j: pl.BlockSpec((CHUNK, D), lambda i: (n - 1 - i, j))
    return pl.pallas_call(
        body, name="hgrn_bwd", grid=(n,),
        in_specs=[col(0), col(1), col(2), pl.BlockSpec((2, D), lambda i: (0, 0)),
                  pl.BlockSpec((1, HEADS, HEAD_W, HEAD_W), lambda i: (n - 1 - i, 0, 0, 0)),
                  pl.BlockSpec((CHUNK, D), lambda i: (n - 1 - i, 0))],
        out_specs=[pl.BlockSpec((CHUNK, 3 * D), lambda i: (n - 1 - i, 0)),
                   pl.BlockSpec((1, D), lambda i: (0, 0))],
        out_shape=[jax.ShapeDtypeStruct((T, 3 * D), BF16), jax.ShapeDtypeStruct((1, D), F32)],
        scratch_shapes=[pltpu.VMEM((HEADS, HEAD_W, HEAD_W), F32)] + [pltpu.VMEM((CHUNK, D), F32)] * 6
                       + [pltpu.VMEM((1, D), F32)],
        compiler_params=_params(("arbitrary",)),
    )(proj, proj, proj, lbw, states, do)


def _attn_masks(blk):
    qi = lax.broadcasted_iota(jnp.int32, (ATT_BLOCK, 2 * ATT_BLOCK), 0)
    kj = lax.broadcasted_iota(jnp.int32, (ATT_BLOCK, 2 * ATT_BLOCK), 1)
    band = (kj > qi) & (kj <= qi + ATT_BLOCK)
    return band & ((blk > 0) | (kj >= ATT_BLOCK))


def _both_halves(t, hp, low):
    return jnp.where(low if hp == 0 else jnp.logical_not(low), t, pltpu.roll(t, HEAD_DIM, 1))


def _attn_probs(qm, kh, sink, valid):
    s = _dot_nt(qm, kh) * SCALE
    s = jnp.where(valid, s, NEG)
    m = jnp.maximum(jnp.max(s, axis=1, keepdims=True), sink)
    p = jnp.exp(s - m)
    es = jnp.exp(sink - m)
    den = jnp.sum(p, axis=1, keepdims=True) + es
    return p / den, es / den


def _attn_fwd(proj, sinks):
    T = proj.shape[0]
    nb = T // ATT_BLOCK

    def body(sink_ref, q_ref, kp_ref, kc_ref, vp_ref, vc_ref, o_ref):
        blk = pl.program_id(0)
        valid = _attn_masks(blk)
        low = lax.broadcasted_iota(jnp.int32, (1, 2 * HEAD_DIM), 1) < HEAD_DIM
        kcat = jnp.concatenate([kp_ref[...], kc_ref[...]], axis=0)
        vcat = jnp.concatenate([vp_ref[...], vc_ref[...]], axis=0)
        for h in range(KV_HEADS):
            tl = slice((h // 2) * 128, (h // 2) * 128 + 128)
            kh = _bf(_both_halves(kcat[:, tl], h % 2, low))
            vh = _bf(_both_halves(vcat[:, tl], h % 2, low))
            for t in range(2):
                ql = slice((2 * h + t) * 128, (2 * h + t) * 128 + 128)
                q2 = q_ref[:, ql]
                outs = []
                for p in range(2):
                    qm = _bf(jnp.where(low if p == 0 else jnp.logical_not(low), q2, 0.0))
                    probs, _ = _attn_probs(qm, kh, sink_ref[0, 4 * h + 2 * t + p], valid)
                    outs.append(_dot(_bf(probs), vh))
                o_ref[:, ql] = jnp.where(low, outs[0], outs[1])

    prev = lambda i: jnp.maximum(i - 1, 0)
    return pl.pallas_call(
        body, name="attn_fwd", grid=(nb,),
        in_specs=[pl.BlockSpec(memory_space=pltpu.SMEM),
                  pl.BlockSpec((ATT_BLOCK, D), lambda i: (i, COL_AQ // D)),
                  pl.BlockSpec((ATT_BLOCK, 256), lambda i: (prev(i), COL_AK // 256)),
                  pl.BlockSpec((ATT_BLOCK, 256), lambda i: (i, COL_AK // 256)),
                  pl.BlockSpec((ATT_BLOCK, 256), lambda i: (prev(i), COL_AV // 256)),
                  pl.BlockSpec((ATT_BLOCK, 256), lambda i: (i, COL_AV // 256))],
        out_specs=pl.BlockSpec((ATT_BLOCK, D), lambda i: (i, 0)),
        out_shape=jax.ShapeDtypeStruct((T, D), F32),
        compiler_params=_params(("arbitrary",)),
    )(sinks, proj, proj, proj, proj, proj)


def _attn_bwd(proj, sinks, o, do):
    T = proj.shape[0]
    nb = T // ATT_BLOCK

    def body(sink_ref, q_ref, kp_ref, kc_ref, vp_ref, vc_ref, o_ref, do_ref,
             dq_ref, dk_ref, dv_ref, dsink_ref, ck_scr, cv_scr, nk_scr, nv_scr):
        blk = pl.program_id(0)

        @pl.when(blk == 0)
        def _():
            ck_scr[...] = jnp.zeros_like(ck_scr)
            cv_scr[...] = jnp.zeros_like(cv_scr)
            dsink_ref[...] = jnp.zeros_like(dsink_ref)

        @pl.when(blk < nb)
        def _():
            valid = _attn_masks(blk)
            low = lax.broadcasted_iota(jnp.int32, (1, 2 * HEAD_DIM), 1) < HEAD_DIM
            kcat = jnp.concatenate([kp_ref[...], kc_ref[...]], axis=0)
            vcat = jnp.concatenate([vp_ref[...], vc_ref[...]], axis=0)
            for h in range(KV_HEADS):
                tl = slice((h // 2) * 128, (h // 2) * 128 + 128)
                kh = _bf(_both_halves(kcat[:, tl], h % 2, low))
                vh = _bf(_both_halves(vcat[:, tl], h % 2, low))
                dk2 = jnp.zeros((2 * ATT_BLOCK, 128), F32)
                dv2 = jnp.zeros((2 * ATT_BLOCK, 128), F32)
                for t in range(2):
                    ql = slice((2 * h + t) * 128, (2 * h + t) * 128 + 128)
                    q2, o2, do2 = q_ref[:, ql], o_ref[:, ql], do_ref[:, ql]
                    dqs = []
                    for p in range(2):
                        head = 4 * h + 2 * t + p
                        mine = low if p == 0 else jnp.logical_not(low)
                        qm = _bf(jnp.where(mine, q2, 0.0))
                        dom = _bf(jnp.where(mine, do2, 0.0))
                        dsum = jnp.sum(jnp.where(mine, do2 * o2, 0.0), axis=1, keepdims=True)
                        probs, psink = _attn_probs(qm, kh, sink_ref[0, head], valid)
                        dp = _dot_nt(dom, vh)
                        ds = _bf(probs * (dp - dsum) * SCALE)
                        dqs.append(_dot(ds, kh))
                        dk2 = dk2 + _dot_tn(ds, qm)
                        dv2 = dv2 + _dot_tn(_bf(probs), dom)
                        dsink_ref[head:head + 1, :] += jnp.zeros((1, 128), F32) - jnp.sum(psink * dsum, axis=0, keepdims=True)
                    dq_ref[:, ql] = _bf(jnp.where(low, dqs[0], dqs[1]))
                dk2 = dk2 + pltpu.roll(dk2, HEAD_DIM, 1)
                dv2 = dv2 + pltpu.roll(dv2, HEAD_DIM, 1)
                if h % 2 == 0:
                    keep_k, keep_v = dk2, dv2
                else:
                    nk_scr[:, tl] = jnp.where(low, keep_k, dk2)
                    nv_scr[:, tl] = jnp.where(low, keep_v, dv2)
            dk_ref[...] = _bf(ck_scr[...] + nk_scr[0:ATT_BLOCK, :])
            dv_ref[...] = _bf(cv_scr[...] + nv_scr[0:ATT_BLOCK, :])
            ck_scr[...] = nk_scr[ATT_BLOCK:2 * ATT_BLOCK, :]
            cv_scr[...] = nv_scr[ATT_BLOCK:2 * ATT_BLOCK, :]

        @pl.when(blk == nb)
        def _():
            dk_ref[...] = _bf(ck_scr[...])
            dv_ref[...] = _bf(cv_scr[...])

    cur = lambda i: jnp.minimum(i, nb - 1)
    prev = lambda i: jnp.maximum(cur(i) - 1, 0)
    late = lambda i: jnp.maximum(i - 1, 0)
    dq, dk, dv, dsink = pl.pallas_call(
        body, name="attn_bwd", grid=(nb + 1,),
        in_specs=[pl.BlockSpec(memory_space=pltpu.SMEM),
                  pl.BlockSpec((ATT_BLOCK, D), lambda i: (cur(i), COL_AQ // D)),
                  pl.BlockSpec((ATT_BLOCK, 256), lambda i: (prev(i), COL_AK // 256)),
                  pl.BlockSpec((ATT_BLOCK, 256), lambda i: (cur(i), COL_AK // 256)),
                  pl.BlockSpec((ATT_BLOCK, 256), lambda i: (prev(i), COL_AV // 256)),
                  pl.BlockSpec((ATT_BLOCK, 256), lambda i: (cur(i), COL_AV // 256)),
                  pl.BlockSpec((ATT_BLOCK, D), lambda i: (cur(i), 0)),
                  pl.BlockSpec((ATT_BLOCK, D), lambda i: (cur(i), 0))],
        out_specs=[pl.BlockSpec((ATT_BLOCK, D), lambda i: (cur(i), 0)),
                   pl.BlockSpec((ATT_BLOCK, 256), lambda i: (late(i), 0)),
                   pl.BlockSpec((ATT_BLOCK, 256), lambda i: (late(i), 0)),
                   pl.BlockSpec((16, 128), lambda i: (0, 0))],
        out_shape=[jax.ShapeDtypeStruct((T, D), BF16), jax.ShapeDtypeStruct((T, 256), BF16),
                   jax.ShapeDtypeStruct((T, 256), BF16), jax.ShapeDtypeStruct((16, 128), F32)],
        scratch_shapes=[pltpu.VMEM((ATT_BLOCK, 256), F32), pltpu.VMEM((ATT_BLOCK, 256), F32),
                        pltpu.VMEM((2 * ATT_BLOCK, 256), F32), pltpu.VMEM((2 * ATT_BLOCK, 256), F32)],
        compiler_params=_params(("arbitrary",)),
    )(sinks, proj, proj, proj, proj, proj, o, do)
    return dq, dk, dv, dsink


def _mid(x, tgt, proj, oh, oa, hnw, fnw, wsq_bf):
    T = x.shape[0]
    tm = min(128, T)
    nt = T // tm

    def body(x_ref, tgt_ref, oh_ref, oa_ref, hg_ref, ag0_ref, ag1_ref, mh0_ref, mh1_ref, ma0_ref, ma1_ref,
             hnw_ref, fnw_ref, w_hbm,
             dx2_ref, doh_ref, doa_ref, dhg_ref, dtail_ref, loss_ref, vec_ref, gw_hbm,
             w_scr, gw_scr, xh_scr, rs_scr, sem):
        i = pl.program_id(0)

        @pl.when(i == 0)
        def _():
            cp = pltpu.make_async_copy(w_hbm, w_scr, sem)
            cp.start()
            cp.wait()
            gw_scr[...] = jnp.zeros_like(gw_scr)
            loss_ref[...] = jnp.zeros_like(loss_ref)
            vec_ref[...] = jnp.zeros_like(vec_ref)

        oh = oh_ref[...]
        for h in range(HEADS):
            sl = slice(h * HEAD_W, (h + 1) * HEAD_W)
            ohh = oh[:, sl]
            rs = lax.rsqrt(jnp.mean(ohh * ohh, axis=1, keepdims=True) + EPS)
            xh_scr[:, sl] = ohh * rs
            rs_scr[:, sl] = jnp.broadcast_to(rs, (tm, HEAD_W))
        xh = xh_scr[...]
        hnw = hnw_ref[...]
        on = xh * hnw
        hg = hg_ref[...]
        sg = _sigmoid(hg)
        silu_g = hg * sg
        gated_h = _bf(on * silu_g)
        yh = _dot(gated_h, w_scr[0])
        oa = oa_ref[...]
        ag = jnp.concatenate([ag0_ref[...], ag1_ref[...]], axis=1)
        sa = _sigmoid(ag)
        silu_a = ag * sa
        gated_a = _bf(oa * silu_a)
        ya = _dot(gated_a, w_scr[1])
        smh = _sigmoid(jnp.concatenate([mh0_ref[...], mh1_ref[...]], axis=1))
        sma = _sigmoid(jnp.concatenate([ma0_ref[...], ma1_ref[...]], axis=1))
        merged = _bf(smh * yh + sma * ya)
        x2 = x_ref[...] + _dot(merged, w_scr[2])
        rs2 = lax.rsqrt(jnp.mean(x2 * x2, axis=1, keepdims=True) + EPS)
        xh2 = x2 * rs2
        fnw = fnw_ref[...]
        diff = xh2 * fnw - tgt_ref[...]
        loss_ref[...] += jnp.zeros_like(loss_ref) + jnp.sum(diff * diff) * (0.5 / D)

        dy = diff * (1.0 / D)
        vec_ref[0:1, :] += jnp.sum(dy * xh2, axis=0, keepdims=True)
        gy = dy * fnw
        dx2 = rs2 * (gy - xh2 * jnp.mean(gy * xh2, axis=1, keepdims=True))
        dx2_ref[...] = dx2
        dx2_b = _bf(dx2)
        dmerged = _dot_nt(dx2_b, w_scr[2])
        gw_scr[2] += _dot_tn(merged, dx2_b)
        dyh = dmerged * smh
        dya = dmerged * sma
        dtail_ref[:, D:2 * D] = _bf(dyh * yh * (1.0 - smh))
        dtail_ref[:, 2 * D:3 * D] = _bf(dya * ya * (1.0 - sma))
        dyh_b, dya_b = _bf(dyh), _bf(dya)
        dgh = _dot_nt(dyh_b, w_scr[0])
        gw_scr[0] += _dot_tn(gated_h, dyh_b)
        dga = _dot_nt(dya_b, w_scr[1])
        gw_scr[1] += _dot_tn(gated_a, dya_b)
        don = dgh * silu_g
        dhg_ref[...] = _bf(dgh * on * (sg * (1.0 + hg * (1.0 - sg))))
        vec_ref[1:2, :] += jnp.sum(don * xh, axis=0, keepdims=True)
        gxh = don * hnw
        rsb = rs_scr[...]
        for h in range(HEADS):
            sl = slice(h * HEAD_W, (h + 1) * HEAD_W)
            gh, xhh = gxh[:, sl], xh[:, sl]
            doh_ref[:, sl] = rsb[:, sl] * (gh - xhh * jnp.mean(gh * xhh, axis=1, keepdims=True))
        doa_ref[...] = dga * silu_a
        dtail_ref[:, 0:D] = _bf(dga * oa * (sa * (1.0 + ag * (1.0 - sa))))

        @pl.when(i == nt - 1)
        def _():
            cp = pltpu.make_async_copy(gw_scr, gw_hbm, sem)
            cp.start()
            cp.wait()

    row = lambda w, j: pl.BlockSpec((tm, w), lambda i: (i, j))
    const = lambda r, c: pl.BlockSpec((r, c), lambda i: (0, 0))
    return pl.pallas_call(
        body, name="mid", grid=(nt,),
        in_specs=[row(D, 0), row(D, 0), row(D, 0), row(D, 0), row(D, COL_HG // D),
                  row(512, COL_AG // 512), row(512, COL_AG // 512 + 1),
                  row(512, COL_MH // 512), row(512, COL_MH // 512 + 1),
                  row(512, COL_MA // 512), row(512, COL_MA // 512 + 1),
                  const(1, D), const(1, D), HBM_SPEC],
        out_specs=[row(D, 0), row(D, 0), row(D, 0), row(D, 0), row(3 * D, 0), const(8, 128), const(8, D), HBM_SPEC],
        out_shape=[jax.ShapeDtypeStruct((T, D), F32), jax.ShapeDtypeStruct((T, D), F32), jax.ShapeDtypeStruct((T, D), F32),
                   jax.ShapeDtypeStruct((T, D), BF16), jax.ShapeDtypeStruct((T, 3 * D), BF16),
                   jax.ShapeDtypeStruct((8, 128), F32), jax.ShapeDtypeStruct((8, D), F32),
                   jax.ShapeDtypeStruct((3, D, D), F32)],
        scratch_shapes=[pltpu.VMEM((3, D, D), BF16), pltpu.VMEM((3, D, D), F32),
                        pltpu.VMEM((tm, D), F32), pltpu.VMEM((tm, D), F32), pltpu.SemaphoreType.DMA],
        compiler_params=_params(("arbitrary",)),
    )(x, tgt, oh, oa, proj, proj, proj, proj, proj, proj, proj, hnw, fnw, wsq_bf)


def _bwd_dx(dproj, win_bf, x, norm_w, dx2):
    T = x.shape[0]
    tm = min(512, T)

    def body(dp_ref, w_ref, x_ref, nw_ref, dx2_ref, gx_ref, gnw_ref, acc):
        i, j = pl.program_id(0), pl.program_id(1)

        @pl.when((i == 0) & (j == 0))
        def _():
            gnw_ref[...] = jnp.zeros_like(gnw_ref)

        part = _dot_nt(dp_ref[...], w_ref[...])

        @pl.when(j == 0)
        def _():
            acc[...] = part

        @pl.when(j > 0)
        def _():
            acc[...] += part

        @pl.when(j == SHARDS - 1)
        def _():
            dxn = acc[...]
            xf = x_ref[...]
            rs = lax.rsqrt(jnp.mean(xf * xf, axis=1, keepdims=True) + EPS)
            xh = xf * rs
            gnw_ref[...] += jnp.sum(dxn * xh, axis=0, keepdims=True)
            gx = dxn * nw_ref[...]
            gx_ref[...] = rs * (gx - xh * jnp.mean(gx * xh, axis=1, keepdims=True)) + dx2_ref[...]

    return pl.pallas_call(
        body, name="bwd_dx", grid=(T // tm, SHARDS),
        in_specs=[pl.BlockSpec((tm, SHARD_W), lambda i, j: (i, j)),
                  pl.BlockSpec((None, D, SHARD_W), lambda i, j: (j, 0, 0)),
                  pl.BlockSpec((tm, D), lambda i, j: (i, 0)),
                  pl.BlockSpec((1, D), lambda i, j: (0, 0)),
                  pl.BlockSpec((tm, D), lambda i, j: (i, 0))],
        out_specs=[pl.BlockSpec((tm, D), lambda i, j: (i, 0)), pl.BlockSpec((1, D), lambda i, j: (0, 0))],
        out_shape=[jax.ShapeDtypeStruct((T, D), F32), jax.ShapeDtypeStruct((1, D), F32)],
        scratch_shapes=[pltpu.VMEM((tm, D), F32)],
        compiler_params=_params(("arbitrary", "arbitrary")),
    )(dproj, win_bf, x, norm_w, dx2)


def _bwd_win(xn_bf, dproj):
    T = xn_bf.shape[0]
    tm = min(512, T)

    def body(xn_ref, dp_ref, g_ref):
        part = _dot_tn(xn_ref[...], dp_ref[...])

        @pl.when(pl.program_id(1) == 0)
        def _():
            g_ref[...] = part

        @pl.when(pl.program_id(1) > 0)
        def _():
            g_ref[...] += part

    return pl.pallas_call(
        body, name="bwd_win", grid=(SHARDS, T // tm),
        in_specs=[pl.BlockSpec((tm, D), lambda j, i: (i, 0)), pl.BlockSpec((tm, SHARD_W), lambda j, i: (i, j))],
        out_specs=pl.BlockSpec((None, D, SHARD_W), lambda j, i: (j, 0, 0)),
        out_shape=jax.ShapeDtypeStruct((SHARDS, D, SHARD_W), F32),
        compiler_params=_params(("parallel", "arbitrary")),
    )(xn_bf, dproj)


def _place():
    return lax.axis_index("x"), lax.axis_index("y"), lax.axis_index("c")


def _flip(v, f):
    return 1 - v if f else v


def _win_half(ref, h):
    return ref.at[pl.ds(h * (D // 2), D // 2), :]


def _sq_half(ref, h):
    return ref.at[:, pl.ds(h * (D // 2), D // 2)]


def _allgather_weights(win_s, wsq_s):
    def body(win_ref, wsq_ref, win_out, wsq_out, send_sems, recv_sems, local_sems):
        x, y, c = _place()
        j = 2 * x + y
        sib = (x, y, 1 - c)
        halves = ((win_ref, win_out, _win_half), (wsq_ref, wsq_out, _sq_half))

        def copy(a, k, src, dst, to):
            return pltpu.make_async_remote_copy(src_ref=src, dst_ref=dst, send_sem=send_sems.at[6 * a + k],
                                                recv_sem=recv_sems.at[6 * a + k], device_id=to, device_id_type=MESH)

        local = [pltpu.make_async_copy(src, out.at[j], local_sems.at[a]) for a, (src, out, _) in enumerate(halves)]
        for cp in local:
            cp.start()
        started = []
        for a, (src, out, half) in enumerate(halves):
            for k, (fx, fy) in enumerate(CHIP_FLIPS):
                cp = copy(a, k, half(src, c), half(out.at[j], c), (_flip(x, fx), _flip(y, fy), c))
                cp.start()
                started.append(cp)
        for a, (src, out, half) in enumerate(halves):
            for k, (fx, fy) in enumerate(CHIP_FLIPS):
                jr = 2 * _flip(x, fx) + _flip(y, fy)
                landed = half(out.at[jr], c)
                copy(a, k, landed, landed, sib).wait_recv()
                cp = copy(a, 3 + k, landed, landed, sib)
                cp.start()
                started.append(cp)
        for a, (src, out, half) in enumerate(halves):
            for k, (fx, fy) in enumerate(CHIP_FLIPS):
                jr = 2 * _flip(x, fx) + _flip(y, fy)
                other = half(out.at[jr], 1 - c)
                copy(a, 3 + k, other, other, sib).wait_recv()
        for cp in started:
            cp.wait_send()
        for cp in local:
            cp.wait()

    return pl.pallas_call(
        body, name="allgather_weights",
        in_specs=[HBM_SPEC, HBM_SPEC], out_specs=[HBM_SPEC, HBM_SPEC],
        out_shape=[jax.ShapeDtypeStruct((SHARDS, D, SHARD_W), BF16), jax.ShapeDtypeStruct((SHARDS, 3 * SQ_ROWS, D), BF16)],
        scratch_shapes=[pltpu.SemaphoreType.DMA((12,)), pltpu.SemaphoreType.DMA((12,)), pltpu.SemaphoreType.DMA((2,))],
    )(win_s, wsq_s)


def _swap_halves(gwin, gsq):
    def body(gwin_ref, gsq_ref, win_got, sq_got, send_sems, recv_sems):
        x, y, c = _place()
        sib = (x, y, 1 - c)
        pairs = ((gwin_ref.at[:, pl.ds((1 - c) * (D // 2), D // 2), :], win_got),
                 (gsq_ref.at[:, :, pl.ds((1 - c) * (D // 2), D // 2)], sq_got))
        copies = [pltpu.make_async_remote_copy(src_ref=src, dst_ref=dst, send_sem=send_sems.at[a], recv_sem=recv_sems.at[a],
                                               device_id=sib, device_id_type=MESH) for a, (src, dst) in enumerate(pairs)]
        for cp in copies:
            cp.start()
        for cp in copies:
            cp.wait()

    return pl.pallas_call(
        body, name="swap_halves",
        in_specs=[HBM_SPEC, HBM_SPEC], out_specs=[HBM_SPEC, HBM_SPEC],
        out_shape=[jax.ShapeDtypeStruct((SHARDS, D // 2, SHARD_W), F32), jax.ShapeDtypeStruct((3, D, D // 2), F32)],
        scratch_shapes=[pltpu.SemaphoreType.DMA((2,)), pltpu.SemaphoreType.DMA((2,))],
    )(gwin, gsq)


def _add_halves(c_arr, gwin, gsq, win_got, sq_got):
    def body(c_ref, a_ref, b_ref, p_ref, q_ref, so_ref, sq_ref):
        so_ref[...] = a_ref[...] + b_ref[...]
        sq_ref[...] = p_ref[...] + q_ref[...]

    rows = 128
    steps = (D // 2) // rows
    return pl.pallas_call(
        body, name="add_halves",
        grid_spec=pltpu.PrefetchScalarGridSpec(
            num_scalar_prefetch=1, grid=(SHARDS, steps),
            in_specs=[pl.BlockSpec((None, rows, SHARD_W), lambda j, i, c: (j, c[0] * steps + i, 0)),
                      pl.BlockSpec((None, rows, SHARD_W), lambda j, i, c: (j, i, 0)),
                      pl.BlockSpec((3, SQ_ROWS // steps, D // 2), lambda j, i, c: (0, j * steps + i, c[0])),
                      pl.BlockSpec((3, SQ_ROWS // steps, D // 2), lambda j, i, c: (0, j * steps + i, 0))],
            out_specs=[pl.BlockSpec((None, rows, SHARD_W), lambda j, i, c: (j, i, 0)),
                       pl.BlockSpec((3, SQ_ROWS // steps, D // 2), lambda j, i, c: (0, j * steps + i, 0))]),
        out_shape=[jax.ShapeDtypeStruct((SHARDS, D // 2, SHARD_W), F32), jax.ShapeDtypeStruct((3, D, D // 2), F32)],
        compiler_params=_params(("arbitrary", "arbitrary")),
    )(c_arr, gwin, win_got, gsq, sq_got)


def _scatter_chip_sums(swin, ssq):
    def body(swin_ref, ssq_ref, win_got, sq_got, send_sems, recv_sems):
        x, y, c = _place()
        copies = []
        for k, (fx, fy) in enumerate(CHIP_FLIPS):
            px, py = _flip(x, fx), _flip(y, fy)
            jr = 2 * px + py
            for a, (src, dst) in enumerate(((swin_ref.at[jr], win_got.at[k]),
                                            (ssq_ref.at[:, pl.ds(jr * SQ_ROWS, SQ_ROWS), :], sq_got.at[k]))):
                copies.append(pltpu.make_async_remote_copy(src_ref=src, dst_ref=dst, send_sem=send_sems.at[2 * k + a],
                                                           recv_sem=recv_sems.at[2 * k + a], device_id=(px, py, c), device_id_type=MESH))
        for cp in copies:
            cp.start()
        for cp in copies:
            cp.wait()

    return pl.pallas_call(
        body, name="scatter_chip_sums",
        in_specs=[HBM_SPEC, HBM_SPEC], out_specs=[HBM_SPEC, HBM_SPEC],
        out_shape=[jax.ShapeDtypeStruct((3, D // 2, SHARD_W), F32), jax.ShapeDtypeStruct((3, 3, SQ_ROWS, D // 2), F32)],
        scratch_shapes=[pltpu.SemaphoreType.DMA((6,)), pltpu.SemaphoreType.DMA((6,))],
    )(swin, ssq)


def _sum_chips(j_arr, swin, ssq, win_got, sq_got):
    def body(j_ref, a_ref, b_ref, p_ref, q_ref, so_ref, sq_ref):
        so_ref[...] = ((a_ref[...] + b_ref[0]) + b_ref[1]) + b_ref[2]
        sq_ref[...] = ((p_ref[...] + q_ref[0]) + q_ref[1]) + q_ref[2]

    rows = 128
    steps = (D // 2) // rows
    sq_rows = SQ_ROWS // steps
    return pl.pallas_call(
        body, name="sum_chips",
        grid_spec=pltpu.PrefetchScalarGridSpec(
            num_scalar_prefetch=1, grid=(steps,),
            in_specs=[pl.BlockSpec((None, rows, SHARD_W), lambda i, j: (j[0], i, 0)),
                      pl.BlockSpec((3, rows, SHARD_W), lambda i, j: (0, i, 0)),
                      pl.BlockSpec((3, sq_rows, D // 2), lambda i, j: (0, j[0] * steps + i, 0)),
                      pl.BlockSpec((3, 3, sq_rows, D // 2), lambda i, j: (0, 0, i, 0))],
            out_specs=[pl.BlockSpec((rows, SHARD_W), lambda i, j: (i, 0)),
                       pl.BlockSpec((3, sq_rows, D // 2), lambda i, j: (0, i, 0))]),
        out_shape=[jax.ShapeDtypeStruct((D // 2, SHARD_W), F32), jax.ShapeDtypeStruct((3, SQ_ROWS, D // 2), F32)],
        compiler_params=_params(("arbitrary",)),
    )(j_arr, swin, win_got, ssq, sq_got)


def _join_halves(fwin, fsq):
    def body(fwin_ref, fsq_ref, win_out, sq_out, send_sems, recv_sems, local_sems):
        x, y, c = _place()
        sib = (x, y, 1 - c)
        dsts = (_win_half(win_out, c), sq_out.at[:, :, pl.ds(c * (D // 2), D // 2)])
        srcs = (fwin_ref, fsq_ref)
        local = [pltpu.make_async_copy(srcs[a], dsts[a], local_sems.at[a]) for a in range(2)]
        remote = [pltpu.make_async_remote_copy(src_ref=srcs[a], dst_ref=dsts[a], send_sem=send_sems.at[a], recv_sem=recv_sems.at[a],
                                               device_id=sib, device_id_type=MESH) for a in range(2)]
        for cp in local + remote:
            cp.start()
        others = (_win_half(win_out, 1 - c), sq_out.at[:, :, pl.ds((1 - c) * (D // 2), D // 2)])
        for a in range(2):
            pltpu.make_async_remote_copy(src_ref=srcs[a], dst_ref=others[a], send_sem=send_sems.at[a], recv_sem=recv_sems.at[a],
                                         device_id=sib, device_id_type=MESH).wait_recv()
        for cp in remote:
            cp.wait_send()
        for cp in local:
            cp.wait()

    return pl.pallas_call(
        body, name="join_halves",
        in_specs=[HBM_SPEC, HBM_SPEC], out_specs=[HBM_SPEC, HBM_SPEC],
        out_shape=[jax.ShapeDtypeStruct((D, SHARD_W), F32), jax.ShapeDtypeStruct((3, SQ_ROWS, D), F32)],
        scratch_shapes=[pltpu.SemaphoreType.DMA((2,)), pltpu.SemaphoreType.DMA((2,)), pltpu.SemaphoreType.DMA((2,))],
    )(fwin, fsq)


def _allreduce_small(vec):
    def body(vec_ref, out_ref, slots, send_sems, recv_sems):
        x, y, c = _place()
        me = 4 * x + 2 * y + c
        slots[me] = vec_ref[...]
        copies = []
        for k in range(1, 8):
            fx, fy, fc = (k >> 2) & 1, (k >> 1) & 1, k & 1
            copies.append(pltpu.make_async_remote_copy(
                src_ref=vec_ref, dst_ref=slots.at[me], send_sem=send_sems.at[k - 1], recv_sem=recv_sems.at[k - 1],
                device_id=(_flip(x, fx), _flip(y, fy), _flip(c, fc)), device_id_type=MESH))
        for cp in copies:
            cp.start()
        for k in range(1, 8):
            fx, fy, fc = (k >> 2) & 1, (k >> 1) & 1, k & 1
            src = 4 * _flip(x, fx) + 2 * _flip(y, fy) + _flip(c, fc)
            pltpu.make_async_remote_copy(src_ref=vec_ref, dst_ref=slots.at[src], send_sem=send_sems.at[k - 1],
                                         recv_sem=recv_sems.at[k - 1], device_id=(x, y, c), device_id_type=MESH).wait_recv()
        for cp in copies:
            cp.wait_send()
        total = slots[0]
        for s in range(1, 8):
            total = total + slots[s]
        out_ref[...] = total

    return pl.pallas_call(
        body, name="allreduce_small",
        in_specs=[pl.BlockSpec(memory_space=pltpu.VMEM)], out_specs=pl.BlockSpec(memory_space=pltpu.VMEM),
        out_shape=jax.ShapeDtypeStruct((8, D), F32),
        scratch_shapes=[pltpu.VMEM((8, 8, D), F32), pltpu.SemaphoreType.DMA((7,)), pltpu.SemaphoreType.DMA((7,))],
    )(vec)


def _adamw_math(w, g, m, v):
    m = ADAM_B1 * m + (1.0 - ADAM_B1) * g
    v = ADAM_B2 * v + (1.0 - ADAM_B2) * (g * g)
    m_hat = m / (1.0 - ADAM_B1 ** ADAM_STEP)
    v_hat = v / (1.0 - ADAM_B2 ** ADAM_STEP)
    delta = -ADAM_LR * (m_hat / (jnp.sqrt(v_hat) + ADAM_EPS) + ADAM_WD * w)
    return delta, m, v


def _adamw(name, w, g, m, v, rows):
    R, C = w.shape

    def body(w_ref, g_ref, m_ref, v_ref, d_out, m_out, v_out):
        d_out[...], m_out[...], v_out[...] = _adamw_math(w_ref[...], g_ref[...], m_ref[...], v_ref[...])

    spec = pl.BlockSpec((rows, C), lambda i: (i, 0))
    return pl.pallas_call(
        body, name=name, grid=(R // rows,), in_specs=[spec] * 4, out_specs=[spec] * 3,
        out_shape=[jax.ShapeDtypeStruct((R, C), F32)] * 3,
        compiler_params=_params(("parallel",)),
    )(w, g, m, v)


def _small_update(total, lbw, w8, m8, v8):
    def body(t_ref, lbw_ref, w_ref, m_ref, v_ref, g_out, d_out, m_out, v_out):
        lb = 1.0 / (1.0 + jnp.exp(lbw_ref[1:2, :] - lbw_ref[0:1, :]))
        dlb = t_ref[2:3, :] * lb * (1.0 - lb)
        g_out[...] = jnp.zeros_like(g_out)
        g_out[0:1, :] = t_ref[3:4, :]
        g_out[1:2, :] = dlb
        g_out[2:3, :] = -dlb
        g_out[3:4, :] = t_ref[1:2, :]
        g_out[4:5, :] = t_ref[0:1, :]
        g_out[5:6, :] = t_ref[4:5, :]
        d_out[...], m_out[...], v_out[...] = _adamw_math(w_ref[...], g_out[...], m_ref[...], v_ref[...])

    return pl.pallas_call(
        body, name="small_update", out_shape=[jax.ShapeDtypeStruct((8, D), F32)] * 4,
        compiler_params=_params(),
    )(total, lbw, w8, m8, v8)


def _pack8(norm_w, lbw, hnw, fnw, sinks):
    pad = jnp.zeros((1, D - 16), F32)
    return jnp.concatenate([norm_w, lbw, hnw, fnw.reshape(1, D), jnp.concatenate([sinks, pad], axis=1),
                            jnp.zeros((2, D), F32)], axis=0)


def _unpack8(a):
    return a[0:1], a[1:3], a[3:4], a[5:6, 0:16], a[4]


def _local_step(x, tgt, norm_w, lbw, hnw, sinks, fnw, win_bf, wsq_bf):
    proj, xn_bf = _fwd_proj(x, norm_w, win_bf)
    oh, states = _hgrn_fwd(proj, lbw)
    oa = _attn_fwd(proj, sinks)
    dx2, doh, doa, dhg, dtail, loss8, vec_mid, gsq = _mid(x, tgt, proj, oh, oa, hnw, fnw.reshape(1, D), wsq_bf)
    dhead, dlb = _hgrn_bwd(proj, lbw, states, doh)
    daq, dak, dav, dsink = _attn_bwd(proj, sinks, oa, doa)
    dproj = jnp.concatenate([dhead, dhg, daq, dak, dav, dtail], axis=1)
    grad_x, gnw = _bwd_dx(dproj, win_bf, x, norm_w, dx2)
    gwin = _bwd_win(xn_bf, dproj)
    sink_row = jnp.concatenate([dsink[:, 0].reshape(1, 16), jnp.zeros((1, D - 16), F32)], axis=1)
    vec = jnp.concatenate([vec_mid[0:2], dlb, gnw, sink_row, jnp.zeros((3, D), F32)], axis=0)
    return loss8[0, 0], grad_x, gwin, gsq, vec


def kernel(x, norm_w, w_in, hgrn_lower_bound, hgrn_norm_w, w_branch_hgrn, attn_sinks, w_branch_attn, w_out, final_norm_w, loss_target, m_norm_w, m_w_in, m_hgrn_lower_bound, m_hgrn_norm_w, m_w_branch_hgrn, m_attn_sinks, m_w_branch_attn, m_w_out, m_final_norm_w, v_norm_w, v_w_in, v_hgrn_lower_bound, v_hgrn_norm_w, v_w_branch_hgrn, v_attn_sinks, v_w_branch_attn, v_w_out, v_final_norm_w):
    c_arr = lax.axis_index("c").astype(jnp.int32).reshape(1)
    j_arr = (2 * lax.axis_index("x") + lax.axis_index("y")).astype(jnp.int32).reshape(1)

    win_s, wsq_s = _cast_shards(w_in[0], w_branch_hgrn[0], w_branch_attn[0], w_out[0])
    win_bf, wsq_all = _allgather_weights(win_s, wsq_s)
    wsq_bf = wsq_all.reshape(SHARDS, 3, SQ_ROWS, D).transpose(1, 0, 2, 3).reshape(3, D, D)

    loss_part, grad_x, gwin, gsq, vec = _local_step(
        x[0], loss_target[0], norm_w, hgrn_lower_bound, hgrn_norm_w, attn_sinks, final_norm_w, win_bf, wsq_bf)
    loss = lax.psum(loss_part, ("x", "y", "c"))

    win_got, sq_got = _swap_halves(gwin, gsq)
    swin, ssq = _add_halves(c_arr, gwin, gsq, win_got, sq_got)
    win_got2, sq_got2 = _scatter_chip_sums(swin, ssq)
    fwin, fsq = _sum_chips(j_arr, swin, ssq, win_got2, sq_got2)
    g_win, g_sq = _join_halves(fwin, fsq)

    d_win, nm_win, nv_win = _adamw("adamw_w_in", w_in[0], g_win, m_w_in[0], v_w_in[0], 128)
    sq_w = jnp.concatenate([w_branch_hgrn[0], w_branch_attn[0], w_out[0]], axis=0)
    sq_m = jnp.concatenate([m_w_branch_hgrn[0], m_w_branch_attn[0], m_w_out[0]], axis=0)
    sq_v = jnp.concatenate([v_w_branch_hgrn[0], v_w_branch_attn[0], v_w_out[0]], axis=0)
    d_sq, nm_sq, nv_sq = _adamw("adamw_square", sq_w, g_sq.reshape(3 * SQ_ROWS, D), sq_m, sq_v, 256)

    total = _allreduce_small(vec)
    g8, d8, nm8, nv8 = _small_update(
        total, hgrn_lower_bound,
        _pack8(norm_w, hgrn_lower_bound, hgrn_norm_w, final_norm_w, attn_sinks),
        _pack8(m_norm_w, m_hgrn_lower_bound, m_hgrn_norm_w, m_final_norm_w, m_attn_sinks),
        _pack8(v_norm_w, v_hgrn_lower_bound, v_hgrn_norm_w, v_final_norm_w, v_attn_sinks))

    def assemble(win, sq, small):
        nw, lb, hn, sk, fn = _unpack8(small)
        sq = sq.reshape(3, 1, SQ_ROWS, D)
        return (nw, win.reshape(1, D, SHARD_W), lb, hn, sq[0], sk, sq[1], sq[2], fn)

    return (loss, grad_x.reshape(1, -1, D),
            *assemble(g_win, g_sq, g8), *assemble(d_win, d_sq, d8),
            *assemble(nm_win, nm_sq, nm8), *assemble(nv_win, nv_sq, nv8))
```

```python
import functools

import jax
import jax.numpy as jnp
from jax import lax
from jax.experimental import pallas as pl
from jax.experimental.pallas import tpu as pltpu

F32 = jnp.float32
BF16 = jnp.bfloat16

D = 1024
D_IN = 8704
SHARDS = 4
SHARD_W = D_IN // SHARDS
SQ_ROWS = D // SHARDS
HEADS = 8
HEAD_W = 128
CHUNK = 64
SUB = 2
ATT_BLOCK = 128
KV_HEADS = 4
HEAD_DIM = 64
EPS = 1e-6
NEG = -1e30
SCALE = HEAD_DIM ** -0.5
COL_HG, COL_AQ, COL_AK, COL_AV, COL_AG, COL_MH, COL_MA = 3072, 4096, 5120, 5376, 5632, 6656, 7680

ADAM_LR, ADAM_B1, ADAM_B2, ADAM_EPS, ADAM_WD, ADAM_STEP = 0.001, 0.9, 0.999, 1e-08, 0.01, 10

VMEM_LIMIT = 56 * 1024 * 1024
MESH = pl.DeviceIdType.MESH
HBM_SPEC = pl.BlockSpec(memory_space=pltpu.HBM)
CHIP_FLIPS = ((1, 0), (0, 1), (1, 1))


def _dot(a, b):
    return jnp.dot(a, b, preferred_element_type=F32)


def _dot_nt(a, b):
    return lax.dot_general(a, b, (((1,), (1,)), ((), ())), preferred_element_type=F32)


def _dot_tn(a, b):
    return lax.dot_general(a, b, (((0,), (0,)), ((), ())), preferred_element_type=F32)


def _sigmoid(v):
    return 1.0 / (1.0 + jnp.exp(-v))


def _bf(v):
    return v.astype(BF16)


def _split3(v):
    a = _bf(v)
    r = v - a.astype(F32)
    b = _bf(r)
    c = _bf(r - b.astype(F32))
    return a, b, c


def _tri_dot(tri, v):
    a, b, c = _split3(v)
    return _dot(tri, a) + _dot(tri, b) + _dot(tri, c)


def _params(sem=None):
    return pltpu.CompilerParams(dimension_semantics=sem, vmem_limit_bytes=VMEM_LIMIT)


def _cast_shards(win_s, wbh_s, wba_s, wout_s):
    def body(win_ref, a_ref, b_ref, c_ref, win_o, sq_o):
        win_o[...] = _bf(win_ref[...])
        sq_o[0:SQ_ROWS, :] = _bf(a_ref[...])
        sq_o[SQ_ROWS:2 * SQ_ROWS, :] = _bf(b_ref[...])
        sq_o[2 * SQ_ROWS:3 * SQ_ROWS, :] = _bf(c_ref[...])

    return pl.pallas_call(
        body, name="cast_shards",
        out_shape=[jax.ShapeDtypeStruct((D, SHARD_W), BF16), jax.ShapeDtypeStruct((3 * SQ_ROWS, D), BF16)],
        compiler_params=_params(),
    )(win_s, wbh_s, wba_s, wout_s)


def _fwd_proj(x, norm_w, win_bf):
    T = x.shape[0]
    tm = min(256, T)

    def body(x_ref, nw_ref, w_hbm, proj_ref, xn_ref, w_scr, sem):
        @pl.when(pl.program_id(0) == 0)
        def _():
            cp = pltpu.make_async_copy(w_hbm, w_scr, sem)
            cp.start()
            cp.wait()

        xf = x_ref[...]
        rs = lax.rsqrt(jnp.mean(xf * xf, axis=1, keepdims=True) + EPS)
        xn = _bf((xf * rs) * nw_ref[...])
        xn_ref[...] = xn
        for j in range(SHARDS):
            proj_ref[:, j * SHARD_W:(j + 1) * SHARD_W] = _dot(xn, w_scr[j])

    return pl.pallas_call(
        body, name="fwd_proj", grid=(T // tm,),
        in_specs=[pl.BlockSpec((tm, D), lambda i: (i, 0)), pl.BlockSpec((1, D), lambda i: (0, 0)), HBM_SPEC],
        out_specs=[pl.BlockSpec((tm, D_IN), lambda i: (i, 0)), pl.BlockSpec((tm, D), lambda i: (i, 0))],
        out_shape=[jax.ShapeDtypeStruct((T, D_IN), F32), jax.ShapeDtypeStruct((T, D), BF16)],
        scratch_shapes=[pltpu.VMEM((SHARDS, D, SHARD_W), BF16), pltpu.SemaphoreType.DMA],
        compiler_params=_params(("arbitrary",)),
    )(x, norm_w, win_bf)


def _hgrn_gates(hq_ref, hf_ref, lbw_ref, b_scr):
    lb = 1.0 / (1.0 + jnp.exp(lbw_ref[1:2, :] - lbw_ref[0:1, :]))
    hf = hf_ref[...]
    sig = _sigmoid(hf)
    f = lb + (1.0 - lb) * sig
    g = jnp.log(f)
    hq = hq_ref[...]
    sq = _sigmoid(hq)
    q = hq * sq
    row = lax.broadcasted_iota(jnp.int32, (CHUNK, CHUNK), 0)
    col = lax.broadcasted_iota(jnp.int32, (CHUNK, CHUNK), 1)
    causal = row >= col
    b = _tri_dot(jnp.where(causal, 1.0, 0.0).astype(BF16), g)
    b_scr[...] = b
    bc = b_scr[CHUNK - 1:CHUNK, :]
    r = b_scr[CHUNK // 2 - 1:CHUNK // 2, :]
    return dict(lb=lb, sig=sig, f=f, k=1.0 - f, hq=hq, sq=sq, q=q, b=b, bc=bc, r=r, causal=causal)


def _hgrn_fwd(proj, lbw):
    T = proj.shape[0]
    n = T // CHUNK

    def body(hq_ref, hf_ref, hi_ref, lbw_ref, o_ref, st_ref, s_scr, b_scr):
        @pl.when(pl.program_id(0) == 0)
        def _():
            s_scr[...] = jnp.zeros_like(s_scr)

        for c in range(SUB):
            rows = pl.ds(c * CHUNK, CHUNK)
            gt = _hgrn_gates(hq_ref.at[rows, :], hf_ref.at[rows, :], lbw_ref, b_scr.at[rows, :])
            b, bc, r, q, k = gt["b"], gt["bc"], gt["r"], gt["q"], gt["k"]
            qe = _bf(q * jnp.exp(b))
            qr = _bf(q * jnp.exp(b - r))
            kr = _bf(k * jnp.exp(r - b))
            kl = _bf(k * jnp.exp(bc - b))
            ebc = jnp.exp(bc)
            v = _bf(hi_ref[rows, :])
            for h in range(HEADS):
                sl = slice(h * HEAD_W, (h + 1) * HEAD_W)
                st = s_scr[h]
                st_ref[c, h] = st
                a = jnp.where(gt["causal"], _dot_nt(qr[:, sl], kr[:, sl]), 0.0)
                o_ref[rows, sl] = _dot(_bf(a), v[:, sl]) + _dot_nt(qe[:, sl], _bf(st))
                s_scr[h] = ebc[:, sl] * st + _dot_tn(v[:, sl], kl[:, sl])

    col = lambda j: pl.BlockSpec((SUB * CHUNK, D), lambda i: (i, j))
    return pl.pallas_call(
        body, name="hgrn_fwd", grid=(n // SUB,),
        in_specs=[col(0), col(1), col(2), pl.BlockSpec((2, D), lambda i: (0, 0))],
        out_specs=[pl.BlockSpec((SUB * CHUNK, D), lambda i: (i, 0)),
                   pl.BlockSpec((SUB, HEADS, HEAD_W, HEAD_W), lambda i: (i, 0, 0, 0))],
        out_shape=[jax.ShapeDtypeStruct((T, D), F32), jax.ShapeDtypeStruct((n, HEADS, HEAD_W, HEAD_W), F32)],
        scratch_shapes=[pltpu.VMEM((HEADS, HEAD_W, HEAD_W), F32), pltpu.VMEM((SUB * CHUNK, D), F32)],
        compiler_params=_params(("arbitrary",)),
    )(proj, proj, proj, lbw)


def _hgrn_bwd(proj, lbw, states, do):
    T = proj.shape[0]
    n = T // CHUNK

    def body(hq_ref, hf_ref, hi_ref, lbw_ref, st_ref, do_ref, dp_ref, dlb_ref,
             ds_scr, b_scr, dq_scr, dk_scr, dv_scr, late_scr, early_scr, ex_scr):
        @pl.when(pl.program_id(0) == 0)
        def _():
            ds_scr[...] = jnp.zeros_like(ds_scr)
            dlb_ref[...] = jnp.zeros_like(dlb_ref)

        for c in reversed(range(SUB)):
            rows = pl.ds(c * CHUNK, CHUNK)
            gt = _hgrn_gates(hq_ref.at[rows, :], hf_ref.at[rows, :], lbw_ref, b_scr.at[rows, :])
            b, bc, r, q, k = gt["b"], gt["bc"], gt["r"], gt["q"], gt["k"]
            eb = jnp.exp(b)
            er = jnp.exp(b - r)
            erk = jnp.exp(r - b)
            el = jnp.exp(bc - b)
            ebc = jnp.exp(bc)
            qe, qr, kr, kl = _bf(q * eb), _bf(q * er), _bf(k * erk), _bf(k * el)
            v = _bf(hi_ref[rows, :])
            do_b = _bf(do_ref[rows, :])
            for h in range(HEADS):
                sl = slice(h * HEAD_W, (h + 1) * HEAD_W)
                st0 = st_ref[c, h]
                dst = ds_scr[h]
                dst_b = _bf(dst)
                a = _bf(jnp.where(gt["causal"], _dot_nt(qr[:, sl], kr[:, sl]), 0.0))
                da = _bf(jnp.where(gt["causal"], _dot_nt(do_b[:, sl], v[:, sl]), 0.0))
                mq = _dot(da, kr[:, sl])
                mk = _dot_tn(da, qr[:, sl])
                dq_in = eb[:, sl] * _dot(do_b[:, sl], _bf(st0))
                dk_in = el[:, sl] * _dot(v[:, sl], dst_b)
                dq_scr[rows, sl] = er[:, sl] * mq + dq_in
                dk_scr[rows, sl] = erk[:, sl] * mk + dk_in
                dv_scr[rows, sl] = _dot_tn(a, do_b[:, sl]) + _dot_nt(kl[:, sl], dst_b)
                late_scr[rows, sl] = q[:, sl] * dq_in + qr[:, sl].astype(F32) * mq - kr[:, sl].astype(F32) * mk
                early_scr[rows, sl] = k[:, sl] * dk_in
                ex_scr[:, sl] = jnp.sum(dst * st0, axis=0, keepdims=True)
                ds_scr[h] = ebc[:, sl] * dst + _dot_tn(do_b[:, sl], qe[:, sl])

            dq, dk = dq_scr[rows, :], dk_scr[rows, :]
            row = lax.broadcasted_iota(jnp.int32, (CHUNK, CHUNK), 0)
            col = lax.broadcasted_iota(jnp.int32, (CHUNK, CHUNK), 1)
            at_or_after = jnp.where(col >= row, 1.0, 0.0).astype(BF16)
            before = jnp.where(col < row, 1.0, 0.0).astype(BF16)
            dg = _tri_dot(at_or_after, late_scr[rows, :]) + _tri_dot(before, early_scr[rows, :]) + ebc * ex_scr[...]
            df = dg / gt["f"] - dk
            sig, sq, hq, lb = gt["sig"], gt["sq"], gt["hq"], gt["lb"]
            dp_ref[rows, 0:D] = _bf(dq * (sq * (1.0 + hq * (1.0 - sq))))
            dp_ref[rows, D:2 * D] = _bf(df * (1.0 - lb) * sig * (1.0 - sig))
            dp_ref[rows, 2 * D:3 * D] = _bf(dv_scr[rows, :])
            dlb_ref[...] += jnp.sum(df * (1.0 - sig), axis=0, keepdims=True)

    ns = n // SUB
    col = lambda j: pl.BlockSpec((SUB * CHUNK, D), lambda i: (ns - 1 - i, j))
    return pl.pallas_call(
        body, name="hgrn_bwd", grid=(ns,),
        in_specs=[col(0), col(1), col(2), pl.BlockSpec((2, D), lambda i: (0, 0)),
                  pl.BlockSpec((SUB, HEADS, HEAD_W, HEAD_W), lambda i: (ns - 1 - i, 0, 0, 0)),
                  pl.BlockSpec((SUB * CHUNK, D), lambda i: (ns - 1 - i, 0))],
        out_specs=[pl.BlockSpec((SUB * CHUNK, 3 * D), lambda i: (ns - 1 - i, 0)),
                   pl.BlockSpec((1, D), lambda i: (0, 0))],
        out_shape=[jax.ShapeDtypeStruct((T, 3 * D), BF16), jax.ShapeDtypeStruct((1, D), F32)],
        scratch_shapes=[pltpu.VMEM((HEADS, HEAD_W, HEAD_W), F32)] + [pltpu.VMEM((SUB * CHUNK, D), F32)] * 6
                       + [pltpu.VMEM((1, D), F32)],
        compiler_params=_params(("arbitrary",)),
    )(proj, proj, proj, lbw, states, do)


def _attn_masks(blk):
    qi = lax.broadcasted_iota(jnp.int32, (ATT_BLOCK, 2 * ATT_BLOCK), 0)
    kj = lax.broadcasted_iota(jnp.int32, (ATT_BLOCK, 2 * ATT_BLOCK), 1)
    band = (kj > qi) & (kj <= qi + ATT_BLOCK)
    return band & ((blk > 0) | (kj >= ATT_BLOCK))


def _head_pair_operand(t, hp, low):
    mine = low if hp == 0 else jnp.logical_not(low)
    both = jnp.where(mine, t, pltpu.roll(t, HEAD_DIM, 1))
    return _bf(jnp.concatenate([jnp.where(low, both, 0.0), jnp.where(low, 0.0, both)], axis=0))


def _attn_probs(s, sink, valid):
    s = jnp.where(valid, s * SCALE, NEG)
    m = jnp.maximum(jnp.max(s, axis=1, keepdims=True), sink)
    p = jnp.exp(s - m)
    es = jnp.exp(sink - m)
    inv = 1.0 / (jnp.sum(p, axis=1, keepdims=True) + es)
    return p * inv, es * inv


def _attn_fwd(proj, sinks):
    T = proj.shape[0]
    nb = T // ATT_BLOCK
    W2 = 2 * ATT_BLOCK

    def body(sink_ref, q_ref, kp_ref, kc_ref, vp_ref, vc_ref, o_ref):
        blk = pl.program_id(0)
        valid = _attn_masks(blk)
        low = lax.broadcasted_iota(jnp.int32, (1, 2 * HEAD_DIM), 1) < HEAD_DIM
        kcat = jnp.concatenate([kp_ref[...], kc_ref[...]], axis=0)
        vcat = jnp.concatenate([vp_ref[...], vc_ref[...]], axis=0)
        for h in range(KV_HEADS):
            tl = slice((h // 2) * 128, (h // 2) * 128 + 128)
            mine = low if h % 2 == 0 else jnp.logical_not(low)
            kh = _bf(jnp.where(mine, kcat[:, tl], pltpu.roll(kcat[:, tl], HEAD_DIM, 1)))
            vh = _bf(jnp.where(mine, vcat[:, tl], pltpu.roll(vcat[:, tl], HEAD_DIM, 1)))
            for t in range(2):
                ql = slice((2 * h + t) * 128, (2 * h + t) * 128 + 128)
                q2 = q_ref[:, ql]
                outs = []
                for p in range(2):
                    qm = _bf(jnp.where(low if p == 0 else jnp.logical_not(low), q2, 0.0))
                    probs, _ = _attn_probs(_dot_nt(qm, kh), sink_ref[0, 4 * h + 2 * t + p], valid)
                    outs.append(_dot(_bf(probs), vh))
                o_ref[:, ql] = jnp.where(low, outs[0], outs[1])

    prev = lambda i: jnp.maximum(i - 1, 0)
    return pl.pallas_call(
        body, name="attn_fwd", grid=(nb,),
        in_specs=[pl.BlockSpec(memory_space=pltpu.SMEM),
                  pl.BlockSpec((ATT_BLOCK, D), lambda i: (i, COL_AQ // D)),
                  pl.BlockSpec((ATT_BLOCK, 256), lambda i: (prev(i), COL_AK // 256)),
                  pl.BlockSpec((ATT_BLOCK, 256), lambda i: (i, COL_AK // 256)),
                  pl.BlockSpec((ATT_BLOCK, 256), lambda i: (prev(i), COL_AV // 256)),
                  pl.BlockSpec((ATT_BLOCK, 256), lambda i: (i, COL_AV // 256))],
        out_specs=pl.BlockSpec((ATT_BLOCK, D), lambda i: (i, 0)),
        out_shape=jax.ShapeDtypeStruct((T, D), F32),
        compiler_params=_params(("arbitrary",)),
    )(sinks, proj, proj, proj, proj, proj)


def _attn_bwd(proj, sinks, o, do):
    T = proj.shape[0]
    nb = T // ATT_BLOCK
    W2 = 2 * ATT_BLOCK

    def body(sink_ref, q_ref, kp_ref, kc_ref, vp_ref, vc_ref, o_ref, do_ref,
             dq_ref, dk_ref, dv_ref, dsink_ref, ck_scr, cv_scr, nk_scr, nv_scr):
        blk = pl.program_id(0)

        @pl.when(blk == 0)
        def _():
            ck_scr[...] = jnp.zeros_like(ck_scr)
            cv_scr[...] = jnp.zeros_like(cv_scr)
            dsink_ref[...] = jnp.zeros_like(dsink_ref)

        @pl.when(blk < nb)
        def _():
            valid = _attn_masks(blk)
            low = lax.broadcasted_iota(jnp.int32, (1, 2 * HEAD_DIM), 1) < HEAD_DIM
            kcat = jnp.concatenate([kp_ref[...], kc_ref[...]], axis=0)
            vcat = jnp.concatenate([vp_ref[...], vc_ref[...]], axis=0)
            for h in range(KV_HEADS):
                tl = slice((h // 2) * 128, (h // 2) * 128 + 128)
                kbd = _head_pair_operand(kcat[:, tl], h % 2, low)
                vbd = _head_pair_operand(vcat[:, tl], h % 2, low)
                dkbd = jnp.zeros((2 * W2, 128), F32)
                dvbd = jnp.zeros((2 * W2, 128), F32)
                for t in range(2):
                    ql = slice((2 * h + t) * 128, (2 * h + t) * 128 + 128)
                    head = 4 * h + 2 * t
                    q2 = _bf(q_ref[:, ql])
                    do2 = do_ref[:, ql]
                    do2_b = _bf(do2)
                    doo = do2 * o_ref[:, ql]
                    dsum0 = jnp.sum(jnp.where(low, doo, 0.0), axis=1, keepdims=True)
                    dsum1 = jnp.sum(jnp.where(low, 0.0, doo), axis=1, keepdims=True)
                    s2 = _dot_nt(q2, kbd)
                    p0, ps0 = _attn_probs(s2[:, 0:W2], sink_ref[0, head], valid)
                    p1, ps1 = _attn_probs(s2[:, W2:2 * W2], sink_ref[0, head + 1], valid)
                    dp2 = _dot_nt(do2_b, vbd)
                    ds2 = _bf(jnp.concatenate([p0 * (dp2[:, 0:W2] - dsum0), p1 * (dp2[:, W2:2 * W2] - dsum1)], axis=1) * SCALE)
                    dq_ref[:, ql] = _bf(_dot(ds2, kbd))
                    dkbd = dkbd + _dot_tn(ds2, q2)
                    dvbd = dvbd + _dot_tn(_bf(jnp.concatenate([p0, p1], axis=1)), do2_b)
                    dsink_ref[head:head + 1, :] += jnp.zeros((1, 128), F32) - jnp.sum(ps0 * dsum0, axis=0, keepdims=True)
                    dsink_ref[head + 1:head + 2, :] += jnp.zeros((1, 128), F32) - jnp.sum(ps1 * dsum1, axis=0, keepdims=True)
                dk2 = jnp.where(low, dkbd[0:W2], dkbd[W2:2 * W2])
                dv2 = jnp.where(low, dvbd[0:W2], dvbd[W2:2 * W2])
                dk2 = dk2 + pltpu.roll(dk2, HEAD_DIM, 1)
                dv2 = dv2 + pltpu.roll(dv2, HEAD_DIM, 1)
                if h % 2 == 0:
                    keep_k, keep_v = dk2, dv2
                else:
                    nk_scr[:, tl] = jnp.where(low, keep_k, dk2)
                    nv_scr[:, tl] = jnp.where(low, keep_v, dv2)
            dk_ref[...] = _bf(ck_scr[...] + nk_scr[0:ATT_BLOCK, :])
            dv_ref[...] = _bf(cv_scr[...] + nv_scr[0:ATT_BLOCK, :])
            ck_scr[...] = nk_scr[ATT_BLOCK:2 * ATT_BLOCK, :]
            cv_scr[...] = nv_scr[ATT_BLOCK:2 * ATT_BLOCK, :]

        @pl.when(blk == nb)
        def _():
            dk_ref[...] = _bf(ck_scr[...])
            dv_ref[...] = _bf(cv_scr[...])

    cur = lambda i: jnp.minimum(i, nb - 1)
    prev = lambda i: jnp.maximum(cur(i) - 1, 0)
    late = lambda i: jnp.maximum(i - 1, 0)
    dq, dk, dv, dsink = pl.pallas_call(
        body, name="attn_bwd", grid=(nb + 1,),
        in_specs=[pl.BlockSpec(memory_space=pltpu.SMEM),
                  pl.BlockSpec((ATT_BLOCK, D), lambda i: (cur(i), COL_AQ // D)),
                  pl.BlockSpec((ATT_BLOCK, 256), lambda i: (prev(i), COL_AK // 256)),
                  pl.BlockSpec((ATT_BLOCK, 256), lambda i: (cur(i), COL_AK // 256)),
                  pl.BlockSpec((ATT_BLOCK, 256), lambda i: (prev(i), COL_AV // 256)),
                  pl.BlockSpec((ATT_BLOCK, 256), lambda i: (cur(i), COL_AV // 256)),
                  pl.BlockSpec((ATT_BLOCK, D), lambda i: (cur(i), 0)),
                  pl.BlockSpec((ATT_BLOCK, D), lambda i: (cur(i), 0))],
        out_specs=[pl.BlockSpec((ATT_BLOCK, D), lambda i: (cur(i), 0)),
                   pl.BlockSpec((ATT_BLOCK, 256), lambda i: (late(i), 0)),
                   pl.BlockSpec((ATT_BLOCK, 256), lambda i: (late(i), 0)),
                   pl.BlockSpec((16, 128), lambda i: (0, 0))],
        out_shape=[jax.ShapeDtypeStruct((T, D), BF16), jax.ShapeDtypeStruct((T, 256), BF16),
                   jax.ShapeDtypeStruct((T, 256), BF16), jax.ShapeDtypeStruct((16, 128), F32)],
        scratch_shapes=[pltpu.VMEM((ATT_BLOCK, 256), F32), pltpu.VMEM((ATT_BLOCK, 256), F32),
                        pltpu.VMEM((2 * ATT_BLOCK, 256), F32), pltpu.VMEM((2 * ATT_BLOCK, 256), F32)],
        compiler_params=_params(("arbitrary",)),
    )(sinks, proj, proj, proj, proj, proj, o, do)
    return dq, dk, dv, dsink


def _mid(x, tgt, proj, oh, oa, hnw, fnw, wsq_bf):
    T = x.shape[0]
    tm = min(256, T)
    nt = T // tm

    def body(x_ref, tgt_ref, oh_ref, oa_ref, hg_ref, ag0_ref, ag1_ref, mh0_ref, mh1_ref, ma0_ref, ma1_ref,
             hnw_ref, fnw_ref, w_hbm,
             dx2_ref, doh_ref, doa_ref, dhg_ref, dtail_ref, lhs_ref, rhs_ref, loss_ref, vec_ref,
             w_scr, xh_scr, rs_scr, sem):
        @pl.when(pl.program_id(0) == 0)
        def _():
            cp = pltpu.make_async_copy(w_hbm, w_scr, sem)
            cp.start()
            cp.wait()
            loss_ref[...] = jnp.zeros_like(loss_ref)
            vec_ref[...] = jnp.zeros_like(vec_ref)

        oh = oh_ref[...]
        for h in range(HEADS):
            sl = slice(h * HEAD_W, (h + 1) * HEAD_W)
            ohh = oh[:, sl]
            rs = lax.rsqrt(jnp.mean(ohh * ohh, axis=1, keepdims=True) + EPS)
            xh_scr[:, sl] = ohh * rs
            rs_scr[:, sl] = jnp.broadcast_to(rs, (tm, HEAD_W))
        xh = xh_scr[...]
        hnw = hnw_ref[...]
        on = xh * hnw
        hg = hg_ref[...]
        sg = _sigmoid(hg)
        silu_g = hg * sg
        gated_h = _bf(on * silu_g)
        lhs_ref[0] = gated_h
        yh = _dot(gated_h, w_scr[0])
        oa = oa_ref[...]
        ag = jnp.concatenate([ag0_ref[...], ag1_ref[...]], axis=1)
        sa = _sigmoid(ag)
        silu_a = ag * sa
        gated_a = _bf(oa * silu_a)
        lhs_ref[1] = gated_a
        ya = _dot(gated_a, w_scr[1])
        smh = _sigmoid(jnp.concatenate([mh0_ref[...], mh1_ref[...]], axis=1))
        sma = _sigmoid(jnp.concatenate([ma0_ref[...], ma1_ref[...]], axis=1))
        merged = _bf(smh * yh + sma * ya)
        lhs_ref[2] = merged
        x2 = x_ref[...] + _dot(merged, w_scr[2])
        rs2 = lax.rsqrt(jnp.mean(x2 * x2, axis=1, keepdims=True) + EPS)
        xh2 = x2 * rs2
        fnw = fnw_ref[...]
        diff = xh2 * fnw - tgt_ref[...]
        loss_ref[...] += jnp.zeros_like(loss_ref) + jnp.sum(diff * diff) * (0.5 / D)

        dy = diff * (1.0 / D)
        vec_ref[0:1, :] += jnp.sum(dy * xh2, axis=0, keepdims=True)
        gy = dy * fnw
        dx2 = rs2 * (gy - xh2 * jnp.mean(gy * xh2, axis=1, keepdims=True))
        dx2_ref[...] = dx2
        dx2_b = _bf(dx2)
        rhs_ref[2] = dx2_b
        dmerged = _dot_nt(dx2_b, w_scr[2])
        dyh = dmerged * smh
        dya = dmerged * sma
        dtail_ref[:, D:2 * D] = _bf(dyh * yh * (1.0 - smh))
        dtail_ref[:, 2 * D:3 * D] = _bf(dya * ya * (1.0 - sma))
        dyh_b, dya_b = _bf(dyh), _bf(dya)
        rhs_ref[0] = dyh_b
        rhs_ref[1] = dya_b
        dgh = _dot_nt(dyh_b, w_scr[0])
        dga = _dot_nt(dya_b, w_scr[1])
        don = dgh * silu_g
        dhg_ref[...] = _bf(dgh * on * (sg * (1.0 + hg * (1.0 - sg))))
        vec_ref[1:2, :] += jnp.sum(don * xh, axis=0, keepdims=True)
        gxh = don * hnw
        rsb = rs_scr[...]
        for h in range(HEADS):
            sl = slice(h * HEAD_W, (h + 1) * HEAD_W)
            gh, xhh = gxh[:, sl], xh[:, sl]
            doh_ref[:, sl] = rsb[:, sl] * (gh - xhh * jnp.mean(gh * xhh, axis=1, keepdims=True))
        doa_ref[...] = dga * silu_a
        dtail_ref[:, 0:D] = _bf(dga * oa * (sa * (1.0 + ag * (1.0 - sa))))

    row = lambda w, j: pl.BlockSpec((tm, w), lambda i: (i, j))
    const = lambda r, c: pl.BlockSpec((r, c), lambda i: (0, 0))
    stack = pl.BlockSpec((3, tm, D), lambda i: (0, i, 0))
    return pl.pallas_call(
        body, name="mid", grid=(nt,),
        in_specs=[row(D, 0), row(D, 0), row(D, 0), row(D, 0), row(D, COL_HG // D),
                  row(512, COL_AG // 512), row(512, COL_AG // 512 + 1),
                  row(512, COL_MH // 512), row(512, COL_MH // 512 + 1),
                  row(512, COL_MA // 512), row(512, COL_MA // 512 + 1),
                  const(1, D), const(1, D), HBM_SPEC],
        out_specs=[row(D, 0), row(D, 0), row(D, 0), row(D, 0), row(3 * D, 0), stack, stack, const(8, 128), const(8, D)],
        out_shape=[jax.ShapeDtypeStruct((T, D), F32), jax.ShapeDtypeStruct((T, D), F32), jax.ShapeDtypeStruct((T, D), F32),
                   jax.ShapeDtypeStruct((T, D), BF16), jax.ShapeDtypeStruct((T, 3 * D), BF16),
                   jax.ShapeDtypeStruct((3, T, D), BF16), jax.ShapeDtypeStruct((3, T, D), BF16),
                   jax.ShapeDtypeStruct((8, 128), F32), jax.ShapeDtypeStruct((8, D), F32)],
        scratch_shapes=[pltpu.VMEM((3, D, D), BF16), pltpu.VMEM((tm, D), F32), pltpu.VMEM((tm, D), F32),
                        pltpu.SemaphoreType.DMA],
        compiler_params=_params(("arbitrary",)),
    )(x, tgt, oh, oa, proj, proj, proj, proj, proj, proj, proj, hnw, fnw, wsq_bf)


def _wgrad_square(lhs, rhs):
    T = lhs.shape[1]
    tk = min(1024, T)

    def body(a_ref, b_ref, g_ref):
        part = _dot_tn(a_ref[...], b_ref[...])

        @pl.when(pl.program_id(1) == 0)
        def _():
            g_ref[...] = part

        @pl.when(pl.program_id(1) > 0)
        def _():
            g_ref[...] += part

    spec = pl.BlockSpec((None, tk, D), lambda k, i: (k, i, 0))
    return pl.pallas_call(
        body, name="wgrad_square", grid=(3, T // tk), in_specs=[spec, spec],
        out_specs=pl.BlockSpec((None, D, D), lambda k, i: (k, 0, 0)),
        out_shape=jax.ShapeDtypeStruct((3, D, D), F32),
        compiler_params=_params(("parallel", "arbitrary")),
    )(lhs, rhs)


def _bwd_dx(pieces, wt_bf, x, norm_w, dx2):
    T = x.shape[0]
    tm = min(256, T)
    widths = [p.shape[1] for p in pieces]
    n_p = len(pieces)

    def body(*refs):
        piece_refs = refs[:n_p]
        w_hbm, x_ref, nw_ref, dx2_ref, gx_ref, gnw_ref, dp_ref, w_scr, sem = refs[n_p:]

        @pl.when(pl.program_id(0) == 0)
        def _():
            cp = pltpu.make_async_copy(w_hbm, w_scr, sem)
            cp.start()
            cp.wait()
            gnw_ref[...] = jnp.zeros_like(gnw_ref)

        dxn = None
        off = 0
        for ref, w in zip(piece_refs, widths):
            blk = ref[...]
            dp_ref[:, off:off + w] = blk
            part = _dot(blk, w_scr[off:off + w, :])
            dxn = part if dxn is None else dxn + part
            off += w
        xf = x_ref[...]
        rs = lax.rsqrt(jnp.mean(xf * xf, axis=1, keepdims=True) + EPS)
        xh = xf * rs
        gnw_ref[...] += jnp.sum(dxn * xh, axis=0, keepdims=True)
        gx = dxn * nw_ref[...]
        gx_ref[...] = rs * (gx - xh * jnp.mean(gx * xh, axis=1, keepdims=True)) + dx2_ref[...]

    row = lambda w: pl.BlockSpec((tm, w), lambda i: (i, 0))
    return pl.pallas_call(
        body, name="bwd_dx", grid=(T // tm,),
        in_specs=[row(w) for w in widths] + [HBM_SPEC, row(D), pl.BlockSpec((1, D), lambda i: (0, 0)), row(D)],
        out_specs=[row(D), pl.BlockSpec((1, D), lambda i: (0, 0)), row(D_IN)],
        out_shape=[jax.ShapeDtypeStruct((T, D), F32), jax.ShapeDtypeStruct((1, D), F32), jax.ShapeDtypeStruct((T, D_IN), BF16)],
        scratch_shapes=[pltpu.VMEM((D_IN, D), BF16), pltpu.SemaphoreType.DMA],
        compiler_params=_params(("arbitrary",)),
    )(*pieces, wt_bf, x, norm_w, dx2)


def _bwd_win(xn_bf, dproj):
    T = xn_bf.shape[0]
    tm = min(1024, T)

    def body(xn_ref, dp_ref, g_ref):
        part = _dot_tn(xn_ref[...], dp_ref[...])

        @pl.when(pl.program_id(1) == 0)
        def _():
            g_ref[...] = part

        @pl.when(pl.program_id(1) > 0)
        def _():
            g_ref[...] += part

    return pl.pallas_call(
        body, name="bwd_win", grid=(SHARDS, T // tm),
        in_specs=[pl.BlockSpec((tm, D), lambda j, i: (i, 0)), pl.BlockSpec((tm, SHARD_W), lambda j, i: (i, j))],
        out_specs=pl.BlockSpec((None, D, SHARD_W), lambda j, i: (j, 0, 0)),
        out_shape=jax.ShapeDtypeStruct((SHARDS, D, SHARD_W), F32),
        compiler_params=_params(("parallel", "arbitrary")),
    )(xn_bf, dproj)


def _place():
    return lax.axis_index("x"), lax.axis_index("y"), lax.axis_index("c")


def _flip(v, f):
    return 1 - v if f else v


def _win_half(ref, h):
    return ref.at[pl.ds(h * (D // 2), D // 2), :]


def _sq_half(ref, h):
    return ref.at[:, pl.ds(h * (D // 2), D // 2)]


def _allgather_weights(win_s, wsq_s):
    def body(win_ref, wsq_ref, win_out, wsq_out, send_sems, recv_sems, local_sems):
        x, y, c = _place()
        j = 2 * x + y
        sib = (x, y, 1 - c)
        halves = ((win_ref, win_out, _win_half), (wsq_ref, wsq_out, _sq_half))

        def copy(a, k, src, dst, to):
            return pltpu.make_async_remote_copy(src_ref=src, dst_ref=dst, send_sem=send_sems.at[6 * a + k],
                                                recv_sem=recv_sems.at[6 * a + k], device_id=to, device_id_type=MESH)

        local = [pltpu.make_async_copy(src, out.at[j], local_sems.at[a]) for a, (src, out, _) in enumerate(halves)]
        for cp in local:
            cp.start()
        started = []
        for a, (src, out, half) in enumerate(halves):
            for k, (fx, fy) in enumerate(CHIP_FLIPS):
                cp = copy(a, k, half(src, c), half(out.at[j], c), (_flip(x, fx), _flip(y, fy), c))
                cp.start()
                started.append(cp)
        for a, (src, out, half) in enumerate(halves):
            for k, (fx, fy) in enumerate(CHIP_FLIPS):
                jr = 2 * _flip(x, fx) + _flip(y, fy)
                landed = half(out.at[jr], c)
                copy(a, k, landed, landed, sib).wait_recv()
                cp = copy(a, 3 + k, landed, landed, sib)
                cp.start()
                started.append(cp)
        for a, (src, out, half) in enumerate(halves):
            for k, (fx, fy) in enumerate(CHIP_FLIPS):
                jr = 2 * _flip(x, fx) + _flip(y, fy)
                other = half(out.at[jr], 1 - c)
                copy(a, 3 + k, other, other, sib).wait_recv()
        for cp in started:
            cp.wait_send()
        for cp in local:
            cp.wait()

    return pl.pallas_call(
        body, name="allgather_weights",
        in_specs=[HBM_SPEC, HBM_SPEC], out_specs=[HBM_SPEC, HBM_SPEC],
        out_shape=[jax.ShapeDtypeStruct((SHARDS, D, SHARD_W), BF16), jax.ShapeDtypeStruct((SHARDS, 3 * SQ_ROWS, D), BF16)],
        scratch_shapes=[pltpu.SemaphoreType.DMA((12,)), pltpu.SemaphoreType.DMA((12,)), pltpu.SemaphoreType.DMA((2,))],
    )(win_s, wsq_s)


def _swap_halves(gwin, gsq):
    def body(gwin_ref, gsq_ref, win_got, sq_got, send_sems, recv_sems):
        x, y, c = _place()
        sib = (x, y, 1 - c)
        pairs = ((gwin_ref.at[:, pl.ds((1 - c) * (D // 2), D // 2), :], win_got),
                 (gsq_ref.at[:, :, pl.ds((1 - c) * (D // 2), D // 2)], sq_got))
        copies = [pltpu.make_async_remote_copy(src_ref=src, dst_ref=dst, send_sem=send_sems.at[a], recv_sem=recv_sems.at[a],
                                               device_id=sib, device_id_type=MESH) for a, (src, dst) in enumerate(pairs)]
        for cp in copies:
            cp.start()
        for cp in copies:
            cp.wait()

    return pl.pallas_call(
        body, name="swap_halves",
        in_specs=[HBM_SPEC, HBM_SPEC], out_specs=[HBM_SPEC, HBM_SPEC],
        out_shape=[jax.ShapeDtypeStruct((SHARDS, D // 2, SHARD_W), F32), jax.ShapeDtypeStruct((3, D, D // 2), F32)],
        scratch_shapes=[pltpu.SemaphoreType.DMA((2,)), pltpu.SemaphoreType.DMA((2,))],
    )(gwin, gsq)


def _add_halves(c_arr, gwin, gsq, win_got, sq_got):
    def body(c_ref, a_ref, b_ref, p_ref, q_ref, so_ref, sq_ref, sob_ref, sqb_ref):
        so = a_ref[...] + b_ref[...]
        sq = p_ref[...] + q_ref[...]
        so_ref[...] = so
        sq_ref[...] = sq
        sob_ref[...] = _bf(so)
        sqb_ref[...] = _bf(sq)

    rows = 128
    steps = (D // 2) // rows
    return pl.pallas_call(
        body, name="add_halves",
        grid_spec=pltpu.PrefetchScalarGridSpec(
            num_scalar_prefetch=1, grid=(SHARDS, steps),
            in_specs=[pl.BlockSpec((None, rows, SHARD_W), lambda j, i, c: (j, c[0] * steps + i, 0)),
                      pl.BlockSpec((None, rows, SHARD_W), lambda j, i, c: (j, i, 0)),
                      pl.BlockSpec((3, SQ_ROWS // steps, D // 2), lambda j, i, c: (0, j * steps + i, c[0])),
                      pl.BlockSpec((3, SQ_ROWS // steps, D // 2), lambda j, i, c: (0, j * steps + i, 0))],
            out_specs=[pl.BlockSpec((None, rows, SHARD_W), lambda j, i, c: (j, i, 0)),
                       pl.BlockSpec((3, SQ_ROWS // steps, D // 2), lambda j, i, c: (0, j * steps + i, 0))] * 2),
        out_shape=[jax.ShapeDtypeStruct((SHARDS, D // 2, SHARD_W), F32), jax.ShapeDtypeStruct((3, D, D // 2), F32),
                   jax.ShapeDtypeStruct((SHARDS, D // 2, SHARD_W), BF16), jax.ShapeDtypeStruct((3, D, D // 2), BF16)],
        compiler_params=_params(("arbitrary", "arbitrary")),
    )(c_arr, gwin, win_got, gsq, sq_got)


def _scatter_chip_sums(swin, ssq):
    def body(swin_ref, ssq_ref, win_got, sq_got, send_sems, recv_sems):
        x, y, c = _place()
        copies = []
        for k, (fx, fy) in enumerate(CHIP_FLIPS):
            px, py = _flip(x, fx), _flip(y, fy)
            jr = 2 * px + py
            for a, (src, dst) in enumerate(((swin_ref.at[jr], win_got.at[k]),
                                            (ssq_ref.at[:, pl.ds(jr * SQ_ROWS, SQ_ROWS), :], sq_got.at[k]))):
                copies.append(pltpu.make_async_remote_copy(src_ref=src, dst_ref=dst, send_sem=send_sems.at[2 * k + a],
                                                           recv_sem=recv_sems.at[2 * k + a], device_id=(px, py, c), device_id_type=MESH))
        for cp in copies:
            cp.start()
        for cp in copies:
            cp.wait()

    return pl.pallas_call(
        body, name="scatter_chip_sums",
        in_specs=[HBM_SPEC, HBM_SPEC], out_specs=[HBM_SPEC, HBM_SPEC],
        out_shape=[jax.ShapeDtypeStruct((3, D // 2, SHARD_W), BF16), jax.ShapeDtypeStruct((3, 3, SQ_ROWS, D // 2), BF16)],
        scratch_shapes=[pltpu.SemaphoreType.DMA((6,)), pltpu.SemaphoreType.DMA((6,))],
    )(swin, ssq)


def _sum_chips(j_arr, swin, ssq, win_got, sq_got):
    def body(j_ref, a_ref, b_ref, p_ref, q_ref, so_ref, sq_ref):
        so_ref[...] = ((a_ref[...] + b_ref[0].astype(F32)) + b_ref[1].astype(F32)) + b_ref[2].astype(F32)
        sq_ref[...] = ((p_ref[...] + q_ref[0].astype(F32)) + q_ref[1].astype(F32)) + q_ref[2].astype(F32)

    rows = 128
    steps = (D // 2) // rows
    sq_rows = SQ_ROWS // steps
    return pl.pallas_call(
        body, name="sum_chips",
        grid_spec=pltpu.PrefetchScalarGridSpec(
            num_scalar_prefetch=1, grid=(steps,),
            in_specs=[pl.BlockSpec((None, rows, SHARD_W), lambda i, j: (j[0], i, 0)),
                      pl.BlockSpec((3, rows, SHARD_W), lambda i, j: (0, i, 0)),
                      pl.BlockSpec((3, sq_rows, D // 2), lambda i, j: (0, j[0] * steps + i, 0)),
                      pl.BlockSpec((3, 3, sq_rows, D // 2), lambda i, j: (0, 0, i, 0))],
            out_specs=[pl.BlockSpec((rows, SHARD_W), lambda i, j: (i, 0)),
                       pl.BlockSpec((3, sq_rows, D // 2), lambda i, j: (0, i, 0))]),
        out_shape=[jax.ShapeDtypeStruct((D // 2, SHARD_W), F32), jax.ShapeDtypeStruct((3, SQ_ROWS, D // 2), F32)],
        compiler_params=_params(("arbitrary",)),
    )(j_arr, swin, win_got, ssq, sq_got)


def _join_halves(fwin, fsq):
    def body(fwin_ref, fsq_ref, win_out, sq_out, send_sems, recv_sems, local_sems):
        x, y, c = _place()
        sib = (x, y, 1 - c)
        dsts = (_win_half(win_out, c), sq_out.at[:, :, pl.ds(c * (D // 2), D // 2)])
        srcs = (fwin_ref, fsq_ref)
        local = [pltpu.make_async_copy(srcs[a], dsts[a], local_sems.at[a]) for a in range(2)]
        remote = [pltpu.make_async_remote_copy(src_ref=srcs[a], dst_ref=dsts[a], send_sem=send_sems.at[a], recv_sem=recv_sems.at[a],
                                               device_id=sib, device_id_type=MESH) for a in range(2)]
        for cp in local + remote:
            cp.start()
        others = (_win_half(win_out, 1 - c), sq_out.at[:, :, pl.ds((1 - c) * (D // 2), D // 2)])
        for a in range(2):
            pltpu.make_async_remote_copy(src_ref=srcs[a], dst_ref=others[a], send_sem=send_sems.at[a], recv_sem=recv_sems.at[a],
                                         device_id=sib, device_id_type=MESH).wait_recv()
        for cp in remote:
            cp.wait_send()
        for cp in local:
            cp.wait()

    return pl.pallas_call(
        body, name="join_halves",
        in_specs=[HBM_SPEC, HBM_SPEC], out_specs=[HBM_SPEC, HBM_SPEC],
        out_shape=[jax.ShapeDtypeStruct((D, SHARD_W), F32), jax.ShapeDtypeStruct((3, SQ_ROWS, D), F32)],
        scratch_shapes=[pltpu.SemaphoreType.DMA((2,)), pltpu.SemaphoreType.DMA((2,)), pltpu.SemaphoreType.DMA((2,))],
    )(fwin, fsq)


def _allreduce_small(vec):
    def body(vec_ref, out_ref, slots, send_sems, recv_sems):
        x, y, c = _place()
        me = 4 * x + 2 * y + c
        slots[me] = vec_ref[...]
        copies = []
        for k in range(1, 8):
            fx, fy, fc = (k >> 2) & 1, (k >> 1) & 1, k & 1
            copies.append(pltpu.make_async_remote_copy(
                src_ref=vec_ref, dst_ref=slots.at[me], send_sem=send_sems.at[k - 1], recv_sem=recv_sems.at[k - 1],
                device_id=(_flip(x, fx), _flip(y, fy), _flip(c, fc)), device_id_type=MESH))
        for cp in copies:
            cp.start()
        for k in range(1, 8):
            fx, fy, fc = (k >> 2) & 1, (k >> 1) & 1, k & 1
            src = 4 * _flip(x, fx) + 2 * _flip(y, fy) + _flip(c, fc)
            pltpu.make_async_remote_copy(src_ref=vec_ref, dst_ref=slots.at[src], send_sem=send_sems.at[k - 1],
                                         recv_sem=recv_sems.at[k - 1], device_id=(x, y, c), device_id_type=MESH).wait_recv()
        for cp in copies:
            cp.wait_send()
        total = slots[0]
        for s in range(1, 8):
            total = total + slots[s]
        out_ref[...] = total

    return pl.pallas_call(
        body, name="allreduce_small",
        in_specs=[pl.BlockSpec(memory_space=pltpu.VMEM)], out_specs=pl.BlockSpec(memory_space=pltpu.VMEM),
        out_shape=jax.ShapeDtypeStruct((8, D), F32),
        scratch_shapes=[pltpu.VMEM((8, 8, D), F32), pltpu.SemaphoreType.DMA((7,)), pltpu.SemaphoreType.DMA((7,))],
    )(vec)


def _adamw_math(w, g, m, v):
    m = ADAM_B1 * m + (1.0 - ADAM_B1) * g
    v = ADAM_B2 * v + (1.0 - ADAM_B2) * (g * g)
    m_hat = m / (1.0 - ADAM_B1 ** ADAM_STEP)
    v_hat = v / (1.0 - ADAM_B2 ** ADAM_STEP)
    delta = -ADAM_LR * (m_hat / (jnp.sqrt(v_hat) + ADAM_EPS) + ADAM_WD * w)
    return delta, m, v


def _adamw(name, w, g, m, v, rows):
    R, C = w.shape

    def body(w_ref, g_ref, m_ref, v_ref, d_out, m_out, v_out):
        d_out[...], m_out[...], v_out[...] = _adamw_math(w_ref[...], g_ref[...], m_ref[...], v_ref[...])

    spec = pl.BlockSpec((rows, C), lambda i: (i, 0))
    return pl.pallas_call(
        body, name=name, grid=(R // rows,), in_specs=[spec] * 4, out_specs=[spec] * 3,
        out_shape=[jax.ShapeDtypeStruct((R, C), F32)] * 3,
        compiler_params=_params(("parallel",)),
    )(w, g, m, v)


def _small_update(total, lbw, w8, m8, v8):
    def body(t_ref, lbw_ref, w_ref, m_ref, v_ref, g_out, d_out, m_out, v_out):
        lb = 1.0 / (1.0 + jnp.exp(lbw_ref[1:2, :] - lbw_ref[0:1, :]))
        dlb = t_ref[2:3, :] * lb * (1.0 - lb)
        g_out[...] = jnp.zeros_like(g_out)
        g_out[0:1, :] = t_ref[3:4, :]
        g_out[1:2, :] = dlb
        g_out[2:3, :] = -dlb
        g_out[3:4, :] = t_ref[1:2, :]
        g_out[4:5, :] = t_ref[0:1, :]
        g_out[5:6, :] = t_ref[4:5, :]
        d_out[...], m_out[...], v_out[...] = _adamw_math(w_ref[...], g_out[...], m_ref[...], v_ref[...])

    return pl.pallas_call(
        body, name="small_update", out_shape=[jax.ShapeDtypeStruct((8, D), F32)] * 4,
        compiler_params=_params(),
    )(total, lbw, w8, m8, v8)


def _pack8(norm_w, lbw, hnw, fnw, sinks):
    pad = jnp.zeros((1, D - 16), F32)
    return jnp.concatenate([norm_w, lbw, hnw, fnw.reshape(1, D), jnp.concatenate([sinks, pad], axis=1),
                            jnp.zeros((2, D), F32)], axis=0)


def _unpack8(a):
    return a[0:1], a[1:3], a[3:4], a[5:6, 0:16], a[4]


def _local_step(x, tgt, norm_w, lbw, hnw, sinks, fnw, win_bf, wsq_bf):
    proj, xn_bf = _fwd_proj(x, norm_w, win_bf)
    oh, states = _hgrn_fwd(proj, lbw)
    oa = _attn_fwd(proj, sinks)
    dx2, doh, doa, dhg, dtail, lhs, rhs, loss8, vec_mid = _mid(x, tgt, proj, oh, oa, hnw, fnw.reshape(1, D), wsq_bf)
    gsq = _wgrad_square(lhs, rhs)
    dhead, dlb = _hgrn_bwd(proj, lbw, states, doh)
    daq, dak, dav, dsink = _attn_bwd(proj, sinks, oa, doa)
    wt_bf = win_bf.transpose(0, 2, 1).reshape(D_IN, D)
    grad_x, gnw, dproj = _bwd_dx([dhead, dhg, daq, dak, dav, dtail], wt_bf, x, norm_w, dx2)
    gwin = _bwd_win(xn_bf, dproj)
    sink_row = jnp.concatenate([dsink[:, 0].reshape(1, 16), jnp.zeros((1, D - 16), F32)], axis=1)
    vec = jnp.concatenate([vec_mid[0:2], dlb, gnw, sink_row, jnp.zeros((3, D), F32)], axis=0)
    return loss8[0, 0], grad_x, gwin, gsq, vec


def kernel(x, norm_w, w_in, hgrn_lower_bound, hgrn_norm_w, w_branch_hgrn, attn_sinks, w_branch_attn, w_out, final_norm_w, loss_target, m_norm_w, m_w_in, m_hgrn_lower_bound, m_hgrn_norm_w, m_w_branch_hgrn, m_attn_sinks, m_w_branch_attn, m_w_out, m_final_norm_w, v_norm_w, v_w_in, v_hgrn_lower_bound, v_hgrn_norm_w, v_w_branch_hgrn, v_attn_sinks, v_w_branch_attn, v_w_out, v_final_norm_w):
    c_arr = lax.axis_index("c").astype(jnp.int32).reshape(1)
    j_arr = (2 * lax.axis_index("x") + lax.axis_index("y")).astype(jnp.int32).reshape(1)

    win_s, wsq_s = _cast_shards(w_in[0], w_branch_hgrn[0], w_branch_attn[0], w_out[0])
    win_bf, wsq_all = _allgather_weights(win_s, wsq_s)
    wsq_bf = wsq_all.reshape(SHARDS, 3, SQ_ROWS, D).transpose(1, 0, 2, 3).reshape(3, D, D)

    loss_part, grad_x, gwin, gsq, vec = _local_step(
        x[0], loss_target[0], norm_w, hgrn_lower_bound, hgrn_norm_w, attn_sinks, final_norm_w, win_bf, wsq_bf)
    loss = lax.psum(loss_part, ("x", "y", "c"))

    win_got, sq_got = _swap_halves(gwin, gsq)
    swin, ssq, swin_b, ssq_b = _add_halves(c_arr, gwin, gsq, win_got, sq_got)
    win_got2, sq_got2 = _scatter_chip_sums(swin_b, ssq_b)
    fwin, fsq = _sum_chips(j_arr, swin, ssq, win_got2, sq_got2)
    g_win, g_sq = _join_halves(fwin, fsq)

    d_win, nm_win, nv_win = _adamw("adamw_w_in", w_in[0], g_win, m_w_in[0], v_w_in[0], 128)
    sq_w = jnp.concatenate([w_branch_hgrn[0], w_branch_attn[0], w_out[0]], axis=0)
    sq_m = jnp.concatenate([m_w_branch_hgrn[0], m_w_branch_attn[0], m_w_out[0]], axis=0)
    sq_v = jnp.concatenate([v_w_branch_hgrn[0], v_w_branch_attn[0], v_w_out[0]], axis=0)
    d_sq, nm_sq, nv_sq = _adamw("adamw_square", sq_w, g_sq.reshape(3 * SQ_ROWS, D), sq_m, sq_v, 256)

    total = _allreduce_small(vec)
    g8, d8, nm8, nv8 = _small_update(
        total, hgrn_lower_bound,
        _pack8(norm_w, hgrn_lower_bound, hgrn_norm_w, final_norm_w, attn_sinks),
        _pack8(m_norm_w, m_hgrn_lower_bound, m_hgrn_norm_w, m_final_norm_w, m_attn_sinks),
        _pack8(v_norm_w, v_hgrn_lower_bound, v_hgrn_norm_w, v_final_norm_w, v_attn_sinks))

    def assemble(win, sq, small):
        nw, lb, hn, sk, fn = _unpack8(small)
        sq = sq.reshape(3, 1, SQ_ROWS, D)
        return (nw, win.reshape(1, D, SHARD_W), lb, hn, sq[0], sk, sq[1], sq[2], fn)

    return (loss, grad_x.reshape(1, -1, D),
            *assemble(g_win, g_sq, g8), *assemble(d_win, d_sq, d8),
            *assemble(nm_win, nm_sq, nm8), *assemble(nv_win, nv_sq, nv8))
```

```python
import functools

import jax
import jax.numpy as jnp
from jax import lax
from jax.experimental import pallas as pl
from jax.experimental.pallas import tpu as pltpu

F32 = jnp.float32
BF16 = jnp.bfloat16

D = 1024
D_IN = 8704
SHARDS = 4
SHARD_W = D_IN // SHARDS
SQ_ROWS = D // SHARDS
HEADS = 8
HEAD_W = 128
CHUNK = 64
SUB = 2
ATT_BLOCK = 128
KV_HEADS = 4
HEAD_DIM = 64
EPS = 1e-6
NEG = -1e30
SCALE = HEAD_DIM ** -0.5
COL_HG, COL_AQ, COL_AK, COL_AV, COL_AG, COL_MH, COL_MA = 3072, 4096, 5120, 5376, 5632, 6656, 7680

ADAM_LR, ADAM_B1, ADAM_B2, ADAM_EPS, ADAM_WD, ADAM_STEP = 0.001, 0.9, 0.999, 1e-08, 0.01, 10

VMEM_LIMIT = 56 * 1024 * 1024
MESH = pl.DeviceIdType.MESH
HBM_SPEC = pl.BlockSpec(memory_space=pltpu.HBM)
CHIP_FLIPS = ((1, 0), (0, 1), (1, 1))


def _dot(a, b):
    return jnp.dot(a, b, preferred_element_type=F32)


def _dot_nt(a, b):
    return lax.dot_general(a, b, (((1,), (1,)), ((), ())), preferred_element_type=F32)


def _dot_tn(a, b):
    return lax.dot_general(a, b, (((0,), (0,)), ((), ())), preferred_element_type=F32)


def _sigmoid(v):
    return 1.0 / (1.0 + jnp.exp(-v))


def _bf(v):
    return v.astype(BF16)


def _split3(v):
    a = _bf(v)
    r = v - a.astype(F32)
    b = _bf(r)
    c = _bf(r - b.astype(F32))
    return a, b, c


def _tri_dot(tri, v):
    a, b, c = _split3(v)
    return _dot(tri, a) + _dot(tri, b) + _dot(tri, c)


def _params(sem=None):
    return pltpu.CompilerParams(dimension_semantics=sem, vmem_limit_bytes=VMEM_LIMIT)


def _cast_shards(j_arr, win_s, wbh_s, wba_s, wout_s):
    steps = 4
    rows = D // steps

    def body(j_ref, win_ref, a_ref, b_ref, c_ref, win_o, sq_o):
        win_o[...] = _bf(win_ref[...])

        @pl.when(pl.program_id(0) == 0)
        def _():
            sq_o[0:SQ_ROWS, :] = _bf(a_ref[...])
            sq_o[SQ_ROWS:2 * SQ_ROWS, :] = _bf(b_ref[...])
            sq_o[2 * SQ_ROWS:3 * SQ_ROWS, :] = _bf(c_ref[...])

    whole = pl.BlockSpec((SQ_ROWS, D), lambda i, j: (0, 0))
    return pl.pallas_call(
        body, name="cast_shards",
        grid_spec=pltpu.PrefetchScalarGridSpec(
            num_scalar_prefetch=1, grid=(steps,),
            in_specs=[pl.BlockSpec((rows, SHARD_W), lambda i, j: (i, 0)), whole, whole, whole],
            out_specs=[pl.BlockSpec((None, rows, SHARD_W), lambda i, j: (j[0], i, 0)),
                       pl.BlockSpec((None, 3 * SQ_ROWS, D), lambda i, j: (j[0], 0, 0))]),
        out_shape=[jax.ShapeDtypeStruct((SHARDS, D, SHARD_W), BF16), jax.ShapeDtypeStruct((SHARDS, 3 * SQ_ROWS, D), BF16)],
        compiler_params=_params(("arbitrary",)),
    )(j_arr, win_s, wbh_s, wba_s, wout_s)


def _fwd_proj(x, norm_w, win_bf):
    T = x.shape[0]
    tm = min(256, T)

    def body(x_ref, nw_ref, w_hbm, proj_ref, xn_ref, w_scr, sem):
        @pl.when(pl.program_id(0) == 0)
        def _():
            cp = pltpu.make_async_copy(w_hbm, w_scr, sem)
            cp.start()
            cp.wait()

        xf = x_ref[...]
        rs = lax.rsqrt(jnp.mean(xf * xf, axis=1, keepdims=True) + EPS)
        xn = _bf((xf * rs) * nw_ref[...])
        xn_ref[...] = xn
        for j in range(SHARDS):
            proj_ref[:, j * SHARD_W:(j + 1) * SHARD_W] = _dot(xn, w_scr[j])

    return pl.pallas_call(
        body, name="fwd_proj", grid=(T // tm,),
        in_specs=[pl.BlockSpec((tm, D), lambda i: (i, 0)), pl.BlockSpec((1, D), lambda i: (0, 0)), HBM_SPEC],
        out_specs=[pl.BlockSpec((tm, D_IN), lambda i: (i, 0)), pl.BlockSpec((tm, D), lambda i: (i, 0))],
        out_shape=[jax.ShapeDtypeStruct((T, D_IN), F32), jax.ShapeDtypeStruct((T, D), BF16)],
        scratch_shapes=[pltpu.VMEM((SHARDS, D, SHARD_W), BF16), pltpu.SemaphoreType.DMA],
        compiler_params=_params(("arbitrary",)),
    )(x, norm_w, win_bf)


def _hgrn_gates(hq_ref, hf_ref, lbw_ref, b_scr):
    lb = 1.0 / (1.0 + jnp.exp(lbw_ref[1:2, :] - lbw_ref[0:1, :]))
    hf = hf_ref[...]
    sig = _sigmoid(hf)
    f = lb + (1.0 - lb) * sig
    g = jnp.log(f)
    hq = hq_ref[...]
    sq = _sigmoid(hq)
    q = hq * sq
    row = lax.broadcasted_iota(jnp.int32, (CHUNK, CHUNK), 0)
    col = lax.broadcasted_iota(jnp.int32, (CHUNK, CHUNK), 1)
    causal = row >= col
    b = _tri_dot(jnp.where(causal, 1.0, 0.0).astype(BF16), g)
    b_scr[...] = b
    bc = b_scr[CHUNK - 1:CHUNK, :]
    r = b_scr[CHUNK // 2 - 1:CHUNK // 2, :]
    return dict(lb=lb, sig=sig, f=f, k=1.0 - f, hq=hq, sq=sq, q=q, b=b, bc=bc, r=r, causal=causal)


def _hgrn_fwd(proj, lbw):
    T = proj.shape[0]
    n = T // CHUNK

    def body(hq_ref, hf_ref, hi_ref, lbw_ref, o_ref, st_ref, s_scr, b_scr):
        @pl.when(pl.program_id(0) == 0)
        def _():
            s_scr[...] = jnp.zeros_like(s_scr)

        for c in range(SUB):
            rows = pl.ds(c * CHUNK, CHUNK)
            gt = _hgrn_gates(hq_ref.at[rows, :], hf_ref.at[rows, :], lbw_ref, b_scr.at[rows, :])
            b, bc, r, q, k = gt["b"], gt["bc"], gt["r"], gt["q"], gt["k"]
            qe = _bf(q * jnp.exp(b))
            qr = _bf(q * jnp.exp(b - r))
            kr = _bf(k * jnp.exp(r - b))
            kl = _bf(k * jnp.exp(bc - b))
            ebc = jnp.exp(bc)
            v = _bf(hi_ref[rows, :])
            for h in range(HEADS):
                sl = slice(h * HEAD_W, (h + 1) * HEAD_W)
                st = s_scr[h]
                st_ref[c, h] = st
                a = jnp.where(gt["causal"], _dot_nt(qr[:, sl], kr[:, sl]), 0.0)
                o_ref[rows, sl] = _dot(_bf(a), v[:, sl]) + _dot_nt(qe[:, sl], _bf(st))
                s_scr[h] = ebc[:, sl] * st + _dot_tn(v[:, sl], kl[:, sl])

    col = lambda j: pl.BlockSpec((SUB * CHUNK, D), lambda i: (i, j))
    return pl.pallas_call(
        body, name="hgrn_fwd", grid=(n // SUB,),
        in_specs=[col(0), col(1), col(2), pl.BlockSpec((2, D), lambda i: (0, 0))],
        out_specs=[pl.BlockSpec((SUB * CHUNK, D), lambda i: (i, 0)),
                   pl.BlockSpec((SUB, HEADS, HEAD_W, HEAD_W), lambda i: (i, 0, 0, 0))],
        out_shape=[jax.ShapeDtypeStruct((T, D), F32), jax.ShapeDtypeStruct((n, HEADS, HEAD_W, HEAD_W), F32)],
        scratch_shapes=[pltpu.VMEM((HEADS, HEAD_W, HEAD_W), F32), pltpu.VMEM((SUB * CHUNK, D), F32)],
        compiler_params=_params(("arbitrary",)),
    )(proj, proj, proj, lbw)


def _hgrn_bwd(proj, lbw, states, do):
    T = proj.shape[0]
    n = T // CHUNK

    def body(hq_ref, hf_ref, hi_ref, lbw_ref, st_ref, do_ref, dp_ref, dlb_ref,
             ds_scr, b_scr, dq_scr, dk_scr, dv_scr, late_scr, early_scr, ex_scr):
        @pl.when(pl.program_id(0) == 0)
        def _():
            ds_scr[...] = jnp.zeros_like(ds_scr)
            dlb_ref[...] = jnp.zeros_like(dlb_ref)

        for c in reversed(range(SUB)):
            rows = pl.ds(c * CHUNK, CHUNK)
            gt = _hgrn_gates(hq_ref.at[rows, :], hf_ref.at[rows, :], lbw_ref, b_scr.at[rows, :])
            b, bc, r, q, k = gt["b"], gt["bc"], gt["r"], gt["q"], gt["k"]
            eb = jnp.exp(b)
            er = jnp.exp(b - r)
            erk = jnp.exp(r - b)
            el = jnp.exp(bc - b)
            ebc = jnp.exp(bc)
            qe, qr, kr, kl = _bf(q * eb), _bf(q * er), _bf(k * erk), _bf(k * el)
            v = _bf(hi_ref[rows, :])
            do_b = _bf(do_ref[rows, :])
            for h in range(HEADS):
                sl = slice(h * HEAD_W, (h + 1) * HEAD_W)
                st0 = st_ref[c, h]
                dst = ds_scr[h]
                dst_b = _bf(dst)
                a = _bf(jnp.where(gt["causal"], _dot_nt(qr[:, sl], kr[:, sl]), 0.0))
                da = _bf(jnp.where(gt["causal"], _dot_nt(do_b[:, sl], v[:, sl]), 0.0))
                mq = _dot(da, kr[:, sl])
                mk = _dot_tn(da, qr[:, sl])
                dq_in = eb[:, sl] * _dot(do_b[:, sl], _bf(st0))
                dk_in = el[:, sl] * _dot(v[:, sl], dst_b)
                dq_scr[rows, sl] = er[:, sl] * mq + dq_in
                dk_scr[rows, sl] = erk[:, sl] * mk + dk_in
                dv_scr[rows, sl] = _dot_tn(a, do_b[:, sl]) + _dot_nt(kl[:, sl], dst_b)
                late_scr[rows, sl] = q[:, sl] * dq_in + qr[:, sl].astype(F32) * mq - kr[:, sl].astype(F32) * mk
                early_scr[rows, sl] = k[:, sl] * dk_in
                ex_scr[:, sl] = jnp.sum(dst * st0, axis=0, keepdims=True)
                ds_scr[h] = ebc[:, sl] * dst + _dot_tn(do_b[:, sl], qe[:, sl])

            dq, dk = dq_scr[rows, :], dk_scr[rows, :]
            row = lax.broadcasted_iota(jnp.int32, (CHUNK, CHUNK), 0)
            col = lax.broadcasted_iota(jnp.int32, (CHUNK, CHUNK), 1)
            at_or_after = jnp.where(col >= row, 1.0, 0.0).astype(BF16)
            before = jnp.where(col < row, 1.0, 0.0).astype(BF16)
            dg = _tri_dot(at_or_after, late_scr[rows, :]) + _tri_dot(before, early_scr[rows, :]) + ebc * ex_scr[...]
            df = dg / gt["f"] - dk
            sig, sq, hq, lb = gt["sig"], gt["sq"], gt["hq"], gt["lb"]
            dp_ref[rows, 0:D] = _bf(dq * (sq * (1.0 + hq * (1.0 - sq))))
            dp_ref[rows, D:2 * D] = _bf(df * (1.0 - lb) * sig * (1.0 - sig))
            dp_ref[rows, 2 * D:3 * D] = _bf(dv_scr[rows, :])
            dlb_ref[...] += jnp.sum(df * (1.0 - sig), axis=0, keepdims=True)

    ns = n // SUB
    col = lambda j: pl.BlockSpec((SUB * CHUNK, D), lambda i: (ns - 1 - i, j))
    return pl.pallas_call(
        body, name="hgrn_bwd", grid=(ns,),
        in_specs=[col(0), col(1), col(2), pl.BlockSpec((2, D), lambda i: (0, 0)),
                  pl.BlockSpec((SUB, HEADS, HEAD_W, HEAD_W), lambda i: (ns - 1 - i, 0, 0, 0)),
                  pl.BlockSpec((SUB * CHUNK, D), lambda i: (ns - 1 - i, 0))],
        out_specs=[pl.BlockSpec((SUB * CHUNK, 3 * D), lambda i: (ns - 1 - i, 0)),
                   pl.BlockSpec((1, D), lambda i: (0, 0))],
        out_shape=[jax.ShapeDtypeStruct((T, 3 * D), BF16), jax.ShapeDtypeStruct((1, D), F32)],
        scratch_shapes=[pltpu.VMEM((HEADS, HEAD_W, HEAD_W), F32)] + [pltpu.VMEM((SUB * CHUNK, D), F32)] * 6
                       + [pltpu.VMEM((1, D), F32)],
        compiler_params=_params(("arbitrary",)),
    )(proj, proj, proj, lbw, states, do)


def _attn_masks(blk):
    qi = lax.broadcasted_iota(jnp.int32, (ATT_BLOCK, 2 * ATT_BLOCK), 0)
    kj = lax.broadcasted_iota(jnp.int32, (ATT_BLOCK, 2 * ATT_BLOCK), 1)
    band = (kj > qi) & (kj <= qi + ATT_BLOCK)
    return band & ((blk > 0) | (kj >= ATT_BLOCK))


def _head_pair_operand(t, hp, low):
    mine = low if hp == 0 else jnp.logical_not(low)
    both = jnp.where(mine, t, pltpu.roll(t, HEAD_DIM, 1))
    return _bf(jnp.concatenate([jnp.where(low, both, 0.0), jnp.where(low, 0.0, both)], axis=0))


def _attn_probs(s, sink, valid):
    s = jnp.where(valid, s * SCALE, NEG)
    m = jnp.maximum(jnp.max(s, axis=1, keepdims=True), sink)
    p = jnp.exp(s - m)
    es = jnp.exp(sink - m)
    inv = 1.0 / (jnp.sum(p, axis=1, keepdims=True) + es)
    return p * inv, es * inv


def _attn_fwd(proj, sinks):
    T = proj.shape[0]
    nb = T // ATT_BLOCK
    W2 = 2 * ATT_BLOCK

    def body(sink_ref, q_ref, kp_ref, kc_ref, vp_ref, vc_ref, o_ref):
        blk = pl.program_id(0)
        valid = _attn_masks(blk)
        low = lax.broadcasted_iota(jnp.int32, (1, 2 * HEAD_DIM), 1) < HEAD_DIM
        kcat = jnp.concatenate([kp_ref[...], kc_ref[...]], axis=0)
        vcat = jnp.concatenate([vp_ref[...], vc_ref[...]], axis=0)
        for h in range(KV_HEADS):
            tl = slice((h // 2) * 128, (h // 2) * 128 + 128)
            mine = low if h % 2 == 0 else jnp.logical_not(low)
            kh = _bf(jnp.where(mine, kcat[:, tl], pltpu.roll(kcat[:, tl], HEAD_DIM, 1)))
            vh = _bf(jnp.where(mine, vcat[:, tl], pltpu.roll(vcat[:, tl], HEAD_DIM, 1)))
            for t in range(2):
                ql = slice((2 * h + t) * 128, (2 * h + t) * 128 + 128)
                q2 = q_ref[:, ql]
                outs = []
                for p in range(2):
                    qm = _bf(jnp.where(low if p == 0 else jnp.logical_not(low), q2, 0.0))
                    probs, _ = _attn_probs(_dot_nt(qm, kh), sink_ref[0, 4 * h + 2 * t + p], valid)
                    outs.append(_dot(_bf(probs), vh))
                o_ref[:, ql] = jnp.where(low, outs[0], outs[1])

    prev = lambda i: jnp.maximum(i - 1, 0)
    return pl.pallas_call(
        body, name="attn_fwd", grid=(nb,),
        in_specs=[pl.BlockSpec(memory_space=pltpu.SMEM),
                  pl.BlockSpec((ATT_BLOCK, D), lambda i: (i, COL_AQ // D)),
                  pl.BlockSpec((ATT_BLOCK, 256), lambda i: (prev(i), COL_AK // 256)),
                  pl.BlockSpec((ATT_BLOCK, 256), lambda i: (i, COL_AK // 256)),
                  pl.BlockSpec((ATT_BLOCK, 256), lambda i: (prev(i), COL_AV // 256)),
                  pl.BlockSpec((ATT_BLOCK, 256), lambda i: (i, COL_AV // 256))],
        out_specs=pl.BlockSpec((ATT_BLOCK, D), lambda i: (i, 0)),
        out_shape=jax.ShapeDtypeStruct((T, D), F32),
        compiler_params=_params(("arbitrary",)),
    )(sinks, proj, proj, proj, proj, proj)


def _attn_bwd(proj, sinks, o, do):
    T = proj.shape[0]
    nb = T // ATT_BLOCK
    W2 = 2 * ATT_BLOCK

    def body(sink_ref, q_ref, kp_ref, kc_ref, vp_ref, vc_ref, o_ref, do_ref,
             dq_ref, dk_ref, dv_ref, dsink_ref, ck_scr, cv_scr, nk_scr, nv_scr):
        blk = pl.program_id(0)

        @pl.when(blk == 0)
        def _():
            ck_scr[...] = jnp.zeros_like(ck_scr)
            cv_scr[...] = jnp.zeros_like(cv_scr)
            dsink_ref[...] = jnp.zeros_like(dsink_ref)

        @pl.when(blk < nb)
        def _():
            valid = _attn_masks(blk)
            low = lax.broadcasted_iota(jnp.int32, (1, 2 * HEAD_DIM), 1) < HEAD_DIM
            kcat = jnp.concatenate([kp_ref[...], kc_ref[...]], axis=0)
            vcat = jnp.concatenate([vp_ref[...], vc_ref[...]], axis=0)
            for h in range(KV_HEADS):
                tl = slice((h // 2) * 128, (h // 2) * 128 + 128)
                kbd = _head_pair_operand(kcat[:, tl], h % 2, low)
                vbd = _head_pair_operand(vcat[:, tl], h % 2, low)
                dkbd = jnp.zeros((2 * W2, 128), F32)
                dvbd = jnp.zeros((2 * W2, 128), F32)
                for t in range(2):
                    ql = slice((2 * h + t) * 128, (2 * h + t) * 128 + 128)
                    head = 4 * h + 2 * t
                    q2 = _bf(q_ref[:, ql])
                    do2 = do_ref[:, ql]
                    do2_b = _bf(do2)
                    doo = do2 * o_ref[:, ql]
                    dsum0 = jnp.sum(jnp.where(low, doo, 0.0), axis=1, keepdims=True)
                    dsum1 = jnp.sum(jnp.where(low, 0.0, doo), axis=1, keepdims=True)
                    s2 = _dot_nt(q2, kbd)
                    p0, ps0 = _attn_probs(s2[:, 0:W2], sink_ref[0, head], valid)
                    p1, ps1 = _attn_probs(s2[:, W2:2 * W2], sink_ref[0, head + 1], valid)
                    dp2 = _dot_nt(do2_b, vbd)
                    ds2 = _bf(jnp.concatenate([p0 * (dp2[:, 0:W2] - dsum0), p1 * (dp2[:, W2:2 * W2] - dsum1)], axis=1) * SCALE)
                    dq_ref[:, ql] = _bf(_dot(ds2, kbd))
                    dkbd = dkbd + _dot_tn(ds2, q2)
                    dvbd = dvbd + _dot_tn(_bf(jnp.concatenate([p0, p1], axis=1)), do2_b)
                    dsink_ref[head:head + 1, :] += jnp.zeros((1, 128), F32) - jnp.sum(ps0 * dsum0, axis=0, keepdims=True)
                    dsink_ref[head + 1:head + 2, :] += jnp.zeros((1, 128), F32) - jnp.sum(ps1 * dsum1, axis=0, keepdims=True)
                dk2 = jnp.where(low, dkbd[0:W2], dkbd[W2:2 * W2])
                dv2 = jnp.where(low, dvbd[0:W2], dvbd[W2:2 * W2])
                dk2 = dk2 + pltpu.roll(dk2, HEAD_DIM, 1)
                dv2 = dv2 + pltpu.roll(dv2, HEAD_DIM, 1)
                if h % 2 == 0:
                    keep_k, keep_v = dk2, dv2
                else:
                    nk_scr[:, tl] = jnp.where(low, keep_k, dk2)
                    nv_scr[:, tl] = jnp.where(low, keep_v, dv2)
            dk_ref[...] = _bf(ck_scr[...] + nk_scr[0:ATT_BLOCK, :])
            dv_ref[...] = _bf(cv_scr[...] + nv_scr[0:ATT_BLOCK, :])
            ck_scr[...] = nk_scr[ATT_BLOCK:2 * ATT_BLOCK, :]
            cv_scr[...] = nv_scr[ATT_BLOCK:2 * ATT_BLOCK, :]

        @pl.when(blk == nb)
        def _():
            dk_ref[...] = _bf(ck_scr[...])
            dv_ref[...] = _bf(cv_scr[...])

    cur = lambda i: jnp.minimum(i, nb - 1)
    prev = lambda i: jnp.maximum(cur(i) - 1, 0)
    late = lambda i: jnp.maximum(i - 1, 0)
    dq, dk, dv, dsink = pl.pallas_call(
        body, name="attn_bwd", grid=(nb + 1,),
        in_specs=[pl.BlockSpec(memory_space=pltpu.SMEM),
                  pl.BlockSpec((ATT_BLOCK, D), lambda i: (cur(i), COL_AQ // D)),
                  pl.BlockSpec((ATT_BLOCK, 256), lambda i: (prev(i), COL_AK // 256)),
                  pl.BlockSpec((ATT_BLOCK, 256), lambda i: (cur(i), COL_AK // 256)),
                  pl.BlockSpec((ATT_BLOCK, 256), lambda i: (prev(i), COL_AV // 256)),
                  pl.BlockSpec((ATT_BLOCK, 256), lambda i: (cur(i), COL_AV // 256)),
                  pl.BlockSpec((ATT_BLOCK, D), lambda i: (cur(i), 0)),
                  pl.BlockSpec((ATT_BLOCK, D), lambda i: (cur(i), 0))],
        out_specs=[pl.BlockSpec((ATT_BLOCK, D), lambda i: (cur(i), 0)),
                   pl.BlockSpec((ATT_BLOCK, 256), lambda i: (late(i), 0)),
                   pl.BlockSpec((ATT_BLOCK, 256), lambda i: (late(i), 0)),
                   pl.BlockSpec((16, 128), lambda i: (0, 0))],
        out_shape=[jax.ShapeDtypeStruct((T, D), BF16), jax.ShapeDtypeStruct((T, 256), BF16),
                   jax.ShapeDtypeStruct((T, 256), BF16), jax.ShapeDtypeStruct((16, 128), F32)],
        scratch_shapes=[pltpu.VMEM((ATT_BLOCK, 256), F32), pltpu.VMEM((ATT_BLOCK, 256), F32),
                        pltpu.VMEM((2 * ATT_BLOCK, 256), F32), pltpu.VMEM((2 * ATT_BLOCK, 256), F32)],
        compiler_params=_params(("arbitrary",)),
    )(sinks, proj, proj, proj, proj, proj, o, do)
    return dq, dk, dv, dsink


def _mid(x, tgt, proj, oh, oa, hnw, fnw, wsq_bf):
    T = x.shape[0]
    tm = min(256, T)
    nt = T // tm

    def body(x_ref, tgt_ref, oh_ref, oa_ref, hg_ref, ag0_ref, ag1_ref, mh0_ref, mh1_ref, ma0_ref, ma1_ref,
             hnw_ref, fnw_ref, w_hbm,
             dx2_ref, doh_ref, doa_ref, dhg_ref, dtail_ref, lhs_ref, rhs_ref, loss_ref, vec_ref,
             w_scr, xh_scr, rs_scr, sem):
        @pl.when(pl.program_id(0) == 0)
        def _():
            cp = pltpu.make_async_copy(w_hbm, w_scr, sem)
            cp.start()
            cp.wait()
            loss_ref[...] = jnp.zeros_like(loss_ref)
            vec_ref[...] = jnp.zeros_like(vec_ref)

        oh = oh_ref[...]
        for h in range(HEADS):
            sl = slice(h * HEAD_W, (h + 1) * HEAD_W)
            ohh = oh[:, sl]
            rs = lax.rsqrt(jnp.mean(ohh * ohh, axis=1, keepdims=True) + EPS)
            xh_scr[:, sl] = ohh * rs
            rs_scr[:, sl] = jnp.broadcast_to(rs, (tm, HEAD_W))
        xh = xh_scr[...]
        hnw = hnw_ref[...]
        on = xh * hnw
        hg = hg_ref[...]
        sg = _sigmoid(hg)
        silu_g = hg * sg
        gated_h = _bf(on * silu_g)
        lhs_ref[0] = gated_h
        yh = _dot(gated_h, w_scr[0])
        oa = oa_ref[...]
        ag = jnp.concatenate([ag0_ref[...], ag1_ref[...]], axis=1)
        sa = _sigmoid(ag)
        silu_a = ag * sa
        gated_a = _bf(oa * silu_a)
        lhs_ref[1] = gated_a
        ya = _dot(gated_a, w_scr[1])
        smh = _sigmoid(jnp.concatenate([mh0_ref[...], mh1_ref[...]], axis=1))
        sma = _sigmoid(jnp.concatenate([ma0_ref[...], ma1_ref[...]], axis=1))
        merged = _bf(smh * yh + sma * ya)
        lhs_ref[2] = merged
        x2 = x_ref[...] + _dot(merged, w_scr[2])
        rs2 = lax.rsqrt(jnp.mean(x2 * x2, axis=1, keepdims=True) + EPS)
        xh2 = x2 * rs2
        fnw = fnw_ref[...]
        diff = xh2 * fnw - tgt_ref[...]
        loss_ref[...] += jnp.zeros_like(loss_ref) + jnp.sum(diff * diff) * (0.5 / D)

        dy = diff * (1.0 / D)
        vec_ref[0:1, :] += jnp.sum(dy * xh2, axis=0, keepdims=True)
        gy = dy * fnw
        dx2 = rs2 * (gy - xh2 * jnp.mean(gy * xh2, axis=1, keepdims=True))
        dx2_ref[...] = dx2
        dx2_b = _bf(dx2)
        rhs_ref[2] = dx2_b
        dmerged = _dot_nt(dx2_b, w_scr[2])
        dyh = dmerged * smh
        dya = dmerged * sma
        dtail_ref[:, D:2 * D] = _bf(dyh * yh * (1.0 - smh))
        dtail_ref[:, 2 * D:3 * D] = _bf(dya * ya * (1.0 - sma))
        dyh_b, dya_b = _bf(dyh), _bf(dya)
        rhs_ref[0] = dyh_b
        rhs_ref[1] = dya_b
        dgh = _dot_nt(dyh_b, w_scr[0])
        dga = _dot_nt(dya_b, w_scr[1])
        don = dgh * silu_g
        dhg_ref[...] = _bf(dgh * on * (sg * (1.0 + hg * (1.0 - sg))))
        vec_ref[1:2, :] += jnp.sum(don * xh, axis=0, keepdims=True)
        gxh = don * hnw
        rsb = rs_scr[...]
        for h in range(HEADS):
            sl = slice(h * HEAD_W, (h + 1) * HEAD_W)
            gh, xhh = gxh[:, sl], xh[:, sl]
            doh_ref[:, sl] = rsb[:, sl] * (gh - xhh * jnp.mean(gh * xhh, axis=1, keepdims=True))
        doa_ref[...] = dga * silu_a
        dtail_ref[:, 0:D] = _bf(dga * oa * (sa * (1.0 + ag * (1.0 - sa))))

    row = lambda w, j: pl.BlockSpec((tm, w), lambda i: (i, j))
    const = lambda r, c: pl.BlockSpec((r, c), lambda i: (0, 0))
    stack = pl.BlockSpec((3, tm, D), lambda i: (0, i, 0))
    return pl.pallas_call(
        body, name="mid", grid=(nt,),
        in_specs=[row(D, 0), row(D, 0), row(D, 0), row(D, 0), row(D, COL_HG // D),
                  row(512, COL_AG // 512), row(512, COL_AG // 512 + 1),
                  row(512, COL_MH // 512), row(512, COL_MH // 512 + 1),
                  row(512, COL_MA // 512), row(512, COL_MA // 512 + 1),
                  const(1, D), const(1, D), HBM_SPEC],
        out_specs=[row(D, 0), row(D, 0), row(D, 0), row(D, 0), row(3 * D, 0), stack, stack, const(8, 128), const(8, D)],
        out_shape=[jax.ShapeDtypeStruct((T, D), F32), jax.ShapeDtypeStruct((T, D), F32), jax.ShapeDtypeStruct((T, D), F32),
                   jax.ShapeDtypeStruct((T, D), BF16), jax.ShapeDtypeStruct((T, 3 * D), BF16),
                   jax.ShapeDtypeStruct((3, T, D), BF16), jax.ShapeDtypeStruct((3, T, D), BF16),
                   jax.ShapeDtypeStruct((8, 128), F32), jax.ShapeDtypeStruct((8, D), F32)],
        scratch_shapes=[pltpu.VMEM((3, D, D), BF16), pltpu.VMEM((tm, D), F32), pltpu.VMEM((tm, D), F32),
                        pltpu.SemaphoreType.DMA],
        compiler_params=_params(("arbitrary",)),
    )(x, tgt, oh, oa, proj, proj, proj, proj, proj, proj, proj, hnw, fnw, wsq_bf)


def _wgrad_square(lhs, rhs):
    T = lhs.shape[1]
    tk = min(1024, T)

    def body(a_ref, b_ref, g_ref):
        part = _dot_tn(a_ref[...], b_ref[...])

        @pl.when(pl.program_id(1) == 0)
        def _():
            g_ref[...] = part

        @pl.when(pl.program_id(1) > 0)
        def _():
            g_ref[...] += part

    spec = pl.BlockSpec((None, tk, D), lambda k, i: (k, i, 0))
    return pl.pallas_call(
        body, name="wgrad_square", grid=(3, T // tk), in_specs=[spec, spec],
        out_specs=pl.BlockSpec((None, D, D), lambda k, i: (k, 0, 0)),
        out_shape=jax.ShapeDtypeStruct((3, D, D), F32),
        compiler_params=_params(("parallel", "arbitrary")),
    )(lhs, rhs)


def _bwd_dx(pieces, wt_bf, x, norm_w, dx2):
    T = x.shape[0]
    tm = min(256, T)
    widths = [p.shape[1] for p in pieces]
    n_p = len(pieces)

    def body(*refs):
        piece_refs = refs[:n_p]
        w_hbm, x_ref, nw_ref, dx2_ref, gx_ref, gnw_ref, dp_ref, w_scr, sem = refs[n_p:]

        @pl.when(pl.program_id(0) == 0)
        def _():
            cp = pltpu.make_async_copy(w_hbm, w_scr, sem)
            cp.start()
            cp.wait()
            gnw_ref[...] = jnp.zeros_like(gnw_ref)

        dxn = None
        off = 0
        for ref, w in zip(piece_refs, widths):
            blk = ref[...]
            dp_ref[:, off:off + w] = blk
            part = _dot(blk, w_scr[off:off + w, :])
            dxn = part if dxn is None else dxn + part
            off += w
        xf = x_ref[...]
        rs = lax.rsqrt(jnp.mean(xf * xf, axis=1, keepdims=True) + EPS)
        xh = xf * rs
        gnw_ref[...] += jnp.sum(dxn * xh, axis=0, keepdims=True)
        gx = dxn * nw_ref[...]
        gx_ref[...] = rs * (gx - xh * jnp.mean(gx * xh, axis=1, keepdims=True)) + dx2_ref[...]

    row = lambda w: pl.BlockSpec((tm, w), lambda i: (i, 0))
    return pl.pallas_call(
        body, name="bwd_dx", grid=(T // tm,),
        in_specs=[row(w) for w in widths] + [HBM_SPEC, row(D), pl.BlockSpec((1, D), lambda i: (0, 0)), row(D)],
        out_specs=[row(D), pl.BlockSpec((1, D), lambda i: (0, 0)), row(D_IN)],
        out_shape=[jax.ShapeDtypeStruct((T, D), F32), jax.ShapeDtypeStruct((1, D), F32), jax.ShapeDtypeStruct((T, D_IN), BF16)],
        scratch_shapes=[pltpu.VMEM((D_IN, D), BF16), pltpu.SemaphoreType.DMA],
        compiler_params=_params(("arbitrary",)),
    )(*pieces, wt_bf, x, norm_w, dx2)


def _bwd_win(xn_bf, dproj):
    T = xn_bf.shape[0]
    tm = min(1024, T)

    def body(xn_ref, dp_ref, g_ref):
        part = _dot_tn(xn_ref[...], dp_ref[...])

        @pl.when(pl.program_id(1) == 0)
        def _():
            g_ref[...] = part

        @pl.when(pl.program_id(1) > 0)
        def _():
            g_ref[...] += part

    return pl.pallas_call(
        body, name="bwd_win", grid=(SHARDS, T // tm),
        in_specs=[pl.BlockSpec((tm, D), lambda j, i: (i, 0)), pl.BlockSpec((tm, SHARD_W), lambda j, i: (i, j))],
        out_specs=pl.BlockSpec((None, D, SHARD_W), lambda j, i: (j, 0, 0)),
        out_shape=jax.ShapeDtypeStruct((SHARDS, D, SHARD_W), F32),
        compiler_params=_params(("parallel", "arbitrary")),
    )(xn_bf, dproj)


def _place():
    return lax.axis_index("x"), lax.axis_index("y"), lax.axis_index("c")


def _flip(v, f):
    return 1 - v if f else v


def _win_half(ref, h):
    return ref.at[pl.ds(h * (D // 2), D // 2), :]


def _sq_half(ref, h):
    return ref.at[:, pl.ds(h * (D // 2), D // 2)]


def _allgather_weights(win_all, wsq_all):
    def body(win_in, wsq_in, win_out, wsq_out, send_sems, recv_sems):
        del win_in, wsq_in
        x, y, c = _place()
        j = 2 * x + y
        sib = (x, y, 1 - c)
        halves = ((win_out, _win_half), (wsq_out, _sq_half))

        def copy(a, k, part, to):
            return pltpu.make_async_remote_copy(src_ref=part, dst_ref=part, send_sem=send_sems.at[6 * a + k],
                                                recv_sem=recv_sems.at[6 * a + k], device_id=to, device_id_type=MESH)

        started = []
        for a, (out, half) in enumerate(halves):
            for k, (fx, fy) in enumerate(CHIP_FLIPS):
                cp = copy(a, k, half(out.at[j], c), (_flip(x, fx), _flip(y, fy), c))
                cp.start()
                started.append(cp)
        for a, (out, half) in enumerate(halves):
            for k, (fx, fy) in enumerate(CHIP_FLIPS):
                landed = half(out.at[2 * _flip(x, fx) + _flip(y, fy)], c)
                copy(a, k, landed, sib).wait_recv()
                cp = copy(a, 3 + k, landed, sib)
                cp.start()
                started.append(cp)
        for a, (out, half) in enumerate(halves):
            for k, (fx, fy) in enumerate(CHIP_FLIPS):
                copy(a, 3 + k, half(out.at[2 * _flip(x, fx) + _flip(y, fy)], 1 - c), sib).wait_recv()
        for cp in started:
            cp.wait_send()

    return pl.pallas_call(
        body, name="allgather_weights",
        in_specs=[HBM_SPEC, HBM_SPEC], out_specs=[HBM_SPEC, HBM_SPEC], input_output_aliases={0: 0, 1: 1},
        out_shape=[jax.ShapeDtypeStruct((SHARDS, D, SHARD_W), BF16), jax.ShapeDtypeStruct((SHARDS, 3 * SQ_ROWS, D), BF16)],
        scratch_shapes=[pltpu.SemaphoreType.DMA((12,)), pltpu.SemaphoreType.DMA((12,))],
    )(win_all, wsq_all)


def _swap_halves(gwin, gsq):
    def body(gwin_ref, gsq_ref, win_got, sq_got, send_sems, recv_sems):
        x, y, c = _place()
        sib = (x, y, 1 - c)
        pairs = ((gwin_ref.at[:, pl.ds((1 - c) * (D // 2), D // 2), :], win_got),
                 (gsq_ref.at[:, :, pl.ds((1 - c) * (D // 2), D // 2)], sq_got))
        copies = [pltpu.make_async_remote_copy(src_ref=src, dst_ref=dst, send_sem=send_sems.at[a], recv_sem=recv_sems.at[a],
                                               device_id=sib, device_id_type=MESH) for a, (src, dst) in enumerate(pairs)]
        for cp in copies:
            cp.start()
        for cp in copies:
            cp.wait()

    return pl.pallas_call(
        body, name="swap_halves",
        in_specs=[HBM_SPEC, HBM_SPEC], out_specs=[HBM_SPEC, HBM_SPEC],
        out_shape=[jax.ShapeDtypeStruct((SHARDS, D // 2, SHARD_W), F32), jax.ShapeDtypeStruct((3, D, D // 2), F32)],
        scratch_shapes=[pltpu.SemaphoreType.DMA((2,)), pltpu.SemaphoreType.DMA((2,))],
    )(gwin, gsq)


def _add_halves(c_arr, gwin, gsq, win_got, sq_got):
    def body(c_ref, a_ref, b_ref, p_ref, q_ref, so_ref, sq_ref, sob_ref, sqb_ref):
        so = a_ref[...] + b_ref[...]
        sq = p_ref[...] + q_ref[...]
        so_ref[...] = so
        sq_ref[...] = sq
        sob_ref[...] = _bf(so)
        sqb_ref[...] = _bf(sq)

    rows = 128
    steps = (D // 2) // rows
    return pl.pallas_call(
        body, name="add_halves",
        grid_spec=pltpu.PrefetchScalarGridSpec(
            num_scalar_prefetch=1, grid=(SHARDS, steps),
            in_specs=[pl.BlockSpec((None, rows, SHARD_W), lambda j, i, c: (j, c[0] * steps + i, 0)),
                      pl.BlockSpec((None, rows, SHARD_W), lambda j, i, c: (j, i, 0)),
                      pl.BlockSpec((3, SQ_ROWS // steps, D // 2), lambda j, i, c: (0, j * steps + i, c[0])),
                      pl.BlockSpec((3, SQ_ROWS // steps, D // 2), lambda j, i, c: (0, j * steps + i, 0))],
            out_specs=[pl.BlockSpec((None, rows, SHARD_W), lambda j, i, c: (j, i, 0)),
                       pl.BlockSpec((3, SQ_ROWS // steps, D // 2), lambda j, i, c: (0, j * steps + i, 0))] * 2),
        out_shape=[jax.ShapeDtypeStruct((SHARDS, D // 2, SHARD_W), F32), jax.ShapeDtypeStruct((3, D, D // 2), F32),
                   jax.ShapeDtypeStruct((SHARDS, D // 2, SHARD_W), BF16), jax.ShapeDtypeStruct((3, D, D // 2), BF16)],
        compiler_params=_params(("arbitrary", "arbitrary")),
    )(c_arr, gwin, win_got, gsq, sq_got)


def _scatter_chip_sums(swin, ssq):
    def body(swin_ref, ssq_ref, win_got, sq_got, send_sems, recv_sems):
        x, y, c = _place()
        copies = []
        for k, (fx, fy) in enumerate(CHIP_FLIPS):
            px, py = _flip(x, fx), _flip(y, fy)
            jr = 2 * px + py
            for a, (src, dst) in enumerate(((swin_ref.at[jr], win_got.at[k]),
                                            (ssq_ref.at[:, pl.ds(jr * SQ_ROWS, SQ_ROWS), :], sq_got.at[k]))):
                copies.append(pltpu.make_async_remote_copy(src_ref=src, dst_ref=dst, send_sem=send_sems.at[2 * k + a],
                                                           recv_sem=recv_sems.at[2 * k + a], device_id=(px, py, c), device_id_type=MESH))
        for cp in copies:
            cp.start()
        for cp in copies:
            cp.wait()

    return pl.pallas_call(
        body, name="scatter_chip_sums",
        in_specs=[HBM_SPEC, HBM_SPEC], out_specs=[HBM_SPEC, HBM_SPEC],
        out_shape=[jax.ShapeDtypeStruct((3, D // 2, SHARD_W), BF16), jax.ShapeDtypeStruct((3, 3, SQ_ROWS, D // 2), BF16)],
        scratch_shapes=[pltpu.SemaphoreType.DMA((6,)), pltpu.SemaphoreType.DMA((6,))],
    )(swin, ssq)


def _sum_chips(jc_arr, swin, ssq, win_got, sq_got):
    def body(jc_ref, a_ref, b_ref, p_ref, q_ref, so_ref, sq_ref):
        so_ref[...] = ((a_ref[...] + b_ref[0].astype(F32)) + b_ref[1].astype(F32)) + b_ref[2].astype(F32)
        sq_ref[...] = ((p_ref[...] + q_ref[0].astype(F32)) + q_ref[1].astype(F32)) + q_ref[2].astype(F32)

    rows = 128
    steps = (D // 2) // rows
    sq_rows = SQ_ROWS // steps
    return pl.pallas_call(
        body, name="sum_chips",
        grid_spec=pltpu.PrefetchScalarGridSpec(
            num_scalar_prefetch=1, grid=(steps,),
            in_specs=[pl.BlockSpec((None, rows, SHARD_W), lambda i, jc: (jc[0], i, 0)),
                      pl.BlockSpec((3, rows, SHARD_W), lambda i, jc: (0, i, 0)),
                      pl.BlockSpec((3, sq_rows, D // 2), lambda i, jc: (0, jc[0] * steps + i, 0)),
                      pl.BlockSpec((3, 3, sq_rows, D // 2), lambda i, jc: (0, 0, i, 0))],
            out_specs=[pl.BlockSpec((rows, SHARD_W), lambda i, jc: (jc[1] * steps + i, 0)),
                       pl.BlockSpec((3, sq_rows, D // 2), lambda i, jc: (0, i, jc[1]))]),
        out_shape=[jax.ShapeDtypeStruct((D, SHARD_W), F32), jax.ShapeDtypeStruct((3, SQ_ROWS, D), F32)],
        compiler_params=_params(("arbitrary",)),
    )(jc_arr, swin, win_got, ssq, sq_got)


def _join_halves(g_win, g_sq):
    def body(win_in, sq_in, win_out, sq_out, send_sems, recv_sems):
        del win_in, sq_in
        x, y, c = _place()
        sib = (x, y, 1 - c)

        def halves(h):
            return _win_half(win_out, h), sq_out.at[:, :, pl.ds(h * (D // 2), D // 2)]

        def copy(a, part):
            return pltpu.make_async_remote_copy(src_ref=part, dst_ref=part, send_sem=send_sems.at[a], recv_sem=recv_sems.at[a],
                                                device_id=sib, device_id_type=MESH)

        sent = [copy(a, part) for a, part in enumerate(halves(c))]
        for cp in sent:
            cp.start()
        for a, part in enumerate(halves(1 - c)):
            copy(a, part).wait_recv()
        for cp in sent:
            cp.wait_send()

    return pl.pallas_call(
        body, name="join_halves",
        in_specs=[HBM_SPEC, HBM_SPEC], out_specs=[HBM_SPEC, HBM_SPEC], input_output_aliases={0: 0, 1: 1},
        out_shape=[jax.ShapeDtypeStruct((D, SHARD_W), F32), jax.ShapeDtypeStruct((3, SQ_ROWS, D), F32)],
        scratch_shapes=[pltpu.SemaphoreType.DMA((2,)), pltpu.SemaphoreType.DMA((2,))],
    )(g_win, g_sq)


def _allreduce_small(vec):
    def body(vec_ref, out_ref, slots, send_sems, recv_sems):
        x, y, c = _place()
        me = 4 * x + 2 * y + c
        slots[me] = vec_ref[...]
        copies = []
        for k in range(1, 8):
            fx, fy, fc = (k >> 2) & 1, (k >> 1) & 1, k & 1
            copies.append(pltpu.make_async_remote_copy(
                src_ref=vec_ref, dst_ref=slots.at[me], send_sem=send_sems.at[k - 1], recv_sem=recv_sems.at[k - 1],
                device_id=(_flip(x, fx), _flip(y, fy), _flip(c, fc)), device_id_type=MESH))
        for cp in copies:
            cp.start()
        for k in range(1, 8):
            fx, fy, fc = (k >> 2) & 1, (k >> 1) & 1, k & 1
            src = 4 * _flip(x, fx) + 2 * _flip(y, fy) + _flip(c, fc)
            pltpu.make_async_remote_copy(src_ref=vec_ref, dst_ref=slots.at[src], send_sem=send_sems.at[k - 1],
                                         recv_sem=recv_sems.at[k - 1], device_id=(x, y, c), device_id_type=MESH).wait_recv()
        for cp in copies:
            cp.wait_send()
        total = slots[0]
        for s in range(1, 8):
            total = total + slots[s]
        out_ref[...] = total

    return pl.pallas_call(
        body, name="allreduce_small",
        in_specs=[pl.BlockSpec(memory_space=pltpu.VMEM)], out_specs=pl.BlockSpec(memory_space=pltpu.VMEM),
        out_shape=jax.ShapeDtypeStruct((8, D), F32),
        scratch_shapes=[pltpu.VMEM((8, 8, D), F32), pltpu.SemaphoreType.DMA((7,)), pltpu.SemaphoreType.DMA((7,))],
    )(vec)


def _adamw_math(w, g, m, v):
    m = ADAM_B1 * m + (1.0 - ADAM_B1) * g
    v = ADAM_B2 * v + (1.0 - ADAM_B2) * (g * g)
    m_hat = m / (1.0 - ADAM_B1 ** ADAM_STEP)
    v_hat = v / (1.0 - ADAM_B2 ** ADAM_STEP)
    delta = -ADAM_LR * (m_hat / (jnp.sqrt(v_hat) + ADAM_EPS) + ADAM_WD * w)
    return delta, m, v


def _adamw(name, w, g, m, v, rows):
    R, C = w.shape

    def body(w_ref, g_ref, m_ref, v_ref, d_out, m_out, v_out):
        d_out[...], m_out[...], v_out[...] = _adamw_math(w_ref[...], g_ref[...], m_ref[...], v_ref[...])

    spec = pl.BlockSpec((rows, C), lambda i: (i, 0))
    return pl.pallas_call(
        body, name=name, grid=(R // rows,), in_specs=[spec] * 4, out_specs=[spec] * 3,
        out_shape=[jax.ShapeDtypeStruct((R, C), F32)] * 3,
        compiler_params=_params(("parallel",)),
    )(w, g, m, v)


def _small_update(total, lbw, w8, m8, v8):
    def body(t_ref, lbw_ref, w_ref, m_ref, v_ref, g_out, d_out, m_out, v_out):
        lb = 1.0 / (1.0 + jnp.exp(lbw_ref[1:2, :] - lbw_ref[0:1, :]))
        dlb = t_ref[2:3, :] * lb * (1.0 - lb)
        g_out[...] = jnp.zeros_like(g_out)
        g_out[0:1, :] = t_ref[3:4, :]
        g_out[1:2, :] = dlb
        g_out[2:3, :] = -dlb
        g_out[3:4, :] = t_ref[1:2, :]
        g_out[4:5, :] = t_ref[0:1, :]
        g_out[5:6, :] = t_ref[4:5, :]
        d_out[...], m_out[...], v_out[...] = _adamw_math(w_ref[...], g_out[...], m_ref[...], v_ref[...])

    return pl.pallas_call(
        body, name="small_update", out_shape=[jax.ShapeDtypeStruct((8, D), F32)] * 4,
        compiler_params=_params(),
    )(total, lbw, w8, m8, v8)


def _pack8(norm_w, lbw, hnw, fnw, sinks):
    pad = jnp.zeros((1, D - 16), F32)
    return jnp.concatenate([norm_w, lbw, hnw, fnw.reshape(1, D), jnp.concatenate([sinks, pad], axis=1),
                            jnp.zeros((2, D), F32)], axis=0)


def _unpack8(a):
    return a[0:1], a[1:3], a[3:4], a[5:6, 0:16], a[4]


def _local_step(x, tgt, norm_w, lbw, hnw, sinks, fnw, win_bf, wsq_bf):
    proj, xn_bf = _fwd_proj(x, norm_w, win_bf)
    oh, states = _hgrn_fwd(proj, lbw)
    oa = _attn_fwd(proj, sinks)
    dx2, doh, doa, dhg, dtail, lhs, rhs, loss8, vec_mid = _mid(x, tgt, proj, oh, oa, hnw, fnw.reshape(1, D), wsq_bf)
    gsq = _wgrad_square(lhs, rhs)
    dhead, dlb = _hgrn_bwd(proj, lbw, states, doh)
    daq, dak, dav, dsink = _attn_bwd(proj, sinks, oa, doa)
    wt_bf = win_bf.transpose(0, 2, 1).reshape(D_IN, D)
    grad_x, gnw, dproj = _bwd_dx([dhead, dhg, daq, dak, dav, dtail], wt_bf, x, norm_w, dx2)
    gwin = _bwd_win(xn_bf, dproj)
    sink_row = jnp.concatenate([dsink[:, 0].reshape(1, 16), jnp.zeros((1, D - 16), F32)], axis=1)
    vec = jnp.concatenate([vec_mid[0:2], dlb, gnw, sink_row, jnp.zeros((3, D), F32)], axis=0)
    return loss8[0, 0], grad_x, gwin, gsq, vec


def kernel(x, norm_w, w_in, hgrn_lower_bound, hgrn_norm_w, w_branch_hgrn, attn_sinks, w_branch_attn, w_out, final_norm_w, loss_target, m_norm_w, m_w_in, m_hgrn_lower_bound, m_hgrn_norm_w, m_w_branch_hgrn, m_attn_sinks, m_w_branch_attn, m_w_out, m_final_norm_w, v_norm_w, v_w_in, v_hgrn_lower_bound, v_hgrn_norm_w, v_w_branch_hgrn, v_attn_sinks, v_w_branch_attn, v_w_out, v_final_norm_w):
    c_arr = lax.axis_index("c").astype(jnp.int32).reshape(1)
    j_arr = (2 * lax.axis_index("x") + lax.axis_index("y")).astype(jnp.int32).reshape(1)
    jc_arr = jnp.concatenate([j_arr, c_arr])

    win_mine, wsq_mine = _cast_shards(j_arr, w_in[0], w_branch_hgrn[0], w_branch_attn[0], w_out[0])
    win_bf, wsq_all = _allgather_weights(win_mine, wsq_mine)
    wsq_bf = wsq_all.reshape(SHARDS, 3, SQ_ROWS, D).transpose(1, 0, 2, 3).reshape(3, D, D)

    loss_part, grad_x, gwin, gsq, vec = _local_step(
        x[0], loss_target[0], norm_w, hgrn_lower_bound, hgrn_norm_w, attn_sinks, final_norm_w, win_bf, wsq_bf)
    loss = lax.psum(loss_part, ("x", "y", "c"))

    win_got, sq_got = _swap_halves(gwin, gsq)
    swin, ssq, swin_b, ssq_b = _add_halves(c_arr, gwin, gsq, win_got, sq_got)
    win_got2, sq_got2 = _scatter_chip_sums(swin_b, ssq_b)
    g_win, g_sq = _join_halves(*_sum_chips(jc_arr, swin, ssq, win_got2, sq_got2))

    d_win, nm_win, nv_win = _adamw("adamw_w_in", w_in[0], g_win, m_w_in[0], v_w_in[0], 128)
    sq_w = jnp.concatenate([w_branch_hgrn[0], w_branch_attn[0], w_out[0]], axis=0)
    sq_m = jnp.concatenate([m_w_branch_hgrn[0], m_w_branch_attn[0], m_w_out[0]], axis=0)
    sq_v = jnp.concatenate([v_w_branch_hgrn[0], v_w_branch_attn[0], v_w_out[0]], axis=0)
    d_sq, nm_sq, nv_sq = _adamw("adamw_square", sq_w, g_sq.reshape(3 * SQ_ROWS, D), sq_m, sq_v, 256)

    total = _allreduce_small(vec)
    g8, d8, nm8, nv8 = _small_update(
        total, hgrn_lower_bound,
        _pack8(norm_w, hgrn_lower_bound, hgrn_norm_w, final_norm_w, attn_sinks),
        _pack8(m_norm_w, m_hgrn_lower_bound, m_hgrn_norm_w, m_final_norm_w, m_attn_sinks),
        _pack8(v_norm_w, v_hgrn_lower_bound, v_hgrn_norm_w, v_final_norm_w, v_attn_sinks))

    def assemble(win, sq, small):
        nw, lb, hn, sk, fn = _unpack8(small)
        sq = sq.reshape(3, 1, SQ_ROWS, D)
        return (nw, win.reshape(1, D, SHARD_W), lb, hn, sq[0], sk, sq[1], sq[2], fn)

    return (loss, grad_x.reshape(1, -1, D),
            *assemble(g_win, g_sq, g8), *assemble(d_win, d_sq, d8),
            *assemble(nm_win, nm_sq, nm8), *assemble(nv_win, nv_sq, nv8))
```

```python
import functools

import jax
import jax.numpy as jnp
from jax import lax
from jax.experimental import pallas as pl
from jax.experimental.pallas import tpu as pltpu

F32 = jnp.float32
BF16 = jnp.bfloat16

D = 1024
D_IN = 8704
SHARDS = 4
SHARD_W = D_IN // SHARDS
SQ_ROWS = D // SHARDS
HEADS = 8
HEAD_W = 128
CHUNK = 64
SUB = 2
ATT_BLOCK = 128
KV_HEADS = 4
HEAD_DIM = 64
EPS = 1e-6
NEG = -1e30
SCALE = HEAD_DIM ** -0.5
COL_HG, COL_AQ, COL_AK, COL_AV, COL_AG, COL_MH, COL_MA = 3072, 4096, 5120, 5376, 5632, 6656, 7680

ADAM_LR, ADAM_B1, ADAM_B2, ADAM_EPS, ADAM_WD, ADAM_STEP = 0.001, 0.9, 0.999, 1e-08, 0.01, 10

VMEM_LIMIT = 56 * 1024 * 1024
MESH = pl.DeviceIdType.MESH
HBM_SPEC = pl.BlockSpec(memory_space=pltpu.HBM)
CHIP_FLIPS = ((1, 0), (0, 1), (1, 1))


def _dot(a, b):
    return jnp.dot(a, b, preferred_element_type=F32)


def _dot_nt(a, b):
    return lax.dot_general(a, b, (((1,), (1,)), ((), ())), preferred_element_type=F32)


def _dot_tn(a, b):
    return lax.dot_general(a, b, (((0,), (0,)), ((), ())), preferred_element_type=F32)


def _sigmoid(v):
    return 1.0 / (1.0 + jnp.exp(-v))


def _bf(v):
    return v.astype(BF16)


def _split3(v):
    a = _bf(v)
    r = v - a.astype(F32)
    b = _bf(r)
    c = _bf(r - b.astype(F32))
    return a, b, c


def _tri_dot(tri, v):
    a, b, c = _split3(v)
    return _dot(tri, a) + _dot(tri, b) + _dot(tri, c)


def _params(sem=None):
    return pltpu.CompilerParams(dimension_semantics=sem, vmem_limit_bytes=VMEM_LIMIT)


def _cast_shards(j_arr, win_s, wbh_s, wba_s, wout_s):
    steps = 4
    rows = D // steps

    def body(j_ref, win_ref, a_ref, b_ref, c_ref, win_o, sq_o):
        win_o[...] = _bf(win_ref[...])

        @pl.when(pl.program_id(0) == 0)
        def _():
            sq_o[0:SQ_ROWS, :] = _bf(a_ref[...])
            sq_o[SQ_ROWS:2 * SQ_ROWS, :] = _bf(b_ref[...])
            sq_o[2 * SQ_ROWS:3 * SQ_ROWS, :] = _bf(c_ref[...])

    whole = pl.BlockSpec((SQ_ROWS, D), lambda i, j: (0, 0))
    return pl.pallas_call(
        body, name="cast_shards",
        grid_spec=pltpu.PrefetchScalarGridSpec(
            num_scalar_prefetch=1, grid=(steps,),
            in_specs=[pl.BlockSpec((rows, SHARD_W), lambda i, j: (i, 0)), whole, whole, whole],
            out_specs=[pl.BlockSpec((None, rows, SHARD_W), lambda i, j: (j[0], i, 0)),
                       pl.BlockSpec((None, 3 * SQ_ROWS, D), lambda i, j: (j[0], 0, 0))]),
        out_shape=[jax.ShapeDtypeStruct((SHARDS, D, SHARD_W), BF16), jax.ShapeDtypeStruct((SHARDS, 3 * SQ_ROWS, D), BF16)],
        compiler_params=_params(("arbitrary",)),
    )(j_arr, win_s, wbh_s, wba_s, wout_s)


def _fwd_proj(x, norm_w, win_bf):
    T = x.shape[0]
    tm = min(256, T)

    def body(x_ref, nw_ref, w_hbm, proj_ref, xn_ref, w_scr, sem):
        @pl.when(pl.program_id(0) == 0)
        def _():
            cp = pltpu.make_async_copy(w_hbm, w_scr, sem)
            cp.start()
            cp.wait()

        xf = x_ref[...]
        rs = lax.rsqrt(jnp.mean(xf * xf, axis=1, keepdims=True) + EPS)
        xn = _bf((xf * rs) * nw_ref[...])
        xn_ref[...] = xn
        for j in range(SHARDS):
            proj_ref[:, j * SHARD_W:(j + 1) * SHARD_W] = _dot(xn, w_scr[j])

    return pl.pallas_call(
        body, name="fwd_proj", grid=(T // tm,),
        in_specs=[pl.BlockSpec((tm, D), lambda i: (i, 0)), pl.BlockSpec((1, D), lambda i: (0, 0)), HBM_SPEC],
        out_specs=[pl.BlockSpec((tm, D_IN), lambda i: (i, 0)), pl.BlockSpec((tm, D), lambda i: (i, 0))],
        out_shape=[jax.ShapeDtypeStruct((T, D_IN), F32), jax.ShapeDtypeStruct((T, D), BF16)],
        scratch_shapes=[pltpu.VMEM((SHARDS, D, SHARD_W), BF16), pltpu.SemaphoreType.DMA],
        compiler_params=_params(("arbitrary",)),
    )(x, norm_w, win_bf)


def _hgrn_gates(hq_ref, hf_ref, lbw_ref, b_scr):
    lb = 1.0 / (1.0 + jnp.exp(lbw_ref[1:2, :] - lbw_ref[0:1, :]))
    hf = hf_ref[...]
    sig = _sigmoid(hf)
    f = lb + (1.0 - lb) * sig
    g = jnp.log(f)
    hq = hq_ref[...]
    sq = _sigmoid(hq)
    q = hq * sq
    row = lax.broadcasted_iota(jnp.int32, (CHUNK, CHUNK), 0)
    col = lax.broadcasted_iota(jnp.int32, (CHUNK, CHUNK), 1)
    causal = row >= col
    b = _tri_dot(jnp.where(causal, 1.0, 0.0).astype(BF16), g)
    b_scr[...] = b
    bc = b_scr[CHUNK - 1:CHUNK, :]
    r = b_scr[CHUNK // 2 - 1:CHUNK // 2, :]
    return dict(lb=lb, sig=sig, f=f, k=1.0 - f, hq=hq, sq=sq, q=q, b=b, bc=bc, r=r, causal=causal)


def _hgrn_fwd(proj, lbw, wsq_all):
    T = proj.shape[0]
    n = T // CHUNK

    def body(hq_ref, hf_ref, hi_ref, lbw_ref, wsq_in, o_ref, st_ref, wsq_out, s_scr, b_scr, send_sems, recv_sems):
        del wsq_in

        @pl.when(pl.program_id(0) == 0)
        def _():
            _gather_start(wsq_out, _sq_half, send_sems, recv_sems)
            s_scr[...] = jnp.zeros_like(s_scr)

        for c in range(SUB):
            rows = pl.ds(c * CHUNK, CHUNK)
            gt = _hgrn_gates(hq_ref.at[rows, :], hf_ref.at[rows, :], lbw_ref, b_scr.at[rows, :])
            b, bc, r, q, k = gt["b"], gt["bc"], gt["r"], gt["q"], gt["k"]
            qe = _bf(q * jnp.exp(b))
            qr = _bf(q * jnp.exp(b - r))
            kr = _bf(k * jnp.exp(r - b))
            kl = _bf(k * jnp.exp(bc - b))
            ebc = jnp.exp(bc)
            v = _bf(hi_ref[rows, :])
            for h in range(HEADS):
                sl = slice(h * HEAD_W, (h + 1) * HEAD_W)
                st = s_scr[h]
                st_ref[c, h] = st
                a = jnp.where(gt["causal"], _dot_nt(qr[:, sl], kr[:, sl]), 0.0)
                o_ref[rows, sl] = _dot(_bf(a), v[:, sl]) + _dot_nt(qe[:, sl], _bf(st))
                s_scr[h] = ebc[:, sl] * st + _dot_tn(v[:, sl], kl[:, sl])

        @pl.when(pl.program_id(0) == n // SUB - 1)
        def _():
            _gather_finish(wsq_out, _sq_half, send_sems, recv_sems)

    col = lambda j: pl.BlockSpec((SUB * CHUNK, D), lambda i: (i, j))
    return pl.pallas_call(
        body, name="hgrn_fwd", grid=(n // SUB,),
        in_specs=[col(0), col(1), col(2), pl.BlockSpec((2, D), lambda i: (0, 0)), HBM_SPEC],
        out_specs=[pl.BlockSpec((SUB * CHUNK, D), lambda i: (i, 0)),
                   pl.BlockSpec((SUB, HEADS, HEAD_W, HEAD_W), lambda i: (i, 0, 0, 0)), HBM_SPEC],
        out_shape=[jax.ShapeDtypeStruct((T, D), F32), jax.ShapeDtypeStruct((n, HEADS, HEAD_W, HEAD_W), F32),
                   jax.ShapeDtypeStruct((SHARDS, 3 * SQ_ROWS, D), BF16)],
        input_output_aliases={4: 2},
        scratch_shapes=[pltpu.VMEM((HEADS, HEAD_W, HEAD_W), F32), pltpu.VMEM((SUB * CHUNK, D), F32),
                        pltpu.SemaphoreType.DMA((6,)), pltpu.SemaphoreType.DMA((6,))],
        compiler_params=_params(("arbitrary",)),
    )(proj, proj, proj, lbw, wsq_all)


def _hgrn_bwd(proj, lbw, states, do):
    T = proj.shape[0]
    n = T // CHUNK

    def body(hq_ref, hf_ref, hi_ref, lbw_ref, st_ref, do_ref, dp_ref, dlb_ref,
             ds_scr, b_scr, dq_scr, dk_scr, dv_scr, late_scr, early_scr, ex_scr):
        @pl.when(pl.program_id(0) == 0)
        def _():
            ds_scr[...] = jnp.zeros_like(ds_scr)
            dlb_ref[...] = jnp.zeros_like(dlb_ref)

        for c in reversed(range(SUB)):
            rows = pl.ds(c * CHUNK, CHUNK)
            gt = _hgrn_gates(hq_ref.at[rows, :], hf_ref.at[rows, :], lbw_ref, b_scr.at[rows, :])
            b, bc, r, q, k = gt["b"], gt["bc"], gt["r"], gt["q"], gt["k"]
            eb = jnp.exp(b)
            er = jnp.exp(b - r)
            erk = jnp.exp(r - b)
            el = jnp.exp(bc - b)
            ebc = jnp.exp(bc)
            qe, qr, kr, kl = _bf(q * eb), _bf(q * er), _bf(k * erk), _bf(k * el)
            v = _bf(hi_ref[rows, :])
            do_b = _bf(do_ref[rows, :])
            for h in range(HEADS):
                sl = slice(h * HEAD_W, (h + 1) * HEAD_W)
                st0 = st_ref[c, h]
                dst = ds_scr[h]
                dst_b = _bf(dst)
                a = _bf(jnp.where(gt["causal"], _dot_nt(qr[:, sl], kr[:, sl]), 0.0))
                da = _bf(jnp.where(gt["causal"], _dot_nt(do_b[:, sl], v[:, sl]), 0.0))
                mq = _dot(da, kr[:, sl])
                mk = _dot_tn(da, qr[:, sl])
                dq_in = eb[:, sl] * _dot(do_b[:, sl], _bf(st0))
                dk_in = el[:, sl] * _dot(v[:, sl], dst_b)
                dq_scr[rows, sl] = er[:, sl] * mq + dq_in
                dk_scr[rows, sl] = erk[:, sl] * mk + dk_in
                dv_scr[rows, sl] = _dot_tn(a, do_b[:, sl]) + _dot_nt(kl[:, sl], dst_b)
                late_scr[rows, sl] = q[:, sl] * dq_in + qr[:, sl].astype(F32) * mq - kr[:, sl].astype(F32) * mk
                early_scr[rows, sl] = k[:, sl] * dk_in
                ex_scr[:, sl] = jnp.sum(dst * st0, axis=0, keepdims=True)
                ds_scr[h] = ebc[:, sl] * dst + _dot_tn(do_b[:, sl], qe[:, sl])

            dq, dk = dq_scr[rows, :], dk_scr[rows, :]
            row = lax.broadcasted_iota(jnp.int32, (CHUNK, CHUNK), 0)
            col = lax.broadcasted_iota(jnp.int32, (CHUNK, CHUNK), 1)
            at_or_after = jnp.where(col >= row, 1.0, 0.0).astype(BF16)
            before = jnp.where(col < row, 1.0, 0.0).astype(BF16)
            dg = _tri_dot(at_or_after, late_scr[rows, :]) + _tri_dot(before, early_scr[rows, :]) + ebc * ex_scr[...]
            df = dg / gt["f"] - dk
            sig, sq, hq, lb = gt["sig"], gt["sq"], gt["hq"], gt["lb"]
            dp_ref[rows, 0:D] = _bf(dq * (sq * (1.0 + hq * (1.0 - sq))))
            dp_ref[rows, D:2 * D] = _bf(df * (1.0 - lb) * sig * (1.0 - sig))
            dp_ref[rows, 2 * D:3 * D] = _bf(dv_scr[rows, :])
            dlb_ref[...] += jnp.sum(df * (1.0 - sig), axis=0, keepdims=True)

    ns = n // SUB
    col = lambda j: pl.BlockSpec((SUB * CHUNK, D), lambda i: (ns - 1 - i, j))
    return pl.pallas_call(
        body, name="hgrn_bwd", grid=(ns,),
        in_specs=[col(0), col(1), col(2), pl.BlockSpec((2, D), lambda i: (0, 0)),
                  pl.BlockSpec((SUB, HEADS, HEAD_W, HEAD_W), lambda i: (ns - 1 - i, 0, 0, 0)),
                  pl.BlockSpec((SUB * CHUNK, D), lambda i: (ns - 1 - i, 0))],
        out_specs=[pl.BlockSpec((SUB * CHUNK, 3 * D), lambda i: (ns - 1 - i, 0)),
                   pl.BlockSpec((1, D), lambda i: (0, 0))],
        out_shape=[jax.ShapeDtypeStruct((T, 3 * D), BF16), jax.ShapeDtypeStruct((1, D), F32)],
        scratch_shapes=[pltpu.VMEM((HEADS, HEAD_W, HEAD_W), F32)] + [pltpu.VMEM((SUB * CHUNK, D), F32)] * 6
                       + [pltpu.VMEM((1, D), F32)],
        compiler_params=_params(("arbitrary",)),
    )(proj, proj, proj, lbw, states, do)


def _attn_masks(blk):
    qi = lax.broadcasted_iota(jnp.int32, (ATT_BLOCK, 2 * ATT_BLOCK), 0)
    kj = lax.broadcasted_iota(jnp.int32, (ATT_BLOCK, 2 * ATT_BLOCK), 1)
    band = (kj > qi) & (kj <= qi + ATT_BLOCK)
    return band & ((blk > 0) | (kj >= ATT_BLOCK))


def _head_pair_operand(t, hp, low):
    mine = low if hp == 0 else jnp.logical_not(low)
    both = jnp.where(mine, t, pltpu.roll(t, HEAD_DIM, 1))
    return _bf(jnp.concatenate([jnp.where(low, both, 0.0), jnp.where(low, 0.0, both)], axis=0))


def _attn_probs(s, sink, valid):
    s = jnp.where(valid, s * SCALE, NEG)
    m = jnp.maximum(jnp.max(s, axis=1, keepdims=True), sink)
    p = jnp.exp(s - m)
    es = jnp.exp(sink - m)
    inv = 1.0 / (jnp.sum(p, axis=1, keepdims=True) + es)
    return p * inv, es * inv


def _attn_fwd(proj, sinks):
    T = proj.shape[0]
    nb = T // ATT_BLOCK
    W2 = 2 * ATT_BLOCK

    def body(sink_ref, q_ref, kp_ref, kc_ref, vp_ref, vc_ref, o_ref):
        blk = pl.program_id(0)
        valid = _attn_masks(blk)
        low = lax.broadcasted_iota(jnp.int32, (1, 2 * HEAD_DIM), 1) < HEAD_DIM
        kcat = jnp.concatenate([kp_ref[...], kc_ref[...]], axis=0)
        vcat = jnp.concatenate([vp_ref[...], vc_ref[...]], axis=0)
        for h in range(KV_HEADS):
            tl = slice((h // 2) * 128, (h // 2) * 128 + 128)
            mine = low if h % 2 == 0 else jnp.logical_not(low)
            kh = _bf(jnp.where(mine, kcat[:, tl], pltpu.roll(kcat[:, tl], HEAD_DIM, 1)))
            vh = _bf(jnp.where(mine, vcat[:, tl], pltpu.roll(vcat[:, tl], HEAD_DIM, 1)))
            for t in range(2):
                ql = slice((2 * h + t) * 128, (2 * h + t) * 128 + 128)
                q2 = q_ref[:, ql]
                outs = []
                for p in range(2):
                    qm = _bf(jnp.where(low if p == 0 else jnp.logical_not(low), q2, 0.0))
                    probs, _ = _attn_probs(_dot_nt(qm, kh), sink_ref[0, 4 * h + 2 * t + p], valid)
                    outs.append(_dot(_bf(probs), vh))
                o_ref[:, ql] = jnp.where(low, outs[0], outs[1])

    prev = lambda i: jnp.maximum(i - 1, 0)
    return pl.pallas_call(
        body, name="attn_fwd", grid=(nb,),
        in_specs=[pl.BlockSpec(memory_space=pltpu.SMEM),
                  pl.BlockSpec((ATT_BLOCK, D), lambda i: (i, COL_AQ // D)),
                  pl.BlockSpec((ATT_BLOCK, 256), lambda i: (prev(i), COL_AK // 256)),
                  pl.BlockSpec((ATT_BLOCK, 256), lambda i: (i, COL_AK // 256)),
                  pl.BlockSpec((ATT_BLOCK, 256), lambda i: (prev(i), COL_AV // 256)),
                  pl.BlockSpec((ATT_BLOCK, 256), lambda i: (i, COL_AV // 256))],
        out_specs=pl.BlockSpec((ATT_BLOCK, D), lambda i: (i, 0)),
        out_shape=jax.ShapeDtypeStruct((T, D), F32),
        compiler_params=_params(("arbitrary",)),
    )(sinks, proj, proj, proj, proj, proj)


def _attn_bwd(proj, sinks, o, do):
    T = proj.shape[0]
    nb = T // ATT_BLOCK
    W2 = 2 * ATT_BLOCK

    def body(sink_ref, q_ref, kp_ref, kc_ref, vp_ref, vc_ref, o_ref, do_ref,
             dq_ref, dk_ref, dv_ref, dsink_ref, ck_scr, cv_scr, nk_scr, nv_scr):
        blk = pl.program_id(0)

        @pl.when(blk == 0)
        def _():
            ck_scr[...] = jnp.zeros_like(ck_scr)
            cv_scr[...] = jnp.zeros_like(cv_scr)
            dsink_ref[...] = jnp.zeros_like(dsink_ref)

        @pl.when(blk < nb)
        def _():
            valid = _attn_masks(blk)
            low = lax.broadcasted_iota(jnp.int32, (1, 2 * HEAD_DIM), 1) < HEAD_DIM
            kcat = jnp.concatenate([kp_ref[...], kc_ref[...]], axis=0)
            vcat = jnp.concatenate([vp_ref[...], vc_ref[...]], axis=0)
            for h in range(KV_HEADS):
                tl = slice((h // 2) * 128, (h // 2) * 128 + 128)
                kbd = _head_pair_operand(kcat[:, tl], h % 2, low)
                vbd = _head_pair_operand(vcat[:, tl], h % 2, low)
                dkbd = jnp.zeros((2 * W2, 128), F32)
                dvbd = jnp.zeros((2 * W2, 128), F32)
                for t in range(2):
                    ql = slice((2 * h + t) * 128, (2 * h + t) * 128 + 128)
                    head = 4 * h + 2 * t
                    q2 = _bf(q_ref[:, ql])
                    do2 = do_ref[:, ql]
                    do2_b = _bf(do2)
                    doo = do2 * o_ref[:, ql]
                    dsum0 = jnp.sum(jnp.where(low, doo, 0.0), axis=1, keepdims=True)
                    dsum1 = jnp.sum(jnp.where(low, 0.0, doo), axis=1, keepdims=True)
                    s2 = _dot_nt(q2, kbd)
                    p0, ps0 = _attn_probs(s2[:, 0:W2], sink_ref[0, head], valid)
                    p1, ps1 = _attn_probs(s2[:, W2:2 * W2], sink_ref[0, head + 1], valid)
                    dp2 = _dot_nt(do2_b, vbd)
                    ds2 = _bf(jnp.concatenate([p0 * (dp2[:, 0:W2] - dsum0), p1 * (dp2[:, W2:2 * W2] - dsum1)], axis=1) * SCALE)
                    dq_ref[:, ql] = _bf(_dot(ds2, kbd))
                    dkbd = dkbd + _dot_tn(ds2, q2)
                    dvbd = dvbd + _dot_tn(_bf(jnp.concatenate([p0, p1], axis=1)), do2_b)
                    dsink_ref[head:head + 1, :] += jnp.zeros((1, 128), F32) - jnp.sum(ps0 * dsum0, axis=0, keepdims=True)
                    dsink_ref[head + 1:head + 2, :] += jnp.zeros((1, 128), F32) - jnp.sum(ps1 * dsum1, axis=0, keepdims=True)
                dk2 = jnp.where(low, dkbd[0:W2], dkbd[W2:2 * W2])
                dv2 = jnp.where(low, dvbd[0:W2], dvbd[W2:2 * W2])
                dk2 = dk2 + pltpu.roll(dk2, HEAD_DIM, 1)
                dv2 = dv2 + pltpu.roll(dv2, HEAD_DIM, 1)
                if h % 2 == 0:
                    keep_k, keep_v = dk2, dv2
                else:
                    nk_scr[:, tl] = jnp.where(low, keep_k, dk2)
                    nv_scr[:, tl] = jnp.where(low, keep_v, dv2)
            dk_ref[...] = _bf(ck_scr[...] + nk_scr[0:ATT_BLOCK, :])
            dv_ref[...] = _bf(cv_scr[...] + nv_scr[0:ATT_BLOCK, :])
            ck_scr[...] = nk_scr[ATT_BLOCK:2 * ATT_BLOCK, :]
            cv_scr[...] = nv_scr[ATT_BLOCK:2 * ATT_BLOCK, :]

        @pl.when(blk == nb)
        def _():
            dk_ref[...] = _bf(ck_scr[...])
            dv_ref[...] = _bf(cv_scr[...])

    cur = lambda i: jnp.minimum(i, nb - 1)
    prev = lambda i: jnp.maximum(cur(i) - 1, 0)
    late = lambda i: jnp.maximum(i - 1, 0)
    dq, dk, dv, dsink = pl.pallas_call(
        body, name="attn_bwd", grid=(nb + 1,),
        in_specs=[pl.BlockSpec(memory_space=pltpu.SMEM),
                  pl.BlockSpec((ATT_BLOCK, D), lambda i: (cur(i), COL_AQ // D)),
                  pl.BlockSpec((ATT_BLOCK, 256), lambda i: (prev(i), COL_AK // 256)),
                  pl.BlockSpec((ATT_BLOCK, 256), lambda i: (cur(i), COL_AK // 256)),
                  pl.BlockSpec((ATT_BLOCK, 256), lambda i: (prev(i), COL_AV // 256)),
                  pl.BlockSpec((ATT_BLOCK, 256), lambda i: (cur(i), COL_AV // 256)),
                  pl.BlockSpec((ATT_BLOCK, D), lambda i: (cur(i), 0)),
                  pl.BlockSpec((ATT_BLOCK, D), lambda i: (cur(i), 0))],
        out_specs=[pl.BlockSpec((ATT_BLOCK, D), lambda i: (cur(i), 0)),
                   pl.BlockSpec((ATT_BLOCK, 256), lambda i: (late(i), 0)),
                   pl.BlockSpec((ATT_BLOCK, 256), lambda i: (late(i), 0)),
                   pl.BlockSpec((16, 128), lambda i: (0, 0))],
        out_shape=[jax.ShapeDtypeStruct((T, D), BF16), jax.ShapeDtypeStruct((T, 256), BF16),
                   jax.ShapeDtypeStruct((T, 256), BF16), jax.ShapeDtypeStruct((16, 128), F32)],
        scratch_shapes=[pltpu.VMEM((ATT_BLOCK, 256), F32), pltpu.VMEM((ATT_BLOCK, 256), F32),
                        pltpu.VMEM((2 * ATT_BLOCK, 256), F32), pltpu.VMEM((2 * ATT_BLOCK, 256), F32)],
        compiler_params=_params(("arbitrary",)),
    )(sinks, proj, proj, proj, proj, proj, o, do)
    return dq, dk, dv, dsink


def _mid(x, tgt, proj, oh, oa, hnw, fnw, wsq_bf):
    T = x.shape[0]
    tm = min(256, T)
    nt = T // tm

    def body(x_ref, tgt_ref, oh_ref, oa_ref, hg_ref, ag0_ref, ag1_ref, mh0_ref, mh1_ref, ma0_ref, ma1_ref,
             hnw_ref, fnw_ref, w_hbm,
             dx2_ref, doh_ref, doa_ref, dhg_ref, dtail_ref, lhs_ref, rhs_ref, loss_ref, vec_ref,
             w_scr, xh_scr, rs_scr, sem):
        @pl.when(pl.program_id(0) == 0)
        def _():
            cp = pltpu.make_async_copy(w_hbm, w_scr, sem)
            cp.start()
            cp.wait()
            loss_ref[...] = jnp.zeros_like(loss_ref)
            vec_ref[...] = jnp.zeros_like(vec_ref)

        oh = oh_ref[...]
        for h in range(HEADS):
            sl = slice(h * HEAD_W, (h + 1) * HEAD_W)
            ohh = oh[:, sl]
            rs = lax.rsqrt(jnp.mean(ohh * ohh, axis=1, keepdims=True) + EPS)
            xh_scr[:, sl] = ohh * rs
            rs_scr[:, sl] = jnp.broadcast_to(rs, (tm, HEAD_W))
        xh = xh_scr[...]
        hnw = hnw_ref[...]
        on = xh * hnw
        hg = hg_ref[...]
        sg = _sigmoid(hg)
        silu_g = hg * sg
        gated_h = _bf(on * silu_g)
        lhs_ref[0] = gated_h
        yh = _dot(gated_h, w_scr[0])
        oa = oa_ref[...]
        ag = jnp.concatenate([ag0_ref[...], ag1_ref[...]], axis=1)
        sa = _sigmoid(ag)
        silu_a = ag * sa
        gated_a = _bf(oa * silu_a)
        lhs_ref[1] = gated_a
        ya = _dot(gated_a, w_scr[1])
        smh = _sigmoid(jnp.concatenate([mh0_ref[...], mh1_ref[...]], axis=1))
        sma = _sigmoid(jnp.concatenate([ma0_ref[...], ma1_ref[...]], axis=1))
        merged = _bf(smh * yh + sma * ya)
        lhs_ref[2] = merged
        x2 = x_ref[...] + _dot(merged, w_scr[2])
        rs2 = lax.rsqrt(jnp.mean(x2 * x2, axis=1, keepdims=True) + EPS)
        xh2 = x2 * rs2
        fnw = fnw_ref[...]
        diff = xh2 * fnw - tgt_ref[...]
        loss_ref[...] += jnp.zeros_like(loss_ref) + jnp.sum(diff * diff) * (0.5 / D)

        dy = diff * (1.0 / D)
        vec_ref[0:1, :] += jnp.sum(dy * xh2, axis=0, keepdims=True)
        gy = dy * fnw
        dx2 = rs2 * (gy - xh2 * jnp.mean(gy * xh2, axis=1, keepdims=True))
        dx2_ref[...] = dx2
        dx2_b = _bf(dx2)
        rhs_ref[2] = dx2_b
        dmerged = _dot_nt(dx2_b, w_scr[2])
        dyh = dmerged * smh
        dya = dmerged * sma
        dtail_ref[:, D:2 * D] = _bf(dyh * yh * (1.0 - smh))
        dtail_ref[:, 2 * D:3 * D] = _bf(dya * ya * (1.0 - sma))
        dyh_b, dya_b = _bf(dyh), _bf(dya)
        rhs_ref[0] = dyh_b
        rhs_ref[1] = dya_b
        dgh = _dot_nt(dyh_b, w_scr[0])
        dga = _dot_nt(dya_b, w_scr[1])
        don = dgh * silu_g
        dhg_ref[...] = _bf(dgh * on * (sg * (1.0 + hg * (1.0 - sg))))
        vec_ref[1:2, :] += jnp.sum(don * xh, axis=0, keepdims=True)
        gxh = don * hnw
        rsb = rs_scr[...]
        for h in range(HEADS):
            sl = slice(h * HEAD_W, (h + 1) * HEAD_W)
            gh, xhh = gxh[:, sl], xh[:, sl]
            doh_ref[:, sl] = rsb[:, sl] * (gh - xhh * jnp.mean(gh * xhh, axis=1, keepdims=True))
        doa_ref[...] = dga * silu_a
        dtail_ref[:, 0:D] = _bf(dga * oa * (sa * (1.0 + ag * (1.0 - sa))))

    row = lambda w, j: pl.BlockSpec((tm, w), lambda i: (i, j))
    const = lambda r, c: pl.BlockSpec((r, c), lambda i: (0, 0))
    stack = pl.BlockSpec((3, tm, D), lambda i: (0, i, 0))
    return pl.pallas_call(
        body, name="mid", grid=(nt,),
        in_specs=[row(D, 0), row(D, 0), row(D, 0), row(D, 0), row(D, COL_HG // D),
                  row(512, COL_AG // 512), row(512, COL_AG // 512 + 1),
                  row(512, COL_MH // 512), row(512, COL_MH // 512 + 1),
                  row(512, COL_MA // 512), row(512, COL_MA // 512 + 1),
                  const(1, D), const(1, D), HBM_SPEC],
        out_specs=[row(D, 0), row(D, 0), row(D, 0), row(D, 0), row(3 * D, 0), stack, stack, const(8, 128), const(8, D)],
        out_shape=[jax.ShapeDtypeStruct((T, D), F32), jax.ShapeDtypeStruct((T, D), F32), jax.ShapeDtypeStruct((T, D), F32),
                   jax.ShapeDtypeStruct((T, D), BF16), jax.ShapeDtypeStruct((T, 3 * D), BF16),
                   jax.ShapeDtypeStruct((3, T, D), BF16), jax.ShapeDtypeStruct((3, T, D), BF16),
                   jax.ShapeDtypeStruct((8, 128), F32), jax.ShapeDtypeStruct((8, D), F32)],
        scratch_shapes=[pltpu.VMEM((3, D, D), BF16), pltpu.VMEM((tm, D), F32), pltpu.VMEM((tm, D), F32),
                        pltpu.SemaphoreType.DMA],
        compiler_params=_params(("arbitrary",)),
    )(x, tgt, oh, oa, proj, proj, proj, proj, proj, proj, proj, hnw, fnw, wsq_bf)


def _wgrad_square(lhs, rhs):
    T = lhs.shape[1]
    tk = min(1024, T)

    def body(a_ref, b_ref, g_ref):
        part = _dot_tn(a_ref[...], b_ref[...])

        @pl.when(pl.program_id(1) == 0)
        def _():
            g_ref[...] = part

        @pl.when(pl.program_id(1) > 0)
        def _():
            g_ref[...] += part

    spec = pl.BlockSpec((None, tk, D), lambda k, i: (k, i, 0))
    return pl.pallas_call(
        body, name="wgrad_square", grid=(3, T // tk), in_specs=[spec, spec],
        out_specs=pl.BlockSpec((None, D, D), lambda k, i: (k, 0, 0)),
        out_shape=jax.ShapeDtypeStruct((3, D, D), F32),
        compiler_params=_params(("parallel", "arbitrary")),
    )(lhs, rhs)


def _bwd_dx(pieces, wt_bf, x, norm_w, dx2, swin_b, ssq_b):
    T = x.shape[0]
    tm = min(256, T)
    nt = T // tm
    widths = [p.shape[1] for p in pieces]
    n_p = len(pieces)

    def body(*refs):
        piece_refs = refs[:n_p]
        (w_hbm, x_ref, nw_ref, dx2_ref, swin_ref, ssq_ref,
         gx_ref, gnw_ref, win_got, sq_got, w_scr, sem, send_sems, recv_sems) = refs[n_p:]

        def scatter_copies():
            x_, y_, c_ = _place()
            copies = []
            for k, (fx, fy) in enumerate(CHIP_FLIPS):
                px, py = _flip(x_, fx), _flip(y_, fy)
                jr = 2 * px + py
                for a, (src, dst) in enumerate(((swin_ref.at[:, pl.ds(jr * SHARD_W, SHARD_W)], win_got.at[k]),
                                                (ssq_ref.at[:, pl.ds(jr * SQ_ROWS, SQ_ROWS), :], sq_got.at[k]))):
                    copies.append(pltpu.make_async_remote_copy(
                        src_ref=src, dst_ref=dst, send_sem=send_sems.at[2 * k + a], recv_sem=recv_sems.at[2 * k + a],
                        device_id=(px, py, c_), device_id_type=MESH))
            return copies

        @pl.when(pl.program_id(0) == 0)
        def _():
            for cp in scatter_copies():
                cp.start()
            cp = pltpu.make_async_copy(w_hbm, w_scr, sem)
            cp.start()
            cp.wait()
            gnw_ref[...] = jnp.zeros_like(gnw_ref)

        dxn = None
        off = 0
        for ref, w in zip(piece_refs, widths):
            part = _dot(ref[...], w_scr[off:off + w, :])
            dxn = part if dxn is None else dxn + part
            off += w
        xf = x_ref[...]
        rs = lax.rsqrt(jnp.mean(xf * xf, axis=1, keepdims=True) + EPS)
        xh = xf * rs
        gnw_ref[...] += jnp.sum(dxn * xh, axis=0, keepdims=True)
        gx = dxn * nw_ref[...]
        gx_ref[...] = rs * (gx - xh * jnp.mean(gx * xh, axis=1, keepdims=True)) + dx2_ref[...]

        @pl.when(pl.program_id(0) == nt - 1)
        def _():
            for cp in scatter_copies():
                cp.wait()

    row = lambda w: pl.BlockSpec((tm, w), lambda i: (i, 0))
    return pl.pallas_call(
        body, name="bwd_dx", grid=(nt,),
        in_specs=[row(w) for w in widths] + [HBM_SPEC, row(D), pl.BlockSpec((1, D), lambda i: (0, 0)), row(D), HBM_SPEC, HBM_SPEC],
        out_specs=[row(D), pl.BlockSpec((1, D), lambda i: (0, 0)), HBM_SPEC, HBM_SPEC],
        out_shape=[jax.ShapeDtypeStruct((T, D), F32), jax.ShapeDtypeStruct((1, D), F32),
                   jax.ShapeDtypeStruct((3, D // 2, SHARD_W), BF16), jax.ShapeDtypeStruct((3, 3, SQ_ROWS, D // 2), BF16)],
        scratch_shapes=[pltpu.VMEM((D_IN, D), BF16), pltpu.SemaphoreType.DMA,
                        pltpu.SemaphoreType.DMA((6,)), pltpu.SemaphoreType.DMA((6,))],
        compiler_params=_params(("arbitrary",)),
    )(*pieces, wt_bf, x, norm_w, dx2, swin_b, ssq_b)


W_PIECES = ((0, 1024, 3), (COL_HG, 1024, 1), (COL_AQ, 1024, 1), (COL_AK, 256, 1), (COL_AV, 256, 1), (COL_AG, 512, 6))


def _wgrad_in(xn_bf, pieces):
    T = xn_bf.shape[0]
    tk = min(1024, T)
    buf = None
    for n, (piece, (col, wb, blocks)) in enumerate(zip(pieces, W_PIECES)):
        first = buf is None

        def body(xn_ref, p_ref, *rest):
            g_ref = rest[-1]
            part = _dot_tn(xn_ref[...], p_ref[...])

            @pl.when(pl.program_id(1) == 0)
            def _():
                g_ref[...] = part

            @pl.when(pl.program_id(1) > 0)
            def _():
                g_ref[...] += part

        call = pl.pallas_call(
            body, name=f"wgrad_in_{n}", grid=(blocks, T // tk),
            in_specs=[pl.BlockSpec((tk, D), lambda jb, i: (i, 0)), pl.BlockSpec((tk, wb), lambda jb, i: (i, jb))]
                     + ([] if first else [HBM_SPEC]),
            out_specs=pl.BlockSpec((D, wb), lambda jb, i, base=col // wb: (0, base + jb)),
            out_shape=jax.ShapeDtypeStruct((D, D_IN), F32),
            input_output_aliases={} if first else {2: 0},
            compiler_params=_params(("parallel", "arbitrary")),
        )
        buf = call(xn_bf, piece) if first else call(xn_bf, piece, buf)
    return buf


def _place():
    return lax.axis_index("x"), lax.axis_index("y"), lax.axis_index("c")


def _flip(v, f):
    return 1 - v if f else v


def _win_half(ref, h):
    return ref.at[pl.ds(h * (D // 2), D // 2), :]


def _sq_half(ref, h):
    return ref.at[:, pl.ds(h * (D // 2), D // 2)]


def _gather_copy(part, k, to, send_sems, recv_sems):
    return pltpu.make_async_remote_copy(src_ref=part, dst_ref=part, send_sem=send_sems.at[k], recv_sem=recv_sems.at[k],
                                        device_id=to, device_id_type=MESH)


def _gather_start(out, half, send_sems, recv_sems):
    x, y, c = _place()
    for k, (fx, fy) in enumerate(CHIP_FLIPS):
        _gather_copy(half(out.at[2 * x + y], c), k, (_flip(x, fx), _flip(y, fy), c), send_sems, recv_sems).start()


def _gather_finish(out, half, send_sems, recv_sems):
    x, y, c = _place()
    sib = (x, y, 1 - c)
    slots = [2 * _flip(x, fx) + _flip(y, fy) for fx, fy in CHIP_FLIPS]
    passed = []
    for k, jr in enumerate(slots):
        landed = half(out.at[jr], c)
        _gather_copy(landed, k, sib, send_sems, recv_sems).wait_recv()
        cp = _gather_copy(landed, 3 + k, sib, send_sems, recv_sems)
        cp.start()
        passed.append(cp)
    for k, jr in enumerate(slots):
        _gather_copy(half(out.at[jr], 1 - c), 3 + k, sib, send_sems, recv_sems).wait_recv()
    for k, (fx, fy) in enumerate(CHIP_FLIPS):
        _gather_copy(half(out.at[2 * x + y], c), k, (_flip(x, fx), _flip(y, fy), c), send_sems, recv_sems).wait_send()
    for cp in passed:
        cp.wait_send()


def _allgather_w_in(win_all):
    def body(win_in, win_out, send_sems, recv_sems):
        del win_in
        _gather_start(win_out, _win_half, send_sems, recv_sems)
        _gather_finish(win_out, _win_half, send_sems, recv_sems)

    return pl.pallas_call(
        body, name="allgather_w_in", in_specs=[HBM_SPEC], out_specs=HBM_SPEC, input_output_aliases={0: 0},
        out_shape=jax.ShapeDtypeStruct((SHARDS, D, SHARD_W), BF16),
        scratch_shapes=[pltpu.SemaphoreType.DMA((6,)), pltpu.SemaphoreType.DMA((6,))],
    )(win_all)


def _swap_halves(gwin, gsq):
    def body(gwin_ref, gsq_ref, win_got, sq_got, send_sems, recv_sems):
        x, y, c = _place()
        sib = (x, y, 1 - c)
        pairs = ((_win_half(gwin_ref, 1 - c), win_got),
                 (gsq_ref.at[:, :, pl.ds((1 - c) * (D // 2), D // 2)], sq_got))
        copies = [pltpu.make_async_remote_copy(src_ref=src, dst_ref=dst, send_sem=send_sems.at[a], recv_sem=recv_sems.at[a],
                                               device_id=sib, device_id_type=MESH) for a, (src, dst) in enumerate(pairs)]
        for cp in copies:
            cp.start()
        for cp in copies:
            cp.wait()

    return pl.pallas_call(
        body, name="swap_halves",
        in_specs=[HBM_SPEC, HBM_SPEC], out_specs=[HBM_SPEC, HBM_SPEC],
        out_shape=[jax.ShapeDtypeStruct((D // 2, D_IN), F32), jax.ShapeDtypeStruct((3, D, D // 2), F32)],
        scratch_shapes=[pltpu.SemaphoreType.DMA((2,)), pltpu.SemaphoreType.DMA((2,))],
    )(gwin, gsq)


def _add_halves(c_arr, gwin, gsq, win_got, sq_got):
    def body(c_ref, a_ref, b_ref, p_ref, q_ref, so_ref, sq_ref, sob_ref, sqb_ref):
        so = a_ref[...] + b_ref[...]
        sq = p_ref[...] + q_ref[...]
        so_ref[...] = so
        sq_ref[...] = sq
        sob_ref[...] = _bf(so)
        sqb_ref[...] = _bf(sq)

    steps = 8
    rows, sq_rows = (D // 2) // steps, D // steps
    win = lambda f: pl.BlockSpec((rows, D_IN), f)
    sq = lambda f: pl.BlockSpec((3, sq_rows, D // 2), f)
    return pl.pallas_call(
        body, name="add_halves",
        grid_spec=pltpu.PrefetchScalarGridSpec(
            num_scalar_prefetch=1, grid=(steps,),
            in_specs=[win(lambda i, c: (c[0] * steps + i, 0)), win(lambda i, c: (i, 0)),
                      sq(lambda i, c: (0, i, c[0])), sq(lambda i, c: (0, i, 0))],
            out_specs=[win(lambda i, c: (i, 0)), sq(lambda i, c: (0, i, 0))] * 2),
        out_shape=[jax.ShapeDtypeStruct((D // 2, D_IN), F32), jax.ShapeDtypeStruct((3, D, D // 2), F32),
                   jax.ShapeDtypeStruct((D // 2, D_IN), BF16), jax.ShapeDtypeStruct((3, D, D // 2), BF16)],
        compiler_params=_params(("arbitrary",)),
    )(c_arr, gwin, win_got, gsq, sq_got)


def _sum_chips(jc_arr, swin, ssq, win_got, sq_got):
    def body(jc_ref, a_ref, b_ref, p_ref, q_ref, so_ref, sq_ref):
        so_ref[...] = ((a_ref[...] + b_ref[0].astype(F32)) + b_ref[1].astype(F32)) + b_ref[2].astype(F32)
        sq_ref[...] = ((p_ref[...] + q_ref[0].astype(F32)) + q_ref[1].astype(F32)) + q_ref[2].astype(F32)

    rows = 128
    steps = (D // 2) // rows
    sq_rows = SQ_ROWS // steps
    return pl.pallas_call(
        body, name="sum_chips",
        grid_spec=pltpu.PrefetchScalarGridSpec(
            num_scalar_prefetch=1, grid=(steps,),
            in_specs=[pl.BlockSpec((rows, SHARD_W), lambda i, jc: (i, jc[0])),
                      pl.BlockSpec((3, rows, SHARD_W), lambda i, jc: (0, i, 0)),
                      pl.BlockSpec((3, sq_rows, D // 2), lambda i, jc: (0, jc[0] * steps + i, 0)),
                      pl.BlockSpec((3, 3, sq_rows, D // 2), lambda i, jc: (0, 0, i, 0))],
            out_specs=[pl.BlockSpec((rows, SHARD_W), lambda i, jc: (jc[1] * steps + i, 0)),
                       pl.BlockSpec((3, sq_rows, D // 2), lambda i, jc: (0, i, jc[1]))]),
        out_shape=[jax.ShapeDtypeStruct((D, SHARD_W), F32), jax.ShapeDtypeStruct((3, SQ_ROWS, D), F32)],
        compiler_params=_params(("arbitrary",)),
    )(jc_arr, swin, win_got, ssq, sq_got)


def _join_halves(g_win, g_sq):
    def body(win_in, sq_in, win_out, sq_out, send_sems, recv_sems):
        del win_in, sq_in
        x, y, c = _place()
        sib = (x, y, 1 - c)

        def halves(h):
            return _win_half(win_out, h), sq_out.at[:, :, pl.ds(h * (D // 2), D // 2)]

        def copy(a, part):
            return pltpu.make_async_remote_copy(src_ref=part, dst_ref=part, send_sem=send_sems.at[a], recv_sem=recv_sems.at[a],
                                                device_id=sib, device_id_type=MESH)

        sent = [copy(a, part) for a, part in enumerate(halves(c))]
        for cp in sent:
            cp.start()
        for a, part in enumerate(halves(1 - c)):
            copy(a, part).wait_recv()
        for cp in sent:
            cp.wait_send()

    return pl.pallas_call(
        body, name="join_halves",
        in_specs=[HBM_SPEC, HBM_SPEC], out_specs=[HBM_SPEC, HBM_SPEC], input_output_aliases={0: 0, 1: 1},
        out_shape=[jax.ShapeDtypeStruct((D, SHARD_W), F32), jax.ShapeDtypeStruct((3, SQ_ROWS, D), F32)],
        scratch_shapes=[pltpu.SemaphoreType.DMA((2,)), pltpu.SemaphoreType.DMA((2,))],
    )(g_win, g_sq)


def _allreduce_small(vec):
    def body(vec_ref, out_ref, slots, send_sems, recv_sems):
        x, y, c = _place()
        me = 4 * x + 2 * y + c
        slots[me] = vec_ref[...]
        copies = []
        for k in range(1, 8):
            fx, fy, fc = (k >> 2) & 1, (k >> 1) & 1, k & 1
            copies.append(pltpu.make_async_remote_copy(
                src_ref=vec_ref, dst_ref=slots.at[me], send_sem=send_sems.at[k - 1], recv_sem=recv_sems.at[k - 1],
                device_id=(_flip(x, fx), _flip(y, fy), _flip(c, fc)), device_id_type=MESH))
        for cp in copies:
            cp.start()
        for k in range(1, 8):
            fx, fy, fc = (k >> 2) & 1, (k >> 1) & 1, k & 1
            src = 4 * _flip(x, fx) + 2 * _flip(y, fy) + _flip(c, fc)
            pltpu.make_async_remote_copy(src_ref=vec_ref, dst_ref=slots.at[src], send_sem=send_sems.at[k - 1],
                                         recv_sem=recv_sems.at[k - 1], device_id=(x, y, c), device_id_type=MESH).wait_recv()
        for cp in copies:
            cp.wait_send()
        total = slots[0]
        for s in range(1, 8):
            total = total + slots[s]
        out_ref[...] = total

    return pl.pallas_call(
        body, name="allreduce_small",
        in_specs=[pl.BlockSpec(memory_space=pltpu.VMEM)], out_specs=pl.BlockSpec(memory_space=pltpu.VMEM),
        out_shape=jax.ShapeDtypeStruct((8, D), F32),
        scratch_shapes=[pltpu.VMEM((8, 8, D), F32), pltpu.SemaphoreType.DMA((7,)), pltpu.SemaphoreType.DMA((7,))],
    )(vec)


def _adamw_math(w, g, m, v):
    m = ADAM_B1 * m + (1.0 - ADAM_B1) * g
    v = ADAM_B2 * v + (1.0 - ADAM_B2) * (g * g)
    m_hat = m / (1.0 - ADAM_B1 ** ADAM_STEP)
    v_hat = v / (1.0 - ADAM_B2 ** ADAM_STEP)
    delta = -ADAM_LR * (m_hat / (jnp.sqrt(v_hat) + ADAM_EPS) + ADAM_WD * w)
    return delta, m, v


def _adamw(name, w, g, m, v, rows):
    R, C = w.shape

    def body(w_ref, g_ref, m_ref, v_ref, d_out, m_out, v_out):
        d_out[...], m_out[...], v_out[...] = _adamw_math(w_ref[...], g_ref[...], m_ref[...], v_ref[...])

    spec = pl.BlockSpec((rows, C), lambda i: (i, 0))
    return pl.pallas_call(
        body, name=name, grid=(R // rows,), in_specs=[spec] * 4, out_specs=[spec] * 3,
        out_shape=[jax.ShapeDtypeStruct((R, C), F32)] * 3,
        compiler_params=_params(("parallel",)),
    )(w, g, m, v)


def _small_update(total, lbw, w8, m8, v8):
    def body(t_ref, lbw_ref, w_ref, m_ref, v_ref, g_out, d_out, m_out, v_out):
        lb = 1.0 / (1.0 + jnp.exp(lbw_ref[1:2, :] - lbw_ref[0:1, :]))
        dlb = t_ref[2:3, :] * lb * (1.0 - lb)
        g_out[...] = jnp.zeros_like(g_out)
        g_out[0:1, :] = t_ref[3:4, :]
        g_out[1:2, :] = dlb
        g_out[2:3, :] = -dlb
        g_out[3:4, :] = t_ref[1:2, :]
        g_out[4:5, :] = t_ref[0:1, :]
        g_out[5:6, :] = t_ref[4:5, :]
        d_out[...], m_out[...], v_out[...] = _adamw_math(w_ref[...], g_out[...], m_ref[...], v_ref[...])

    return pl.pallas_call(
        body, name="small_update", out_shape=[jax.ShapeDtypeStruct((8, D), F32)] * 4,
        compiler_params=_params(),
    )(total, lbw, w8, m8, v8)


def _pack8(norm_w, lbw, hnw, fnw, sinks):
    pad = jnp.zeros((1, D - 16), F32)
    return jnp.concatenate([norm_w, lbw, hnw, fnw.reshape(1, D), jnp.concatenate([sinks, pad], axis=1),
                            jnp.zeros((2, D), F32)], axis=0)


def _unpack8(a):
    return a[0:1], a[1:3], a[3:4], a[5:6, 0:16], a[4]


def _local_step(x, tgt, norm_w, lbw, hnw, sinks, fnw, win_bf, wsq_mine, exchange):
    proj, xn_bf = _fwd_proj(x, norm_w, win_bf)
    oh, states, wsq_all = _hgrn_fwd(proj, lbw, wsq_mine)
    wsq_bf = wsq_all.reshape(SHARDS, 3, SQ_ROWS, D).transpose(1, 0, 2, 3).reshape(3, D, D)
    oa = _attn_fwd(proj, sinks)
    dx2, doh, doa, dhg, dtail, lhs, rhs, loss8, vec_mid = _mid(x, tgt, proj, oh, oa, hnw, fnw.reshape(1, D), wsq_bf)
    gsq = _wgrad_square(lhs, rhs)
    dhead, dlb = _hgrn_bwd(proj, lbw, states, doh)
    daq, dak, dav, dsink = _attn_bwd(proj, sinks, oa, doa)
    pieces = [dhead, dhg, daq, dak, dav, dtail]
    sums = exchange(_wgrad_in(xn_bf, pieces), gsq)
    wt_bf = win_bf.transpose(0, 2, 1).reshape(D_IN, D)
    grad_x, gnw, win_got, sq_got = _bwd_dx(pieces, wt_bf, x, norm_w, dx2, sums[2], sums[3])
    sink_row = jnp.concatenate([dsink[:, 0].reshape(1, 16), jnp.zeros((1, D - 16), F32)], axis=1)
    vec = jnp.concatenate([vec_mid[0:2], dlb, gnw, sink_row, jnp.zeros((3, D), F32)], axis=0)
    return loss8[0, 0], grad_x, sums, (win_got, sq_got), vec


def kernel(x, norm_w, w_in, hgrn_lower_bound, hgrn_norm_w, w_branch_hgrn, attn_sinks, w_branch_attn, w_out, final_norm_w, loss_target, m_norm_w, m_w_in, m_hgrn_lower_bound, m_hgrn_norm_w, m_w_branch_hgrn, m_attn_sinks, m_w_branch_attn, m_w_out, m_final_norm_w, v_norm_w, v_w_in, v_hgrn_lower_bound, v_hgrn_norm_w, v_w_branch_hgrn, v_attn_sinks, v_w_branch_attn, v_w_out, v_final_norm_w):
    c_arr = lax.axis_index("c").astype(jnp.int32).reshape(1)
    j_arr = (2 * lax.axis_index("x") + lax.axis_index("y")).astype(jnp.int32).reshape(1)
    jc_arr = jnp.concatenate([j_arr, c_arr])

    win_mine, wsq_mine = _cast_shards(j_arr, w_in[0], w_branch_hgrn[0], w_branch_attn[0], w_out[0])
    win_bf = _allgather_w_in(win_mine)

    def chip_sums(gwin, gsq):
        return _add_halves(c_arr, gwin, gsq, *_swap_halves(gwin, gsq))

    loss_part, grad_x, (swin, ssq, _, _), arrived, vec = _local_step(
        x[0], loss_target[0], norm_w, hgrn_lower_bound, hgrn_norm_w, attn_sinks, final_norm_w, win_bf, wsq_mine, chip_sums)
    loss = lax.psum(loss_part, ("x", "y", "c"))
    g_win, g_sq = _join_halves(*_sum_chips(jc_arr, swin, ssq, *arrived))

    d_win, nm_win, nv_win = _adamw("adamw_w_in", w_in[0], g_win, m_w_in[0], v_w_in[0], 128)
    sq_w = jnp.concatenate([w_branch_hgrn[0], w_branch_attn[0], w_out[0]], axis=0)
    sq_m = jnp.concatenate([m_w_branch_hgrn[0], m_w_branch_attn[0], m_w_out[0]], axis=0)
    sq_v = jnp.concatenate([v_w_branch_hgrn[0], v_w_branch_attn[0], v_w_out[0]], axis=0)
    d_sq, nm_sq, nv_sq = _adamw("adamw_square", sq_w, g_sq.reshape(3 * SQ_ROWS, D), sq_m, sq_v, 256)

    total = _allreduce_small(vec)
    g8, d8, nm8, nv8 = _small_update(
        total, hgrn_lower_bound,
        _pack8(norm_w, hgrn_lower_bound, hgrn_norm_w, final_norm_w, attn_sinks),
        _pack8(m_norm_w, m_hgrn_lower_bound, m_hgrn_norm_w, m_final_norm_w, m_attn_sinks),
        _pack8(v_norm_w, v_hgrn_lower_bound, v_hgrn_norm_w, v_final_norm_w, v_attn_sinks))

    def assemble(win, sq, small):
        nw, lb, hn, sk, fn = _unpack8(small)
        sq = sq.reshape(3, 1, SQ_ROWS, D)
        return (nw, win.reshape(1, D, SHARD_W), lb, hn, sq[0], sk, sq[1], sq[2], fn)

    return (loss, grad_x.reshape(1, -1, D),
            *assemble(g_win, g_sq, g8), *assemble(d_win, d_sq, d8),
            *assemble(nm_win, nm_sq, nm8), *assemble(nv_win, nv_sq, nv8))
```

```python
import functools

import jax
import jax.numpy as jnp
from jax import lax
from jax.experimental import pallas as pl
from jax.experimental.pallas import tpu as pltpu

F32 = jnp.float32
BF16 = jnp.bfloat16

D = 1024
D_IN = 8704
SHARDS = 4
SHARD_W = D_IN // SHARDS
SQ_ROWS = D // SHARDS
HEADS = 8
HEAD_W = 128
CHUNK = 64
SUB = 2
ATT_BLOCK = 128
KV_HEADS = 4
HEAD_DIM = 64
EPS = 1e-6
NEG = -1e30
SCALE = HEAD_DIM ** -0.5
COL_HG, COL_AQ, COL_AK, COL_AV, COL_AG, COL_MH, COL_MA = 3072, 4096, 5120, 5376, 5632, 6656, 7680

ADAM_LR, ADAM_B1, ADAM_B2, ADAM_EPS, ADAM_WD, ADAM_STEP = 0.001, 0.9, 0.999, 1e-08, 0.01, 10

VMEM_LIMIT = 56 * 1024 * 1024
MESH = pl.DeviceIdType.MESH
HBM_SPEC = pl.BlockSpec(memory_space=pltpu.HBM)
CHIP_FLIPS = ((1, 0), (0, 1), (1, 1))


def _dot(a, b):
    return jnp.dot(a, b, preferred_element_type=F32)


def _dot_nt(a, b):
    return lax.dot_general(a, b, (((1,), (1,)), ((), ())), preferred_element_type=F32)


def _dot_tn(a, b):
    return lax.dot_general(a, b, (((0,), (0,)), ((), ())), preferred_element_type=F32)


def _sigmoid(v):
    return 1.0 / (1.0 + jnp.exp(-v))


def _bf(v):
    return v.astype(BF16)


def _split3(v):
    a = _bf(v)
    r = v - a.astype(F32)
    b = _bf(r)
    c = _bf(r - b.astype(F32))
    return a, b, c


def _tri_dot(tri, v):
    a, b, c = _split3(v)
    return _dot(tri, a) + _dot(tri, b) + _dot(tri, c)


def _params(sem=None):
    return pltpu.CompilerParams(dimension_semantics=sem, vmem_limit_bytes=VMEM_LIMIT)


def _cast_shards(j_arr, win_s, wbh_s, wba_s, wout_s):
    steps = 4
    rows = D // steps

    def body(j_ref, win_ref, a_ref, b_ref, c_ref, win_o, sq_o):
        win_o[...] = _bf(win_ref[...])

        @pl.when(pl.program_id(0) == 0)
        def _():
            sq_o[0:SQ_ROWS, :] = _bf(a_ref[...])
            sq_o[SQ_ROWS:2 * SQ_ROWS, :] = _bf(b_ref[...])
            sq_o[2 * SQ_ROWS:3 * SQ_ROWS, :] = _bf(c_ref[...])

    whole = pl.BlockSpec((SQ_ROWS, D), lambda i, j: (0, 0))
    return pl.pallas_call(
        body, name="cast_shards",
        grid_spec=pltpu.PrefetchScalarGridSpec(
            num_scalar_prefetch=1, grid=(steps,),
            in_specs=[pl.BlockSpec((rows, SHARD_W), lambda i, j: (i, 0)), whole, whole, whole],
            out_specs=[pl.BlockSpec((None, rows, SHARD_W), lambda i, j: (j[0], i, 0)),
                       pl.BlockSpec((None, 3 * SQ_ROWS, D), lambda i, j: (j[0], 0, 0))]),
        out_shape=[jax.ShapeDtypeStruct((SHARDS, D, SHARD_W), BF16), jax.ShapeDtypeStruct((SHARDS, 3 * SQ_ROWS, D), BF16)],
        compiler_params=_params(("arbitrary",)),
    )(j_arr, win_s, wbh_s, wba_s, wout_s)


def _fwd_proj(x, norm_w, win_bf):
    T = x.shape[0]
    tm = min(256, T)

    def body(x_ref, nw_ref, w_hbm, proj_ref, xn_ref, w_scr, sem):
        @pl.when(pl.program_id(0) == 0)
        def _():
            cp = pltpu.make_async_copy(w_hbm, w_scr, sem)
            cp.start()
            cp.wait()

        xf = x_ref[...]
        rs = lax.rsqrt(jnp.mean(xf * xf, axis=1, keepdims=True) + EPS)
        xn = _bf((xf * rs) * nw_ref[...])
        xn_ref[...] = xn.T
        for j in range(SHARDS):
            proj_ref[:, j * SHARD_W:(j + 1) * SHARD_W] = _dot(xn, w_scr[j])

    return pl.pallas_call(
        body, name="fwd_proj", grid=(T // tm,),
        in_specs=[pl.BlockSpec((tm, D), lambda i: (i, 0)), pl.BlockSpec((1, D), lambda i: (0, 0)), HBM_SPEC],
        out_specs=[pl.BlockSpec((tm, D_IN), lambda i: (i, 0)), pl.BlockSpec((D, tm), lambda i: (0, i))],
        out_shape=[jax.ShapeDtypeStruct((T, D_IN), F32), jax.ShapeDtypeStruct((D, T), BF16)],
        scratch_shapes=[pltpu.VMEM((SHARDS, D, SHARD_W), BF16), pltpu.SemaphoreType.DMA],
        compiler_params=_params(("arbitrary",)),
    )(x, norm_w, win_bf)


def _hgrn_gates(hq_ref, hf_ref, lbw_ref, b_scr):
    lb = 1.0 / (1.0 + jnp.exp(lbw_ref[1:2, :] - lbw_ref[0:1, :]))
    hf = hf_ref[...]
    sig = _sigmoid(hf)
    f = lb + (1.0 - lb) * sig
    g = jnp.log(f)
    hq = hq_ref[...]
    sq = _sigmoid(hq)
    q = hq * sq
    row = lax.broadcasted_iota(jnp.int32, (CHUNK, CHUNK), 0)
    col = lax.broadcasted_iota(jnp.int32, (CHUNK, CHUNK), 1)
    causal = row >= col
    b = _tri_dot(jnp.where(causal, 1.0, 0.0).astype(BF16), g)
    b_scr[...] = b
    bc = b_scr[CHUNK - 1:CHUNK, :]
    r = b_scr[CHUNK // 2 - 1:CHUNK // 2, :]
    return dict(lb=lb, sig=sig, f=f, k=1.0 - f, hq=hq, sq=sq, q=q, b=b, bc=bc, r=r, causal=causal)


def _hgrn_fwd(proj, lbw, wsq_all):
    T = proj.shape[0]
    n = T // CHUNK

    def body(hq_ref, hf_ref, hi_ref, lbw_ref, wsq_in, o_ref, st_ref, wsq_out, s_scr, b_scr, send_sems, recv_sems):
        del wsq_in

        @pl.when(pl.program_id(0) == 0)
        def _():
            _gather_start(wsq_out, _sq_half, send_sems, recv_sems)
            s_scr[...] = jnp.zeros_like(s_scr)

        for c in range(SUB):
            rows = pl.ds(c * CHUNK, CHUNK)
            gt = _hgrn_gates(hq_ref.at[rows, :], hf_ref.at[rows, :], lbw_ref, b_scr.at[rows, :])
            b, bc, r, q, k = gt["b"], gt["bc"], gt["r"], gt["q"], gt["k"]
            qe = _bf(q * jnp.exp(b))
            qr = _bf(q * jnp.exp(b - r))
            kr = _bf(k * jnp.exp(r - b))
            kl = _bf(k * jnp.exp(bc - b))
            ebc = jnp.exp(bc)
            v = _bf(hi_ref[rows, :])
            for h in range(HEADS):
                sl = slice(h * HEAD_W, (h + 1) * HEAD_W)
                st = s_scr[h]
                st_ref[c, h] = st
                a = jnp.where(gt["causal"], _dot_nt(qr[:, sl], kr[:, sl]), 0.0)
                o_ref[rows, sl] = _dot(_bf(a), v[:, sl]) + _dot_nt(qe[:, sl], _bf(st))
                s_scr[h] = ebc[:, sl] * st + _dot_tn(v[:, sl], kl[:, sl])

        @pl.when(pl.program_id(0) == n // SUB - 1)
        def _():
            _gather_finish(wsq_out, _sq_half, send_sems, recv_sems)

    col = lambda j: pl.BlockSpec((SUB * CHUNK, D), lambda i: (i, j))
    return pl.pallas_call(
        body, name="hgrn_fwd", grid=(n // SUB,),
        in_specs=[col(0), col(1), col(2), pl.BlockSpec((2, D), lambda i: (0, 0)), HBM_SPEC],
        out_specs=[pl.BlockSpec((SUB * CHUNK, D), lambda i: (i, 0)),
                   pl.BlockSpec((SUB, HEADS, HEAD_W, HEAD_W), lambda i: (i, 0, 0, 0)), HBM_SPEC],
        out_shape=[jax.ShapeDtypeStruct((T, D), F32), jax.ShapeDtypeStruct((n, HEADS, HEAD_W, HEAD_W), F32),
                   jax.ShapeDtypeStruct((SHARDS, 3 * SQ_ROWS, D), BF16)],
        input_output_aliases={4: 2},
        scratch_shapes=[pltpu.VMEM((HEADS, HEAD_W, HEAD_W), F32), pltpu.VMEM((SUB * CHUNK, D), F32),
                        pltpu.SemaphoreType.DMA((6,)), pltpu.SemaphoreType.DMA((6,))],
        compiler_params=_params(("arbitrary",)),
    )(proj, proj, proj, lbw, wsq_all)


def _hgrn_bwd(proj, lbw, states, do):
    T = proj.shape[0]
    n = T // CHUNK

    def body(hq_ref, hf_ref, hi_ref, lbw_ref, st_ref, do_ref, dp_ref, dlb_ref,
             ds_scr, b_scr, dq_scr, dk_scr, dv_scr, late_scr, early_scr, ex_scr):
        @pl.when(pl.program_id(0) == 0)
        def _():
            ds_scr[...] = jnp.zeros_like(ds_scr)
            dlb_ref[...] = jnp.zeros_like(dlb_ref)

        for c in reversed(range(SUB)):
            rows = pl.ds(c * CHUNK, CHUNK)
            gt = _hgrn_gates(hq_ref.at[rows, :], hf_ref.at[rows, :], lbw_ref, b_scr.at[rows, :])
            b, bc, r, q, k = gt["b"], gt["bc"], gt["r"], gt["q"], gt["k"]
            eb = jnp.exp(b)
            er = jnp.exp(b - r)
            erk = jnp.exp(r - b)
            el = jnp.exp(bc - b)
            ebc = jnp.exp(bc)
            qe, qr, kr, kl = _bf(q * eb), _bf(q * er), _bf(k * erk), _bf(k * el)
            v = _bf(hi_ref[rows, :])
            do_b = _bf(do_ref[rows, :])
            for h in range(HEADS):
                sl = slice(h * HEAD_W, (h + 1) * HEAD_W)
                st0 = st_ref[c, h]
                dst = ds_scr[h]
                dst_b = _bf(dst)
                a = _bf(jnp.where(gt["causal"], _dot_nt(qr[:, sl], kr[:, sl]), 0.0))
                da = _bf(jnp.where(gt["causal"], _dot_nt(do_b[:, sl], v[:, sl]), 0.0))
                mq = _dot(da, kr[:, sl])
                mk = _dot_tn(da, qr[:, sl])
                dq_in = eb[:, sl] * _dot(do_b[:, sl], _bf(st0))
                dk_in = el[:, sl] * _dot(v[:, sl], dst_b)
                dq_scr[rows, sl] = er[:, sl] * mq + dq_in
                dk_scr[rows, sl] = erk[:, sl] * mk + dk_in
                dv_scr[rows, sl] = _dot_tn(a, do_b[:, sl]) + _dot_nt(kl[:, sl], dst_b)
                late_scr[rows, sl] = q[:, sl] * dq_in + qr[:, sl].astype(F32) * mq - kr[:, sl].astype(F32) * mk
                early_scr[rows, sl] = k[:, sl] * dk_in
                ex_scr[:, sl] = jnp.sum(dst * st0, axis=0, keepdims=True)
                ds_scr[h] = ebc[:, sl] * dst + _dot_tn(do_b[:, sl], qe[:, sl])

            dq, dk = dq_scr[rows, :], dk_scr[rows, :]
            row = lax.broadcasted_iota(jnp.int32, (CHUNK, CHUNK), 0)
            col = lax.broadcasted_iota(jnp.int32, (CHUNK, CHUNK), 1)
            at_or_after = jnp.where(col >= row, 1.0, 0.0).astype(BF16)
            before = jnp.where(col < row, 1.0, 0.0).astype(BF16)
            dg = _tri_dot(at_or_after, late_scr[rows, :]) + _tri_dot(before, early_scr[rows, :]) + ebc * ex_scr[...]
            df = dg / gt["f"] - dk
            sig, sq, hq, lb = gt["sig"], gt["sq"], gt["hq"], gt["lb"]
            dp_ref[rows, 0:D] = _bf(dq * (sq * (1.0 + hq * (1.0 - sq))))
            dp_ref[rows, D:2 * D] = _bf(df * (1.0 - lb) * sig * (1.0 - sig))
            dp_ref[rows, 2 * D:3 * D] = _bf(dv_scr[rows, :])
            dlb_ref[...] += jnp.sum(df * (1.0 - sig), axis=0, keepdims=True)

    ns = n // SUB
    col = lambda j: pl.BlockSpec((SUB * CHUNK, D), lambda i: (ns - 1 - i, j))
    return pl.pallas_call(
        body, name="hgrn_bwd", grid=(ns,),
        in_specs=[col(0), col(1), col(2), pl.BlockSpec((2, D), lambda i: (0, 0)),
                  pl.BlockSpec((SUB, HEADS, HEAD_W, HEAD_W), lambda i: (ns - 1 - i, 0, 0, 0)),
                  pl.BlockSpec((SUB * CHUNK, D), lambda i: (ns - 1 - i, 0))],
        out_specs=[pl.BlockSpec((SUB * CHUNK, 3 * D), lambda i: (ns - 1 - i, 0)),
                   pl.BlockSpec((1, D), lambda i: (0, 0))],
        out_shape=[jax.ShapeDtypeStruct((T, 3 * D), BF16), jax.ShapeDtypeStruct((1, D), F32)],
        scratch_shapes=[pltpu.VMEM((HEADS, HEAD_W, HEAD_W), F32)] + [pltpu.VMEM((SUB * CHUNK, D), F32)] * 6
                       + [pltpu.VMEM((1, D), F32)],
        compiler_params=_params(("arbitrary",)),
    )(proj, proj, proj, lbw, states, do)


def _attn_masks(blk):
    qi = lax.broadcasted_iota(jnp.int32, (ATT_BLOCK, 2 * ATT_BLOCK), 0)
    kj = lax.broadcasted_iota(jnp.int32, (ATT_BLOCK, 2 * ATT_BLOCK), 1)
    band = (kj > qi) & (kj <= qi + ATT_BLOCK)
    return band & ((blk > 0) | (kj >= ATT_BLOCK))


def _head_pair_operand(t, hp, low):
    mine = low if hp == 0 else jnp.logical_not(low)
    both = jnp.where(mine, t, pltpu.roll(t, HEAD_DIM, 1))
    return _bf(jnp.concatenate([jnp.where(low, both, 0.0), jnp.where(low, 0.0, both)], axis=0))


def _attn_probs(s, sink, valid):
    s = jnp.where(valid, s * SCALE, NEG)
    m = jnp.maximum(jnp.max(s, axis=1, keepdims=True), sink)
    p = jnp.exp(s - m)
    es = jnp.exp(sink - m)
    inv = 1.0 / (jnp.sum(p, axis=1, keepdims=True) + es)
    return p * inv, es * inv


def _attn_fwd(proj, sinks):
    T = proj.shape[0]
    nb = T // ATT_BLOCK
    W2 = 2 * ATT_BLOCK

    def body(sink_ref, q_ref, kp_ref, kc_ref, vp_ref, vc_ref, o_ref):
        blk = pl.program_id(0)
        valid = _attn_masks(blk)
        low = lax.broadcasted_iota(jnp.int32, (1, 2 * HEAD_DIM), 1) < HEAD_DIM
        kcat = jnp.concatenate([kp_ref[...], kc_ref[...]], axis=0)
        vcat = jnp.concatenate([vp_ref[...], vc_ref[...]], axis=0)
        for h in range(KV_HEADS):
            tl = slice((h // 2) * 128, (h // 2) * 128 + 128)
            mine = low if h % 2 == 0 else jnp.logical_not(low)
            kh = _bf(jnp.where(mine, kcat[:, tl], pltpu.roll(kcat[:, tl], HEAD_DIM, 1)))
            vh = _bf(jnp.where(mine, vcat[:, tl], pltpu.roll(vcat[:, tl], HEAD_DIM, 1)))
            for t in range(2):
                ql = slice((2 * h + t) * 128, (2 * h + t) * 128 + 128)
                q2 = q_ref[:, ql]
                outs = []
                for p in range(2):
                    qm = _bf(jnp.where(low if p == 0 else jnp.logical_not(low), q2, 0.0))
                    probs, _ = _attn_probs(_dot_nt(qm, kh), sink_ref[0, 4 * h + 2 * t + p], valid)
                    outs.append(_dot(_bf(probs), vh))
                o_ref[:, ql] = jnp.where(low, outs[0], outs[1])

    prev = lambda i: jnp.maximum(i - 1, 0)
    return pl.pallas_call(
        body, name="attn_fwd", grid=(nb,),
        in_specs=[pl.BlockSpec(memory_space=pltpu.SMEM),
                  pl.BlockSpec((ATT_BLOCK, D), lambda i: (i, COL_AQ // D)),
                  pl.BlockSpec((ATT_BLOCK, 256), lambda i: (prev(i), COL_AK // 256)),
                  pl.BlockSpec((ATT_BLOCK, 256), lambda i: (i, COL_AK // 256)),
                  pl.BlockSpec((ATT_BLOCK, 256), lambda i: (prev(i), COL_AV // 256)),
                  pl.BlockSpec((ATT_BLOCK, 256), lambda i: (i, COL_AV // 256))],
        out_specs=pl.BlockSpec((ATT_BLOCK, D), lambda i: (i, 0)),
        out_shape=jax.ShapeDtypeStruct((T, D), F32),
        compiler_params=_params(("arbitrary",)),
    )(sinks, proj, proj, proj, proj, proj)


def _attn_bwd(proj, sinks, o, do):
    T = proj.shape[0]
    nb = T // ATT_BLOCK
    W2 = 2 * ATT_BLOCK

    def body(sink_ref, q_ref, kp_ref, kc_ref, vp_ref, vc_ref, o_ref, do_ref,
             dq_ref, dk_ref, dv_ref, dsink_ref, ck_scr, cv_scr, nk_scr, nv_scr):
        blk = pl.program_id(0)

        @pl.when(blk == 0)
        def _():
            ck_scr[...] = jnp.zeros_like(ck_scr)
            cv_scr[...] = jnp.zeros_like(cv_scr)
            dsink_ref[...] = jnp.zeros_like(dsink_ref)

        @pl.when(blk < nb)
        def _():
            valid = _attn_masks(blk)
            low = lax.broadcasted_iota(jnp.int32, (1, 2 * HEAD_DIM), 1) < HEAD_DIM
            kcat = jnp.concatenate([kp_ref[...], kc_ref[...]], axis=0)
            vcat = jnp.concatenate([vp_ref[...], vc_ref[...]], axis=0)
            for h in range(KV_HEADS):
                tl = slice((h // 2) * 128, (h // 2) * 128 + 128)
                kbd = _head_pair_operand(kcat[:, tl], h % 2, low)
                vbd = _head_pair_operand(vcat[:, tl], h % 2, low)
                dkbd = jnp.zeros((2 * W2, 128), F32)
                dvbd = jnp.zeros((2 * W2, 128), F32)
                for t in range(2):
                    ql = slice((2 * h + t) * 128, (2 * h + t) * 128 + 128)
                    head = 4 * h + 2 * t
                    q2 = _bf(q_ref[:, ql])
                    do2 = do_ref[:, ql]
                    do2_b = _bf(do2)
                    doo = do2 * o_ref[:, ql]
                    dsum0 = jnp.sum(jnp.where(low, doo, 0.0), axis=1, keepdims=True)
                    dsum1 = jnp.sum(jnp.where(low, 0.0, doo), axis=1, keepdims=True)
                    s2 = _dot_nt(q2, kbd)
                    p0, ps0 = _attn_probs(s2[:, 0:W2], sink_ref[0, head], valid)
                    p1, ps1 = _attn_probs(s2[:, W2:2 * W2], sink_ref[0, head + 1], valid)
                    dp2 = _dot_nt(do2_b, vbd)
                    ds2 = _bf(jnp.concatenate([p0 * (dp2[:, 0:W2] - dsum0), p1 * (dp2[:, W2:2 * W2] - dsum1)], axis=1) * SCALE)
                    dq_ref[:, ql] = _bf(_dot(ds2, kbd))
                    dkbd = dkbd + _dot_tn(ds2, q2)
                    dvbd = dvbd + _dot_tn(_bf(jnp.concatenate([p0, p1], axis=1)), do2_b)
                    dsink_ref[head:head + 1, :] += jnp.zeros((1, 128), F32) - jnp.sum(ps0 * dsum0, axis=0, keepdims=True)
                    dsink_ref[head + 1:head + 2, :] += jnp.zeros((1, 128), F32) - jnp.sum(ps1 * dsum1, axis=0, keepdims=True)
                dk2 = jnp.where(low, dkbd[0:W2], dkbd[W2:2 * W2])
                dv2 = jnp.where(low, dvbd[0:W2], dvbd[W2:2 * W2])
                dk2 = dk2 + pltpu.roll(dk2, HEAD_DIM, 1)
                dv2 = dv2 + pltpu.roll(dv2, HEAD_DIM, 1)
                if h % 2 == 0:
                    keep_k, keep_v = dk2, dv2
                else:
                    nk_scr[:, tl] = jnp.where(low, keep_k, dk2)
                    nv_scr[:, tl] = jnp.where(low, keep_v, dv2)
            dk_ref[...] = _bf(ck_scr[...] + nk_scr[0:ATT_BLOCK, :])
            dv_ref[...] = _bf(cv_scr[...] + nv_scr[0:ATT_BLOCK, :])
            ck_scr[...] = nk_scr[ATT_BLOCK:2 * ATT_BLOCK, :]
            cv_scr[...] = nv_scr[ATT_BLOCK:2 * ATT_BLOCK, :]

        @pl.when(blk == nb)
        def _():
            dk_ref[...] = _bf(ck_scr[...])
            dv_ref[...] = _bf(cv_scr[...])

    cur = lambda i: jnp.minimum(i, nb - 1)
    prev = lambda i: jnp.maximum(cur(i) - 1, 0)
    late = lambda i: jnp.maximum(i - 1, 0)
    dq, dk, dv, dsink = pl.pallas_call(
        body, name="attn_bwd", grid=(nb + 1,),
        in_specs=[pl.BlockSpec(memory_space=pltpu.SMEM),
                  pl.BlockSpec((ATT_BLOCK, D), lambda i: (cur(i), COL_AQ // D)),
                  pl.BlockSpec((ATT_BLOCK, 256), lambda i: (prev(i), COL_AK // 256)),
                  pl.BlockSpec((ATT_BLOCK, 256), lambda i: (cur(i), COL_AK // 256)),
                  pl.BlockSpec((ATT_BLOCK, 256), lambda i: (prev(i), COL_AV // 256)),
                  pl.BlockSpec((ATT_BLOCK, 256), lambda i: (cur(i), COL_AV // 256)),
                  pl.BlockSpec((ATT_BLOCK, D), lambda i: (cur(i), 0)),
                  pl.BlockSpec((ATT_BLOCK, D), lambda i: (cur(i), 0))],
        out_specs=[pl.BlockSpec((ATT_BLOCK, D), lambda i: (cur(i), 0)),
                   pl.BlockSpec((ATT_BLOCK, 256), lambda i: (late(i), 0)),
                   pl.BlockSpec((ATT_BLOCK, 256), lambda i: (late(i), 0)),
                   pl.BlockSpec((16, 128), lambda i: (0, 0))],
        out_shape=[jax.ShapeDtypeStruct((T, D), BF16), jax.ShapeDtypeStruct((T, 256), BF16),
                   jax.ShapeDtypeStruct((T, 256), BF16), jax.ShapeDtypeStruct((16, 128), F32)],
        scratch_shapes=[pltpu.VMEM((ATT_BLOCK, 256), F32), pltpu.VMEM((ATT_BLOCK, 256), F32),
                        pltpu.VMEM((2 * ATT_BLOCK, 256), F32), pltpu.VMEM((2 * ATT_BLOCK, 256), F32)],
        compiler_params=_params(("arbitrary",)),
    )(sinks, proj, proj, proj, proj, proj, o, do)
    return dq, dk, dv, dsink


def _mid(x, tgt, proj, oh, oa, hnw, fnw, wsq_bf):
    T = x.shape[0]
    tm = min(256, T)
    nt = T // tm

    def body(x_ref, tgt_ref, oh_ref, oa_ref, hg_ref, ag0_ref, ag1_ref, mh0_ref, mh1_ref, ma0_ref, ma1_ref,
             hnw_ref, fnw_ref, w_hbm,
             dx2_ref, doh_ref, doa_ref, dhg_ref, dtail_ref, lhs_ref, rhs_ref, loss_ref, vec_ref,
             w_scr, xh_scr, rs_scr, sem):
        @pl.when(pl.program_id(0) == 0)
        def _():
            cp = pltpu.make_async_copy(w_hbm, w_scr, sem)
            cp.start()
            cp.wait()
            loss_ref[...] = jnp.zeros_like(loss_ref)
            vec_ref[...] = jnp.zeros_like(vec_ref)

        oh = oh_ref[...]
        for h in range(HEADS):
            sl = slice(h * HEAD_W, (h + 1) * HEAD_W)
            ohh = oh[:, sl]
            rs = lax.rsqrt(jnp.mean(ohh * ohh, axis=1, keepdims=True) + EPS)
            xh_scr[:, sl] = ohh * rs
            rs_scr[:, sl] = jnp.broadcast_to(rs, (tm, HEAD_W))
        xh = xh_scr[...]
        hnw = hnw_ref[...]
        on = xh * hnw
        hg = hg_ref[...]
        sg = _sigmoid(hg)
        silu_g = hg * sg
        gated_h = _bf(on * silu_g)
        lhs_ref[0] = gated_h.T
        yh = _dot(gated_h, w_scr[0])
        oa = oa_ref[...]
        ag = jnp.concatenate([ag0_ref[...], ag1_ref[...]], axis=1)
        sa = _sigmoid(ag)
        silu_a = ag * sa
        gated_a = _bf(oa * silu_a)
        lhs_ref[1] = gated_a.T
        ya = _dot(gated_a, w_scr[1])
        smh = _sigmoid(jnp.concatenate([mh0_ref[...], mh1_ref[...]], axis=1))
        sma = _sigmoid(jnp.concatenate([ma0_ref[...], ma1_ref[...]], axis=1))
        merged = _bf(smh * yh + sma * ya)
        lhs_ref[2] = merged.T
        x2 = x_ref[...] + _dot(merged, w_scr[2])
        rs2 = lax.rsqrt(jnp.mean(x2 * x2, axis=1, keepdims=True) + EPS)
        xh2 = x2 * rs2
        fnw = fnw_ref[...]
        diff = xh2 * fnw - tgt_ref[...]
        loss_ref[...] += jnp.zeros_like(loss_ref) + jnp.sum(diff * diff) * (0.5 / D)

        dy = diff * (1.0 / D)
        vec_ref[0:1, :] += jnp.sum(dy * xh2, axis=0, keepdims=True)
        gy = dy * fnw
        dx2 = rs2 * (gy - xh2 * jnp.mean(gy * xh2, axis=1, keepdims=True))
        dx2_ref[...] = dx2
        dx2_b = _bf(dx2)
        rhs_ref[2] = dx2_b
        dmerged = _dot_nt(dx2_b, w_scr[2])
        dyh = dmerged * smh
        dya = dmerged * sma
        dtail_ref[:, D:2 * D] = _bf(dyh * yh * (1.0 - smh))
        dtail_ref[:, 2 * D:3 * D] = _bf(dya * ya * (1.0 - sma))
        dyh_b, dya_b = _bf(dyh), _bf(dya)
        rhs_ref[0] = dyh_b
        rhs_ref[1] = dya_b
        dgh = _dot_nt(dyh_b, w_scr[0])
        dga = _dot_nt(dya_b, w_scr[1])
        don = dgh * silu_g
        dhg_ref[...] = _bf(dgh * on * (sg * (1.0 + hg * (1.0 - sg))))
        vec_ref[1:2, :] += jnp.sum(don * xh, axis=0, keepdims=True)
        gxh = don * hnw
        rsb = rs_scr[...]
        for h in range(HEADS):
            sl = slice(h * HEAD_W, (h + 1) * HEAD_W)
            gh, xhh = gxh[:, sl], xh[:, sl]
            doh_ref[:, sl] = rsb[:, sl] * (gh - xhh * jnp.mean(gh * xhh, axis=1, keepdims=True))
        doa_ref[...] = dga * silu_a
        dtail_ref[:, 0:D] = _bf(dga * oa * (sa * (1.0 + ag * (1.0 - sa))))

    row = lambda w, j: pl.BlockSpec((tm, w), lambda i: (i, j))
    const = lambda r, c: pl.BlockSpec((r, c), lambda i: (0, 0))
    stack = pl.BlockSpec((3, tm, D), lambda i: (0, i, 0))
    stack_t = pl.BlockSpec((3, D, tm), lambda i: (0, 0, i))
    return pl.pallas_call(
        body, name="mid", grid=(nt,),
        in_specs=[row(D, 0), row(D, 0), row(D, 0), row(D, 0), row(D, COL_HG // D),
                  row(512, COL_AG // 512), row(512, COL_AG // 512 + 1),
                  row(512, COL_MH // 512), row(512, COL_MH // 512 + 1),
                  row(512, COL_MA // 512), row(512, COL_MA // 512 + 1),
                  const(1, D), const(1, D), HBM_SPEC],
        out_specs=[row(D, 0), row(D, 0), row(D, 0), row(D, 0), row(3 * D, 0), stack_t, stack, const(8, 128), const(8, D)],
        out_shape=[jax.ShapeDtypeStruct((T, D), F32), jax.ShapeDtypeStruct((T, D), F32), jax.ShapeDtypeStruct((T, D), F32),
                   jax.ShapeDtypeStruct((T, D), BF16), jax.ShapeDtypeStruct((T, 3 * D), BF16),
                   jax.ShapeDtypeStruct((3, D, T), BF16), jax.ShapeDtypeStruct((3, T, D), BF16),
                   jax.ShapeDtypeStruct((8, 128), F32), jax.ShapeDtypeStruct((8, D), F32)],
        scratch_shapes=[pltpu.VMEM((3, D, D), BF16), pltpu.VMEM((tm, D), F32), pltpu.VMEM((tm, D), F32),
                        pltpu.SemaphoreType.DMA],
        compiler_params=_params(("arbitrary",)),
    )(x, tgt, oh, oa, proj, proj, proj, proj, proj, proj, proj, hnw, fnw, wsq_bf)


def _wgrad_square(lhs_t, rhs):
    T = rhs.shape[1]
    tk = min(1024, T)

    def body(a_ref, b_ref, g_ref):
        part = _dot(a_ref[...], b_ref[...])

        @pl.when(pl.program_id(1) == 0)
        def _():
            g_ref[...] = part

        @pl.when(pl.program_id(1) > 0)
        def _():
            g_ref[...] += part

    return pl.pallas_call(
        body, name="wgrad_square", grid=(3, T // tk),
        in_specs=[pl.BlockSpec((None, D, tk), lambda k, i: (k, 0, i)), pl.BlockSpec((None, tk, D), lambda k, i: (k, i, 0))],
        out_specs=pl.BlockSpec((None, D, D), lambda k, i: (k, 0, 0)),
        out_shape=jax.ShapeDtypeStruct((3, D, D), F32),
        compiler_params=_params(("parallel", "arbitrary")),
    )(lhs_t, rhs)


def _bwd_dx(pieces, wt_bf, x, norm_w, dx2, swin_b, ssq_b):
    T = x.shape[0]
    tm = min(256, T)
    nt = T // tm
    widths = [p.shape[1] for p in pieces]
    n_p = len(pieces)

    def body(*refs):
        piece_refs = refs[:n_p]
        (w_hbm, x_ref, nw_ref, dx2_ref, swin_ref, ssq_ref,
         gx_ref, gnw_ref, win_got, sq_got, w_scr, sem, send_sems, recv_sems) = refs[n_p:]

        def scatter_copies():
            x_, y_, c_ = _place()
            copies = []
            for k, (fx, fy) in enumerate(CHIP_FLIPS):
                px, py = _flip(x_, fx), _flip(y_, fy)
                jr = 2 * px + py
                for a, (src, dst) in enumerate(((swin_ref.at[:, pl.ds(jr * SHARD_W, SHARD_W)], win_got.at[k]),
                                                (ssq_ref.at[:, pl.ds(jr * SQ_ROWS, SQ_ROWS), :], sq_got.at[k]))):
                    copies.append(pltpu.make_async_remote_copy(
                        src_ref=src, dst_ref=dst, send_sem=send_sems.at[2 * k + a], recv_sem=recv_sems.at[2 * k + a],
                        device_id=(px, py, c_), device_id_type=MESH))
            return copies

        @pl.when(pl.program_id(0) == 0)
        def _():
            for cp in scatter_copies():
                cp.start()
            cp = pltpu.make_async_copy(w_hbm, w_scr, sem)
            cp.start()
            cp.wait()
            gnw_ref[...] = jnp.zeros_like(gnw_ref)

        dxn = None
        off = 0
        for ref, w in zip(piece_refs, widths):
            part = _dot(ref[...], w_scr[off:off + w, :])
            dxn = part if dxn is None else dxn + part
            off += w
        xf = x_ref[...]
        rs = lax.rsqrt(jnp.mean(xf * xf, axis=1, keepdims=True) + EPS)
        xh = xf * rs
        gnw_ref[...] += jnp.sum(dxn * xh, axis=0, keepdims=True)
        gx = dxn * nw_ref[...]
        gx_ref[...] = rs * (gx - xh * jnp.mean(gx * xh, axis=1, keepdims=True)) + dx2_ref[...]

        @pl.when(pl.program_id(0) == nt - 1)
        def _():
            for cp in scatter_copies():
                cp.wait()

    row = lambda w: pl.BlockSpec((tm, w), lambda i: (i, 0))
    return pl.pallas_call(
        body, name="bwd_dx", grid=(nt,),
        in_specs=[row(w) for w in widths] + [HBM_SPEC, row(D), pl.BlockSpec((1, D), lambda i: (0, 0)), row(D), HBM_SPEC, HBM_SPEC],
        out_specs=[row(D), pl.BlockSpec((1, D), lambda i: (0, 0)), HBM_SPEC, HBM_SPEC],
        out_shape=[jax.ShapeDtypeStruct((T, D), F32), jax.ShapeDtypeStruct((1, D), F32),
                   jax.ShapeDtypeStruct((3, D // 2, SHARD_W), BF16), jax.ShapeDtypeStruct((3, 3, SQ_ROWS, D // 2), BF16)],
        scratch_shapes=[pltpu.VMEM((D_IN, D), BF16), pltpu.SemaphoreType.DMA,
                        pltpu.SemaphoreType.DMA((6,)), pltpu.SemaphoreType.DMA((6,))],
        compiler_params=_params(("arbitrary",)),
    )(*pieces, wt_bf, x, norm_w, dx2, swin_b, ssq_b)


W_PIECES = ((0, 1024, 3), (COL_HG, 1024, 1), (COL_AQ, 1024, 1), (COL_AK, 256, 1), (COL_AV, 256, 1), (COL_AG, 512, 6))


def _wgrad_in(xnt_bf, pieces):
    T = xnt_bf.shape[1]
    tk = min(1024, T)
    buf = None
    for n, (piece, (col, wb, blocks)) in enumerate(zip(pieces, W_PIECES)):
        first = buf is None

        def body(xnt_ref, p_ref, *rest):
            g_ref = rest[-1]
            part = _dot(xnt_ref[...], p_ref[...])

            @pl.when(pl.program_id(1) == 0)
            def _():
                g_ref[...] = part

            @pl.when(pl.program_id(1) > 0)
            def _():
                g_ref[...] += part

        call = pl.pallas_call(
            body, name=f"wgrad_in_{n}", grid=(blocks, T // tk),
            in_specs=[pl.BlockSpec((D, tk), lambda jb, i: (0, i)), pl.BlockSpec((tk, wb), lambda jb, i: (i, jb))]
                     + ([] if first else [HBM_SPEC]),
            out_specs=pl.BlockSpec((D, wb), lambda jb, i, base=col // wb: (0, base + jb)),
            out_shape=jax.ShapeDtypeStruct((D, D_IN), F32),
            input_output_aliases={} if first else {2: 0},
            compiler_params=_params(("parallel", "arbitrary")),
        )
        buf = call(xnt_bf, piece) if first else call(xnt_bf, piece, buf)
    return buf


def _place():
    return lax.axis_index("x"), lax.axis_index("y"), lax.axis_index("c")


def _flip(v, f):
    return 1 - v if f else v


def _win_half(ref, h):
    return ref.at[pl.ds(h * (D // 2), D // 2), :]


def _sq_half(ref, h):
    return ref.at[:, pl.ds(h * (D // 2), D // 2)]


def _gather_copy(part, k, to, send_sems, recv_sems):
    return pltpu.make_async_remote_copy(src_ref=part, dst_ref=part, send_sem=send_sems.at[k], recv_sem=recv_sems.at[k],
                                        device_id=to, device_id_type=MESH)


def _gather_start(out, half, send_sems, recv_sems):
    x, y, c = _place()
    for k, (fx, fy) in enumerate(CHIP_FLIPS):
        _gather_copy(half(out.at[2 * x + y], c), k, (_flip(x, fx), _flip(y, fy), c), send_sems, recv_sems).start()


def _gather_finish(out, half, send_sems, recv_sems):
    x, y, c = _place()
    sib = (x, y, 1 - c)
    slots = [2 * _flip(x, fx) + _flip(y, fy) for fx, fy in CHIP_FLIPS]
    passed = []
    for k, jr in enumerate(slots):
        landed = half(out.at[jr], c)
        _gather_copy(landed, k, sib, send_sems, recv_sems).wait_recv()
        cp = _gather_copy(landed, 3 + k, sib, send_sems, recv_sems)
        cp.start()
        passed.append(cp)
    for k, jr in enumerate(slots):
        _gather_copy(half(out.at[jr], 1 - c), 3 + k, sib, send_sems, recv_sems).wait_recv()
    for k, (fx, fy) in enumerate(CHIP_FLIPS):
        _gather_copy(half(out.at[2 * x + y], c), k, (_flip(x, fx), _flip(y, fy), c), send_sems, recv_sems).wait_send()
    for cp in passed:
        cp.wait_send()


def _allgather_w_in(win_all):
    def body(win_in, win_out, send_sems, recv_sems):
        del win_in
        _gather_start(win_out, _win_half, send_sems, recv_sems)
        _gather_finish(win_out, _win_half, send_sems, recv_sems)

    return pl.pallas_call(
        body, name="allgather_w_in", in_specs=[HBM_SPEC], out_specs=HBM_SPEC, input_output_aliases={0: 0},
        out_shape=jax.ShapeDtypeStruct((SHARDS, D, SHARD_W), BF16),
        scratch_shapes=[pltpu.SemaphoreType.DMA((6,)), pltpu.SemaphoreType.DMA((6,))],
    )(win_all)


def _swap_halves(gwin, gsq):
    def body(gwin_ref, gsq_ref, win_got, sq_got, send_sems, recv_sems):
        x, y, c = _place()
        sib = (x, y, 1 - c)
        pairs = ((_win_half(gwin_ref, 1 - c), win_got),
                 (gsq_ref.at[:, :, pl.ds((1 - c) * (D // 2), D // 2)], sq_got))
        copies = [pltpu.make_async_remote_copy(src_ref=src, dst_ref=dst, send_sem=send_sems.at[a], recv_sem=recv_sems.at[a],
                                               device_id=sib, device_id_type=MESH) for a, (src, dst) in enumerate(pairs)]
        for cp in copies:
            cp.start()
        for cp in copies:
            cp.wait()

    return pl.pallas_call(
        body, name="swap_halves",
        in_specs=[HBM_SPEC, HBM_SPEC], out_specs=[HBM_SPEC, HBM_SPEC],
        out_shape=[jax.ShapeDtypeStruct((D // 2, D_IN), F32), jax.ShapeDtypeStruct((3, D, D // 2), F32)],
        scratch_shapes=[pltpu.SemaphoreType.DMA((2,)), pltpu.SemaphoreType.DMA((2,))],
    )(gwin, gsq)


def _add_halves(c_arr, gwin, gsq, win_got, sq_got):
    def body(c_ref, a_ref, b_ref, p_ref, q_ref, so_ref, sq_ref, sob_ref, sqb_ref):
        so = a_ref[...] + b_ref[...]
        sq = p_ref[...] + q_ref[...]
        so_ref[...] = so
        sq_ref[...] = sq
        sob_ref[...] = _bf(so)
        sqb_ref[...] = _bf(sq)

    steps = 8
    rows, sq_rows = (D // 2) // steps, D // steps
    win = lambda f: pl.BlockSpec((rows, D_IN), f)
    sq = lambda f: pl.BlockSpec((3, sq_rows, D // 2), f)
    return pl.pallas_call(
        body, name="add_halves",
        grid_spec=pltpu.PrefetchScalarGridSpec(
            num_scalar_prefetch=1, grid=(steps,),
            in_specs=[win(lambda i, c: (c[0] * steps + i, 0)), win(lambda i, c: (i, 0)),
                      sq(lambda i, c: (0, i, c[0])), sq(lambda i, c: (0, i, 0))],
            out_specs=[win(lambda i, c: (i, 0)), sq(lambda i, c: (0, i, 0))] * 2),
        out_shape=[jax.ShapeDtypeStruct((D // 2, D_IN), F32), jax.ShapeDtypeStruct((3, D, D // 2), F32),
                   jax.ShapeDtypeStruct((D // 2, D_IN), BF16), jax.ShapeDtypeStruct((3, D, D // 2), BF16)],
        compiler_params=_params(("arbitrary",)),
    )(c_arr, gwin, win_got, gsq, sq_got)


def _sum_chips(jc_arr, swin, ssq, win_got, sq_got):
    def body(jc_ref, a_ref, b_ref, p_ref, q_ref, so_ref, sq_ref):
        so_ref[...] = ((a_ref[...] + b_ref[0].astype(F32)) + b_ref[1].astype(F32)) + b_ref[2].astype(F32)
        sq_ref[...] = ((p_ref[...] + q_ref[0].astype(F32)) + q_ref[1].astype(F32)) + q_ref[2].astype(F32)

    rows = 128
    steps = (D // 2) // rows
    sq_rows = SQ_ROWS // steps
    return pl.pallas_call(
        body, name="sum_chips",
        grid_spec=pltpu.PrefetchScalarGridSpec(
            num_scalar_prefetch=1, grid=(steps,),
            in_specs=[pl.BlockSpec((rows, SHARD_W), lambda i, jc: (i, jc[0])),
                      pl.BlockSpec((3, rows, SHARD_W), lambda i, jc: (0, i, 0)),
                      pl.BlockSpec((3, sq_rows, D // 2), lambda i, jc: (0, jc[0] * steps + i, 0)),
                      pl.BlockSpec((3, 3, sq_rows, D // 2), lambda i, jc: (0, 0, i, 0))],
            out_specs=[pl.BlockSpec((rows, SHARD_W), lambda i, jc: (jc[1] * steps + i, 0)),
                       pl.BlockSpec((3, sq_rows, D // 2), lambda i, jc: (0, i, jc[1]))]),
        out_shape=[jax.ShapeDtypeStruct((D, SHARD_W), F32), jax.ShapeDtypeStruct((3, SQ_ROWS, D), F32)],
        compiler_params=_params(("arbitrary",)),
    )(jc_arr, swin, win_got, ssq, sq_got)


def _join_halves(g_win, g_sq):
    def body(win_in, sq_in, win_out, sq_out, send_sems, recv_sems):
        del win_in, sq_in
        x, y, c = _place()
        sib = (x, y, 1 - c)

        def halves(h):
            return _win_half(win_out, h), sq_out.at[:, :, pl.ds(h * (D // 2), D // 2)]

        def copy(a, part):
            return pltpu.make_async_remote_copy(src_ref=part, dst_ref=part, send_sem=send_sems.at[a], recv_sem=recv_sems.at[a],
                                                device_id=sib, device_id_type=MESH)

        sent = [copy(a, part) for a, part in enumerate(halves(c))]
        for cp in sent:
            cp.start()
        for a, part in enumerate(halves(1 - c)):
            copy(a, part).wait_recv()
        for cp in sent:
            cp.wait_send()

    return pl.pallas_call(
        body, name="join_halves",
        in_specs=[HBM_SPEC, HBM_SPEC], out_specs=[HBM_SPEC, HBM_SPEC], input_output_aliases={0: 0, 1: 1},
        out_shape=[jax.ShapeDtypeStruct((D, SHARD_W), F32), jax.ShapeDtypeStruct((3, SQ_ROWS, D), F32)],
        scratch_shapes=[pltpu.SemaphoreType.DMA((2,)), pltpu.SemaphoreType.DMA((2,))],
    )(g_win, g_sq)


def _allreduce_small(vec):
    def body(vec_ref, out_ref, slots, send_sems, recv_sems):
        x, y, c = _place()
        me = 4 * x + 2 * y + c
        slots[me] = vec_ref[...]
        copies = []
        for k in range(1, 8):
            fx, fy, fc = (k >> 2) & 1, (k >> 1) & 1, k & 1
            copies.append(pltpu.make_async_remote_copy(
                src_ref=vec_ref, dst_ref=slots.at[me], send_sem=send_sems.at[k - 1], recv_sem=recv_sems.at[k - 1],
                device_id=(_flip(x, fx), _flip(y, fy), _flip(c, fc)), device_id_type=MESH))
        for cp in copies:
            cp.start()
        for k in range(1, 8):
            fx, fy, fc = (k >> 2) & 1, (k >> 1) & 1, k & 1
            src = 4 * _flip(x, fx) + 2 * _flip(y, fy) + _flip(c, fc)
            pltpu.make_async_remote_copy(src_ref=vec_ref, dst_ref=slots.at[src], send_sem=send_sems.at[k - 1],
                                         recv_sem=recv_sems.at[k - 1], device_id=(x, y, c), device_id_type=MESH).wait_recv()
        for cp in copies:
            cp.wait_send()
        total = slots[0]
        for s in range(1, 8):
            total = total + slots[s]
        out_ref[...] = total

    return pl.pallas_call(
        body, name="allreduce_small",
        in_specs=[pl.BlockSpec(memory_space=pltpu.VMEM)], out_specs=pl.BlockSpec(memory_space=pltpu.VMEM),
        out_shape=jax.ShapeDtypeStruct((8, D), F32),
        scratch_shapes=[pltpu.VMEM((8, 8, D), F32), pltpu.SemaphoreType.DMA((7,)), pltpu.SemaphoreType.DMA((7,))],
    )(vec)


def _adamw_math(w, g, m, v):
    m = ADAM_B1 * m + (1.0 - ADAM_B1) * g
    v = ADAM_B2 * v + (1.0 - ADAM_B2) * (g * g)
    m_hat = m / (1.0 - ADAM_B1 ** ADAM_STEP)
    v_hat = v / (1.0 - ADAM_B2 ** ADAM_STEP)
    delta = -ADAM_LR * (m_hat / (jnp.sqrt(v_hat) + ADAM_EPS) + ADAM_WD * w)
    return delta, m, v


def _adamw(name, w, g, m, v, rows):
    R, C = w.shape

    def body(w_ref, g_ref, m_ref, v_ref, d_out, m_out, v_out):
        d_out[...], m_out[...], v_out[...] = _adamw_math(w_ref[...], g_ref[...], m_ref[...], v_ref[...])

    spec = pl.BlockSpec((rows, C), lambda i: (i, 0))
    return pl.pallas_call(
        body, name=name, grid=(R // rows,), in_specs=[spec] * 4, out_specs=[spec] * 3,
        out_shape=[jax.ShapeDtypeStruct((R, C), F32)] * 3,
        compiler_params=_params(("parallel",)),
    )(w, g, m, v)


def _small_update(total, lbw, w8, m8, v8):
    def body(t_ref, lbw_ref, w_ref, m_ref, v_ref, g_out, d_out, m_out, v_out):
        lb = 1.0 / (1.0 + jnp.exp(lbw_ref[1:2, :] - lbw_ref[0:1, :]))
        dlb = t_ref[2:3, :] * lb * (1.0 - lb)
        g_out[...] = jnp.zeros_like(g_out)
        g_out[0:1, :] = t_ref[3:4, :]
        g_out[1:2, :] = dlb
        g_out[2:3, :] = -dlb
        g_out[3:4, :] = t_ref[1:2, :]
        g_out[4:5, :] = t_ref[0:1, :]
        g_out[5:6, :] = t_ref[4:5, :]
        d_out[...], m_out[...], v_out[...] = _adamw_math(w_ref[...], g_out[...], m_ref[...], v_ref[...])

    return pl.pallas_call(
        body, name="small_update", out_shape=[jax.ShapeDtypeStruct((8, D), F32)] * 4,
        compiler_params=_params(),
    )(total, lbw, w8, m8, v8)


def _pack8(norm_w, lbw, hnw, fnw, sinks):
    pad = jnp.zeros((1, D - 16), F32)
    return jnp.concatenate([norm_w, lbw, hnw, fnw.reshape(1, D), jnp.concatenate([sinks, pad], axis=1),
                            jnp.zeros((2, D), F32)], axis=0)


def _unpack8(a):
    return a[0:1], a[1:3], a[3:4], a[5:6, 0:16], a[4]


def _local_step(x, tgt, norm_w, lbw, hnw, sinks, fnw, win_bf, wsq_mine, exchange):
    proj, xnt_bf = _fwd_proj(x, norm_w, win_bf)
    oh, states, wsq_all = _hgrn_fwd(proj, lbw, wsq_mine)
    wsq_bf = wsq_all.reshape(SHARDS, 3, SQ_ROWS, D).transpose(1, 0, 2, 3).reshape(3, D, D)
    oa = _attn_fwd(proj, sinks)
    dx2, doh, doa, dhg, dtail, lhs, rhs, loss8, vec_mid = _mid(x, tgt, proj, oh, oa, hnw, fnw.reshape(1, D), wsq_bf)
    gsq = _wgrad_square(lhs, rhs)
    dhead, dlb = _hgrn_bwd(proj, lbw, states, doh)
    daq, dak, dav, dsink = _attn_bwd(proj, sinks, oa, doa)
    pieces = [dhead, dhg, daq, dak, dav, dtail]
    sums = exchange(_wgrad_in(xnt_bf, pieces), gsq)
    wt_bf = win_bf.transpose(0, 2, 1).reshape(D_IN, D)
    grad_x, gnw, win_got, sq_got = _bwd_dx(pieces, wt_bf, x, norm_w, dx2, sums[2], sums[3])
    sink_row = jnp.concatenate([dsink[:, 0].reshape(1, 16), jnp.zeros((1, D - 16), F32)], axis=1)
    vec = jnp.concatenate([vec_mid[0:2], dlb, gnw, sink_row, jnp.zeros((3, D), F32)], axis=0)
    return loss8[0, 0], grad_x, sums, (win_got, sq_got), vec


def kernel(x, norm_w, w_in, hgrn_lower_bound, hgrn_norm_w, w_branch_hgrn, attn_sinks, w_branch_attn, w_out, final_norm_w, loss_target, m_norm_w, m_w_in, m_hgrn_lower_bound, m_hgrn_norm_w, m_w_branch_hgrn, m_attn_sinks, m_w_branch_attn, m_w_out, m_final_norm_w, v_norm_w, v_w_in, v_hgrn_lower_bound, v_hgrn_norm_w, v_w_branch_hgrn, v_attn_sinks, v_w_branch_attn, v_w_out, v_final_norm_w):
    c_arr = lax.axis_index("c").astype(jnp.int32).reshape(1)
    j_arr = (2 * lax.axis_index("x") + lax.axis_index("y")).astype(jnp.int32).reshape(1)
    jc_arr = jnp.concatenate([j_arr, c_arr])

    win_mine, wsq_mine = _cast_shards(j_arr, w_in[0], w_branch_hgrn[0], w_branch_attn[0], w_out[0])
    win_bf = _allgather_w_in(win_mine)

    def chip_sums(gwin, gsq):
        return _add_halves(c_arr, gwin, gsq, *_swap_halves(gwin, gsq))

    loss_part, grad_x, (swin, ssq, _, _), arrived, vec = _local_step(
        x[0], loss_target[0], norm_w, hgrn_lower_bound, hgrn_norm_w, attn_sinks, final_norm_w, win_bf, wsq_mine, chip_sums)
    loss = lax.psum(loss_part, ("x", "y", "c"))
    g_win, g_sq = _join_halves(*_sum_chips(jc_arr, swin, ssq, *arrived))

    d_win, nm_win, nv_win = _adamw("adamw_w_in", w_in[0], g_win, m_w_in[0], v_w_in[0], 128)
    sq_w = jnp.concatenate([w_branch_hgrn[0], w_branch_attn[0], w_out[0]], axis=0)
    sq_m = jnp.concatenate([m_w_branch_hgrn[0], m_w_branch_attn[0], m_w_out[0]], axis=0)
    sq_v = jnp.concatenate([v_w_branch_hgrn[0], v_w_branch_attn[0], v_w_out[0]], axis=0)
    d_sq, nm_sq, nv_sq = _adamw("adamw_square", sq_w, g_sq.reshape(3 * SQ_ROWS, D), sq_m, sq_v, 256)

    total = _allreduce_small(vec)
    g8, d8, nm8, nv8 = _small_update(
        total, hgrn_lower_bound,
        _pack8(norm_w, hgrn_lower_bound, hgrn_norm_w, final_norm_w, attn_sinks),
        _pack8(m_norm_w, m_hgrn_lower_bound, m_hgrn_norm_w, m_final_norm_w, m_attn_sinks),
        _pack8(v_norm_w, v_hgrn_lower_bound, v_hgrn_norm_w, v_final_norm_w, v_attn_sinks))

    def assemble(win, sq, small):
        nw, lb, hn, sk, fn = _unpack8(small)
        sq = sq.reshape(3, 1, SQ_ROWS, D)
        return (nw, win.reshape(1, D, SHARD_W), lb, hn, sq[0], sk, sq[1], sq[2], fn)

    return (loss, grad_x.reshape(1, -1, D),
            *assemble(g_win, g_sq, g8), *assemble(d_win, d_sq, d8),
            *assemble(nm_win, nm_sq, nm8), *assemble(nv_win, nv_sq, nv8))
```

```python
import functools

import jax
import jax.numpy as jnp
from jax import lax
from jax.experimental import pallas as pl
from jax.experimental.pallas import tpu as pltpu

F32 = jnp.float32
BF16 = jnp.bfloat16

D = 1024
D_IN = 8704
SHARDS = 4
SHARD_W = D_IN // SHARDS
SQ_ROWS = D // SHARDS
HEADS = 8
HEAD_W = 128
CHUNK = 64
SUB = 2
ATT_BLOCK = 128
KV_HEADS = 4
HEAD_DIM = 64
EPS = 1e-6
NEG = -1e30
SCALE = HEAD_DIM ** -0.5
COL_HG, COL_AQ, COL_AK, COL_AV, COL_AG, COL_MH, COL_MA = 3072, 4096, 5120, 5376, 5632, 6656, 7680

ADAM_LR, ADAM_B1, ADAM_B2, ADAM_EPS, ADAM_WD, ADAM_STEP = 0.001, 0.9, 0.999, 1e-08, 0.01, 10

VMEM_LIMIT = 56 * 1024 * 1024
MESH = pl.DeviceIdType.MESH
HBM_SPEC = pl.BlockSpec(memory_space=pltpu.HBM)
CHIP_FLIPS = ((1, 0), (0, 1), (1, 1))


def _dot(a, b):
    return jnp.dot(a, b, preferred_element_type=F32)


def _dot_nt(a, b):
    return lax.dot_general(a, b, (((1,), (1,)), ((), ())), preferred_element_type=F32)


def _dot_tn(a, b):
    return lax.dot_general(a, b, (((0,), (0,)), ((), ())), preferred_element_type=F32)


def _sigmoid(v):
    return 1.0 / (1.0 + jnp.exp(-v))


def _bf(v):
    return v.astype(BF16)


def _split3(v):
    a = _bf(v)
    r = v - a.astype(F32)
    b = _bf(r)
    c = _bf(r - b.astype(F32))
    return a, b, c


def _tri_dot(tri, v):
    a, b, c = _split3(v)
    return _dot(tri, a) + _dot(tri, b) + _dot(tri, c)


def _params(sem=None):
    return pltpu.CompilerParams(dimension_semantics=sem, vmem_limit_bytes=VMEM_LIMIT)


def _cast_shards(j_arr, win_s, wbh_s, wba_s, wout_s):
    steps = 4
    rows = D // steps

    def body(j_ref, win_ref, a_ref, b_ref, c_ref, win_o, sq_o):
        win_o[...] = _bf(win_ref[...])

        @pl.when(pl.program_id(0) == 0)
        def _():
            sq_o[0:SQ_ROWS, :] = _bf(a_ref[...])
            sq_o[SQ_ROWS:2 * SQ_ROWS, :] = _bf(b_ref[...])
            sq_o[2 * SQ_ROWS:3 * SQ_ROWS, :] = _bf(c_ref[...])

    whole = pl.BlockSpec((SQ_ROWS, D), lambda i, j: (0, 0))
    return pl.pallas_call(
        body, name="cast_shards",
        grid_spec=pltpu.PrefetchScalarGridSpec(
            num_scalar_prefetch=1, grid=(steps,),
            in_specs=[pl.BlockSpec((rows, SHARD_W), lambda i, j: (i, 0)), whole, whole, whole],
            out_specs=[pl.BlockSpec((None, rows, SHARD_W), lambda i, j: (j[0], i, 0)),
                       pl.BlockSpec((None, 3 * SQ_ROWS, D), lambda i, j: (j[0], 0, 0))]),
        out_shape=[jax.ShapeDtypeStruct((SHARDS, D, SHARD_W), BF16), jax.ShapeDtypeStruct((SHARDS, 3 * SQ_ROWS, D), BF16)],
        compiler_params=_params(("arbitrary",)),
    )(j_arr, win_s, wbh_s, wba_s, wout_s)


def _fwd_proj(x, norm_w, win_bf):
    T = x.shape[0]
    tm = min(256, T)

    def body(x_ref, nw_ref, w_hbm, proj_ref, xn_ref, w_scr, sem):
        @pl.when(pl.program_id(0) == 0)
        def _():
            cp = pltpu.make_async_copy(w_hbm, w_scr, sem)
            cp.start()
            cp.wait()

        xf = x_ref[...]
        rs = lax.rsqrt(jnp.mean(xf * xf, axis=1, keepdims=True) + EPS)
        xn = _bf((xf * rs) * nw_ref[...])
        xn_ref[...] = xn.T
        for j in range(SHARDS):
            proj_ref[:, j * SHARD_W:(j + 1) * SHARD_W] = _dot(xn, w_scr[j])

    return pl.pallas_call(
        body, name="fwd_proj", grid=(T // tm,),
        in_specs=[pl.BlockSpec((tm, D), lambda i: (i, 0)), pl.BlockSpec((1, D), lambda i: (0, 0)), HBM_SPEC],
        out_specs=[pl.BlockSpec((tm, D_IN), lambda i: (i, 0)), pl.BlockSpec((D, tm), lambda i: (0, i))],
        out_shape=[jax.ShapeDtypeStruct((T, D_IN), F32), jax.ShapeDtypeStruct((D, T), BF16)],
        scratch_shapes=[pltpu.VMEM((SHARDS, D, SHARD_W), BF16), pltpu.SemaphoreType.DMA],
        compiler_params=_params(("arbitrary",)),
    )(x, norm_w, win_bf)


def _hgrn_gates(hq_ref, hf_ref, lbw_ref, b_scr):
    lb = 1.0 / (1.0 + jnp.exp(lbw_ref[1:2, :] - lbw_ref[0:1, :]))
    hf = hf_ref[...]
    sig = _sigmoid(hf)
    f = lb + (1.0 - lb) * sig
    g = jnp.log(f)
    hq = hq_ref[...]
    sq = _sigmoid(hq)
    q = hq * sq
    row = lax.broadcasted_iota(jnp.int32, (CHUNK, CHUNK), 0)
    col = lax.broadcasted_iota(jnp.int32, (CHUNK, CHUNK), 1)
    causal = row >= col
    b = _tri_dot(jnp.where(causal, 1.0, 0.0).astype(BF16), g)
    b_scr[...] = b
    bc = b_scr[CHUNK - 1:CHUNK, :]
    r = b_scr[CHUNK // 2 - 1:CHUNK // 2, :]
    return dict(lb=lb, sig=sig, f=f, k=1.0 - f, hq=hq, sq=sq, q=q, b=b, bc=bc, r=r, causal=causal)


def _hgrn_fwd(proj, lbw, wsq_all):
    T = proj.shape[0]
    n = T // CHUNK

    def body(hq_ref, hf_ref, hi_ref, lbw_ref, wsq_in, o_ref, st_ref, wsq_out, s_scr, b_scr, send_sems, recv_sems):
        del wsq_in

        @pl.when(pl.program_id(0) == 0)
        def _():
            _gather_start(wsq_out, _sq_half, send_sems, recv_sems)
            s_scr[...] = jnp.zeros_like(s_scr)

        for c in range(SUB):
            rows = pl.ds(c * CHUNK, CHUNK)
            gt = _hgrn_gates(hq_ref.at[rows, :], hf_ref.at[rows, :], lbw_ref, b_scr.at[rows, :])
            b, bc, r, q, k = gt["b"], gt["bc"], gt["r"], gt["q"], gt["k"]
            qe = _bf(q * jnp.exp(b))
            qr = _bf(q * jnp.exp(b - r))
            kr = _bf(k * jnp.exp(r - b))
            kl = _bf(k * jnp.exp(bc - b))
            ebc = jnp.exp(bc)
            v = _bf(hi_ref[rows, :])
            scores = [_bf(jnp.where(gt["causal"], _dot_nt(qr[:, h * HEAD_W:(h + 1) * HEAD_W], kr[:, h * HEAD_W:(h + 1) * HEAD_W]), 0.0))
                      for h in range(HEADS)]
            for h in range(HEADS):
                sl = slice(h * HEAD_W, (h + 1) * HEAD_W)
                st = s_scr[h]
                st_ref[c, h] = st
                o_ref[rows, sl] = _dot(scores[h], v[:, sl]) + _dot_nt(qe[:, sl], _bf(st))
                s_scr[h] = ebc[:, sl] * st + _dot_tn(v[:, sl], kl[:, sl])

        @pl.when(pl.program_id(0) == n // SUB - 1)
        def _():
            _gather_finish(wsq_out, _sq_half, send_sems, recv_sems)

    col = lambda j: pl.BlockSpec((SUB * CHUNK, D), lambda i: (i, j))
    return pl.pallas_call(
        body, name="hgrn_fwd", grid=(n // SUB,),
        in_specs=[col(0), col(1), col(2), pl.BlockSpec((2, D), lambda i: (0, 0)), HBM_SPEC],
        out_specs=[pl.BlockSpec((SUB * CHUNK, D), lambda i: (i, 0)),
                   pl.BlockSpec((SUB, HEADS, HEAD_W, HEAD_W), lambda i: (i, 0, 0, 0)), HBM_SPEC],
        out_shape=[jax.ShapeDtypeStruct((T, D), F32), jax.ShapeDtypeStruct((n, HEADS, HEAD_W, HEAD_W), F32),
                   jax.ShapeDtypeStruct((SHARDS, 3 * SQ_ROWS, D), BF16)],
        input_output_aliases={4: 2},
        scratch_shapes=[pltpu.VMEM((HEADS, HEAD_W, HEAD_W), F32), pltpu.VMEM((SUB * CHUNK, D), F32),
                        pltpu.SemaphoreType.DMA((6,)), pltpu.SemaphoreType.DMA((6,))],
        compiler_params=_params(("arbitrary",)),
    )(proj, proj, proj, lbw, wsq_all)


def _hgrn_bwd(proj, lbw, states, do):
    T = proj.shape[0]
    n = T // CHUNK

    def body(hq_ref, hf_ref, hi_ref, lbw_ref, st_ref, do_ref, dp_ref, dlb_ref,
             ds_scr, b_scr, dq_scr, dk_scr, dv_scr, late_scr, early_scr, ex_scr):
        @pl.when(pl.program_id(0) == 0)
        def _():
            ds_scr[...] = jnp.zeros_like(ds_scr)
            dlb_ref[...] = jnp.zeros_like(dlb_ref)

        for c in reversed(range(SUB)):
            rows = pl.ds(c * CHUNK, CHUNK)
            gt = _hgrn_gates(hq_ref.at[rows, :], hf_ref.at[rows, :], lbw_ref, b_scr.at[rows, :])
            b, bc, r, q, k = gt["b"], gt["bc"], gt["r"], gt["q"], gt["k"]
            eb = jnp.exp(b)
            er = jnp.exp(b - r)
            erk = jnp.exp(r - b)
            el = jnp.exp(bc - b)
            ebc = jnp.exp(bc)
            qe, qr, kr, kl = _bf(q * eb), _bf(q * er), _bf(k * erk), _bf(k * el)
            v = _bf(hi_ref[rows, :])
            do_b = _bf(do_ref[rows, :])
            do_t = do_b.T
            causal_t = lax.broadcasted_iota(jnp.int32, (CHUNK, CHUNK), 0) <= lax.broadcasted_iota(jnp.int32, (CHUNK, CHUNK), 1)
            firsts = []
            for h in range(HEADS):
                sl = slice(h * HEAD_W, (h + 1) * HEAD_W)
                firsts.append((_bf(jnp.where(causal_t, _dot_nt(kr[:, sl], qr[:, sl]), 0.0)),
                               _bf(jnp.where(gt["causal"], _dot_nt(do_b[:, sl], v[:, sl]), 0.0)),
                               _bf(jnp.where(causal_t, _dot_nt(v[:, sl], do_b[:, sl]), 0.0))))
            for h in range(HEADS):
                sl = slice(h * HEAD_W, (h + 1) * HEAD_W)
                st0 = st_ref[c, h]
                dst = ds_scr[h]
                dst_b = _bf(dst)
                a_t, da, da_t = firsts[h]
                mq = _dot(da, kr[:, sl])
                mk = _dot(da_t, qr[:, sl])
                dq_in = eb[:, sl] * _dot(do_b[:, sl], _bf(st0))
                dk_in = el[:, sl] * _dot(v[:, sl], dst_b)
                dq_scr[rows, sl] = er[:, sl] * mq + dq_in
                dk_scr[rows, sl] = erk[:, sl] * mk + dk_in
                dv_scr[rows, sl] = _dot(a_t, do_b[:, sl]) + _dot_nt(kl[:, sl], dst_b)
                late_scr[rows, sl] = q[:, sl] * dq_in + qr[:, sl].astype(F32) * mq - kr[:, sl].astype(F32) * mk
                early_scr[rows, sl] = k[:, sl] * dk_in
                ex_scr[:, sl] = jnp.sum(dst * st0, axis=0, keepdims=True)
                ds_scr[h] = ebc[:, sl] * dst + _dot(do_t[sl, :], qe[:, sl])

            dq, dk = dq_scr[rows, :], dk_scr[rows, :]
            row = lax.broadcasted_iota(jnp.int32, (CHUNK, CHUNK), 0)
            col = lax.broadcasted_iota(jnp.int32, (CHUNK, CHUNK), 1)
            at_or_after = jnp.where(col >= row, 1.0, 0.0).astype(BF16)
            before = jnp.where(col < row, 1.0, 0.0).astype(BF16)
            dg = _tri_dot(at_or_after, late_scr[rows, :]) + _tri_dot(before, early_scr[rows, :]) + ebc * ex_scr[...]
            df = dg / gt["f"] - dk
            sig, sq, hq, lb = gt["sig"], gt["sq"], gt["hq"], gt["lb"]
            dp_ref[rows, 0:D] = _bf(dq * (sq * (1.0 + hq * (1.0 - sq))))
            dp_ref[rows, D:2 * D] = _bf(df * (1.0 - lb) * sig * (1.0 - sig))
            dp_ref[rows, 2 * D:3 * D] = _bf(dv_scr[rows, :])
            dlb_ref[...] += jnp.sum(df * (1.0 - sig), axis=0, keepdims=True)

    ns = n // SUB
    col = lambda j: pl.BlockSpec((SUB * CHUNK, D), lambda i: (ns - 1 - i, j))
    return pl.pallas_call(
        body, name="hgrn_bwd", grid=(ns,),
        in_specs=[col(0), col(1), col(2), pl.BlockSpec((2, D), lambda i: (0, 0)),
                  pl.BlockSpec((SUB, HEADS, HEAD_W, HEAD_W), lambda i: (ns - 1 - i, 0, 0, 0)),
                  pl.BlockSpec((SUB * CHUNK, D), lambda i: (ns - 1 - i, 0))],
        out_specs=[pl.BlockSpec((SUB * CHUNK, 3 * D), lambda i: (ns - 1 - i, 0)),
                   pl.BlockSpec((1, D), lambda i: (0, 0))],
        out_shape=[jax.ShapeDtypeStruct((T, 3 * D), BF16), jax.ShapeDtypeStruct((1, D), F32)],
        scratch_shapes=[pltpu.VMEM((HEADS, HEAD_W, HEAD_W), F32)] + [pltpu.VMEM((SUB * CHUNK, D), F32)] * 6
                       + [pltpu.VMEM((1, D), F32)],
        compiler_params=_params(("arbitrary",)),
    )(proj, proj, proj, lbw, states, do)


def _attn_masks(blk):
    qi = lax.broadcasted_iota(jnp.int32, (ATT_BLOCK, 2 * ATT_BLOCK), 0)
    kj = lax.broadcasted_iota(jnp.int32, (ATT_BLOCK, 2 * ATT_BLOCK), 1)
    band = (kj > qi) & (kj <= qi + ATT_BLOCK)
    return band & ((blk > 0) | (kj >= ATT_BLOCK))


def _head_pair_operand(t, hp, low):
    mine = low if hp == 0 else jnp.logical_not(low)
    both = jnp.where(mine, t, pltpu.roll(t, HEAD_DIM, 1))
    return _bf(jnp.concatenate([jnp.where(low, both, 0.0), jnp.where(low, 0.0, both)], axis=0))


def _attn_probs(s, sink, valid):
    s = jnp.where(valid, s * SCALE, NEG)
    m = jnp.maximum(jnp.max(s, axis=1, keepdims=True), sink)
    p = jnp.exp(s - m)
    es = jnp.exp(sink - m)
    inv = 1.0 / (jnp.sum(p, axis=1, keepdims=True) + es)
    return p * inv, es * inv


def _attn_fwd(proj, sinks):
    T = proj.shape[0]
    nb = T // ATT_BLOCK
    W2 = 2 * ATT_BLOCK

    def body(sink_ref, q_ref, kp_ref, kc_ref, vp_ref, vc_ref, o_ref):
        blk = pl.program_id(0)
        valid = _attn_masks(blk)
        low = lax.broadcasted_iota(jnp.int32, (1, 2 * HEAD_DIM), 1) < HEAD_DIM
        kcat = jnp.concatenate([kp_ref[...], kc_ref[...]], axis=0)
        vcat = jnp.concatenate([vp_ref[...], vc_ref[...]], axis=0)
        for h in range(KV_HEADS):
            tl = slice((h // 2) * 128, (h // 2) * 128 + 128)
            mine = low if h % 2 == 0 else jnp.logical_not(low)
            kh = _bf(jnp.where(mine, kcat[:, tl], pltpu.roll(kcat[:, tl], HEAD_DIM, 1)))
            vh = _bf(jnp.where(mine, vcat[:, tl], pltpu.roll(vcat[:, tl], HEAD_DIM, 1)))
            for t in range(2):
                ql = slice((2 * h + t) * 128, (2 * h + t) * 128 + 128)
                q2 = q_ref[:, ql]
                outs = []
                for p in range(2):
                    qm = _bf(jnp.where(low if p == 0 else jnp.logical_not(low), q2, 0.0))
                    probs, _ = _attn_probs(_dot_nt(qm, kh), sink_ref[0, 4 * h + 2 * t + p], valid)
                    outs.append(_dot(_bf(probs), vh))
                o_ref[:, ql] = jnp.where(low, outs[0], outs[1])

    prev = lambda i: jnp.maximum(i - 1, 0)
    return pl.pallas_call(
        body, name="attn_fwd", grid=(nb,),
        in_specs=[pl.BlockSpec(memory_space=pltpu.SMEM),
                  pl.BlockSpec((ATT_BLOCK, D), lambda i: (i, COL_AQ // D)),
                  pl.BlockSpec((ATT_BLOCK, 256), lambda i: (prev(i), COL_AK // 256)),
                  pl.BlockSpec((ATT_BLOCK, 256), lambda i: (i, COL_AK // 256)),
                  pl.BlockSpec((ATT_BLOCK, 256), lambda i: (prev(i), COL_AV // 256)),
                  pl.BlockSpec((ATT_BLOCK, 256), lambda i: (i, COL_AV // 256))],
        out_specs=pl.BlockSpec((ATT_BLOCK, D), lambda i: (i, 0)),
        out_shape=jax.ShapeDtypeStruct((T, D), F32),
        compiler_params=_params(("arbitrary",)),
    )(sinks, proj, proj, proj, proj, proj)


def _attn_bwd(proj, sinks, o, do):
    T = proj.shape[0]
    nb = T // ATT_BLOCK
    W2 = 2 * ATT_BLOCK

    def body(sink_ref, q_ref, kp_ref, kc_ref, vp_ref, vc_ref, o_ref, do_ref,
             dq_ref, dk_ref, dv_ref, dsink_ref, ck_scr, cv_scr, nk_scr, nv_scr):
        blk = pl.program_id(0)

        @pl.when(blk == 0)
        def _():
            ck_scr[...] = jnp.zeros_like(ck_scr)
            cv_scr[...] = jnp.zeros_like(cv_scr)
            dsink_ref[...] = jnp.zeros_like(dsink_ref)

        @pl.when(blk < nb)
        def _():
            valid = _attn_masks(blk)
            low = lax.broadcasted_iota(jnp.int32, (1, 2 * HEAD_DIM), 1) < HEAD_DIM
            kcat = jnp.concatenate([kp_ref[...], kc_ref[...]], axis=0)
            vcat = jnp.concatenate([vp_ref[...], vc_ref[...]], axis=0)
            for h in range(KV_HEADS):
                tl = slice((h // 2) * 128, (h // 2) * 128 + 128)
                kbd = _head_pair_operand(kcat[:, tl], h % 2, low)
                vbd = _head_pair_operand(vcat[:, tl], h % 2, low)
                dkbd = jnp.zeros((2 * W2, 128), F32)
                dvbd = jnp.zeros((2 * W2, 128), F32)
                tiles = []
                for t in range(2):
                    ql = slice((2 * h + t) * 128, (2 * h + t) * 128 + 128)
                    q2 = _bf(q_ref[:, ql])
                    do2 = do_ref[:, ql]
                    do2_b = _bf(do2)
                    doo = do2 * o_ref[:, ql]
                    dsum0 = jnp.sum(jnp.where(low, doo, 0.0), axis=1, keepdims=True)
                    dsum1 = jnp.sum(jnp.where(low, 0.0, doo), axis=1, keepdims=True)
                    tiles.append((ql, q2, do2_b, dsum0, dsum1, _dot_nt(q2, kbd), _dot_nt(do2_b, vbd)))
                grads = []
                for t, (ql, q2, do2_b, dsum0, dsum1, s2, dp2) in enumerate(tiles):
                    head = 4 * h + 2 * t
                    p0, ps0 = _attn_probs(s2[:, 0:W2], sink_ref[0, head], valid)
                    p1, ps1 = _attn_probs(s2[:, W2:2 * W2], sink_ref[0, head + 1], valid)
                    ds2 = _bf(jnp.concatenate([p0 * (dp2[:, 0:W2] - dsum0), p1 * (dp2[:, W2:2 * W2] - dsum1)], axis=1) * SCALE)
                    grads.append((ds2, _bf(jnp.concatenate([p0, p1], axis=1))))
                    dsink_ref[head:head + 1, :] += jnp.zeros((1, 128), F32) - jnp.sum(ps0 * dsum0, axis=0, keepdims=True)
                    dsink_ref[head + 1:head + 2, :] += jnp.zeros((1, 128), F32) - jnp.sum(ps1 * dsum1, axis=0, keepdims=True)
                for (ql, q2, do2_b, _, _, _, _), (ds2, p2) in zip(tiles, grads):
                    dq_ref[:, ql] = _bf(_dot(ds2, kbd))
                    dkbd = dkbd + _dot_tn(ds2, q2)
                    dvbd = dvbd + _dot_tn(p2, do2_b)
                dk2 = jnp.where(low, dkbd[0:W2], dkbd[W2:2 * W2])
                dv2 = jnp.where(low, dvbd[0:W2], dvbd[W2:2 * W2])
                dk2 = dk2 + pltpu.roll(dk2, HEAD_DIM, 1)
                dv2 = dv2 + pltpu.roll(dv2, HEAD_DIM, 1)
                if h % 2 == 0:
                    keep_k, keep_v = dk2, dv2
                else:
                    nk_scr[:, tl] = jnp.where(low, keep_k, dk2)
                    nv_scr[:, tl] = jnp.where(low, keep_v, dv2)
            dk_ref[...] = _bf(ck_scr[...] + nk_scr[0:ATT_BLOCK, :])
            dv_ref[...] = _bf(cv_scr[...] + nv_scr[0:ATT_BLOCK, :])
            ck_scr[...] = nk_scr[ATT_BLOCK:2 * ATT_BLOCK, :]
            cv_scr[...] = nv_scr[ATT_BLOCK:2 * ATT_BLOCK, :]

        @pl.when(blk == nb)
        def _():
            dk_ref[...] = _bf(ck_scr[...])
            dv_ref[...] = _bf(cv_scr[...])

    cur = lambda i: jnp.minimum(i, nb - 1)
    prev = lambda i: jnp.maximum(cur(i) - 1, 0)
    late = lambda i: jnp.maximum(i - 1, 0)
    dq, dk, dv, dsink = pl.pallas_call(
        body, name="attn_bwd", grid=(nb + 1,),
        in_specs=[pl.BlockSpec(memory_space=pltpu.SMEM),
                  pl.BlockSpec((ATT_BLOCK, D), lambda i: (cur(i), COL_AQ // D)),
                  pl.BlockSpec((ATT_BLOCK, 256), lambda i: (prev(i), COL_AK // 256)),
                  pl.BlockSpec((ATT_BLOCK, 256), lambda i: (cur(i), COL_AK // 256)),
                  pl.BlockSpec((ATT_BLOCK, 256), lambda i: (prev(i), COL_AV // 256)),
                  pl.BlockSpec((ATT_BLOCK, 256), lambda i: (cur(i), COL_AV // 256)),
                  pl.BlockSpec((ATT_BLOCK, D), lambda i: (cur(i), 0)),
                  pl.BlockSpec((ATT_BLOCK, D), lambda i: (cur(i), 0))],
        out_specs=[pl.BlockSpec((ATT_BLOCK, D), lambda i: (cur(i), 0)),
                   pl.BlockSpec((ATT_BLOCK, 256), lambda i: (late(i), 0)),
                   pl.BlockSpec((ATT_BLOCK, 256), lambda i: (late(i), 0)),
                   pl.BlockSpec((16, 128), lambda i: (0, 0))],
        out_shape=[jax.ShapeDtypeStruct((T, D), BF16), jax.ShapeDtypeStruct((T, 256), BF16),
                   jax.ShapeDtypeStruct((T, 256), BF16), jax.ShapeDtypeStruct((16, 128), F32)],
        scratch_shapes=[pltpu.VMEM((ATT_BLOCK, 256), F32), pltpu.VMEM((ATT_BLOCK, 256), F32),
                        pltpu.VMEM((2 * ATT_BLOCK, 256), F32), pltpu.VMEM((2 * ATT_BLOCK, 256), F32)],
        compiler_params=_params(("arbitrary",)),
    )(sinks, proj, proj, proj, proj, proj, o, do)
    return dq, dk, dv, dsink


def _mid(x, tgt, proj, oh, oa, hnw, fnw, wsq_bf):
    T = x.shape[0]
    tm = min(256, T)
    nt = T // tm

    def body(x_ref, tgt_ref, oh_ref, oa_ref, hg_ref, ag0_ref, ag1_ref, mh0_ref, mh1_ref, ma0_ref, ma1_ref,
             hnw_ref, fnw_ref, w_hbm,
             dx2_ref, doh_ref, doa_ref, dhg_ref, dtail_ref, lhs_ref, rhs_ref, loss_ref, vec_ref,
             w_scr, xh_scr, rs_scr, sem):
        @pl.when(pl.program_id(0) == 0)
        def _():
            cp = pltpu.make_async_copy(w_hbm, w_scr, sem)
            cp.start()
            cp.wait()
            loss_ref[...] = jnp.zeros_like(loss_ref)
            vec_ref[...] = jnp.zeros_like(vec_ref)

        oh = oh_ref[...]
        for h in range(HEADS):
            sl = slice(h * HEAD_W, (h + 1) * HEAD_W)
            ohh = oh[:, sl]
            rs = lax.rsqrt(jnp.mean(ohh * ohh, axis=1, keepdims=True) + EPS)
            xh_scr[:, sl] = ohh * rs
            rs_scr[:, sl] = jnp.broadcast_to(rs, (tm, HEAD_W))
        xh = xh_scr[...]
        hnw = hnw_ref[...]
        on = xh * hnw
        hg = hg_ref[...]
        sg = _sigmoid(hg)
        silu_g = hg * sg
        gated_h = _bf(on * silu_g)
        lhs_ref[0] = gated_h.T
        yh = _dot(gated_h, w_scr[0])
        oa = oa_ref[...]
        ag = jnp.concatenate([ag0_ref[...], ag1_ref[...]], axis=1)
        sa = _sigmoid(ag)
        silu_a = ag * sa
        gated_a = _bf(oa * silu_a)
        lhs_ref[1] = gated_a.T
        ya = _dot(gated_a, w_scr[1])
        smh = _sigmoid(jnp.concatenate([mh0_ref[...], mh1_ref[...]], axis=1))
        sma = _sigmoid(jnp.concatenate([ma0_ref[...], ma1_ref[...]], axis=1))
        merged = _bf(smh * yh + sma * ya)
        lhs_ref[2] = merged.T
        x2 = x_ref[...] + _dot(merged, w_scr[2])
        rs2 = lax.rsqrt(jnp.mean(x2 * x2, axis=1, keepdims=True) + EPS)
        xh2 = x2 * rs2
        fnw = fnw_ref[...]
        diff = xh2 * fnw - tgt_ref[...]
        loss_ref[...] += jnp.zeros_like(loss_ref) + jnp.sum(diff * diff) * (0.5 / D)

        dy = diff * (1.0 / D)
        vec_ref[0:1, :] += jnp.sum(dy * xh2, axis=0, keepdims=True)
        gy = dy * fnw
        dx2 = rs2 * (gy - xh2 * jnp.mean(gy * xh2, axis=1, keepdims=True))
        dx2_ref[...] = dx2
        dx2_b = _bf(dx2)
        rhs_ref[2] = dx2_b
        dmerged = _dot_nt(dx2_b, w_scr[2])
        dyh = dmerged * smh
        dya = dmerged * sma
        dtail_ref[:, D:2 * D] = _bf(dyh * yh * (1.0 - smh))
        dtail_ref[:, 2 * D:3 * D] = _bf(dya * ya * (1.0 - sma))
        dyh_b, dya_b = _bf(dyh), _bf(dya)
        rhs_ref[0] = dyh_b
        rhs_ref[1] = dya_b
        dgh = _dot_nt(dyh_b, w_scr[0])
        dga = _dot_nt(dya_b, w_scr[1])
        don = dgh * silu_g
        dhg_ref[...] = _bf(dgh * on * (sg * (1.0 + hg * (1.0 - sg))))
        vec_ref[1:2, :] += jnp.sum(don * xh, axis=0, keepdims=True)
        gxh = don * hnw
        rsb = rs_scr[...]
        for h in range(HEADS):
            sl = slice(h * HEAD_W, (h + 1) * HEAD_W)
            gh, xhh = gxh[:, sl], xh[:, sl]
            doh_ref[:, sl] = rsb[:, sl] * (gh - xhh * jnp.mean(gh * xhh, axis=1, keepdims=True))
        doa_ref[...] = dga * silu_a
        dtail_ref[:, 0:D] = _bf(dga * oa * (sa * (1.0 + ag * (1.0 - sa))))

    row = lambda w, j: pl.BlockSpec((tm, w), lambda i: (i, j))
    const = lambda r, c: pl.BlockSpec((r, c), lambda i: (0, 0))
    stack = pl.BlockSpec((3, tm, D), lambda i: (0, i, 0))
    stack_t = pl.BlockSpec((3, D, tm), lambda i: (0, 0, i))
    return pl.pallas_call(
        body, name="mid", grid=(nt,),
        in_specs=[row(D, 0), row(D, 0), row(D, 0), row(D, 0), row(D, COL_HG // D),
                  row(512, COL_AG // 512), row(512, COL_AG // 512 + 1),
                  row(512, COL_MH // 512), row(512, COL_MH // 512 + 1),
                  row(512, COL_MA // 512), row(512, COL_MA // 512 + 1),
                  const(1, D), const(1, D), HBM_SPEC],
        out_specs=[row(D, 0), row(D, 0), row(D, 0), row(D, 0), row(3 * D, 0), stack_t, stack, const(8, 128), const(8, D)],
        out_shape=[jax.ShapeDtypeStruct((T, D), F32), jax.ShapeDtypeStruct((T, D), F32), jax.ShapeDtypeStruct((T, D), F32),
                   jax.ShapeDtypeStruct((T, D), BF16), jax.ShapeDtypeStruct((T, 3 * D), BF16),
                   jax.ShapeDtypeStruct((3, D, T), BF16), jax.ShapeDtypeStruct((3, T, D), BF16),
                   jax.ShapeDtypeStruct((8, 128), F32), jax.ShapeDtypeStruct((8, D), F32)],
        scratch_shapes=[pltpu.VMEM((3, D, D), BF16), pltpu.VMEM((tm, D), F32), pltpu.VMEM((tm, D), F32),
                        pltpu.SemaphoreType.DMA],
        compiler_params=_params(("arbitrary",)),
    )(x, tgt, oh, oa, proj, proj, proj, proj, proj, proj, proj, hnw, fnw, wsq_bf)


def _wgrad_square(lhs_t, rhs):
    T = rhs.shape[1]
    tk = min(1024, T)

    def body(a_ref, b_ref, g_ref):
        part = _dot(a_ref[...], b_ref[...])

        @pl.when(pl.program_id(1) == 0)
        def _():
            g_ref[...] = part

        @pl.when(pl.program_id(1) > 0)
        def _():
            g_ref[...] += part

    return pl.pallas_call(
        body, name="wgrad_square", grid=(3, T // tk),
        in_specs=[pl.BlockSpec((None, D, tk), lambda k, i: (k, 0, i)), pl.BlockSpec((None, tk, D), lambda k, i: (k, i, 0))],
        out_specs=pl.BlockSpec((None, D, D), lambda k, i: (k, 0, 0)),
        out_shape=jax.ShapeDtypeStruct((3, D, D), F32),
        compiler_params=_params(("parallel", "arbitrary")),
    )(lhs_t, rhs)


def _bwd_dx(pieces, wt_bf, x, norm_w, dx2, swin_b, ssq_b):
    T = x.shape[0]
    tm = min(256, T)
    nt = T // tm
    widths = [p.shape[1] for p in pieces]
    n_p = len(pieces)

    def body(*refs):
        piece_refs = refs[:n_p]
        (w_hbm, x_ref, nw_ref, dx2_ref, swin_ref, ssq_ref,
         gx_ref, gnw_ref, win_got, sq_got, w_scr, sem, send_sems, recv_sems) = refs[n_p:]

        def scatter_copies():
            x_, y_, c_ = _place()
            copies = []
            for k, (fx, fy) in enumerate(CHIP_FLIPS):
                px, py = _flip(x_, fx), _flip(y_, fy)
                jr = 2 * px + py
                for a, (src, dst) in enumerate(((swin_ref.at[:, pl.ds(jr * SHARD_W, SHARD_W)], win_got.at[k]),
                                                (ssq_ref.at[:, pl.ds(jr * SQ_ROWS, SQ_ROWS), :], sq_got.at[k]))):
                    copies.append(pltpu.make_async_remote_copy(
                        src_ref=src, dst_ref=dst, send_sem=send_sems.at[2 * k + a], recv_sem=recv_sems.at[2 * k + a],
                        device_id=(px, py, c_), device_id_type=MESH))
            return copies

        @pl.when(pl.program_id(0) == 0)
        def _():
            for cp in scatter_copies():
                cp.start()
            cp = pltpu.make_async_copy(w_hbm, w_scr, sem)
            cp.start()
            cp.wait()
            gnw_ref[...] = jnp.zeros_like(gnw_ref)

        dxn = None
        off = 0
        for ref, w in zip(piece_refs, widths):
            part = _dot(ref[...], w_scr[off:off + w, :])
            dxn = part if dxn is None else dxn + part
            off += w
        xf = x_ref[...]
        rs = lax.rsqrt(jnp.mean(xf * xf, axis=1, keepdims=True) + EPS)
        xh = xf * rs
        gnw_ref[...] += jnp.sum(dxn * xh, axis=0, keepdims=True)
        gx = dxn * nw_ref[...]
        gx_ref[...] = rs * (gx - xh * jnp.mean(gx * xh, axis=1, keepdims=True)) + dx2_ref[...]

        @pl.when(pl.program_id(0) == nt - 1)
        def _():
            for cp in scatter_copies():
                cp.wait()

    row = lambda w: pl.BlockSpec((tm, w), lambda i: (i, 0))
    return pl.pallas_call(
        body, name="bwd_dx", grid=(nt,),
        in_specs=[row(w) for w in widths] + [HBM_SPEC, row(D), pl.BlockSpec((1, D), lambda i: (0, 0)), row(D), HBM_SPEC, HBM_SPEC],
        out_specs=[row(D), pl.BlockSpec((1, D), lambda i: (0, 0)), HBM_SPEC, HBM_SPEC],
        out_shape=[jax.ShapeDtypeStruct((T, D), F32), jax.ShapeDtypeStruct((1, D), F32),
                   jax.ShapeDtypeStruct((3, D // 2, SHARD_W), BF16), jax.ShapeDtypeStruct((3, 3, SQ_ROWS, D // 2), BF16)],
        scratch_shapes=[pltpu.VMEM((D_IN, D), BF16), pltpu.SemaphoreType.DMA,
                        pltpu.SemaphoreType.DMA((6,)), pltpu.SemaphoreType.DMA((6,))],
        compiler_params=_params(("arbitrary",)),
    )(*pieces, wt_bf, x, norm_w, dx2, swin_b, ssq_b)


W_PIECES = ((0, 1024, 3), (COL_HG, 1024, 1), (COL_AQ, 1024, 1), (COL_AK, 256, 1), (COL_AV, 256, 1), (COL_AG, 512, 6))


def _wgrad_in(xnt_bf, pieces):
    T = xnt_bf.shape[1]
    tk = min(1024, T)
    buf = None
    for n, (piece, (col, wb, blocks)) in enumerate(zip(pieces, W_PIECES)):
        first = buf is None

        def body(xnt_ref, p_ref, *rest):
            g_ref = rest[-1]
            part = _dot(xnt_ref[...], p_ref[...])

            @pl.when(pl.program_id(1) == 0)
            def _():
                g_ref[...] = part

            @pl.when(pl.program_id(1) > 0)
            def _():
                g_ref[...] += part

        call = pl.pallas_call(
            body, name=f"wgrad_in_{n}", grid=(blocks, T // tk),
            in_specs=[pl.BlockSpec((D, tk), lambda jb, i: (0, i)), pl.BlockSpec((tk, wb), lambda jb, i: (i, jb))]
                     + ([] if first else [HBM_SPEC]),
            out_specs=pl.BlockSpec((D, wb), lambda jb, i, base=col // wb: (0, base + jb)),
            out_shape=jax.ShapeDtypeStruct((D, D_IN), F32),
            input_output_aliases={} if first else {2: 0},
            compiler_params=_params(("parallel", "arbitrary")),
        )
        buf = call(xnt_bf, piece) if first else call(xnt_bf, piece, buf)
    return buf


def _place():
    return lax.axis_index("x"), lax.axis_index("y"), lax.axis_index("c")


def _flip(v, f):
    return 1 - v if f else v


def _win_half(ref, h):
    return ref.at[pl.ds(h * (D // 2), D // 2), :]


def _sq_half(ref, h):
    return ref.at[:, pl.ds(h * (D // 2), D // 2)]


def _gather_copy(part, k, to, send_sems, recv_sems):
    return pltpu.make_async_remote_copy(src_ref=part, dst_ref=part, send_sem=send_sems.at[k], recv_sem=recv_sems.at[k],
                                        device_id=to, device_id_type=MESH)


def _gather_start(out, half, send_sems, recv_sems):
    x, y, c = _place()
    for k, (fx, fy) in enumerate(CHIP_FLIPS):
        _gather_copy(half(out.at[2 * x + y], c), k, (_flip(x, fx), _flip(y, fy), c), send_sems, recv_sems).start()


def _gather_finish(out, half, send_sems, recv_sems):
    x, y, c = _place()
    sib = (x, y, 1 - c)
    slots = [2 * _flip(x, fx) + _flip(y, fy) for fx, fy in CHIP_FLIPS]
    passed = []
    for k, jr in enumerate(slots):
        landed = half(out.at[jr], c)
        _gather_copy(landed, k, sib, send_sems, recv_sems).wait_recv()
        cp = _gather_copy(landed, 3 + k, sib, send_sems, recv_sems)
        cp.start()
        passed.append(cp)
    for k, jr in enumerate(slots):
        _gather_copy(half(out.at[jr], 1 - c), 3 + k, sib, send_sems, recv_sems).wait_recv()
    for k, (fx, fy) in enumerate(CHIP_FLIPS):
        _gather_copy(half(out.at[2 * x + y], c), k, (_flip(x, fx), _flip(y, fy), c), send_sems, recv_sems).wait_send()
    for cp in passed:
        cp.wait_send()


def _allgather_w_in(win_all):
    def body(win_in, win_out, send_sems, recv_sems):
        del win_in
        _gather_start(win_out, _win_half, send_sems, recv_sems)
        _gather_finish(win_out, _win_half, send_sems, recv_sems)

    return pl.pallas_call(
        body, name="allgather_w_in", in_specs=[HBM_SPEC], out_specs=HBM_SPEC, input_output_aliases={0: 0},
        out_shape=jax.ShapeDtypeStruct((SHARDS, D, SHARD_W), BF16),
        scratch_shapes=[pltpu.SemaphoreType.DMA((6,)), pltpu.SemaphoreType.DMA((6,))],
    )(win_all)


def _swap_halves(gwin, gsq):
    def body(gwin_ref, gsq_ref, win_got, sq_got, send_sems, recv_sems):
        x, y, c = _place()
        sib = (x, y, 1 - c)
        pairs = ((_win_half(gwin_ref, 1 - c), win_got),
                 (gsq_ref.at[:, :, pl.ds((1 - c) * (D // 2), D // 2)], sq_got))
        copies = [pltpu.make_async_remote_copy(src_ref=src, dst_ref=dst, send_sem=send_sems.at[a], recv_sem=recv_sems.at[a],
                                               device_id=sib, device_id_type=MESH) for a, (src, dst) in enumerate(pairs)]
        for cp in copies:
            cp.start()
        for cp in copies:
            cp.wait()

    return pl.pallas_call(
        body, name="swap_halves",
        in_specs=[HBM_SPEC, HBM_SPEC], out_specs=[HBM_SPEC, HBM_SPEC],
        out_shape=[jax.ShapeDtypeStruct((D // 2, D_IN), F32), jax.ShapeDtypeStruct((3, D, D // 2), F32)],
        scratch_shapes=[pltpu.SemaphoreType.DMA((2,)), pltpu.SemaphoreType.DMA((2,))],
    )(gwin, gsq)


def _add_halves(c_arr, gwin, gsq, win_got, sq_got):
    def body(c_ref, a_ref, b_ref, p_ref, q_ref, so_ref, sq_ref, sob_ref, sqb_ref):
        so = a_ref[...] + b_ref[...]
        sq = p_ref[...] + q_ref[...]
        so_ref[...] = so
        sq_ref[...] = sq
        sob_ref[...] = _bf(so)
        sqb_ref[...] = _bf(sq)

    steps = 8
    rows, sq_rows = (D // 2) // steps, D // steps
    win = lambda f: pl.BlockSpec((rows, D_IN), f)
    sq = lambda f: pl.BlockSpec((3, sq_rows, D // 2), f)
    return pl.pallas_call(
        body, name="add_halves",
        grid_spec=pltpu.PrefetchScalarGridSpec(
            num_scalar_prefetch=1, grid=(steps,),
            in_specs=[win(lambda i, c: (c[0] * steps + i, 0)), win(lambda i, c: (i, 0)),
                      sq(lambda i, c: (0, i, c[0])), sq(lambda i, c: (0, i, 0))],
            out_specs=[win(lambda i, c: (i, 0)), sq(lambda i, c: (0, i, 0))] * 2),
        out_shape=[jax.ShapeDtypeStruct((D // 2, D_IN), F32), jax.ShapeDtypeStruct((3, D, D // 2), F32),
                   jax.ShapeDtypeStruct((D // 2, D_IN), BF16), jax.ShapeDtypeStruct((3, D, D // 2), BF16)],
        compiler_params=_params(("arbitrary",)),
    )(c_arr, gwin, win_got, gsq, sq_got)


def _sum_chips(jc_arr, swin, ssq, win_got, sq_got):
    def body(jc_ref, a_ref, b_ref, p_ref, q_ref, so_ref, sq_ref):
        so_ref[...] = ((a_ref[...] + b_ref[0].astype(F32)) + b_ref[1].astype(F32)) + b_ref[2].astype(F32)
        sq_ref[...] = ((p_ref[...] + q_ref[0].astype(F32)) + q_ref[1].astype(F32)) + q_ref[2].astype(F32)

    rows = 128
    steps = (D // 2) // rows
    sq_rows = SQ_ROWS // steps
    return pl.pallas_call(
        body, name="sum_chips",
        grid_spec=pltpu.PrefetchScalarGridSpec(
            num_scalar_prefetch=1, grid=(steps,),
            in_specs=[pl.BlockSpec((rows, SHARD_W), lambda i, jc: (i, jc[0])),
                      pl.BlockSpec((3, rows, SHARD_W), lambda i, jc: (0, i, 0)),
                      pl.BlockSpec((3, sq_rows, D // 2), lambda i, jc: (0, jc[0] * steps + i, 0)),
                      pl.BlockSpec((3, 3, sq_rows, D // 2), lambda i, jc: (0, 0, i, 0))],
            out_specs=[pl.BlockSpec((rows, SHARD_W), lambda i, jc: (jc[1] * steps + i, 0)),
                       pl.BlockSpec((3, sq_rows, D // 2), lambda i, jc: (0, i, jc[1]))]),
        out_shape=[jax.ShapeDtypeStruct((D, SHARD_W), F32), jax.ShapeDtypeStruct((3, SQ_ROWS, D), F32)],
        compiler_params=_params(("arbitrary",)),
    )(jc_arr, swin, win_got, ssq, sq_got)


def _join_halves(g_win, g_sq):
    def body(win_in, sq_in, win_out, sq_out, send_sems, recv_sems):
        del win_in, sq_in
        x, y, c = _place()
        sib = (x, y, 1 - c)

        def halves(h):
            return _win_half(win_out, h), sq_out.at[:, :, pl.ds(h * (D // 2), D // 2)]

        def copy(a, part):
            return pltpu.make_async_remote_copy(src_ref=part, dst_ref=part, send_sem=send_sems.at[a], recv_sem=recv_sems.at[a],
                                                device_id=sib, device_id_type=MESH)

        sent = [copy(a, part) for a, part in enumerate(halves(c))]
        for cp in sent:
            cp.start()
        for a, part in enumerate(halves(1 - c)):
            copy(a, part).wait_recv()
        for cp in sent:
            cp.wait_send()

    return pl.pallas_call(
        body, name="join_halves",
        in_specs=[HBM_SPEC, HBM_SPEC], out_specs=[HBM_SPEC, HBM_SPEC], input_output_aliases={0: 0, 1: 1},
        out_shape=[jax.ShapeDtypeStruct((D, SHARD_W), F32), jax.ShapeDtypeStruct((3, SQ_ROWS, D), F32)],
        scratch_shapes=[pltpu.SemaphoreType.DMA((2,)), pltpu.SemaphoreType.DMA((2,))],
    )(g_win, g_sq)


def _allreduce_small(vec):
    def body(vec_ref, out_ref, slots, send_sems, recv_sems):
        x, y, c = _place()
        me = 4 * x + 2 * y + c
        slots[me] = vec_ref[...]
        copies = []
        for k in range(1, 8):
            fx, fy, fc = (k >> 2) & 1, (k >> 1) & 1, k & 1
            copies.append(pltpu.make_async_remote_copy(
                src_ref=vec_ref, dst_ref=slots.at[me], send_sem=send_sems.at[k - 1], recv_sem=recv_sems.at[k - 1],
                device_id=(_flip(x, fx), _flip(y, fy), _flip(c, fc)), device_id_type=MESH))
        for cp in copies:
            cp.start()
        for k in range(1, 8):
            fx, fy, fc = (k >> 2) & 1, (k >> 1) & 1, k & 1
            src = 4 * _flip(x, fx) + 2 * _flip(y, fy) + _flip(c, fc)
            pltpu.make_async_remote_copy(src_ref=vec_ref, dst_ref=slots.at[src], send_sem=send_sems.at[k - 1],
                                         recv_sem=recv_sems.at[k - 1], device_id=(x, y, c), device_id_type=MESH).wait_recv()
        for cp in copies:
            cp.wait_send()
        total = slots[0]
        for s in range(1, 8):
            total = total + slots[s]
        out_ref[...] = total

    return pl.pallas_call(
        body, name="allreduce_small",
        in_specs=[pl.BlockSpec(memory_space=pltpu.VMEM)], out_specs=pl.BlockSpec(memory_space=pltpu.VMEM),
        out_shape=jax.ShapeDtypeStruct((8, D), F32),
        scratch_shapes=[pltpu.VMEM((8, 8, D), F32), pltpu.SemaphoreType.DMA((7,)), pltpu.SemaphoreType.DMA((7,))],
    )(vec)


def _adamw_math(w, g, m, v):
    m = ADAM_B1 * m + (1.0 - ADAM_B1) * g
    v = ADAM_B2 * v + (1.0 - ADAM_B2) * (g * g)
    m_hat = m / (1.0 - ADAM_B1 ** ADAM_STEP)
    v_hat = v / (1.0 - ADAM_B2 ** ADAM_STEP)
    delta = -ADAM_LR * (m_hat / (jnp.sqrt(v_hat) + ADAM_EPS) + ADAM_WD * w)
    return delta, m, v


def _adamw(name, w, g, m, v, rows):
    R, C = w.shape

    def body(w_ref, g_ref, m_ref, v_ref, d_out, m_out, v_out):
        d_out[...], m_out[...], v_out[...] = _adamw_math(w_ref[...], g_ref[...], m_ref[...], v_ref[...])

    spec = pl.BlockSpec((rows, C), lambda i: (i, 0))
    return pl.pallas_call(
        body, name=name, grid=(R // rows,), in_specs=[spec] * 4, out_specs=[spec] * 3,
        out_shape=[jax.ShapeDtypeStruct((R, C), F32)] * 3,
        compiler_params=_params(("parallel",)),
    )(w, g, m, v)


def _small_update(total, lbw, w8, m8, v8):
    def body(t_ref, lbw_ref, w_ref, m_ref, v_ref, g_out, d_out, m_out, v_out):
        lb = 1.0 / (1.0 + jnp.exp(lbw_ref[1:2, :] - lbw_ref[0:1, :]))
        dlb = t_ref[2:3, :] * lb * (1.0 - lb)
        g_out[...] = jnp.zeros_like(g_out)
        g_out[0:1, :] = t_ref[3:4, :]
        g_out[1:2, :] = dlb
        g_out[2:3, :] = -dlb
        g_out[3:4, :] = t_ref[1:2, :]
        g_out[4:5, :] = t_ref[0:1, :]
        g_out[5:6, :] = t_ref[4:5, :]
        d_out[...], m_out[...], v_out[...] = _adamw_math(w_ref[...], g_out[...], m_ref[...], v_ref[...])

    return pl.pallas_call(
        body, name="small_update", out_shape=[jax.ShapeDtypeStruct((8, D), F32)] * 4,
        compiler_params=_params(),
    )(total, lbw, w8, m8, v8)


def _pack8(norm_w, lbw, hnw, fnw, sinks):
    pad = jnp.zeros((1, D - 16), F32)
    return jnp.concatenate([norm_w, lbw, hnw, fnw.reshape(1, D), jnp.concatenate([sinks, pad], axis=1),
                            jnp.zeros((2, D), F32)], axis=0)


def _unpack8(a):
    return a[0:1], a[1:3], a[3:4], a[5:6, 0:16], a[4]


def _local_step(x, tgt, norm_w, lbw, hnw, sinks, fnw, win_bf, wsq_mine, exchange):
    proj, xnt_bf = _fwd_proj(x, norm_w, win_bf)
    oh, states, wsq_all = _hgrn_fwd(proj, lbw, wsq_mine)
    wsq_bf = wsq_all.reshape(SHARDS, 3, SQ_ROWS, D).transpose(1, 0, 2, 3).reshape(3, D, D)
    oa = _attn_fwd(proj, sinks)
    dx2, doh, doa, dhg, dtail, lhs, rhs, loss8, vec_mid = _mid(x, tgt, proj, oh, oa, hnw, fnw.reshape(1, D), wsq_bf)
    gsq = _wgrad_square(lhs, rhs)
    dhead, dlb = _hgrn_bwd(proj, lbw, states, doh)
    daq, dak, dav, dsink = _attn_bwd(proj, sinks, oa, doa)
    pieces = [dhead, dhg, daq, dak, dav, dtail]
    sums = exchange(_wgrad_in(xnt_bf, pieces), gsq)
    wt_bf = win_bf.transpose(0, 2, 1).reshape(D_IN, D)
    grad_x, gnw, win_got, sq_got = _bwd_dx(pieces, wt_bf, x, norm_w, dx2, sums[2], sums[3])
    sink_row = jnp.concatenate([dsink[:, 0].reshape(1, 16), jnp.zeros((1, D - 16), F32)], axis=1)
    vec = jnp.concatenate([vec_mid[0:2], dlb, gnw, sink_row, jnp.zeros((3, D), F32)], axis=0)
    return loss8[0, 0], grad_x, sums, (win_got, sq_got), vec


def kernel(x, norm_w, w_in, hgrn_lower_bound, hgrn_norm_w, w_branch_hgrn, attn_sinks, w_branch_attn, w_out, final_norm_w, loss_target, m_norm_w, m_w_in, m_hgrn_lower_bound, m_hgrn_norm_w, m_w_branch_hgrn, m_attn_sinks, m_w_branch_attn, m_w_out, m_final_norm_w, v_norm_w, v_w_in, v_hgrn_lower_bound, v_hgrn_norm_w, v_w_branch_hgrn, v_attn_sinks, v_w_branch_attn, v_w_out, v_final_norm_w):
    c_arr = lax.axis_index("c").astype(jnp.int32).reshape(1)
    j_arr = (2 * lax.axis_index("x") + lax.axis_index("y")).astype(jnp.int32).reshape(1)
    jc_arr = jnp.concatenate([j_arr, c_arr])

    win_mine, wsq_mine = _cast_shards(j_arr, w_in[0], w_branch_hgrn[0], w_branch_attn[0], w_out[0])
    win_bf = _allgather_w_in(win_mine)

    def chip_sums(gwin, gsq):
        return _add_halves(c_arr, gwin, gsq, *_swap_halves(gwin, gsq))

    loss_part, grad_x, (swin, ssq, _, _), arrived, vec = _local_step(
        x[0], loss_target[0], norm_w, hgrn_lower_bound, hgrn_norm_w, attn_sinks, final_norm_w, win_bf, wsq_mine, chip_sums)
    loss = lax.psum(loss_part, ("x", "y", "c"))
    g_win, g_sq = _join_halves(*_sum_chips(jc_arr, swin, ssq, *arrived))

    d_win, nm_win, nv_win = _adamw("adamw_w_in", w_in[0], g_win, m_w_in[0], v_w_in[0], 128)
    sq_w = jnp.concatenate([w_branch_hgrn[0], w_branch_attn[0], w_out[0]], axis=0)
    sq_m = jnp.concatenate([m_w_branch_hgrn[0], m_w_branch_attn[0], m_w_out[0]], axis=0)
    sq_v = jnp.concatenate([v_w_branch_hgrn[0], v_w_branch_attn[0], v_w_out[0]], axis=0)
    d_sq, nm_sq, nv_sq = _adamw("adamw_square", sq_w, g_sq.reshape(3 * SQ_ROWS, D), sq_m, sq_v, 256)

    total = _allreduce_small(vec)
    g8, d8, nm8, nv8 = _small_update(
        total, hgrn_lower_bound,
        _pack8(norm_w, hgrn_lower_bound, hgrn_norm_w, final_norm_w, attn_sinks),
        _pack8(m_norm_w, m_hgrn_lower_bound, m_hgrn_norm_w, m_final_norm_w, m_attn_sinks),
        _pack8(v_norm_w, v_hgrn_lower_bound, v_hgrn_norm_w, v_final_norm_w, v_attn_sinks))

    def assemble(win, sq, small):
        nw, lb, hn, sk, fn = _unpack8(small)
        sq = sq.reshape(3, 1, SQ_ROWS, D)
        return (nw, win.reshape(1, D, SHARD_W), lb, hn, sq[0], sk, sq[1], sq[2], fn)

    return (loss, grad_x.reshape(1, -1, D),
            *assemble(g_win, g_sq, g8), *assemble(d_win, d_sq, d8),
            *assemble(nm_win, nm_sq, nm8), *assemble(nv_win, nv_sq, nv8))
```

```python
import functools

import jax
import jax.numpy as jnp
from jax import lax
from jax.experimental import pallas as pl
from jax.experimental.pallas import tpu as pltpu

F32 = jnp.float32
BF16 = jnp.bfloat16

D = 1024
D_IN = 8704
SHARDS = 4
SHARD_W = D_IN // SHARDS
SQ_ROWS = D // SHARDS
HEADS = 8
HEAD_W = 128
CHUNK = 64
SUB = 4
ATT_BLOCK = 128
KV_HEADS = 4
HEAD_DIM = 64
EPS = 1e-6
NEG = -1e30
SCALE = HEAD_DIM ** -0.5
COL_HG, COL_AQ, COL_AK, COL_AV, COL_AG, COL_MH, COL_MA = 3072, 4096, 5120, 5376, 5632, 6656, 7680

ADAM_LR, ADAM_B1, ADAM_B2, ADAM_EPS, ADAM_WD, ADAM_STEP = 0.001, 0.9, 0.999, 1e-08, 0.01, 10

VMEM_LIMIT = 56 * 1024 * 1024
MESH = pl.DeviceIdType.MESH
HBM_SPEC = pl.BlockSpec(memory_space=pltpu.HBM)
CHIP_FLIPS = ((1, 0), (0, 1), (1, 1))


def _dot(a, b):
    return jnp.dot(a, b, preferred_element_type=F32)


def _dot_nt(a, b):
    return lax.dot_general(a, b, (((1,), (1,)), ((), ())), preferred_element_type=F32)


def _dot_tn(a, b):
    return lax.dot_general(a, b, (((0,), (0,)), ((), ())), preferred_element_type=F32)


def _sigmoid(v):
    return 1.0 / (1.0 + jnp.exp(-v))


def _bf(v):
    return v.astype(BF16)


def _split3(v):
    a = _bf(v)
    r = v - a.astype(F32)
    b = _bf(r)
    c = _bf(r - b.astype(F32))
    return a, b, c


def _tri_dot(tri, v):
    a, b, c = _split3(v)
    return _dot(tri, a) + _dot(tri, b) + _dot(tri, c)


def _params(sem=None):
    return pltpu.CompilerParams(dimension_semantics=sem, vmem_limit_bytes=VMEM_LIMIT)


def _cast_shards(j_arr, win_s, wbh_s, wba_s, wout_s):
    steps = 4
    rows = D // steps

    def body(j_ref, win_ref, a_ref, b_ref, c_ref, win_o, sq_o):
        win_o[...] = _bf(win_ref[...])

        @pl.when(pl.program_id(0) == 0)
        def _():
            sq_o[0:SQ_ROWS, :] = _bf(a_ref[...])
            sq_o[SQ_ROWS:2 * SQ_ROWS, :] = _bf(b_ref[...])
            sq_o[2 * SQ_ROWS:3 * SQ_ROWS, :] = _bf(c_ref[...])

    whole = pl.BlockSpec((SQ_ROWS, D), lambda i, j: (0, 0))
    return pl.pallas_call(
        body, name="cast_shards",
        grid_spec=pltpu.PrefetchScalarGridSpec(
            num_scalar_prefetch=1, grid=(steps,),
            in_specs=[pl.BlockSpec((rows, SHARD_W), lambda i, j: (i, 0)), whole, whole, whole],
            out_specs=[pl.BlockSpec((None, rows, SHARD_W), lambda i, j: (j[0], i, 0)),
                       pl.BlockSpec((None, 3 * SQ_ROWS, D), lambda i, j: (j[0], 0, 0))]),
        out_shape=[jax.ShapeDtypeStruct((SHARDS, D, SHARD_W), BF16), jax.ShapeDtypeStruct((SHARDS, 3 * SQ_ROWS, D), BF16)],
        compiler_params=_params(("arbitrary",)),
    )(j_arr, win_s, wbh_s, wba_s, wout_s)


def _fwd_proj(order_arr, x, norm_w, win_all):
    T = x.shape[0]
    tm = min(512, T)
    nt = T // tm

    def body(order_ref, x_ref, nw_ref, win_in, proj_ref, xn_ref, win_out, w_scr, sem, send_sems, recv_sems):
        del win_in
        p, i = pl.program_id(0), pl.program_id(1)

        def load(slot):
            cp = pltpu.make_async_copy(win_out.at[slot], w_scr, sem)
            cp.start()
            cp.wait()

        @pl.when((p == 0) & (i == 0))
        def _():
            _gather_start(win_out, _win_half, send_sems, recv_sems)
            load(order_ref[0])

        for k in range(SHARDS - 1):
            @pl.when((p == k + 1) & (i == 0))
            def _():
                _gather_land(win_out, _win_half, k, send_sems, recv_sems)
                load(order_ref[k + 1])

        xf = x_ref[...]
        rs = lax.rsqrt(jnp.mean(xf * xf, axis=1, keepdims=True) + EPS)
        xn = _bf((xf * rs) * nw_ref[...])

        @pl.when(p == 0)
        def _():
            xn_ref[...] = xn.T

        proj_ref[...] = _dot(xn, w_scr[...])

        @pl.when((p == SHARDS - 1) & (i == nt - 1))
        def _():
            _gather_drain(win_out, _win_half, send_sems, recv_sems)

    return pl.pallas_call(
        body, name="fwd_proj",
        grid_spec=pltpu.PrefetchScalarGridSpec(
            num_scalar_prefetch=1, grid=(SHARDS, nt),
            in_specs=[pl.BlockSpec((tm, D), lambda p, i, order: (i, 0)), pl.BlockSpec((1, D), lambda p, i, order: (0, 0)), HBM_SPEC],
            out_specs=[pl.BlockSpec((tm, SHARD_W), lambda p, i, order: (i, order[p])),
                       pl.BlockSpec((D, tm), lambda p, i, order: (0, jnp.where(p == 0, i, nt - 1))),
                       HBM_SPEC],
            scratch_shapes=[pltpu.VMEM((D, SHARD_W), BF16), pltpu.SemaphoreType.DMA,
                            pltpu.SemaphoreType.DMA((6,)), pltpu.SemaphoreType.DMA((6,))]),
        out_shape=[jax.ShapeDtypeStruct((T, D_IN), F32), jax.ShapeDtypeStruct((D, T), BF16),
                   jax.ShapeDtypeStruct((SHARDS, D, SHARD_W), BF16)],
        input_output_aliases={3: 2},
        compiler_params=_params(("arbitrary", "arbitrary")),
    )(order_arr, x, norm_w, win_all)


def _hgrn_gates(hq_ref, hf_ref, lbw_ref, b_scr):
    lb = 1.0 / (1.0 + jnp.exp(lbw_ref[1:2, :] - lbw_ref[0:1, :]))
    hf = hf_ref[...]
    sig = _sigmoid(hf)
    f = lb + (1.0 - lb) * sig
    g = jnp.log(f)
    hq = hq_ref[...]
    sq = _sigmoid(hq)
    q = hq * sq
    row = lax.broadcasted_iota(jnp.int32, (CHUNK, CHUNK), 0)
    col = lax.broadcasted_iota(jnp.int32, (CHUNK, CHUNK), 1)
    causal = row >= col
    b = _tri_dot(jnp.where(causal, 1.0, 0.0).astype(BF16), g)
    b_scr[...] = b
    bc = b_scr[CHUNK - 1:CHUNK, :]
    r = b_scr[CHUNK // 2 - 1:CHUNK // 2, :]
    return dict(lb=lb, sig=sig, f=f, k=1.0 - f, hq=hq, sq=sq, q=q, b=b, bc=bc, r=r, causal=causal)


def _hgrn_fwd(proj, lbw, wsq_all):
    T = proj.shape[0]
    n = T // CHUNK

    def body(hq_ref, hf_ref, hi_ref, lbw_ref, wsq_in, o_ref, st_ref, wsq_out, s_scr, b_scr, send_sems, recv_sems):
        del wsq_in

        @pl.when(pl.program_id(0) == 0)
        def _():
            _gather_start(wsq_out, _sq_half, send_sems, recv_sems)
            s_scr[...] = jnp.zeros_like(s_scr)

        for c in range(SUB):
            rows = pl.ds(c * CHUNK, CHUNK)
            gt = _hgrn_gates(hq_ref.at[rows, :], hf_ref.at[rows, :], lbw_ref, b_scr.at[rows, :])
            b, bc, r, q, k = gt["b"], gt["bc"], gt["r"], gt["q"], gt["k"]
            qe = _bf(q * jnp.exp(b))
            qr = _bf(q * jnp.exp(b - r))
            kr = _bf(k * jnp.exp(r - b))
            kl = _bf(k * jnp.exp(bc - b))
            ebc = jnp.exp(bc)
            v = _bf(hi_ref[rows, :])
            scores = [_bf(jnp.where(gt["causal"], _dot_nt(qr[:, h * HEAD_W:(h + 1) * HEAD_W], kr[:, h * HEAD_W:(h + 1) * HEAD_W]), 0.0))
                      for h in range(HEADS)]
            for h in range(HEADS):
                sl = slice(h * HEAD_W, (h + 1) * HEAD_W)
                st = s_scr[h]
                st_ref[c, h] = st
                o_ref[rows, sl] = _dot(scores[h], v[:, sl]) + _dot_nt(qe[:, sl], _bf(st))
                s_scr[h] = ebc[:, sl] * st + _dot_tn(v[:, sl], kl[:, sl])

        @pl.when(pl.program_id(0) == n // SUB - 1)
        def _():
            _gather_finish(wsq_out, _sq_half, send_sems, recv_sems)

    col = lambda j: pl.BlockSpec((SUB * CHUNK, D), lambda i: (i, j))
    return pl.pallas_call(
        body, name="hgrn_fwd", grid=(n // SUB,),
        in_specs=[col(0), col(1), col(2), pl.BlockSpec((2, D), lambda i: (0, 0)), HBM_SPEC],
        out_specs=[pl.BlockSpec((SUB * CHUNK, D), lambda i: (i, 0)),
                   pl.BlockSpec((SUB, HEADS, HEAD_W, HEAD_W), lambda i: (i, 0, 0, 0)), HBM_SPEC],
        out_shape=[jax.ShapeDtypeStruct((T, D), F32), jax.ShapeDtypeStruct((n, HEADS, HEAD_W, HEAD_W), F32),
                   jax.ShapeDtypeStruct((SHARDS, 3 * SQ_ROWS, D), BF16)],
        input_output_aliases={4: 2},
        scratch_shapes=[pltpu.VMEM((HEADS, HEAD_W, HEAD_W), F32), pltpu.VMEM((SUB * CHUNK, D), F32),
                        pltpu.SemaphoreType.DMA((6,)), pltpu.SemaphoreType.DMA((6,))],
        compiler_params=_params(("arbitrary",)),
    )(proj, proj, proj, lbw, wsq_all)


def _hgrn_bwd(proj, lbw, states, do):
    T = proj.shape[0]
    n = T // CHUNK

    def body(hq_ref, hf_ref, hi_ref, lbw_ref, st_ref, do_ref, dp_ref, dlb_ref,
             ds_scr, b_scr, dq_scr, dk_scr, dv_scr, late_scr, early_scr, ex_scr):
        @pl.when(pl.program_id(0) == 0)
        def _():
            ds_scr[...] = jnp.zeros_like(ds_scr)
            dlb_ref[...] = jnp.zeros_like(dlb_ref)

        for c in reversed(range(SUB)):
            rows = pl.ds(c * CHUNK, CHUNK)
            gt = _hgrn_gates(hq_ref.at[rows, :], hf_ref.at[rows, :], lbw_ref, b_scr.at[rows, :])
            b, bc, r, q, k = gt["b"], gt["bc"], gt["r"], gt["q"], gt["k"]
            eb = jnp.exp(b)
            er = jnp.exp(b - r)
            erk = jnp.exp(r - b)
            el = jnp.exp(bc - b)
            ebc = jnp.exp(bc)
            qe, qr, kr, kl = _bf(q * eb), _bf(q * er), _bf(k * erk), _bf(k * el)
            v = _bf(hi_ref[rows, :])
            do_b = _bf(do_ref[rows, :])
            do_t = do_b.T
            causal_t = lax.broadcasted_iota(jnp.int32, (CHUNK, CHUNK), 0) <= lax.broadcasted_iota(jnp.int32, (CHUNK, CHUNK), 1)
            firsts = []
            for h in range(HEADS):
                sl = slice(h * HEAD_W, (h + 1) * HEAD_W)
                firsts.append((_bf(jnp.where(causal_t, _dot_nt(kr[:, sl], qr[:, sl]), 0.0)),
                               _bf(jnp.where(gt["causal"], _dot_nt(do_b[:, sl], v[:, sl]), 0.0)),
                               _bf(jnp.where(causal_t, _dot_nt(v[:, sl], do_b[:, sl]), 0.0))))
            for h in range(HEADS):
                sl = slice(h * HEAD_W, (h + 1) * HEAD_W)
                st0 = st_ref[c, h]
                dst = ds_scr[h]
                dst_b = _bf(dst)
                a_t, da, da_t = firsts[h]
                mq = _dot(da, kr[:, sl])
                mk = _dot(da_t, qr[:, sl])
                dq_in = eb[:, sl] * _dot(do_b[:, sl], _bf(st0))
                dk_in = el[:, sl] * _dot(v[:, sl], dst_b)
                dq_scr[rows, sl] = er[:, sl] * mq + dq_in
                dk_scr[rows, sl] = erk[:, sl] * mk + dk_in
                dv_scr[rows, sl] = _dot(a_t, do_b[:, sl]) + _dot_nt(kl[:, sl], dst_b)
                late_scr[rows, sl] = q[:, sl] * dq_in + qr[:, sl].astype(F32) * mq - kr[:, sl].astype(F32) * mk
                early_scr[rows, sl] = k[:, sl] * dk_in
                ex_scr[:, sl] = jnp.sum(dst * st0, axis=0, keepdims=True)
                ds_scr[h] = ebc[:, sl] * dst + _dot(do_t[sl, :], qe[:, sl])

            dq, dk = dq_scr[rows, :], dk_scr[rows, :]
            row = lax.broadcasted_iota(jnp.int32, (CHUNK, CHUNK), 0)
            col = lax.broadcasted_iota(jnp.int32, (CHUNK, CHUNK), 1)
            at_or_after = jnp.where(col >= row, 1.0, 0.0).astype(BF16)
            before = jnp.where(col < row, 1.0, 0.0).astype(BF16)
            dg = _tri_dot(at_or_after, late_scr[rows, :]) + _tri_dot(before, early_scr[rows, :]) + ebc * ex_scr[...]
            df = dg / gt["f"] - dk
            sig, sq, hq, lb = gt["sig"], gt["sq"], gt["hq"], gt["lb"]
            dp_ref[rows, 0:D] = _bf(dq * (sq * (1.0 + hq * (1.0 - sq))))
            dp_ref[rows, D:2 * D] = _bf(df * (1.0 - lb) * sig * (1.0 - sig))
            dp_ref[rows, 2 * D:3 * D] = _bf(dv_scr[rows, :])
            dlb_ref[...] += jnp.sum(df * (1.0 - sig), axis=0, keepdims=True)

    ns = n // SUB
    col = lambda j: pl.BlockSpec((SUB * CHUNK, D), lambda i: (ns - 1 - i, j))
    return pl.pallas_call(
        body, name="hgrn_bwd", grid=(ns,),
        in_specs=[col(0), col(1), col(2), pl.BlockSpec((2, D), lambda i: (0, 0)),
                  pl.BlockSpec((SUB, HEADS, HEAD_W, HEAD_W), lambda i: (ns - 1 - i, 0, 0, 0)),
                  pl.BlockSpec((SUB * CHUNK, D), lambda i: (ns - 1 - i, 0))],
        out_specs=[pl.BlockSpec((SUB * CHUNK, 3 * D), lambda i: (ns - 1 - i, 0)),
                   pl.BlockSpec((1, D), lambda i: (0, 0))],
        out_shape=[jax.ShapeDtypeStruct((T, 3 * D), BF16), jax.ShapeDtypeStruct((1, D), F32)],
        scratch_shapes=[pltpu.VMEM((HEADS, HEAD_W, HEAD_W), F32)] + [pltpu.VMEM((SUB * CHUNK, D), F32)] * 6
                       + [pltpu.VMEM((1, D), F32)],
        compiler_params=_params(("arbitrary",)),
    )(proj, proj, proj, lbw, states, do)


def _attn_masks(blk):
    qi = lax.broadcasted_iota(jnp.int32, (ATT_BLOCK, 2 * ATT_BLOCK), 0)
    kj = lax.broadcasted_iota(jnp.int32, (ATT_BLOCK, 2 * ATT_BLOCK), 1)
    band = (kj > qi) & (kj <= qi + ATT_BLOCK)
    return band & ((blk > 0) | (kj >= ATT_BLOCK))


def _head_pair_operand(t, hp, low):
    mine = low if hp == 0 else jnp.logical_not(low)
    both = jnp.where(mine, t, pltpu.roll(t, HEAD_DIM, 1))
    return _bf(jnp.concatenate([jnp.where(low, both, 0.0), jnp.where(low, 0.0, both)], axis=0))


def _attn_probs(s, sink, valid):
    s = jnp.where(valid, s * SCALE, NEG)
    m = jnp.maximum(jnp.max(s, axis=1, keepdims=True), sink)
    p = jnp.exp(s - m)
    es = jnp.exp(sink - m)
    inv = 1.0 / (jnp.sum(p, axis=1, keepdims=True) + es)
    return p * inv, es * inv


def _attn_fwd(proj, sinks):
    T = proj.shape[0]
    nb = T // ATT_BLOCK
    W2 = 2 * ATT_BLOCK

    def body(sink_ref, q_ref, kp_ref, kc_ref, vp_ref, vc_ref, o_ref):
        blk = pl.program_id(0)
        valid = _attn_masks(blk)
        low = lax.broadcasted_iota(jnp.int32, (1, 2 * HEAD_DIM), 1) < HEAD_DIM
        kcat = jnp.concatenate([kp_ref[...], kc_ref[...]], axis=0)
        vcat = jnp.concatenate([vp_ref[...], vc_ref[...]], axis=0)
        for h in range(KV_HEADS):
            tl = slice((h // 2) * 128, (h // 2) * 128 + 128)
            mine = low if h % 2 == 0 else jnp.logical_not(low)
            kh = _bf(jnp.where(mine, kcat[:, tl], pltpu.roll(kcat[:, tl], HEAD_DIM, 1)))
            vh = _bf(jnp.where(mine, vcat[:, tl], pltpu.roll(vcat[:, tl], HEAD_DIM, 1)))
            for t in range(2):
                ql = slice((2 * h + t) * 128, (2 * h + t) * 128 + 128)
                q2 = q_ref[:, ql]
                outs = []
                for p in range(2):
                    qm = _bf(jnp.where(low if p == 0 else jnp.logical_not(low), q2, 0.0))
                    probs, _ = _attn_probs(_dot_nt(qm, kh), sink_ref[0, 4 * h + 2 * t + p], valid)
                    outs.append(_dot(_bf(probs), vh))
                o_ref[:, ql] = jnp.where(low, outs[0], outs[1])

    prev = lambda i: jnp.maximum(i - 1, 0)
    return pl.pallas_call(
        body, name="attn_fwd", grid=(nb,),
        in_specs=[pl.BlockSpec(memory_space=pltpu.SMEM),
                  pl.BlockSpec((ATT_BLOCK, D), lambda i: (i, COL_AQ // D)),
                  pl.BlockSpec((ATT_BLOCK, 256), lambda i: (prev(i), COL_AK // 256)),
                  pl.BlockSpec((ATT_BLOCK, 256), lambda i: (i, COL_AK // 256)),
                  pl.BlockSpec((ATT_BLOCK, 256), lambda i: (prev(i), COL_AV // 256)),
                  pl.BlockSpec((ATT_BLOCK, 256), lambda i: (i, COL_AV // 256))],
        out_specs=pl.BlockSpec((ATT_BLOCK, D), lambda i: (i, 0)),
        out_shape=jax.ShapeDtypeStruct((T, D), F32),
        compiler_params=_params(("arbitrary",)),
    )(sinks, proj, proj, proj, proj, proj)


def _attn_bwd(proj, sinks, o, do):
    T = proj.shape[0]
    nb = T // ATT_BLOCK
    W2 = 2 * ATT_BLOCK

    def body(sink_ref, q_ref, kp_ref, kc_ref, vp_ref, vc_ref, o_ref, do_ref,
             dq_ref, dk_ref, dv_ref, dsink_ref, ck_scr, cv_scr, nk_scr, nv_scr):
        blk = pl.program_id(0)

        @pl.when(blk == 0)
        def _():
            ck_scr[...] = jnp.zeros_like(ck_scr)
            cv_scr[...] = jnp.zeros_like(cv_scr)
            dsink_ref[...] = jnp.zeros_like(dsink_ref)

        @pl.when(blk < nb)
        def _():
            valid = _attn_masks(blk)
            low = lax.broadcasted_iota(jnp.int32, (1, 2 * HEAD_DIM), 1) < HEAD_DIM
            kcat = jnp.concatenate([kp_ref[...], kc_ref[...]], axis=0)
            vcat = jnp.concatenate([vp_ref[...], vc_ref[...]], axis=0)
            for h in range(KV_HEADS):
                tl = slice((h // 2) * 128, (h // 2) * 128 + 128)
                kbd = _head_pair_operand(kcat[:, tl], h % 2, low)
                vbd = _head_pair_operand(vcat[:, tl], h % 2, low)
                dkbd = jnp.zeros((2 * W2, 128), F32)
                dvbd = jnp.zeros((2 * W2, 128), F32)
                tiles = []
                for t in range(2):
                    ql = slice((2 * h + t) * 128, (2 * h + t) * 128 + 128)
                    q2 = _bf(q_ref[:, ql])
                    do2 = do_ref[:, ql]
                    do2_b = _bf(do2)
                    doo = do2 * o_ref[:, ql]
                    dsum0 = jnp.sum(jnp.where(low, doo, 0.0), axis=1, keepdims=True)
                    dsum1 = jnp.sum(jnp.where(low, 0.0, doo), axis=1, keepdims=True)
                    tiles.append((ql, q2, do2_b, dsum0, dsum1, _dot_nt(q2, kbd), _dot_nt(do2_b, vbd)))
                grads = []
                for t, (ql, q2, do2_b, dsum0, dsum1, s2, dp2) in enumerate(tiles):
                    head = 4 * h + 2 * t
                    p0, ps0 = _attn_probs(s2[:, 0:W2], sink_ref[0, head], valid)
                    p1, ps1 = _attn_probs(s2[:, W2:2 * W2], sink_ref[0, head + 1], valid)
                    ds2 = _bf(jnp.concatenate([p0 * (dp2[:, 0:W2] - dsum0), p1 * (dp2[:, W2:2 * W2] - dsum1)], axis=1) * SCALE)
                    grads.append((ds2, _bf(jnp.concatenate([p0, p1], axis=1))))
                    dsink_ref[head:head + 1, :] += jnp.zeros((1, 128), F32) - jnp.sum(ps0 * dsum0, axis=0, keepdims=True)
                    dsink_ref[head + 1:head + 2, :] += jnp.zeros((1, 128), F32) - jnp.sum(ps1 * dsum1, axis=0, keepdims=True)
                for (ql, q2, do2_b, _, _, _, _), (ds2, p2) in zip(tiles, grads):
                    dq_ref[:, ql] = _bf(_dot(ds2, kbd))
                    dkbd = dkbd + _dot_tn(ds2, q2)
                    dvbd = dvbd + _dot_tn(p2, do2_b)
                dk2 = jnp.where(low, dkbd[0:W2], dkbd[W2:2 * W2])
                dv2 = jnp.where(low, dvbd[0:W2], dvbd[W2:2 * W2])
                dk2 = dk2 + pltpu.roll(dk2, HEAD_DIM, 1)
                dv2 = dv2 + pltpu.roll(dv2, HEAD_DIM, 1)
                if h % 2 == 0:
                    keep_k, keep_v = dk2, dv2
                else:
                    nk_scr[:, tl] = jnp.where(low, keep_k, dk2)
                    nv_scr[:, tl] = jnp.where(low, keep_v, dv2)
            dk_ref[...] = _bf(ck_scr[...] + nk_scr[0:ATT_BLOCK, :])
            dv_ref[...] = _bf(cv_scr[...] + nv_scr[0:ATT_BLOCK, :])
            ck_scr[...] = nk_scr[ATT_BLOCK:2 * ATT_BLOCK, :]
            cv_scr[...] = nv_scr[ATT_BLOCK:2 * ATT_BLOCK, :]

        @pl.when(blk == nb)
        def _():
            dk_ref[...] = _bf(ck_scr[...])
            dv_ref[...] = _bf(cv_scr[...])

    cur = lambda i: jnp.minimum(i, nb - 1)
    prev = lambda i: jnp.maximum(cur(i) - 1, 0)
    late = lambda i: jnp.maximum(i - 1, 0)
    dq, dk, dv, dsink = pl.pallas_call(
        body, name="attn_bwd", grid=(nb + 1,),
        in_specs=[pl.BlockSpec(memory_space=pltpu.SMEM),
                  pl.BlockSpec((ATT_BLOCK, D), lambda i: (cur(i), COL_AQ // D)),
                  pl.BlockSpec((ATT_BLOCK, 256), lambda i: (prev(i), COL_AK // 256)),
                  pl.BlockSpec((ATT_BLOCK, 256), lambda i: (cur(i), COL_AK // 256)),
                  pl.BlockSpec((ATT_BLOCK, 256), lambda i: (prev(i), COL_AV // 256)),
                  pl.BlockSpec((ATT_BLOCK, 256), lambda i: (cur(i), COL_AV // 256)),
                  pl.BlockSpec((ATT_BLOCK, D), lambda i: (cur(i), 0)),
                  pl.BlockSpec((ATT_BLOCK, D), lambda i: (cur(i), 0))],
        out_specs=[pl.BlockSpec((ATT_BLOCK, D), lambda i: (cur(i), 0)),
                   pl.BlockSpec((ATT_BLOCK, 256), lambda i: (late(i), 0)),
                   pl.BlockSpec((ATT_BLOCK, 256), lambda i: (late(i), 0)),
                   pl.BlockSpec((16, 128), lambda i: (0, 0))],
        out_shape=[jax.ShapeDtypeStruct((T, D), BF16), jax.ShapeDtypeStruct((T, 256), BF16),
                   jax.ShapeDtypeStruct((T, 256), BF16), jax.ShapeDtypeStruct((16, 128), F32)],
        scratch_shapes=[pltpu.VMEM((ATT_BLOCK, 256), F32), pltpu.VMEM((ATT_BLOCK, 256), F32),
                        pltpu.VMEM((2 * ATT_BLOCK, 256), F32), pltpu.VMEM((2 * ATT_BLOCK, 256), F32)],
        compiler_params=_params(("arbitrary",)),
    )(sinks, proj, proj, proj, proj, proj, o, do)
    return dq, dk, dv, dsink


def _mid(x, tgt, proj, oh, oa, hnw, fnw, wsq_bf):
    T = x.shape[0]
    tm = min(256, T)
    nt = T // tm

    def body(x_ref, tgt_ref, oh_ref, oa_ref, hg_ref, ag0_ref, ag1_ref, mh0_ref, mh1_ref, ma0_ref, ma1_ref,
             hnw_ref, fnw_ref, w_hbm,
             dx2_ref, doh_ref, doa_ref, dhg_ref, dtail_ref, lhs_ref, rhs_ref, loss_ref, vec_ref,
             w_scr, xh_scr, rs_scr, sem):
        @pl.when(pl.program_id(0) == 0)
        def _():
            cp = pltpu.make_async_copy(w_hbm, w_scr, sem)
            cp.start()
            cp.wait()
            loss_ref[...] = jnp.zeros_like(loss_ref)
            vec_ref[...] = jnp.zeros_like(vec_ref)

        oh = oh_ref[...]
        for h in range(HEADS):
            sl = slice(h * HEAD_W, (h + 1) * HEAD_W)
            ohh = oh[:, sl]
            rs = lax.rsqrt(jnp.mean(ohh * ohh, axis=1, keepdims=True) + EPS)
            xh_scr[:, sl] = ohh * rs
            rs_scr[:, sl] = jnp.broadcast_to(rs, (tm, HEAD_W))
        xh = xh_scr[...]
        hnw = hnw_ref[...]
        on = xh * hnw
        hg = hg_ref[...]
        sg = _sigmoid(hg)
        silu_g = hg * sg
        gated_h = _bf(on * silu_g)
        lhs_ref[0] = gated_h.T
        yh = _dot(gated_h, w_scr[0])
        oa = oa_ref[...]
        ag = jnp.concatenate([ag0_ref[...], ag1_ref[...]], axis=1)
        sa = _sigmoid(ag)
        silu_a = ag * sa
        gated_a = _bf(oa * silu_a)
        lhs_ref[1] = gated_a.T
        ya = _dot(gated_a, w_scr[1])
        smh = _sigmoid(jnp.concatenate([mh0_ref[...], mh1_ref[...]], axis=1))
        sma = _sigmoid(jnp.concatenate([ma0_ref[...], ma1_ref[...]], axis=1))
        merged = _bf(smh * yh + sma * ya)
        lhs_ref[2] = merged.T
        x2 = x_ref[...] + _dot(merged, w_scr[2])
        rs2 = lax.rsqrt(jnp.mean(x2 * x2, axis=1, keepdims=True) + EPS)
        xh2 = x2 * rs2
        fnw = fnw_ref[...]
        diff = xh2 * fnw - tgt_ref[...]
        loss_ref[...] += jnp.zeros_like(loss_ref) + jnp.sum(diff * diff) * (0.5 / D)

        dy = diff * (1.0 / D)
        vec_ref[0:1, :] += jnp.sum(dy * xh2, axis=0, keepdims=True)
        gy = dy * fnw
        dx2 = rs2 * (gy - xh2 * jnp.mean(gy * xh2, axis=1, keepdims=True))
        dx2_ref[...] = dx2
        dx2_b = _bf(dx2)
        rhs_ref[2] = dx2_b
        dmerged = _dot_nt(dx2_b, w_scr[2])
        dyh = dmerged * smh
        dya = dmerged * sma
        dtail_ref[:, D:2 * D] = _bf(dyh * yh * (1.0 - smh))
        dtail_ref[:, 2 * D:3 * D] = _bf(dya * ya * (1.0 - sma))
        dyh_b, dya_b = _bf(dyh), _bf(dya)
        rhs_ref[0] = dyh_b
        rhs_ref[1] = dya_b
        dgh = _dot_nt(dyh_b, w_scr[0])
        dga = _dot_nt(dya_b, w_scr[1])
        don = dgh * silu_g
        dhg_ref[...] = _bf(dgh * on * (sg * (1.0 + hg * (1.0 - sg))))
        vec_ref[1:2, :] += jnp.sum(don * xh, axis=0, keepdims=True)
        gxh = don * hnw
        rsb = rs_scr[...]
        for h in range(HEADS):
            sl = slice(h * HEAD_W, (h + 1) * HEAD_W)
            gh, xhh = gxh[:, sl], xh[:, sl]
            doh_ref[:, sl] = rsb[:, sl] * (gh - xhh * jnp.mean(gh * xhh, axis=1, keepdims=True))
        doa_ref[...] = dga * silu_a
        dtail_ref[:, 0:D] = _bf(dga * oa * (sa * (1.0 + ag * (1.0 - sa))))

    row = lambda w, j: pl.BlockSpec((tm, w), lambda i: (i, j))
    const = lambda r, c: pl.BlockSpec((r, c), lambda i: (0, 0))
    stack = pl.BlockSpec((3, tm, D), lambda i: (0, i, 0))
    stack_t = pl.BlockSpec((3, D, tm), lambda i: (0, 0, i))
    return pl.pallas_call(
        body, name="mid", grid=(nt,),
        in_specs=[row(D, 0), row(D, 0), row(D, 0), row(D, 0), row(D, COL_HG // D),
                  row(512, COL_AG // 512), row(512, COL_AG // 512 + 1),
                  row(512, COL_MH // 512), row(512, COL_MH // 512 + 1),
                  row(512, COL_MA // 512), row(512, COL_MA // 512 + 1),
                  const(1, D), const(1, D), HBM_SPEC],
        out_specs=[row(D, 0), row(D, 0), row(D, 0), row(D, 0), row(3 * D, 0), stack_t, stack, const(8, 128), const(8, D)],
        out_shape=[jax.ShapeDtypeStruct((T, D), F32), jax.ShapeDtypeStruct((T, D), F32), jax.ShapeDtypeStruct((T, D), F32),
                   jax.ShapeDtypeStruct((T, D), BF16), jax.ShapeDtypeStruct((T, 3 * D), BF16),
                   jax.ShapeDtypeStruct((3, D, T), BF16), jax.ShapeDtypeStruct((3, T, D), BF16),
                   jax.ShapeDtypeStruct((8, 128), F32), jax.ShapeDtypeStruct((8, D), F32)],
        scratch_shapes=[pltpu.VMEM((3, D, D), BF16), pltpu.VMEM((tm, D), F32), pltpu.VMEM((tm, D), F32),
                        pltpu.SemaphoreType.DMA],
        compiler_params=_params(("arbitrary",)),
    )(x, tgt, oh, oa, proj, proj, proj, proj, proj, proj, proj, hnw, fnw, wsq_bf)


def _wgrad_square(lhs_t, rhs):
    T = rhs.shape[1]
    tk = min(1024, T)

    def body(a_ref, b_ref, g_ref):
        part = _dot(a_ref[...], b_ref[...])

        @pl.when(pl.program_id(1) == 0)
        def _():
            g_ref[...] = part

        @pl.when(pl.program_id(1) > 0)
        def _():
            g_ref[...] += part

    return pl.pallas_call(
        body, name="wgrad_square", grid=(3, T // tk),
        in_specs=[pl.BlockSpec((None, D, tk), lambda k, i: (k, 0, i)), pl.BlockSpec((None, tk, D), lambda k, i: (k, i, 0))],
        out_specs=pl.BlockSpec((None, D, D), lambda k, i: (k, 0, 0)),
        out_shape=jax.ShapeDtypeStruct((3, D, D), F32),
        compiler_params=_params(("parallel", "arbitrary")),
    )(lhs_t, rhs)


def _bwd_dx(pieces, wt_bf, x, norm_w, dx2, swin_b, ssq_b):
    T = x.shape[0]
    tm = min(256, T)
    nt = T // tm
    widths = [p.shape[1] for p in pieces]
    n_p = len(pieces)

    def body(*refs):
        piece_refs = refs[:n_p]
        (w_hbm, x_ref, nw_ref, dx2_ref, swin_ref, ssq_ref,
         gx_ref, gnw_ref, win_got, sq_got, w_scr, sem, send_sems, recv_sems) = refs[n_p:]

        def scatter_copies():
            x_, y_, c_ = _place()
            copies = []
            for k, (fx, fy) in enumerate(CHIP_FLIPS):
                px, py = _flip(x_, fx), _flip(y_, fy)
                jr = 2 * px + py
                for a, (src, dst) in enumerate(((swin_ref.at[:, pl.ds(jr * SHARD_W, SHARD_W)], win_got.at[k]),
                                                (ssq_ref.at[:, pl.ds(jr * SQ_ROWS, SQ_ROWS), :], sq_got.at[k]))):
                    copies.append(pltpu.make_async_remote_copy(
                        src_ref=src, dst_ref=dst, send_sem=send_sems.at[2 * k + a], recv_sem=recv_sems.at[2 * k + a],
                        device_id=(px, py, c_), device_id_type=MESH))
            return copies

        @pl.when(pl.program_id(0) == 0)
        def _():
            for cp in scatter_copies():
                cp.start()
            cp = pltpu.make_async_copy(w_hbm, w_scr, sem)
            cp.start()
            cp.wait()
            gnw_ref[...] = jnp.zeros_like(gnw_ref)

        dxn = None
        off = 0
        for ref, w in zip(piece_refs, widths):
            part = _dot(ref[...], w_scr[off:off + w, :])
            dxn = part if dxn is None else dxn + part
            off += w
        xf = x_ref[...]
        rs = lax.rsqrt(jnp.mean(xf * xf, axis=1, keepdims=True) + EPS)
        xh = xf * rs
        gnw_ref[...] += jnp.sum(dxn * xh, axis=0, keepdims=True)
        gx = dxn * nw_ref[...]
        gx_ref[...] = rs * (gx - xh * jnp.mean(gx * xh, axis=1, keepdims=True)) + dx2_ref[...]

        @pl.when(pl.program_id(0) == nt - 1)
        def _():
            for cp in scatter_copies():
                cp.wait()

    row = lambda w: pl.BlockSpec((tm, w), lambda i: (i, 0))
    return pl.pallas_call(
        body, name="bwd_dx", grid=(nt,),
        in_specs=[row(w) for w in widths] + [HBM_SPEC, row(D), pl.BlockSpec((1, D), lambda i: (0, 0)), row(D), HBM_SPEC, HBM_SPEC],
        out_specs=[row(D), pl.BlockSpec((1, D), lambda i: (0, 0)), HBM_SPEC, HBM_SPEC],
        out_shape=[jax.ShapeDtypeStruct((T, D), F32), jax.ShapeDtypeStruct((1, D), F32),
                   jax.ShapeDtypeStruct((3, D // 2, SHARD_W), BF16), jax.ShapeDtypeStruct((3, 3, SQ_ROWS, D // 2), BF16)],
        scratch_shapes=[pltpu.VMEM((D_IN, D), BF16), pltpu.SemaphoreType.DMA,
                        pltpu.SemaphoreType.DMA((6,)), pltpu.SemaphoreType.DMA((6,))],
        compiler_params=_params(("arbitrary",)),
    )(*pieces, wt_bf, x, norm_w, dx2, swin_b, ssq_b)


W_PIECES = ((0, 1024, 3), (COL_HG, 1024, 1), (COL_AQ, 1024, 1), (COL_AK, 256, 1), (COL_AV, 256, 1), (COL_AG, 512, 6))


def _wgrad_in(xnt_bf, pieces):
    T = xnt_bf.shape[1]
    buf = None
    for n, (piece, (col, wb, blocks)) in enumerate(zip(pieces, W_PIECES)):
        first = buf is None
        tk = min(1024 if wb == 1024 else 2048, T)

        def body(xnt_ref, p_ref, *rest):
            g_ref = rest[-1]
            part = _dot(xnt_ref[...], p_ref[...])

            @pl.when(pl.program_id(1) == 0)
            def _():
                g_ref[...] = part

            @pl.when(pl.program_id(1) > 0)
            def _():
                g_ref[...] += part

        call = pl.pallas_call(
            body, name=f"wgrad_in_{n}", grid=(blocks, T // tk),
            in_specs=[pl.BlockSpec((D, tk), lambda jb, i: (0, i)), pl.BlockSpec((tk, wb), lambda jb, i: (i, jb))]
                     + ([] if first else [HBM_SPEC]),
            out_specs=pl.BlockSpec((D, wb), lambda jb, i, base=col // wb: (0, base + jb)),
            out_shape=jax.ShapeDtypeStruct((D, D_IN), F32),
            input_output_aliases={} if first else {2: 0},
            compiler_params=_params(("parallel", "arbitrary")),
        )
        buf = call(xnt_bf, piece) if first else call(xnt_bf, piece, buf)
    return buf


def _place():
    return lax.axis_index("x"), lax.axis_index("y"), lax.axis_index("c")


def _flip(v, f):
    return 1 - v if f else v


def _win_half(ref, h):
    return ref.at[pl.ds(h * (D // 2), D // 2), :]


def _sq_half(ref, h):
    return ref.at[:, pl.ds(h * (D // 2), D // 2)]


def _gather_copy(part, k, to, send_sems, recv_sems):
    return pltpu.make_async_remote_copy(src_ref=part, dst_ref=part, send_sem=send_sems.at[k], recv_sem=recv_sems.at[k],
                                        device_id=to, device_id_type=MESH)


def _gather_start(out, half, send_sems, recv_sems):
    x, y, c = _place()
    for k, (fx, fy) in enumerate(CHIP_FLIPS):
        _gather_copy(half(out.at[2 * x + y], c), k, (_flip(x, fx), _flip(y, fy), c), send_sems, recv_sems).start()


def _gather_land(out, half, k, send_sems, recv_sems):
    x, y, c = _place()
    sib = (x, y, 1 - c)
    fx, fy = CHIP_FLIPS[k]
    slot = out.at[2 * _flip(x, fx) + _flip(y, fy)]
    _gather_copy(half(slot, c), k, sib, send_sems, recv_sems).wait_recv()
    _gather_copy(half(slot, c), 3 + k, sib, send_sems, recv_sems).start()
    _gather_copy(half(slot, 1 - c), 3 + k, sib, send_sems, recv_sems).wait_recv()


def _gather_drain(out, half, send_sems, recv_sems):
    x, y, c = _place()
    for k, (fx, fy) in enumerate(CHIP_FLIPS):
        _gather_copy(half(out.at[2 * x + y], c), k, (_flip(x, fx), _flip(y, fy), c), send_sems, recv_sems).wait_send()
        _gather_copy(half(out.at[2 * _flip(x, fx) + _flip(y, fy)], c), 3 + k, (x, y, 1 - c), send_sems, recv_sems).wait_send()


def _gather_finish(out, half, send_sems, recv_sems):
    for k in range(len(CHIP_FLIPS)):
        _gather_land(out, half, k, send_sems, recv_sems)
    _gather_drain(out, half, send_sems, recv_sems)


def _swap_halves(gwin, gsq):
    def body(gwin_ref, gsq_ref, win_got, sq_got, send_sems, recv_sems):
        x, y, c = _place()
        sib = (x, y, 1 - c)
        pairs = ((_win_half(gwin_ref, 1 - c), win_got),
                 (gsq_ref.at[:, :, pl.ds((1 - c) * (D // 2), D // 2)], sq_got))
        copies = [pltpu.make_async_remote_copy(src_ref=src, dst_ref=dst, send_sem=send_sems.at[a], recv_sem=recv_sems.at[a],
                                               device_id=sib, device_id_type=MESH) for a, (src, dst) in enumerate(pairs)]
        for cp in copies:
            cp.start()
        for cp in copies:
            cp.wait()

    return pl.pallas_call(
        body, name="swap_halves",
        in_specs=[HBM_SPEC, HBM_SPEC], out_specs=[HBM_SPEC, HBM_SPEC],
        out_shape=[jax.ShapeDtypeStruct((D // 2, D_IN), F32), jax.ShapeDtypeStruct((3, D, D // 2), F32)],
        scratch_shapes=[pltpu.SemaphoreType.DMA((2,)), pltpu.SemaphoreType.DMA((2,))],
    )(gwin, gsq)


def _add_halves(c_arr, gwin, gsq, win_got, sq_got):
    def body(c_ref, a_ref, b_ref, p_ref, q_ref, so_ref, sq_ref, sob_ref, sqb_ref):
        so = a_ref[...] + b_ref[...]
        sq = p_ref[...] + q_ref[...]
        so_ref[...] = so
        sq_ref[...] = sq
        sob_ref[...] = _bf(so)
        sqb_ref[...] = _bf(sq)

    steps = 8
    rows, sq_rows = (D // 2) // steps, D // steps
    win = lambda f: pl.BlockSpec((rows, D_IN), f)
    sq = lambda f: pl.BlockSpec((3, sq_rows, D // 2), f)
    return pl.pallas_call(
        body, name="add_halves",
        grid_spec=pltpu.PrefetchScalarGridSpec(
            num_scalar_prefetch=1, grid=(steps,),
            in_specs=[win(lambda i, c: (c[0] * steps + i, 0)), win(lambda i, c: (i, 0)),
                      sq(lambda i, c: (0, i, c[0])), sq(lambda i, c: (0, i, 0))],
            out_specs=[win(lambda i, c: (i, 0)), sq(lambda i, c: (0, i, 0))] * 2),
        out_shape=[jax.ShapeDtypeStruct((D // 2, D_IN), F32), jax.ShapeDtypeStruct((3, D, D // 2), F32),
                   jax.ShapeDtypeStruct((D // 2, D_IN), BF16), jax.ShapeDtypeStruct((3, D, D // 2), BF16)],
        compiler_params=_params(("arbitrary",)),
    )(c_arr, gwin, win_got, gsq, sq_got)


def _sum_chips(jc_arr, swin, ssq, win_got, sq_got):
    def body(jc_ref, a_ref, b_ref, p_ref, q_ref, so_ref, sq_ref):
        so_ref[...] = ((a_ref[...] + b_ref[0].astype(F32)) + b_ref[1].astype(F32)) + b_ref[2].astype(F32)
        sq_ref[...] = ((p_ref[...] + q_ref[0].astype(F32)) + q_ref[1].astype(F32)) + q_ref[2].astype(F32)

    rows = 128
    steps = (D // 2) // rows
    sq_rows = SQ_ROWS // steps
    return pl.pallas_call(
        body, name="sum_chips",
        grid_spec=pltpu.PrefetchScalarGridSpec(
            num_scalar_prefetch=1, grid=(steps,),
            in_specs=[pl.BlockSpec((rows, SHARD_W), lambda i, jc: (i, jc[0])),
                      pl.BlockSpec((3, rows, SHARD_W), lambda i, jc: (0, i, 0)),
                      pl.BlockSpec((3, sq_rows, D // 2), lambda i, jc: (0, jc[0] * steps + i, 0)),
                      pl.BlockSpec((3, 3, sq_rows, D // 2), lambda i, jc: (0, 0, i, 0))],
            out_specs=[pl.BlockSpec((rows, SHARD_W), lambda i, jc: (jc[1] * steps + i, 0)),
                       pl.BlockSpec((3, sq_rows, D // 2), lambda i, jc: (0, i, jc[1]))]),
        out_shape=[jax.ShapeDtypeStruct((D, SHARD_W), F32), jax.ShapeDtypeStruct((3, SQ_ROWS, D), F32)],
        compiler_params=_params(("arbitrary",)),
    )(jc_arr, swin, win_got, ssq, sq_got)


def _join_halves(g_win, g_sq):
    def body(win_in, sq_in, win_out, sq_out, send_sems, recv_sems):
        del win_in, sq_in
        x, y, c = _place()
        sib = (x, y, 1 - c)

        def halves(h):
            return _win_half(win_out, h), sq_out.at[:, :, pl.ds(h * (D // 2), D // 2)]

        def copy(a, part):
            return pltpu.make_async_remote_copy(src_ref=part, dst_ref=part, send_sem=send_sems.at[a], recv_sem=recv_sems.at[a],
                                                device_id=sib, device_id_type=MESH)

        sent = [copy(a, part) for a, part in enumerate(halves(c))]
        for cp in sent:
            cp.start()
        for a, part in enumerate(halves(1 - c)):
            copy(a, part).wait_recv()
        for cp in sent:
            cp.wait_send()

    return pl.pallas_call(
        body, name="join_halves",
        in_specs=[HBM_SPEC, HBM_SPEC], out_specs=[HBM_SPEC, HBM_SPEC], input_output_aliases={0: 0, 1: 1},
        out_shape=[jax.ShapeDtypeStruct((D, SHARD_W), F32), jax.ShapeDtypeStruct((3, SQ_ROWS, D), F32)],
        scratch_shapes=[pltpu.SemaphoreType.DMA((2,)), pltpu.SemaphoreType.DMA((2,))],
    )(g_win, g_sq)


def _allreduce_small(vec):
    def body(vec_ref, out_ref, slots, send_sems, recv_sems):
        x, y, c = _place()
        me = 4 * x + 2 * y + c
        slots[me] = vec_ref[...]
        copies = []
        for k in range(1, 8):
            fx, fy, fc = (k >> 2) & 1, (k >> 1) & 1, k & 1
            copies.append(pltpu.make_async_remote_copy(
                src_ref=vec_ref, dst_ref=slots.at[me], send_sem=send_sems.at[k - 1], recv_sem=recv_sems.at[k - 1],
                device_id=(_flip(x, fx), _flip(y, fy), _flip(c, fc)), device_id_type=MESH))
        for cp in copies:
            cp.start()
        for k in range(1, 8):
            fx, fy, fc = (k >> 2) & 1, (k >> 1) & 1, k & 1
            src = 4 * _flip(x, fx) + 2 * _flip(y, fy) + _flip(c, fc)
            pltpu.make_async_remote_copy(src_ref=vec_ref, dst_ref=slots.at[src], send_sem=send_sems.at[k - 1],
                                         recv_sem=recv_sems.at[k - 1], device_id=(x, y, c), device_id_type=MESH).wait_recv()
        for cp in copies:
            cp.wait_send()
        total = slots[0]
        for s in range(1, 8):
            total = total + slots[s]
        out_ref[...] = total

    return pl.pallas_call(
        body, name="allreduce_small",
        in_specs=[pl.BlockSpec(memory_space=pltpu.VMEM)], out_specs=pl.BlockSpec(memory_space=pltpu.VMEM),
        out_shape=jax.ShapeDtypeStruct((8, D), F32),
        scratch_shapes=[pltpu.VMEM((8, 8, D), F32), pltpu.SemaphoreType.DMA((7,)), pltpu.SemaphoreType.DMA((7,))],
    )(vec)


def _adamw_math(w, g, m, v):
    m = ADAM_B1 * m + (1.0 - ADAM_B1) * g
    v = ADAM_B2 * v + (1.0 - ADAM_B2) * (g * g)
    m_hat = m / (1.0 - ADAM_B1 ** ADAM_STEP)
    v_hat = v / (1.0 - ADAM_B2 ** ADAM_STEP)
    delta = -ADAM_LR * (m_hat / (jnp.sqrt(v_hat) + ADAM_EPS) + ADAM_WD * w)
    return delta, m, v


def _adamw(name, w, g, m, v, rows):
    R, C = w.shape

    def body(w_ref, g_ref, m_ref, v_ref, d_out, m_out, v_out):
        d_out[...], m_out[...], v_out[...] = _adamw_math(w_ref[...], g_ref[...], m_ref[...], v_ref[...])

    spec = pl.BlockSpec((rows, C), lambda i: (i, 0))
    return pl.pallas_call(
        body, name=name, grid=(R // rows,), in_specs=[spec] * 4, out_specs=[spec] * 3,
        out_shape=[jax.ShapeDtypeStruct((R, C), F32)] * 3,
        compiler_params=_params(("parallel",)),
    )(w, g, m, v)


def _small_update(total, lbw, w8, m8, v8):
    def body(t_ref, lbw_ref, w_ref, m_ref, v_ref, g_out, d_out, m_out, v_out):
        lb = 1.0 / (1.0 + jnp.exp(lbw_ref[1:2, :] - lbw_ref[0:1, :]))
        dlb = t_ref[2:3, :] * lb * (1.0 - lb)
        g_out[...] = jnp.zeros_like(g_out)
        g_out[0:1, :] = t_ref[3:4, :]
        g_out[1:2, :] = dlb
        g_out[2:3, :] = -dlb
        g_out[3:4, :] = t_ref[1:2, :]
        g_out[4:5, :] = t_ref[0:1, :]
        g_out[5:6, :] = t_ref[4:5, :]
        d_out[...], m_out[...], v_out[...] = _adamw_math(w_ref[...], g_out[...], m_ref[...], v_ref[...])

    return pl.pallas_call(
        body, name="small_update", out_shape=[jax.ShapeDtypeStruct((8, D), F32)] * 4,
        compiler_params=_params(),
    )(total, lbw, w8, m8, v8)


def _pack8(norm_w, lbw, hnw, fnw, sinks):
    pad = jnp.zeros((1, D - 16), F32)
    return jnp.concatenate([norm_w, lbw, hnw, fnw.reshape(1, D), jnp.concatenate([sinks, pad], axis=1),
                            jnp.zeros((2, D), F32)], axis=0)


def _unpack8(a):
    return a[0:1], a[1:3], a[3:4], a[5:6, 0:16], a[4]


def _local_step(order_arr, x, tgt, norm_w, lbw, hnw, sinks, fnw, win_mine, wsq_mine, exchange):
    proj, xnt_bf, win_bf = _fwd_proj(order_arr, x, norm_w, win_mine)
    oh, states, wsq_all = _hgrn_fwd(proj, lbw, wsq_mine)
    wsq_bf = wsq_all.reshape(SHARDS, 3, SQ_ROWS, D).transpose(1, 0, 2, 3).reshape(3, D, D)
    oa = _attn_fwd(proj, sinks)
    dx2, doh, doa, dhg, dtail, lhs, rhs, loss8, vec_mid = _mid(x, tgt, proj, oh, oa, hnw, fnw.reshape(1, D), wsq_bf)
    gsq = _wgrad_square(lhs, rhs)
    dhead, dlb = _hgrn_bwd(proj, lbw, states, doh)
    daq, dak, dav, dsink = _attn_bwd(proj, sinks, oa, doa)
    pieces = [dhead, dhg, daq, dak, dav, dtail]
    sums = exchange(_wgrad_in(xnt_bf, pieces), gsq)
    wt_bf = win_bf.transpose(0, 2, 1).reshape(D_IN, D)
    grad_x, gnw, win_got, sq_got = _bwd_dx(pieces, wt_bf, x, norm_w, dx2, sums[2], sums[3])
    sink_row = jnp.concatenate([dsink[:, 0].reshape(1, 16), jnp.zeros((1, D - 16), F32)], axis=1)
    vec = jnp.concatenate([vec_mid[0:2], dlb, gnw, sink_row, jnp.zeros((3, D), F32)], axis=0)
    return loss8[0, 0], grad_x, sums, (win_got, sq_got), vec


def kernel(x, norm_w, w_in, hgrn_lower_bound, hgrn_norm_w, w_branch_hgrn, attn_sinks, w_branch_attn, w_out, final_norm_w, loss_target, m_norm_w, m_w_in, m_hgrn_lower_bound, m_hgrn_norm_w, m_w_branch_hgrn, m_attn_sinks, m_w_branch_attn, m_w_out, m_final_norm_w, v_norm_w, v_w_in, v_hgrn_lower_bound, v_hgrn_norm_w, v_w_branch_hgrn, v_attn_sinks, v_w_branch_attn, v_w_out, v_final_norm_w):
    c_arr = lax.axis_index("c").astype(jnp.int32).reshape(1)
    j_arr = (2 * lax.axis_index("x") + lax.axis_index("y")).astype(jnp.int32).reshape(1)
    jc_arr = jnp.concatenate([j_arr, c_arr])

    win_mine, wsq_mine = _cast_shards(j_arr, w_in[0], w_branch_hgrn[0], w_branch_attn[0], w_out[0])
    xi, yi = lax.axis_index("x"), lax.axis_index("y")
    order_arr = jnp.stack([2 * xi + yi] + [2 * _flip(xi, fx) + _flip(yi, fy) for fx, fy in CHIP_FLIPS]).astype(jnp.int32)

    def chip_sums(gwin, gsq):
        return _add_halves(c_arr, gwin, gsq, *_swap_halves(gwin, gsq))

    loss_part, grad_x, (swin, ssq, _, _), arrived, vec = _local_step(
        order_arr, x[0], loss_target[0], norm_w, hgrn_lower_bound, hgrn_norm_w, attn_sinks, final_norm_w, win_mine, wsq_mine,
        chip_sums)
    loss = lax.psum(loss_part, ("x", "y", "c"))
    g_win, g_sq = _join_halves(*_sum_chips(jc_arr, swin, ssq, *arrived))

    d_win, nm_win, nv_win = _adamw("adamw_w_in", w_in[0], g_win, m_w_in[0], v_w_in[0], 128)
    sq_w = jnp.concatenate([w_branch_hgrn[0], w_branch_attn[0], w_out[0]], axis=0)
    sq_m = jnp.concatenate([m_w_branch_hgrn[0], m_w_branch_attn[0], m_w_out[0]], axis=0)
    sq_v = jnp.concatenate([v_w_branch_hgrn[0], v_w_branch_attn[0], v_w_out[0]], axis=0)
    d_sq, nm_sq, nv_sq = _adamw("adamw_square", sq_w, g_sq.reshape(3 * SQ_ROWS, D), sq_m, sq_v, 256)

    total = _allreduce_small(vec)
    g8, d8, nm8, nv8 = _small_update(
        total, hgrn_lower_bound,
        _pack8(norm_w, hgrn_lower_bound, hgrn_norm_w, final_norm_w, attn_sinks),
        _pack8(m_norm_w, m_hgrn_lower_bound, m_hgrn_norm_w, m_final_norm_w, m_attn_sinks),
        _pack8(v_norm_w, v_hgrn_lower_bound, v_hgrn_norm_w, v_final_norm_w, v_attn_sinks))

    def assemble(win, sq, small):
        nw, lb, hn, sk, fn = _unpack8(small)
        sq = sq.reshape(3, 1, SQ_ROWS, D)
        return (nw, win.reshape(1, D, SHARD_W), lb, hn, sq[0], sk, sq[1], sq[2], fn)

    return (loss, grad_x.reshape(1, -1, D),
            *assemble(g_win, g_sq, g8), *assemble(d_win, d_sq, d8),
            *assemble(nm_win, nm_sq, nm8), *assemble(nv_win, nv_sq, nv8))
```

```python
import functools

import jax
import jax.numpy as jnp
from jax import lax
from jax.experimental import pallas as pl
from jax.experimental.pallas import tpu as pltpu

F32 = jnp.float32
BF16 = jnp.bfloat16

D = 1024
D_IN = 8704
SHARDS = 4
SHARD_W = D_IN // SHARDS
SQ_ROWS = D // SHARDS
HEADS = 8
HEAD_W = 128
CHUNK = 64
SUB = 4
ATT_BLOCK = 128
KV_HEADS = 4
HEAD_DIM = 64
EPS = 1e-6
NEG = -1e30
SCALE = HEAD_DIM ** -0.5
COL_HG, COL_AQ, COL_AK, COL_AV, COL_AG, COL_MH, COL_MA = 3072, 4096, 5120, 5376, 5632, 6656, 7680

ADAM_LR, ADAM_B1, ADAM_B2, ADAM_EPS, ADAM_WD, ADAM_STEP = 0.001, 0.9, 0.999, 1e-08, 0.01, 10

VMEM_LIMIT = 56 * 1024 * 1024
MESH = pl.DeviceIdType.MESH
HBM_SPEC = pl.BlockSpec(memory_space=pltpu.HBM)
CHIP_FLIPS = ((1, 0), (0, 1), (1, 1))


def _dot(a, b):
    return jnp.dot(a, b, preferred_element_type=F32)


def _dot_nt(a, b):
    return lax.dot_general(a, b, (((1,), (1,)), ((), ())), preferred_element_type=F32)


def _dot_tn(a, b):
    return lax.dot_general(a, b, (((0,), (0,)), ((), ())), preferred_element_type=F32)


def _sigmoid(v):
    return 1.0 / (1.0 + jnp.exp(-v))


def _bf(v):
    return v.astype(BF16)


def _split3(v):
    a = _bf(v)
    r = v - a.astype(F32)
    b = _bf(r)
    c = _bf(r - b.astype(F32))
    return a, b, c


def _tri_dot(tri, v):
    a, b, c = _split3(v)
    return _dot(tri, a) + _dot(tri, b) + _dot(tri, c)


def _params(sem=None):
    return pltpu.CompilerParams(dimension_semantics=sem, vmem_limit_bytes=VMEM_LIMIT)


def _cast_shards(j_arr, win_s, wbh_s, wba_s, wout_s):
    steps = 4
    rows = D // steps

    def body(j_ref, win_ref, a_ref, b_ref, c_ref, win_o, sq_o):
        win_o[...] = _bf(win_ref[...])

        @pl.when(pl.program_id(0) == 0)
        def _():
            sq_o[0:SQ_ROWS, :] = _bf(a_ref[...])
            sq_o[SQ_ROWS:2 * SQ_ROWS, :] = _bf(b_ref[...])
            sq_o[2 * SQ_ROWS:3 * SQ_ROWS, :] = _bf(c_ref[...])

    whole = pl.BlockSpec((SQ_ROWS, D), lambda i, j: (0, 0))
    return pl.pallas_call(
        body, name="cast_shards",
        grid_spec=pltpu.PrefetchScalarGridSpec(
            num_scalar_prefetch=1, grid=(steps,),
            in_specs=[pl.BlockSpec((rows, SHARD_W), lambda i, j: (i, 0)), whole, whole, whole],
            out_specs=[pl.BlockSpec((None, rows, SHARD_W), lambda i, j: (j[0], i, 0)),
                       pl.BlockSpec((None, 3 * SQ_ROWS, D), lambda i, j: (j[0], 0, 0))]),
        out_shape=[jax.ShapeDtypeStruct((SHARDS, D, SHARD_W), BF16), jax.ShapeDtypeStruct((SHARDS, 3 * SQ_ROWS, D), BF16)],
        compiler_params=_params(("arbitrary",)),
    )(j_arr, win_s, wbh_s, wba_s, wout_s)


def _fwd_proj(order_arr, x, norm_w, win_all):
    T = x.shape[0]
    tm = min(512, T)
    nt = T // tm

    def body(order_ref, x_ref, nw_ref, win_in, proj_ref, xn_ref, win_out, w_scr, sem, send_sems, recv_sems):
        del win_in
        p, i = pl.program_id(0), pl.program_id(1)

        def load(slot):
            cp = pltpu.make_async_copy(win_out.at[slot], w_scr, sem)
            cp.start()
            cp.wait()

        @pl.when((p == 0) & (i == 0))
        def _():
            _gather_start(win_out, _win_half, send_sems, recv_sems, chips=(0, 1))
            load(order_ref[0])

        for k in range(SHARDS - 1):
            @pl.when((p == k + 1) & (i == 0))
            def _():
                _gather_land(win_out, _win_half, k, send_sems, recv_sems)
                if k == 0:
                    _gather_start(win_out, _win_half, send_sems, recv_sems, chips=(2,))
                load(order_ref[k + 1])

        xf = x_ref[...]
        rs = lax.rsqrt(jnp.mean(xf * xf, axis=1, keepdims=True) + EPS)
        xn = _bf((xf * rs) * nw_ref[...])

        @pl.when(p == 0)
        def _():
            xn_ref[...] = xn.T

        proj_ref[...] = _dot(xn, w_scr[...])

        @pl.when((p == SHARDS - 1) & (i == nt - 1))
        def _():
            _gather_drain(win_out, _win_half, send_sems, recv_sems)

    return pl.pallas_call(
        body, name="fwd_proj",
        grid_spec=pltpu.PrefetchScalarGridSpec(
            num_scalar_prefetch=1, grid=(SHARDS, nt),
            in_specs=[pl.BlockSpec((tm, D), lambda p, i, order: (i, 0)), pl.BlockSpec((1, D), lambda p, i, order: (0, 0)), HBM_SPEC],
            out_specs=[pl.BlockSpec((tm, SHARD_W), lambda p, i, order: (i, order[p])),
                       pl.BlockSpec((D, tm), lambda p, i, order: (0, jnp.where(p == 0, i, nt - 1))),
                       HBM_SPEC],
            scratch_shapes=[pltpu.VMEM((D, SHARD_W), BF16), pltpu.SemaphoreType.DMA,
                            pltpu.SemaphoreType.DMA((6,)), pltpu.SemaphoreType.DMA((6,))]),
        out_shape=[jax.ShapeDtypeStruct((T, D_IN), F32), jax.ShapeDtypeStruct((D, T), BF16),
                   jax.ShapeDtypeStruct((SHARDS, D, SHARD_W), BF16)],
        input_output_aliases={3: 2},
        compiler_params=_params(("arbitrary", "arbitrary")),
    )(order_arr, x, norm_w, win_all)


def _hgrn_gates(hq_ref, hf_ref, lbw_ref, b_scr):
    lb = 1.0 / (1.0 + jnp.exp(lbw_ref[1:2, :] - lbw_ref[0:1, :]))
    hf = hf_ref[...]
    sig = _sigmoid(hf)
    f = lb + (1.0 - lb) * sig
    g = jnp.log(f)
    hq = hq_ref[...]
    sq = _sigmoid(hq)
    q = hq * sq
    row = lax.broadcasted_iota(jnp.int32, (CHUNK, CHUNK), 0)
    col = lax.broadcasted_iota(jnp.int32, (CHUNK, CHUNK), 1)
    causal = row >= col
    b = _tri_dot(jnp.where(causal, 1.0, 0.0).astype(BF16), g)
    b_scr[...] = b
    bc = b_scr[CHUNK - 1:CHUNK, :]
    r = b_scr[CHUNK // 2 - 1:CHUNK // 2, :]
    return dict(lb=lb, sig=sig, f=f, k=1.0 - f, hq=hq, sq=sq, q=q, b=b, bc=bc, r=r, causal=causal)


def _hgrn_fwd(proj, lbw, wsq_all):
    T = proj.shape[0]
    n = T // CHUNK

    def body(hq_ref, hf_ref, hi_ref, lbw_ref, wsq_in, o_ref, st_ref, wsq_out, s_scr, b_scr, send_sems, recv_sems):
        del wsq_in

        @pl.when(pl.program_id(0) == 0)
        def _():
            _gather_start(wsq_out, _sq_half, send_sems, recv_sems)
            s_scr[...] = jnp.zeros_like(s_scr)

        for c in range(SUB):
            rows = pl.ds(c * CHUNK, CHUNK)
            gt = _hgrn_gates(hq_ref.at[rows, :], hf_ref.at[rows, :], lbw_ref, b_scr.at[rows, :])
            b, bc, r, q, k = gt["b"], gt["bc"], gt["r"], gt["q"], gt["k"]
            qe = _bf(q * jnp.exp(b))
            qr = _bf(q * jnp.exp(b - r))
            kr = _bf(k * jnp.exp(r - b))
            kl = _bf(k * jnp.exp(bc - b))
            ebc = jnp.exp(bc)
            v = _bf(hi_ref[rows, :])
            scores = [_bf(jnp.where(gt["causal"], _dot_nt(qr[:, h * HEAD_W:(h + 1) * HEAD_W], kr[:, h * HEAD_W:(h + 1) * HEAD_W]), 0.0))
                      for h in range(HEADS)]
            for h in range(HEADS):
                sl = slice(h * HEAD_W, (h + 1) * HEAD_W)
                st = s_scr[h]
                st_ref[c, h] = st
                o_ref[rows, sl] = _dot(scores[h], v[:, sl]) + _dot_nt(qe[:, sl], _bf(st))
                s_scr[h] = ebc[:, sl] * st + _dot_tn(v[:, sl], kl[:, sl])

        @pl.when(pl.program_id(0) == n // SUB - 1)
        def _():
            _gather_finish(wsq_out, _sq_half, send_sems, recv_sems)

    col = lambda j: pl.BlockSpec((SUB * CHUNK, D), lambda i: (i, j))
    return pl.pallas_call(
        body, name="hgrn_fwd", grid=(n // SUB,),
        in_specs=[col(0), col(1), col(2), pl.BlockSpec((2, D), lambda i: (0, 0)), HBM_SPEC],
        out_specs=[pl.BlockSpec((SUB * CHUNK, D), lambda i: (i, 0)),
                   pl.BlockSpec((SUB, HEADS, HEAD_W, HEAD_W), lambda i: (i, 0, 0, 0)), HBM_SPEC],
        out_shape=[jax.ShapeDtypeStruct((T, D), F32), jax.ShapeDtypeStruct((n, HEADS, HEAD_W, HEAD_W), F32),
                   jax.ShapeDtypeStruct((SHARDS, 3 * SQ_ROWS, D), BF16)],
        input_output_aliases={4: 2},
        scratch_shapes=[pltpu.VMEM((HEADS, HEAD_W, HEAD_W), F32), pltpu.VMEM((SUB * CHUNK, D), F32),
                        pltpu.SemaphoreType.DMA((6,)), pltpu.SemaphoreType.DMA((6,))],
        compiler_params=_params(("arbitrary",)),
    )(proj, proj, proj, lbw, wsq_all)


def _hgrn_bwd(proj, lbw, states, do):
    T = proj.shape[0]
    n = T // CHUNK

    def body(hq_ref, hf_ref, hi_ref, lbw_ref, st_ref, do_ref, dp_ref, dlb_ref,
             ds_scr, b_scr, dq_scr, dk_scr, dv_scr, late_scr, early_scr, ex_scr):
        @pl.when(pl.program_id(0) == 0)
        def _():
            ds_scr[...] = jnp.zeros_like(ds_scr)
            dlb_ref[...] = jnp.zeros_like(dlb_ref)

        for c in reversed(range(SUB)):
            rows = pl.ds(c * CHUNK, CHUNK)
            gt = _hgrn_gates(hq_ref.at[rows, :], hf_ref.at[rows, :], lbw_ref, b_scr.at[rows, :])
            b, bc, r, q, k = gt["b"], gt["bc"], gt["r"], gt["q"], gt["k"]
            eb = jnp.exp(b)
            er = jnp.exp(b - r)
            erk = jnp.exp(r - b)
            el = jnp.exp(bc - b)
            ebc = jnp.exp(bc)
            qe, qr, kr, kl = _bf(q * eb), _bf(q * er), _bf(k * erk), _bf(k * el)
            v = _bf(hi_ref[rows, :])
            do_b = do_ref[rows, :]
            do_t = do_b.T
            causal_t = lax.broadcasted_iota(jnp.int32, (CHUNK, CHUNK), 0) <= lax.broadcasted_iota(jnp.int32, (CHUNK, CHUNK), 1)
            firsts = []
            for h in range(HEADS):
                sl = slice(h * HEAD_W, (h + 1) * HEAD_W)
                firsts.append((_bf(jnp.where(causal_t, _dot_nt(kr[:, sl], qr[:, sl]), 0.0)),
                               _bf(jnp.where(gt["causal"], _dot_nt(do_b[:, sl], v[:, sl]), 0.0)),
                               _bf(jnp.where(causal_t, _dot_nt(v[:, sl], do_b[:, sl]), 0.0))))
            for h in range(HEADS):
                sl = slice(h * HEAD_W, (h + 1) * HEAD_W)
                st0 = st_ref[c, h]
                dst = ds_scr[h]
                dst_b = _bf(dst)
                a_t, da, da_t = firsts[h]
                mq = _dot(da, kr[:, sl])
                mk = _dot(da_t, qr[:, sl])
                dq_in = eb[:, sl] * _dot(do_b[:, sl], _bf(st0))
                dk_in = el[:, sl] * _dot(v[:, sl], dst_b)
                dq_scr[rows, sl] = er[:, sl] * mq + dq_in
                dk_scr[rows, sl] = erk[:, sl] * mk + dk_in
                dv_scr[rows, sl] = _dot(a_t, do_b[:, sl]) + _dot_nt(kl[:, sl], dst_b)
                late_scr[rows, sl] = q[:, sl] * dq_in + qr[:, sl].astype(F32) * mq - kr[:, sl].astype(F32) * mk
                early_scr[rows, sl] = k[:, sl] * dk_in
                ex_scr[:, sl] = jnp.sum(dst * st0, axis=0, keepdims=True)
                ds_scr[h] = ebc[:, sl] * dst + _dot(do_t[sl, :], qe[:, sl])

            dq, dk = dq_scr[rows, :], dk_scr[rows, :]
            row = lax.broadcasted_iota(jnp.int32, (CHUNK, CHUNK), 0)
            col = lax.broadcasted_iota(jnp.int32, (CHUNK, CHUNK), 1)
            at_or_after = jnp.where(col >= row, 1.0, 0.0).astype(BF16)
            before = jnp.where(col < row, 1.0, 0.0).astype(BF16)
            dg = _tri_dot(at_or_after, late_scr[rows, :]) + _tri_dot(before, early_scr[rows, :]) + ebc * ex_scr[...]
            df = dg / gt["f"] - dk
            sig, sq, hq, lb = gt["sig"], gt["sq"], gt["hq"], gt["lb"]
            dp_ref[rows, 0:D] = _bf(dq * (sq * (1.0 + hq * (1.0 - sq))))
            dp_ref[rows, D:2 * D] = _bf(df * (1.0 - lb) * sig * (1.0 - sig))
            dp_ref[rows, 2 * D:3 * D] = _bf(dv_scr[rows, :])
            dlb_ref[...] += jnp.sum(df * (1.0 - sig), axis=0, keepdims=True)

    ns = n // SUB
    col = lambda j: pl.BlockSpec((SUB * CHUNK, D), lambda i: (ns - 1 - i, j))
    return pl.pallas_call(
        body, name="hgrn_bwd", grid=(ns,),
        in_specs=[col(0), col(1), col(2), pl.BlockSpec((2, D), lambda i: (0, 0)),
                  pl.BlockSpec((SUB, HEADS, HEAD_W, HEAD_W), lambda i: (ns - 1 - i, 0, 0, 0)),
                  pl.BlockSpec((SUB * CHUNK, D), lambda i: (ns - 1 - i, 0))],
        out_specs=[pl.BlockSpec((SUB * CHUNK, 3 * D), lambda i: (ns - 1 - i, 0)),
                   pl.BlockSpec((1, D), lambda i: (0, 0))],
        out_shape=[jax.ShapeDtypeStruct((T, 3 * D), BF16), jax.ShapeDtypeStruct((1, D), F32)],
        scratch_shapes=[pltpu.VMEM((HEADS, HEAD_W, HEAD_W), F32)] + [pltpu.VMEM((SUB * CHUNK, D), F32)] * 6
                       + [pltpu.VMEM((1, D), F32)],
        compiler_params=_params(("arbitrary",)),
    )(proj, proj, proj, lbw, states, do)


def _attn_masks(blk):
    qi = lax.broadcasted_iota(jnp.int32, (ATT_BLOCK, 2 * ATT_BLOCK), 0)
    kj = lax.broadcasted_iota(jnp.int32, (ATT_BLOCK, 2 * ATT_BLOCK), 1)
    band = (kj > qi) & (kj <= qi + ATT_BLOCK)
    return band & ((blk > 0) | (kj >= ATT_BLOCK))


def _head_pair_operand(t, hp, low):
    mine = low if hp == 0 else jnp.logical_not(low)
    both = jnp.where(mine, t, pltpu.roll(t, HEAD_DIM, 1))
    return _bf(jnp.concatenate([jnp.where(low, both, 0.0), jnp.where(low, 0.0, both)], axis=0))


def _attn_probs(s, sink, valid):
    s = jnp.where(valid, s * SCALE, NEG)
    m = jnp.maximum(jnp.max(s, axis=1, keepdims=True), sink)
    p = jnp.exp(s - m)
    es = jnp.exp(sink - m)
    inv = 1.0 / (jnp.sum(p, axis=1, keepdims=True) + es)
    return p * inv, es * inv


def _attn_fwd(proj, sinks):
    T = proj.shape[0]
    nb = T // ATT_BLOCK
    W2 = 2 * ATT_BLOCK

    def body(sink_ref, q_ref, kp_ref, kc_ref, vp_ref, vc_ref, o_ref):
        blk = pl.program_id(0)
        valid = _attn_masks(blk)
        low = lax.broadcasted_iota(jnp.int32, (1, 2 * HEAD_DIM), 1) < HEAD_DIM
        kcat = jnp.concatenate([kp_ref[...], kc_ref[...]], axis=0)
        vcat = jnp.concatenate([vp_ref[...], vc_ref[...]], axis=0)
        for h in range(KV_HEADS):
            tl = slice((h // 2) * 128, (h // 2) * 128 + 128)
            mine = low if h % 2 == 0 else jnp.logical_not(low)
            kh = _bf(jnp.where(mine, kcat[:, tl], pltpu.roll(kcat[:, tl], HEAD_DIM, 1)))
            vh = _bf(jnp.where(mine, vcat[:, tl], pltpu.roll(vcat[:, tl], HEAD_DIM, 1)))
            for t in range(2):
                ql = slice((2 * h + t) * 128, (2 * h + t) * 128 + 128)
                q2 = q_ref[:, ql]
                outs = []
                for p in range(2):
                    qm = _bf(jnp.where(low if p == 0 else jnp.logical_not(low), q2, 0.0))
                    probs, _ = _attn_probs(_dot_nt(qm, kh), sink_ref[0, 4 * h + 2 * t + p], valid)
                    outs.append(_dot(_bf(probs), vh))
                o_ref[:, ql] = jnp.where(low, outs[0], outs[1])

    prev = lambda i: jnp.maximum(i - 1, 0)
    return pl.pallas_call(
        body, name="attn_fwd", grid=(nb,),
        in_specs=[pl.BlockSpec(memory_space=pltpu.SMEM),
                  pl.BlockSpec((ATT_BLOCK, D), lambda i: (i, COL_AQ // D)),
                  pl.BlockSpec((ATT_BLOCK, 256), lambda i: (prev(i), COL_AK // 256)),
                  pl.BlockSpec((ATT_BLOCK, 256), lambda i: (i, COL_AK // 256)),
                  pl.BlockSpec((ATT_BLOCK, 256), lambda i: (prev(i), COL_AV // 256)),
                  pl.BlockSpec((ATT_BLOCK, 256), lambda i: (i, COL_AV // 256))],
        out_specs=pl.BlockSpec((ATT_BLOCK, D), lambda i: (i, 0)),
        out_shape=jax.ShapeDtypeStruct((T, D), F32),
        compiler_params=_params(("arbitrary",)),
    )(sinks, proj, proj, proj, proj, proj)


def _attn_bwd(proj, sinks, o, do):
    T = proj.shape[0]
    nb = T // ATT_BLOCK
    W2 = 2 * ATT_BLOCK

    def body(sink_ref, q_ref, kp_ref, kc_ref, vp_ref, vc_ref, o_ref, do_ref,
             dq_ref, dk_ref, dv_ref, dsink_ref, ck_scr, cv_scr, nk_scr, nv_scr):
        blk = pl.program_id(0)

        @pl.when(blk == 0)
        def _():
            ck_scr[...] = jnp.zeros_like(ck_scr)
            cv_scr[...] = jnp.zeros_like(cv_scr)
            dsink_ref[...] = jnp.zeros_like(dsink_ref)

        @pl.when(blk < nb)
        def _():
            valid = _attn_masks(blk)
            low = lax.broadcasted_iota(jnp.int32, (1, 2 * HEAD_DIM), 1) < HEAD_DIM
            kcat = jnp.concatenate([kp_ref[...], kc_ref[...]], axis=0)
            vcat = jnp.concatenate([vp_ref[...], vc_ref[...]], axis=0)
            for h in range(KV_HEADS):
                tl = slice((h // 2) * 128, (h // 2) * 128 + 128)
                kbd = _head_pair_operand(kcat[:, tl], h % 2, low)
                vbd = _head_pair_operand(vcat[:, tl], h % 2, low)
                dkbd = jnp.zeros((2 * W2, 128), F32)
                dvbd = jnp.zeros((2 * W2, 128), F32)
                tiles = []
                for t in range(2):
                    ql = slice((2 * h + t) * 128, (2 * h + t) * 128 + 128)
                    q2 = _bf(q_ref[:, ql])
                    do2_b = do_ref[:, ql]
                    doo = do2_b.astype(F32) * o_ref[:, ql]
                    dsum0 = jnp.sum(jnp.where(low, doo, 0.0), axis=1, keepdims=True)
                    dsum1 = jnp.sum(jnp.where(low, 0.0, doo), axis=1, keepdims=True)
                    tiles.append((ql, q2, do2_b, dsum0, dsum1, _dot_nt(q2, kbd), _dot_nt(do2_b, vbd)))
                grads = []
                for t, (ql, q2, do2_b, dsum0, dsum1, s2, dp2) in enumerate(tiles):
                    head = 4 * h + 2 * t
                    p0, ps0 = _attn_probs(s2[:, 0:W2], sink_ref[0, head], valid)
                    p1, ps1 = _attn_probs(s2[:, W2:2 * W2], sink_ref[0, head + 1], valid)
                    ds2 = _bf(jnp.concatenate([p0 * (dp2[:, 0:W2] - dsum0), p1 * (dp2[:, W2:2 * W2] - dsum1)], axis=1) * SCALE)
                    grads.append((ds2, _bf(jnp.concatenate([p0, p1], axis=1))))
                    dsink_ref[head:head + 1, :] += jnp.zeros((1, 128), F32) - jnp.sum(ps0 * dsum0, axis=0, keepdims=True)
                    dsink_ref[head + 1:head + 2, :] += jnp.zeros((1, 128), F32) - jnp.sum(ps1 * dsum1, axis=0, keepdims=True)
                for (ql, q2, do2_b, _, _, _, _), (ds2, p2) in zip(tiles, grads):
                    dq_ref[:, ql] = _bf(_dot(ds2, kbd))
                    dkbd = dkbd + _dot_tn(ds2, q2)
                    dvbd = dvbd + _dot_tn(p2, do2_b)
                dk2 = jnp.where(low, dkbd[0:W2], dkbd[W2:2 * W2])
                dv2 = jnp.where(low, dvbd[0:W2], dvbd[W2:2 * W2])
                dk2 = dk2 + pltpu.roll(dk2, HEAD_DIM, 1)
                dv2 = dv2 + pltpu.roll(dv2, HEAD_DIM, 1)
                if h % 2 == 0:
                    keep_k, keep_v = dk2, dv2
                else:
                    nk_scr[:, tl] = jnp.where(low, keep_k, dk2)
                    nv_scr[:, tl] = jnp.where(low, keep_v, dv2)
            dk_ref[...] = _bf(ck_scr[...] + nk_scr[0:ATT_BLOCK, :])
            dv_ref[...] = _bf(cv_scr[...] + nv_scr[0:ATT_BLOCK, :])
            ck_scr[...] = nk_scr[ATT_BLOCK:2 * ATT_BLOCK, :]
            cv_scr[...] = nv_scr[ATT_BLOCK:2 * ATT_BLOCK, :]

        @pl.when(blk == nb)
        def _():
            dk_ref[...] = _bf(ck_scr[...])
            dv_ref[...] = _bf(cv_scr[...])

    cur = lambda i: jnp.minimum(i, nb - 1)
    prev = lambda i: jnp.maximum(cur(i) - 1, 0)
    late = lambda i: jnp.maximum(i - 1, 0)
    dq, dk, dv, dsink = pl.pallas_call(
        body, name="attn_bwd", grid=(nb + 1,),
        in_specs=[pl.BlockSpec(memory_space=pltpu.SMEM),
                  pl.BlockSpec((ATT_BLOCK, D), lambda i: (cur(i), COL_AQ // D)),
                  pl.BlockSpec((ATT_BLOCK, 256), lambda i: (prev(i), COL_AK // 256)),
                  pl.BlockSpec((ATT_BLOCK, 256), lambda i: (cur(i), COL_AK // 256)),
                  pl.BlockSpec((ATT_BLOCK, 256), lambda i: (prev(i), COL_AV // 256)),
                  pl.BlockSpec((ATT_BLOCK, 256), lambda i: (cur(i), COL_AV // 256)),
                  pl.BlockSpec((ATT_BLOCK, D), lambda i: (cur(i), 0)),
                  pl.BlockSpec((ATT_BLOCK, D), lambda i: (cur(i), 0))],
        out_specs=[pl.BlockSpec((ATT_BLOCK, D), lambda i: (cur(i), 0)),
                   pl.BlockSpec((ATT_BLOCK, 256), lambda i: (late(i), 0)),
                   pl.BlockSpec((ATT_BLOCK, 256), lambda i: (late(i), 0)),
                   pl.BlockSpec((16, 128), lambda i: (0, 0))],
        out_shape=[jax.ShapeDtypeStruct((T, D), BF16), jax.ShapeDtypeStruct((T, 256), BF16),
                   jax.ShapeDtypeStruct((T, 256), BF16), jax.ShapeDtypeStruct((16, 128), F32)],
        scratch_shapes=[pltpu.VMEM((ATT_BLOCK, 256), F32), pltpu.VMEM((ATT_BLOCK, 256), F32),
                        pltpu.VMEM((2 * ATT_BLOCK, 256), F32), pltpu.VMEM((2 * ATT_BLOCK, 256), F32)],
        compiler_params=_params(("arbitrary",)),
    )(sinks, proj, proj, proj, proj, proj, o, do)
    return dq, dk, dv, dsink


def _mid(x, tgt, proj, oh, oa, hnw, fnw, wsq_bf):
    T = x.shape[0]
    tm = min(256, T)
    nt = T // tm

    def body(x_ref, tgt_ref, oh_ref, oa_ref, hg_ref, ag0_ref, ag1_ref, mh0_ref, mh1_ref, ma0_ref, ma1_ref,
             hnw_ref, fnw_ref, w_hbm,
             dx2_ref, doh_ref, doa_ref, dhg_ref, dtail_ref, lhs_ref, rhs_ref, loss_ref, vec_ref,
             w_scr, xh_scr, rs_scr, sem):
        @pl.when(pl.program_id(0) == 0)
        def _():
            cp = pltpu.make_async_copy(w_hbm, w_scr, sem)
            cp.start()
            cp.wait()
            loss_ref[...] = jnp.zeros_like(loss_ref)
            vec_ref[...] = jnp.zeros_like(vec_ref)

        oh = oh_ref[...]
        for h in range(HEADS):
            sl = slice(h * HEAD_W, (h + 1) * HEAD_W)
            ohh = oh[:, sl]
            rs = lax.rsqrt(jnp.mean(ohh * ohh, axis=1, keepdims=True) + EPS)
            xh_scr[:, sl] = ohh * rs
            rs_scr[:, sl] = jnp.broadcast_to(rs, (tm, HEAD_W))
        xh = xh_scr[...]
        hnw = hnw_ref[...]
        on = xh * hnw
        hg = hg_ref[...]
        sg = _sigmoid(hg)
        silu_g = hg * sg
        gated_h = _bf(on * silu_g)
        lhs_ref[0] = gated_h.T
        yh = _dot(gated_h, w_scr[0])
        oa = oa_ref[...]
        ag = jnp.concatenate([ag0_ref[...], ag1_ref[...]], axis=1)
        sa = _sigmoid(ag)
        silu_a = ag * sa
        gated_a = _bf(oa * silu_a)
        lhs_ref[1] = gated_a.T
        ya = _dot(gated_a, w_scr[1])
        smh = _sigmoid(jnp.concatenate([mh0_ref[...], mh1_ref[...]], axis=1))
        sma = _sigmoid(jnp.concatenate([ma0_ref[...], ma1_ref[...]], axis=1))
        merged = _bf(smh * yh + sma * ya)
        lhs_ref[2] = merged.T
        x2 = x_ref[...] + _dot(merged, w_scr[2])
        rs2 = lax.rsqrt(jnp.mean(x2 * x2, axis=1, keepdims=True) + EPS)
        xh2 = x2 * rs2
        fnw = fnw_ref[...]
        diff = xh2 * fnw - tgt_ref[...]
        loss_ref[...] += jnp.zeros_like(loss_ref) + jnp.sum(diff * diff) * (0.5 / D)

        dy = diff * (1.0 / D)
        vec_ref[0:1, :] += jnp.sum(dy * xh2, axis=0, keepdims=True)
        gy = dy * fnw
        dx2 = rs2 * (gy - xh2 * jnp.mean(gy * xh2, axis=1, keepdims=True))
        dx2_ref[...] = dx2
        dx2_b = _bf(dx2)
        rhs_ref[2] = dx2_b
        dmerged = _dot_nt(dx2_b, w_scr[2])
        dyh = dmerged * smh
        dya = dmerged * sma
        dtail_ref[:, D:2 * D] = _bf(dyh * yh * (1.0 - smh))
        dtail_ref[:, 2 * D:3 * D] = _bf(dya * ya * (1.0 - sma))
        dyh_b, dya_b = _bf(dyh), _bf(dya)
        rhs_ref[0] = dyh_b
        rhs_ref[1] = dya_b
        dgh = _dot_nt(dyh_b, w_scr[0])
        dga = _dot_nt(dya_b, w_scr[1])
        don = dgh * silu_g
        dhg_ref[...] = _bf(dgh * on * (sg * (1.0 + hg * (1.0 - sg))))
        vec_ref[1:2, :] += jnp.sum(don * xh, axis=0, keepdims=True)
        gxh = don * hnw
        rsb = rs_scr[...]
        for h in range(HEADS):
            sl = slice(h * HEAD_W, (h + 1) * HEAD_W)
            gh, xhh = gxh[:, sl], xh[:, sl]
            doh_ref[:, sl] = _bf(rsb[:, sl] * (gh - xhh * jnp.mean(gh * xhh, axis=1, keepdims=True)))
        doa_ref[...] = _bf(dga * silu_a)
        dtail_ref[:, 0:D] = _bf(dga * oa * (sa * (1.0 + ag * (1.0 - sa))))

    row = lambda w, j: pl.BlockSpec((tm, w), lambda i: (i, j))
    const = lambda r, c: pl.BlockSpec((r, c), lambda i: (0, 0))
    stack = pl.BlockSpec((3, tm, D), lambda i: (0, i, 0))
    stack_t = pl.BlockSpec((3, D, tm), lambda i: (0, 0, i))
    return pl.pallas_call(
        body, name="mid", grid=(nt,),
        in_specs=[row(D, 0), row(D, 0), row(D, 0), row(D, 0), row(D, COL_HG // D),
                  row(512, COL_AG // 512), row(512, COL_AG // 512 + 1),
                  row(512, COL_MH // 512), row(512, COL_MH // 512 + 1),
                  row(512, COL_MA // 512), row(512, COL_MA // 512 + 1),
                  const(1, D), const(1, D), HBM_SPEC],
        out_specs=[row(D, 0), row(D, 0), row(D, 0), row(D, 0), row(3 * D, 0), stack_t, stack, const(8, 128), const(8, D)],
        out_shape=[jax.ShapeDtypeStruct((T, D), F32), jax.ShapeDtypeStruct((T, D), BF16), jax.ShapeDtypeStruct((T, D), BF16),
                   jax.ShapeDtypeStruct((T, D), BF16), jax.ShapeDtypeStruct((T, 3 * D), BF16),
                   jax.ShapeDtypeStruct((3, D, T), BF16), jax.ShapeDtypeStruct((3, T, D), BF16),
                   jax.ShapeDtypeStruct((8, 128), F32), jax.ShapeDtypeStruct((8, D), F32)],
        scratch_shapes=[pltpu.VMEM((3, D, D), BF16), pltpu.VMEM((tm, D), F32), pltpu.VMEM((tm, D), F32),
                        pltpu.SemaphoreType.DMA],
        compiler_params=_params(("arbitrary",)),
    )(x, tgt, oh, oa, proj, proj, proj, proj, proj, proj, proj, hnw, fnw, wsq_bf)


def _wgrad_square(lhs_t, rhs):
    T = rhs.shape[1]
    tk = min(1024, T)

    def body(a_ref, b_ref, g_ref):
        part = _dot(a_ref[...], b_ref[...])

        @pl.when(pl.program_id(1) == 0)
        def _():
            g_ref[...] = part

        @pl.when(pl.program_id(1) > 0)
        def _():
            g_ref[...] += part

    return pl.pallas_call(
        body, name="wgrad_square", grid=(3, T // tk),
        in_specs=[pl.BlockSpec((None, D, tk), lambda k, i: (k, 0, i)), pl.BlockSpec((None, tk, D), lambda k, i: (k, i, 0))],
        out_specs=pl.BlockSpec((None, D, D), lambda k, i: (k, 0, 0)),
        out_shape=jax.ShapeDtypeStruct((3, D, D), F32),
        compiler_params=_params(("parallel", "arbitrary")),
    )(lhs_t, rhs)


def _bwd_dx(pieces, wt_bf, x, norm_w, dx2, swin_b, ssq_b):
    T = x.shape[0]
    tm = min(256, T)
    nt = T // tm
    widths = [p.shape[1] for p in pieces]
    n_p = len(pieces)

    def body(*refs):
        piece_refs = refs[:n_p]
        (w_hbm, x_ref, nw_ref, dx2_ref, swin_ref, ssq_ref,
         gx_ref, gnw_ref, win_got, sq_got, w_scr, sem, send_sems, recv_sems) = refs[n_p:]

        def scatter_copies():
            x_, y_, c_ = _place()
            copies = []
            for k, (fx, fy) in enumerate(CHIP_FLIPS):
                px, py = _flip(x_, fx), _flip(y_, fy)
                jr = 2 * px + py
                for a, (src, dst) in enumerate(((swin_ref.at[:, pl.ds(jr * SHARD_W, SHARD_W)], win_got.at[k]),
                                                (ssq_ref.at[:, pl.ds(jr * SQ_ROWS, SQ_ROWS), :], sq_got.at[k]))):
                    copies.append(pltpu.make_async_remote_copy(
                        src_ref=src, dst_ref=dst, send_sem=send_sems.at[2 * k + a], recv_sem=recv_sems.at[2 * k + a],
                        device_id=(px, py, c_), device_id_type=MESH))
            return copies

        @pl.when(pl.program_id(0) == 0)
        def _():
            for cp in scatter_copies():
                cp.start()
            cp = pltpu.make_async_copy(w_hbm, w_scr, sem)
            cp.start()
            cp.wait()
            gnw_ref[...] = jnp.zeros_like(gnw_ref)

        dxn = None
        off = 0
        for ref, w in zip(piece_refs, widths):
            part = _dot(ref[...], w_scr[off:off + w, :])
            dxn = part if dxn is None else dxn + part
            off += w
        xf = x_ref[...]
        rs = lax.rsqrt(jnp.mean(xf * xf, axis=1, keepdims=True) + EPS)
        xh = xf * rs
        gnw_ref[...] += jnp.sum(dxn * xh, axis=0, keepdims=True)
        gx = dxn * nw_ref[...]
        gx_ref[...] = rs * (gx - xh * jnp.mean(gx * xh, axis=1, keepdims=True)) + dx2_ref[...]

        @pl.when(pl.program_id(0) == nt - 1)
        def _():
            for cp in scatter_copies():
                cp.wait()

    row = lambda w: pl.BlockSpec((tm, w), lambda i: (i, 0))
    return pl.pallas_call(
        body, name="bwd_dx", grid=(nt,),
        in_specs=[row(w) for w in widths] + [HBM_SPEC, row(D), pl.BlockSpec((1, D), lambda i: (0, 0)), row(D), HBM_SPEC, HBM_SPEC],
        out_specs=[row(D), pl.BlockSpec((1, D), lambda i: (0, 0)), HBM_SPEC, HBM_SPEC],
        out_shape=[jax.ShapeDtypeStruct((T, D), F32), jax.ShapeDtypeStruct((1, D), F32),
                   jax.ShapeDtypeStruct((3, D // 2, SHARD_W), BF16), jax.ShapeDtypeStruct((3, 3, SQ_ROWS, D // 2), BF16)],
        scratch_shapes=[pltpu.VMEM((D_IN, D), BF16), pltpu.SemaphoreType.DMA,
                        pltpu.SemaphoreType.DMA((6,)), pltpu.SemaphoreType.DMA((6,))],
        compiler_params=_params(("arbitrary",)),
    )(*pieces, wt_bf, x, norm_w, dx2, swin_b, ssq_b)


W_PIECES = ((0, 1024, 3), (COL_HG, 1024, 1), (COL_AQ, 1024, 1), (COL_AK, 256, 1), (COL_AV, 256, 1), (COL_AG, 512, 6))


def _wgrad_in(xnt_bf, pieces):
    T = xnt_bf.shape[1]
    buf = None
    for n, (piece, (col, wb, blocks)) in enumerate(zip(pieces, W_PIECES)):
        first = buf is None
        tk = min(1024 if wb == 1024 else 2048, T)

        def body(xnt_ref, p_ref, *rest):
            g_ref = rest[-1]
            part = _dot(xnt_ref[...], p_ref[...])

            @pl.when(pl.program_id(1) == 0)
            def _():
                g_ref[...] = part

            @pl.when(pl.program_id(1) > 0)
            def _():
                g_ref[...] += part

        call = pl.pallas_call(
            body, name=f"wgrad_in_{n}", grid=(blocks, T // tk),
            in_specs=[pl.BlockSpec((D, tk), lambda jb, i: (0, i)), pl.BlockSpec((tk, wb), lambda jb, i: (i, jb))]
                     + ([] if first else [HBM_SPEC]),
            out_specs=pl.BlockSpec((D, wb), lambda jb, i, base=col // wb: (0, base + jb)),
            out_shape=jax.ShapeDtypeStruct((D, D_IN), F32),
            input_output_aliases={} if first else {2: 0},
            compiler_params=_params(("parallel", "arbitrary")),
        )
        buf = call(xnt_bf, piece) if first else call(xnt_bf, piece, buf)
    return buf


def _place():
    return lax.axis_index("x"), lax.axis_index("y"), lax.axis_index("c")


def _flip(v, f):
    return 1 - v if f else v


def _win_half(ref, h):
    return ref.at[pl.ds(h * (D // 2), D // 2), :]


def _sq_half(ref, h):
    return ref.at[:, pl.ds(h * (D // 2), D // 2)]


def _gather_copy(part, k, to, send_sems, recv_sems):
    return pltpu.make_async_remote_copy(src_ref=part, dst_ref=part, send_sem=send_sems.at[k], recv_sem=recv_sems.at[k],
                                        device_id=to, device_id_type=MESH)


def _gather_start(out, half, send_sems, recv_sems, chips=(0, 1, 2)):
    x, y, c = _place()
    for k in chips:
        fx, fy = CHIP_FLIPS[k]
        _gather_copy(half(out.at[2 * x + y], c), k, (_flip(x, fx), _flip(y, fy), c), send_sems, recv_sems).start()


def _gather_land(out, half, k, send_sems, recv_sems):
    x, y, c = _place()
    sib = (x, y, 1 - c)
    fx, fy = CHIP_FLIPS[k]
    slot = out.at[2 * _flip(x, fx) + _flip(y, fy)]
    _gather_copy(half(slot, c), k, sib, send_sems, recv_sems).wait_recv()
    _gather_copy(half(slot, c), 3 + k, sib, send_sems, recv_sems).start()
    _gather_copy(half(slot, 1 - c), 3 + k, sib, send_sems, recv_sems).wait_recv()


def _gather_drain(out, half, send_sems, recv_sems):
    x, y, c = _place()
    for k, (fx, fy) in enumerate(CHIP_FLIPS):
        _gather_copy(half(out.at[2 * x + y], c), k, (_flip(x, fx), _flip(y, fy), c), send_sems, recv_sems).wait_send()
        _gather_copy(half(out.at[2 * _flip(x, fx) + _flip(y, fy)], c), 3 + k, (x, y, 1 - c), send_sems, recv_sems).wait_send()


def _gather_finish(out, half, send_sems, recv_sems):
    for k in range(len(CHIP_FLIPS)):
        _gather_land(out, half, k, send_sems, recv_sems)
    _gather_drain(out, half, send_sems, recv_sems)


def _swap_halves(gwin, gsq):
    def body(gwin_ref, gsq_ref, win_got, sq_got, send_sems, recv_sems):
        x, y, c = _place()
        sib = (x, y, 1 - c)
        pairs = ((_win_half(gwin_ref, 1 - c), win_got),
                 (gsq_ref.at[:, :, pl.ds((1 - c) * (D // 2), D // 2)], sq_got))
        copies = [pltpu.make_async_remote_copy(src_ref=src, dst_ref=dst, send_sem=send_sems.at[a], recv_sem=recv_sems.at[a],
                                               device_id=sib, device_id_type=MESH) for a, (src, dst) in enumerate(pairs)]
        for cp in copies:
            cp.start()
        for cp in copies:
            cp.wait()

    return pl.pallas_call(
        body, name="swap_halves",
        in_specs=[HBM_SPEC, HBM_SPEC], out_specs=[HBM_SPEC, HBM_SPEC],
        out_shape=[jax.ShapeDtypeStruct((D // 2, D_IN), F32), jax.ShapeDtypeStruct((3, D, D // 2), F32)],
        scratch_shapes=[pltpu.SemaphoreType.DMA((2,)), pltpu.SemaphoreType.DMA((2,))],
    )(gwin, gsq)


def _add_halves(c_arr, gwin, gsq, win_got, sq_got):
    def body(c_ref, a_ref, b_ref, p_ref, q_ref, so_ref, sq_ref, sob_ref, sqb_ref):
        so = a_ref[...] + b_ref[...]
        sq = p_ref[...] + q_ref[...]
        so_ref[...] = so
        sq_ref[...] = sq
        sob_ref[...] = _bf(so)
        sqb_ref[...] = _bf(sq)

    steps = 8
    rows, sq_rows = (D // 2) // steps, D // steps
    win = lambda f: pl.BlockSpec((rows, D_IN), f)
    sq = lambda f: pl.BlockSpec((3, sq_rows, D // 2), f)
    return pl.pallas_call(
        body, name="add_halves",
        grid_spec=pltpu.PrefetchScalarGridSpec(
            num_scalar_prefetch=1, grid=(steps,),
            in_specs=[win(lambda i, c: (c[0] * steps + i, 0)), win(lambda i, c: (i, 0)),
                      sq(lambda i, c: (0, i, c[0])), sq(lambda i, c: (0, i, 0))],
            out_specs=[win(lambda i, c: (i, 0)), sq(lambda i, c: (0, i, 0))] * 2),
        out_shape=[jax.ShapeDtypeStruct((D // 2, D_IN), F32), jax.ShapeDtypeStruct((3, D, D // 2), F32),
                   jax.ShapeDtypeStruct((D // 2, D_IN), BF16), jax.ShapeDtypeStruct((3, D, D // 2), BF16)],
        compiler_params=_params(("arbitrary",)),
    )(c_arr, gwin, win_got, gsq, sq_got)


def _sum_chips(jc_arr, swin, ssq, win_got, sq_got):
    def body(jc_ref, a_ref, b_ref, p_ref, q_ref, so_ref, sq_ref):
        so_ref[...] = ((a_ref[...] + b_ref[0].astype(F32)) + b_ref[1].astype(F32)) + b_ref[2].astype(F32)
        sq_ref[...] = ((p_ref[...] + q_ref[0].astype(F32)) + q_ref[1].astype(F32)) + q_ref[2].astype(F32)

    rows = 128
    steps = (D // 2) // rows
    sq_rows = SQ_ROWS // steps
    return pl.pallas_call(
        body, name="sum_chips",
        grid_spec=pltpu.PrefetchScalarGridSpec(
            num_scalar_prefetch=1, grid=(steps,),
            in_specs=[pl.BlockSpec((rows, SHARD_W), lambda i, jc: (i, jc[0])),
                      pl.BlockSpec((3, rows, SHARD_W), lambda i, jc: (0, i, 0)),
                      pl.BlockSpec((3, sq_rows, D // 2), lambda i, jc: (0, jc[0] * steps + i, 0)),
                      pl.BlockSpec((3, 3, sq_rows, D // 2), lambda i, jc: (0, 0, i, 0))],
            out_specs=[pl.BlockSpec((rows, SHARD_W), lambda i, jc: (jc[1] * steps + i, 0)),
                       pl.BlockSpec((3, sq_rows, D // 2), lambda i, jc: (0, i, jc[1]))]),
        out_shape=[jax.ShapeDtypeStruct((D, SHARD_W), F32), jax.ShapeDtypeStruct((3, SQ_ROWS, D), F32)],
        compiler_params=_params(("arbitrary",)),
    )(jc_arr, swin, win_got, ssq, sq_got)


def _join_halves(g_win, g_sq):
    def body(win_in, sq_in, win_out, sq_out, send_sems, recv_sems):
        del win_in, sq_in
        x, y, c = _place()
        sib = (x, y, 1 - c)

        def halves(h):
            return _win_half(win_out, h), sq_out.at[:, :, pl.ds(h * (D // 2), D // 2)]

        def copy(a, part):
            return pltpu.make_async_remote_copy(src_ref=part, dst_ref=part, send_sem=send_sems.at[a], recv_sem=recv_sems.at[a],
                                                device_id=sib, device_id_type=MESH)

        sent = [copy(a, part) for a, part in enumerate(halves(c))]
        for cp in sent:
            cp.start()
        for a, part in enumerate(halves(1 - c)):
            copy(a, part).wait_recv()
        for cp in sent:
            cp.wait_send()

    return pl.pallas_call(
        body, name="join_halves",
        in_specs=[HBM_SPEC, HBM_SPEC], out_specs=[HBM_SPEC, HBM_SPEC], input_output_aliases={0: 0, 1: 1},
        out_shape=[jax.ShapeDtypeStruct((D, SHARD_W), F32), jax.ShapeDtypeStruct((3, SQ_ROWS, D), F32)],
        scratch_shapes=[pltpu.SemaphoreType.DMA((2,)), pltpu.SemaphoreType.DMA((2,))],
    )(g_win, g_sq)


def _allreduce_small(vec):
    def body(vec_ref, out_ref, slots, send_sems, recv_sems):
        x, y, c = _place()
        me = 4 * x + 2 * y + c
        slots[me] = vec_ref[...]
        copies = []
        for k in range(1, 8):
            fx, fy, fc = (k >> 2) & 1, (k >> 1) & 1, k & 1
            copies.append(pltpu.make_async_remote_copy(
                src_ref=vec_ref, dst_ref=slots.at[me], send_sem=send_sems.at[k - 1], recv_sem=recv_sems.at[k - 1],
                device_id=(_flip(x, fx), _flip(y, fy), _flip(c, fc)), device_id_type=MESH))
        for cp in copies:
            cp.start()
        for k in range(1, 8):
            fx, fy, fc = (k >> 2) & 1, (k >> 1) & 1, k & 1
            src = 4 * _flip(x, fx) + 2 * _flip(y, fy) + _flip(c, fc)
            pltpu.make_async_remote_copy(src_ref=vec_ref, dst_ref=slots.at[src], send_sem=send_sems.at[k - 1],
                                         recv_sem=recv_sems.at[k - 1], device_id=(x, y, c), device_id_type=MESH).wait_recv()
        for cp in copies:
            cp.wait_send()
        total = slots[0]
        for s in range(1, 8):
            total = total + slots[s]
        out_ref[...] = total

    return pl.pallas_call(
        body, name="allreduce_small",
        in_specs=[pl.BlockSpec(memory_space=pltpu.VMEM)], out_specs=pl.BlockSpec(memory_space=pltpu.VMEM),
        out_shape=jax.ShapeDtypeStruct((8, D), F32),
        scratch_shapes=[pltpu.VMEM((8, 8, D), F32), pltpu.SemaphoreType.DMA((7,)), pltpu.SemaphoreType.DMA((7,))],
    )(vec)


def _adamw_math(w, g, m, v):
    m = ADAM_B1 * m + (1.0 - ADAM_B1) * g
    v = ADAM_B2 * v + (1.0 - ADAM_B2) * (g * g)
    m_hat = m / (1.0 - ADAM_B1 ** ADAM_STEP)
    v_hat = v / (1.0 - ADAM_B2 ** ADAM_STEP)
    delta = -ADAM_LR * (m_hat / (jnp.sqrt(v_hat) + ADAM_EPS) + ADAM_WD * w)
    return delta, m, v


def _adamw(name, w, g, m, v, rows):
    R, C = w.shape

    def body(w_ref, g_ref, m_ref, v_ref, d_out, m_out, v_out):
        d_out[...], m_out[...], v_out[...] = _adamw_math(w_ref[...], g_ref[...], m_ref[...], v_ref[...])

    spec = pl.BlockSpec((rows, C), lambda i: (i, 0))
    return pl.pallas_call(
        body, name=name, grid=(R // rows,), in_specs=[spec] * 4, out_specs=[spec] * 3,
        out_shape=[jax.ShapeDtypeStruct((R, C), F32)] * 3,
        compiler_params=_params(("parallel",)),
    )(w, g, m, v)


def _small_update(total, lbw, w8, m8, v8):
    def body(t_ref, lbw_ref, w_ref, m_ref, v_ref, g_out, d_out, m_out, v_out):
        lb = 1.0 / (1.0 + jnp.exp(lbw_ref[1:2, :] - lbw_ref[0:1, :]))
        dlb = t_ref[2:3, :] * lb * (1.0 - lb)
        g_out[...] = jnp.zeros_like(g_out)
        g_out[0:1, :] = t_ref[3:4, :]
        g_out[1:2, :] = dlb
        g_out[2:3, :] = -dlb
        g_out[3:4, :] = t_ref[1:2, :]
        g_out[4:5, :] = t_ref[0:1, :]
        g_out[5:6, :] = t_ref[4:5, :]
        d_out[...], m_out[...], v_out[...] = _adamw_math(w_ref[...], g_out[...], m_ref[...], v_ref[...])

    return pl.pallas_call(
        body, name="small_update", out_shape=[jax.ShapeDtypeStruct((8, D), F32)] * 4,
        compiler_params=_params(),
    )(total, lbw, w8, m8, v8)


def _pack8(norm_w, lbw, hnw, fnw, sinks):
    pad = jnp.zeros((1, D - 16), F32)
    return jnp.concatenate([norm_w, lbw, hnw, fnw.reshape(1, D), jnp.concatenate([sinks, pad], axis=1),
                            jnp.zeros((2, D), F32)], axis=0)


def _unpack8(a):
    return a[0:1], a[1:3], a[3:4], a[5:6, 0:16], a[4]


def _local_step(order_arr, x, tgt, norm_w, lbw, hnw, sinks, fnw, win_mine, wsq_mine, exchange):
    proj, xnt_bf, win_bf = _fwd_proj(order_arr, x, norm_w, win_mine)
    oh, states, wsq_all = _hgrn_fwd(proj, lbw, wsq_mine)
    wsq_bf = wsq_all.reshape(SHARDS, 3, SQ_ROWS, D).transpose(1, 0, 2, 3).reshape(3, D, D)
    oa = _attn_fwd(proj, sinks)
    dx2, doh, doa, dhg, dtail, lhs, rhs, loss8, vec_mid = _mid(x, tgt, proj, oh, oa, hnw, fnw.reshape(1, D), wsq_bf)
    gsq = _wgrad_square(lhs, rhs)
    dhead, dlb = _hgrn_bwd(proj, lbw, states, doh)
    daq, dak, dav, dsink = _attn_bwd(proj, sinks, oa, doa)
    pieces = [dhead, dhg, daq, dak, dav, dtail]
    sums = exchange(_wgrad_in(xnt_bf, pieces), gsq)
    wt_bf = win_bf.transpose(0, 2, 1).reshape(D_IN, D)
    grad_x, gnw, win_got, sq_got = _bwd_dx(pieces, wt_bf, x, norm_w, dx2, sums[2], sums[3])
    sink_row = jnp.concatenate([dsink[:, 0].reshape(1, 16), jnp.zeros((1, D - 16), F32)], axis=1)
    loss_row = jnp.broadcast_to(loss8[0:1, 0:1], (1, D))
    vec = jnp.concatenate([vec_mid[0:2], dlb, gnw, sink_row, loss_row, jnp.zeros((2, D), F32)], axis=0)
    return grad_x, sums, (win_got, sq_got), vec


def kernel(x, norm_w, w_in, hgrn_lower_bound, hgrn_norm_w, w_branch_hgrn, attn_sinks, w_branch_attn, w_out, final_norm_w, loss_target, m_norm_w, m_w_in, m_hgrn_lower_bound, m_hgrn_norm_w, m_w_branch_hgrn, m_attn_sinks, m_w_branch_attn, m_w_out, m_final_norm_w, v_norm_w, v_w_in, v_hgrn_lower_bound, v_hgrn_norm_w, v_w_branch_hgrn, v_attn_sinks, v_w_branch_attn, v_w_out, v_final_norm_w):
    c_arr = lax.axis_index("c").astype(jnp.int32).reshape(1)
    j_arr = (2 * lax.axis_index("x") + lax.axis_index("y")).astype(jnp.int32).reshape(1)
    jc_arr = jnp.concatenate([j_arr, c_arr])

    win_mine, wsq_mine = _cast_shards(j_arr, w_in[0], w_branch_hgrn[0], w_branch_attn[0], w_out[0])
    xi, yi = lax.axis_index("x"), lax.axis_index("y")
    order_arr = jnp.stack([2 * xi + yi] + [2 * _flip(xi, fx) + _flip(yi, fy) for fx, fy in CHIP_FLIPS]).astype(jnp.int32)

    def chip_sums(gwin, gsq):
        return _add_halves(c_arr, gwin, gsq, *_swap_halves(gwin, gsq))

    grad_x, (swin, ssq, _, _), arrived, vec = _local_step(
        order_arr, x[0], loss_target[0], norm_w, hgrn_lower_bound, hgrn_norm_w, attn_sinks, final_norm_w, win_mine, wsq_mine,
        chip_sums)
    g_win, g_sq = _join_halves(*_sum_chips(jc_arr, swin, ssq, *arrived))

    d_win, nm_win, nv_win = _adamw("adamw_w_in", w_in[0], g_win, m_w_in[0], v_w_in[0], 128)
    sq_w = jnp.concatenate([w_branch_hgrn[0], w_branch_attn[0], w_out[0]], axis=0)
    sq_m = jnp.concatenate([m_w_branch_hgrn[0], m_w_branch_attn[0], m_w_out[0]], axis=0)
    sq_v = jnp.concatenate([v_w_branch_hgrn[0], v_w_branch_attn[0], v_w_out[0]], axis=0)
    d_sq, nm_sq, nv_sq = _adamw("adamw_square", sq_w, g_sq.reshape(3 * SQ_ROWS, D), sq_m, sq_v, 256)

    total = _allreduce_small(vec)
    loss = total[5, 0]
    g8, d8, nm8, nv8 = _small_update(
        total, hgrn_lower_bound,
        _pack8(norm_w, hgrn_lower_bound, hgrn_norm_w, final_norm_w, attn_sinks),
        _pack8(m_norm_w, m_hgrn_lower_bound, m_hgrn_norm_w, m_final_norm_w, m_attn_sinks),
        _pack8(v_norm_w, v_hgrn_lower_bound, v_hgrn_norm_w, v_final_norm_w, v_attn_sinks))

    def assemble(win, sq, small):
        nw, lb, hn, sk, fn = _unpack8(small)
        sq = sq.reshape(3, 1, SQ_ROWS, D)
        return (nw, win.reshape(1, D, SHARD_W), lb, hn, sq[0], sk, sq[1], sq[2], fn)

    return (loss, grad_x.reshape(1, -1, D),
            *assemble(g_win, g_sq, g8), *assemble(d_win, d_sq, d8),
            *assemble(nm_win, nm_sq, nm8), *assemble(nv_win, nv_sq, nv8))
```

```python
import functools

import jax
import jax.numpy as jnp
from jax import lax
from jax.experimental import pallas as pl
from jax.experimental.pallas import tpu as pltpu

F32 = jnp.float32
BF16 = jnp.bfloat16

D = 1024
D_IN = 8704
SHARDS = 4
SHARD_W = D_IN // SHARDS
SQ_ROWS = D // SHARDS
HEADS = 8
HEAD_W = 128
CHUNK = 64
SUB = 4
ATT_BLOCK = 128
KV_HEADS = 4
HEAD_DIM = 64
EPS = 1e-6
NEG = -1e30
SCALE = HEAD_DIM ** -0.5
COL_HG, COL_AQ, COL_AK, COL_AV, COL_AG, COL_MH, COL_MA = 3072, 4096, 5120, 5376, 5632, 6656, 7680

ADAM_LR, ADAM_B1, ADAM_B2, ADAM_EPS, ADAM_WD, ADAM_STEP = 0.001, 0.9, 0.999, 1e-08, 0.01, 10

VMEM_LIMIT = 56 * 1024 * 1024
MESH = pl.DeviceIdType.MESH
HBM_SPEC = pl.BlockSpec(memory_space=pltpu.HBM)
CHIP_FLIPS = ((1, 0), (0, 1), (1, 1))


def _dot(a, b):
    return jnp.dot(a, b, preferred_element_type=F32)


def _dot_nt(a, b):
    return lax.dot_general(a, b, (((1,), (1,)), ((), ())), preferred_element_type=F32)


def _dot_tn(a, b):
    return lax.dot_general(a, b, (((0,), (0,)), ((), ())), preferred_element_type=F32)


def _sigmoid(v):
    return 1.0 / (1.0 + jnp.exp(-v))


def _bf(v):
    return v.astype(BF16)


def _split3(v):
    a = _bf(v)
    r = v - a.astype(F32)
    b = _bf(r)
    c = _bf(r - b.astype(F32))
    return a, b, c


def _tri_dot(tri, v):
    a, b, c = _split3(v)
    return _dot(tri, a) + _dot(tri, b) + _dot(tri, c)


def _params(sem=None):
    return pltpu.CompilerParams(dimension_semantics=sem, vmem_limit_bytes=VMEM_LIMIT)


def _cast_shards(j_arr, win_s, wbh_s, wba_s, wout_s):
    steps = 4
    rows = D // steps

    def body(j_ref, win_ref, a_ref, b_ref, c_ref, win_o, sq_o):
        win_o[...] = _bf(win_ref[...])

        @pl.when(pl.program_id(0) == 0)
        def _():
            sq_o[0:SQ_ROWS, :] = _bf(a_ref[...])
            sq_o[SQ_ROWS:2 * SQ_ROWS, :] = _bf(b_ref[...])
            sq_o[2 * SQ_ROWS:3 * SQ_ROWS, :] = _bf(c_ref[...])

    whole = pl.BlockSpec((SQ_ROWS, D), lambda i, j: (0, 0))
    return pl.pallas_call(
        body, name="cast_shards",
        grid_spec=pltpu.PrefetchScalarGridSpec(
            num_scalar_prefetch=1, grid=(steps,),
            in_specs=[pl.BlockSpec((rows, SHARD_W), lambda i, j: (i, 0)), whole, whole, whole],
            out_specs=[pl.BlockSpec((None, rows, SHARD_W), lambda i, j: (j[0], i, 0)),
                       pl.BlockSpec((None, 3 * SQ_ROWS, D), lambda i, j: (j[0], 0, 0))]),
        out_shape=[jax.ShapeDtypeStruct((SHARDS, D, SHARD_W), BF16), jax.ShapeDtypeStruct((SHARDS, 3 * SQ_ROWS, D), BF16)],
        compiler_params=_params(("arbitrary",)),
    )(j_arr, win_s, wbh_s, wba_s, wout_s)


def _fwd_proj(order_arr, x, norm_w, win_all):
    T = x.shape[0]
    tm = min(512, T)
    nt = T // tm

    def body(order_ref, x_ref, nw_ref, win_in, proj_ref, xn_ref, win_out, w_scr, xn_scr, sem, send_sems, recv_sems):
        del win_in
        p, i = pl.program_id(0), pl.program_id(1)

        def load(slot):
            cp = pltpu.make_async_copy(win_out.at[slot], w_scr, sem)
            cp.start()
            cp.wait()

        @pl.when((p == 0) & (i == 0))
        def _():
            _gather_start(win_out, _win_half, send_sems, recv_sems)
            load(order_ref[0])

        for k in range(SHARDS - 1):
            @pl.when((p == k + 1) & (i == 0))
            def _():
                _gather_land(win_out, _win_half, k, send_sems, recv_sems)
                load(order_ref[k + 1])

        @pl.when(p == 0)
        def _():
            xf = x_ref[...]
            rs = lax.rsqrt(jnp.mean(xf * xf, axis=1, keepdims=True) + EPS)
            xn = _bf((xf * rs) * nw_ref[...])
            xn_scr[i] = xn
            xn_ref[...] = xn.T

        proj_ref[...] = _dot(xn_scr[i], w_scr[...])

        @pl.when((p == SHARDS - 1) & (i == nt - 1))
        def _():
            _gather_drain(win_out, _win_half, send_sems, recv_sems)

    first = lambda p, i: jnp.where(p == 0, i, nt - 1)
    return pl.pallas_call(
        body, name="fwd_proj",
        grid_spec=pltpu.PrefetchScalarGridSpec(
            num_scalar_prefetch=1, grid=(SHARDS, nt),
            in_specs=[pl.BlockSpec((tm, D), lambda p, i, order: (first(p, i), 0)),
                      pl.BlockSpec((1, D), lambda p, i, order: (0, 0)), HBM_SPEC],
            out_specs=[pl.BlockSpec((tm, SHARD_W), lambda p, i, order: (i, order[p])),
                       pl.BlockSpec((D, tm), lambda p, i, order: (0, first(p, i))),
                       HBM_SPEC],
            scratch_shapes=[pltpu.VMEM((D, SHARD_W), BF16), pltpu.VMEM((nt, tm, D), BF16), pltpu.SemaphoreType.DMA,
                            pltpu.SemaphoreType.DMA((6,)), pltpu.SemaphoreType.DMA((6,))]),
        out_shape=[jax.ShapeDtypeStruct((T, D_IN), F32), jax.ShapeDtypeStruct((D, T), BF16),
                   jax.ShapeDtypeStruct((SHARDS, D, SHARD_W), BF16)],
        input_output_aliases={3: 2},
        compiler_params=_params(("arbitrary", "arbitrary")),
    )(order_arr, x, norm_w, win_all)


def _hgrn_gates(hq_ref, hf_ref, lbw_ref, b_scr):
    lb = 1.0 / (1.0 + jnp.exp(lbw_ref[1:2, :] - lbw_ref[0:1, :]))
    hf = hf_ref[...]
    sig = _sigmoid(hf)
    f = lb + (1.0 - lb) * sig
    g = jnp.log(f)
    hq = hq_ref[...]
    sq = _sigmoid(hq)
    q = hq * sq
    row = lax.broadcasted_iota(jnp.int32, (CHUNK, CHUNK), 0)
    col = lax.broadcasted_iota(jnp.int32, (CHUNK, CHUNK), 1)
    causal = row >= col
    b = _tri_dot(jnp.where(causal, 1.0, 0.0).astype(BF16), g)
    b_scr[...] = b
    bc = b_scr[CHUNK - 1:CHUNK, :]
    r = b_scr[CHUNK // 2 - 1:CHUNK // 2, :]
    return dict(lb=lb, sig=sig, f=f, k=1.0 - f, hq=hq, sq=sq, q=q, b=b, bc=bc, r=r, causal=causal)


def _hgrn_fwd(proj, lbw, wsq_all):
    T = proj.shape[0]
    n = T // CHUNK

    def body(hq_ref, hf_ref, hi_ref, lbw_ref, wsq_in, o_ref, st_ref, wsq_out, s_scr, b_scr, send_sems, recv_sems):
        del wsq_in

        @pl.when(pl.program_id(0) == 0)
        def _():
            _gather_start(wsq_out, _sq_half, send_sems, recv_sems)
            s_scr[...] = jnp.zeros_like(s_scr)

        for c in range(SUB):
            rows = pl.ds(c * CHUNK, CHUNK)
            gt = _hgrn_gates(hq_ref.at[rows, :], hf_ref.at[rows, :], lbw_ref, b_scr.at[rows, :])
            b, bc, r, q, k = gt["b"], gt["bc"], gt["r"], gt["q"], gt["k"]
            qe = _bf(q * jnp.exp(b))
            qr = _bf(q * jnp.exp(b - r))
            kr = _bf(k * jnp.exp(r - b))
            kl = _bf(k * jnp.exp(bc - b))
            ebc = jnp.exp(bc)
            v = _bf(hi_ref[rows, :])
            scores = [_bf(jnp.where(gt["causal"], _dot_nt(qr[:, h * HEAD_W:(h + 1) * HEAD_W], kr[:, h * HEAD_W:(h + 1) * HEAD_W]), 0.0))
                      for h in range(HEADS)]
            for h in range(HEADS):
                sl = slice(h * HEAD_W, (h + 1) * HEAD_W)
                st = s_scr[h]
                st_ref[c, h] = st
                o_ref[rows, sl] = _dot(scores[h], v[:, sl]) + _dot_nt(qe[:, sl], _bf(st))
                s_scr[h] = ebc[:, sl] * st + _dot_tn(v[:, sl], kl[:, sl])

        @pl.when(pl.program_id(0) == n // SUB - 1)
        def _():
            _gather_finish(wsq_out, _sq_half, send_sems, recv_sems)

    col = lambda j: pl.BlockSpec((SUB * CHUNK, D), lambda i: (i, j))
    return pl.pallas_call(
        body, name="hgrn_fwd", grid=(n // SUB,),
        in_specs=[col(0), col(1), col(2), pl.BlockSpec((2, D), lambda i: (0, 0)), HBM_SPEC],
        out_specs=[pl.BlockSpec((SUB * CHUNK, D), lambda i: (i, 0)),
                   pl.BlockSpec((SUB, HEADS, HEAD_W, HEAD_W), lambda i: (i, 0, 0, 0)), HBM_SPEC],
        out_shape=[jax.ShapeDtypeStruct((T, D), F32), jax.ShapeDtypeStruct((n, HEADS, HEAD_W, HEAD_W), F32),
                   jax.ShapeDtypeStruct((SHARDS, 3 * SQ_ROWS, D), BF16)],
        input_output_aliases={4: 2},
        scratch_shapes=[pltpu.VMEM((HEADS, HEAD_W, HEAD_W), F32), pltpu.VMEM((SUB * CHUNK, D), F32),
                        pltpu.SemaphoreType.DMA((6,)), pltpu.SemaphoreType.DMA((6,))],
        compiler_params=_params(("arbitrary",)),
    )(proj, proj, proj, lbw, wsq_all)


def _hgrn_bwd(proj, lbw, states, do):
    T = proj.shape[0]
    n = T // CHUNK

    def body(hq_ref, hf_ref, hi_ref, lbw_ref, st_ref, do_ref, dp_ref, dlb_ref,
             ds_scr, b_scr, dq_scr, dk_scr, dv_scr, late_scr, early_scr, ex_scr):
        @pl.when(pl.program_id(0) == 0)
        def _():
            ds_scr[...] = jnp.zeros_like(ds_scr)
            dlb_ref[...] = jnp.zeros_like(dlb_ref)

        for c in reversed(range(SUB)):
            rows = pl.ds(c * CHUNK, CHUNK)
            gt = _hgrn_gates(hq_ref.at[rows, :], hf_ref.at[rows, :], lbw_ref, b_scr.at[rows, :])
            b, bc, r, q, k = gt["b"], gt["bc"], gt["r"], gt["q"], gt["k"]
            eb = jnp.exp(b)
            er = jnp.exp(b - r)
            erk = jnp.exp(r - b)
            el = jnp.exp(bc - b)
            ebc = jnp.exp(bc)
            qe, qr, kr, kl = _bf(q * eb), _bf(q * er), _bf(k * erk), _bf(k * el)
            v = _bf(hi_ref[rows, :])
            do_b = do_ref[rows, :]
            do_t = do_b.T
            causal_t = lax.broadcasted_iota(jnp.int32, (CHUNK, CHUNK), 0) <= lax.broadcasted_iota(jnp.int32, (CHUNK, CHUNK), 1)
            firsts = []
            for h in range(HEADS):
                sl = slice(h * HEAD_W, (h + 1) * HEAD_W)
                firsts.append((_bf(jnp.where(causal_t, _dot_nt(kr[:, sl], qr[:, sl]), 0.0)),
                               _bf(jnp.where(gt["causal"], _dot_nt(do_b[:, sl], v[:, sl]), 0.0)),
                               _bf(jnp.where(causal_t, _dot_nt(v[:, sl], do_b[:, sl]), 0.0))))
            for h in range(HEADS):
                sl = slice(h * HEAD_W, (h + 1) * HEAD_W)
                st0 = st_ref[c, h]
                dst = ds_scr[h]
                dst_b = _bf(dst)
                a_t, da, da_t = firsts[h]
                mq = _dot(da, kr[:, sl])
                mk = _dot(da_t, qr[:, sl])
                dq_in = eb[:, sl] * _dot(do_b[:, sl], _bf(st0))
                dk_in = el[:, sl] * _dot(v[:, sl], dst_b)
                dq_scr[rows, sl] = er[:, sl] * mq + dq_in
                dk_scr[rows, sl] = erk[:, sl] * mk + dk_in
                dv_scr[rows, sl] = _dot(a_t, do_b[:, sl]) + _dot_nt(kl[:, sl], dst_b)
                late_scr[rows, sl] = q[:, sl] * dq_in + qr[:, sl].astype(F32) * mq - kr[:, sl].astype(F32) * mk
                early_scr[rows, sl] = k[:, sl] * dk_in
                ex_scr[:, sl] = jnp.sum(dst * st0, axis=0, keepdims=True)
                ds_scr[h] = ebc[:, sl] * dst + _dot(do_t[sl, :], qe[:, sl])

            dq, dk = dq_scr[rows, :], dk_scr[rows, :]
            row = lax.broadcasted_iota(jnp.int32, (CHUNK, CHUNK), 0)
            col = lax.broadcasted_iota(jnp.int32, (CHUNK, CHUNK), 1)
            at_or_after = jnp.where(col >= row, 1.0, 0.0).astype(BF16)
            before = jnp.where(col < row, 1.0, 0.0).astype(BF16)
            dg = _tri_dot(at_or_after, late_scr[rows, :]) + _tri_dot(before, early_scr[rows, :]) + ebc * ex_scr[...]
            df = dg / gt["f"] - dk
            sig, sq, hq, lb = gt["sig"], gt["sq"], gt["hq"], gt["lb"]
            dp_ref[rows, 0:D] = _bf(dq * (sq * (1.0 + hq * (1.0 - sq))))
            dp_ref[rows, D:2 * D] = _bf(df * (1.0 - lb) * sig * (1.0 - sig))
            dp_ref[rows, 2 * D:3 * D] = _bf(dv_scr[rows, :])
            dlb_ref[...] += jnp.sum(df * (1.0 - sig), axis=0, keepdims=True)

    ns = n // SUB
    col = lambda j: pl.BlockSpec((SUB * CHUNK, D), lambda i: (ns - 1 - i, j))
    return pl.pallas_call(
        body, name="hgrn_bwd", grid=(ns,),
        in_specs=[col(0), col(1), col(2), pl.BlockSpec((2, D), lambda i: (0, 0)),
                  pl.BlockSpec((SUB, HEADS, HEAD_W, HEAD_W), lambda i: (ns - 1 - i, 0, 0, 0)),
                  pl.BlockSpec((SUB * CHUNK, D), lambda i: (ns - 1 - i, 0))],
        out_specs=[pl.BlockSpec((SUB * CHUNK, 3 * D), lambda i: (ns - 1 - i, 0)),
                   pl.BlockSpec((1, D), lambda i: (0, 0))],
        out_shape=[jax.ShapeDtypeStruct((T, 3 * D), BF16), jax.ShapeDtypeStruct((1, D), F32)],
        scratch_shapes=[pltpu.VMEM((HEADS, HEAD_W, HEAD_W), F32)] + [pltpu.VMEM((SUB * CHUNK, D), F32)] * 6
                       + [pltpu.VMEM((1, D), F32)],
        compiler_params=_params(("arbitrary",)),
    )(proj, proj, proj, lbw, states, do)


def _attn_masks(blk):
    qi = lax.broadcasted_iota(jnp.int32, (ATT_BLOCK, 2 * ATT_BLOCK), 0)
    kj = lax.broadcasted_iota(jnp.int32, (ATT_BLOCK, 2 * ATT_BLOCK), 1)
    band = (kj > qi) & (kj <= qi + ATT_BLOCK)
    return band & ((blk > 0) | (kj >= ATT_BLOCK))


def _head_pair_operand(t, hp, low):
    mine = low if hp == 0 else jnp.logical_not(low)
    both = jnp.where(mine, t, pltpu.roll(t, HEAD_DIM, 1))
    return _bf(jnp.concatenate([jnp.where(low, both, 0.0), jnp.where(low, 0.0, both)], axis=0))


def _attn_probs(s, sink, valid):
    s = jnp.where(valid, s * SCALE, NEG)
    m = jnp.maximum(jnp.max(s, axis=1, keepdims=True), sink)
    p = jnp.exp(s - m)
    es = jnp.exp(sink - m)
    inv = 1.0 / (jnp.sum(p, axis=1, keepdims=True) + es)
    return p * inv, es * inv


def _attn_fwd(proj, sinks):
    T = proj.shape[0]
    nb = T // ATT_BLOCK
    W2 = 2 * ATT_BLOCK

    def body(sink_ref, q_ref, kp_ref, kc_ref, vp_ref, vc_ref, o_ref):
        blk = pl.program_id(0)
        valid = _attn_masks(blk)
        low = lax.broadcasted_iota(jnp.int32, (1, 2 * HEAD_DIM), 1) < HEAD_DIM
        kcat = jnp.concatenate([kp_ref[...], kc_ref[...]], axis=0)
        vcat = jnp.concatenate([vp_ref[...], vc_ref[...]], axis=0)
        for h in range(KV_HEADS):
            tl = slice((h // 2) * 128, (h // 2) * 128 + 128)
            mine = low if h % 2 == 0 else jnp.logical_not(low)
            kh = _bf(jnp.where(mine, kcat[:, tl], pltpu.roll(kcat[:, tl], HEAD_DIM, 1)))
            vh = _bf(jnp.where(mine, vcat[:, tl], pltpu.roll(vcat[:, tl], HEAD_DIM, 1)))
            for t in range(2):
                ql = slice((2 * h + t) * 128, (2 * h + t) * 128 + 128)
                q2 = q_ref[:, ql]
                outs = []
                for p in range(2):
                    qm = _bf(jnp.where(low if p == 0 else jnp.logical_not(low), q2, 0.0))
                    probs, _ = _attn_probs(_dot_nt(qm, kh), sink_ref[0, 4 * h + 2 * t + p], valid)
                    outs.append(_dot(_bf(probs), vh))
                o_ref[:, ql] = jnp.where(low, outs[0], outs[1])

    prev = lambda i: jnp.maximum(i - 1, 0)
    return pl.pallas_call(
        body, name="attn_fwd", grid=(nb,),
        in_specs=[pl.BlockSpec(memory_space=pltpu.SMEM),
                  pl.BlockSpec((ATT_BLOCK, D), lambda i: (i, COL_AQ // D)),
                  pl.BlockSpec((ATT_BLOCK, 256), lambda i: (prev(i), COL_AK // 256)),
                  pl.BlockSpec((ATT_BLOCK, 256), lambda i: (i, COL_AK // 256)),
                  pl.BlockSpec((ATT_BLOCK, 256), lambda i: (prev(i), COL_AV // 256)),
                  pl.BlockSpec((ATT_BLOCK, 256), lambda i: (i, COL_AV // 256))],
        out_specs=pl.BlockSpec((ATT_BLOCK, D), lambda i: (i, 0)),
        out_shape=jax.ShapeDtypeStruct((T, D), F32),
        compiler_params=_params(("arbitrary",)),
    )(sinks, proj, proj, proj, proj, proj)


def _attn_bwd(proj, sinks, o, do):
    T = proj.shape[0]
    nb = T // ATT_BLOCK
    W2 = 2 * ATT_BLOCK

    def body(sink_ref, q_ref, kp_ref, kc_ref, vp_ref, vc_ref, o_ref, do_ref,
             dq_ref, dk_ref, dv_ref, dsink_ref, ck_scr, cv_scr, nk_scr, nv_scr):
        blk = pl.program_id(0)

        @pl.when(blk == 0)
        def _():
            ck_scr[...] = jnp.zeros_like(ck_scr)
            cv_scr[...] = jnp.zeros_like(cv_scr)
            dsink_ref[...] = jnp.zeros_like(dsink_ref)

        @pl.when(blk < nb)
        def _():
            valid = _attn_masks(blk)
            low = lax.broadcasted_iota(jnp.int32, (1, 2 * HEAD_DIM), 1) < HEAD_DIM
            kcat = jnp.concatenate([kp_ref[...], kc_ref[...]], axis=0)
            vcat = jnp.concatenate([vp_ref[...], vc_ref[...]], axis=0)
            for h in range(KV_HEADS):
                tl = slice((h // 2) * 128, (h // 2) * 128 + 128)
                kbd = _head_pair_operand(kcat[:, tl], h % 2, low)
                vbd = _head_pair_operand(vcat[:, tl], h % 2, low)
                dkbd = jnp.zeros((2 * W2, 128), F32)
                dvbd = jnp.zeros((2 * W2, 128), F32)
                tiles = []
                for t in range(2):
                    ql = slice((2 * h + t) * 128, (2 * h + t) * 128 + 128)
                    q2 = _bf(q_ref[:, ql])
                    do2_b = do_ref[:, ql]
                    doo = do2_b.astype(F32) * o_ref[:, ql]
                    dsum0 = jnp.sum(jnp.where(low, doo, 0.0), axis=1, keepdims=True)
                    dsum1 = jnp.sum(jnp.where(low, 0.0, doo), axis=1, keepdims=True)
                    tiles.append((ql, q2, do2_b, dsum0, dsum1, _dot_nt(q2, kbd), _dot_nt(do2_b, vbd)))
                grads = []
                for t, (ql, q2, do2_b, dsum0, dsum1, s2, dp2) in enumerate(tiles):
                    head = 4 * h + 2 * t
                    p0, ps0 = _attn_probs(s2[:, 0:W2], sink_ref[0, head], valid)
                    p1, ps1 = _attn_probs(s2[:, W2:2 * W2], sink_ref[0, head + 1], valid)
                    ds2 = _bf(jnp.concatenate([p0 * (dp2[:, 0:W2] - dsum0), p1 * (dp2[:, W2:2 * W2] - dsum1)], axis=1) * SCALE)
                    grads.append((ds2, _bf(jnp.concatenate([p0, p1], axis=1))))
                    dsink_ref[head:head + 1, :] += jnp.zeros((1, 128), F32) - jnp.sum(ps0 * dsum0, axis=0, keepdims=True)
                    dsink_ref[head + 1:head + 2, :] += jnp.zeros((1, 128), F32) - jnp.sum(ps1 * dsum1, axis=0, keepdims=True)
                for (ql, q2, do2_b, _, _, _, _), (ds2, p2) in zip(tiles, grads):
                    dq_ref[:, ql] = _bf(_dot(ds2, kbd))
                    dkbd = dkbd + _dot_tn(ds2, q2)
                    dvbd = dvbd + _dot_tn(p2, do2_b)
                dk2 = jnp.where(low, dkbd[0:W2], dkbd[W2:2 * W2])
                dv2 = jnp.where(low, dvbd[0:W2], dvbd[W2:2 * W2])
                dk2 = dk2 + pltpu.roll(dk2, HEAD_DIM, 1)
                dv2 = dv2 + pltpu.roll(dv2, HEAD_DIM, 1)
                if h % 2 == 0:
                    keep_k, keep_v = dk2, dv2
                else:
                    nk_scr[:, tl] = jnp.where(low, keep_k, dk2)
                    nv_scr[:, tl] = jnp.where(low, keep_v, dv2)
            dk_ref[...] = _bf(ck_scr[...] + nk_scr[0:ATT_BLOCK, :])
            dv_ref[...] = _bf(cv_scr[...] + nv_scr[0:ATT_BLOCK, :])
            ck_scr[...] = nk_scr[ATT_BLOCK:2 * ATT_BLOCK, :]
            cv_scr[...] = nv_scr[ATT_BLOCK:2 * ATT_BLOCK, :]

        @pl.when(blk == nb)
        def _():
            dk_ref[...] = _bf(ck_scr[...])
            dv_ref[...] = _bf(cv_scr[...])

    cur = lambda i: jnp.minimum(i, nb - 1)
    prev = lambda i: jnp.maximum(cur(i) - 1, 0)
    late = lambda i: jnp.maximum(i - 1, 0)
    dq, dk, dv, dsink = pl.pallas_call(
        body, name="attn_bwd", grid=(nb + 1,),
        in_specs=[pl.BlockSpec(memory_space=pltpu.SMEM),
                  pl.BlockSpec((ATT_BLOCK, D), lambda i: (cur(i), COL_AQ // D)),
                  pl.BlockSpec((ATT_BLOCK, 256), lambda i: (prev(i), COL_AK // 256)),
                  pl.BlockSpec((ATT_BLOCK, 256), lambda i: (cur(i), COL_AK // 256)),
                  pl.BlockSpec((ATT_BLOCK, 256), lambda i: (prev(i), COL_AV // 256)),
                  pl.BlockSpec((ATT_BLOCK, 256), lambda i: (cur(i), COL_AV // 256)),
                  pl.BlockSpec((ATT_BLOCK, D), lambda i: (cur(i), 0)),
                  pl.BlockSpec((ATT_BLOCK, D), lambda i: (cur(i), 0))],
        out_specs=[pl.BlockSpec((ATT_BLOCK, D), lambda i: (cur(i), 0)),
                   pl.BlockSpec((ATT_BLOCK, 256), lambda i: (late(i), 0)),
                   pl.BlockSpec((ATT_BLOCK, 256), lambda i: (late(i), 0)),
                   pl.BlockSpec((16, 128), lambda i: (0, 0))],
        out_shape=[jax.ShapeDtypeStruct((T, D), BF16), jax.ShapeDtypeStruct((T, 256), BF16),
                   jax.ShapeDtypeStruct((T, 256), BF16), jax.ShapeDtypeStruct((16, 128), F32)],
        scratch_shapes=[pltpu.VMEM((ATT_BLOCK, 256), F32), pltpu.VMEM((ATT_BLOCK, 256), F32),
                        pltpu.VMEM((2 * ATT_BLOCK, 256), F32), pltpu.VMEM((2 * ATT_BLOCK, 256), F32)],
        compiler_params=_params(("arbitrary",)),
    )(sinks, proj, proj, proj, proj, proj, o, do)
    return dq, dk, dv, dsink


def _mid(x, tgt, proj, oh, oa, hnw, fnw, wsq_bf):
    T = x.shape[0]
    tm = min(256, T)
    nt = T // tm

    def body(x_ref, tgt_ref, oh_ref, oa_ref, hg_ref, ag0_ref, ag1_ref, mh0_ref, mh1_ref, ma0_ref, ma1_ref,
             hnw_ref, fnw_ref, w_hbm,
             dx2_ref, doh_ref, doa_ref, dhg_ref, dtail_ref, lhs_ref, rhs_ref, loss_ref, vec_ref,
             w_scr, xh_scr, rs_scr, sem):
        @pl.when(pl.program_id(0) == 0)
        def _():
            cp = pltpu.make_async_copy(w_hbm, w_scr, sem)
            cp.start()
            cp.wait()
            loss_ref[...] = jnp.zeros_like(loss_ref)
            vec_ref[...] = jnp.zeros_like(vec_ref)

        oh = oh_ref[...]
        for h in range(HEADS):
            sl = slice(h * HEAD_W, (h + 1) * HEAD_W)
            ohh = oh[:, sl]
            rs = lax.rsqrt(jnp.mean(ohh * ohh, axis=1, keepdims=True) + EPS)
            xh_scr[:, sl] = ohh * rs
            rs_scr[:, sl] = jnp.broadcast_to(rs, (tm, HEAD_W))
        xh = xh_scr[...]
        hnw = hnw_ref[...]
        on = xh * hnw
        hg = hg_ref[...]
        sg = _sigmoid(hg)
        silu_g = hg * sg
        gated_h = _bf(on * silu_g)
        oa = oa_ref[...]
        ag = jnp.concatenate([ag0_ref[...], ag1_ref[...]], axis=1)
        sa = _sigmoid(ag)
        silu_a = ag * sa
        gated_a = _bf(oa * silu_a)
        yh = _dot(gated_h, w_scr[0])
        ya = _dot(gated_a, w_scr[1])
        lhs_ref[0] = gated_h.T
        lhs_ref[1] = gated_a.T
        smh = _sigmoid(jnp.concatenate([mh0_ref[...], mh1_ref[...]], axis=1))
        sma = _sigmoid(jnp.concatenate([ma0_ref[...], ma1_ref[...]], axis=1))
        merged = _bf(smh * yh + sma * ya)
        lhs_ref[2] = merged.T
        x2 = x_ref[...] + _dot(merged, w_scr[2])
        rs2 = lax.rsqrt(jnp.mean(x2 * x2, axis=1, keepdims=True) + EPS)
        xh2 = x2 * rs2
        fnw = fnw_ref[...]
        diff = xh2 * fnw - tgt_ref[...]
        loss_ref[...] += jnp.zeros_like(loss_ref) + jnp.sum(diff * diff) * (0.5 / D)

        dy = diff * (1.0 / D)
        vec_ref[0:1, :] += jnp.sum(dy * xh2, axis=0, keepdims=True)
        gy = dy * fnw
        dx2 = rs2 * (gy - xh2 * jnp.mean(gy * xh2, axis=1, keepdims=True))
        dx2_ref[...] = dx2
        dx2_b = _bf(dx2)
        rhs_ref[2] = dx2_b
        dmerged = _dot_nt(dx2_b, w_scr[2])
        dyh = dmerged * smh
        dya = dmerged * sma
        dtail_ref[:, D:2 * D] = _bf(dyh * yh * (1.0 - smh))
        dtail_ref[:, 2 * D:3 * D] = _bf(dya * ya * (1.0 - sma))
        dyh_b, dya_b = _bf(dyh), _bf(dya)
        rhs_ref[0] = dyh_b
        rhs_ref[1] = dya_b
        dgh = _dot_nt(dyh_b, w_scr[0])
        dga = _dot_nt(dya_b, w_scr[1])
        don = dgh * silu_g
        dhg_ref[...] = _bf(dgh * on * (sg * (1.0 + hg * (1.0 - sg))))
        vec_ref[1:2, :] += jnp.sum(don * xh, axis=0, keepdims=True)
        gxh = don * hnw
        rsb = rs_scr[...]
        for h in range(HEADS):
            sl = slice(h * HEAD_W, (h + 1) * HEAD_W)
            gh, xhh = gxh[:, sl], xh[:, sl]
            doh_ref[:, sl] = _bf(rsb[:, sl] * (gh - xhh * jnp.mean(gh * xhh, axis=1, keepdims=True)))
        doa_ref[...] = _bf(dga * silu_a)
        dtail_ref[:, 0:D] = _bf(dga * oa * (sa * (1.0 + ag * (1.0 - sa))))

    row = lambda w, j: pl.BlockSpec((tm, w), lambda i: (i, j))
    const = lambda r, c: pl.BlockSpec((r, c), lambda i: (0, 0))
    stack = pl.BlockSpec((3, tm, D), lambda i: (0, i, 0))
    stack_t = pl.BlockSpec((3, D, tm), lambda i: (0, 0, i))
    return pl.pallas_call(
        body, name="mid", grid=(nt,),
        in_specs=[row(D, 0), row(D, 0), row(D, 0), row(D, 0), row(D, COL_HG // D),
                  row(512, COL_AG // 512), row(512, COL_AG // 512 + 1),
                  row(512, COL_MH // 512), row(512, COL_MH // 512 + 1),
                  row(512, COL_MA // 512), row(512, COL_MA // 512 + 1),
                  const(1, D), const(1, D), HBM_SPEC],
        out_specs=[row(D, 0), row(D, 0), row(D, 0), row(D, 0), row(3 * D, 0), stack_t, stack, const(8, 128), const(8, D)],
        out_shape=[jax.ShapeDtypeStruct((T, D), F32), jax.ShapeDtypeStruct((T, D), BF16), jax.ShapeDtypeStruct((T, D), BF16),
                   jax.ShapeDtypeStruct((T, D), BF16), jax.ShapeDtypeStruct((T, 3 * D), BF16),
                   jax.ShapeDtypeStruct((3, D, T), BF16), jax.ShapeDtypeStruct((3, T, D), BF16),
                   jax.ShapeDtypeStruct((8, 128), F32), jax.ShapeDtypeStruct((8, D), F32)],
        scratch_shapes=[pltpu.VMEM((3, D, D), BF16), pltpu.VMEM((tm, D), F32), pltpu.VMEM((tm, D), F32),
                        pltpu.SemaphoreType.DMA],
        compiler_params=_params(("arbitrary",)),
    )(x, tgt, oh, oa, proj, proj, proj, proj, proj, proj, proj, hnw, fnw, wsq_bf)


def _wgrad_square(lhs_t, rhs):
    T = rhs.shape[1]
    tk = min(1024, T)

    def body(a_ref, b_ref, g_ref):
        part = _dot(a_ref[...], b_ref[...])

        @pl.when(pl.program_id(1) == 0)
        def _():
            g_ref[...] = part

        @pl.when(pl.program_id(1) > 0)
        def _():
            g_ref[...] += part

    return pl.pallas_call(
        body, name="wgrad_square", grid=(3, T // tk),
        in_specs=[pl.BlockSpec((None, D, tk), lambda k, i: (k, 0, i)), pl.BlockSpec((None, tk, D), lambda k, i: (k, i, 0))],
        out_specs=pl.BlockSpec((None, D, D), lambda k, i: (k, 0, 0)),
        out_shape=jax.ShapeDtypeStruct((3, D, D), F32),
        compiler_params=_params(("parallel", "arbitrary")),
    )(lhs_t, rhs)


def _bwd_dx(pieces, wt_bf, x, norm_w, dx2, swin_b, ssq_b):
    T = x.shape[0]
    tm = min(256, T)
    nt = T // tm
    widths = [p.shape[1] for p in pieces]
    n_p = len(pieces)

    def body(*refs):
        piece_refs = refs[:n_p]
        (w_hbm, x_ref, nw_ref, dx2_ref, swin_ref, ssq_ref,
         gx_ref, gnw_ref, win_got, sq_got, w_scr, sem, send_sems, recv_sems) = refs[n_p:]

        def scatter_copies():
            x_, y_, c_ = _place()
            copies = []
            for k, (fx, fy) in enumerate(CHIP_FLIPS):
                px, py = _flip(x_, fx), _flip(y_, fy)
                jr = 2 * px + py
                for a, (src, dst) in enumerate(((swin_ref.at[:, pl.ds(jr * SHARD_W, SHARD_W)], win_got.at[k]),
                                                (ssq_ref.at[:, pl.ds(jr * SQ_ROWS, SQ_ROWS), :], sq_got.at[k]))):
                    copies.append(pltpu.make_async_remote_copy(
                        src_ref=src, dst_ref=dst, send_sem=send_sems.at[2 * k + a], recv_sem=recv_sems.at[2 * k + a],
                        device_id=(px, py, c_), device_id_type=MESH))
            return copies

        @pl.when(pl.program_id(0) == 0)
        def _():
            for cp in scatter_copies():
                cp.start()
            cp = pltpu.make_async_copy(w_hbm, w_scr, sem)
            cp.start()
            cp.wait()
            gnw_ref[...] = jnp.zeros_like(gnw_ref)

        dxn = None
        off = 0
        for ref, w in zip(piece_refs, widths):
            part = _dot(ref[...], w_scr[off:off + w, :])
            dxn = part if dxn is None else dxn + part
            off += w
        xf = x_ref[...]
        rs = lax.rsqrt(jnp.mean(xf * xf, axis=1, keepdims=True) + EPS)
        xh = xf * rs
        gnw_ref[...] += jnp.sum(dxn * xh, axis=0, keepdims=True)
        gx = dxn * nw_ref[...]
        gx_ref[...] = rs * (gx - xh * jnp.mean(gx * xh, axis=1, keepdims=True)) + dx2_ref[...]

        @pl.when(pl.program_id(0) == nt - 1)
        def _():
            for cp in scatter_copies():
                cp.wait()

    row = lambda w: pl.BlockSpec((tm, w), lambda i: (i, 0))
    return pl.pallas_call(
        body, name="bwd_dx", grid=(nt,),
        in_specs=[row(w) for w in widths] + [HBM_SPEC, row(D), pl.BlockSpec((1, D), lambda i: (0, 0)), row(D), HBM_SPEC, HBM_SPEC],
        out_specs=[row(D), pl.BlockSpec((1, D), lambda i: (0, 0)), HBM_SPEC, HBM_SPEC],
        out_shape=[jax.ShapeDtypeStruct((T, D), F32), jax.ShapeDtypeStruct((1, D), F32),
                   jax.ShapeDtypeStruct((3, D // 2, SHARD_W), BF16), jax.ShapeDtypeStruct((3, 3, SQ_ROWS, D // 2), BF16)],
        scratch_shapes=[pltpu.VMEM((D_IN, D), BF16), pltpu.SemaphoreType.DMA,
                        pltpu.SemaphoreType.DMA((6,)), pltpu.SemaphoreType.DMA((6,))],
        compiler_params=_params(("arbitrary",)),
    )(*pieces, wt_bf, x, norm_w, dx2, swin_b, ssq_b)


W_PIECES = ((0, 1024, 3), (COL_HG, 1024, 1), (COL_AQ, 1024, 1), (COL_AK, 256, 1), (COL_AV, 256, 1), (COL_AG, 512, 6))


def _wgrad_in(xnt_bf, pieces):
    T = xnt_bf.shape[1]
    buf = None
    for n, (piece, (col, wb, blocks)) in enumerate(zip(pieces, W_PIECES)):
        first = buf is None
        tk = min(1024 if wb == 1024 else 2048, T)

        def body(xnt_ref, p_ref, *rest):
            g_ref = rest[-1]
            part = _dot(xnt_ref[...], p_ref[...])

            @pl.when(pl.program_id(1) == 0)
            def _():
                g_ref[...] = part

            @pl.when(pl.program_id(1) > 0)
            def _():
                g_ref[...] += part

        call = pl.pallas_call(
            body, name=f"wgrad_in_{n}", grid=(blocks, T // tk),
            in_specs=[pl.BlockSpec((D, tk), lambda jb, i: (0, i)), pl.BlockSpec((tk, wb), lambda jb, i: (i, jb))]
                     + ([] if first else [HBM_SPEC]),
            out_specs=pl.BlockSpec((D, wb), lambda jb, i, base=col // wb: (0, base + jb)),
            out_shape=jax.ShapeDtypeStruct((D, D_IN), F32),
            input_output_aliases={} if first else {2: 0},
            compiler_params=_params(("parallel", "arbitrary")),
        )
        buf = call(xnt_bf, piece) if first else call(xnt_bf, piece, buf)
    return buf


def _place():
    return lax.axis_index("x"), lax.axis_index("y"), lax.axis_index("c")


def _flip(v, f):
    return 1 - v if f else v


def _win_half(ref, h):
    return ref.at[pl.ds(h * (D // 2), D // 2), :]


def _sq_half(ref, h):
    return ref.at[:, pl.ds(h * (D // 2), D // 2)]


def _gather_copy(part, k, to, send_sems, recv_sems):
    return pltpu.make_async_remote_copy(src_ref=part, dst_ref=part, send_sem=send_sems.at[k], recv_sem=recv_sems.at[k],
                                        device_id=to, device_id_type=MESH)


def _gather_start(out, half, send_sems, recv_sems):
    x, y, c = _place()
    for k, (fx, fy) in enumerate(CHIP_FLIPS):
        _gather_copy(half(out.at[2 * x + y], c), k, (_flip(x, fx), _flip(y, fy), c), send_sems, recv_sems).start()


def _gather_land(out, half, k, send_sems, recv_sems):
    x, y, c = _place()
    sib = (x, y, 1 - c)
    fx, fy = CHIP_FLIPS[k]
    slot = out.at[2 * _flip(x, fx) + _flip(y, fy)]
    _gather_copy(half(slot, c), k, sib, send_sems, recv_sems).wait_recv()
    _gather_copy(half(slot, c), 3 + k, sib, send_sems, recv_sems).start()
    _gather_copy(half(slot, 1 - c), 3 + k, sib, send_sems, recv_sems).wait_recv()


def _gather_drain(out, half, send_sems, recv_sems):
    x, y, c = _place()
    for k, (fx, fy) in enumerate(CHIP_FLIPS):
        _gather_copy(half(out.at[2 * x + y], c), k, (_flip(x, fx), _flip(y, fy), c), send_sems, recv_sems).wait_send()
        _gather_copy(half(out.at[2 * _flip(x, fx) + _flip(y, fy)], c), 3 + k, (x, y, 1 - c), send_sems, recv_sems).wait_send()


def _gather_finish(out, half, send_sems, recv_sems):
    for k in range(len(CHIP_FLIPS)):
        _gather_land(out, half, k, send_sems, recv_sems)
    _gather_drain(out, half, send_sems, recv_sems)


def _swap_halves(gwin, gsq):
    def body(gwin_ref, gsq_ref, win_got, sq_got, send_sems, recv_sems):
        x, y, c = _place()
        sib = (x, y, 1 - c)
        pairs = ((_win_half(gwin_ref, 1 - c), win_got),
                 (gsq_ref.at[:, :, pl.ds((1 - c) * (D // 2), D // 2)], sq_got))
        copies = [pltpu.make_async_remote_copy(src_ref=src, dst_ref=dst, send_sem=send_sems.at[a], recv_sem=recv_sems.at[a],
                                               device_id=sib, device_id_type=MESH) for a, (src, dst) in enumerate(pairs)]
        for cp in copies:
            cp.start()
        for cp in copies:
            cp.wait()

    return pl.pallas_call(
        body, name="swap_halves",
        in_specs=[HBM_SPEC, HBM_SPEC], out_specs=[HBM_SPEC, HBM_SPEC],
        out_shape=[jax.ShapeDtypeStruct((D // 2, D_IN), F32), jax.ShapeDtypeStruct((3, D, D // 2), F32)],
        scratch_shapes=[pltpu.SemaphoreType.DMA((2,)), pltpu.SemaphoreType.DMA((2,))],
    )(gwin, gsq)


def _add_halves(c_arr, gwin, gsq, win_got, sq_got):
    def body(c_ref, a_ref, b_ref, p_ref, q_ref, so_ref, sq_ref, sob_ref, sqb_ref):
        so = a_ref[...] + b_ref[...]
        sq = p_ref[...] + q_ref[...]
        so_ref[...] = so
        sq_ref[...] = sq
        sob_ref[...] = _bf(so)
        sqb_ref[...] = _bf(sq)

    steps = 8
    rows, sq_rows = (D // 2) // steps, D // steps
    win = lambda f: pl.BlockSpec((rows, D_IN), f)
    sq = lambda f: pl.BlockSpec((3, sq_rows, D // 2), f)
    return pl.pallas_call(
        body, name="add_halves",
        grid_spec=pltpu.PrefetchScalarGridSpec(
            num_scalar_prefetch=1, grid=(steps,),
            in_specs=[win(lambda i, c: (c[0] * steps + i, 0)), win(lambda i, c: (i, 0)),
                      sq(lambda i, c: (0, i, c[0])), sq(lambda i, c: (0, i, 0))],
            out_specs=[win(lambda i, c: (i, 0)), sq(lambda i, c: (0, i, 0))] * 2),
        out_shape=[jax.ShapeDtypeStruct((D // 2, D_IN), F32), jax.ShapeDtypeStruct((3, D, D // 2), F32),
                   jax.ShapeDtypeStruct((D // 2, D_IN), BF16), jax.ShapeDtypeStruct((3, D, D // 2), BF16)],
        compiler_params=_params(("arbitrary",)),
    )(c_arr, gwin, win_got, gsq, sq_got)


def _sum_chips(jc_arr, swin, ssq, win_got, sq_got):
    def body(jc_ref, a_ref, b_ref, p_ref, q_ref, so_ref, sq_ref):
        so_ref[...] = ((a_ref[...] + b_ref[0].astype(F32)) + b_ref[1].astype(F32)) + b_ref[2].astype(F32)
        sq_ref[...] = ((p_ref[...] + q_ref[0].astype(F32)) + q_ref[1].astype(F32)) + q_ref[2].astype(F32)

    rows = 128
    steps = (D // 2) // rows
    sq_rows = SQ_ROWS // steps
    return pl.pallas_call(
        body, name="sum_chips",
        grid_spec=pltpu.PrefetchScalarGridSpec(
            num_scalar_prefetch=1, grid=(steps,),
            in_specs=[pl.BlockSpec((rows, SHARD_W), lambda i, jc: (i, jc[0])),
                      pl.BlockSpec((3, rows, SHARD_W), lambda i, jc: (0, i, 0)),
                      pl.BlockSpec((3, sq_rows, D // 2), lambda i, jc: (0, jc[0] * steps + i, 0)),
                      pl.BlockSpec((3, 3, sq_rows, D // 2), lambda i, jc: (0, 0, i, 0))],
            out_specs=[pl.BlockSpec((rows, SHARD_W), lambda i, jc: (jc[1] * steps + i, 0)),
                       pl.BlockSpec((3, sq_rows, D // 2), lambda i, jc: (0, i, jc[1]))]),
        out_shape=[jax.ShapeDtypeStruct((D, SHARD_W), F32), jax.ShapeDtypeStruct((3, SQ_ROWS, D), F32)],
        compiler_params=_params(("arbitrary",)),
    )(jc_arr, swin, win_got, ssq, sq_got)


def _join_halves(g_win, g_sq):
    def body(win_in, sq_in, win_out, sq_out, send_sems, recv_sems):
        del win_in, sq_in
        x, y, c = _place()
        sib = (x, y, 1 - c)

        def halves(h):
            return _win_half(win_out, h), sq_out.at[:, :, pl.ds(h * (D // 2), D // 2)]

        def copy(a, part):
            return pltpu.make_async_remote_copy(src_ref=part, dst_ref=part, send_sem=send_sems.at[a], recv_sem=recv_sems.at[a],
                                                device_id=sib, device_id_type=MESH)

        sent = [copy(a, part) for a, part in enumerate(halves(c))]
        for cp in sent:
            cp.start()
        for a, part in enumerate(halves(1 - c)):
            copy(a, part).wait_recv()
        for cp in sent:
            cp.wait_send()

    return pl.pallas_call(
        body, name="join_halves",
        in_specs=[HBM_SPEC, HBM_SPEC], out_specs=[HBM_SPEC, HBM_SPEC], input_output_aliases={0: 0, 1: 1},
        out_shape=[jax.ShapeDtypeStruct((D, SHARD_W), F32), jax.ShapeDtypeStruct((3, SQ_ROWS, D), F32)],
        scratch_shapes=[pltpu.SemaphoreType.DMA((2,)), pltpu.SemaphoreType.DMA((2,))],
    )(g_win, g_sq)


def _allreduce_small(vec):
    def body(vec_ref, out_ref, slots, send_sems, recv_sems):
        x, y, c = _place()
        me = 4 * x + 2 * y + c
        slots[me] = vec_ref[...]
        copies = []
        for k in range(1, 8):
            fx, fy, fc = (k >> 2) & 1, (k >> 1) & 1, k & 1
            copies.append(pltpu.make_async_remote_copy(
                src_ref=vec_ref, dst_ref=slots.at[me], send_sem=send_sems.at[k - 1], recv_sem=recv_sems.at[k - 1],
                device_id=(_flip(x, fx), _flip(y, fy), _flip(c, fc)), device_id_type=MESH))
        for cp in copies:
            cp.start()
        for k in range(1, 8):
            fx, fy, fc = (k >> 2) & 1, (k >> 1) & 1, k & 1
            src = 4 * _flip(x, fx) + 2 * _flip(y, fy) + _flip(c, fc)
            pltpu.make_async_remote_copy(src_ref=vec_ref, dst_ref=slots.at[src], send_sem=send_sems.at[k - 1],
                                         recv_sem=recv_sems.at[k - 1], device_id=(x, y, c), device_id_type=MESH).wait_recv()
        for cp in copies:
            cp.wait_send()
        total = slots[0]
        for s in range(1, 8):
            total = total + slots[s]
        out_ref[...] = total

    return pl.pallas_call(
        body, name="allreduce_small",
        in_specs=[pl.BlockSpec(memory_space=pltpu.VMEM)], out_specs=pl.BlockSpec(memory_space=pltpu.VMEM),
        out_shape=jax.ShapeDtypeStruct((8, D), F32),
        scratch_shapes=[pltpu.VMEM((8, 8, D), F32), pltpu.SemaphoreType.DMA((7,)), pltpu.SemaphoreType.DMA((7,))],
    )(vec)


def _adamw_math(w, g, m, v):
    m = ADAM_B1 * m + (1.0 - ADAM_B1) * g
    v = ADAM_B2 * v + (1.0 - ADAM_B2) * (g * g)
    m_hat = m / (1.0 - ADAM_B1 ** ADAM_STEP)
    v_hat = v / (1.0 - ADAM_B2 ** ADAM_STEP)
    delta = -ADAM_LR * (m_hat / (jnp.sqrt(v_hat) + ADAM_EPS) + ADAM_WD * w)
    return delta, m, v


def _adamw(name, w, g, m, v, rows):
    R, C = w.shape

    def body(w_ref, g_ref, m_ref, v_ref, d_out, m_out, v_out):
        d_out[...], m_out[...], v_out[...] = _adamw_math(w_ref[...], g_ref[...], m_ref[...], v_ref[...])

    spec = pl.BlockSpec((rows, C), lambda i: (i, 0))
    return pl.pallas_call(
        body, name=name, grid=(R // rows,), in_specs=[spec] * 4, out_specs=[spec] * 3,
        out_shape=[jax.ShapeDtypeStruct((R, C), F32)] * 3,
        compiler_params=_params(("parallel",)),
    )(w, g, m, v)


def _small_update(total, lbw, w8, m8, v8):
    def body(t_ref, lbw_ref, w_ref, m_ref, v_ref, g_out, d_out, m_out, v_out):
        lb = 1.0 / (1.0 + jnp.exp(lbw_ref[1:2, :] - lbw_ref[0:1, :]))
        dlb = t_ref[2:3, :] * lb * (1.0 - lb)
        g_out[...] = jnp.zeros_like(g_out)
        g_out[0:1, :] = t_ref[3:4, :]
        g_out[1:2, :] = dlb
        g_out[2:3, :] = -dlb
        g_out[3:4, :] = t_ref[1:2, :]
        g_out[4:5, :] = t_ref[0:1, :]
        g_out[5:6, :] = t_ref[4:5, :]
        d_out[...], m_out[...], v_out[...] = _adamw_math(w_ref[...], g_out[...], m_ref[...], v_ref[...])

    return pl.pallas_call(
        body, name="small_update", out_shape=[jax.ShapeDtypeStruct((8, D), F32)] * 4,
        compiler_params=_params(),
    )(total, lbw, w8, m8, v8)


def _pack8(norm_w, lbw, hnw, fnw, sinks):
    pad = jnp.zeros((1, D - 16), F32)
    return jnp.concatenate([norm_w, lbw, hnw, fnw.reshape(1, D), jnp.concatenate([sinks, pad], axis=1),
                            jnp.zeros((2, D), F32)], axis=0)


def _unpack8(a):
    return a[0:1], a[1:3], a[3:4], a[5:6, 0:16], a[4]


def _local_step(order_arr, x, tgt, norm_w, lbw, hnw, sinks, fnw, win_mine, wsq_mine, exchange):
    proj, xnt_bf, win_bf = _fwd_proj(order_arr, x, norm_w, win_mine)
    oh, states, wsq_all = _hgrn_fwd(proj, lbw, wsq_mine)
    wsq_bf = wsq_all.reshape(SHARDS, 3, SQ_ROWS, D).transpose(1, 0, 2, 3).reshape(3, D, D)
    oa = _attn_fwd(proj, sinks)
    dx2, doh, doa, dhg, dtail, lhs, rhs, loss8, vec_mid = _mid(x, tgt, proj, oh, oa, hnw, fnw.reshape(1, D), wsq_bf)
    gsq = _wgrad_square(lhs, rhs)
    dhead, dlb = _hgrn_bwd(proj, lbw, states, doh)
    daq, dak, dav, dsink = _attn_bwd(proj, sinks, oa, doa)
    pieces = [dhead, dhg, daq, dak, dav, dtail]
    sums = exchange(_wgrad_in(xnt_bf, pieces), gsq)
    wt_bf = win_bf.transpose(0, 2, 1).reshape(D_IN, D)
    grad_x, gnw, win_got, sq_got = _bwd_dx(pieces, wt_bf, x, norm_w, dx2, sums[2], sums[3])
    sink_row = jnp.concatenate([dsink[:, 0].reshape(1, 16), jnp.zeros((1, D - 16), F32)], axis=1)
    loss_row = jnp.broadcast_to(loss8[0:1, 0:1], (1, D))
    vec = jnp.concatenate([vec_mid[0:2], dlb, gnw, sink_row, loss_row, jnp.zeros((2, D), F32)], axis=0)
    return grad_x, sums, (win_got, sq_got), vec


def kernel(x, norm_w, w_in, hgrn_lower_bound, hgrn_norm_w, w_branch_hgrn, attn_sinks, w_branch_attn, w_out, final_norm_w, loss_target, m_norm_w, m_w_in, m_hgrn_lower_bound, m_hgrn_norm_w, m_w_branch_hgrn, m_attn_sinks, m_w_branch_attn, m_w_out, m_final_norm_w, v_norm_w, v_w_in, v_hgrn_lower_bound, v_hgrn_norm_w, v_w_branch_hgrn, v_attn_sinks, v_w_branch_attn, v_w_out, v_final_norm_w):
    c_arr = lax.axis_index("c").astype(jnp.int32).reshape(1)
    j_arr = (2 * lax.axis_index("x") + lax.axis_index("y")).astype(jnp.int32).reshape(1)
    jc_arr = jnp.concatenate([j_arr, c_arr])

    win_mine, wsq_mine = _cast_shards(j_arr, w_in[0], w_branch_hgrn[0], w_branch_attn[0], w_out[0])
    xi, yi = lax.axis_index("x"), lax.axis_index("y")
    order_arr = jnp.stack([2 * xi + yi] + [2 * _flip(xi, fx) + _flip(yi, fy) for fx, fy in CHIP_FLIPS]).astype(jnp.int32)

    def chip_sums(gwin, gsq):
        return _add_halves(c_arr, gwin, gsq, *_swap_halves(gwin, gsq))

    grad_x, (swin, ssq, _, _), arrived, vec = _local_step(
        order_arr, x[0], loss_target[0], norm_w, hgrn_lower_bound, hgrn_norm_w, attn_sinks, final_norm_w, win_mine, wsq_mine,
        chip_sums)
    g_win, g_sq = _join_halves(*_sum_chips(jc_arr, swin, ssq, *arrived))

    d_win, nm_win, nv_win = _adamw("adamw_w_in", w_in[0], g_win, m_w_in[0], v_w_in[0], 128)
    sq_w = jnp.concatenate([w_branch_hgrn[0], w_branch_attn[0], w_out[0]], axis=0)
    sq_m = jnp.concatenate([m_w_branch_hgrn[0], m_w_branch_attn[0], m_w_out[0]], axis=0)
    sq_v = jnp.concatenate([v_w_branch_hgrn[0], v_w_branch_attn[0], v_w_out[0]], axis=0)
    d_sq, nm_sq, nv_sq = _adamw("adamw_square", sq_w, g_sq.reshape(3 * SQ_ROWS, D), sq_m, sq_v, 256)

    total = _allreduce_small(vec)
    loss = total[5, 0]
    g8, d8, nm8, nv8 = _small_update(
        total, hgrn_lower_bound,
        _pack8(norm_w, hgrn_lower_bound, hgrn_norm_w, final_norm_w, attn_sinks),
        _pack8(m_norm_w, m_hgrn_lower_bound, m_hgrn_norm_w, m_final_norm_w, m_attn_sinks),
        _pack8(v_norm_w, v_hgrn_lower_bound, v_hgrn_norm_w, v_final_norm_w, v_attn_sinks))

    def assemble(win, sq, small):
        nw, lb, hn, sk, fn = _unpack8(small)
        sq = sq.reshape(3, 1, SQ_ROWS, D)
        return (nw, win.reshape(1, D, SHARD_W), lb, hn, sq[0], sk, sq[1], sq[2], fn)

    return (loss, grad_x.reshape(1, -1, D),
            *assemble(g_win, g_sq, g8), *assemble(d_win, d_sq, d8),
            *assemble(nm_win, nm_sq, nm8), *assemble(nv_win, nv_sq, nv8))
```

```python
import functools

import jax
import jax.numpy as jnp
from jax import lax
from jax.experimental import pallas as pl
from jax.experimental.pallas import tpu as pltpu

F32 = jnp.float32
BF16 = jnp.bfloat16

D = 1024
D_IN = 8704
SHARDS = 4
SHARD_W = D_IN // SHARDS
SQ_ROWS = D // SHARDS
HEADS = 8
HEAD_W = 128
CHUNK = 64
SUB = 4
ATT_BLOCK = 128
KV_HEADS = 4
HEAD_DIM = 64
EPS = 1e-6
NEG = -1e30
SCALE = HEAD_DIM ** -0.5
COL_HG, COL_AQ, COL_AK, COL_AV, COL_AG, COL_MH, COL_MA = 3072, 4096, 5120, 5376, 5632, 6656, 7680

ADAM_LR, ADAM_B1, ADAM_B2, ADAM_EPS, ADAM_WD, ADAM_STEP = 0.001, 0.9, 0.999, 1e-08, 0.01, 10

VMEM_LIMIT = 56 * 1024 * 1024
MESH = pl.DeviceIdType.MESH
HBM_SPEC = pl.BlockSpec(memory_space=pltpu.HBM)
CHIP_FLIPS = ((1, 0), (0, 1), (1, 1))


def _dot(a, b):
    return jnp.dot(a, b, preferred_element_type=F32)


def _dot_nt(a, b):
    return lax.dot_general(a, b, (((1,), (1,)), ((), ())), preferred_element_type=F32)


def _dot_tn(a, b):
    return lax.dot_general(a, b, (((0,), (0,)), ((), ())), preferred_element_type=F32)


def _sigmoid(v):
    return 1.0 / (1.0 + jnp.exp(-v))


def _bf(v):
    return v.astype(BF16)


def _split3(v):
    a = _bf(v)
    r = v - a.astype(F32)
    b = _bf(r)
    c = _bf(r - b.astype(F32))
    return a, b, c


def _tri_dot(tri, v):
    a, b, c = _split3(v)
    return _dot(tri, a) + _dot(tri, b) + _dot(tri, c)


def _params(sem=None):
    return pltpu.CompilerParams(dimension_semantics=sem, vmem_limit_bytes=VMEM_LIMIT)


def _cast_shards(j_arr, win_s, wbh_s, wba_s, wout_s):
    steps = 4
    rows = D // steps

    def body(j_ref, win_ref, a_ref, b_ref, c_ref, win_o, sq_o):
        win_o[...] = _bf(win_ref[...])

        @pl.when(pl.program_id(0) == 0)
        def _():
            sq_o[0:SQ_ROWS, :] = _bf(a_ref[...])
            sq_o[SQ_ROWS:2 * SQ_ROWS, :] = _bf(b_ref[...])
            sq_o[2 * SQ_ROWS:3 * SQ_ROWS, :] = _bf(c_ref[...])

    whole = pl.BlockSpec((SQ_ROWS, D), lambda i, j: (0, 0))
    return pl.pallas_call(
        body, name="cast_shards",
        grid_spec=pltpu.PrefetchScalarGridSpec(
            num_scalar_prefetch=1, grid=(steps,),
            in_specs=[pl.BlockSpec((rows, SHARD_W), lambda i, j: (i, 0)), whole, whole, whole],
            out_specs=[pl.BlockSpec((None, rows, SHARD_W), lambda i, j: (j[0], i, 0)),
                       pl.BlockSpec((None, 3 * SQ_ROWS, D), lambda i, j: (j[0], 0, 0))]),
        out_shape=[jax.ShapeDtypeStruct((SHARDS, D, SHARD_W), BF16), jax.ShapeDtypeStruct((SHARDS, 3 * SQ_ROWS, D), BF16)],
        compiler_params=_params(("arbitrary",)),
    )(j_arr, win_s, wbh_s, wba_s, wout_s)


def _fwd_proj(order_arr, x, norm_w, win_all):
    T = x.shape[0]
    tm = min(512, T)
    nt = T // tm

    def body(order_ref, x_ref, nw_ref, win_in, proj_ref, xn_ref, win_out, w_scr, xn_scr, sems, send_sems, recv_sems):
        del win_in
        p, i = pl.program_id(0), pl.program_id(1)

        def load(n):
            return pltpu.make_async_copy(win_out.at[order_ref[n]], w_scr.at[n % 2], sems.at[n % 2])

        @pl.when((p == 0) & (i == 0))
        def _():
            _gather_start(win_out, _win_half, send_sems, recv_sems)
            load(0).start()
            load(0).wait()

        for k in range(SHARDS - 1):
            @pl.when((p == k) & (i == nt // 2))
            def _():
                _gather_land(win_out, _win_half, k, send_sems, recv_sems)
                load(k + 1).start()

            @pl.when((p == k + 1) & (i == 0))
            def _():
                load(k + 1).wait()

        @pl.when(p == 0)
        def _():
            xf = x_ref[...]
            rs = lax.rsqrt(jnp.mean(xf * xf, axis=1, keepdims=True) + EPS)
            xn = _bf((xf * rs) * nw_ref[...])
            xn_scr[i] = xn
            xn_ref[...] = xn.T

        proj_ref[...] = _dot(xn_scr[i], w_scr[p % 2])

        @pl.when((p == SHARDS - 1) & (i == nt - 1))
        def _():
            _gather_drain(win_out, _win_half, send_sems, recv_sems)

    first = lambda p, i: jnp.where(p == 0, i, nt - 1)
    return pl.pallas_call(
        body, name="fwd_proj",
        grid_spec=pltpu.PrefetchScalarGridSpec(
            num_scalar_prefetch=1, grid=(SHARDS, nt),
            in_specs=[pl.BlockSpec((tm, D), lambda p, i, order: (first(p, i), 0)),
                      pl.BlockSpec((1, D), lambda p, i, order: (0, 0)), HBM_SPEC],
            out_specs=[pl.BlockSpec((tm, SHARD_W), lambda p, i, order: (i, order[p])),
                       pl.BlockSpec((D, tm), lambda p, i, order: (0, first(p, i))),
                       HBM_SPEC],
            scratch_shapes=[pltpu.VMEM((2, D, SHARD_W), BF16), pltpu.VMEM((nt, tm, D), BF16), pltpu.SemaphoreType.DMA((2,)),
                            pltpu.SemaphoreType.DMA((6,)), pltpu.SemaphoreType.DMA((6,))]),
        out_shape=[jax.ShapeDtypeStruct((T, D_IN), F32), jax.ShapeDtypeStruct((D, T), BF16),
                   jax.ShapeDtypeStruct((SHARDS, D, SHARD_W), BF16)],
        input_output_aliases={3: 2},
        compiler_params=_params(("arbitrary", "arbitrary")),
    )(order_arr, x, norm_w, win_all)


def _hgrn_gates(hq_ref, hf_ref, lbw_ref, b_scr):
    lb = 1.0 / (1.0 + jnp.exp(lbw_ref[1:2, :] - lbw_ref[0:1, :]))
    hf = hf_ref[...]
    sig = _sigmoid(hf)
    f = lb + (1.0 - lb) * sig
    g = jnp.log(f)
    hq = hq_ref[...]
    sq = _sigmoid(hq)
    q = hq * sq
    row = lax.broadcasted_iota(jnp.int32, (CHUNK, CHUNK), 0)
    col = lax.broadcasted_iota(jnp.int32, (CHUNK, CHUNK), 1)
    causal = row >= col
    b = _tri_dot(jnp.where(causal, 1.0, 0.0).astype(BF16), g)
    b_scr[...] = b
    bc = b_scr[CHUNK - 1:CHUNK, :]
    r = b_scr[CHUNK // 2 - 1:CHUNK // 2, :]
    return dict(lb=lb, sig=sig, f=f, k=1.0 - f, hq=hq, sq=sq, q=q, b=b, bc=bc, r=r, causal=causal)


def _hgrn_fwd(proj, lbw, wsq_all):
    T = proj.shape[0]
    n = T // CHUNK

    def body(hq_ref, hf_ref, hi_ref, lbw_ref, wsq_in, o_ref, st_ref, wsq_out, s_scr, b_scr, send_sems, recv_sems):
        del wsq_in

        @pl.when(pl.program_id(0) == 0)
        def _():
            _gather_start(wsq_out, _sq_half, send_sems, recv_sems)
            s_scr[...] = jnp.zeros_like(s_scr)

        for c in range(SUB):
            rows = pl.ds(c * CHUNK, CHUNK)
            gt = _hgrn_gates(hq_ref.at[rows, :], hf_ref.at[rows, :], lbw_ref, b_scr.at[rows, :])
            b, bc, r, q, k = gt["b"], gt["bc"], gt["r"], gt["q"], gt["k"]
            qe = _bf(q * jnp.exp(b))
            qr = _bf(q * jnp.exp(b - r))
            kr = _bf(k * jnp.exp(r - b))
            kl = _bf(k * jnp.exp(bc - b))
            ebc = jnp.exp(bc)
            v = _bf(hi_ref[rows, :])
            scores = [_bf(jnp.where(gt["causal"], _dot_nt(qr[:, h * HEAD_W:(h + 1) * HEAD_W], kr[:, h * HEAD_W:(h + 1) * HEAD_W]), 0.0))
                      for h in range(HEADS)]
            for h in range(HEADS):
                sl = slice(h * HEAD_W, (h + 1) * HEAD_W)
                st = s_scr[h]
                st_ref[c, h] = st
                o_ref[rows, sl] = _dot(scores[h], v[:, sl]) + _dot_nt(qe[:, sl], _bf(st))
                s_scr[h] = ebc[:, sl] * st + _dot_tn(v[:, sl], kl[:, sl])

        @pl.when(pl.program_id(0) == n // SUB - 1)
        def _():
            _gather_finish(wsq_out, _sq_half, send_sems, recv_sems)

    col = lambda j: pl.BlockSpec((SUB * CHUNK, D), lambda i: (i, j))
    return pl.pallas_call(
        body, name="hgrn_fwd", grid=(n // SUB,),
        in_specs=[col(0), col(1), col(2), pl.BlockSpec((2, D), lambda i: (0, 0)), HBM_SPEC],
        out_specs=[pl.BlockSpec((SUB * CHUNK, D), lambda i: (i, 0)),
                   pl.BlockSpec((SUB, HEADS, HEAD_W, HEAD_W), lambda i: (i, 0, 0, 0)), HBM_SPEC],
        out_shape=[jax.ShapeDtypeStruct((T, D), F32), jax.ShapeDtypeStruct((n, HEADS, HEAD_W, HEAD_W), F32),
                   jax.ShapeDtypeStruct((SHARDS, 3 * SQ_ROWS, D), BF16)],
        input_output_aliases={4: 2},
        scratch_shapes=[pltpu.VMEM((HEADS, HEAD_W, HEAD_W), F32), pltpu.VMEM((SUB * CHUNK, D), F32),
                        pltpu.SemaphoreType.DMA((6,)), pltpu.SemaphoreType.DMA((6,))],
        compiler_params=_params(("arbitrary",)),
    )(proj, proj, proj, lbw, wsq_all)


def _hgrn_bwd(proj, lbw, states, do):
    T = proj.shape[0]
    n = T // CHUNK

    def body(hq_ref, hf_ref, hi_ref, lbw_ref, st_ref, do_ref, dp_ref, dlb_ref,
             ds_scr, b_scr, dq_scr, dk_scr, dv_scr, late_scr, early_scr, ex_scr):
        @pl.when(pl.program_id(0) == 0)
        def _():
            ds_scr[...] = jnp.zeros_like(ds_scr)
            dlb_ref[...] = jnp.zeros_like(dlb_ref)

        for c in reversed(range(SUB)):
            rows = pl.ds(c * CHUNK, CHUNK)
            gt = _hgrn_gates(hq_ref.at[rows, :], hf_ref.at[rows, :], lbw_ref, b_scr.at[rows, :])
            b, bc, r, q, k = gt["b"], gt["bc"], gt["r"], gt["q"], gt["k"]
            eb = jnp.exp(b)
            er = jnp.exp(b - r)
            erk = jnp.exp(r - b)
            el = jnp.exp(bc - b)
            ebc = jnp.exp(bc)
            qe, qr, kr, kl = _bf(q * eb), _bf(q * er), _bf(k * erk), _bf(k * el)
            v = _bf(hi_ref[rows, :])
            do_b = do_ref[rows, :]
            do_t = do_b.T
            causal_t = lax.broadcasted_iota(jnp.int32, (CHUNK, CHUNK), 0) <= lax.broadcasted_iota(jnp.int32, (CHUNK, CHUNK), 1)
            firsts = []
            for h in range(HEADS):
                sl = slice(h * HEAD_W, (h + 1) * HEAD_W)
                firsts.append((_bf(jnp.where(causal_t, _dot_nt(kr[:, sl], qr[:, sl]), 0.0)),
                               _bf(jnp.where(gt["causal"], _dot_nt(do_b[:, sl], v[:, sl]), 0.0)),
                               _bf(jnp.where(causal_t, _dot_nt(v[:, sl], do_b[:, sl]), 0.0))))
            for h in range(HEADS):
                sl = slice(h * HEAD_W, (h + 1) * HEAD_W)
                st0 = st_ref[c, h]
                dst = ds_scr[h]
                dst_b = _bf(dst)
                a_t, da, da_t = firsts[h]
                mq = _dot(da, kr[:, sl])
                mk = _dot(da_t, qr[:, sl])
                dq_in = eb[:, sl] * _dot(do_b[:, sl], _bf(st0))
                dk_in = el[:, sl] * _dot(v[:, sl], dst_b)
                dq_scr[rows, sl] = er[:, sl] * mq + dq_in
                dk_scr[rows, sl] = erk[:, sl] * mk + dk_in
                dv_scr[rows, sl] = _dot(a_t, do_b[:, sl]) + _dot_nt(kl[:, sl], dst_b)
                late_scr[rows, sl] = q[:, sl] * dq_in + qr[:, sl].astype(F32) * mq - kr[:, sl].astype(F32) * mk
                early_scr[rows, sl] = k[:, sl] * dk_in
                ex_scr[:, sl] = jnp.sum(dst * st0, axis=0, keepdims=True)
                ds_scr[h] = ebc[:, sl] * dst + _dot(do_t[sl, :], qe[:, sl])

            dq, dk = dq_scr[rows, :], dk_scr[rows, :]
            row = lax.broadcasted_iota(jnp.int32, (CHUNK, CHUNK), 0)
            col = lax.broadcasted_iota(jnp.int32, (CHUNK, CHUNK), 1)
            at_or_after = jnp.where(col >= row, 1.0, 0.0).astype(BF16)
            before = jnp.where(col < row, 1.0, 0.0).astype(BF16)
            dg = _tri_dot(at_or_after, late_scr[rows, :]) + _tri_dot(before, early_scr[rows, :]) + ebc * ex_scr[...]
            df = dg / gt["f"] - dk
            sig, sq, hq, lb = gt["sig"], gt["sq"], gt["hq"], gt["lb"]
            dp_ref[rows, 0:D] = _bf(dq * (sq * (1.0 + hq * (1.0 - sq))))
            dp_ref[rows, D:2 * D] = _bf(df * (1.0 - lb) * sig * (1.0 - sig))
            dp_ref[rows, 2 * D:3 * D] = _bf(dv_scr[rows, :])
            dlb_ref[...] += jnp.sum(df * (1.0 - sig), axis=0, keepdims=True)

    ns = n // SUB
    col = lambda j: pl.BlockSpec((SUB * CHUNK, D), lambda i: (ns - 1 - i, j))
    return pl.pallas_call(
        body, name="hgrn_bwd", grid=(ns,),
        in_specs=[col(0), col(1), col(2), pl.BlockSpec((2, D), lambda i: (0, 0)),
                  pl.BlockSpec((SUB, HEADS, HEAD_W, HEAD_W), lambda i: (ns - 1 - i, 0, 0, 0)),
                  pl.BlockSpec((SUB * CHUNK, D), lambda i: (ns - 1 - i, 0))],
        out_specs=[pl.BlockSpec((SUB * CHUNK, 3 * D), lambda i: (ns - 1 - i, 0)),
                   pl.BlockSpec((1, D), lambda i: (0, 0))],
        out_shape=[jax.ShapeDtypeStruct((T, 3 * D), BF16), jax.ShapeDtypeStruct((1, D), F32)],
        scratch_shapes=[pltpu.VMEM((HEADS, HEAD_W, HEAD_W), F32)] + [pltpu.VMEM((SUB * CHUNK, D), F32)] * 6
                       + [pltpu.VMEM((1, D), F32)],
        compiler_params=_params(("arbitrary",)),
    )(proj, proj, proj, lbw, states, do)


def _attn_masks(blk):
    qi = lax.broadcasted_iota(jnp.int32, (ATT_BLOCK, 2 * ATT_BLOCK), 0)
    kj = lax.broadcasted_iota(jnp.int32, (ATT_BLOCK, 2 * ATT_BLOCK), 1)
    band = (kj > qi) & (kj <= qi + ATT_BLOCK)
    return band & ((blk > 0) | (kj >= ATT_BLOCK))


def _head_pair_operand(t, hp, low):
    mine = low if hp == 0 else jnp.logical_not(low)
    both = jnp.where(mine, t, pltpu.roll(t, HEAD_DIM, 1))
    return _bf(jnp.concatenate([jnp.where(low, both, 0.0), jnp.where(low, 0.0, both)], axis=0))


def _attn_probs(s, sink, valid):
    s = jnp.where(valid, s * SCALE, NEG)
    m = jnp.maximum(jnp.max(s, axis=1, keepdims=True), sink)
    p = jnp.exp(s - m)
    es = jnp.exp(sink - m)
    inv = 1.0 / (jnp.sum(p, axis=1, keepdims=True) + es)
    return p * inv, es * inv


def _attn_fwd(proj, sinks):
    T = proj.shape[0]
    nb = T // ATT_BLOCK
    W2 = 2 * ATT_BLOCK

    def body(sink_ref, q_ref, kp_ref, kc_ref, vp_ref, vc_ref, o_ref):
        blk = pl.program_id(0)
        valid = _attn_masks(blk)
        low = lax.broadcasted_iota(jnp.int32, (1, 2 * HEAD_DIM), 1) < HEAD_DIM
        kcat = jnp.concatenate([kp_ref[...], kc_ref[...]], axis=0)
        vcat = jnp.concatenate([vp_ref[...], vc_ref[...]], axis=0)
        for h in range(KV_HEADS):
            tl = slice((h // 2) * 128, (h // 2) * 128 + 128)
            mine = low if h % 2 == 0 else jnp.logical_not(low)
            kh = _bf(jnp.where(mine, kcat[:, tl], pltpu.roll(kcat[:, tl], HEAD_DIM, 1)))
            vh = _bf(jnp.where(mine, vcat[:, tl], pltpu.roll(vcat[:, tl], HEAD_DIM, 1)))
            for t in range(2):
                ql = slice((2 * h + t) * 128, (2 * h + t) * 128 + 128)
                q2 = q_ref[:, ql]
                outs = []
                for p in range(2):
                    qm = _bf(jnp.where(low if p == 0 else jnp.logical_not(low), q2, 0.0))
                    probs, _ = _attn_probs(_dot_nt(qm, kh), sink_ref[0, 4 * h + 2 * t + p], valid)
                    outs.append(_dot(_bf(probs), vh))
                o_ref[:, ql] = jnp.where(low, outs[0], outs[1])

    prev = lambda i: jnp.maximum(i - 1, 0)
    return pl.pallas_call(
        body, name="attn_fwd", grid=(nb,),
        in_specs=[pl.BlockSpec(memory_space=pltpu.SMEM),
                  pl.BlockSpec((ATT_BLOCK, D), lambda i: (i, COL_AQ // D)),
                  pl.BlockSpec((ATT_BLOCK, 256), lambda i: (prev(i), COL_AK // 256)),
                  pl.BlockSpec((ATT_BLOCK, 256), lambda i: (i, COL_AK // 256)),
                  pl.BlockSpec((ATT_BLOCK, 256), lambda i: (prev(i), COL_AV // 256)),
                  pl.BlockSpec((ATT_BLOCK, 256), lambda i: (i, COL_AV // 256))],
        out_specs=pl.BlockSpec((ATT_BLOCK, D), lambda i: (i, 0)),
        out_shape=jax.ShapeDtypeStruct((T, D), F32),
        compiler_params=_params(("arbitrary",)),
    )(sinks, proj, proj, proj, proj, proj)


def _attn_bwd(proj, sinks, o, do):
    T = proj.shape[0]
    nb = T // ATT_BLOCK
    W2 = 2 * ATT_BLOCK

    def body(sink_ref, q_ref, kp_ref, kc_ref, vp_ref, vc_ref, o_ref, do_ref,
             dq_ref, dk_ref, dv_ref, dsink_ref, ck_scr, cv_scr, nk_scr, nv_scr):
        blk = pl.program_id(0)

        @pl.when(blk == 0)
        def _():
            ck_scr[...] = jnp.zeros_like(ck_scr)
            cv_scr[...] = jnp.zeros_like(cv_scr)
            dsink_ref[...] = jnp.zeros_like(dsink_ref)

        @pl.when(blk < nb)
        def _():
            valid = _attn_masks(blk)
            low = lax.broadcasted_iota(jnp.int32, (1, 2 * HEAD_DIM), 1) < HEAD_DIM
            kcat = jnp.concatenate([kp_ref[...], kc_ref[...]], axis=0)
            vcat = jnp.concatenate([vp_ref[...], vc_ref[...]], axis=0)
            for h in range(KV_HEADS):
                tl = slice((h // 2) * 128, (h // 2) * 128 + 128)
                kbd = _head_pair_operand(kcat[:, tl], h % 2, low)
                vbd = _head_pair_operand(vcat[:, tl], h % 2, low)
                dkbd = jnp.zeros((2 * W2, 128), F32)
                dvbd = jnp.zeros((2 * W2, 128), F32)
                tiles = []
                for t in range(2):
                    ql = slice((2 * h + t) * 128, (2 * h + t) * 128 + 128)
                    q2 = _bf(q_ref[:, ql])
                    do2_b = do_ref[:, ql]
                    doo = do2_b.astype(F32) * o_ref[:, ql]
                    dsum0 = jnp.sum(jnp.where(low, doo, 0.0), axis=1, keepdims=True)
                    dsum1 = jnp.sum(jnp.where(low, 0.0, doo), axis=1, keepdims=True)
                    tiles.append((ql, q2, do2_b, dsum0, dsum1, _dot_nt(q2, kbd), _dot_nt(do2_b, vbd)))
                grads = []
                for t, (ql, q2, do2_b, dsum0, dsum1, s2, dp2) in enumerate(tiles):
                    head = 4 * h + 2 * t
                    p0, ps0 = _attn_probs(s2[:, 0:W2], sink_ref[0, head], valid)
                    p1, ps1 = _attn_probs(s2[:, W2:2 * W2], sink_ref[0, head + 1], valid)
                    ds2 = _bf(jnp.concatenate([p0 * (dp2[:, 0:W2] - dsum0), p1 * (dp2[:, W2:2 * W2] - dsum1)], axis=1) * SCALE)
                    grads.append((ds2, _bf(jnp.concatenate([p0, p1], axis=1))))
                    dsink_ref[head:head + 1, :] += jnp.zeros((1, 128), F32) - jnp.sum(ps0 * dsum0, axis=0, keepdims=True)
                    dsink_ref[head + 1:head + 2, :] += jnp.zeros((1, 128), F32) - jnp.sum(ps1 * dsum1, axis=0, keepdims=True)
                for (ql, q2, do2_b, _, _, _, _), (ds2, p2) in zip(tiles, grads):
                    dq_ref[:, ql] = _bf(_dot(ds2, kbd))
                    dkbd = dkbd + _dot_tn(ds2, q2)
                    dvbd = dvbd + _dot_tn(p2, do2_b)
                dk2 = jnp.where(low, dkbd[0:W2], dkbd[W2:2 * W2])
                dv2 = jnp.where(low, dvbd[0:W2], dvbd[W2:2 * W2])
                dk2 = dk2 + pltpu.roll(dk2, HEAD_DIM, 1)
                dv2 = dv2 + pltpu.roll(dv2, HEAD_DIM, 1)
                if h % 2 == 0:
                    keep_k, keep_v = dk2, dv2
                else:
                    nk_scr[:, tl] = jnp.where(low, keep_k, dk2)
                    nv_scr[:, tl] = jnp.where(low, keep_v, dv2)
            dk_ref[...] = _bf(ck_scr[...] + nk_scr[0:ATT_BLOCK, :])
            dv_ref[...] = _bf(cv_scr[...] + nv_scr[0:ATT_BLOCK, :])
            ck_scr[...] = nk_scr[ATT_BLOCK:2 * ATT_BLOCK, :]
            cv_scr[...] = nv_scr[ATT_BLOCK:2 * ATT_BLOCK, :]

        @pl.when(blk == nb)
        def _():
            dk_ref[...] = _bf(ck_scr[...])
            dv_ref[...] = _bf(cv_scr[...])

    cur = lambda i: jnp.minimum(i, nb - 1)
    prev = lambda i: jnp.maximum(cur(i) - 1, 0)
    late = lambda i: jnp.maximum(i - 1, 0)
    dq, dk, dv, dsink = pl.pallas_call(
        body, name="attn_bwd", grid=(nb + 1,),
        in_specs=[pl.BlockSpec(memory_space=pltpu.SMEM),
                  pl.BlockSpec((ATT_BLOCK, D), lambda i: (cur(i), COL_AQ // D)),
                  pl.BlockSpec((ATT_BLOCK, 256), lambda i: (prev(i), COL_AK // 256)),
                  pl.BlockSpec((ATT_BLOCK, 256), lambda i: (cur(i), COL_AK // 256)),
                  pl.BlockSpec((ATT_BLOCK, 256), lambda i: (prev(i), COL_AV // 256)),
                  pl.BlockSpec((ATT_BLOCK, 256), lambda i: (cur(i), COL_AV // 256)),
                  pl.BlockSpec((ATT_BLOCK, D), lambda i: (cur(i), 0)),
                  pl.BlockSpec((ATT_BLOCK, D), lambda i: (cur(i), 0))],
        out_specs=[pl.BlockSpec((ATT_BLOCK, D), lambda i: (cur(i), 0)),
                   pl.BlockSpec((ATT_BLOCK, 256), lambda i: (late(i), 0)),
                   pl.BlockSpec((ATT_BLOCK, 256), lambda i: (late(i), 0)),
                   pl.BlockSpec((16, 128), lambda i: (0, 0))],
        out_shape=[jax.ShapeDtypeStruct((T, D), BF16), jax.ShapeDtypeStruct((T, 256), BF16),
                   jax.ShapeDtypeStruct((T, 256), BF16), jax.ShapeDtypeStruct((16, 128), F32)],
        scratch_shapes=[pltpu.VMEM((ATT_BLOCK, 256), F32), pltpu.VMEM((ATT_BLOCK, 256), F32),
                        pltpu.VMEM((2 * ATT_BLOCK, 256), F32), pltpu.VMEM((2 * ATT_BLOCK, 256), F32)],
        compiler_params=_params(("arbitrary",)),
    )(sinks, proj, proj, proj, proj, proj, o, do)
    return dq, dk, dv, dsink


def _mid(x, tgt, proj, oh, oa, hnw, fnw, wsq_bf):
    T = x.shape[0]
    tm = min(256, T)
    nt = T // tm

    def body(x_ref, tgt_ref, oh_ref, oa_ref, hg_ref, ag0_ref, ag1_ref, mh0_ref, mh1_ref, ma0_ref, ma1_ref,
             hnw_ref, fnw_ref, w_hbm,
             dx2_ref, doh_ref, doa_ref, dhg_ref, dtail_ref, lhs_ref, rhs_ref, loss_ref, vec_ref,
             w_scr, xh_scr, rs_scr, sem):
        @pl.when(pl.program_id(0) == 0)
        def _():
            cp = pltpu.make_async_copy(w_hbm, w_scr, sem)
            cp.start()
            cp.wait()
            loss_ref[...] = jnp.zeros_like(loss_ref)
            vec_ref[...] = jnp.zeros_like(vec_ref)

        oh = oh_ref[...]
        for h in range(HEADS):
            sl = slice(h * HEAD_W, (h + 1) * HEAD_W)
            ohh = oh[:, sl]
            rs = lax.rsqrt(jnp.mean(ohh * ohh, axis=1, keepdims=True) + EPS)
            xh_scr[:, sl] = ohh * rs
            rs_scr[:, sl] = jnp.broadcast_to(rs, (tm, HEAD_W))
        xh = xh_scr[...]
        hnw = hnw_ref[...]
        on = xh * hnw
        hg = hg_ref[...]
        sg = _sigmoid(hg)
        silu_g = hg * sg
        gated_h = _bf(on * silu_g)
        oa = oa_ref[...]
        ag = jnp.concatenate([ag0_ref[...], ag1_ref[...]], axis=1)
        sa = _sigmoid(ag)
        silu_a = ag * sa
        gated_a = _bf(oa * silu_a)
        yh = _dot(gated_h, w_scr[0])
        ya = _dot(gated_a, w_scr[1])
        lhs_ref[0] = gated_h.T
        lhs_ref[1] = gated_a.T
        smh = _sigmoid(jnp.concatenate([mh0_ref[...], mh1_ref[...]], axis=1))
        sma = _sigmoid(jnp.concatenate([ma0_ref[...], ma1_ref[...]], axis=1))
        merged = _bf(smh * yh + sma * ya)
        lhs_ref[2] = merged.T
        x2 = x_ref[...] + _dot(merged, w_scr[2])
        rs2 = lax.rsqrt(jnp.mean(x2 * x2, axis=1, keepdims=True) + EPS)
        xh2 = x2 * rs2
        fnw = fnw_ref[...]
        diff = xh2 * fnw - tgt_ref[...]
        loss_ref[...] += jnp.zeros_like(loss_ref) + jnp.sum(diff * diff) * (0.5 / D)

        dy = diff * (1.0 / D)
        vec_ref[0:1, :] += jnp.sum(dy * xh2, axis=0, keepdims=True)
        gy = dy * fnw
        dx2 = rs2 * (gy - xh2 * jnp.mean(gy * xh2, axis=1, keepdims=True))
        dx2_ref[...] = dx2
        dx2_b = _bf(dx2)
        rhs_ref[2] = dx2_b
        dmerged = _dot_nt(dx2_b, w_scr[2])
        dyh = dmerged * smh
        dya = dmerged * sma
        dtail_ref[:, D:2 * D] = _bf(dyh * yh * (1.0 - smh))
        dtail_ref[:, 2 * D:3 * D] = _bf(dya * ya * (1.0 - sma))
        dyh_b, dya_b = _bf(dyh), _bf(dya)
        rhs_ref[0] = dyh_b
        rhs_ref[1] = dya_b
        dgh = _dot_nt(dyh_b, w_scr[0])
        dga = _dot_nt(dya_b, w_scr[1])
        don = dgh * silu_g
        dhg_ref[...] = _bf(dgh * on * (sg * (1.0 + hg * (1.0 - sg))))
        vec_ref[1:2, :] += jnp.sum(don * xh, axis=0, keepdims=True)
        gxh = don * hnw
        rsb = rs_scr[...]
        for h in range(HEADS):
            sl = slice(h * HEAD_W, (h + 1) * HEAD_W)
            gh, xhh = gxh[:, sl], xh[:, sl]
            doh_ref[:, sl] = _bf(rsb[:, sl] * (gh - xhh * jnp.mean(gh * xhh, axis=1, keepdims=True)))
        doa_ref[...] = _bf(dga * silu_a)
        dtail_ref[:, 0:D] = _bf(dga * oa * (sa * (1.0 + ag * (1.0 - sa))))

    row = lambda w, j: pl.BlockSpec((tm, w), lambda i: (i, j))
    const = lambda r, c: pl.BlockSpec((r, c), lambda i: (0, 0))
    stack = pl.BlockSpec((3, tm, D), lambda i: (0, i, 0))
    stack_t = pl.BlockSpec((3, D, tm), lambda i: (0, 0, i))
    return pl.pallas_call(
        body, name="mid", grid=(nt,),
        in_specs=[row(D, 0), row(D, 0), row(D, 0), row(D, 0), row(D, COL_HG // D),
                  row(512, COL_AG // 512), row(512, COL_AG // 512 + 1),
                  row(512, COL_MH // 512), row(512, COL_MH // 512 + 1),
                  row(512, COL_MA // 512), row(512, COL_MA // 512 + 1),
                  const(1, D), const(1, D), HBM_SPEC],
        out_specs=[row(D, 0), row(D, 0), row(D, 0), row(D, 0), row(3 * D, 0), stack_t, stack, const(8, 128), const(8, D)],
        out_shape=[jax.ShapeDtypeStruct((T, D), F32), jax.ShapeDtypeStruct((T, D), BF16), jax.ShapeDtypeStruct((T, D), BF16),
                   jax.ShapeDtypeStruct((T, D), BF16), jax.ShapeDtypeStruct((T, 3 * D), BF16),
                   jax.ShapeDtypeStruct((3, D, T), BF16), jax.ShapeDtypeStruct((3, T, D), BF16),
                   jax.ShapeDtypeStruct((8, 128), F32), jax.ShapeDtypeStruct((8, D), F32)],
        scratch_shapes=[pltpu.VMEM((3, D, D), BF16), pltpu.VMEM((tm, D), F32), pltpu.VMEM((tm, D), F32),
                        pltpu.SemaphoreType.DMA],
        compiler_params=_params(("arbitrary",)),
    )(x, tgt, oh, oa, proj, proj, proj, proj, proj, proj, proj, hnw, fnw, wsq_bf)


def _wgrad_square(lhs_t, rhs):
    T = rhs.shape[1]
    tk = min(1024, T)
    steps = T // tk

    def body(a_ref, b_ref, g_ref, gb_ref):
        part = _dot(a_ref[...], b_ref[...])

        @pl.when(pl.program_id(1) == 0)
        def _():
            g_ref[...] = part

        @pl.when(pl.program_id(1) > 0)
        def _():
            g_ref[...] += part

        @pl.when(pl.program_id(1) == steps - 1)
        def _():
            gb_ref[...] = _bf(g_ref[...])

    return pl.pallas_call(
        body, name="wgrad_square", grid=(3, steps),
        in_specs=[pl.BlockSpec((None, D, tk), lambda k, i: (k, 0, i)), pl.BlockSpec((None, tk, D), lambda k, i: (k, i, 0))],
        out_specs=[pl.BlockSpec((None, D, D), lambda k, i: (k, 0, 0))] * 2,
        out_shape=[jax.ShapeDtypeStruct((3, D, D), F32), jax.ShapeDtypeStruct((3, D, D), BF16)],
        compiler_params=_params(("parallel", "arbitrary")),
    )(lhs_t, rhs)


def _bwd_dx(pieces, wt_bf, x, norm_w, dx2, swin_b, ssq_b):
    T = x.shape[0]
    tm = min(256, T)
    nt = T // tm
    widths = [p.shape[1] for p in pieces]
    n_p = len(pieces)

    def body(*refs):
        piece_refs = refs[:n_p]
        (w_hbm, x_ref, nw_ref, dx2_ref, swin_ref, ssq_ref,
         gx_ref, gnw_ref, win_got, sq_got, w_scr, sem, send_sems, recv_sems) = refs[n_p:]

        def scatter_copies():
            x_, y_, c_ = _place()
            copies = []
            for k, (fx, fy) in enumerate(CHIP_FLIPS):
                px, py = _flip(x_, fx), _flip(y_, fy)
                jr = 2 * px + py
                for a, (src, dst) in enumerate(((swin_ref.at[:, pl.ds(jr * SHARD_W, SHARD_W)], win_got.at[k]),
                                                (ssq_ref.at[:, pl.ds(jr * SQ_ROWS, SQ_ROWS), :], sq_got.at[k]))):
                    copies.append(pltpu.make_async_remote_copy(
                        src_ref=src, dst_ref=dst, send_sem=send_sems.at[2 * k + a], recv_sem=recv_sems.at[2 * k + a],
                        device_id=(px, py, c_), device_id_type=MESH))
            return copies

        @pl.when(pl.program_id(0) == 0)
        def _():
            for cp in scatter_copies():
                cp.start()
            cp = pltpu.make_async_copy(w_hbm, w_scr, sem)
            cp.start()
            cp.wait()
            gnw_ref[...] = jnp.zeros_like(gnw_ref)

        dxn = None
        off = 0
        for ref, w in zip(piece_refs, widths):
            part = _dot(ref[...], w_scr[off:off + w, :])
            dxn = part if dxn is None else dxn + part
            off += w
        xf = x_ref[...]
        rs = lax.rsqrt(jnp.mean(xf * xf, axis=1, keepdims=True) + EPS)
        xh = xf * rs
        gnw_ref[...] += jnp.sum(dxn * xh, axis=0, keepdims=True)
        gx = dxn * nw_ref[...]
        gx_ref[...] = rs * (gx - xh * jnp.mean(gx * xh, axis=1, keepdims=True)) + dx2_ref[...]

        @pl.when(pl.program_id(0) == nt - 1)
        def _():
            for cp in scatter_copies():
                cp.wait()

    row = lambda w: pl.BlockSpec((tm, w), lambda i: (i, 0))
    return pl.pallas_call(
        body, name="bwd_dx", grid=(nt,),
        in_specs=[row(w) for w in widths] + [HBM_SPEC, row(D), pl.BlockSpec((1, D), lambda i: (0, 0)), row(D), HBM_SPEC, HBM_SPEC],
        out_specs=[row(D), pl.BlockSpec((1, D), lambda i: (0, 0)), HBM_SPEC, HBM_SPEC],
        out_shape=[jax.ShapeDtypeStruct((T, D), F32), jax.ShapeDtypeStruct((1, D), F32),
                   jax.ShapeDtypeStruct((3, D // 2, SHARD_W), BF16), jax.ShapeDtypeStruct((3, 3, SQ_ROWS, D // 2), BF16)],
        scratch_shapes=[pltpu.VMEM((D_IN, D), BF16), pltpu.SemaphoreType.DMA,
                        pltpu.SemaphoreType.DMA((6,)), pltpu.SemaphoreType.DMA((6,))],
        compiler_params=_params(("arbitrary",)),
    )(*pieces, wt_bf, x, norm_w, dx2, swin_b, ssq_b)


W_PIECES = ((0, 1024, 3), (COL_HG, 1024, 1), (COL_AQ, 1024, 1), (COL_AK, 256, 1), (COL_AV, 256, 1), (COL_AG, 512, 6))


def _wgrad_in(xnt_bf, pieces):
    T = xnt_bf.shape[1]
    bufs = ()
    for n, (piece, (col, wb, blocks)) in enumerate(zip(pieces, W_PIECES)):
        tk = min(1024 if wb == 1024 else 2048, T)
        steps = T // tk

        def body(xnt_ref, p_ref, *rest):
            g_ref, gb_ref = rest[-2:]
            part = _dot(xnt_ref[...], p_ref[...])

            @pl.when(pl.program_id(1) == 0)
            def _():
                g_ref[...] = part

            @pl.when(pl.program_id(1) > 0)
            def _():
                g_ref[...] += part

            @pl.when(pl.program_id(1) == steps - 1)
            def _():
                gb_ref[...] = _bf(g_ref[...])

        out = pl.BlockSpec((D, wb), lambda jb, i, base=col // wb: (0, base + jb))
        bufs = pl.pallas_call(
            body, name=f"wgrad_in_{n}", grid=(blocks, steps),
            in_specs=[pl.BlockSpec((D, tk), lambda jb, i: (0, i)), pl.BlockSpec((tk, wb), lambda jb, i: (i, jb))]
                     + [HBM_SPEC] * len(bufs),
            out_specs=[out, out],
            out_shape=[jax.ShapeDtypeStruct((D, D_IN), F32), jax.ShapeDtypeStruct((D, D_IN), BF16)],
            input_output_aliases={2: 0, 3: 1} if bufs else {},
            compiler_params=_params(("parallel", "arbitrary")),
        )(xnt_bf, piece, *bufs)
    return bufs


def _place():
    return lax.axis_index("x"), lax.axis_index("y"), lax.axis_index("c")


def _flip(v, f):
    return 1 - v if f else v


def _win_half(ref, h):
    return ref.at[pl.ds(h * (D // 2), D // 2), :]


def _sq_half(ref, h):
    return ref.at[:, pl.ds(h * (D // 2), D // 2)]


def _gather_copy(part, k, to, send_sems, recv_sems):
    return pltpu.make_async_remote_copy(src_ref=part, dst_ref=part, send_sem=send_sems.at[k], recv_sem=recv_sems.at[k],
                                        device_id=to, device_id_type=MESH)


def _gather_start(out, half, send_sems, recv_sems):
    x, y, c = _place()
    for k, (fx, fy) in enumerate(CHIP_FLIPS):
        _gather_copy(half(out.at[2 * x + y], c), k, (_flip(x, fx), _flip(y, fy), c), send_sems, recv_sems).start()


def _gather_land(out, half, k, send_sems, recv_sems):
    x, y, c = _place()
    sib = (x, y, 1 - c)
    fx, fy = CHIP_FLIPS[k]
    slot = out.at[2 * _flip(x, fx) + _flip(y, fy)]
    _gather_copy(half(slot, c), k, sib, send_sems, recv_sems).wait_recv()
    _gather_copy(half(slot, c), 3 + k, sib, send_sems, recv_sems).start()
    _gather_copy(half(slot, 1 - c), 3 + k, sib, send_sems, recv_sems).wait_recv()


def _gather_drain(out, half, send_sems, recv_sems):
    x, y, c = _place()
    for k, (fx, fy) in enumerate(CHIP_FLIPS):
        _gather_copy(half(out.at[2 * x + y], c), k, (_flip(x, fx), _flip(y, fy), c), send_sems, recv_sems).wait_send()
        _gather_copy(half(out.at[2 * _flip(x, fx) + _flip(y, fy)], c), 3 + k, (x, y, 1 - c), send_sems, recv_sems).wait_send()


def _gather_finish(out, half, send_sems, recv_sems):
    for k in range(len(CHIP_FLIPS)):
        _gather_land(out, half, k, send_sems, recv_sems)
    _gather_drain(out, half, send_sems, recv_sems)


def _swap_halves(gwin, gsq):
    def body(gwin_ref, gsq_ref, win_got, sq_got, send_sems, recv_sems):
        x, y, c = _place()
        sib = (x, y, 1 - c)
        pairs = ((_win_half(gwin_ref, 1 - c), win_got),
                 (gsq_ref.at[:, :, pl.ds((1 - c) * (D // 2), D // 2)], sq_got))
        copies = [pltpu.make_async_remote_copy(src_ref=src, dst_ref=dst, send_sem=send_sems.at[a], recv_sem=recv_sems.at[a],
                                               device_id=sib, device_id_type=MESH) for a, (src, dst) in enumerate(pairs)]
        for cp in copies:
            cp.start()
        for cp in copies:
            cp.wait()

    return pl.pallas_call(
        body, name="swap_halves",
        in_specs=[HBM_SPEC, HBM_SPEC], out_specs=[HBM_SPEC, HBM_SPEC],
        out_shape=[jax.ShapeDtypeStruct((D // 2, D_IN), BF16), jax.ShapeDtypeStruct((3, D, D // 2), BF16)],
        scratch_shapes=[pltpu.SemaphoreType.DMA((2,)), pltpu.SemaphoreType.DMA((2,))],
    )(gwin, gsq)


def _add_halves(c_arr, gwin, gsq, win_got, sq_got):
    def body(c_ref, a_ref, b_ref, p_ref, q_ref, so_ref, sq_ref, sob_ref, sqb_ref):
        so = a_ref[...] + b_ref[...].astype(F32)
        sq = p_ref[...] + q_ref[...].astype(F32)
        so_ref[...] = so
        sq_ref[...] = sq
        sob_ref[...] = _bf(so)
        sqb_ref[...] = _bf(sq)

    steps = 8
    rows, sq_rows = (D // 2) // steps, D // steps
    win = lambda f: pl.BlockSpec((rows, D_IN), f)
    sq = lambda f: pl.BlockSpec((3, sq_rows, D // 2), f)
    return pl.pallas_call(
        body, name="add_halves",
        grid_spec=pltpu.PrefetchScalarGridSpec(
            num_scalar_prefetch=1, grid=(steps,),
            in_specs=[win(lambda i, c: (c[0] * steps + i, 0)), win(lambda i, c: (i, 0)),
                      sq(lambda i, c: (0, i, c[0])), sq(lambda i, c: (0, i, 0))],
            out_specs=[win(lambda i, c: (i, 0)), sq(lambda i, c: (0, i, 0))] * 2),
        out_shape=[jax.ShapeDtypeStruct((D // 2, D_IN), F32), jax.ShapeDtypeStruct((3, D, D // 2), F32),
                   jax.ShapeDtypeStruct((D // 2, D_IN), BF16), jax.ShapeDtypeStruct((3, D, D // 2), BF16)],
        compiler_params=_params(("arbitrary",)),
    )(c_arr, gwin, win_got, gsq, sq_got)


def _sum_chips(jc_arr, swin, ssq, win_got, sq_got):
    def body(jc_ref, a_ref, b_ref, p_ref, q_ref, so_ref, sq_ref):
        so_ref[...] = ((a_ref[...] + b_ref[0].astype(F32)) + b_ref[1].astype(F32)) + b_ref[2].astype(F32)
        sq_ref[...] = ((p_ref[...] + q_ref[0].astype(F32)) + q_ref[1].astype(F32)) + q_ref[2].astype(F32)

    rows = 128
    steps = (D // 2) // rows
    sq_rows = SQ_ROWS // steps
    return pl.pallas_call(
        body, name="sum_chips",
        grid_spec=pltpu.PrefetchScalarGridSpec(
            num_scalar_prefetch=1, grid=(steps,),
            in_specs=[pl.BlockSpec((rows, SHARD_W), lambda i, jc: (i, jc[0])),
                      pl.BlockSpec((3, rows, SHARD_W), lambda i, jc: (0, i, 0)),
                      pl.BlockSpec((3, sq_rows, D // 2), lambda i, jc: (0, jc[0] * steps + i, 0)),
                      pl.BlockSpec((3, 3, sq_rows, D // 2), lambda i, jc: (0, 0, i, 0))],
            out_specs=[pl.BlockSpec((rows, SHARD_W), lambda i, jc: (jc[1] * steps + i, 0)),
                       pl.BlockSpec((3, sq_rows, D // 2), lambda i, jc: (0, i, jc[1]))]),
        out_shape=[jax.ShapeDtypeStruct((D, SHARD_W), F32), jax.ShapeDtypeStruct((3, SQ_ROWS, D), F32)],
        compiler_params=_params(("arbitrary",)),
    )(jc_arr, swin, win_got, ssq, sq_got)


def _join_halves(g_win, g_sq):
    def body(win_in, sq_in, win_out, sq_out, send_sems, recv_sems):
        del win_in, sq_in
        x, y, c = _place()
        sib = (x, y, 1 - c)

        def halves(h):
            return _win_half(win_out, h), sq_out.at[:, :, pl.ds(h * (D // 2), D // 2)]

        def copy(a, part):
            return pltpu.make_async_remote_copy(src_ref=part, dst_ref=part, send_sem=send_sems.at[a], recv_sem=recv_sems.at[a],
                                                device_id=sib, device_id_type=MESH)

        sent = [copy(a, part) for a, part in enumerate(halves(c))]
        for cp in sent:
            cp.start()
        for a, part in enumerate(halves(1 - c)):
            copy(a, part).wait_recv()
        for cp in sent:
            cp.wait_send()

    return pl.pallas_call(
        body, name="join_halves",
        in_specs=[HBM_SPEC, HBM_SPEC], out_specs=[HBM_SPEC, HBM_SPEC], input_output_aliases={0: 0, 1: 1},
        out_shape=[jax.ShapeDtypeStruct((D, SHARD_W), F32), jax.ShapeDtypeStruct((3, SQ_ROWS, D), F32)],
        scratch_shapes=[pltpu.SemaphoreType.DMA((2,)), pltpu.SemaphoreType.DMA((2,))],
    )(g_win, g_sq)


def _allreduce_small(vec):
    def body(vec_ref, out_ref, slots, send_sems, recv_sems):
        x, y, c = _place()
        me = 4 * x + 2 * y + c
        slots[me] = vec_ref[...]
        copies = []
        for k in range(1, 8):
            fx, fy, fc = (k >> 2) & 1, (k >> 1) & 1, k & 1
            copies.append(pltpu.make_async_remote_copy(
                src_ref=vec_ref, dst_ref=slots.at[me], send_sem=send_sems.at[k - 1], recv_sem=recv_sems.at[k - 1],
                device_id=(_flip(x, fx), _flip(y, fy), _flip(c, fc)), device_id_type=MESH))
        for cp in copies:
            cp.start()
        for k in range(1, 8):
            fx, fy, fc = (k >> 2) & 1, (k >> 1) & 1, k & 1
            src = 4 * _flip(x, fx) + 2 * _flip(y, fy) + _flip(c, fc)
            pltpu.make_async_remote_copy(src_ref=vec_ref, dst_ref=slots.at[src], send_sem=send_sems.at[k - 1],
                                         recv_sem=recv_sems.at[k - 1], device_id=(x, y, c), device_id_type=MESH).wait_recv()
        for cp in copies:
            cp.wait_send()
        total = slots[0]
        for s in range(1, 8):
            total = total + slots[s]
        out_ref[...] = total

    return pl.pallas_call(
        body, name="allreduce_small",
        in_specs=[pl.BlockSpec(memory_space=pltpu.VMEM)], out_specs=pl.BlockSpec(memory_space=pltpu.VMEM),
        out_shape=jax.ShapeDtypeStruct((8, D), F32),
        scratch_shapes=[pltpu.VMEM((8, 8, D), F32), pltpu.SemaphoreType.DMA((7,)), pltpu.SemaphoreType.DMA((7,))],
    )(vec)


def _adamw_math(w, g, m, v):
    m = ADAM_B1 * m + (1.0 - ADAM_B1) * g
    v = ADAM_B2 * v + (1.0 - ADAM_B2) * (g * g)
    m_hat = m / (1.0 - ADAM_B1 ** ADAM_STEP)
    v_hat = v / (1.0 - ADAM_B2 ** ADAM_STEP)
    delta = -ADAM_LR * (m_hat / (jnp.sqrt(v_hat) + ADAM_EPS) + ADAM_WD * w)
    return delta, m, v


def _adamw(name, w, g, m, v, rows):
    R, C = w.shape

    def body(w_ref, g_ref, m_ref, v_ref, d_out, m_out, v_out):
        d_out[...], m_out[...], v_out[...] = _adamw_math(w_ref[...], g_ref[...], m_ref[...], v_ref[...])

    spec = pl.BlockSpec((rows, C), lambda i: (i, 0))
    return pl.pallas_call(
        body, name=name, grid=(R // rows,), in_specs=[spec] * 4, out_specs=[spec] * 3,
        out_shape=[jax.ShapeDtypeStruct((R, C), F32)] * 3,
        compiler_params=_params(("parallel",)),
    )(w, g, m, v)


def _small_update(total, lbw, w8, m8, v8):
    def body(t_ref, lbw_ref, w_ref, m_ref, v_ref, g_out, d_out, m_out, v_out):
        lb = 1.0 / (1.0 + jnp.exp(lbw_ref[1:2, :] - lbw_ref[0:1, :]))
        dlb = t_ref[2:3, :] * lb * (1.0 - lb)
        g_out[...] = jnp.zeros_like(g_out)
        g_out[0:1, :] = t_ref[3:4, :]
        g_out[1:2, :] = dlb
        g_out[2:3, :] = -dlb
        g_out[3:4, :] = t_ref[1:2, :]
        g_out[4:5, :] = t_ref[0:1, :]
        g_out[5:6, :] = t_ref[4:5, :]
        d_out[...], m_out[...], v_out[...] = _adamw_math(w_ref[...], g_out[...], m_ref[...], v_ref[...])

    return pl.pallas_call(
        body, name="small_update", out_shape=[jax.ShapeDtypeStruct((8, D), F32)] * 4,
        compiler_params=_params(),
    )(total, lbw, w8, m8, v8)


def _pack8(norm_w, lbw, hnw, fnw, sinks):
    pad = jnp.zeros((1, D - 16), F32)
    return jnp.concatenate([norm_w, lbw, hnw, fnw.reshape(1, D), jnp.concatenate([sinks, pad], axis=1),
                            jnp.zeros((2, D), F32)], axis=0)


def _unpack8(a):
    return a[0:1], a[1:3], a[3:4], a[5:6, 0:16], a[4]


def _local_step(order_arr, x, tgt, norm_w, lbw, hnw, sinks, fnw, win_mine, wsq_mine, exchange):
    proj, xnt_bf, win_bf = _fwd_proj(order_arr, x, norm_w, win_mine)
    oh, states, wsq_all = _hgrn_fwd(proj, lbw, wsq_mine)
    wsq_bf = wsq_all.reshape(SHARDS, 3, SQ_ROWS, D).transpose(1, 0, 2, 3).reshape(3, D, D)
    oa = _attn_fwd(proj, sinks)
    dx2, doh, doa, dhg, dtail, lhs, rhs, loss8, vec_mid = _mid(x, tgt, proj, oh, oa, hnw, fnw.reshape(1, D), wsq_bf)
    gsq, gsq_b = _wgrad_square(lhs, rhs)
    dhead, dlb = _hgrn_bwd(proj, lbw, states, doh)
    daq, dak, dav, dsink = _attn_bwd(proj, sinks, oa, doa)
    pieces = [dhead, dhg, daq, dak, dav, dtail]
    sums = exchange(*_wgrad_in(xnt_bf, pieces), gsq, gsq_b)
    wt_bf = win_bf.transpose(0, 2, 1).reshape(D_IN, D)
    grad_x, gnw, win_got, sq_got = _bwd_dx(pieces, wt_bf, x, norm_w, dx2, sums[2], sums[3])
    sink_row = jnp.concatenate([dsink[:, 0].reshape(1, 16), jnp.zeros((1, D - 16), F32)], axis=1)
    loss_row = jnp.broadcast_to(loss8[0:1, 0:1], (1, D))
    vec = jnp.concatenate([vec_mid[0:2], dlb, gnw, sink_row, loss_row, jnp.zeros((2, D), F32)], axis=0)
    return grad_x, sums, (win_got, sq_got), vec


def kernel(x, norm_w, w_in, hgrn_lower_bound, hgrn_norm_w, w_branch_hgrn, attn_sinks, w_branch_attn, w_out, final_norm_w, loss_target, m_norm_w, m_w_in, m_hgrn_lower_bound, m_hgrn_norm_w, m_w_branch_hgrn, m_attn_sinks, m_w_branch_attn, m_w_out, m_final_norm_w, v_norm_w, v_w_in, v_hgrn_lower_bound, v_hgrn_norm_w, v_w_branch_hgrn, v_attn_sinks, v_w_branch_attn, v_w_out, v_final_norm_w):
    c_arr = lax.axis_index("c").astype(jnp.int32).reshape(1)
    j_arr = (2 * lax.axis_index("x") + lax.axis_index("y")).astype(jnp.int32).reshape(1)
    jc_arr = jnp.concatenate([j_arr, c_arr])

    win_mine, wsq_mine = _cast_shards(j_arr, w_in[0], w_branch_hgrn[0], w_branch_attn[0], w_out[0])
    xi, yi = lax.axis_index("x"), lax.axis_index("y")
    order_arr = jnp.stack([2 * xi + yi] + [2 * _flip(xi, fx) + _flip(yi, fy) for fx, fy in CHIP_FLIPS]).astype(jnp.int32)

    def chip_sums(gwin, gwin_b, gsq, gsq_b):
        return _add_halves(c_arr, gwin, gsq, *_swap_halves(gwin_b, gsq_b))

    grad_x, (swin, ssq, _, _), arrived, vec = _local_step(
        order_arr, x[0], loss_target[0], norm_w, hgrn_lower_bound, hgrn_norm_w, attn_sinks, final_norm_w, win_mine, wsq_mine,
        chip_sums)
    g_win, g_sq = _join_halves(*_sum_chips(jc_arr, swin, ssq, *arrived))

    d_win, nm_win, nv_win = _adamw("adamw_w_in", w_in[0], g_win, m_w_in[0], v_w_in[0], 128)
    sq_w = jnp.concatenate([w_branch_hgrn[0], w_branch_attn[0], w_out[0]], axis=0)
    sq_m = jnp.concatenate([m_w_branch_hgrn[0], m_w_branch_attn[0], m_w_out[0]], axis=0)
    sq_v = jnp.concatenate([v_w_branch_hgrn[0], v_w_branch_attn[0], v_w_out[0]], axis=0)
    d_sq, nm_sq, nv_sq = _adamw("adamw_square", sq_w, g_sq.reshape(3 * SQ_ROWS, D), sq_m, sq_v, 256)

    total = _allreduce_small(vec)
    loss = total[5, 0]
    g8, d8, nm8, nv8 = _small_update(
        total, hgrn_lower_bound,
        _pack8(norm_w, hgrn_lower_bound, hgrn_norm_w, final_norm_w, attn_sinks),
        _pack8(m_norm_w, m_hgrn_lower_bound, m_hgrn_norm_w, m_final_norm_w, m_attn_sinks),
        _pack8(v_norm_w, v_hgrn_lower_bound, v_hgrn_norm_w, v_final_norm_w, v_attn_sinks))

    def assemble(win, sq, small):
        nw, lb, hn, sk, fn = _unpack8(small)
        sq = sq.reshape(3, 1, SQ_ROWS, D)
        return (nw, win.reshape(1, D, SHARD_W), lb, hn, sq[0], sk, sq[1], sq[2], fn)

    return (loss, grad_x.reshape(1, -1, D),
            *assemble(g_win, g_sq, g8), *assemble(d_win, d_sq, d8),
            *assemble(nm_win, nm_sq, nm8), *assemble(nv_win, nv_sq, nv8))
```

```python
import functools

import jax
import jax.numpy as jnp
from jax import lax
from jax.experimental import pallas as pl
from jax.experimental.pallas import tpu as pltpu

F32 = jnp.float32
BF16 = jnp.bfloat16

D = 1024
D_IN = 8704
SHARDS = 4
SHARD_W = D_IN // SHARDS
SQ_ROWS = D // SHARDS
HEADS = 8
HEAD_W = 128
CHUNK = 64
SUB = 4
ATT_BLOCK = 128
KV_HEADS = 4
HEAD_DIM = 64
EPS = 1e-6
NEG = -1e30
SCALE = HEAD_DIM ** -0.5
COL_HG, COL_AQ, COL_AK, COL_AV, COL_AG, COL_MH, COL_MA = 3072, 4096, 5120, 5376, 5632, 6656, 7680

ADAM_LR, ADAM_B1, ADAM_B2, ADAM_EPS, ADAM_WD, ADAM_STEP = 0.001, 0.9, 0.999, 1e-08, 0.01, 10

VMEM_LIMIT = 56 * 1024 * 1024
MESH = pl.DeviceIdType.MESH
HBM_SPEC = pl.BlockSpec(memory_space=pltpu.HBM)
CHIP_FLIPS = ((1, 0), (0, 1), (1, 1))


def _dot(a, b):
    return jnp.dot(a, b, preferred_element_type=F32)


def _dot_nt(a, b):
    return lax.dot_general(a, b, (((1,), (1,)), ((), ())), preferred_element_type=F32)


def _dot_tn(a, b):
    return lax.dot_general(a, b, (((0,), (0,)), ((), ())), preferred_element_type=F32)


def _sigmoid(v):
    return 1.0 / (1.0 + jnp.exp(-v))


def _bf(v):
    return v.astype(BF16)


def _split3(v):
    a = _bf(v)
    r = v - a.astype(F32)
    b = _bf(r)
    c = _bf(r - b.astype(F32))
    return a, b, c


def _tri_dot(tri, v):
    a, b, c = _split3(v)
    return _dot(tri, a) + _dot(tri, b) + _dot(tri, c)


def _params(sem=None):
    return pltpu.CompilerParams(dimension_semantics=sem, vmem_limit_bytes=VMEM_LIMIT)


def _cast_shards(j_arr, win_s, wbh_s, wba_s, wout_s):
    steps = 4
    rows = D // steps

    def body(j_ref, win_ref, a_ref, b_ref, c_ref, win_o, sq_o):
        win_o[...] = _bf(win_ref[...])

        @pl.when(pl.program_id(0) == 0)
        def _():
            sq_o[0:SQ_ROWS, :] = _bf(a_ref[...])
            sq_o[SQ_ROWS:2 * SQ_ROWS, :] = _bf(b_ref[...])
            sq_o[2 * SQ_ROWS:3 * SQ_ROWS, :] = _bf(c_ref[...])

    whole = pl.BlockSpec((SQ_ROWS, D), lambda i, j: (0, 0))
    return pl.pallas_call(
        body, name="cast_shards",
        grid_spec=pltpu.PrefetchScalarGridSpec(
            num_scalar_prefetch=1, grid=(steps,),
            in_specs=[pl.BlockSpec((rows, SHARD_W), lambda i, j: (i, 0)), whole, whole, whole],
            out_specs=[pl.BlockSpec((None, rows, SHARD_W), lambda i, j: (j[0], i, 0)),
                       pl.BlockSpec((None, 3 * SQ_ROWS, D), lambda i, j: (j[0], 0, 0))]),
        out_shape=[jax.ShapeDtypeStruct((SHARDS, D, SHARD_W), BF16), jax.ShapeDtypeStruct((SHARDS, 3 * SQ_ROWS, D), BF16)],
        compiler_params=_params(("arbitrary",)),
    )(j_arr, win_s, wbh_s, wba_s, wout_s)


def _fwd_proj(order_arr, x, norm_w, win_all):
    T = x.shape[0]
    tm = min(512, T)
    nt = T // tm

    def body(order_ref, x_ref, nw_ref, win_in, proj_ref, xn_ref, win_out, w_scr, xn_scr, sems, send_sems, recv_sems):
        del win_in
        p, i = pl.program_id(0), pl.program_id(1)

        def load(n):
            return pltpu.make_async_copy(win_out.at[order_ref[n]], w_scr.at[n % 2], sems.at[n % 2])

        @pl.when((p == 0) & (i == 0))
        def _():
            _gather_start(win_out, _win_half, send_sems, recv_sems)
            load(0).start()
            load(0).wait()

        @pl.when((p == 1) & (i == 0))
        def _():
            _gather_land(win_out, _win_half, 0, send_sems, recv_sems)
            load(1).start()
            load(1).wait()

        for k in range(1, SHARDS - 1):
            @pl.when((p == k) & (i == nt // 2))
            def _():
                _gather_land(win_out, _win_half, k, send_sems, recv_sems)
                load(k + 1).start()

            @pl.when((p == k + 1) & (i == 0))
            def _():
                load(k + 1).wait()

        @pl.when(p == 0)
        def _():
            xf = x_ref[...]
            rs = lax.rsqrt(jnp.mean(xf * xf, axis=1, keepdims=True) + EPS)
            xn = _bf((xf * rs) * nw_ref[...])
            xn_scr[i] = xn
            xn_ref[...] = xn.T

        proj_ref[...] = _dot(xn_scr[i], w_scr[p % 2])

        @pl.when((p == SHARDS - 1) & (i == nt - 1))
        def _():
            _gather_drain(win_out, _win_half, send_sems, recv_sems)

    first = lambda p, i: jnp.where(p == 0, i, nt - 1)
    return pl.pallas_call(
        body, name="fwd_proj",
        grid_spec=pltpu.PrefetchScalarGridSpec(
            num_scalar_prefetch=1, grid=(SHARDS, nt),
            in_specs=[pl.BlockSpec((tm, D), lambda p, i, order: (first(p, i), 0)),
                      pl.BlockSpec((1, D), lambda p, i, order: (0, 0)), HBM_SPEC],
            out_specs=[pl.BlockSpec((tm, SHARD_W), lambda p, i, order: (i, order[p])),
                       pl.BlockSpec((D, tm), lambda p, i, order: (0, first(p, i))),
                       HBM_SPEC],
            scratch_shapes=[pltpu.VMEM((2, D, SHARD_W), BF16), pltpu.VMEM((nt, tm, D), BF16), pltpu.SemaphoreType.DMA((2,)),
                            pltpu.SemaphoreType.DMA((6,)), pltpu.SemaphoreType.DMA((6,))]),
        out_shape=[jax.ShapeDtypeStruct((T, D_IN), F32), jax.ShapeDtypeStruct((D, T), BF16),
                   jax.ShapeDtypeStruct((SHARDS, D, SHARD_W), BF16)],
        input_output_aliases={3: 2},
        compiler_params=_params(("arbitrary", "arbitrary")),
    )(order_arr, x, norm_w, win_all)


def _hgrn_gates(hq_ref, hf_ref, lbw_ref, b_scr):
    lb = 1.0 / (1.0 + jnp.exp(lbw_ref[1:2, :] - lbw_ref[0:1, :]))
    hf = hf_ref[...]
    sig = _sigmoid(hf)
    f = lb + (1.0 - lb) * sig
    g = jnp.log(f)
    hq = hq_ref[...]
    sq = _sigmoid(hq)
    q = hq * sq
    row = lax.broadcasted_iota(jnp.int32, (CHUNK, CHUNK), 0)
    col = lax.broadcasted_iota(jnp.int32, (CHUNK, CHUNK), 1)
    causal = row >= col
    b = _tri_dot(jnp.where(causal, 1.0, 0.0).astype(BF16), g)
    b_scr[...] = b
    bc = b_scr[CHUNK - 1:CHUNK, :]
    r = b_scr[CHUNK // 2 - 1:CHUNK // 2, :]
    return dict(lb=lb, sig=sig, f=f, k=1.0 - f, hq=hq, sq=sq, q=q, b=b, bc=bc, r=r, causal=causal)


def _hgrn_fwd(proj, lbw, wsq_all):
    T = proj.shape[0]
    n = T // CHUNK

    def body(hq_ref, hf_ref, hi_ref, lbw_ref, wsq_in, o_ref, st_ref, wsq_out, s_scr, b_scr, send_sems, recv_sems):
        del wsq_in

        @pl.when(pl.program_id(0) == 0)
        def _():
            _gather_start(wsq_out, _sq_half, send_sems, recv_sems)
            s_scr[...] = jnp.zeros_like(s_scr)

        for c in range(SUB):
            rows = pl.ds(c * CHUNK, CHUNK)
            gt = _hgrn_gates(hq_ref.at[rows, :], hf_ref.at[rows, :], lbw_ref, b_scr.at[rows, :])
            b, bc, r, q, k = gt["b"], gt["bc"], gt["r"], gt["q"], gt["k"]
            qe = _bf(q * jnp.exp(b))
            qr = _bf(q * jnp.exp(b - r))
            kr = _bf(k * jnp.exp(r - b))
            kl = _bf(k * jnp.exp(bc - b))
            ebc = jnp.exp(bc)
            v = _bf(hi_ref[rows, :])
            scores = [_bf(jnp.where(gt["causal"], _dot_nt(qr[:, h * HEAD_W:(h + 1) * HEAD_W], kr[:, h * HEAD_W:(h + 1) * HEAD_W]), 0.0))
                      for h in range(HEADS)]
            for h in range(HEADS):
                sl = slice(h * HEAD_W, (h + 1) * HEAD_W)
                st = s_scr[h]
                st_ref[c, h] = st
                o_ref[rows, sl] = _dot(scores[h], v[:, sl]) + _dot_nt(qe[:, sl], _bf(st))
                s_scr[h] = ebc[:, sl] * st + _dot_tn(v[:, sl], kl[:, sl])

        @pl.when(pl.program_id(0) == n // SUB - 1)
        def _():
            _gather_finish(wsq_out, _sq_half, send_sems, recv_sems)

    col = lambda j: pl.BlockSpec((SUB * CHUNK, D), lambda i: (i, j))
    return pl.pallas_call(
        body, name="hgrn_fwd", grid=(n // SUB,),
        in_specs=[col(0), col(1), col(2), pl.BlockSpec((2, D), lambda i: (0, 0)), HBM_SPEC],
        out_specs=[pl.BlockSpec((SUB * CHUNK, D), lambda i: (i, 0)),
                   pl.BlockSpec((SUB, HEADS, HEAD_W, HEAD_W), lambda i: (i, 0, 0, 0)), HBM_SPEC],
        out_shape=[jax.ShapeDtypeStruct((T, D), F32), jax.ShapeDtypeStruct((n, HEADS, HEAD_W, HEAD_W), F32),
                   jax.ShapeDtypeStruct((SHARDS, 3 * SQ_ROWS, D), BF16)],
        input_output_aliases={4: 2},
        scratch_shapes=[pltpu.VMEM((HEADS, HEAD_W, HEAD_W), F32), pltpu.VMEM((SUB * CHUNK, D), F32),
                        pltpu.SemaphoreType.DMA((6,)), pltpu.SemaphoreType.DMA((6,))],
        compiler_params=_params(("arbitrary",)),
    )(proj, proj, proj, lbw, wsq_all)


def _hgrn_bwd(proj, lbw, states, do):
    T = proj.shape[0]
    n = T // CHUNK

    def body(hq_ref, hf_ref, hi_ref, lbw_ref, st_ref, do_ref, dp_ref, dlb_ref,
             ds_scr, b_scr, dq_scr, dk_scr, dv_scr, late_scr, early_scr, ex_scr):
        @pl.when(pl.program_id(0) == 0)
        def _():
            ds_scr[...] = jnp.zeros_like(ds_scr)
            dlb_ref[...] = jnp.zeros_like(dlb_ref)

        for c in reversed(range(SUB)):
            rows = pl.ds(c * CHUNK, CHUNK)
            gt = _hgrn_gates(hq_ref.at[rows, :], hf_ref.at[rows, :], lbw_ref, b_scr.at[rows, :])
            b, bc, r, q, k = gt["b"], gt["bc"], gt["r"], gt["q"], gt["k"]
            eb = jnp.exp(b)
            er = jnp.exp(b - r)
            erk = jnp.exp(r - b)
            el = jnp.exp(bc - b)
            ebc = jnp.exp(bc)
            qe, qr, kr, kl = _bf(q * eb), _bf(q * er), _bf(k * erk), _bf(k * el)
            v = _bf(hi_ref[rows, :])
            do_b = do_ref[rows, :]
            do_t = do_b.T
            causal_t = lax.broadcasted_iota(jnp.int32, (CHUNK, CHUNK), 0) <= lax.broadcasted_iota(jnp.int32, (CHUNK, CHUNK), 1)
            firsts = []
            for h in range(HEADS):
                sl = slice(h * HEAD_W, (h + 1) * HEAD_W)
                firsts.append((_bf(jnp.where(causal_t, _dot_nt(kr[:, sl], qr[:, sl]), 0.0)),
                               _bf(jnp.where(gt["causal"], _dot_nt(do_b[:, sl], v[:, sl]), 0.0)),
                               _bf(jnp.where(causal_t, _dot_nt(v[:, sl], do_b[:, sl]), 0.0))))
            for h in range(HEADS):
                sl = slice(h * HEAD_W, (h + 1) * HEAD_W)
                st0 = st_ref[c, h]
                dst = ds_scr[h]
                dst_b = _bf(dst)
                a_t, da, da_t = firsts[h]
                mq = _dot(da, kr[:, sl])
                mk = _dot(da_t, qr[:, sl])
                dq_in = eb[:, sl] * _dot(do_b[:, sl], _bf(st0))
                dk_in = el[:, sl] * _dot(v[:, sl], dst_b)
                dq_scr[rows, sl] = er[:, sl] * mq + dq_in
                dk_scr[rows, sl] = erk[:, sl] * mk + dk_in
                dv_scr[rows, sl] = _dot(a_t, do_b[:, sl]) + _dot_nt(kl[:, sl], dst_b)
                late_scr[rows, sl] = q[:, sl] * dq_in + qr[:, sl].astype(F32) * mq - kr[:, sl].astype(F32) * mk
                early_scr[rows, sl] = k[:, sl] * dk_in
                ex_scr[:, sl] = jnp.sum(dst * st0, axis=0, keepdims=True)
                ds_scr[h] = ebc[:, sl] * dst + _dot(do_t[sl, :], qe[:, sl])

            dq, dk = dq_scr[rows, :], dk_scr[rows, :]
            row = lax.broadcasted_iota(jnp.int32, (CHUNK, CHUNK), 0)
            col = lax.broadcasted_iota(jnp.int32, (CHUNK, CHUNK), 1)
            at_or_after = jnp.where(col >= row, 1.0, 0.0).astype(BF16)
            before = jnp.where(col < row, 1.0, 0.0).astype(BF16)
            dg = _tri_dot(at_or_after, late_scr[rows, :]) + _tri_dot(before, early_scr[rows, :]) + ebc * ex_scr[...]
            df = dg / gt["f"] - dk
            sig, sq, hq, lb = gt["sig"], gt["sq"], gt["hq"], gt["lb"]
            dp_ref[rows, 0:D] = _bf(dq * (sq * (1.0 + hq * (1.0 - sq))))
            dp_ref[rows, D:2 * D] = _bf(df * (1.0 - lb) * sig * (1.0 - sig))
            dp_ref[rows, 2 * D:3 * D] = _bf(dv_scr[rows, :])
            dlb_ref[...] += jnp.sum(df * (1.0 - sig), axis=0, keepdims=True)

    ns = n // SUB
    col = lambda j: pl.BlockSpec((SUB * CHUNK, D), lambda i: (ns - 1 - i, j))
    return pl.pallas_call(
        body, name="hgrn_bwd", grid=(ns,),
        in_specs=[col(0), col(1), col(2), pl.BlockSpec((2, D), lambda i: (0, 0)),
                  pl.BlockSpec((SUB, HEADS, HEAD_W, HEAD_W), lambda i: (ns - 1 - i, 0, 0, 0)),
                  pl.BlockSpec((SUB * CHUNK, D), lambda i: (ns - 1 - i, 0))],
        out_specs=[pl.BlockSpec((SUB * CHUNK, 3 * D), lambda i: (ns - 1 - i, 0)),
                   pl.BlockSpec((1, D), lambda i: (0, 0))],
        out_shape=[jax.ShapeDtypeStruct((T, 3 * D), BF16), jax.ShapeDtypeStruct((1, D), F32)],
        scratch_shapes=[pltpu.VMEM((HEADS, HEAD_W, HEAD_W), F32)] + [pltpu.VMEM((SUB * CHUNK, D), F32)] * 6
                       + [pltpu.VMEM((1, D), F32)],
        compiler_params=_params(("arbitrary",)),
    )(proj, proj, proj, lbw, states, do)


def _attn_masks(blk):
    qi = lax.broadcasted_iota(jnp.int32, (ATT_BLOCK, 2 * ATT_BLOCK), 0)
    kj = lax.broadcasted_iota(jnp.int32, (ATT_BLOCK, 2 * ATT_BLOCK), 1)
    band = (kj > qi) & (kj <= qi + ATT_BLOCK)
    return band & ((blk > 0) | (kj >= ATT_BLOCK))


def _head_pair_operand(t, hp, low):
    mine = low if hp == 0 else jnp.logical_not(low)
    both = jnp.where(mine, t, pltpu.roll(t, HEAD_DIM, 1))
    return _bf(jnp.concatenate([jnp.where(low, both, 0.0), jnp.where(low, 0.0, both)], axis=0))


def _attn_probs(s, sink, valid):
    s = jnp.where(valid, s * SCALE, NEG)
    m = jnp.maximum(jnp.max(s, axis=1, keepdims=True), sink)
    p = jnp.exp(s - m)
    es = jnp.exp(sink - m)
    inv = 1.0 / (jnp.sum(p, axis=1, keepdims=True) + es)
    return p * inv, es * inv


def _attn_fwd(proj, sinks):
    T = proj.shape[0]
    nb = T // ATT_BLOCK
    W2 = 2 * ATT_BLOCK

    def body(sink_ref, q_ref, kp_ref, kc_ref, vp_ref, vc_ref, o_ref):
        blk = pl.program_id(0)
        valid = _attn_masks(blk)
        low = lax.broadcasted_iota(jnp.int32, (1, 2 * HEAD_DIM), 1) < HEAD_DIM
        kcat = jnp.concatenate([kp_ref[...], kc_ref[...]], axis=0)
        vcat = jnp.concatenate([vp_ref[...], vc_ref[...]], axis=0)
        for h in range(KV_HEADS):
            tl = slice((h // 2) * 128, (h // 2) * 128 + 128)
            mine = low if h % 2 == 0 else jnp.logical_not(low)
            kh = _bf(jnp.where(mine, kcat[:, tl], pltpu.roll(kcat[:, tl], HEAD_DIM, 1)))
            vh = _bf(jnp.where(mine, vcat[:, tl], pltpu.roll(vcat[:, tl], HEAD_DIM, 1)))
            for t in range(2):
                ql = slice((2 * h + t) * 128, (2 * h + t) * 128 + 128)
                q2 = q_ref[:, ql]
                outs = []
                for p in range(2):
                    qm = _bf(jnp.where(low if p == 0 else jnp.logical_not(low), q2, 0.0))
                    probs, _ = _attn_probs(_dot_nt(qm, kh), sink_ref[0, 4 * h + 2 * t + p], valid)
                    outs.append(_dot(_bf(probs), vh))
                o_ref[:, ql] = jnp.where(low, outs[0], outs[1])

    prev = lambda i: jnp.maximum(i - 1, 0)
    return pl.pallas_call(
        body, name="attn_fwd", grid=(nb,),
        in_specs=[pl.BlockSpec(memory_space=pltpu.SMEM),
                  pl.BlockSpec((ATT_BLOCK, D), lambda i: (i, COL_AQ // D)),
                  pl.BlockSpec((ATT_BLOCK, 256), lambda i: (prev(i), COL_AK // 256)),
                  pl.BlockSpec((ATT_BLOCK, 256), lambda i: (i, COL_AK // 256)),
                  pl.BlockSpec((ATT_BLOCK, 256), lambda i: (prev(i), COL_AV // 256)),
                  pl.BlockSpec((ATT_BLOCK, 256), lambda i: (i, COL_AV // 256))],
        out_specs=pl.BlockSpec((ATT_BLOCK, D), lambda i: (i, 0)),
        out_shape=jax.ShapeDtypeStruct((T, D), F32),
        compiler_params=_params(("arbitrary",)),
    )(sinks, proj, proj, proj, proj, proj)


def _attn_bwd(proj, sinks, o, do):
    T = proj.shape[0]
    nb = T // ATT_BLOCK
    W2 = 2 * ATT_BLOCK

    def body(sink_ref, q_ref, kp_ref, kc_ref, vp_ref, vc_ref, o_ref, do_ref,
             dq_ref, dk_ref, dv_ref, dsink_ref, ck_scr, cv_scr, nk_scr, nv_scr):
        blk = pl.program_id(0)

        @pl.when(blk == 0)
        def _():
            ck_scr[...] = jnp.zeros_like(ck_scr)
            cv_scr[...] = jnp.zeros_like(cv_scr)
            dsink_ref[...] = jnp.zeros_like(dsink_ref)

        @pl.when(blk < nb)
        def _():
            valid = _attn_masks(blk)
            low = lax.broadcasted_iota(jnp.int32, (1, 2 * HEAD_DIM), 1) < HEAD_DIM
            kcat = jnp.concatenate([kp_ref[...], kc_ref[...]], axis=0)
            vcat = jnp.concatenate([vp_ref[...], vc_ref[...]], axis=0)
            for h in range(KV_HEADS):
                tl = slice((h // 2) * 128, (h // 2) * 128 + 128)
                kbd = _head_pair_operand(kcat[:, tl], h % 2, low)
                vbd = _head_pair_operand(vcat[:, tl], h % 2, low)
                dkbd = jnp.zeros((2 * W2, 128), F32)
                dvbd = jnp.zeros((2 * W2, 128), F32)
                tiles = []
                for t in range(2):
                    ql = slice((2 * h + t) * 128, (2 * h + t) * 128 + 128)
                    q2 = _bf(q_ref[:, ql])
                    do2_b = do_ref[:, ql]
                    doo = do2_b.astype(F32) * o_ref[:, ql]
                    dsum0 = jnp.sum(jnp.where(low, doo, 0.0), axis=1, keepdims=True)
                    dsum1 = jnp.sum(jnp.where(low, 0.0, doo), axis=1, keepdims=True)
                    tiles.append((ql, q2, do2_b, dsum0, dsum1, _dot_nt(q2, kbd), _dot_nt(do2_b, vbd)))
                grads = []
                for t, (ql, q2, do2_b, dsum0, dsum1, s2, dp2) in enumerate(tiles):
                    head = 4 * h + 2 * t
                    p0, ps0 = _attn_probs(s2[:, 0:W2], sink_ref[0, head], valid)
                    p1, ps1 = _attn_probs(s2[:, W2:2 * W2], sink_ref[0, head + 1], valid)
                    ds2 = _bf(jnp.concatenate([p0 * (dp2[:, 0:W2] - dsum0), p1 * (dp2[:, W2:2 * W2] - dsum1)], axis=1) * SCALE)
                    grads.append((ds2, _bf(jnp.concatenate([p0, p1], axis=1))))
                    dsink_ref[head:head + 1, :] += jnp.zeros((1, 128), F32) - jnp.sum(ps0 * dsum0, axis=0, keepdims=True)
                    dsink_ref[head + 1:head + 2, :] += jnp.zeros((1, 128), F32) - jnp.sum(ps1 * dsum1, axis=0, keepdims=True)
                for (ql, q2, do2_b, _, _, _, _), (ds2, p2) in zip(tiles, grads):
                    dq_ref[:, ql] = _bf(_dot(ds2, kbd))
                    dkbd = dkbd + _dot_tn(ds2, q2)
                    dvbd = dvbd + _dot_tn(p2, do2_b)
                dk2 = jnp.where(low, dkbd[0:W2], dkbd[W2:2 * W2])
                dv2 = jnp.where(low, dvbd[0:W2], dvbd[W2:2 * W2])
                dk2 = dk2 + pltpu.roll(dk2, HEAD_DIM, 1)
                dv2 = dv2 + pltpu.roll(dv2, HEAD_DIM, 1)
                if h % 2 == 0:
                    keep_k, keep_v = dk2, dv2
                else:
                    nk_scr[:, tl] = jnp.where(low, keep_k, dk2)
                    nv_scr[:, tl] = jnp.where(low, keep_v, dv2)
            dk_ref[...] = _bf(ck_scr[...] + nk_scr[0:ATT_BLOCK, :])
            dv_ref[...] = _bf(cv_scr[...] + nv_scr[0:ATT_BLOCK, :])
            ck_scr[...] = nk_scr[ATT_BLOCK:2 * ATT_BLOCK, :]
            cv_scr[...] = nv_scr[ATT_BLOCK:2 * ATT_BLOCK, :]

        @pl.when(blk == nb)
        def _():
            dk_ref[...] = _bf(ck_scr[...])
            dv_ref[...] = _bf(cv_scr[...])

    cur = lambda i: jnp.minimum(i, nb - 1)
    prev = lambda i: jnp.maximum(cur(i) - 1, 0)
    late = lambda i: jnp.maximum(i - 1, 0)
    dq, dk, dv, dsink = pl.pallas_call(
        body, name="attn_bwd", grid=(nb + 1,),
        in_specs=[pl.BlockSpec(memory_space=pltpu.SMEM),
                  pl.BlockSpec((ATT_BLOCK, D), lambda i: (cur(i), COL_AQ // D)),
                  pl.BlockSpec((ATT_BLOCK, 256), lambda i: (prev(i), COL_AK // 256)),
                  pl.BlockSpec((ATT_BLOCK, 256), lambda i: (cur(i), COL_AK // 256)),
                  pl.BlockSpec((ATT_BLOCK, 256), lambda i: (prev(i), COL_AV // 256)),
                  pl.BlockSpec((ATT_BLOCK, 256), lambda i: (cur(i), COL_AV // 256)),
                  pl.BlockSpec((ATT_BLOCK, D), lambda i: (cur(i), 0)),
                  pl.BlockSpec((ATT_BLOCK, D), lambda i: (cur(i), 0))],
        out_specs=[pl.BlockSpec((ATT_BLOCK, D), lambda i: (cur(i), 0)),
                   pl.BlockSpec((ATT_BLOCK, 256), lambda i: (late(i), 0)),
                   pl.BlockSpec((ATT_BLOCK, 256), lambda i: (late(i), 0)),
                   pl.BlockSpec((16, 128), lambda i: (0, 0))],
        out_shape=[jax.ShapeDtypeStruct((T, D), BF16), jax.ShapeDtypeStruct((T, 256), BF16),
                   jax.ShapeDtypeStruct((T, 256), BF16), jax.ShapeDtypeStruct((16, 128), F32)],
        scratch_shapes=[pltpu.VMEM((ATT_BLOCK, 256), F32), pltpu.VMEM((ATT_BLOCK, 256), F32),
                        pltpu.VMEM((2 * ATT_BLOCK, 256), F32), pltpu.VMEM((2 * ATT_BLOCK, 256), F32)],
        compiler_params=_params(("arbitrary",)),
    )(sinks, proj, proj, proj, proj, proj, o, do)
    return dq, dk, dv, dsink


def _mid(x, tgt, proj, oh, oa, hnw, fnw, wsq_bf):
    T = x.shape[0]
    tm = min(256, T)
    nt = T // tm

    def body(x_ref, tgt_ref, oh_ref, oa_ref, hg_ref, ag0_ref, ag1_ref, mh0_ref, mh1_ref, ma0_ref, ma1_ref,
             hnw_ref, fnw_ref, w_hbm,
             dx2_ref, doh_ref, doa_ref, dhg_ref, dtail_ref, lhs_ref, rhs_ref, loss_ref, vec_ref,
             w_scr, xh_scr, rs_scr, sem):
        @pl.when(pl.program_id(0) == 0)
        def _():
            cp = pltpu.make_async_copy(w_hbm, w_scr, sem)
            cp.start()
            cp.wait()
            loss_ref[...] = jnp.zeros_like(loss_ref)
            vec_ref[...] = jnp.zeros_like(vec_ref)

        oh = oh_ref[...]
        for h in range(HEADS):
            sl = slice(h * HEAD_W, (h + 1) * HEAD_W)
            ohh = oh[:, sl]
            rs = lax.rsqrt(jnp.mean(ohh * ohh, axis=1, keepdims=True) + EPS)
            xh_scr[:, sl] = ohh * rs
            rs_scr[:, sl] = jnp.broadcast_to(rs, (tm, HEAD_W))
        xh = xh_scr[...]
        hnw = hnw_ref[...]
        on = xh * hnw
        hg = hg_ref[...]
        sg = _sigmoid(hg)
        silu_g = hg * sg
        gated_h = _bf(on * silu_g)
        oa = oa_ref[...]
        ag = jnp.concatenate([ag0_ref[...], ag1_ref[...]], axis=1)
        sa = _sigmoid(ag)
        silu_a = ag * sa
        gated_a = _bf(oa * silu_a)
        yh = _dot(gated_h, w_scr[0])
        ya = _dot(gated_a, w_scr[1])
        lhs_ref[0] = gated_h.T
        lhs_ref[1] = gated_a.T
        smh = _sigmoid(jnp.concatenate([mh0_ref[...], mh1_ref[...]], axis=1))
        sma = _sigmoid(jnp.concatenate([ma0_ref[...], ma1_ref[...]], axis=1))
        merged = _bf(smh * yh + sma * ya)
        lhs_ref[2] = merged.T
        x2 = x_ref[...] + _dot(merged, w_scr[2])
        rs2 = lax.rsqrt(jnp.mean(x2 * x2, axis=1, keepdims=True) + EPS)
        xh2 = x2 * rs2
        fnw = fnw_ref[...]
        diff = xh2 * fnw - tgt_ref[...]
        loss_ref[...] += jnp.zeros_like(loss_ref) + jnp.sum(diff * diff) * (0.5 / D)

        dy = diff * (1.0 / D)
        vec_ref[0:1, :] += jnp.sum(dy * xh2, axis=0, keepdims=True)
        gy = dy * fnw
        dx2 = rs2 * (gy - xh2 * jnp.mean(gy * xh2, axis=1, keepdims=True))
        dx2_ref[...] = dx2
        dx2_b = _bf(dx2)
        rhs_ref[2] = dx2_b
        dmerged = _dot_nt(dx2_b, w_scr[2])
        dyh = dmerged * smh
        dya = dmerged * sma
        dtail_ref[:, D:2 * D] = _bf(dyh * yh * (1.0 - smh))
        dtail_ref[:, 2 * D:3 * D] = _bf(dya * ya * (1.0 - sma))
        dyh_b, dya_b = _bf(dyh), _bf(dya)
        rhs_ref[0] = dyh_b
        rhs_ref[1] = dya_b
        dgh = _dot_nt(dyh_b, w_scr[0])
        dga = _dot_nt(dya_b, w_scr[1])
        don = dgh * silu_g
        dhg_ref[...] = _bf(dgh * on * (sg * (1.0 + hg * (1.0 - sg))))
        vec_ref[1:2, :] += jnp.sum(don * xh, axis=0, keepdims=True)
        gxh = don * hnw
        rsb = rs_scr[...]
        for h in range(HEADS):
            sl = slice(h * HEAD_W, (h + 1) * HEAD_W)
            gh, xhh = gxh[:, sl], xh[:, sl]
            doh_ref[:, sl] = _bf(rsb[:, sl] * (gh - xhh * jnp.mean(gh * xhh, axis=1, keepdims=True)))
        doa_ref[...] = _bf(dga * silu_a)
        dtail_ref[:, 0:D] = _bf(dga * oa * (sa * (1.0 + ag * (1.0 - sa))))

    row = lambda w, j: pl.BlockSpec((tm, w), lambda i: (i, j))
    const = lambda r, c: pl.BlockSpec((r, c), lambda i: (0, 0))
    stack = pl.BlockSpec((3, tm, D), lambda i: (0, i, 0))
    stack_t = pl.BlockSpec((3, D, tm), lambda i: (0, 0, i))
    return pl.pallas_call(
        body, name="mid", grid=(nt,),
        in_specs=[row(D, 0), row(D, 0), row(D, 0), row(D, 0), row(D, COL_HG // D),
                  row(512, COL_AG // 512), row(512, COL_AG // 512 + 1),
                  row(512, COL_MH // 512), row(512, COL_MH // 512 + 1),
                  row(512, COL_MA // 512), row(512, COL_MA // 512 + 1),
                  const(1, D), const(1, D), HBM_SPEC],
        out_specs=[row(D, 0), row(D, 0), row(D, 0), row(D, 0), row(3 * D, 0), stack_t, stack, const(8, 128), const(8, D)],
        out_shape=[jax.ShapeDtypeStruct((T, D), F32), jax.ShapeDtypeStruct((T, D), BF16), jax.ShapeDtypeStruct((T, D), BF16),
                   jax.ShapeDtypeStruct((T, D), BF16), jax.ShapeDtypeStruct((T, 3 * D), BF16),
                   jax.ShapeDtypeStruct((3, D, T), BF16), jax.ShapeDtypeStruct((3, T, D), BF16),
                   jax.ShapeDtypeStruct((8, 128), F32), jax.ShapeDtypeStruct((8, D), F32)],
        scratch_shapes=[pltpu.VMEM((3, D, D), BF16), pltpu.VMEM((tm, D), F32), pltpu.VMEM((tm, D), F32),
                        pltpu.SemaphoreType.DMA],
        compiler_params=_params(("arbitrary",)),
    )(x, tgt, oh, oa, proj, proj, proj, proj, proj, proj, proj, hnw, fnw, wsq_bf)


def _wgrad_square(lhs_t, rhs):
    T = rhs.shape[1]
    tk = min(1024, T)
    steps = T // tk

    def body(a_ref, b_ref, g_ref, gb_ref):
        part = _dot(a_ref[...], b_ref[...])

        @pl.when(pl.program_id(1) == 0)
        def _():
            g_ref[...] = part

        @pl.when(pl.program_id(1) > 0)
        def _():
            g_ref[...] += part

        @pl.when(pl.program_id(1) == steps - 1)
        def _():
            gb_ref[...] = _bf(g_ref[...])

    return pl.pallas_call(
        body, name="wgrad_square", grid=(3, steps),
        in_specs=[pl.BlockSpec((None, D, tk), lambda k, i: (k, 0, i)), pl.BlockSpec((None, tk, D), lambda k, i: (k, i, 0))],
        out_specs=[pl.BlockSpec((None, D, D), lambda k, i: (k, 0, 0))] * 2,
        out_shape=[jax.ShapeDtypeStruct((3, D, D), F32), jax.ShapeDtypeStruct((3, D, D), BF16)],
        compiler_params=_params(("parallel", "arbitrary")),
    )(lhs_t, rhs)


def _bwd_dx(pieces, wt_bf, x, norm_w, dx2, swin_b, ssq_b):
    T = x.shape[0]
    tm = min(256, T)
    nt = T // tm
    widths = [p.shape[1] for p in pieces]
    n_p = len(pieces)

    def body(*refs):
        piece_refs = refs[:n_p]
        (w_hbm, x_ref, nw_ref, dx2_ref, swin_ref, ssq_ref,
         gx_ref, gnw_ref, win_got, sq_got, w_scr, sem, send_sems, recv_sems) = refs[n_p:]

        def scatter_copies():
            x_, y_, c_ = _place()
            copies = []
            for k, (fx, fy) in enumerate(CHIP_FLIPS):
                px, py = _flip(x_, fx), _flip(y_, fy)
                jr = 2 * px + py
                for a, (src, dst) in enumerate(((swin_ref.at[:, pl.ds(jr * SHARD_W, SHARD_W)], win_got.at[k]),
                                                (ssq_ref.at[:, pl.ds(jr * SQ_ROWS, SQ_ROWS), :], sq_got.at[k]))):
                    copies.append(pltpu.make_async_remote_copy(
                        src_ref=src, dst_ref=dst, send_sem=send_sems.at[2 * k + a], recv_sem=recv_sems.at[2 * k + a],
                        device_id=(px, py, c_), device_id_type=MESH))
            return copies

        @pl.when(pl.program_id(0) == 0)
        def _():
            for cp in scatter_copies():
                cp.start()
            cp = pltpu.make_async_copy(w_hbm, w_scr, sem)
            cp.start()
            cp.wait()
            gnw_ref[...] = jnp.zeros_like(gnw_ref)

        dxn = None
        off = 0
        for ref, w in zip(piece_refs, widths):
            part = _dot(ref[...], w_scr[off:off + w, :])
            dxn = part if dxn is None else dxn + part
            off += w
        xf = x_ref[...]
        rs = lax.rsqrt(jnp.mean(xf * xf, axis=1, keepdims=True) + EPS)
        xh = xf * rs
        gnw_ref[...] += jnp.sum(dxn * xh, axis=0, keepdims=True)
        gx = dxn * nw_ref[...]
        gx_ref[...] = rs * (gx - xh * jnp.mean(gx * xh, axis=1, keepdims=True)) + dx2_ref[...]

        @pl.when(pl.program_id(0) == nt - 1)
        def _():
            for cp in scatter_copies():
                cp.wait()

    row = lambda w: pl.BlockSpec((tm, w), lambda i: (i, 0))
    return pl.pallas_call(
        body, name="bwd_dx", grid=(nt,),
        in_specs=[row(w) for w in widths] + [HBM_SPEC, row(D), pl.BlockSpec((1, D), lambda i: (0, 0)), row(D), HBM_SPEC, HBM_SPEC],
        out_specs=[row(D), pl.BlockSpec((1, D), lambda i: (0, 0)), HBM_SPEC, HBM_SPEC],
        out_shape=[jax.ShapeDtypeStruct((T, D), F32), jax.ShapeDtypeStruct((1, D), F32),
                   jax.ShapeDtypeStruct((3, D // 2, SHARD_W), BF16), jax.ShapeDtypeStruct((3, 3, SQ_ROWS, D // 2), BF16)],
        scratch_shapes=[pltpu.VMEM((D_IN, D), BF16), pltpu.SemaphoreType.DMA,
                        pltpu.SemaphoreType.DMA((6,)), pltpu.SemaphoreType.DMA((6,))],
        compiler_params=_params(("arbitrary",)),
    )(*pieces, wt_bf, x, norm_w, dx2, swin_b, ssq_b)


W_PIECES = ((0, 1024, 3), (COL_HG, 1024, 1), (COL_AQ, 1024, 1), (COL_AK, 256, 1), (COL_AV, 256, 1), (COL_AG, 512, 6))


def _wgrad_in(xnt_bf, pieces):
    T = xnt_bf.shape[1]
    bufs = ()
    for n, (piece, (col, wb, blocks)) in enumerate(zip(pieces, W_PIECES)):
        tk = min(1024 if wb == 1024 else 2048, T)
        steps = T // tk

        def body(xnt_ref, p_ref, *rest):
            g_ref, gb_ref = rest[-2:]
            part = _dot(xnt_ref[...], p_ref[...])

            @pl.when(pl.program_id(1) == 0)
            def _():
                g_ref[...] = part

            @pl.when(pl.program_id(1) > 0)
            def _():
                g_ref[...] += part

            @pl.when(pl.program_id(1) == steps - 1)
            def _():
                gb_ref[...] = _bf(g_ref[...])

        out = pl.BlockSpec((D, wb), lambda jb, i, base=col // wb: (0, base + jb))
        bufs = pl.pallas_call(
            body, name=f"wgrad_in_{n}", grid=(blocks, steps),
            in_specs=[pl.BlockSpec((D, tk), lambda jb, i: (0, i)), pl.BlockSpec((tk, wb), lambda jb, i: (i, jb))]
                     + [HBM_SPEC] * len(bufs),
            out_specs=[out, out],
            out_shape=[jax.ShapeDtypeStruct((D, D_IN), F32), jax.ShapeDtypeStruct((D, D_IN), BF16)],
            input_output_aliases={2: 0, 3: 1} if bufs else {},
            compiler_params=_params(("parallel", "arbitrary")),
        )(xnt_bf, piece, *bufs)
    return bufs


def _place():
    return lax.axis_index("x"), lax.axis_index("y"), lax.axis_index("c")


def _flip(v, f):
    return 1 - v if f else v


def _win_half(ref, h):
    return ref.at[pl.ds(h * (D // 2), D // 2), :]


def _sq_half(ref, h):
    return ref.at[:, pl.ds(h * (D // 2), D // 2)]


def _gather_copy(part, k, to, send_sems, recv_sems):
    return pltpu.make_async_remote_copy(src_ref=part, dst_ref=part, send_sem=send_sems.at[k], recv_sem=recv_sems.at[k],
                                        device_id=to, device_id_type=MESH)


def _gather_start(out, half, send_sems, recv_sems):
    x, y, c = _place()
    for k, (fx, fy) in enumerate(CHIP_FLIPS):
        _gather_copy(half(out.at[2 * x + y], c), k, (_flip(x, fx), _flip(y, fy), c), send_sems, recv_sems).start()


def _gather_land(out, half, k, send_sems, recv_sems):
    x, y, c = _place()
    sib = (x, y, 1 - c)
    fx, fy = CHIP_FLIPS[k]
    slot = out.at[2 * _flip(x, fx) + _flip(y, fy)]
    _gather_copy(half(slot, c), k, sib, send_sems, recv_sems).wait_recv()
    _gather_copy(half(slot, c), 3 + k, sib, send_sems, recv_sems).start()
    _gather_copy(half(slot, 1 - c), 3 + k, sib, send_sems, recv_sems).wait_recv()


def _gather_drain(out, half, send_sems, recv_sems):
    x, y, c = _place()
    for k, (fx, fy) in enumerate(CHIP_FLIPS):
        _gather_copy(half(out.at[2 * x + y], c), k, (_flip(x, fx), _flip(y, fy), c), send_sems, recv_sems).wait_send()
        _gather_copy(half(out.at[2 * _flip(x, fx) + _flip(y, fy)], c), 3 + k, (x, y, 1 - c), send_sems, recv_sems).wait_send()


def _gather_finish(out, half, send_sems, recv_sems):
    for k in range(len(CHIP_FLIPS)):
        _gather_land(out, half, k, send_sems, recv_sems)
    _gather_drain(out, half, send_sems, recv_sems)


def _swap_halves(gwin, gsq):
    def body(gwin_ref, gsq_ref, win_got, sq_got, send_sems, recv_sems):
        x, y, c = _place()
        sib = (x, y, 1 - c)
        pairs = ((_win_half(gwin_ref, 1 - c), win_got),
                 (gsq_ref.at[:, :, pl.ds((1 - c) * (D // 2), D // 2)], sq_got))
        copies = [pltpu.make_async_remote_copy(src_ref=src, dst_ref=dst, send_sem=send_sems.at[a], recv_sem=recv_sems.at[a],
                                               device_id=sib, device_id_type=MESH) for a, (src, dst) in enumerate(pairs)]
        for cp in copies:
            cp.start()
        for cp in copies:
            cp.wait()

    return pl.pallas_call(
        body, name="swap_halves",
        in_specs=[HBM_SPEC, HBM_SPEC], out_specs=[HBM_SPEC, HBM_SPEC],
        out_shape=[jax.ShapeDtypeStruct((D // 2, D_IN), BF16), jax.ShapeDtypeStruct((3, D, D // 2), BF16)],
        scratch_shapes=[pltpu.SemaphoreType.DMA((2,)), pltpu.SemaphoreType.DMA((2,))],
    )(gwin, gsq)


def _add_halves(c_arr, gwin, gsq, win_got, sq_got):
    def body(c_ref, a_ref, b_ref, p_ref, q_ref, so_ref, sq_ref, sob_ref, sqb_ref):
        so = a_ref[...] + b_ref[...].astype(F32)
        sq = p_ref[...] + q_ref[...].astype(F32)
        so_ref[...] = so
        sq_ref[...] = sq
        sob_ref[...] = _bf(so)
        sqb_ref[...] = _bf(sq)

    steps = 8
    rows, sq_rows = (D // 2) // steps, D // steps
    win = lambda f: pl.BlockSpec((rows, D_IN), f)
    sq = lambda f: pl.BlockSpec((3, sq_rows, D // 2), f)
    return pl.pallas_call(
        body, name="add_halves",
        grid_spec=pltpu.PrefetchScalarGridSpec(
            num_scalar_prefetch=1, grid=(steps,),
            in_specs=[win(lambda i, c: (c[0] * steps + i, 0)), win(lambda i, c: (i, 0)),
                      sq(lambda i, c: (0, i, c[0])), sq(lambda i, c: (0, i, 0))],
            out_specs=[win(lambda i, c: (i, 0)), sq(lambda i, c: (0, i, 0))] * 2),
        out_shape=[jax.ShapeDtypeStruct((D // 2, D_IN), F32), jax.ShapeDtypeStruct((3, D, D // 2), F32),
                   jax.ShapeDtypeStruct((D // 2, D_IN), BF16), jax.ShapeDtypeStruct((3, D, D // 2), BF16)],
        compiler_params=_params(("arbitrary",)),
    )(c_arr, gwin, win_got, gsq, sq_got)


def _sum_chips(jc_arr, swin, ssq, win_got, sq_got):
    def body(jc_ref, a_ref, b_ref, p_ref, q_ref, so_ref, sq_ref):
        so_ref[...] = ((a_ref[...] + b_ref[0].astype(F32)) + b_ref[1].astype(F32)) + b_ref[2].astype(F32)
        sq_ref[...] = ((p_ref[...] + q_ref[0].astype(F32)) + q_ref[1].astype(F32)) + q_ref[2].astype(F32)

    rows = 128
    steps = (D // 2) // rows
    sq_rows = SQ_ROWS // steps
    return pl.pallas_call(
        body, name="sum_chips",
        grid_spec=pltpu.PrefetchScalarGridSpec(
            num_scalar_prefetch=1, grid=(steps,),
            in_specs=[pl.BlockSpec((rows, SHARD_W), lambda i, jc: (i, jc[0])),
                      pl.BlockSpec((3, rows, SHARD_W), lambda i, jc: (0, i, 0)),
                      pl.BlockSpec((3, sq_rows, D // 2), lambda i, jc: (0, jc[0] * steps + i, 0)),
                      pl.BlockSpec((3, 3, sq_rows, D // 2), lambda i, jc: (0, 0, i, 0))],
            out_specs=[pl.BlockSpec((rows, SHARD_W), lambda i, jc: (jc[1] * steps + i, 0)),
                       pl.BlockSpec((3, sq_rows, D // 2), lambda i, jc: (0, i, jc[1]))]),
        out_shape=[jax.ShapeDtypeStruct((D, SHARD_W), F32), jax.ShapeDtypeStruct((3, SQ_ROWS, D), F32)],
        compiler_params=_params(("arbitrary",)),
    )(jc_arr, swin, win_got, ssq, sq_got)


def _join_halves(g_win, g_sq):
    def body(win_in, sq_in, win_out, sq_out, send_sems, recv_sems):
        del win_in, sq_in
        x, y, c = _place()
        sib = (x, y, 1 - c)

        def halves(h):
            return _win_half(win_out, h), sq_out.at[:, :, pl.ds(h * (D // 2), D // 2)]

        def copy(a, part):
            return pltpu.make_async_remote_copy(src_ref=part, dst_ref=part, send_sem=send_sems.at[a], recv_sem=recv_sems.at[a],
                                                device_id=sib, device_id_type=MESH)

        sent = [copy(a, part) for a, part in enumerate(halves(c))]
        for cp in sent:
            cp.start()
        for a, part in enumerate(halves(1 - c)):
            copy(a, part).wait_recv()
        for cp in sent:
            cp.wait_send()

    return pl.pallas_call(
        body, name="join_halves",
        in_specs=[HBM_SPEC, HBM_SPEC], out_specs=[HBM_SPEC, HBM_SPEC], input_output_aliases={0: 0, 1: 1},
        out_shape=[jax.ShapeDtypeStruct((D, SHARD_W), F32), jax.ShapeDtypeStruct((3, SQ_ROWS, D), F32)],
        scratch_shapes=[pltpu.SemaphoreType.DMA((2,)), pltpu.SemaphoreType.DMA((2,))],
    )(g_win, g_sq)


def _allreduce_small(vec):
    def body(vec_ref, out_ref, slots, send_sems, recv_sems):
        x, y, c = _place()
        me = 4 * x + 2 * y + c
        slots[me] = vec_ref[...]
        copies = []
        for k in range(1, 8):
            fx, fy, fc = (k >> 2) & 1, (k >> 1) & 1, k & 1
            copies.append(pltpu.make_async_remote_copy(
                src_ref=vec_ref, dst_ref=slots.at[me], send_sem=send_sems.at[k - 1], recv_sem=recv_sems.at[k - 1],
                device_id=(_flip(x, fx), _flip(y, fy), _flip(c, fc)), device_id_type=MESH))
        for cp in copies:
            cp.start()
        for k in range(1, 8):
            fx, fy, fc = (k >> 2) & 1, (k >> 1) & 1, k & 1
            src = 4 * _flip(x, fx) + 2 * _flip(y, fy) + _flip(c, fc)
            pltpu.make_async_remote_copy(src_ref=vec_ref, dst_ref=slots.at[src], send_sem=send_sems.at[k - 1],
                                         recv_sem=recv_sems.at[k - 1], device_id=(x, y, c), device_id_type=MESH).wait_recv()
        for cp in copies:
            cp.wait_send()
        total = slots[0]
        for s in range(1, 8):
            total = total + slots[s]
        out_ref[...] = total

    return pl.pallas_call(
        body, name="allreduce_small",
        in_specs=[pl.BlockSpec(memory_space=pltpu.VMEM)], out_specs=pl.BlockSpec(memory_space=pltpu.VMEM),
        out_shape=jax.ShapeDtypeStruct((8, D), F32),
        scratch_shapes=[pltpu.VMEM((8, 8, D), F32), pltpu.SemaphoreType.DMA((7,)), pltpu.SemaphoreType.DMA((7,))],
    )(vec)


def _adamw_math(w, g, m, v):
    m = ADAM_B1 * m + (1.0 - ADAM_B1) * g
    v = ADAM_B2 * v + (1.0 - ADAM_B2) * (g * g)
    m_hat = m / (1.0 - ADAM_B1 ** ADAM_STEP)
    v_hat = v / (1.0 - ADAM_B2 ** ADAM_STEP)
    delta = -ADAM_LR * (m_hat / (jnp.sqrt(v_hat) + ADAM_EPS) + ADAM_WD * w)
    return delta, m, v


def _adamw(name, w, g, m, v, rows):
    R, C = w.shape

    def body(w_ref, g_ref, m_ref, v_ref, d_out, m_out, v_out):
        d_out[...], m_out[...], v_out[...] = _adamw_math(w_ref[...], g_ref[...], m_ref[...], v_ref[...])

    spec = pl.BlockSpec((rows, C), lambda i: (i, 0))
    return pl.pallas_call(
        body, name=name, grid=(R // rows,), in_specs=[spec] * 4, out_specs=[spec] * 3,
        out_shape=[jax.ShapeDtypeStruct((R, C), F32)] * 3,
        compiler_params=_params(("parallel",)),
    )(w, g, m, v)


def _adamw_square(g_sq, ws, ms, vs):
    def body(g_ref, *refs):
        w_refs, m_refs, v_refs, outs = refs[0:3], refs[3:6], refs[6:9], refs[9:]
        for k in range(3):
            g = g_ref[k]
            outs[k][0] = g
            outs[3 + k][0], outs[6 + k][0], outs[9 + k][0] = _adamw_math(w_refs[k][0], g, m_refs[k][0], v_refs[k][0])

    out = pl.pallas_call(
        body, name="adamw_square", out_shape=[jax.ShapeDtypeStruct((1, SQ_ROWS, D), F32)] * 12,
        compiler_params=_params(),
    )(g_sq, *ws, *ms, *vs)
    return out[0:3], out[3:6], out[6:9], out[9:12]


def _small_update(total, lbw, w8, m8, v8):
    def body(t_ref, lbw_ref, w_ref, m_ref, v_ref, g_out, d_out, m_out, v_out):
        lb = 1.0 / (1.0 + jnp.exp(lbw_ref[1:2, :] - lbw_ref[0:1, :]))
        dlb = t_ref[2:3, :] * lb * (1.0 - lb)
        g_out[...] = jnp.zeros_like(g_out)
        g_out[0:1, :] = t_ref[3:4, :]
        g_out[1:2, :] = dlb
        g_out[2:3, :] = -dlb
        g_out[3:4, :] = t_ref[1:2, :]
        g_out[4:5, :] = t_ref[0:1, :]
        g_out[5:6, :] = t_ref[4:5, :]
        d_out[...], m_out[...], v_out[...] = _adamw_math(w_ref[...], g_out[...], m_ref[...], v_ref[...])

    return pl.pallas_call(
        body, name="small_update", out_shape=[jax.ShapeDtypeStruct((8, D), F32)] * 4,
        compiler_params=_params(),
    )(total, lbw, w8, m8, v8)


def _pack8(norm_w, lbw, hnw, fnw, sinks):
    pad = jnp.zeros((1, D - 16), F32)
    return jnp.concatenate([norm_w, lbw, hnw, fnw.reshape(1, D), jnp.concatenate([sinks, pad], axis=1),
                            jnp.zeros((2, D), F32)], axis=0)


def _unpack8(a):
    return a[0:1], a[1:3], a[3:4], a[5:6, 0:16], a[4]


def _local_step(order_arr, x, tgt, norm_w, lbw, hnw, sinks, fnw, win_mine, wsq_mine, exchange):
    proj, xnt_bf, win_bf = _fwd_proj(order_arr, x, norm_w, win_mine)
    oh, states, wsq_all = _hgrn_fwd(proj, lbw, wsq_mine)
    wsq_bf = wsq_all.reshape(SHARDS, 3, SQ_ROWS, D).transpose(1, 0, 2, 3).reshape(3, D, D)
    oa = _attn_fwd(proj, sinks)
    dx2, doh, doa, dhg, dtail, lhs, rhs, loss8, vec_mid = _mid(x, tgt, proj, oh, oa, hnw, fnw.reshape(1, D), wsq_bf)
    gsq, gsq_b = _wgrad_square(lhs, rhs)
    dhead, dlb = _hgrn_bwd(proj, lbw, states, doh)
    daq, dak, dav, dsink = _attn_bwd(proj, sinks, oa, doa)
    pieces = [dhead, dhg, daq, dak, dav, dtail]
    sums = exchange(*_wgrad_in(xnt_bf, pieces), gsq, gsq_b)
    wt_bf = win_bf.transpose(0, 2, 1).reshape(D_IN, D)
    grad_x, gnw, win_got, sq_got = _bwd_dx(pieces, wt_bf, x, norm_w, dx2, sums[2], sums[3])
    sink_row = jnp.concatenate([dsink[:, 0].reshape(1, 16), jnp.zeros((1, D - 16), F32)], axis=1)
    loss_row = jnp.broadcast_to(loss8[0:1, 0:1], (1, D))
    vec = jnp.concatenate([vec_mid[0:2], dlb, gnw, sink_row, loss_row, jnp.zeros((2, D), F32)], axis=0)
    return grad_x, sums, (win_got, sq_got), vec


def kernel(x, norm_w, w_in, hgrn_lower_bound, hgrn_norm_w, w_branch_hgrn, attn_sinks, w_branch_attn, w_out, final_norm_w, loss_target, m_norm_w, m_w_in, m_hgrn_lower_bound, m_hgrn_norm_w, m_w_branch_hgrn, m_attn_sinks, m_w_branch_attn, m_w_out, m_final_norm_w, v_norm_w, v_w_in, v_hgrn_lower_bound, v_hgrn_norm_w, v_w_branch_hgrn, v_attn_sinks, v_w_branch_attn, v_w_out, v_final_norm_w):
    c_arr = lax.axis_index("c").astype(jnp.int32).reshape(1)
    j_arr = (2 * lax.axis_index("x") + lax.axis_index("y")).astype(jnp.int32).reshape(1)
    jc_arr = jnp.concatenate([j_arr, c_arr])

    win_mine, wsq_mine = _cast_shards(j_arr, w_in[0], w_branch_hgrn[0], w_branch_attn[0], w_out[0])
    xi, yi = lax.axis_index("x"), lax.axis_index("y")
    order_arr = jnp.stack([2 * xi + yi] + [2 * _flip(xi, fx) + _flip(yi, fy) for fx, fy in CHIP_FLIPS]).astype(jnp.int32)

    def chip_sums(gwin, gwin_b, gsq, gsq_b):
        return _add_halves(c_arr, gwin, gsq, *_swap_halves(gwin_b, gsq_b))

    grad_x, (swin, ssq, _, _), arrived, vec = _local_step(
        order_arr, x[0], loss_target[0], norm_w, hgrn_lower_bound, hgrn_norm_w, attn_sinks, final_norm_w, win_mine, wsq_mine,
        chip_sums)
    g_win, g_sq = _join_halves(*_sum_chips(jc_arr, swin, ssq, *arrived))

    d_win, nm_win, nv_win = _adamw("adamw_w_in", w_in[0], g_win, m_w_in[0], v_w_in[0], 128)
    g_sqs, d_sqs, nm_sqs, nv_sqs = _adamw_square(
        g_sq, (w_branch_hgrn, w_branch_attn, w_out), (m_w_branch_hgrn, m_w_branch_attn, m_w_out),
        (v_w_branch_hgrn, v_w_branch_attn, v_w_out))

    total = _allreduce_small(vec)
    loss = total[5, 0]
    g8, d8, nm8, nv8 = _small_update(
        total, hgrn_lower_bound,
        _pack8(norm_w, hgrn_lower_bound, hgrn_norm_w, final_norm_w, attn_sinks),
        _pack8(m_norm_w, m_hgrn_lower_bound, m_hgrn_norm_w, m_final_norm_w, m_attn_sinks),
        _pack8(v_norm_w, v_hgrn_lower_bound, v_hgrn_norm_w, v_final_norm_w, v_attn_sinks))

    def assemble(win, sq, small):
        nw, lb, hn, sk, fn = _unpack8(small)
        return (nw, win.reshape(1, D, SHARD_W), lb, hn, sq[0], sk, sq[1], sq[2], fn)

    return (loss, grad_x.reshape(1, -1, D),
            *assemble(g_win, g_sqs, g8), *assemble(d_win, d_sqs, d8),
            *assemble(nm_win, nm_sqs, nm8), *assemble(nv_win, nv_sqs, nv8))
```

```python
import functools

import jax
import jax.numpy as jnp
from jax import lax
from jax.experimental import pallas as pl
from jax.experimental.pallas import tpu as pltpu

F32 = jnp.float32
BF16 = jnp.bfloat16

D = 1024
D_IN = 8704
SHARDS = 4
SHARD_W = D_IN // SHARDS
SQ_ROWS = D // SHARDS
HEADS = 8
HEAD_W = 128
CHUNK = 64
SUB = 4
ATT_BLOCK = 128
KV_HEADS = 4
HEAD_DIM = 64
EPS = 1e-6
NEG = -1e30
SCALE = HEAD_DIM ** -0.5
COL_HG, COL_AQ, COL_AK, COL_AV, COL_AG, COL_MH, COL_MA = 3072, 4096, 5120, 5376, 5632, 6656, 7680

ADAM_LR, ADAM_B1, ADAM_B2, ADAM_EPS, ADAM_WD, ADAM_STEP = 0.001, 0.9, 0.999, 1e-08, 0.01, 10

VMEM_LIMIT = 56 * 1024 * 1024
MESH = pl.DeviceIdType.MESH
HBM_SPEC = pl.BlockSpec(memory_space=pltpu.HBM)
CHIP_FLIPS = ((1, 0), (0, 1), (1, 1))


def _dot(a, b):
    return jnp.dot(a, b, preferred_element_type=F32)


def _dot_nt(a, b):
    return lax.dot_general(a, b, (((1,), (1,)), ((), ())), preferred_element_type=F32)


def _dot_tn(a, b):
    return lax.dot_general(a, b, (((0,), (0,)), ((), ())), preferred_element_type=F32)


def _sigmoid(v):
    return 1.0 / (1.0 + jnp.exp(-v))


def _bf(v):
    return v.astype(BF16)


def _tri_dot2(tri, v):
    a = _bf(v)
    return _dot(tri, a) + _dot(tri, _bf(v - a.astype(F32)))


def _params(sem=None):
    return pltpu.CompilerParams(dimension_semantics=sem, vmem_limit_bytes=VMEM_LIMIT)


def _cast_shards(j_arr, win_s, wbh_s, wba_s, wout_s):
    steps = 4
    rows = D // steps

    def body(j_ref, win_ref, a_ref, b_ref, c_ref, win_o, sq_o):
        win_o[...] = _bf(win_ref[...])

        @pl.when(pl.program_id(0) == 0)
        def _():
            sq_o[0:SQ_ROWS, :] = _bf(a_ref[...])
            sq_o[SQ_ROWS:2 * SQ_ROWS, :] = _bf(b_ref[...])
            sq_o[2 * SQ_ROWS:3 * SQ_ROWS, :] = _bf(c_ref[...])

    whole = pl.BlockSpec((SQ_ROWS, D), lambda i, j: (0, 0))
    return pl.pallas_call(
        body, name="cast_shards",
        grid_spec=pltpu.PrefetchScalarGridSpec(
            num_scalar_prefetch=1, grid=(steps,),
            in_specs=[pl.BlockSpec((rows, SHARD_W), lambda i, j: (i, 0)), whole, whole, whole],
            out_specs=[pl.BlockSpec((None, rows, SHARD_W), lambda i, j: (j[0], i, 0)),
                       pl.BlockSpec((None, 3 * SQ_ROWS, D), lambda i, j: (j[0], 0, 0))]),
        out_shape=[jax.ShapeDtypeStruct((SHARDS, D, SHARD_W), BF16), jax.ShapeDtypeStruct((SHARDS, 3 * SQ_ROWS, D), BF16)],
        compiler_params=_params(("arbitrary",)),
    )(j_arr, win_s, wbh_s, wba_s, wout_s)


def _fwd_proj(order_arr, x, norm_w, win_all):
    T = x.shape[0]
    tm = min(512, T)
    nt = T // tm

    def body(order_ref, x_ref, nw_ref, win_in, proj_ref, xn_ref, win_out, w_scr, xn_scr, sems, send_sems, recv_sems):
        del win_in
        p, i = pl.program_id(0), pl.program_id(1)

        def load(n):
            return pltpu.make_async_copy(win_out.at[order_ref[n]], w_scr.at[n % 2], sems.at[n % 2])

        @pl.when((p == 0) & (i == 0))
        def _():
            _gather_start(win_out, _win_half, send_sems, recv_sems)
            load(0).start()
            load(0).wait()

        @pl.when((p == 1) & (i == 0))
        def _():
            _gather_land(win_out, _win_half, 0, send_sems, recv_sems)
            load(1).start()
            load(1).wait()

        for k in range(1, SHARDS - 1):
            @pl.when((p == k) & (i == nt // 2))
            def _():
                _gather_land(win_out, _win_half, k, send_sems, recv_sems)
                load(k + 1).start()

            @pl.when((p == k + 1) & (i == 0))
            def _():
                load(k + 1).wait()

        @pl.when(p == 0)
        def _():
            xf = x_ref[...]
            rs = lax.rsqrt(jnp.mean(xf * xf, axis=1, keepdims=True) + EPS)
            xn = _bf((xf * rs) * nw_ref[...])
            xn_scr[i] = xn
            xn_ref[...] = xn.T

        proj_ref[...] = _dot(xn_scr[i], w_scr[p % 2])

        @pl.when((p == SHARDS - 1) & (i == nt - 1))
        def _():
            _gather_drain(win_out, _win_half, send_sems, recv_sems)

    first = lambda p, i: jnp.where(p == 0, i, nt - 1)
    return pl.pallas_call(
        body, name="fwd_proj",
        grid_spec=pltpu.PrefetchScalarGridSpec(
            num_scalar_prefetch=1, grid=(SHARDS, nt),
            in_specs=[pl.BlockSpec((tm, D), lambda p, i, order: (first(p, i), 0)),
                      pl.BlockSpec((1, D), lambda p, i, order: (0, 0)), HBM_SPEC],
            out_specs=[pl.BlockSpec((tm, SHARD_W), lambda p, i, order: (i, order[p])),
                       pl.BlockSpec((D, tm), lambda p, i, order: (0, first(p, i))),
                       HBM_SPEC],
            scratch_shapes=[pltpu.VMEM((2, D, SHARD_W), BF16), pltpu.VMEM((nt, tm, D), BF16), pltpu.SemaphoreType.DMA((2,)),
                            pltpu.SemaphoreType.DMA((6,)), pltpu.SemaphoreType.DMA((6,))]),
        out_shape=[jax.ShapeDtypeStruct((T, D_IN), F32), jax.ShapeDtypeStruct((D, T), BF16),
                   jax.ShapeDtypeStruct((SHARDS, D, SHARD_W), BF16)],
        input_output_aliases={3: 2},
        compiler_params=_params(("arbitrary", "arbitrary")),
    )(order_arr, x, norm_w, win_all)


def _hgrn_gates(hq_ref, hf_ref, lbw_ref, b_scr):
    lb = 1.0 / (1.0 + jnp.exp(lbw_ref[1:2, :] - lbw_ref[0:1, :]))
    hf = hf_ref[...]
    sig = _sigmoid(hf)
    f = lb + (1.0 - lb) * sig
    g = jnp.log(f)
    hq = hq_ref[...]
    sq = _sigmoid(hq)
    q = hq * sq
    row = lax.broadcasted_iota(jnp.int32, (CHUNK, CHUNK), 0)
    col = lax.broadcasted_iota(jnp.int32, (CHUNK, CHUNK), 1)
    causal = row >= col
    b = _tri_dot2(jnp.where(causal, 1.0, 0.0).astype(BF16), g)
    b_scr[...] = b
    bc = b_scr[CHUNK - 1:CHUNK, :]
    r = b_scr[CHUNK // 2 - 1:CHUNK // 2, :]
    return dict(lb=lb, sig=sig, f=f, k=1.0 - f, hq=hq, sq=sq, q=q, b=b, bc=bc, r=r, causal=causal)


def _hgrn_fwd(proj, lbw, wsq_all):
    T = proj.shape[0]
    n = T // CHUNK

    def body(hq_ref, hf_ref, hi_ref, lbw_ref, wsq_in, o_ref, st_ref, wsq_out, s_scr, b_scr, send_sems, recv_sems):
        del wsq_in

        @pl.when(pl.program_id(0) == 0)
        def _():
            _gather_start(wsq_out, _sq_half, send_sems, recv_sems)
            s_scr[...] = jnp.zeros_like(s_scr)

        for c in range(SUB):
            rows = pl.ds(c * CHUNK, CHUNK)
            gt = _hgrn_gates(hq_ref.at[rows, :], hf_ref.at[rows, :], lbw_ref, b_scr.at[rows, :])
            b, bc, r, q, k = gt["b"], gt["bc"], gt["r"], gt["q"], gt["k"]
            qe = _bf(q * jnp.exp(b))
            qr = _bf(q * jnp.exp(b - r))
            kr = _bf(k * jnp.exp(r - b))
            kl = _bf(k * jnp.exp(bc - b))
            ebc = jnp.exp(bc)
            v = _bf(hi_ref[rows, :])
            scores = [_bf(jnp.where(gt["causal"], _dot_nt(qr[:, h * HEAD_W:(h + 1) * HEAD_W], kr[:, h * HEAD_W:(h + 1) * HEAD_W]), 0.0))
                      for h in range(HEADS)]
            for h in range(HEADS):
                sl = slice(h * HEAD_W, (h + 1) * HEAD_W)
                st = s_scr[h]
                st_ref[c, h] = st
                o_ref[rows, sl] = _dot(scores[h], v[:, sl]) + _dot_nt(qe[:, sl], _bf(st))
                s_scr[h] = ebc[:, sl] * st + _dot_tn(v[:, sl], kl[:, sl])

        @pl.when(pl.program_id(0) == n // SUB - 1)
        def _():
            _gather_finish(wsq_out, _sq_half, send_sems, recv_sems)

    col = lambda j: pl.BlockSpec((SUB * CHUNK, D), lambda i: (i, j))
    return pl.pallas_call(
        body, name="hgrn_fwd", grid=(n // SUB,),
        in_specs=[col(0), col(1), col(2), pl.BlockSpec((2, D), lambda i: (0, 0)), HBM_SPEC],
        out_specs=[pl.BlockSpec((SUB * CHUNK, D), lambda i: (i, 0)),
                   pl.BlockSpec((SUB, HEADS, HEAD_W, HEAD_W), lambda i: (i, 0, 0, 0)), HBM_SPEC],
        out_shape=[jax.ShapeDtypeStruct((T, D), F32), jax.ShapeDtypeStruct((n, HEADS, HEAD_W, HEAD_W), F32),
                   jax.ShapeDtypeStruct((SHARDS, 3 * SQ_ROWS, D), BF16)],
        input_output_aliases={4: 2},
        scratch_shapes=[pltpu.VMEM((HEADS, HEAD_W, HEAD_W), F32), pltpu.VMEM((SUB * CHUNK, D), F32),
                        pltpu.SemaphoreType.DMA((6,)), pltpu.SemaphoreType.DMA((6,))],
        compiler_params=_params(("arbitrary",)),
    )(proj, proj, proj, lbw, wsq_all)


def _hgrn_bwd(proj, lbw, states, do):
    T = proj.shape[0]
    n = T // CHUNK

    def body(hq_ref, hf_ref, hi_ref, lbw_ref, st_ref, do_ref, dp_ref, dlb_ref,
             ds_scr, b_scr, dq_scr, dk_scr, dv_scr, late_scr, early_scr, ex_scr):
        @pl.when(pl.program_id(0) == 0)
        def _():
            ds_scr[...] = jnp.zeros_like(ds_scr)
            dlb_ref[...] = jnp.zeros_like(dlb_ref)

        for c in reversed(range(SUB)):
            rows = pl.ds(c * CHUNK, CHUNK)
            gt = _hgrn_gates(hq_ref.at[rows, :], hf_ref.at[rows, :], lbw_ref, b_scr.at[rows, :])
            b, bc, r, q, k = gt["b"], gt["bc"], gt["r"], gt["q"], gt["k"]
            eb = jnp.exp(b)
            er = jnp.exp(b - r)
            erk = jnp.exp(r - b)
            el = jnp.exp(bc - b)
            ebc = jnp.exp(bc)
            qe, qr, kr, kl = _bf(q * eb), _bf(q * er), _bf(k * erk), _bf(k * el)
            v = _bf(hi_ref[rows, :])
            do_b = do_ref[rows, :]
            do_t = do_b.T
            causal_t = lax.broadcasted_iota(jnp.int32, (CHUNK, CHUNK), 0) <= lax.broadcasted_iota(jnp.int32, (CHUNK, CHUNK), 1)
            firsts = []
            for h in range(HEADS):
                sl = slice(h * HEAD_W, (h + 1) * HEAD_W)
                firsts.append((_bf(jnp.where(causal_t, _dot_nt(kr[:, sl], qr[:, sl]), 0.0)),
                               _bf(jnp.where(gt["causal"], _dot_nt(do_b[:, sl], v[:, sl]), 0.0)),
                               _bf(jnp.where(causal_t, _dot_nt(v[:, sl], do_b[:, sl]), 0.0))))
            for h in range(HEADS):
                sl = slice(h * HEAD_W, (h + 1) * HEAD_W)
                st0 = st_ref[c, h]
                dst = ds_scr[h]
                dst_b = _bf(dst)
                a_t, da, da_t = firsts[h]
                mq = _dot(da, kr[:, sl])
                mk = _dot(da_t, qr[:, sl])
                dq_in = eb[:, sl] * _dot(do_b[:, sl], _bf(st0))
                dk_in = el[:, sl] * _dot(v[:, sl], dst_b)
                dq_scr[rows, sl] = er[:, sl] * mq + dq_in
                dk_scr[rows, sl] = erk[:, sl] * mk + dk_in
                dv_scr[rows, sl] = _dot(a_t, do_b[:, sl]) + _dot_nt(kl[:, sl], dst_b)
                late_scr[rows, sl] = q[:, sl] * dq_in + qr[:, sl].astype(F32) * mq - kr[:, sl].astype(F32) * mk
                early_scr[rows, sl] = k[:, sl] * dk_in
                ex_scr[:, sl] = jnp.sum(dst * st0, axis=0, keepdims=True)
                ds_scr[h] = ebc[:, sl] * dst + _dot(do_t[sl, :], qe[:, sl])

            dq, dk = dq_scr[rows, :], dk_scr[rows, :]
            row = lax.broadcasted_iota(jnp.int32, (CHUNK, CHUNK), 0)
            col = lax.broadcasted_iota(jnp.int32, (CHUNK, CHUNK), 1)
            at_or_after = jnp.where(col >= row, 1.0, 0.0).astype(BF16)
            before = jnp.where(col < row, 1.0, 0.0).astype(BF16)
            dg = _tri_dot2(jnp.concatenate([at_or_after, before], axis=1),
                           jnp.concatenate([late_scr[rows, :], early_scr[rows, :]], axis=0)) + ebc * ex_scr[...]
            df = dg / gt["f"] - dk
            sig, sq, hq, lb = gt["sig"], gt["sq"], gt["hq"], gt["lb"]
            dp_ref[rows, 0:D] = _bf(dq * (sq * (1.0 + hq * (1.0 - sq))))
            dp_ref[rows, D:2 * D] = _bf(df * (1.0 - lb) * sig * (1.0 - sig))
            dp_ref[rows, 2 * D:3 * D] = _bf(dv_scr[rows, :])
            dlb_ref[...] += jnp.sum(df * (1.0 - sig), axis=0, keepdims=True)

    ns = n // SUB
    col = lambda j: pl.BlockSpec((SUB * CHUNK, D), lambda i: (ns - 1 - i, j))
    return pl.pallas_call(
        body, name="hgrn_bwd", grid=(ns,),
        in_specs=[col(0), col(1), col(2), pl.BlockSpec((2, D), lambda i: (0, 0)),
                  pl.BlockSpec((SUB, HEADS, HEAD_W, HEAD_W), lambda i: (ns - 1 - i, 0, 0, 0)),
                  pl.BlockSpec((SUB * CHUNK, D), lambda i: (ns - 1 - i, 0))],
        out_specs=[pl.BlockSpec((SUB * CHUNK, 3 * D), lambda i: (ns - 1 - i, 0)),
                   pl.BlockSpec((1, D), lambda i: (0, 0))],
        out_shape=[jax.ShapeDtypeStruct((T, 3 * D), BF16), jax.ShapeDtypeStruct((1, D), F32)],
        scratch_shapes=[pltpu.VMEM((HEADS, HEAD_W, HEAD_W), F32)] + [pltpu.VMEM((SUB * CHUNK, D), F32)] * 6
                       + [pltpu.VMEM((1, D), F32)],
        compiler_params=_params(("arbitrary",)),
    )(proj, proj, proj, lbw, states, do)


def _attn_masks(blk):
    qi = lax.broadcasted_iota(jnp.int32, (ATT_BLOCK, 2 * ATT_BLOCK), 0)
    kj = lax.broadcasted_iota(jnp.int32, (ATT_BLOCK, 2 * ATT_BLOCK), 1)
    band = (kj > qi) & (kj <= qi + ATT_BLOCK)
    return band & ((blk > 0) | (kj >= ATT_BLOCK))


def _head_pair_operand(t, hp, low):
    mine = low if hp == 0 else jnp.logical_not(low)
    both = jnp.where(mine, t, pltpu.roll(t, HEAD_DIM, 1))
    return _bf(jnp.concatenate([jnp.where(low, both, 0.0), jnp.where(low, 0.0, both)], axis=0))


def _attn_probs(s, sink, valid):
    s = jnp.where(valid, s * SCALE, NEG)
    m = jnp.maximum(jnp.max(s, axis=1, keepdims=True), sink)
    p = jnp.exp(s - m)
    es = jnp.exp(sink - m)
    inv = 1.0 / (jnp.sum(p, axis=1, keepdims=True) + es)
    return p * inv, es * inv


def _attn_fwd(proj, sinks):
    T = proj.shape[0]
    nb = T // ATT_BLOCK
    W2 = 2 * ATT_BLOCK

    def body(sink_ref, q_ref, kp_ref, kc_ref, vp_ref, vc_ref, o_ref):
        blk = pl.program_id(0)
        valid = _attn_masks(blk)
        low = lax.broadcasted_iota(jnp.int32, (1, 2 * HEAD_DIM), 1) < HEAD_DIM
        kcat = jnp.concatenate([kp_ref[...], kc_ref[...]], axis=0)
        vcat = jnp.concatenate([vp_ref[...], vc_ref[...]], axis=0)
        for h in range(KV_HEADS):
            tl = slice((h // 2) * 128, (h // 2) * 128 + 128)
            mine = low if h % 2 == 0 else jnp.logical_not(low)
            kh = _bf(jnp.where(mine, kcat[:, tl], pltpu.roll(kcat[:, tl], HEAD_DIM, 1)))
            vh = _bf(jnp.where(mine, vcat[:, tl], pltpu.roll(vcat[:, tl], HEAD_DIM, 1)))
            for t in range(2):
                ql = slice((2 * h + t) * 128, (2 * h + t) * 128 + 128)
                q2 = q_ref[:, ql]
                outs = []
                for p in range(2):
                    qm = _bf(jnp.where(low if p == 0 else jnp.logical_not(low), q2, 0.0))
                    probs, _ = _attn_probs(_dot_nt(qm, kh), sink_ref[0, 4 * h + 2 * t + p], valid)
                    outs.append(_dot(_bf(probs), vh))
                o_ref[:, ql] = jnp.where(low, outs[0], outs[1])

    prev = lambda i: jnp.maximum(i - 1, 0)
    return pl.pallas_call(
        body, name="attn_fwd", grid=(nb,),
        in_specs=[pl.BlockSpec(memory_space=pltpu.SMEM),
                  pl.BlockSpec((ATT_BLOCK, D), lambda i: (i, COL_AQ // D)),
                  pl.BlockSpec((ATT_BLOCK, 256), lambda i: (prev(i), COL_AK // 256)),
                  pl.BlockSpec((ATT_BLOCK, 256), lambda i: (i, COL_AK // 256)),
                  pl.BlockSpec((ATT_BLOCK, 256), lambda i: (prev(i), COL_AV // 256)),
                  pl.BlockSpec((ATT_BLOCK, 256), lambda i: (i, COL_AV // 256))],
        out_specs=pl.BlockSpec((ATT_BLOCK, D), lambda i: (i, 0)),
        out_shape=jax.ShapeDtypeStruct((T, D), F32),
        compiler_params=_params(("arbitrary",)),
    )(sinks, proj, proj, proj, proj, proj)


def _attn_bwd(proj, sinks, o, do):
    T = proj.shape[0]
    nb = T // ATT_BLOCK
    W2 = 2 * ATT_BLOCK

    def body(sink_ref, q_ref, kp_ref, kc_ref, vp_ref, vc_ref, o_ref, do_ref,
             dq_ref, dk_ref, dv_ref, dsink_ref, ck_scr, cv_scr, nk_scr, nv_scr):
        blk = pl.program_id(0)

        @pl.when(blk == 0)
        def _():
            ck_scr[...] = jnp.zeros_like(ck_scr)
            cv_scr[...] = jnp.zeros_like(cv_scr)
            dsink_ref[...] = jnp.zeros_like(dsink_ref)

        @pl.when(blk < nb)
        def _():
            valid = _attn_masks(blk)
            low = lax.broadcasted_iota(jnp.int32, (1, 2 * HEAD_DIM), 1) < HEAD_DIM
            kcat = jnp.concatenate([kp_ref[...], kc_ref[...]], axis=0)
            vcat = jnp.concatenate([vp_ref[...], vc_ref[...]], axis=0)
            for h in range(KV_HEADS):
                tl = slice((h // 2) * 128, (h // 2) * 128 + 128)
                kbd = _head_pair_operand(kcat[:, tl], h % 2, low)
                vbd = _head_pair_operand(vcat[:, tl], h % 2, low)
                dkbd = jnp.zeros((2 * W2, 128), F32)
                dvbd = jnp.zeros((2 * W2, 128), F32)
                tiles = []
                for t in range(2):
                    ql = slice((2 * h + t) * 128, (2 * h + t) * 128 + 128)
                    q2 = _bf(q_ref[:, ql])
                    do2_b = do_ref[:, ql]
                    doo = do2_b.astype(F32) * o_ref[:, ql]
                    dsum0 = jnp.sum(jnp.where(low, doo, 0.0), axis=1, keepdims=True)
                    dsum1 = jnp.sum(jnp.where(low, 0.0, doo), axis=1, keepdims=True)
                    tiles.append((ql, q2, do2_b, dsum0, dsum1, _dot_nt(q2, kbd), _dot_nt(do2_b, vbd)))
                grads = []
                for t, (ql, q2, do2_b, dsum0, dsum1, s2, dp2) in enumerate(tiles):
                    head = 4 * h + 2 * t
                    p0, ps0 = _attn_probs(s2[:, 0:W2], sink_ref[0, head], valid)
                    p1, ps1 = _attn_probs(s2[:, W2:2 * W2], sink_ref[0, head + 1], valid)
                    ds2 = _bf(jnp.concatenate([p0 * (dp2[:, 0:W2] - dsum0), p1 * (dp2[:, W2:2 * W2] - dsum1)], axis=1) * SCALE)
                    grads.append((ds2, _bf(jnp.concatenate([p0, p1], axis=1))))
                    dsink_ref[head:head + 1, :] += jnp.zeros((1, 128), F32) - jnp.sum(ps0 * dsum0, axis=0, keepdims=True)
                    dsink_ref[head + 1:head + 2, :] += jnp.zeros((1, 128), F32) - jnp.sum(ps1 * dsum1, axis=0, keepdims=True)
                for (ql, q2, do2_b, _, _, _, _), (ds2, p2) in zip(tiles, grads):
                    dq_ref[:, ql] = _bf(_dot(ds2, kbd))
                    dkbd = dkbd + _dot_tn(ds2, q2)
                    dvbd = dvbd + _dot_tn(p2, do2_b)
                dk2 = jnp.where(low, dkbd[0:W2], dkbd[W2:2 * W2])
                dv2 = jnp.where(low, dvbd[0:W2], dvbd[W2:2 * W2])
                dk2 = dk2 + pltpu.roll(dk2, HEAD_DIM, 1)
                dv2 = dv2 + pltpu.roll(dv2, HEAD_DIM, 1)
                if h % 2 == 0:
                    keep_k, keep_v = dk2, dv2
                else:
                    nk_scr[:, tl] = jnp.where(low, keep_k, dk2)
                    nv_scr[:, tl] = jnp.where(low, keep_v, dv2)
            dk_ref[...] = _bf(ck_scr[...] + nk_scr[0:ATT_BLOCK, :])
            dv_ref[...] = _bf(cv_scr[...] + nv_scr[0:ATT_BLOCK, :])
            ck_scr[...] = nk_scr[ATT_BLOCK:2 * ATT_BLOCK, :]
            cv_scr[...] = nv_scr[ATT_BLOCK:2 * ATT_BLOCK, :]

        @pl.when(blk == nb)
        def _():
            dk_ref[...] = _bf(ck_scr[...])
            dv_ref[...] = _bf(cv_scr[...])

    cur = lambda i: jnp.minimum(i, nb - 1)
    prev = lambda i: jnp.maximum(cur(i) - 1, 0)
    late = lambda i: jnp.maximum(i - 1, 0)
    dq, dk, dv, dsink = pl.pallas_call(
        body, name="attn_bwd", grid=(nb + 1,),
        in_specs=[pl.BlockSpec(memory_space=pltpu.SMEM),
                  pl.BlockSpec((ATT_BLOCK, D), lambda i: (cur(i), COL_AQ // D)),
                  pl.BlockSpec((ATT_BLOCK, 256), lambda i: (prev(i), COL_AK // 256)),
                  pl.BlockSpec((ATT_BLOCK, 256), lambda i: (cur(i), COL_AK // 256)),
                  pl.BlockSpec((ATT_BLOCK, 256), lambda i: (prev(i), COL_AV // 256)),
                  pl.BlockSpec((ATT_BLOCK, 256), lambda i: (cur(i), COL_AV // 256)),
                  pl.BlockSpec((ATT_BLOCK, D), lambda i: (cur(i), 0)),
                  pl.BlockSpec((ATT_BLOCK, D), lambda i: (cur(i), 0))],
        out_specs=[pl.BlockSpec((ATT_BLOCK, D), lambda i: (cur(i), 0)),
                   pl.BlockSpec((ATT_BLOCK, 256), lambda i: (late(i), 0)),
                   pl.BlockSpec((ATT_BLOCK, 256), lambda i: (late(i), 0)),
                   pl.BlockSpec((16, 128), lambda i: (0, 0))],
        out_shape=[jax.ShapeDtypeStruct((T, D), BF16), jax.ShapeDtypeStruct((T, 256), BF16),
                   jax.ShapeDtypeStruct((T, 256), BF16), jax.ShapeDtypeStruct((16, 128), F32)],
        scratch_shapes=[pltpu.VMEM((ATT_BLOCK, 256), F32), pltpu.VMEM((ATT_BLOCK, 256), F32),
                        pltpu.VMEM((2 * ATT_BLOCK, 256), F32), pltpu.VMEM((2 * ATT_BLOCK, 256), F32)],
        compiler_params=_params(("arbitrary",)),
    )(sinks, proj, proj, proj, proj, proj, o, do)
    return dq, dk, dv, dsink


def _mid(x, tgt, proj, oh, oa, hnw, fnw, wsq_bf):
    T = x.shape[0]
    tm = min(256, T)
    nt = T // tm

    def body(x_ref, tgt_ref, oh_ref, oa_ref, hg_ref, ag0_ref, ag1_ref, mh0_ref, mh1_ref, ma0_ref, ma1_ref,
             hnw_ref, fnw_ref, w_hbm,
             dx2_ref, doh_ref, doa_ref, dhg_ref, dtail_ref, lhs_ref, rhs_ref, loss_ref, vec_ref,
             w_scr, xh_scr, rs_scr, sem):
        @pl.when(pl.program_id(0) == 0)
        def _():
            cp = pltpu.make_async_copy(w_hbm, w_scr, sem)
            cp.start()
            cp.wait()
            loss_ref[...] = jnp.zeros_like(loss_ref)
            vec_ref[...] = jnp.zeros_like(vec_ref)

        oh = oh_ref[...]
        for h in range(HEADS):
            sl = slice(h * HEAD_W, (h + 1) * HEAD_W)
            ohh = oh[:, sl]
            rs = lax.rsqrt(jnp.mean(ohh * ohh, axis=1, keepdims=True) + EPS)
            xh_scr[:, sl] = ohh * rs
            rs_scr[:, sl] = jnp.broadcast_to(rs, (tm, HEAD_W))
        xh = xh_scr[...]
        hnw = hnw_ref[...]
        on = xh * hnw
        hg = hg_ref[...]
        sg = _sigmoid(hg)
        silu_g = hg * sg
        gated_h = _bf(on * silu_g)
        oa = oa_ref[...]
        ag = jnp.concatenate([ag0_ref[...], ag1_ref[...]], axis=1)
        sa = _sigmoid(ag)
        silu_a = ag * sa
        gated_a = _bf(oa * silu_a)
        yh = _dot(gated_h, w_scr[0])
        ya = _dot(gated_a, w_scr[1])
        lhs_ref[0] = gated_h.T
        lhs_ref[1] = gated_a.T
        smh = _sigmoid(jnp.concatenate([mh0_ref[...], mh1_ref[...]], axis=1))
        sma = _sigmoid(jnp.concatenate([ma0_ref[...], ma1_ref[...]], axis=1))
        merged = _bf(smh * yh + sma * ya)
        lhs_ref[2] = merged.T
        x2 = x_ref[...] + _dot(merged, w_scr[2])
        rs2 = lax.rsqrt(jnp.mean(x2 * x2, axis=1, keepdims=True) + EPS)
        xh2 = x2 * rs2
        fnw = fnw_ref[...]
        diff = xh2 * fnw - tgt_ref[...]
        loss_ref[...] += jnp.zeros_like(loss_ref) + jnp.sum(diff * diff) * (0.5 / D)

        dy = diff * (1.0 / D)
        vec_ref[0:1, :] += jnp.sum(dy * xh2, axis=0, keepdims=True)
        gy = dy * fnw
        dx2 = rs2 * (gy - xh2 * jnp.mean(gy * xh2, axis=1, keepdims=True))
        dx2_ref[...] = dx2
        dx2_b = _bf(dx2)
        rhs_ref[2] = dx2_b
        dmerged = _dot_nt(dx2_b, w_scr[2])
        dyh = dmerged * smh
        dya = dmerged * sma
        dtail_ref[:, D:2 * D] = _bf(dyh * yh * (1.0 - smh))
        dtail_ref[:, 2 * D:3 * D] = _bf(dya * ya * (1.0 - sma))
        dyh_b, dya_b = _bf(dyh), _bf(dya)
        rhs_ref[0] = dyh_b
        rhs_ref[1] = dya_b
        dgh = _dot_nt(dyh_b, w_scr[0])
        dga = _dot_nt(dya_b, w_scr[1])
        don = dgh * silu_g
        dhg_ref[...] = _bf(dgh * on * (sg * (1.0 + hg * (1.0 - sg))))
        vec_ref[1:2, :] += jnp.sum(don * xh, axis=0, keepdims=True)
        gxh = don * hnw
        rsb = rs_scr[...]
        for h in range(HEADS):
            sl = slice(h * HEAD_W, (h + 1) * HEAD_W)
            gh, xhh = gxh[:, sl], xh[:, sl]
            doh_ref[:, sl] = _bf(rsb[:, sl] * (gh - xhh * jnp.mean(gh * xhh, axis=1, keepdims=True)))
        doa_ref[...] = _bf(dga * silu_a)
        dtail_ref[:, 0:D] = _bf(dga * oa * (sa * (1.0 + ag * (1.0 - sa))))

    row = lambda w, j: pl.BlockSpec((tm, w), lambda i: (i, j))
    const = lambda r, c: pl.BlockSpec((r, c), lambda i: (0, 0))
    stack = pl.BlockSpec((3, tm, D), lambda i: (0, i, 0))
    stack_t = pl.BlockSpec((3, D, tm), lambda i: (0, 0, i))
    return pl.pallas_call(
        body, name="mid", grid=(nt,),
        in_specs=[row(D, 0), row(D, 0), row(D, 0), row(D, 0), row(D, COL_HG // D),
                  row(512, COL_AG // 512), row(512, COL_AG // 512 + 1),
                  row(512, COL_MH // 512), row(512, COL_MH // 512 + 1),
                  row(512, COL_MA // 512), row(512, COL_MA // 512 + 1),
                  const(1, D), const(1, D), HBM_SPEC],
        out_specs=[row(D, 0), row(D, 0), row(D, 0), row(D, 0), row(3 * D, 0), stack_t, stack, const(8, 128), const(8, D)],
        out_shape=[jax.ShapeDtypeStruct((T, D), F32), jax.ShapeDtypeStruct((T, D), BF16), jax.ShapeDtypeStruct((T, D), BF16),
                   jax.ShapeDtypeStruct((T, D), BF16), jax.ShapeDtypeStruct((T, 3 * D), BF16),
                   jax.ShapeDtypeStruct((3, D, T), BF16), jax.ShapeDtypeStruct((3, T, D), BF16),
                   jax.ShapeDtypeStruct((8, 128), F32), jax.ShapeDtypeStruct((8, D), F32)],
        scratch_shapes=[pltpu.VMEM((3, D, D), BF16), pltpu.VMEM((tm, D), F32), pltpu.VMEM((tm, D), F32),
                        pltpu.SemaphoreType.DMA],
        compiler_params=_params(("arbitrary",)),
    )(x, tgt, oh, oa, proj, proj, proj, proj, proj, proj, proj, hnw, fnw, wsq_bf)


def _wgrad_square(lhs_t, rhs):
    T = rhs.shape[1]
    tk = min(1024, T)
    steps = T // tk

    def body(a_ref, b_ref, g_ref, gb_ref):
        part = _dot(a_ref[...], b_ref[...])

        @pl.when(pl.program_id(1) == 0)
        def _():
            g_ref[...] = part

        @pl.when(pl.program_id(1) > 0)
        def _():
            g_ref[...] += part

        @pl.when(pl.program_id(1) == steps - 1)
        def _():
            gb_ref[...] = _bf(g_ref[...])

    return pl.pallas_call(
        body, name="wgrad_square", grid=(3, steps),
        in_specs=[pl.BlockSpec((None, D, tk), lambda k, i: (k, 0, i)), pl.BlockSpec((None, tk, D), lambda k, i: (k, i, 0))],
        out_specs=[pl.BlockSpec((None, D, D), lambda k, i: (k, 0, 0))] * 2,
        out_shape=[jax.ShapeDtypeStruct((3, D, D), F32), jax.ShapeDtypeStruct((3, D, D), BF16)],
        compiler_params=_params(("parallel", "arbitrary")),
    )(lhs_t, rhs)


def _bwd_dx(pieces, wt_bf, x, norm_w, dx2, swin_b, ssq_b):
    T = x.shape[0]
    tm = min(256, T)
    nt = T // tm
    widths = [p.shape[1] for p in pieces]
    n_p = len(pieces)

    def body(*refs):
        piece_refs = refs[:n_p]
        (w_hbm, x_ref, nw_ref, dx2_ref, swin_ref, ssq_ref,
         gx_ref, gnw_ref, win_got, sq_got, w_scr, sem, send_sems, recv_sems) = refs[n_p:]

        def scatter_copies():
            x_, y_, c_ = _place()
            copies = []
            for k, (fx, fy) in enumerate(CHIP_FLIPS):
                px, py = _flip(x_, fx), _flip(y_, fy)
                jr = 2 * px + py
                for a, (src, dst) in enumerate(((swin_ref.at[:, pl.ds(jr * SHARD_W, SHARD_W)], win_got.at[k]),
                                                (ssq_ref.at[:, pl.ds(jr * SQ_ROWS, SQ_ROWS), :], sq_got.at[k]))):
                    copies.append(pltpu.make_async_remote_copy(
                        src_ref=src, dst_ref=dst, send_sem=send_sems.at[2 * k + a], recv_sem=recv_sems.at[2 * k + a],
                        device_id=(px, py, c_), device_id_type=MESH))
            return copies

        @pl.when(pl.program_id(0) == 0)
        def _():
            for cp in scatter_copies():
                cp.start()
            cp = pltpu.make_async_copy(w_hbm, w_scr, sem)
            cp.start()
            cp.wait()
            gnw_ref[...] = jnp.zeros_like(gnw_ref)

        dxn = None
        off = 0
        for ref, w in zip(piece_refs, widths):
            part = _dot(ref[...], w_scr[off:off + w, :])
            dxn = part if dxn is None else dxn + part
            off += w
        xf = x_ref[...]
        rs = lax.rsqrt(jnp.mean(xf * xf, axis=1, keepdims=True) + EPS)
        xh = xf * rs
        gnw_ref[...] += jnp.sum(dxn * xh, axis=0, keepdims=True)
        gx = dxn * nw_ref[...]
        gx_ref[...] = rs * (gx - xh * jnp.mean(gx * xh, axis=1, keepdims=True)) + dx2_ref[...]

        @pl.when(pl.program_id(0) == nt - 1)
        def _():
            for cp in scatter_copies():
                cp.wait()

    row = lambda w: pl.BlockSpec((tm, w), lambda i: (i, 0))
    return pl.pallas_call(
        body, name="bwd_dx", grid=(nt,),
        in_specs=[row(w) for w in widths] + [HBM_SPEC, row(D), pl.BlockSpec((1, D), lambda i: (0, 0)), row(D), HBM_SPEC, HBM_SPEC],
        out_specs=[row(D), pl.BlockSpec((1, D), lambda i: (0, 0)), HBM_SPEC, HBM_SPEC],
        out_shape=[jax.ShapeDtypeStruct((T, D), F32), jax.ShapeDtypeStruct((1, D), F32),
                   jax.ShapeDtypeStruct((3, D // 2, SHARD_W), BF16), jax.ShapeDtypeStruct((3, 3, SQ_ROWS, D // 2), BF16)],
        scratch_shapes=[pltpu.VMEM((D_IN, D), BF16), pltpu.SemaphoreType.DMA,
                        pltpu.SemaphoreType.DMA((6,)), pltpu.SemaphoreType.DMA((6,))],
        compiler_params=_params(("arbitrary",)),
    )(*pieces, wt_bf, x, norm_w, dx2, swin_b, ssq_b)


W_PIECES = ((0, 1024, 3), (COL_HG, 1024, 1), (COL_AQ, 1024, 1), (COL_AK, 256, 1), (COL_AV, 256, 1), (COL_AG, 512, 6))


def _wgrad_in(xnt_bf, pieces):
    T = xnt_bf.shape[1]
    bufs = ()
    for n, (piece, (col, wb, blocks)) in enumerate(zip(pieces, W_PIECES)):
        tk = min(1024 if wb == 1024 else 2048, T)
        steps = T // tk

        def body(xnt_ref, p_ref, *rest):
            g_ref, gb_ref = rest[-2:]
            part = _dot(xnt_ref[...], p_ref[...])

            @pl.when(pl.program_id(1) == 0)
            def _():
                g_ref[...] = part

            @pl.when(pl.program_id(1) > 0)
            def _():
                g_ref[...] += part

            @pl.when(pl.program_id(1) == steps - 1)
            def _():
                gb_ref[...] = _bf(g_ref[...])

        out = pl.BlockSpec((D, wb), lambda jb, i, base=col // wb: (0, base + jb))
        bufs = pl.pallas_call(
            body, name=f"wgrad_in_{n}", grid=(blocks, steps),
            in_specs=[pl.BlockSpec((D, tk), lambda jb, i: (0, i)), pl.BlockSpec((tk, wb), lambda jb, i: (i, jb))]
                     + [HBM_SPEC] * len(bufs),
            out_specs=[out, out],
            out_shape=[jax.ShapeDtypeStruct((D, D_IN), F32), jax.ShapeDtypeStruct((D, D_IN), BF16)],
            input_output_aliases={2: 0, 3: 1} if bufs else {},
            compiler_params=_params(("parallel", "arbitrary")),
        )(xnt_bf, piece, *bufs)
    return bufs


def _place():
    return lax.axis_index("x"), lax.axis_index("y"), lax.axis_index("c")


def _flip(v, f):
    return 1 - v if f else v


def _win_half(ref, h):
    return ref.at[pl.ds(h * (D // 2), D // 2), :]


def _sq_half(ref, h):
    return ref.at[:, pl.ds(h * (D // 2), D // 2)]


def _gather_copy(part, k, to, send_sems, recv_sems):
    return pltpu.make_async_remote_copy(src_ref=part, dst_ref=part, send_sem=send_sems.at[k], recv_sem=recv_sems.at[k],
                                        device_id=to, device_id_type=MESH)


def _gather_start(out, half, send_sems, recv_sems):
    x, y, c = _place()
    for k, (fx, fy) in enumerate(CHIP_FLIPS):
        _gather_copy(half(out.at[2 * x + y], c), k, (_flip(x, fx), _flip(y, fy), c), send_sems, recv_sems).start()


def _gather_land(out, half, k, send_sems, recv_sems):
    x, y, c = _place()
    sib = (x, y, 1 - c)
    fx, fy = CHIP_FLIPS[k]
    slot = out.at[2 * _flip(x, fx) + _flip(y, fy)]
    _gather_copy(half(slot, c), k, sib, send_sems, recv_sems).wait_recv()
    _gather_copy(half(slot, c), 3 + k, sib, send_sems, recv_sems).start()
    _gather_copy(half(slot, 1 - c), 3 + k, sib, send_sems, recv_sems).wait_recv()


def _gather_drain(out, half, send_sems, recv_sems):
    x, y, c = _place()
    for k, (fx, fy) in enumerate(CHIP_FLIPS):
        _gather_copy(half(out.at[2 * x + y], c), k, (_flip(x, fx), _flip(y, fy), c), send_sems, recv_sems).wait_send()
        _gather_copy(half(out.at[2 * _flip(x, fx) + _flip(y, fy)], c), 3 + k, (x, y, 1 - c), send_sems, recv_sems).wait_send()


def _gather_finish(out, half, send_sems, recv_sems):
    for k in range(len(CHIP_FLIPS)):
        _gather_land(out, half, k, send_sems, recv_sems)
    _gather_drain(out, half, send_sems, recv_sems)


def _swap_halves(gwin, gsq):
    def body(gwin_ref, gsq_ref, win_got, sq_got, send_sems, recv_sems):
        x, y, c = _place()
        sib = (x, y, 1 - c)
        pairs = ((_win_half(gwin_ref, 1 - c), win_got),
                 (gsq_ref.at[:, :, pl.ds((1 - c) * (D // 2), D // 2)], sq_got))
        copies = [pltpu.make_async_remote_copy(src_ref=src, dst_ref=dst, send_sem=send_sems.at[a], recv_sem=recv_sems.at[a],
                                               device_id=sib, device_id_type=MESH) for a, (src, dst) in enumerate(pairs)]
        for cp in copies:
            cp.start()
        for cp in copies:
            cp.wait()

    return pl.pallas_call(
        body, name="swap_halves",
        in_specs=[HBM_SPEC, HBM_SPEC], out_specs=[HBM_SPEC, HBM_SPEC],
        out_shape=[jax.ShapeDtypeStruct((D // 2, D_IN), BF16), jax.ShapeDtypeStruct((3, D, D // 2), BF16)],
        scratch_shapes=[pltpu.SemaphoreType.DMA((2,)), pltpu.SemaphoreType.DMA((2,))],
    )(gwin, gsq)


def _add_halves(c_arr, gwin, gsq, win_got, sq_got):
    def body(c_ref, a_ref, b_ref, p_ref, q_ref, so_ref, sq_ref, sob_ref, sqb_ref):
        so = a_ref[...] + b_ref[...].astype(F32)
        sq = p_ref[...] + q_ref[...].astype(F32)
        so_ref[...] = so
        sq_ref[...] = sq
        sob_ref[...] = _bf(so)
        sqb_ref[...] = _bf(sq)

    steps = 8
    rows, sq_rows = (D // 2) // steps, D // steps
    win = lambda f: pl.BlockSpec((rows, D_IN), f)
    sq = lambda f: pl.BlockSpec((3, sq_rows, D // 2), f)
    return pl.pallas_call(
        body, name="add_halves",
        grid_spec=pltpu.PrefetchScalarGridSpec(
            num_scalar_prefetch=1, grid=(steps,),
            in_specs=[win(lambda i, c: (c[0] * steps + i, 0)), win(lambda i, c: (i, 0)),
                      sq(lambda i, c: (0, i, c[0])), sq(lambda i, c: (0, i, 0))],
            out_specs=[win(lambda i, c: (i, 0)), sq(lambda i, c: (0, i, 0))] * 2),
        out_shape=[jax.ShapeDtypeStruct((D // 2, D_IN), F32), jax.ShapeDtypeStruct((3, D, D // 2), F32),
                   jax.ShapeDtypeStruct((D // 2, D_IN), BF16), jax.ShapeDtypeStruct((3, D, D // 2), BF16)],
        compiler_params=_params(("arbitrary",)),
    )(c_arr, gwin, win_got, gsq, sq_got)


def _sum_chips(jc_arr, swin, ssq, win_got, sq_got):
    def body(jc_ref, a_ref, b_ref, p_ref, q_ref, so_ref, sq_ref):
        so_ref[...] = ((a_ref[...] + b_ref[0].astype(F32)) + b_ref[1].astype(F32)) + b_ref[2].astype(F32)
        sq_ref[...] = ((p_ref[...] + q_ref[0].astype(F32)) + q_ref[1].astype(F32)) + q_ref[2].astype(F32)

    rows = 128
    steps = (D // 2) // rows
    sq_rows = SQ_ROWS // steps
    return pl.pallas_call(
        body, name="sum_chips",
        grid_spec=pltpu.PrefetchScalarGridSpec(
            num_scalar_prefetch=1, grid=(steps,),
            in_specs=[pl.BlockSpec((rows, SHARD_W), lambda i, jc: (i, jc[0])),
                      pl.BlockSpec((3, rows, SHARD_W), lambda i, jc: (0, i, 0)),
                      pl.BlockSpec((3, sq_rows, D // 2), lambda i, jc: (0, jc[0] * steps + i, 0)),
                      pl.BlockSpec((3, 3, sq_rows, D // 2), lambda i, jc: (0, 0, i, 0))],
            out_specs=[pl.BlockSpec((rows, SHARD_W), lambda i, jc: (jc[1] * steps + i, 0)),
                       pl.BlockSpec((3, sq_rows, D // 2), lambda i, jc: (0, i, jc[1]))]),
        out_shape=[jax.ShapeDtypeStruct((D, SHARD_W), F32), jax.ShapeDtypeStruct((3, SQ_ROWS, D), F32)],
        compiler_params=_params(("arbitrary",)),
    )(jc_arr, swin, win_got, ssq, sq_got)


def _join_halves(g_win, g_sq):
    def body(win_in, sq_in, win_out, sq_out, send_sems, recv_sems):
        del win_in, sq_in
        x, y, c = _place()
        sib = (x, y, 1 - c)

        def halves(h):
            return _win_half(win_out, h), sq_out.at[:, :, pl.ds(h * (D // 2), D // 2)]

        def copy(a, part):
            return pltpu.make_async_remote_copy(src_ref=part, dst_ref=part, send_sem=send_sems.at[a], recv_sem=recv_sems.at[a],
                                                device_id=sib, device_id_type=MESH)

        sent = [copy(a, part) for a, part in enumerate(halves(c))]
        for cp in sent:
            cp.start()
        for a, part in enumerate(halves(1 - c)):
            copy(a, part).wait_recv()
        for cp in sent:
            cp.wait_send()

    return pl.pallas_call(
        body, name="join_halves",
        in_specs=[HBM_SPEC, HBM_SPEC], out_specs=[HBM_SPEC, HBM_SPEC], input_output_aliases={0: 0, 1: 1},
        out_shape=[jax.ShapeDtypeStruct((D, SHARD_W), F32), jax.ShapeDtypeStruct((3, SQ_ROWS, D), F32)],
        scratch_shapes=[pltpu.SemaphoreType.DMA((2,)), pltpu.SemaphoreType.DMA((2,))],
    )(g_win, g_sq)


def _allreduce_small(vec):
    def body(vec_ref, out_ref, slots, send_sems, recv_sems):
        x, y, c = _place()
        me = 4 * x + 2 * y + c
        slots[me] = vec_ref[...]
        copies = []
        for k in range(1, 8):
            fx, fy, fc = (k >> 2) & 1, (k >> 1) & 1, k & 1
            copies.append(pltpu.make_async_remote_copy(
                src_ref=vec_ref, dst_ref=slots.at[me], send_sem=send_sems.at[k - 1], recv_sem=recv_sems.at[k - 1],
                device_id=(_flip(x, fx), _flip(y, fy), _flip(c, fc)), device_id_type=MESH))
        for cp in copies:
            cp.start()
        for k in range(1, 8):
            fx, fy, fc = (k >> 2) & 1, (k >> 1) & 1, k & 1
            src = 4 * _flip(x, fx) + 2 * _flip(y, fy) + _flip(c, fc)
            pltpu.make_async_remote_copy(src_ref=vec_ref, dst_ref=slots.at[src], send_sem=send_sems.at[k - 1],
                                         recv_sem=recv_sems.at[k - 1], device_id=(x, y, c), device_id_type=MESH).wait_recv()
        for cp in copies:
            cp.wait_send()
        total = slots[0]
        for s in range(1, 8):
            total = total + slots[s]
        out_ref[...] = total

    return pl.pallas_call(
        body, name="allreduce_small",
        in_specs=[pl.BlockSpec(memory_space=pltpu.VMEM)], out_specs=pl.BlockSpec(memory_space=pltpu.VMEM),
        out_shape=jax.ShapeDtypeStruct((8, D), F32),
        scratch_shapes=[pltpu.VMEM((8, 8, D), F32), pltpu.SemaphoreType.DMA((7,)), pltpu.SemaphoreType.DMA((7,))],
    )(vec)


def _adamw_math(w, g, m, v):
    m = ADAM_B1 * m + (1.0 - ADAM_B1) * g
    v = ADAM_B2 * v + (1.0 - ADAM_B2) * (g * g)
    m_hat = m / (1.0 - ADAM_B1 ** ADAM_STEP)
    v_hat = v / (1.0 - ADAM_B2 ** ADAM_STEP)
    delta = -ADAM_LR * (m_hat / (jnp.sqrt(v_hat) + ADAM_EPS) + ADAM_WD * w)
    return delta, m, v


def _adamw(name, w, g, m, v, rows):
    R, C = w.shape

    def body(w_ref, g_ref, m_ref, v_ref, d_out, m_out, v_out):
        d_out[...], m_out[...], v_out[...] = _adamw_math(w_ref[...], g_ref[...], m_ref[...], v_ref[...])

    spec = pl.BlockSpec((rows, C), lambda i: (i, 0))
    return pl.pallas_call(
        body, name=name, grid=(R // rows,), in_specs=[spec] * 4, out_specs=[spec] * 3,
        out_shape=[jax.ShapeDtypeStruct((R, C), F32)] * 3,
        compiler_params=_params(("parallel",)),
    )(w, g, m, v)


def _adamw_square(g_sq, ws, ms, vs):
    def body(g_ref, *refs):
        w_refs, m_refs, v_refs, outs = refs[0:3], refs[3:6], refs[6:9], refs[9:]
        for k in range(3):
            g = g_ref[k]
            outs[k][0] = g
            outs[3 + k][0], outs[6 + k][0], outs[9 + k][0] = _adamw_math(w_refs[k][0], g, m_refs[k][0], v_refs[k][0])

    out = pl.pallas_call(
        body, name="adamw_square", out_shape=[jax.ShapeDtypeStruct((1, SQ_ROWS, D), F32)] * 12,
        compiler_params=_params(),
    )(g_sq, *ws, *ms, *vs)
    return out[0:3], out[3:6], out[6:9], out[9:12]


def _small_update(total, lbw, w8, m8, v8):
    def body(t_ref, lbw_ref, w_ref, m_ref, v_ref, g_out, d_out, m_out, v_out):
        lb = 1.0 / (1.0 + jnp.exp(lbw_ref[1:2, :] - lbw_ref[0:1, :]))
        dlb = t_ref[2:3, :] * lb * (1.0 - lb)
        g_out[...] = jnp.zeros_like(g_out)
        g_out[0:1, :] = t_ref[3:4, :]
        g_out[1:2, :] = dlb
        g_out[2:3, :] = -dlb
        g_out[3:4, :] = t_ref[1:2, :]
        g_out[4:5, :] = t_ref[0:1, :]
        g_out[5:6, :] = t_ref[4:5, :]
        d_out[...], m_out[...], v_out[...] = _adamw_math(w_ref[...], g_out[...], m_ref[...], v_ref[...])

    return pl.pallas_call(
        body, name="small_update", out_shape=[jax.ShapeDtypeStruct((8, D), F32)] * 4,
        compiler_params=_params(),
    )(total, lbw, w8, m8, v8)


def _pack8(norm_w, lbw, hnw, fnw, sinks):
    pad = jnp.zeros((1, D - 16), F32)
    return jnp.concatenate([norm_w, lbw, hnw, fnw.reshape(1, D), jnp.concatenate([sinks, pad], axis=1),
                            jnp.zeros((2, D), F32)], axis=0)


def _unpack8(a):
    return a[0:1], a[1:3], a[3:4], a[5:6, 0:16], a[4]


def _local_step(order_arr, x, tgt, norm_w, lbw, hnw, sinks, fnw, win_mine, wsq_mine, exchange):
    proj, xnt_bf, win_bf = _fwd_proj(order_arr, x, norm_w, win_mine)
    oh, states, wsq_all = _hgrn_fwd(proj, lbw, wsq_mine)
    wsq_bf = wsq_all.reshape(SHARDS, 3, SQ_ROWS, D).transpose(1, 0, 2, 3).reshape(3, D, D)
    oa = _attn_fwd(proj, sinks)
    dx2, doh, doa, dhg, dtail, lhs, rhs, loss8, vec_mid = _mid(x, tgt, proj, oh, oa, hnw, fnw.reshape(1, D), wsq_bf)
    gsq, gsq_b = _wgrad_square(lhs, rhs)
    dhead, dlb = _hgrn_bwd(proj, lbw, states, doh)
    daq, dak, dav, dsink = _attn_bwd(proj, sinks, oa, doa)
    pieces = [dhead, dhg, daq, dak, dav, dtail]
    sums = exchange(*_wgrad_in(xnt_bf, pieces), gsq, gsq_b)
    wt_bf = win_bf.transpose(0, 2, 1).reshape(D_IN, D)
    grad_x, gnw, win_got, sq_got = _bwd_dx(pieces, wt_bf, x, norm_w, dx2, sums[2], sums[3])
    sink_row = jnp.concatenate([dsink[:, 0].reshape(1, 16), jnp.zeros((1, D - 16), F32)], axis=1)
    loss_row = jnp.broadcast_to(loss8[0:1, 0:1], (1, D))
    vec = jnp.concatenate([vec_mid[0:2], dlb, gnw, sink_row, loss_row, jnp.zeros((2, D), F32)], axis=0)
    return grad_x, sums, (win_got, sq_got), vec


def kernel(x, norm_w, w_in, hgrn_lower_bound, hgrn_norm_w, w_branch_hgrn, attn_sinks, w_branch_attn, w_out, final_norm_w, loss_target, m_norm_w, m_w_in, m_hgrn_lower_bound, m_hgrn_norm_w, m_w_branch_hgrn, m_attn_sinks, m_w_branch_attn, m_w_out, m_final_norm_w, v_norm_w, v_w_in, v_hgrn_lower_bound, v_hgrn_norm_w, v_w_branch_hgrn, v_attn_sinks, v_w_branch_attn, v_w_out, v_final_norm_w):
    c_arr = lax.axis_index("c").astype(jnp.int32).reshape(1)
    j_arr = (2 * lax.axis_index("x") + lax.axis_index("y")).astype(jnp.int32).reshape(1)
    jc_arr = jnp.concatenate([j_arr, c_arr])

    win_mine, wsq_mine = _cast_shards(j_arr, w_in[0], w_branch_hgrn[0], w_branch_attn[0], w_out[0])
    xi, yi = lax.axis_index("x"), lax.axis_index("y")
    order_arr = jnp.stack([2 * xi + yi] + [2 * _flip(xi, fx) + _flip(yi, fy) for fx, fy in CHIP_FLIPS]).astype(jnp.int32)

    def chip_sums(gwin, gwin_b, gsq, gsq_b):
        return _add_halves(c_arr, gwin, gsq, *_swap_halves(gwin_b, gsq_b))

    grad_x, (swin, ssq, _, _), arrived, vec = _local_step(
        order_arr, x[0], loss_target[0], norm_w, hgrn_lower_bound, hgrn_norm_w, attn_sinks, final_norm_w, win_mine, wsq_mine,
        chip_sums)
    g_win, g_sq = _join_halves(*_sum_chips(jc_arr, swin, ssq, *arrived))

    d_win, nm_win, nv_win = _adamw("adamw_w_in", w_in[0], g_win, m_w_in[0], v_w_in[0], 128)
    g_sqs, d_sqs, nm_sqs, nv_sqs = _adamw_square(
        g_sq, (w_branch_hgrn, w_branch_attn, w_out), (m_w_branch_hgrn, m_w_branch_attn, m_w_out),
        (v_w_branch_hgrn, v_w_branch_attn, v_w_out))

    total = _allreduce_small(vec)
    loss = total[5, 0]
    g8, d8, nm8, nv8 = _small_update(
        total, hgrn_lower_bound,
        _pack8(norm_w, hgrn_lower_bound, hgrn_norm_w, final_norm_w, attn_sinks),
        _pack8(m_norm_w, m_hgrn_lower_bound, m_hgrn_norm_w, m_final_norm_w, m_attn_sinks),
        _pack8(v_norm_w, v_hgrn_lower_bound, v_hgrn_norm_w, v_final_norm_w, v_attn_sinks))

    def assemble(win, sq, small):
        nw, lb, hn, sk, fn = _unpack8(small)
        return (nw, win.reshape(1, D, SHARD_W), lb, hn, sq[0], sk, sq[1], sq[2], fn)

    return (loss, grad_x.reshape(1, -1, D),
            *assemble(g_win, g_sqs, g8), *assemble(d_win, d_sqs, d8),
            *assemble(nm_win, nm_sqs, nm8), *assemble(nv_win, nv_sqs, nv8))
```

```python
import functools

import jax
import jax.numpy as jnp
from jax import lax
from jax.experimental import pallas as pl
from jax.experimental.pallas import tpu as pltpu

F32 = jnp.float32
BF16 = jnp.bfloat16

D = 1024
D_IN = 8704
SHARDS = 4
SHARD_W = D_IN // SHARDS
SQ_ROWS = D // SHARDS
HEADS = 8
HEAD_W = 128
CHUNK = 64
SUB = 4
ATT_BLOCK = 128
KV_HEADS = 4
HEAD_DIM = 64
EPS = 1e-6
NEG = -1e30
SCALE = HEAD_DIM ** -0.5
COL_HG, COL_AQ, COL_AK, COL_AV, COL_AG, COL_MH, COL_MA = 3072, 4096, 5120, 5376, 5632, 6656, 7680

ADAM_LR, ADAM_B1, ADAM_B2, ADAM_EPS, ADAM_WD, ADAM_STEP = 0.001, 0.9, 0.999, 1e-08, 0.01, 10

VMEM_LIMIT = 56 * 1024 * 1024
MESH = pl.DeviceIdType.MESH
HBM_SPEC = pl.BlockSpec(memory_space=pltpu.HBM)
CHIP_FLIPS = ((1, 0), (0, 1), (1, 1))


def _dot(a, b):
    return jnp.dot(a, b, preferred_element_type=F32)


def _dot_nt(a, b):
    return lax.dot_general(a, b, (((1,), (1,)), ((), ())), preferred_element_type=F32)


def _dot_tn(a, b):
    return lax.dot_general(a, b, (((0,), (0,)), ((), ())), preferred_element_type=F32)


def _sigmoid(v):
    return 1.0 / (1.0 + jnp.exp(-v))


def _bf(v):
    return v.astype(BF16)


def _tri_dot2(tri, v):
    a = _bf(v)
    return _dot(tri, a) + _dot(tri, _bf(v - a.astype(F32)))


def _params(sem=None):
    return pltpu.CompilerParams(dimension_semantics=sem, vmem_limit_bytes=VMEM_LIMIT)


def _cast_shards(j_arr, win_s, wbh_s, wba_s, wout_s):
    steps = 4
    rows = D // steps

    def body(j_ref, win_ref, a_ref, b_ref, c_ref, win_o, sq_o):
        win_o[...] = _bf(win_ref[...])

        @pl.when(pl.program_id(0) == 0)
        def _():
            sq_o[0:SQ_ROWS, :] = _bf(a_ref[...])
            sq_o[SQ_ROWS:2 * SQ_ROWS, :] = _bf(b_ref[...])
            sq_o[2 * SQ_ROWS:3 * SQ_ROWS, :] = _bf(c_ref[...])

    whole = pl.BlockSpec((SQ_ROWS, D), lambda i, j: (0, 0))
    return pl.pallas_call(
        body, name="cast_shards",
        grid_spec=pltpu.PrefetchScalarGridSpec(
            num_scalar_prefetch=1, grid=(steps,),
            in_specs=[pl.BlockSpec((rows, SHARD_W), lambda i, j: (i, 0)), whole, whole, whole],
            out_specs=[pl.BlockSpec((None, rows, SHARD_W), lambda i, j: (j[0], i, 0)),
                       pl.BlockSpec((None, 3 * SQ_ROWS, D), lambda i, j: (j[0], 0, 0))]),
        out_shape=[jax.ShapeDtypeStruct((SHARDS, D, SHARD_W), BF16), jax.ShapeDtypeStruct((SHARDS, 3 * SQ_ROWS, D), BF16)],
        compiler_params=_params(("arbitrary",)),
    )(j_arr, win_s, wbh_s, wba_s, wout_s)


def _fwd_proj(order_arr, x, norm_w, win_all):
    T = x.shape[0]
    tm = min(512, T)
    nt = T // tm

    def body(order_ref, x_ref, nw_ref, win_in, proj_ref, xn_ref, win_out, w_scr, xn_scr, sems, send_sems, recv_sems):
        del win_in
        p, i = pl.program_id(0), pl.program_id(1)

        def load(n):
            return pltpu.make_async_copy(win_out.at[order_ref[n]], w_scr.at[n % 2], sems.at[n % 2])

        @pl.when((p == 0) & (i == 0))
        def _():
            _gather_start(win_out, _win_half, send_sems, recv_sems)
            load(0).start()
            load(0).wait()

        @pl.when((p == 1) & (i == 0))
        def _():
            _gather_land(win_out, _win_half, 0, send_sems, recv_sems)
            load(1).start()
            load(1).wait()

        for k in range(1, SHARDS - 1):
            @pl.when((p == k) & (i == nt // 2))
            def _():
                _gather_land(win_out, _win_half, k, send_sems, recv_sems)
                load(k + 1).start()

            @pl.when((p == k + 1) & (i == 0))
            def _():
                load(k + 1).wait()

        @pl.when(p == 0)
        def _():
            xf = x_ref[...]
            rs = lax.rsqrt(jnp.mean(xf * xf, axis=1, keepdims=True) + EPS)
            xn = _bf((xf * rs) * nw_ref[...])
            xn_scr[i] = xn
            xn_ref[...] = xn.T

        proj_ref[...] = _bf(_dot(xn_scr[i], w_scr[p % 2]))

        @pl.when((p == SHARDS - 1) & (i == nt - 1))
        def _():
            _gather_drain(win_out, _win_half, send_sems, recv_sems)

    first = lambda p, i: jnp.where(p == 0, i, nt - 1)
    return pl.pallas_call(
        body, name="fwd_proj",
        grid_spec=pltpu.PrefetchScalarGridSpec(
            num_scalar_prefetch=1, grid=(SHARDS, nt),
            in_specs=[pl.BlockSpec((tm, D), lambda p, i, order: (first(p, i), 0)),
                      pl.BlockSpec((1, D), lambda p, i, order: (0, 0)), HBM_SPEC],
            out_specs=[pl.BlockSpec((tm, SHARD_W), lambda p, i, order: (i, order[p])),
                       pl.BlockSpec((D, tm), lambda p, i, order: (0, first(p, i))),
                       HBM_SPEC],
            scratch_shapes=[pltpu.VMEM((2, D, SHARD_W), BF16), pltpu.VMEM((nt, tm, D), BF16), pltpu.SemaphoreType.DMA((2,)),
                            pltpu.SemaphoreType.DMA((6,)), pltpu.SemaphoreType.DMA((6,))]),
        out_shape=[jax.ShapeDtypeStruct((T, D_IN), BF16), jax.ShapeDtypeStruct((D, T), BF16),
                   jax.ShapeDtypeStruct((SHARDS, D, SHARD_W), BF16)],
        input_output_aliases={3: 2},
        compiler_params=_params(("arbitrary", "arbitrary")),
    )(order_arr, x, norm_w, win_all)


def _hgrn_gates(hq_ref, hf_ref, lbw_ref, b_scr):
    lb = 1.0 / (1.0 + jnp.exp(lbw_ref[1:2, :] - lbw_ref[0:1, :]))
    hf = hf_ref[...].astype(F32)
    sig = _sigmoid(hf)
    f = lb + (1.0 - lb) * sig
    g = jnp.log(f)
    hq = hq_ref[...].astype(F32)
    sq = _sigmoid(hq)
    q = hq * sq
    row = lax.broadcasted_iota(jnp.int32, (CHUNK, CHUNK), 0)
    col = lax.broadcasted_iota(jnp.int32, (CHUNK, CHUNK), 1)
    causal = row >= col
    b = _tri_dot2(jnp.where(causal, 1.0, 0.0).astype(BF16), g)
    b_scr[...] = b
    bc = b_scr[CHUNK - 1:CHUNK, :]
    r = b_scr[CHUNK // 2 - 1:CHUNK // 2, :]
    return dict(lb=lb, sig=sig, f=f, k=1.0 - f, hq=hq, sq=sq, q=q, b=b, bc=bc, r=r, causal=causal)


def _hgrn_fwd(proj, lbw, wsq_all):
    T = proj.shape[0]
    n = T // CHUNK

    def body(hq_ref, hf_ref, hi_ref, lbw_ref, wsq_in, o_ref, st_ref, wsq_out, s_scr, b_scr, send_sems, recv_sems):
        del wsq_in

        @pl.when(pl.program_id(0) == 0)
        def _():
            _gather_start(wsq_out, _sq_half, send_sems, recv_sems)
            s_scr[...] = jnp.zeros_like(s_scr)

        for c in range(SUB):
            rows = pl.ds(c * CHUNK, CHUNK)
            gt = _hgrn_gates(hq_ref.at[rows, :], hf_ref.at[rows, :], lbw_ref, b_scr.at[rows, :])
            b, bc, r, q, k = gt["b"], gt["bc"], gt["r"], gt["q"], gt["k"]
            qe = _bf(q * jnp.exp(b))
            qr = _bf(q * jnp.exp(b - r))
            kr = _bf(k * jnp.exp(r - b))
            kl = _bf(k * jnp.exp(bc - b))
            ebc = jnp.exp(bc)
            v = hi_ref[rows, :]
            scores = [_bf(jnp.where(gt["causal"], _dot_nt(qr[:, h * HEAD_W:(h + 1) * HEAD_W], kr[:, h * HEAD_W:(h + 1) * HEAD_W]), 0.0))
                      for h in range(HEADS)]
            for h in range(HEADS):
                sl = slice(h * HEAD_W, (h + 1) * HEAD_W)
                st = s_scr[h]
                st_ref[c, h] = st
                o_ref[rows, sl] = _bf(_dot(scores[h], v[:, sl]) + _dot_nt(qe[:, sl], _bf(st)))
                s_scr[h] = ebc[:, sl] * st + _dot_tn(v[:, sl], kl[:, sl])

        @pl.when(pl.program_id(0) == n // SUB - 1)
        def _():
            _gather_finish(wsq_out, _sq_half, send_sems, recv_sems)

    col = lambda j: pl.BlockSpec((SUB * CHUNK, D), lambda i: (i, j))
    return pl.pallas_call(
        body, name="hgrn_fwd", grid=(n // SUB,),
        in_specs=[col(0), col(1), col(2), pl.BlockSpec((2, D), lambda i: (0, 0)), HBM_SPEC],
        out_specs=[pl.BlockSpec((SUB * CHUNK, D), lambda i: (i, 0)),
                   pl.BlockSpec((SUB, HEADS, HEAD_W, HEAD_W), lambda i: (i, 0, 0, 0)), HBM_SPEC],
        out_shape=[jax.ShapeDtypeStruct((T, D), BF16), jax.ShapeDtypeStruct((n, HEADS, HEAD_W, HEAD_W), F32),
                   jax.ShapeDtypeStruct((SHARDS, 3 * SQ_ROWS, D), BF16)],
        input_output_aliases={4: 2},
        scratch_shapes=[pltpu.VMEM((HEADS, HEAD_W, HEAD_W), F32), pltpu.VMEM((SUB * CHUNK, D), F32),
                        pltpu.SemaphoreType.DMA((6,)), pltpu.SemaphoreType.DMA((6,))],
        compiler_params=_params(("arbitrary",)),
    )(proj, proj, proj, lbw, wsq_all)


def _hgrn_bwd(proj, lbw, states, do):
    T = proj.shape[0]
    n = T // CHUNK

    def body(hq_ref, hf_ref, hi_ref, lbw_ref, st_ref, do_ref, dp_ref, dlb_ref,
             ds_scr, b_scr, dq_scr, dk_scr, dv_scr, late_scr, early_scr, ex_scr):
        @pl.when(pl.program_id(0) == 0)
        def _():
            ds_scr[...] = jnp.zeros_like(ds_scr)
            dlb_ref[...] = jnp.zeros_like(dlb_ref)

        for c in reversed(range(SUB)):
            rows = pl.ds(c * CHUNK, CHUNK)
            gt = _hgrn_gates(hq_ref.at[rows, :], hf_ref.at[rows, :], lbw_ref, b_scr.at[rows, :])
            b, bc, r, q, k = gt["b"], gt["bc"], gt["r"], gt["q"], gt["k"]
            eb = jnp.exp(b)
            er = jnp.exp(b - r)
            erk = jnp.exp(r - b)
            el = jnp.exp(bc - b)
            ebc = jnp.exp(bc)
            qe, qr, kr, kl = _bf(q * eb), _bf(q * er), _bf(k * erk), _bf(k * el)
            v = hi_ref[rows, :]
            do_b = do_ref[rows, :]
            do_t = do_b.T
            causal_t = lax.broadcasted_iota(jnp.int32, (CHUNK, CHUNK), 0) <= lax.broadcasted_iota(jnp.int32, (CHUNK, CHUNK), 1)
            firsts = []
            for h in range(HEADS):
                sl = slice(h * HEAD_W, (h + 1) * HEAD_W)
                firsts.append((_bf(jnp.where(causal_t, _dot_nt(kr[:, sl], qr[:, sl]), 0.0)),
                               _bf(jnp.where(gt["causal"], _dot_nt(do_b[:, sl], v[:, sl]), 0.0)),
                               _bf(jnp.where(causal_t, _dot_nt(v[:, sl], do_b[:, sl]), 0.0))))
            for h in range(HEADS):
                sl = slice(h * HEAD_W, (h + 1) * HEAD_W)
                st0 = st_ref[c, h]
                dst = ds_scr[h]
                dst_b = _bf(dst)
                a_t, da, da_t = firsts[h]
                mq = _dot(da, kr[:, sl])
                mk = _dot(da_t, qr[:, sl])
                dq_in = eb[:, sl] * _dot(do_b[:, sl], _bf(st0))
                dk_in = el[:, sl] * _dot(v[:, sl], dst_b)
                dq_scr[rows, sl] = er[:, sl] * mq + dq_in
                dk_scr[rows, sl] = erk[:, sl] * mk + dk_in
                dv_scr[rows, sl] = _dot(a_t, do_b[:, sl]) + _dot_nt(kl[:, sl], dst_b)
                late_scr[rows, sl] = q[:, sl] * dq_in + qr[:, sl].astype(F32) * mq - kr[:, sl].astype(F32) * mk
                early_scr[rows, sl] = k[:, sl] * dk_in
                ex_scr[:, sl] = jnp.sum(dst * st0, axis=0, keepdims=True)
                ds_scr[h] = ebc[:, sl] * dst + _dot(do_t[sl, :], qe[:, sl])

            dq, dk = dq_scr[rows, :], dk_scr[rows, :]
            row = lax.broadcasted_iota(jnp.int32, (CHUNK, CHUNK), 0)
            col = lax.broadcasted_iota(jnp.int32, (CHUNK, CHUNK), 1)
            at_or_after = jnp.where(col >= row, 1.0, 0.0).astype(BF16)
            before = jnp.where(col < row, 1.0, 0.0).astype(BF16)
            dg = _tri_dot2(jnp.concatenate([at_or_after, before], axis=1),
                           jnp.concatenate([late_scr[rows, :], early_scr[rows, :]], axis=0)) + ebc * ex_scr[...]
            df = dg / gt["f"] - dk
            sig, sq, hq, lb = gt["sig"], gt["sq"], gt["hq"], gt["lb"]
            dp_ref[rows, 0:D] = _bf(dq * (sq * (1.0 + hq * (1.0 - sq))))
            dp_ref[rows, D:2 * D] = _bf(df * (1.0 - lb) * sig * (1.0 - sig))
            dp_ref[rows, 2 * D:3 * D] = _bf(dv_scr[rows, :])
            dlb_ref[...] += jnp.sum(df * (1.0 - sig), axis=0, keepdims=True)

    ns = n // SUB
    col = lambda j: pl.BlockSpec((SUB * CHUNK, D), lambda i: (ns - 1 - i, j))
    return pl.pallas_call(
        body, name="hgrn_bwd", grid=(ns,),
        in_specs=[col(0), col(1), col(2), pl.BlockSpec((2, D), lambda i: (0, 0)),
                  pl.BlockSpec((SUB, HEADS, HEAD_W, HEAD_W), lambda i: (ns - 1 - i, 0, 0, 0)),
                  pl.BlockSpec((SUB * CHUNK, D), lambda i: (ns - 1 - i, 0))],
        out_specs=[pl.BlockSpec((SUB * CHUNK, 3 * D), lambda i: (ns - 1 - i, 0)),
                   pl.BlockSpec((1, D), lambda i: (0, 0))],
        out_shape=[jax.ShapeDtypeStruct((T, 3 * D), BF16), jax.ShapeDtypeStruct((1, D), F32)],
        scratch_shapes=[pltpu.VMEM((HEADS, HEAD_W, HEAD_W), F32)] + [pltpu.VMEM((SUB * CHUNK, D), F32)] * 6
                       + [pltpu.VMEM((1, D), F32)],
        compiler_params=_params(("arbitrary",)),
    )(proj, proj, proj, lbw, states, do)


def _attn_masks(blk):
    qi = lax.broadcasted_iota(jnp.int32, (ATT_BLOCK, 2 * ATT_BLOCK), 0)
    kj = lax.broadcasted_iota(jnp.int32, (ATT_BLOCK, 2 * ATT_BLOCK), 1)
    band = (kj > qi) & (kj <= qi + ATT_BLOCK)
    return band & ((blk > 0) | (kj >= ATT_BLOCK))


def _head_pair_operand(t, hp, low):
    mine = low if hp == 0 else jnp.logical_not(low)
    both = jnp.where(mine, t, pltpu.roll(t, HEAD_DIM, 1))
    return _bf(jnp.concatenate([jnp.where(low, both, 0.0), jnp.where(low, 0.0, both)], axis=0))


def _attn_probs(s, sink, valid):
    s = jnp.where(valid, s * SCALE, NEG)
    m = jnp.maximum(jnp.max(s, axis=1, keepdims=True), sink)
    p = jnp.exp(s - m)
    es = jnp.exp(sink - m)
    inv = 1.0 / (jnp.sum(p, axis=1, keepdims=True) + es)
    return p * inv, es * inv


def _attn_fwd(proj, sinks):
    T = proj.shape[0]
    nb = T // ATT_BLOCK
    W2 = 2 * ATT_BLOCK

    def body(sink_ref, q_ref, kp_ref, kc_ref, vp_ref, vc_ref, o_ref):
        blk = pl.program_id(0)
        valid = _attn_masks(blk)
        low = lax.broadcasted_iota(jnp.int32, (1, 2 * HEAD_DIM), 1) < HEAD_DIM
        kcat = jnp.concatenate([kp_ref[...], kc_ref[...]], axis=0).astype(F32)
        vcat = jnp.concatenate([vp_ref[...], vc_ref[...]], axis=0).astype(F32)
        for h in range(KV_HEADS):
            tl = slice((h // 2) * 128, (h // 2) * 128 + 128)
            mine = low if h % 2 == 0 else jnp.logical_not(low)
            kh = _bf(jnp.where(mine, kcat[:, tl], pltpu.roll(kcat[:, tl], HEAD_DIM, 1)))
            vh = _bf(jnp.where(mine, vcat[:, tl], pltpu.roll(vcat[:, tl], HEAD_DIM, 1)))
            for t in range(2):
                ql = slice((2 * h + t) * 128, (2 * h + t) * 128 + 128)
                q2 = q_ref[:, ql]
                outs = []
                for p in range(2):
                    qm = _bf(jnp.where(low if p == 0 else jnp.logical_not(low), q2, 0.0))
                    probs, _ = _attn_probs(_dot_nt(qm, kh), sink_ref[0, 4 * h + 2 * t + p], valid)
                    outs.append(_dot(_bf(probs), vh))
                o_ref[:, ql] = _bf(jnp.where(low, outs[0], outs[1]))

    prev = lambda i: jnp.maximum(i - 1, 0)
    return pl.pallas_call(
        body, name="attn_fwd", grid=(nb,),
        in_specs=[pl.BlockSpec(memory_space=pltpu.SMEM),
                  pl.BlockSpec((ATT_BLOCK, D), lambda i: (i, COL_AQ // D)),
                  pl.BlockSpec((ATT_BLOCK, 256), lambda i: (prev(i), COL_AK // 256)),
                  pl.BlockSpec((ATT_BLOCK, 256), lambda i: (i, COL_AK // 256)),
                  pl.BlockSpec((ATT_BLOCK, 256), lambda i: (prev(i), COL_AV // 256)),
                  pl.BlockSpec((ATT_BLOCK, 256), lambda i: (i, COL_AV // 256))],
        out_specs=pl.BlockSpec((ATT_BLOCK, D), lambda i: (i, 0)),
        out_shape=jax.ShapeDtypeStruct((T, D), BF16),
        compiler_params=_params(("arbitrary",)),
    )(sinks, proj, proj, proj, proj, proj)


def _attn_bwd(proj, sinks, o, do):
    T = proj.shape[0]
    nb = T // ATT_BLOCK
    W2 = 2 * ATT_BLOCK

    def body(sink_ref, q_ref, kp_ref, kc_ref, vp_ref, vc_ref, o_ref, do_ref,
             dq_ref, dk_ref, dv_ref, dsink_ref, ck_scr, cv_scr, nk_scr, nv_scr):
        blk = pl.program_id(0)

        @pl.when(blk == 0)
        def _():
            ck_scr[...] = jnp.zeros_like(ck_scr)
            cv_scr[...] = jnp.zeros_like(cv_scr)
            dsink_ref[...] = jnp.zeros_like(dsink_ref)

        @pl.when(blk < nb)
        def _():
            valid = _attn_masks(blk)
            low = lax.broadcasted_iota(jnp.int32, (1, 2 * HEAD_DIM), 1) < HEAD_DIM
            kcat = jnp.concatenate([kp_ref[...], kc_ref[...]], axis=0).astype(F32)
            vcat = jnp.concatenate([vp_ref[...], vc_ref[...]], axis=0).astype(F32)
            for h in range(KV_HEADS):
                tl = slice((h // 2) * 128, (h // 2) * 128 + 128)
                kbd = _head_pair_operand(kcat[:, tl], h % 2, low)
                vbd = _head_pair_operand(vcat[:, tl], h % 2, low)
                dkbd = jnp.zeros((2 * W2, 128), F32)
                dvbd = jnp.zeros((2 * W2, 128), F32)
                tiles = []
                for t in range(2):
                    ql = slice((2 * h + t) * 128, (2 * h + t) * 128 + 128)
                    q2 = _bf(q_ref[:, ql])
                    do2_b = do_ref[:, ql]
                    doo = do2_b.astype(F32) * o_ref[:, ql].astype(F32)
                    dsum0 = jnp.sum(jnp.where(low, doo, 0.0), axis=1, keepdims=True)
                    dsum1 = jnp.sum(jnp.where(low, 0.0, doo), axis=1, keepdims=True)
                    tiles.append((ql, q2, do2_b, dsum0, dsum1, _dot_nt(q2, kbd), _dot_nt(do2_b, vbd)))
                grads = []
                for t, (ql, q2, do2_b, dsum0, dsum1, s2, dp2) in enumerate(tiles):
                    head = 4 * h + 2 * t
                    p0, ps0 = _attn_probs(s2[:, 0:W2], sink_ref[0, head], valid)
                    p1, ps1 = _attn_probs(s2[:, W2:2 * W2], sink_ref[0, head + 1], valid)
                    ds2 = _bf(jnp.concatenate([p0 * (dp2[:, 0:W2] - dsum0), p1 * (dp2[:, W2:2 * W2] - dsum1)], axis=1) * SCALE)
                    grads.append((ds2, _bf(jnp.concatenate([p0, p1], axis=1))))
                    dsink_ref[head:head + 1, :] += jnp.zeros((1, 128), F32) - jnp.sum(ps0 * dsum0, axis=0, keepdims=True)
                    dsink_ref[head + 1:head + 2, :] += jnp.zeros((1, 128), F32) - jnp.sum(ps1 * dsum1, axis=0, keepdims=True)
                for (ql, q2, do2_b, _, _, _, _), (ds2, p2) in zip(tiles, grads):
                    dq_ref[:, ql] = _bf(_dot(ds2, kbd))
                    dkbd = dkbd + _dot_tn(ds2, q2)
                    dvbd = dvbd + _dot_tn(p2, do2_b)
                dk2 = jnp.where(low, dkbd[0:W2], dkbd[W2:2 * W2])
                dv2 = jnp.where(low, dvbd[0:W2], dvbd[W2:2 * W2])
                dk2 = dk2 + pltpu.roll(dk2, HEAD_DIM, 1)
                dv2 = dv2 + pltpu.roll(dv2, HEAD_DIM, 1)
                if h % 2 == 0:
                    keep_k, keep_v = dk2, dv2
                else:
                    nk_scr[:, tl] = jnp.where(low, keep_k, dk2)
                    nv_scr[:, tl] = jnp.where(low, keep_v, dv2)
            dk_ref[...] = _bf(ck_scr[...] + nk_scr[0:ATT_BLOCK, :])
            dv_ref[...] = _bf(cv_scr[...] + nv_scr[0:ATT_BLOCK, :])
            ck_scr[...] = nk_scr[ATT_BLOCK:2 * ATT_BLOCK, :]
            cv_scr[...] = nv_scr[ATT_BLOCK:2 * ATT_BLOCK, :]

        @pl.when(blk == nb)
        def _():
            dk_ref[...] = _bf(ck_scr[...])
            dv_ref[...] = _bf(cv_scr[...])

    cur = lambda i: jnp.minimum(i, nb - 1)
    prev = lambda i: jnp.maximum(cur(i) - 1, 0)
    late = lambda i: jnp.maximum(i - 1, 0)
    dq, dk, dv, dsink = pl.pallas_call(
        body, name="attn_bwd", grid=(nb + 1,),
        in_specs=[pl.BlockSpec(memory_space=pltpu.SMEM),
                  pl.BlockSpec((ATT_BLOCK, D), lambda i: (cur(i), COL_AQ // D)),
                  pl.BlockSpec((ATT_BLOCK, 256), lambda i: (prev(i), COL_AK // 256)),
                  pl.BlockSpec((ATT_BLOCK, 256), lambda i: (cur(i), COL_AK // 256)),
                  pl.BlockSpec((ATT_BLOCK, 256), lambda i: (prev(i), COL_AV // 256)),
                  pl.BlockSpec((ATT_BLOCK, 256), lambda i: (cur(i), COL_AV // 256)),
                  pl.BlockSpec((ATT_BLOCK, D), lambda i: (cur(i), 0)),
                  pl.BlockSpec((ATT_BLOCK, D), lambda i: (cur(i), 0))],
        out_specs=[pl.BlockSpec((ATT_BLOCK, D), lambda i: (cur(i), 0)),
                   pl.BlockSpec((ATT_BLOCK, 256), lambda i: (late(i), 0)),
                   pl.BlockSpec((ATT_BLOCK, 256), lambda i: (late(i), 0)),
                   pl.BlockSpec((16, 128), lambda i: (0, 0))],
        out_shape=[jax.ShapeDtypeStruct((T, D), BF16), jax.ShapeDtypeStruct((T, 256), BF16),
                   jax.ShapeDtypeStruct((T, 256), BF16), jax.ShapeDtypeStruct((16, 128), F32)],
        scratch_shapes=[pltpu.VMEM((ATT_BLOCK, 256), F32), pltpu.VMEM((ATT_BLOCK, 256), F32),
                        pltpu.VMEM((2 * ATT_BLOCK, 256), F32), pltpu.VMEM((2 * ATT_BLOCK, 256), F32)],
        compiler_params=_params(("arbitrary",)),
    )(sinks, proj, proj, proj, proj, proj, o, do)
    return dq, dk, dv, dsink


def _mid(x, tgt, proj, oh, oa, hnw, fnw, wsq_bf):
    T = x.shape[0]
    tm = min(256, T)
    nt = T // tm

    def body(x_ref, tgt_ref, oh_ref, oa_ref, hg_ref, ag0_ref, ag1_ref, mh0_ref, mh1_ref, ma0_ref, ma1_ref,
             hnw_ref, fnw_ref, w_hbm,
             dx2_ref, doh_ref, doa_ref, dhg_ref, dtail_ref, lhs_ref, rhs_ref, loss_ref, vec_ref,
             w_scr, xh_scr, rs_scr, sem):
        @pl.when(pl.program_id(0) == 0)
        def _():
            cp = pltpu.make_async_copy(w_hbm, w_scr, sem)
            cp.start()
            cp.wait()
            loss_ref[...] = jnp.zeros_like(loss_ref)
            vec_ref[...] = jnp.zeros_like(vec_ref)

        oh = oh_ref[...].astype(F32)
        for h in range(HEADS):
            sl = slice(h * HEAD_W, (h + 1) * HEAD_W)
            ohh = oh[:, sl]
            rs = lax.rsqrt(jnp.mean(ohh * ohh, axis=1, keepdims=True) + EPS)
            xh_scr[:, sl] = ohh * rs
            rs_scr[:, sl] = jnp.broadcast_to(rs, (tm, HEAD_W))
        xh = xh_scr[...]
        hnw = hnw_ref[...]
        on = xh * hnw
        hg = hg_ref[...].astype(F32)
        sg = _sigmoid(hg)
        silu_g = hg * sg
        gated_h = _bf(on * silu_g)
        oa = oa_ref[...].astype(F32)
        ag = jnp.concatenate([ag0_ref[...], ag1_ref[...]], axis=1).astype(F32)
        sa = _sigmoid(ag)
        silu_a = ag * sa
        gated_a = _bf(oa * silu_a)
        yh = _dot(gated_h, w_scr[0])
        ya = _dot(gated_a, w_scr[1])
        lhs_ref[0] = gated_h.T
        lhs_ref[1] = gated_a.T
        smh = _sigmoid(jnp.concatenate([mh0_ref[...], mh1_ref[...]], axis=1).astype(F32))
        sma = _sigmoid(jnp.concatenate([ma0_ref[...], ma1_ref[...]], axis=1).astype(F32))
        merged = _bf(smh * yh + sma * ya)
        lhs_ref[2] = merged.T
        x2 = x_ref[...] + _dot(merged, w_scr[2])
        rs2 = lax.rsqrt(jnp.mean(x2 * x2, axis=1, keepdims=True) + EPS)
        xh2 = x2 * rs2
        fnw = fnw_ref[...]
        diff = xh2 * fnw - tgt_ref[...]
        loss_ref[...] += jnp.zeros_like(loss_ref) + jnp.sum(diff * diff) * (0.5 / D)

        dy = diff * (1.0 / D)
        vec_ref[0:1, :] += jnp.sum(dy * xh2, axis=0, keepdims=True)
        gy = dy * fnw
        dx2 = rs2 * (gy - xh2 * jnp.mean(gy * xh2, axis=1, keepdims=True))
        dx2_ref[...] = dx2
        dx2_b = _bf(dx2)
        rhs_ref[2] = dx2_b
        dmerged = _dot_nt(dx2_b, w_scr[2])
        dyh = dmerged * smh
        dya = dmerged * sma
        dtail_ref[:, D:2 * D] = _bf(dyh * yh * (1.0 - smh))
        dtail_ref[:, 2 * D:3 * D] = _bf(dya * ya * (1.0 - sma))
        dyh_b, dya_b = _bf(dyh), _bf(dya)
        rhs_ref[0] = dyh_b
        rhs_ref[1] = dya_b
        dgh = _dot_nt(dyh_b, w_scr[0])
        dga = _dot_nt(dya_b, w_scr[1])
        don = dgh * silu_g
        dhg_ref[...] = _bf(dgh * on * (sg * (1.0 + hg * (1.0 - sg))))
        vec_ref[1:2, :] += jnp.sum(don * xh, axis=0, keepdims=True)
        gxh = don * hnw
        rsb = rs_scr[...]
        for h in range(HEADS):
            sl = slice(h * HEAD_W, (h + 1) * HEAD_W)
            gh, xhh = gxh[:, sl], xh[:, sl]
            doh_ref[:, sl] = _bf(rsb[:, sl] * (gh - xhh * jnp.mean(gh * xhh, axis=1, keepdims=True)))
        doa_ref[...] = _bf(dga * silu_a)
        dtail_ref[:, 0:D] = _bf(dga * oa * (sa * (1.0 + ag * (1.0 - sa))))

    row = lambda w, j: pl.BlockSpec((tm, w), lambda i: (i, j))
    const = lambda r, c: pl.BlockSpec((r, c), lambda i: (0, 0))
    stack = pl.BlockSpec((3, tm, D), lambda i: (0, i, 0))
    stack_t = pl.BlockSpec((3, D, tm), lambda i: (0, 0, i))
    return pl.pallas_call(
        body, name="mid", grid=(nt,),
        in_specs=[row(D, 0), row(D, 0), row(D, 0), row(D, 0), row(D, COL_HG // D),
                  row(512, COL_AG // 512), row(512, COL_AG // 512 + 1),
                  row(512, COL_MH // 512), row(512, COL_MH // 512 + 1),
                  row(512, COL_MA // 512), row(512, COL_MA // 512 + 1),
                  const(1, D), const(1, D), HBM_SPEC],
        out_specs=[row(D, 0), row(D, 0), row(D, 0), row(D, 0), row(3 * D, 0), stack_t, stack, const(8, 128), const(8, D)],
        out_shape=[jax.ShapeDtypeStruct((T, D), F32), jax.ShapeDtypeStruct((T, D), BF16), jax.ShapeDtypeStruct((T, D), BF16),
                   jax.ShapeDtypeStruct((T, D), BF16), jax.ShapeDtypeStruct((T, 3 * D), BF16),
                   jax.ShapeDtypeStruct((3, D, T), BF16), jax.ShapeDtypeStruct((3, T, D), BF16),
                   jax.ShapeDtypeStruct((8, 128), F32), jax.ShapeDtypeStruct((8, D), F32)],
        scratch_shapes=[pltpu.VMEM((3, D, D), BF16), pltpu.VMEM((tm, D), F32), pltpu.VMEM((tm, D), F32),
                        pltpu.SemaphoreType.DMA],
        compiler_params=_params(("arbitrary",)),
    )(x, tgt, oh, oa, proj, proj, proj, proj, proj, proj, proj, hnw, fnw, wsq_bf)


def _wgrad_square(lhs_t, rhs):
    T = rhs.shape[1]
    tk = min(1024, T)
    steps = T // tk

    def body(a_ref, b_ref, g_ref, gb_ref):
        part = _dot(a_ref[...], b_ref[...])

        @pl.when(pl.program_id(1) == 0)
        def _():
            g_ref[...] = part

        @pl.when(pl.program_id(1) > 0)
        def _():
            g_ref[...] += part

        @pl.when(pl.program_id(1) == steps - 1)
        def _():
            gb_ref[...] = _bf(g_ref[...])

    return pl.pallas_call(
        body, name="wgrad_square", grid=(3, steps),
        in_specs=[pl.BlockSpec((None, D, tk), lambda k, i: (k, 0, i)), pl.BlockSpec((None, tk, D), lambda k, i: (k, i, 0))],
        out_specs=[pl.BlockSpec((None, D, D), lambda k, i: (k, 0, 0))] * 2,
        out_shape=[jax.ShapeDtypeStruct((3, D, D), F32), jax.ShapeDtypeStruct((3, D, D), BF16)],
        compiler_params=_params(("parallel", "arbitrary")),
    )(lhs_t, rhs)


def _bwd_dx(pieces, wt_bf, x, norm_w, dx2, swin_b, ssq_b):
    T = x.shape[0]
    tm = min(256, T)
    nt = T // tm
    widths = [p.shape[1] for p in pieces]
    n_p = len(pieces)

    def body(*refs):
        piece_refs = refs[:n_p]
        (w_hbm, x_ref, nw_ref, dx2_ref, swin_ref, ssq_ref,
         gx_ref, gnw_ref, win_got, sq_got, w_scr, sem, send_sems, recv_sems) = refs[n_p:]

        def scatter_copies():
            x_, y_, c_ = _place()
            copies = []
            for k, (fx, fy) in enumerate(CHIP_FLIPS):
                px, py = _flip(x_, fx), _flip(y_, fy)
                jr = 2 * px + py
                for a, (src, dst) in enumerate(((swin_ref.at[:, pl.ds(jr * SHARD_W, SHARD_W)], win_got.at[k]),
                                                (ssq_ref.at[:, pl.ds(jr * SQ_ROWS, SQ_ROWS), :], sq_got.at[k]))):
                    copies.append(pltpu.make_async_remote_copy(
                        src_ref=src, dst_ref=dst, send_sem=send_sems.at[2 * k + a], recv_sem=recv_sems.at[2 * k + a],
                        device_id=(px, py, c_), device_id_type=MESH))
            return copies

        @pl.when(pl.program_id(0) == 0)
        def _():
            for cp in scatter_copies():
                cp.start()
            cp = pltpu.make_async_copy(w_hbm, w_scr, sem)
            cp.start()
            cp.wait()
            gnw_ref[...] = jnp.zeros_like(gnw_ref)

        dxn = None
        off = 0
        for ref, w in zip(piece_refs, widths):
            part = _dot(ref[...], w_scr[off:off + w, :])
            dxn = part if dxn is None else dxn + part
            off += w
        xf = x_ref[...]
        rs = lax.rsqrt(jnp.mean(xf * xf, axis=1, keepdims=True) + EPS)
        xh = xf * rs
        gnw_ref[...] += jnp.sum(dxn * xh, axis=0, keepdims=True)
        gx = dxn * nw_ref[...]
        gx_ref[...] = rs * (gx - xh * jnp.mean(gx * xh, axis=1, keepdims=True)) + dx2_ref[...]

        @pl.when(pl.program_id(0) == nt - 1)
        def _():
            for cp in scatter_copies():
                cp.wait()

    row = lambda w: pl.BlockSpec((tm, w), lambda i: (i, 0))
    return pl.pallas_call(
        body, name="bwd_dx", grid=(nt,),
        in_specs=[row(w) for w in widths] + [HBM_SPEC, row(D), pl.BlockSpec((1, D), lambda i: (0, 0)), row(D), HBM_SPEC, HBM_SPEC],
        out_specs=[row(D), pl.BlockSpec((1, D), lambda i: (0, 0)), HBM_SPEC, HBM_SPEC],
        out_shape=[jax.ShapeDtypeStruct((T, D), F32), jax.ShapeDtypeStruct((1, D), F32),
                   jax.ShapeDtypeStruct((3, D // 2, SHARD_W), BF16), jax.ShapeDtypeStruct((3, 3, SQ_ROWS, D // 2), BF16)],
        scratch_shapes=[pltpu.VMEM((D_IN, D), BF16), pltpu.SemaphoreType.DMA,
                        pltpu.SemaphoreType.DMA((6,)), pltpu.SemaphoreType.DMA((6,))],
        compiler_params=_params(("arbitrary",)),
    )(*pieces, wt_bf, x, norm_w, dx2, swin_b, ssq_b)


W_PIECES = ((0, 1024, 3), (COL_HG, 1024, 1), (COL_AQ, 1024, 1), (COL_AK, 256, 1), (COL_AV, 256, 1), (COL_AG, 512, 6))


def _wgrad_in(xnt_bf, pieces):
    T = xnt_bf.shape[1]
    bufs = ()
    for n, (piece, (col, wb, blocks)) in enumerate(zip(pieces, W_PIECES)):
        tk = min(1024 if wb == 1024 else 2048, T)
        steps = T // tk

        def body(xnt_ref, p_ref, *rest):
            g_ref, gb_ref = rest[-2:]
            part = _dot(xnt_ref[...], p_ref[...])

            @pl.when(pl.program_id(1) == 0)
            def _():
                g_ref[...] = part

            @pl.when(pl.program_id(1) > 0)
            def _():
                g_ref[...] += part

            @pl.when(pl.program_id(1) == steps - 1)
            def _():
                gb_ref[...] = _bf(g_ref[...])

        out = pl.BlockSpec((D, wb), lambda jb, i, base=col // wb: (0, base + jb))
        bufs = pl.pallas_call(
            body, name=f"wgrad_in_{n}", grid=(blocks, steps),
            in_specs=[pl.BlockSpec((D, tk), lambda jb, i: (0, i)), pl.BlockSpec((tk, wb), lambda jb, i: (i, jb))]
                     + [HBM_SPEC] * len(bufs),
            out_specs=[out, out],
            out_shape=[jax.ShapeDtypeStruct((D, D_IN), F32), jax.ShapeDtypeStruct((D, D_IN), BF16)],
            input_output_aliases={2: 0, 3: 1} if bufs else {},
            compiler_params=_params(("parallel", "arbitrary")),
        )(xnt_bf, piece, *bufs)
    return bufs


def _place():
    return lax.axis_index("x"), lax.axis_index("y"), lax.axis_index("c")


def _flip(v, f):
    return 1 - v if f else v


def _win_half(ref, h):
    return ref.at[pl.ds(h * (D // 2), D // 2), :]


def _sq_half(ref, h):
    return ref.at[:, pl.ds(h * (D // 2), D // 2)]


def _gather_copy(part, k, to, send_sems, recv_sems):
    return pltpu.make_async_remote_copy(src_ref=part, dst_ref=part, send_sem=send_sems.at[k], recv_sem=recv_sems.at[k],
                                        device_id=to, device_id_type=MESH)


def _gather_start(out, half, send_sems, recv_sems):
    x, y, c = _place()
    for k, (fx, fy) in enumerate(CHIP_FLIPS):
        _gather_copy(half(out.at[2 * x + y], c), k, (_flip(x, fx), _flip(y, fy), c), send_sems, recv_sems).start()


def _gather_land(out, half, k, send_sems, recv_sems):
    x, y, c = _place()
    sib = (x, y, 1 - c)
    fx, fy = CHIP_FLIPS[k]
    slot = out.at[2 * _flip(x, fx) + _flip(y, fy)]
    _gather_copy(half(slot, c), k, sib, send_sems, recv_sems).wait_recv()
    _gather_copy(half(slot, c), 3 + k, sib, send_sems, recv_sems).start()
    _gather_copy(half(slot, 1 - c), 3 + k, sib, send_sems, recv_sems).wait_recv()


def _gather_drain(out, half, send_sems, recv_sems):
    x, y, c = _place()
    for k, (fx, fy) in enumerate(CHIP_FLIPS):
        _gather_copy(half(out.at[2 * x + y], c), k, (_flip(x, fx), _flip(y, fy), c), send_sems, recv_sems).wait_send()
        _gather_copy(half(out.at[2 * _flip(x, fx) + _flip(y, fy)], c), 3 + k, (x, y, 1 - c), send_sems, recv_sems).wait_send()


def _gather_finish(out, half, send_sems, recv_sems):
    for k in range(len(CHIP_FLIPS)):
        _gather_land(out, half, k, send_sems, recv_sems)
    _gather_drain(out, half, send_sems, recv_sems)


def _swap_halves(gwin, gsq):
    def body(gwin_ref, gsq_ref, win_got, sq_got, send_sems, recv_sems):
        x, y, c = _place()
        sib = (x, y, 1 - c)
        pairs = ((_win_half(gwin_ref, 1 - c), win_got),
                 (gsq_ref.at[:, :, pl.ds((1 - c) * (D // 2), D // 2)], sq_got))
        copies = [pltpu.make_async_remote_copy(src_ref=src, dst_ref=dst, send_sem=send_sems.at[a], recv_sem=recv_sems.at[a],
                                               device_id=sib, device_id_type=MESH) for a, (src, dst) in enumerate(pairs)]
        for cp in copies:
            cp.start()
        for cp in copies:
            cp.wait()

    return pl.pallas_call(
        body, name="swap_halves",
        in_specs=[HBM_SPEC, HBM_SPEC], out_specs=[HBM_SPEC, HBM_SPEC],
        out_shape=[jax.ShapeDtypeStruct((D // 2, D_IN), BF16), jax.ShapeDtypeStruct((3, D, D // 2), BF16)],
        scratch_shapes=[pltpu.SemaphoreType.DMA((2,)), pltpu.SemaphoreType.DMA((2,))],
    )(gwin, gsq)


def _add_halves(c_arr, gwin, gsq, win_got, sq_got):
    def body(c_ref, a_ref, b_ref, p_ref, q_ref, so_ref, sq_ref, sob_ref, sqb_ref):
        so = a_ref[...] + b_ref[...].astype(F32)
        sq = p_ref[...] + q_ref[...].astype(F32)
        so_ref[...] = so
        sq_ref[...] = sq
        sob_ref[...] = _bf(so)
        sqb_ref[...] = _bf(sq)

    steps = 8
    rows, sq_rows = (D // 2) // steps, D // steps
    win = lambda f: pl.BlockSpec((rows, D_IN), f)
    sq = lambda f: pl.BlockSpec((3, sq_rows, D // 2), f)
    return pl.pallas_call(
        body, name="add_halves",
        grid_spec=pltpu.PrefetchScalarGridSpec(
            num_scalar_prefetch=1, grid=(steps,),
            in_specs=[win(lambda i, c: (c[0] * steps + i, 0)), win(lambda i, c: (i, 0)),
                      sq(lambda i, c: (0, i, c[0])), sq(lambda i, c: (0, i, 0))],
            out_specs=[win(lambda i, c: (i, 0)), sq(lambda i, c: (0, i, 0))] * 2),
        out_shape=[jax.ShapeDtypeStruct((D // 2, D_IN), F32), jax.ShapeDtypeStruct((3, D, D // 2), F32),
                   jax.ShapeDtypeStruct((D // 2, D_IN), BF16), jax.ShapeDtypeStruct((3, D, D // 2), BF16)],
        compiler_params=_params(("arbitrary",)),
    )(c_arr, gwin, win_got, gsq, sq_got)


def _sum_chips(jc_arr, swin, ssq, win_got, sq_got):
    def body(jc_ref, a_ref, b_ref, p_ref, q_ref, so_ref, sq_ref):
        so_ref[...] = ((a_ref[...] + b_ref[0].astype(F32)) + b_ref[1].astype(F32)) + b_ref[2].astype(F32)
        sq_ref[...] = ((p_ref[...] + q_ref[0].astype(F32)) + q_ref[1].astype(F32)) + q_ref[2].astype(F32)

    rows = 128
    steps = (D // 2) // rows
    sq_rows = SQ_ROWS // steps
    return pl.pallas_call(
        body, name="sum_chips",
        grid_spec=pltpu.PrefetchScalarGridSpec(
            num_scalar_prefetch=1, grid=(steps,),
            in_specs=[pl.BlockSpec((rows, SHARD_W), lambda i, jc: (i, jc[0])),
                      pl.BlockSpec((3, rows, SHARD_W), lambda i, jc: (0, i, 0)),
                      pl.BlockSpec((3, sq_rows, D // 2), lambda i, jc: (0, jc[0] * steps + i, 0)),
                      pl.BlockSpec((3, 3, sq_rows, D // 2), lambda i, jc: (0, 0, i, 0))],
            out_specs=[pl.BlockSpec((rows, SHARD_W), lambda i, jc: (jc[1] * steps + i, 0)),
                       pl.BlockSpec((3, sq_rows, D // 2), lambda i, jc: (0, i, jc[1]))]),
        out_shape=[jax.ShapeDtypeStruct((D, SHARD_W), F32), jax.ShapeDtypeStruct((3, SQ_ROWS, D), F32)],
        compiler_params=_params(("arbitrary",)),
    )(jc_arr, swin, win_got, ssq, sq_got)


def _join_halves(g_win, g_sq):
    def body(win_in, sq_in, win_out, sq_out, send_sems, recv_sems):
        del win_in, sq_in
        x, y, c = _place()
        sib = (x, y, 1 - c)

        def halves(h):
            return _win_half(win_out, h), sq_out.at[:, :, pl.ds(h * (D // 2), D // 2)]

        def copy(a, part):
            return pltpu.make_async_remote_copy(src_ref=part, dst_ref=part, send_sem=send_sems.at[a], recv_sem=recv_sems.at[a],
                                                device_id=sib, device_id_type=MESH)

        sent = [copy(a, part) for a, part in enumerate(halves(c))]
        for cp in sent:
            cp.start()
        for a, part in enumerate(halves(1 - c)):
            copy(a, part).wait_recv()
        for cp in sent:
            cp.wait_send()

    return pl.pallas_call(
        body, name="join_halves",
        in_specs=[HBM_SPEC, HBM_SPEC], out_specs=[HBM_SPEC, HBM_SPEC], input_output_aliases={0: 0, 1: 1},
        out_shape=[jax.ShapeDtypeStruct((D, SHARD_W), F32), jax.ShapeDtypeStruct((3, SQ_ROWS, D), F32)],
        scratch_shapes=[pltpu.SemaphoreType.DMA((2,)), pltpu.SemaphoreType.DMA((2,))],
    )(g_win, g_sq)


def _allreduce_small(vec):
    def body(vec_ref, out_ref, slots, send_sems, recv_sems):
        x, y, c = _place()
        me = 4 * x + 2 * y + c
        slots[me] = vec_ref[...]
        copies = []
        for k in range(1, 8):
            fx, fy, fc = (k >> 2) & 1, (k >> 1) & 1, k & 1
            copies.append(pltpu.make_async_remote_copy(
                src_ref=vec_ref, dst_ref=slots.at[me], send_sem=send_sems.at[k - 1], recv_sem=recv_sems.at[k - 1],
                device_id=(_flip(x, fx), _flip(y, fy), _flip(c, fc)), device_id_type=MESH))
        for cp in copies:
            cp.start()
        for k in range(1, 8):
            fx, fy, fc = (k >> 2) & 1, (k >> 1) & 1, k & 1
            src = 4 * _flip(x, fx) + 2 * _flip(y, fy) + _flip(c, fc)
            pltpu.make_async_remote_copy(src_ref=vec_ref, dst_ref=slots.at[src], send_sem=send_sems.at[k - 1],
                                         recv_sem=recv_sems.at[k - 1], device_id=(x, y, c), device_id_type=MESH).wait_recv()
        for cp in copies:
            cp.wait_send()
        total = slots[0]
        for s in range(1, 8):
            total = total + slots[s]
        out_ref[...] = total

    return pl.pallas_call(
        body, name="allreduce_small",
        in_specs=[pl.BlockSpec(memory_space=pltpu.VMEM)], out_specs=pl.BlockSpec(memory_space=pltpu.VMEM),
        out_shape=jax.ShapeDtypeStruct((8, D), F32),
        scratch_shapes=[pltpu.VMEM((8, 8, D), F32), pltpu.SemaphoreType.DMA((7,)), pltpu.SemaphoreType.DMA((7,))],
    )(vec)


def _adamw_math(w, g, m, v):
    m = ADAM_B1 * m + (1.0 - ADAM_B1) * g
    v = ADAM_B2 * v + (1.0 - ADAM_B2) * (g * g)
    m_hat = m / (1.0 - ADAM_B1 ** ADAM_STEP)
    v_hat = v / (1.0 - ADAM_B2 ** ADAM_STEP)
    delta = -ADAM_LR * (m_hat / (jnp.sqrt(v_hat) + ADAM_EPS) + ADAM_WD * w)
    return delta, m, v


def _adamw(name, w, g, m, v, rows):
    R, C = w.shape

    def body(w_ref, g_ref, m_ref, v_ref, d_out, m_out, v_out):
        d_out[...], m_out[...], v_out[...] = _adamw_math(w_ref[...], g_ref[...], m_ref[...], v_ref[...])

    spec = pl.BlockSpec((rows, C), lambda i: (i, 0))
    return pl.pallas_call(
        body, name=name, grid=(R // rows,), in_specs=[spec] * 4, out_specs=[spec] * 3,
        out_shape=[jax.ShapeDtypeStruct((R, C), F32)] * 3,
        compiler_params=_params(("parallel",)),
    )(w, g, m, v)


def _adamw_square(g_sq, ws, ms, vs):
    def body(g_ref, *refs):
        w_refs, m_refs, v_refs, outs = refs[0:3], refs[3:6], refs[6:9], refs[9:]
        for k in range(3):
            g = g_ref[k]
            outs[k][0] = g
            outs[3 + k][0], outs[6 + k][0], outs[9 + k][0] = _adamw_math(w_refs[k][0], g, m_refs[k][0], v_refs[k][0])

    out = pl.pallas_call(
        body, name="adamw_square", out_shape=[jax.ShapeDtypeStruct((1, SQ_ROWS, D), F32)] * 12,
        compiler_params=_params(),
    )(g_sq, *ws, *ms, *vs)
    return out[0:3], out[3:6], out[6:9], out[9:12]


def _small_update(total, lbw, w8, m8, v8):
    def body(t_ref, lbw_ref, w_ref, m_ref, v_ref, g_out, d_out, m_out, v_out):
        lb = 1.0 / (1.0 + jnp.exp(lbw_ref[1:2, :] - lbw_ref[0:1, :]))
        dlb = t_ref[2:3, :] * lb * (1.0 - lb)
        g_out[...] = jnp.zeros_like(g_out)
        g_out[0:1, :] = t_ref[3:4, :]
        g_out[1:2, :] = dlb
        g_out[2:3, :] = -dlb
        g_out[3:4, :] = t_ref[1:2, :]
        g_out[4:5, :] = t_ref[0:1, :]
        g_out[5:6, :] = t_ref[4:5, :]
        d_out[...], m_out[...], v_out[...] = _adamw_math(w_ref[...], g_out[...], m_ref[...], v_ref[...])

    return pl.pallas_call(
        body, name="small_update", out_shape=[jax.ShapeDtypeStruct((8, D), F32)] * 4,
        compiler_params=_params(),
    )(total, lbw, w8, m8, v8)


def _pack8(norm_w, lbw, hnw, fnw, sinks):
    pad = jnp.zeros((1, D - 16), F32)
    return jnp.concatenate([norm_w, lbw, hnw, fnw.reshape(1, D), jnp.concatenate([sinks, pad], axis=1),
                            jnp.zeros((2, D), F32)], axis=0)


def _unpack8(a):
    return a[0:1], a[1:3], a[3:4], a[5:6, 0:16], a[4]


def _local_step(order_arr, x, tgt, norm_w, lbw, hnw, sinks, fnw, win_mine, wsq_mine, exchange):
    proj, xnt_bf, win_bf = _fwd_proj(order_arr, x, norm_w, win_mine)
    oh, states, wsq_all = _hgrn_fwd(proj, lbw, wsq_mine)
    wsq_bf = wsq_all.reshape(SHARDS, 3, SQ_ROWS, D).transpose(1, 0, 2, 3).reshape(3, D, D)
    oa = _attn_fwd(proj, sinks)
    dx2, doh, doa, dhg, dtail, lhs, rhs, loss8, vec_mid = _mid(x, tgt, proj, oh, oa, hnw, fnw.reshape(1, D), wsq_bf)
    gsq, gsq_b = _wgrad_square(lhs, rhs)
    dhead, dlb = _hgrn_bwd(proj, lbw, states, doh)
    daq, dak, dav, dsink = _attn_bwd(proj, sinks, oa, doa)
    pieces = [dhead, dhg, daq, dak, dav, dtail]
    sums = exchange(*_wgrad_in(xnt_bf, pieces), gsq, gsq_b)
    wt_bf = win_bf.transpose(0, 2, 1).reshape(D_IN, D)
    grad_x, gnw, win_got, sq_got = _bwd_dx(pieces, wt_bf, x, norm_w, dx2, sums[2], sums[3])
    sink_row = jnp.concatenate([dsink[:, 0].reshape(1, 16), jnp.zeros((1, D - 16), F32)], axis=1)
    loss_row = jnp.broadcast_to(loss8[0:1, 0:1], (1, D))
    vec = jnp.concatenate([vec_mid[0:2], dlb, gnw, sink_row, loss_row, jnp.zeros((2, D), F32)], axis=0)
    return grad_x, sums, (win_got, sq_got), vec


def kernel(x, norm_w, w_in, hgrn_lower_bound, hgrn_norm_w, w_branch_hgrn, attn_sinks, w_branch_attn, w_out, final_norm_w, loss_target, m_norm_w, m_w_in, m_hgrn_lower_bound, m_hgrn_norm_w, m_w_branch_hgrn, m_attn_sinks, m_w_branch_attn, m_w_out, m_final_norm_w, v_norm_w, v_w_in, v_hgrn_lower_bound, v_hgrn_norm_w, v_w_branch_hgrn, v_attn_sinks, v_w_branch_attn, v_w_out, v_final_norm_w):
    c_arr = lax.axis_index("c").astype(jnp.int32).reshape(1)
    j_arr = (2 * lax.axis_index("x") + lax.axis_index("y")).astype(jnp.int32).reshape(1)
    jc_arr = jnp.concatenate([j_arr, c_arr])

    win_mine, wsq_mine = _cast_shards(j_arr, w_in[0], w_branch_hgrn[0], w_branch_attn[0], w_out[0])
    xi, yi = lax.axis_index("x"), lax.axis_index("y")
    order_arr = jnp.stack([2 * xi + yi] + [2 * _flip(xi, fx) + _flip(yi, fy) for fx, fy in CHIP_FLIPS]).astype(jnp.int32)

    def chip_sums(gwin, gwin_b, gsq, gsq_b):
        return _add_halves(c_arr, gwin, gsq, *_swap_halves(gwin_b, gsq_b))

    grad_x, (swin, ssq, _, _), arrived, vec = _local_step(
        order_arr, x[0], loss_target[0], norm_w, hgrn_lower_bound, hgrn_norm_w, attn_sinks, final_norm_w, win_mine, wsq_mine,
        chip_sums)
    g_win, g_sq = _join_halves(*_sum_chips(jc_arr, swin, ssq, *arrived))

    d_win, nm_win, nv_win = _adamw("adamw_w_in", w_in[0], g_win, m_w_in[0], v_w_in[0], 128)
    g_sqs, d_sqs, nm_sqs, nv_sqs = _adamw_square(
        g_sq, (w_branch_hgrn, w_branch_attn, w_out), (m_w_branch_hgrn, m_w_branch_attn, m_w_out),
        (v_w_branch_hgrn, v_w_branch_attn, v_w_out))

    total = _allreduce_small(vec)
    loss = total[5, 0]
    g8, d8, nm8, nv8 = _small_update(
        total, hgrn_lower_bound,
        _pack8(norm_w, hgrn_lower_bound, hgrn_norm_w, final_norm_w, attn_sinks),
        _pack8(m_norm_w, m_hgrn_lower_bound, m_hgrn_norm_w, m_final_norm_w, m_attn_sinks),
        _pack8(v_norm_w, v_hgrn_lower_bound, v_hgrn_norm_w, v_final_norm_w, v_attn_sinks))

    def assemble(win, sq, small):
        nw, lb, hn, sk, fn = _unpack8(small)
        return (nw, win.reshape(1, D, SHARD_W), lb, hn, sq[0], sk, sq[1], sq[2], fn)

    return (loss, grad_x.reshape(1, -1, D),
            *assemble(g_win, g_sqs, g8), *assemble(d_win, d_sqs, d8),
            *assemble(nm_win, nm_sqs, nm8), *assemble(nv_win, nv_sqs, nv8))
```

```python
import functools

import jax
import jax.numpy as jnp
from jax import lax
from jax.experimental import pallas as pl
from jax.experimental.pallas import tpu as pltpu

F32 = jnp.float32
BF16 = jnp.bfloat16

D = 1024
D_IN = 8704
SHARDS = 4
SHARD_W = D_IN // SHARDS
SQ_ROWS = D // SHARDS
HEADS = 8
HEAD_W = 128
CHUNK = 64
SUB = 4
ATT_BLOCK = 128
KV_HEADS = 4
HEAD_DIM = 64
EPS = 1e-6
NEG = -1e30
SCALE = HEAD_DIM ** -0.5
COL_HG, COL_AQ, COL_AK, COL_AV, COL_AG, COL_MH, COL_MA = 3072, 4096, 5120, 5376, 5632, 6656, 7680

ADAM_LR, ADAM_B1, ADAM_B2, ADAM_EPS, ADAM_WD, ADAM_STEP = 0.001, 0.9, 0.999, 1e-08, 0.01, 10

VMEM_LIMIT = 56 * 1024 * 1024
MESH = pl.DeviceIdType.MESH
HBM_SPEC = pl.BlockSpec(memory_space=pltpu.HBM)
CHIP_FLIPS = ((1, 0), (0, 1), (1, 1))


def _dot(a, b):
    return jnp.dot(a, b, preferred_element_type=F32)


def _dot_nt(a, b):
    return lax.dot_general(a, b, (((1,), (1,)), ((), ())), preferred_element_type=F32)


def _dot_tn(a, b):
    return lax.dot_general(a, b, (((0,), (0,)), ((), ())), preferred_element_type=F32)


def _sigmoid(v):
    return 1.0 / (1.0 + jnp.exp(-v))


def _bf(v):
    return v.astype(BF16)


def _tri_dot2(tri, v):
    a = _bf(v)
    return _dot(tri, a) + _dot(tri, _bf(v - a.astype(F32)))


def _params(sem=None):
    return pltpu.CompilerParams(dimension_semantics=sem, vmem_limit_bytes=VMEM_LIMIT)


def _cast_shards(j_arr, win_s, wbh_s, wba_s, wout_s):
    steps = 4
    rows = D // steps

    def body(j_ref, win_ref, a_ref, b_ref, c_ref, win_o, sq_o):
        win_o[...] = _bf(win_ref[...])

        @pl.when(pl.program_id(0) == 0)
        def _():
            sq_o[0:SQ_ROWS, :] = _bf(a_ref[...])
            sq_o[SQ_ROWS:2 * SQ_ROWS, :] = _bf(b_ref[...])
            sq_o[2 * SQ_ROWS:3 * SQ_ROWS, :] = _bf(c_ref[...])

    whole = pl.BlockSpec((SQ_ROWS, D), lambda i, j: (0, 0))
    return pl.pallas_call(
        body, name="cast_shards",
        grid_spec=pltpu.PrefetchScalarGridSpec(
            num_scalar_prefetch=1, grid=(steps,),
            in_specs=[pl.BlockSpec((rows, SHARD_W), lambda i, j: (i, 0)), whole, whole, whole],
            out_specs=[pl.BlockSpec((None, rows, SHARD_W), lambda i, j: (j[0], i, 0)),
                       pl.BlockSpec((None, 3 * SQ_ROWS, D), lambda i, j: (j[0], 0, 0))]),
        out_shape=[jax.ShapeDtypeStruct((SHARDS, D, SHARD_W), BF16), jax.ShapeDtypeStruct((SHARDS, 3 * SQ_ROWS, D), BF16)],
        compiler_params=_params(("arbitrary",)),
    )(j_arr, win_s, wbh_s, wba_s, wout_s)


def _fwd_proj(order_arr, x, norm_w, win_all):
    T = x.shape[0]
    tm = min(512, T)
    nt = T // tm

    def body(order_ref, x_ref, nw_ref, win_in, proj_ref, xn_ref, win_out, w_scr, xn_scr, sems, send_sems, recv_sems):
        del win_in
        p, i = pl.program_id(0), pl.program_id(1)

        def load(n):
            return pltpu.make_async_copy(win_out.at[order_ref[n]], w_scr.at[n % 2], sems.at[n % 2])

        @pl.when((p == 0) & (i == 0))
        def _():
            _gather_start(win_out, _win_half, send_sems, recv_sems)
            load(0).start()
            load(0).wait()

        @pl.when((p == 1) & (i == 0))
        def _():
            _gather_land(win_out, _win_half, 0, send_sems, recv_sems)
            load(1).start()
            load(1).wait()

        for k in range(1, SHARDS - 1):
            @pl.when((p == k) & (i == nt // 2))
            def _():
                _gather_land(win_out, _win_half, k, send_sems, recv_sems)
                load(k + 1).start()

            @pl.when((p == k + 1) & (i == 0))
            def _():
                load(k + 1).wait()

        @pl.when(p == 0)
        def _():
            xf = x_ref[...]
            rs = lax.rsqrt(jnp.mean(xf * xf, axis=1, keepdims=True) + EPS)
            xn = _bf((xf * rs) * nw_ref[...])
            xn_scr[i] = xn
            xn_ref[...] = xn.T

        proj_ref[...] = _dot(xn_scr[i], w_scr[p % 2])

        @pl.when((p == SHARDS - 1) & (i == nt - 1))
        def _():
            _gather_drain(win_out, _win_half, send_sems, recv_sems)

    first = lambda p, i: jnp.where(p == 0, i, nt - 1)
    return pl.pallas_call(
        body, name="fwd_proj",
        grid_spec=pltpu.PrefetchScalarGridSpec(
            num_scalar_prefetch=1, grid=(SHARDS, nt),
            in_specs=[pl.BlockSpec((tm, D), lambda p, i, order: (first(p, i), 0)),
                      pl.BlockSpec((1, D), lambda p, i, order: (0, 0)), HBM_SPEC],
            out_specs=[pl.BlockSpec((tm, SHARD_W), lambda p, i, order: (i, order[p])),
                       pl.BlockSpec((D, tm), lambda p, i, order: (0, first(p, i))),
                       HBM_SPEC],
            scratch_shapes=[pltpu.VMEM((2, D, SHARD_W), BF16), pltpu.VMEM((nt, tm, D), BF16), pltpu.SemaphoreType.DMA((2,)),
                            pltpu.SemaphoreType.DMA((6,)), pltpu.SemaphoreType.DMA((6,))]),
        out_shape=[jax.ShapeDtypeStruct((T, D_IN), F32), jax.ShapeDtypeStruct((D, T), BF16),
                   jax.ShapeDtypeStruct((SHARDS, D, SHARD_W), BF16)],
        input_output_aliases={3: 2},
        compiler_params=_params(("arbitrary", "arbitrary")),
    )(order_arr, x, norm_w, win_all)


def _hgrn_gates(hq_ref, hf_ref, lbw_ref, b_scr):
    lb = 1.0 / (1.0 + jnp.exp(lbw_ref[1:2, :] - lbw_ref[0:1, :]))
    hf = hf_ref[...]
    sig = _sigmoid(hf)
    f = lb + (1.0 - lb) * sig
    g = jnp.log(f)
    hq = hq_ref[...]
    sq = _sigmoid(hq)
    q = hq * sq
    row = lax.broadcasted_iota(jnp.int32, (CHUNK, CHUNK), 0)
    col = lax.broadcasted_iota(jnp.int32, (CHUNK, CHUNK), 1)
    causal = row >= col
    b = _tri_dot2(jnp.where(causal, 1.0, 0.0).astype(BF16), g)
    b_scr[...] = b
    bc = b_scr[CHUNK - 1:CHUNK, :]
    r = b_scr[CHUNK // 2 - 1:CHUNK // 2, :]
    return dict(lb=lb, sig=sig, f=f, k=1.0 - f, hq=hq, sq=sq, q=q, b=b, bc=bc, r=r, causal=causal)


def _hgrn_fwd(proj, lbw, wsq_all):
    T = proj.shape[0]
    n = T // CHUNK

    def body(hq_ref, hf_ref, hi_ref, lbw_ref, wsq_in, o_ref, st_ref, wsq_out, s_scr, b_scr, send_sems, recv_sems):
        del wsq_in

        @pl.when(pl.program_id(0) == 0)
        def _():
            _gather_start(wsq_out, _sq_half, send_sems, recv_sems)
            s_scr[...] = jnp.zeros_like(s_scr)

        for c in range(SUB):
            rows = pl.ds(c * CHUNK, CHUNK)
            gt = _hgrn_gates(hq_ref.at[rows, :], hf_ref.at[rows, :], lbw_ref, b_scr.at[rows, :])
            b, bc, r, q, k = gt["b"], gt["bc"], gt["r"], gt["q"], gt["k"]
            qe = _bf(q * jnp.exp(b))
            qr = _bf(q * jnp.exp(b - r))
            kr = _bf(k * jnp.exp(r - b))
            kl = _bf(k * jnp.exp(bc - b))
            ebc = jnp.exp(bc)
            v = _bf(hi_ref[rows, :])
            scores = [_bf(jnp.where(gt["causal"], _dot_nt(qr[:, h * HEAD_W:(h + 1) * HEAD_W], kr[:, h * HEAD_W:(h + 1) * HEAD_W]), 0.0))
                      for h in range(HEADS)]
            for h in range(HEADS):
                sl = slice(h * HEAD_W, (h + 1) * HEAD_W)
                st = s_scr[h]
                st_ref[c, h] = st
                o_ref[rows, sl] = _dot(scores[h], v[:, sl]) + _dot_nt(qe[:, sl], _bf(st))
                s_scr[h] = ebc[:, sl] * st + _dot_tn(v[:, sl], kl[:, sl])

        @pl.when(pl.program_id(0) == n // SUB - 1)
        def _():
            _gather_finish(wsq_out, _sq_half, send_sems, recv_sems)

    col = lambda j: pl.BlockSpec((SUB * CHUNK, D), lambda i: (i, j))
    return pl.pallas_call(
        body, name="hgrn_fwd", grid=(n // SUB,),
        in_specs=[col(0), col(1), col(2), pl.BlockSpec((2, D), lambda i: (0, 0)), HBM_SPEC],
        out_specs=[pl.BlockSpec((SUB * CHUNK, D), lambda i: (i, 0)),
                   pl.BlockSpec((SUB, HEADS, HEAD_W, HEAD_W), lambda i: (i, 0, 0, 0)), HBM_SPEC],
        out_shape=[jax.ShapeDtypeStruct((T, D), F32), jax.ShapeDtypeStruct((n, HEADS, HEAD_W, HEAD_W), F32),
                   jax.ShapeDtypeStruct((SHARDS, 3 * SQ_ROWS, D), BF16)],
        input_output_aliases={4: 2},
        scratch_shapes=[pltpu.VMEM((HEADS, HEAD_W, HEAD_W), F32), pltpu.VMEM((SUB * CHUNK, D), F32),
                        pltpu.SemaphoreType.DMA((6,)), pltpu.SemaphoreType.DMA((6,))],
        compiler_params=_params(("arbitrary",)),
    )(proj, proj, proj, lbw, wsq_all)


def _hgrn_bwd(proj, lbw, states, do):
    T = proj.shape[0]
    n = T // CHUNK

    def body(hq_ref, hf_ref, hi_ref, lbw_ref, st_ref, do_ref, dp_ref, dlb_ref,
             ds_scr, b_scr, dq_scr, dk_scr, dv_scr, late_scr, early_scr, ex_scr):
        @pl.when(pl.program_id(0) == 0)
        def _():
            ds_scr[...] = jnp.zeros_like(ds_scr)
            dlb_ref[...] = jnp.zeros_like(dlb_ref)

        for c in reversed(range(SUB)):
            rows = pl.ds(c * CHUNK, CHUNK)
            gt = _hgrn_gates(hq_ref.at[rows, :], hf_ref.at[rows, :], lbw_ref, b_scr.at[rows, :])
            b, bc, r, q, k = gt["b"], gt["bc"], gt["r"], gt["q"], gt["k"]
            eb = jnp.exp(b)
            er = jnp.exp(b - r)
            erk = jnp.exp(r - b)
            el = jnp.exp(bc - b)
            ebc = jnp.exp(bc)
            qe, qr, kr, kl = _bf(q * eb), _bf(q * er), _bf(k * erk), _bf(k * el)
            v = _bf(hi_ref[rows, :])
            do_b = do_ref[rows, :]
            do_t = do_b.T
            causal_t = lax.broadcasted_iota(jnp.int32, (CHUNK, CHUNK), 0) <= lax.broadcasted_iota(jnp.int32, (CHUNK, CHUNK), 1)
            firsts = []
            for h in range(HEADS):
                sl = slice(h * HEAD_W, (h + 1) * HEAD_W)
                firsts.append((_bf(jnp.where(causal_t, _dot_nt(kr[:, sl], qr[:, sl]), 0.0)),
                               _bf(jnp.where(gt["causal"], _dot_nt(do_b[:, sl], v[:, sl]), 0.0)),
                               _bf(jnp.where(causal_t, _dot_nt(v[:, sl], do_b[:, sl]), 0.0))))
            for h in range(HEADS):
                sl = slice(h * HEAD_W, (h + 1) * HEAD_W)
                st0 = st_ref[c, h]
                dst = ds_scr[h]
                dst_b = _bf(dst)
                a_t, da, da_t = firsts[h]
                mq = _dot(da, kr[:, sl])
                mk = _dot(da_t, qr[:, sl])
                dq_in = eb[:, sl] * _dot(do_b[:, sl], _bf(st0))
                dk_in = el[:, sl] * _dot(v[:, sl], dst_b)
                dq_scr[rows, sl] = er[:, sl] * mq + dq_in
                dk_scr[rows, sl] = erk[:, sl] * mk + dk_in
                dv_scr[rows, sl] = _dot(a_t, do_b[:, sl]) + _dot_nt(kl[:, sl], dst_b)
                late_scr[rows, sl] = q[:, sl] * dq_in + qr[:, sl].astype(F32) * mq - kr[:, sl].astype(F32) * mk
                early_scr[rows, sl] = k[:, sl] * dk_in
                ex_scr[:, sl] = jnp.sum(dst * st0, axis=0, keepdims=True)
                ds_scr[h] = ebc[:, sl] * dst + _dot(do_t[sl, :], qe[:, sl])

            dq, dk = dq_scr[rows, :], dk_scr[rows, :]
            row = lax.broadcasted_iota(jnp.int32, (CHUNK, CHUNK), 0)
            col = lax.broadcasted_iota(jnp.int32, (CHUNK, CHUNK), 1)
            at_or_after = jnp.where(col >= row, 1.0, 0.0).astype(BF16)
            before = jnp.where(col < row, 1.0, 0.0).astype(BF16)
            dg = _tri_dot2(jnp.concatenate([at_or_after, before], axis=1),
                           jnp.concatenate([late_scr[rows, :], early_scr[rows, :]], axis=0)) + ebc * ex_scr[...]
            df = dg / gt["f"] - dk
            sig, sq, hq, lb = gt["sig"], gt["sq"], gt["hq"], gt["lb"]
            dp_ref[rows, 0:D] = _bf(dq * (sq * (1.0 + hq * (1.0 - sq))))
            dp_ref[rows, D:2 * D] = _bf(df * (1.0 - lb) * sig * (1.0 - sig))
            dp_ref[rows, 2 * D:3 * D] = _bf(dv_scr[rows, :])
            dlb_ref[...] += jnp.sum(df * (1.0 - sig), axis=0, keepdims=True)

    ns = n // SUB
    col = lambda j: pl.BlockSpec((SUB * CHUNK, D), lambda i: (ns - 1 - i, j))
    return pl.pallas_call(
        body, name="hgrn_bwd", grid=(ns,),
        in_specs=[col(0), col(1), col(2), pl.BlockSpec((2, D), lambda i: (0, 0)),
                  pl.BlockSpec((SUB, HEADS, HEAD_W, HEAD_W), lambda i: (ns - 1 - i, 0, 0, 0)),
                  pl.BlockSpec((SUB * CHUNK, D), lambda i: (ns - 1 - i, 0))],
        out_specs=[pl.BlockSpec((SUB * CHUNK, 3 * D), lambda i: (ns - 1 - i, 0)),
                   pl.BlockSpec((1, D), lambda i: (0, 0))],
        out_shape=[jax.ShapeDtypeStruct((T, 3 * D), BF16), jax.ShapeDtypeStruct((1, D), F32)],
        scratch_shapes=[pltpu.VMEM((HEADS, HEAD_W, HEAD_W), F32)] + [pltpu.VMEM((SUB * CHUNK, D), F32)] * 6
                       + [pltpu.VMEM((1, D), F32)],
        compiler_params=_params(("arbitrary",)),
    )(proj, proj, proj, lbw, states, do)


def _attn_masks(blk):
    qi = lax.broadcasted_iota(jnp.int32, (ATT_BLOCK, 2 * ATT_BLOCK), 0)
    kj = lax.broadcasted_iota(jnp.int32, (ATT_BLOCK, 2 * ATT_BLOCK), 1)
    band = (kj > qi) & (kj <= qi + ATT_BLOCK)
    return band & ((blk > 0) | (kj >= ATT_BLOCK))


def _head_pair_operand(t, hp, low):
    mine = low if hp == 0 else jnp.logical_not(low)
    both = jnp.where(mine, t, pltpu.roll(t, HEAD_DIM, 1))
    return _bf(jnp.concatenate([jnp.where(low, both, 0.0), jnp.where(low, 0.0, both)], axis=0))


def _attn_probs(s, sink, valid):
    s = jnp.where(valid, s, NEG)
    m = jnp.maximum(jnp.max(s, axis=1, keepdims=True), sink)
    p = jnp.exp(s - m)
    es = jnp.exp(sink - m)
    inv = 1.0 / (jnp.sum(p, axis=1, keepdims=True) + es)
    return p * inv, es * inv


def _attn_fwd(proj, sinks):
    T = proj.shape[0]
    nb = T // ATT_BLOCK
    W2 = 2 * ATT_BLOCK

    def body(sink_ref, q_ref, kp_ref, kc_ref, vp_ref, vc_ref, o_ref):
        blk = pl.program_id(0)
        valid = _attn_masks(blk)
        low = lax.broadcasted_iota(jnp.int32, (1, 2 * HEAD_DIM), 1) < HEAD_DIM
        kcat = jnp.concatenate([kp_ref[...], kc_ref[...]], axis=0)
        vcat = jnp.concatenate([vp_ref[...], vc_ref[...]], axis=0)
        for h in range(KV_HEADS):
            tl = slice((h // 2) * 128, (h // 2) * 128 + 128)
            mine = low if h % 2 == 0 else jnp.logical_not(low)
            kh = _bf(jnp.where(mine, kcat[:, tl], pltpu.roll(kcat[:, tl], HEAD_DIM, 1)))
            vh = _bf(jnp.where(mine, vcat[:, tl], pltpu.roll(vcat[:, tl], HEAD_DIM, 1)))
            for t in range(2):
                ql = slice((2 * h + t) * 128, (2 * h + t) * 128 + 128)
                q2 = q_ref[:, ql] * SCALE
                outs = []
                for p in range(2):
                    qm = _bf(jnp.where(low if p == 0 else jnp.logical_not(low), q2, 0.0))
                    probs, _ = _attn_probs(_dot_nt(qm, kh), sink_ref[0, 4 * h + 2 * t + p], valid)
                    outs.append(_dot(_bf(probs), vh))
                o_ref[:, ql] = jnp.where(low, outs[0], outs[1])

    prev = lambda i: jnp.maximum(i - 1, 0)
    return pl.pallas_call(
        body, name="attn_fwd", grid=(nb,),
        in_specs=[pl.BlockSpec(memory_space=pltpu.SMEM),
                  pl.BlockSpec((ATT_BLOCK, D), lambda i: (i, COL_AQ // D)),
                  pl.BlockSpec((ATT_BLOCK, 256), lambda i: (prev(i), COL_AK // 256)),
                  pl.BlockSpec((ATT_BLOCK, 256), lambda i: (i, COL_AK // 256)),
                  pl.BlockSpec((ATT_BLOCK, 256), lambda i: (prev(i), COL_AV // 256)),
                  pl.BlockSpec((ATT_BLOCK, 256), lambda i: (i, COL_AV // 256))],
        out_specs=pl.BlockSpec((ATT_BLOCK, D), lambda i: (i, 0)),
        out_shape=jax.ShapeDtypeStruct((T, D), F32),
        compiler_params=_params(("arbitrary",)),
    )(sinks, proj, proj, proj, proj, proj)


def _attn_bwd(proj, sinks, o, do):
    T = proj.shape[0]
    nb = T // ATT_BLOCK
    W2 = 2 * ATT_BLOCK

    def body(sink_ref, q_ref, kp_ref, kc_ref, vp_ref, vc_ref, o_ref, do_ref,
             dq_ref, dkv_ref, dsink_ref, ck_scr, cv_scr, nk_scr, nv_scr):
        blk = pl.program_id(0)

        @pl.when(blk == 0)
        def _():
            ck_scr[...] = jnp.zeros_like(ck_scr)
            cv_scr[...] = jnp.zeros_like(cv_scr)
            dsink_ref[...] = jnp.zeros_like(dsink_ref)

        @pl.when(blk < nb)
        def _():
            valid = _attn_masks(blk)
            low = lax.broadcasted_iota(jnp.int32, (1, 2 * HEAD_DIM), 1) < HEAD_DIM
            kcat = jnp.concatenate([kp_ref[...], kc_ref[...]], axis=0)
            vcat = jnp.concatenate([vp_ref[...], vc_ref[...]], axis=0)
            for h in range(KV_HEADS):
                tl = slice((h // 2) * 128, (h // 2) * 128 + 128)
                kbd = _head_pair_operand(kcat[:, tl], h % 2, low)
                vbd = _head_pair_operand(vcat[:, tl], h % 2, low)
                dkbd = jnp.zeros((2 * W2, 128), F32)
                dvbd = jnp.zeros((2 * W2, 128), F32)
                tiles = []
                for t in range(2):
                    ql = slice((2 * h + t) * 128, (2 * h + t) * 128 + 128)
                    q2 = _bf(q_ref[:, ql] * SCALE)
                    do2_b = do_ref[:, ql]
                    doo = do2_b.astype(F32) * o_ref[:, ql]
                    dsum0 = jnp.sum(jnp.where(low, doo, 0.0), axis=1, keepdims=True)
                    dsum1 = jnp.sum(jnp.where(low, 0.0, doo), axis=1, keepdims=True)
                    tiles.append((ql, q2, do2_b, dsum0, dsum1, _dot_nt(q2, kbd), _dot_nt(do2_b, vbd)))
                grads = []
                for t, (ql, q2, do2_b, dsum0, dsum1, s2, dp2) in enumerate(tiles):
                    head = 4 * h + 2 * t
                    p0, ps0 = _attn_probs(s2[:, 0:W2], sink_ref[0, head], valid)
                    p1, ps1 = _attn_probs(s2[:, W2:2 * W2], sink_ref[0, head + 1], valid)
                    ds2 = _bf(jnp.concatenate([p0 * (dp2[:, 0:W2] - dsum0), p1 * (dp2[:, W2:2 * W2] - dsum1)], axis=1))
                    grads.append((ds2, _bf(jnp.concatenate([p0, p1], axis=1))))
                    dsink_ref[head:head + 1, :] += jnp.zeros((1, 128), F32) - jnp.sum(ps0 * dsum0, axis=0, keepdims=True)
                    dsink_ref[head + 1:head + 2, :] += jnp.zeros((1, 128), F32) - jnp.sum(ps1 * dsum1, axis=0, keepdims=True)
                for (ql, q2, do2_b, _, _, _, _), (ds2, p2) in zip(tiles, grads):
                    dq_ref[:, ql] = _bf(_dot(ds2, kbd) * SCALE)
                    dkbd = dkbd + _dot_tn(ds2, q2)
                    dvbd = dvbd + _dot_tn(p2, do2_b)
                dk2 = jnp.where(low, dkbd[0:W2], dkbd[W2:2 * W2])
                dv2 = jnp.where(low, dvbd[0:W2], dvbd[W2:2 * W2])
                dk2 = dk2 + pltpu.roll(dk2, HEAD_DIM, 1)
                dv2 = dv2 + pltpu.roll(dv2, HEAD_DIM, 1)
                if h % 2 == 0:
                    keep_k, keep_v = dk2, dv2
                else:
                    nk_scr[:, tl] = jnp.where(low, keep_k, dk2)
                    nv_scr[:, tl] = jnp.where(low, keep_v, dv2)
            dkv_ref[:, 0:256] = _bf(ck_scr[...] + nk_scr[0:ATT_BLOCK, :])
            dkv_ref[:, 256:512] = _bf(cv_scr[...] + nv_scr[0:ATT_BLOCK, :])
            ck_scr[...] = nk_scr[ATT_BLOCK:2 * ATT_BLOCK, :]
            cv_scr[...] = nv_scr[ATT_BLOCK:2 * ATT_BLOCK, :]

        @pl.when(blk == nb)
        def _():
            dkv_ref[:, 0:256] = _bf(ck_scr[...])
            dkv_ref[:, 256:512] = _bf(cv_scr[...])

    cur = lambda i: jnp.minimum(i, nb - 1)
    prev = lambda i: jnp.maximum(cur(i) - 1, 0)
    late = lambda i: jnp.maximum(i - 1, 0)
    return pl.pallas_call(
        body, name="attn_bwd", grid=(nb + 1,),
        in_specs=[pl.BlockSpec(memory_space=pltpu.SMEM),
                  pl.BlockSpec((ATT_BLOCK, D), lambda i: (cur(i), COL_AQ // D)),
                  pl.BlockSpec((ATT_BLOCK, 256), lambda i: (prev(i), COL_AK // 256)),
                  pl.BlockSpec((ATT_BLOCK, 256), lambda i: (cur(i), COL_AK // 256)),
                  pl.BlockSpec((ATT_BLOCK, 256), lambda i: (prev(i), COL_AV // 256)),
                  pl.BlockSpec((ATT_BLOCK, 256), lambda i: (cur(i), COL_AV // 256)),
                  pl.BlockSpec((ATT_BLOCK, D), lambda i: (cur(i), 0)),
                  pl.BlockSpec((ATT_BLOCK, D), lambda i: (cur(i), 0))],
        out_specs=[pl.BlockSpec((ATT_BLOCK, D), lambda i: (cur(i), 0)),
                   pl.BlockSpec((ATT_BLOCK, 512), lambda i: (late(i), 0)),
                   pl.BlockSpec((16, 128), lambda i: (0, 0))],
        out_shape=[jax.ShapeDtypeStruct((T, D), BF16), jax.ShapeDtypeStruct((T, 512), BF16),
                   jax.ShapeDtypeStruct((16, 128), F32)],
        scratch_shapes=[pltpu.VMEM((ATT_BLOCK, 256), F32), pltpu.VMEM((ATT_BLOCK, 256), F32),
                        pltpu.VMEM((2 * ATT_BLOCK, 256), F32), pltpu.VMEM((2 * ATT_BLOCK, 256), F32)],
        compiler_params=_params(("arbitrary",)),
    )(sinks, proj, proj, proj, proj, proj, o, do)


def _mid(x, tgt, proj, oh, oa, hnw, fnw, wsq_bf):
    T = x.shape[0]
    tm = min(256, T)
    nt = T // tm

    def body(x_ref, tgt_ref, oh_ref, oa_ref, hg_ref, ag0_ref, ag1_ref, mh0_ref, mh1_ref, ma0_ref, ma1_ref,
             hnw_ref, fnw_ref, w_hbm,
             dx2_ref, doh_ref, doa_ref, dhg_ref, dtail_ref, lhs_ref, rhs_ref, loss_ref, vec_ref,
             w_scr, xh_scr, rs_scr, sem):
        @pl.when(pl.program_id(0) == 0)
        def _():
            cp = pltpu.make_async_copy(w_hbm, w_scr, sem)
            cp.start()
            cp.wait()
            loss_ref[...] = jnp.zeros_like(loss_ref)
            vec_ref[...] = jnp.zeros_like(vec_ref)

        oh = oh_ref[...]
        for h in range(HEADS):
            sl = slice(h * HEAD_W, (h + 1) * HEAD_W)
            ohh = oh[:, sl]
            rs = lax.rsqrt(jnp.mean(ohh * ohh, axis=1, keepdims=True) + EPS)
            xh_scr[:, sl] = ohh * rs
            rs_scr[:, sl] = jnp.broadcast_to(rs, (tm, HEAD_W))
        xh = xh_scr[...]
        hnw = hnw_ref[...]
        on = xh * hnw
        hg = hg_ref[...]
        sg = _sigmoid(hg)
        silu_g = hg * sg
        gated_h = _bf(on * silu_g)
        oa = oa_ref[...]
        ag = jnp.concatenate([ag0_ref[...], ag1_ref[...]], axis=1)
        sa = _sigmoid(ag)
        silu_a = ag * sa
        gated_a = _bf(oa * silu_a)
        yh = _dot(gated_h, w_scr[0])
        ya = _dot(gated_a, w_scr[1])
        lhs_ref[0] = gated_h.T
        lhs_ref[1] = gated_a.T
        smh = _sigmoid(jnp.concatenate([mh0_ref[...], mh1_ref[...]], axis=1))
        sma = _sigmoid(jnp.concatenate([ma0_ref[...], ma1_ref[...]], axis=1))
        merged = _bf(smh * yh + sma * ya)
        lhs_ref[2] = merged.T
        x2 = x_ref[...] + _dot(merged, w_scr[2])
        rs2 = lax.rsqrt(jnp.mean(x2 * x2, axis=1, keepdims=True) + EPS)
        xh2 = x2 * rs2
        fnw = fnw_ref[...]
        diff = xh2 * fnw - tgt_ref[...]
        loss_ref[...] += jnp.zeros_like(loss_ref) + jnp.sum(diff * diff) * (0.5 / D)

        dy = diff * (1.0 / D)
        vec_ref[0:1, :] += jnp.sum(dy * xh2, axis=0, keepdims=True)
        gy = dy * fnw
        dx2 = rs2 * (gy - xh2 * jnp.mean(gy * xh2, axis=1, keepdims=True))
        dx2_ref[...] = dx2
        dx2_b = _bf(dx2)
        rhs_ref[2] = dx2_b
        dmerged = _dot_nt(dx2_b, w_scr[2])
        dyh = dmerged * smh
        dya = dmerged * sma
        dtail_ref[:, D:2 * D] = _bf(dyh * yh * (1.0 - smh))
        dtail_ref[:, 2 * D:3 * D] = _bf(dya * ya * (1.0 - sma))
        dyh_b, dya_b = _bf(dyh), _bf(dya)
        rhs_ref[0] = dyh_b
        rhs_ref[1] = dya_b
        dgh = _dot_nt(dyh_b, w_scr[0])
        dga = _dot_nt(dya_b, w_scr[1])
        don = dgh * silu_g
        dhg_ref[...] = _bf(dgh * on * (sg * (1.0 + hg * (1.0 - sg))))
        vec_ref[1:2, :] += jnp.sum(don * xh, axis=0, keepdims=True)
        gxh = don * hnw
        rsb = rs_scr[...]
        for h in range(HEADS):
            sl = slice(h * HEAD_W, (h + 1) * HEAD_W)
            gh, xhh = gxh[:, sl], xh[:, sl]
            doh_ref[:, sl] = _bf(rsb[:, sl] * (gh - xhh * jnp.mean(gh * xhh, axis=1, keepdims=True)))
        doa_ref[...] = _bf(dga * silu_a)
        dtail_ref[:, 0:D] = _bf(dga * oa * (sa * (1.0 + ag * (1.0 - sa))))

    row = lambda w, j: pl.BlockSpec((tm, w), lambda i: (i, j))
    const = lambda r, c: pl.BlockSpec((r, c), lambda i: (0, 0))
    stack = pl.BlockSpec((3, tm, D), lambda i: (0, i, 0))
    stack_t = pl.BlockSpec((3, D, tm), lambda i: (0, 0, i))
    return pl.pallas_call(
        body, name="mid", grid=(nt,),
        in_specs=[row(D, 0), row(D, 0), row(D, 0), row(D, 0), row(D, COL_HG // D),
                  row(512, COL_AG // 512), row(512, COL_AG // 512 + 1),
                  row(512, COL_MH // 512), row(512, COL_MH // 512 + 1),
                  row(512, COL_MA // 512), row(512, COL_MA // 512 + 1),
                  const(1, D), const(1, D), HBM_SPEC],
        out_specs=[row(D, 0), row(D, 0), row(D, 0), row(D, 0), row(3 * D, 0), stack_t, stack, const(8, 128), const(8, D)],
        out_shape=[jax.ShapeDtypeStruct((T, D), F32), jax.ShapeDtypeStruct((T, D), BF16), jax.ShapeDtypeStruct((T, D), BF16),
                   jax.ShapeDtypeStruct((T, D), BF16), jax.ShapeDtypeStruct((T, 3 * D), BF16),
                   jax.ShapeDtypeStruct((3, D, T), BF16), jax.ShapeDtypeStruct((3, T, D), BF16),
                   jax.ShapeDtypeStruct((8, 128), F32), jax.ShapeDtypeStruct((8, D), F32)],
        scratch_shapes=[pltpu.VMEM((3, D, D), BF16), pltpu.VMEM((tm, D), F32), pltpu.VMEM((tm, D), F32),
                        pltpu.SemaphoreType.DMA],
        compiler_params=_params(("arbitrary",)),
    )(x, tgt, oh, oa, proj, proj, proj, proj, proj, proj, proj, hnw, fnw, wsq_bf)


def _wgrad_square(lhs_t, rhs):
    T = rhs.shape[1]
    tk = min(1024, T)
    steps = T // tk

    def body(a_ref, b_ref, g_ref, gb_ref):
        part = _dot(a_ref[...], b_ref[...])

        @pl.when(pl.program_id(1) == 0)
        def _():
            g_ref[...] = part

        @pl.when(pl.program_id(1) > 0)
        def _():
            g_ref[...] += part

        @pl.when(pl.program_id(1) == steps - 1)
        def _():
            gb_ref[...] = _bf(g_ref[...])

    return pl.pallas_call(
        body, name="wgrad_square", grid=(3, steps),
        in_specs=[pl.BlockSpec((None, D, tk), lambda k, i: (k, 0, i)), pl.BlockSpec((None, tk, D), lambda k, i: (k, i, 0))],
        out_specs=[pl.BlockSpec((None, D, D), lambda k, i: (k, 0, 0))] * 2,
        out_shape=[jax.ShapeDtypeStruct((3, D, D), F32), jax.ShapeDtypeStruct((3, D, D), BF16)],
        compiler_params=_params(("parallel", "arbitrary")),
    )(lhs_t, rhs)


def _bwd_dx(pieces, wt_bf, x, norm_w, dx2, swin_b, ssq_b):
    T = x.shape[0]
    tm = min(256, T)
    nt = T // tm
    widths = [p.shape[1] for p in pieces]
    n_p = len(pieces)

    def body(*refs):
        piece_refs = refs[:n_p]
        (w_hbm, x_ref, nw_ref, dx2_ref, swin_ref, ssq_ref,
         gx_ref, gnw_ref, win_got, sq_got, w_scr, sem, send_sems, recv_sems) = refs[n_p:]

        def scatter_copies():
            x_, y_, c_ = _place()
            copies = []
            for k, (fx, fy) in enumerate(CHIP_FLIPS):
                px, py = _flip(x_, fx), _flip(y_, fy)
                jr = 2 * px + py
                for a, (src, dst) in enumerate(((swin_ref.at[:, pl.ds(jr * SHARD_W, SHARD_W)], win_got.at[k]),
                                                (ssq_ref.at[:, pl.ds(jr * SQ_ROWS, SQ_ROWS), :], sq_got.at[k]))):
                    copies.append(pltpu.make_async_remote_copy(
                        src_ref=src, dst_ref=dst, send_sem=send_sems.at[2 * k + a], recv_sem=recv_sems.at[2 * k + a],
                        device_id=(px, py, c_), device_id_type=MESH))
            return copies

        @pl.when(pl.program_id(0) == 0)
        def _():
            for cp in scatter_copies():
                cp.start()
            cp = pltpu.make_async_copy(w_hbm, w_scr, sem)
            cp.start()
            cp.wait()
            gnw_ref[...] = jnp.zeros_like(gnw_ref)

        dxn = None
        off = 0
        for ref, w in zip(piece_refs, widths):
            part = _dot(ref[...], w_scr[off:off + w, :])
            dxn = part if dxn is None else dxn + part
            off += w
        xf = x_ref[...]
        rs = lax.rsqrt(jnp.mean(xf * xf, axis=1, keepdims=True) + EPS)
        xh = xf * rs
        gnw_ref[...] += jnp.sum(dxn * xh, axis=0, keepdims=True)
        gx = dxn * nw_ref[...]
        gx_ref[...] = rs * (gx - xh * jnp.mean(gx * xh, axis=1, keepdims=True)) + dx2_ref[...]

        @pl.when(pl.program_id(0) == nt - 1)
        def _():
            for cp in scatter_copies():
                cp.wait()

    row = lambda w: pl.BlockSpec((tm, w), lambda i: (i, 0))
    return pl.pallas_call(
        body, name="bwd_dx", grid=(nt,),
        in_specs=[row(w) for w in widths] + [HBM_SPEC, row(D), pl.BlockSpec((1, D), lambda i: (0, 0)), row(D), HBM_SPEC, HBM_SPEC],
        out_specs=[row(D), pl.BlockSpec((1, D), lambda i: (0, 0)), HBM_SPEC, HBM_SPEC],
        out_shape=[jax.ShapeDtypeStruct((T, D), F32), jax.ShapeDtypeStruct((1, D), F32),
                   jax.ShapeDtypeStruct((3, D // 2, SHARD_W), BF16), jax.ShapeDtypeStruct((3, 3, SQ_ROWS, D // 2), BF16)],
        scratch_shapes=[pltpu.VMEM((D_IN, D), BF16), pltpu.SemaphoreType.DMA,
                        pltpu.SemaphoreType.DMA((6,)), pltpu.SemaphoreType.DMA((6,))],
        compiler_params=_params(("arbitrary",)),
    )(*pieces, wt_bf, x, norm_w, dx2, swin_b, ssq_b)


W_PIECES = ((0, 1024, 3), (COL_HG, 1024, 1), (COL_AQ, 1024, 1), (COL_AK, 512, 1), (COL_AG, 512, 6))


def _wgrad_in(xnt_bf, pieces):
    T = xnt_bf.shape[1]
    bufs = ()
    for n, (piece, (col, wb, blocks)) in enumerate(zip(pieces, W_PIECES)):
        tk = min(2048, T)
        steps = T // tk

        def body(xnt_ref, p_ref, *rest):
            g_ref, gb_ref = rest[-2:]
            part = _dot(xnt_ref[...], p_ref[...])

            @pl.when(pl.program_id(1) == 0)
            def _():
                g_ref[...] = part

            @pl.when(pl.program_id(1) > 0)
            def _():
                g_ref[...] += part

            @pl.when(pl.program_id(1) == steps - 1)
            def _():
                gb_ref[...] = _bf(g_ref[...])

        out = pl.BlockSpec((D, wb), lambda jb, i, base=col // wb: (0, base + jb))
        bufs = pl.pallas_call(
            body, name=f"wgrad_in_{n}", grid=(blocks, steps),
            in_specs=[pl.BlockSpec((D, tk), lambda jb, i: (0, i)), pl.BlockSpec((tk, wb), lambda jb, i: (i, jb))]
                     + [HBM_SPEC] * len(bufs),
            out_specs=[out, out],
            out_shape=[jax.ShapeDtypeStruct((D, D_IN), F32), jax.ShapeDtypeStruct((D, D_IN), BF16)],
            input_output_aliases={2: 0, 3: 1} if bufs else {},
            compiler_params=_params(("parallel", "arbitrary")),
        )(xnt_bf, piece, *bufs)
    return bufs


def _place():
    return lax.axis_index("x"), lax.axis_index("y"), lax.axis_index("c")


def _flip(v, f):
    return 1 - v if f else v


def _win_half(ref, h):
    return ref.at[pl.ds(h * (D // 2), D // 2), :]


def _sq_half(ref, h):
    return ref.at[:, pl.ds(h * (D // 2), D // 2)]


def _gather_copy(part, k, to, send_sems, recv_sems):
    return pltpu.make_async_remote_copy(src_ref=part, dst_ref=part, send_sem=send_sems.at[k], recv_sem=recv_sems.at[k],
                                        device_id=to, device_id_type=MESH)


def _gather_start(out, half, send_sems, recv_sems):
    x, y, c = _place()
    for k, (fx, fy) in enumerate(CHIP_FLIPS):
        _gather_copy(half(out.at[2 * x + y], c), k, (_flip(x, fx), _flip(y, fy), c), send_sems, recv_sems).start()


def _gather_land(out, half, k, send_sems, recv_sems):
    x, y, c = _place()
    sib = (x, y, 1 - c)
    fx, fy = CHIP_FLIPS[k]
    slot = out.at[2 * _flip(x, fx) + _flip(y, fy)]
    _gather_copy(half(slot, c), k, sib, send_sems, recv_sems).wait_recv()
    _gather_copy(half(slot, c), 3 + k, sib, send_sems, recv_sems).start()
    _gather_copy(half(slot, 1 - c), 3 + k, sib, send_sems, recv_sems).wait_recv()


def _gather_drain(out, half, send_sems, recv_sems):
    x, y, c = _place()
    for k, (fx, fy) in enumerate(CHIP_FLIPS):
        _gather_copy(half(out.at[2 * x + y], c), k, (_flip(x, fx), _flip(y, fy), c), send_sems, recv_sems).wait_send()
        _gather_copy(half(out.at[2 * _flip(x, fx) + _flip(y, fy)], c), 3 + k, (x, y, 1 - c), send_sems, recv_sems).wait_send()


def _gather_finish(out, half, send_sems, recv_sems):
    for k in range(len(CHIP_FLIPS)):
        _gather_land(out, half, k, send_sems, recv_sems)
    _gather_drain(out, half, send_sems, recv_sems)


def _swap_halves(gwin, gsq):
    def body(gwin_ref, gsq_ref, win_got, sq_got, send_sems, recv_sems):
        x, y, c = _place()
        sib = (x, y, 1 - c)
        pairs = ((_win_half(gwin_ref, 1 - c), win_got),
                 (gsq_ref.at[:, :, pl.ds((1 - c) * (D // 2), D // 2)], sq_got))
        copies = [pltpu.make_async_remote_copy(src_ref=src, dst_ref=dst, send_sem=send_sems.at[a], recv_sem=recv_sems.at[a],
                                               device_id=sib, device_id_type=MESH) for a, (src, dst) in enumerate(pairs)]
        for cp in copies:
            cp.start()
        for cp in copies:
            cp.wait()

    return pl.pallas_call(
        body, name="swap_halves",
        in_specs=[HBM_SPEC, HBM_SPEC], out_specs=[HBM_SPEC, HBM_SPEC],
        out_shape=[jax.ShapeDtypeStruct((D // 2, D_IN), BF16), jax.ShapeDtypeStruct((3, D, D // 2), BF16)],
        scratch_shapes=[pltpu.SemaphoreType.DMA((2,)), pltpu.SemaphoreType.DMA((2,))],
    )(gwin, gsq)


def _add_halves(c_arr, gwin, gsq, win_got, sq_got):
    def body(c_ref, a_ref, b_ref, p_ref, q_ref, so_ref, sq_ref, sob_ref, sqb_ref):
        so = a_ref[...] + b_ref[...].astype(F32)
        sq = p_ref[...] + q_ref[...].astype(F32)
        so_ref[...] = so
        sq_ref[...] = sq
        sob_ref[...] = _bf(so)
        sqb_ref[...] = _bf(sq)

    steps = 8
    rows, sq_rows = (D // 2) // steps, D // steps
    win = lambda f: pl.BlockSpec((rows, D_IN), f)
    sq = lambda f: pl.BlockSpec((3, sq_rows, D // 2), f)
    return pl.pallas_call(
        body, name="add_halves",
        grid_spec=pltpu.PrefetchScalarGridSpec(
            num_scalar_prefetch=1, grid=(steps,),
            in_specs=[win(lambda i, c: (c[0] * steps + i, 0)), win(lambda i, c: (i, 0)),
                      sq(lambda i, c: (0, i, c[0])), sq(lambda i, c: (0, i, 0))],
            out_specs=[win(lambda i, c: (i, 0)), sq(lambda i, c: (0, i, 0))] * 2),
        out_shape=[jax.ShapeDtypeStruct((D // 2, D_IN), F32), jax.ShapeDtypeStruct((3, D, D // 2), F32),
                   jax.ShapeDtypeStruct((D // 2, D_IN), BF16), jax.ShapeDtypeStruct((3, D, D // 2), BF16)],
        compiler_params=_params(("arbitrary",)),
    )(c_arr, gwin, win_got, gsq, sq_got)


def _sum_chips(jc_arr, swin, ssq, win_got, sq_got):
    def body(jc_ref, a_ref, b_ref, p_ref, q_ref, so_ref, sq_ref):
        so_ref[...] = ((a_ref[...] + b_ref[0].astype(F32)) + b_ref[1].astype(F32)) + b_ref[2].astype(F32)
        sq_ref[...] = ((p_ref[...] + q_ref[0].astype(F32)) + q_ref[1].astype(F32)) + q_ref[2].astype(F32)

    rows = 128
    steps = (D // 2) // rows
    sq_rows = SQ_ROWS // steps
    return pl.pallas_call(
        body, name="sum_chips",
        grid_spec=pltpu.PrefetchScalarGridSpec(
            num_scalar_prefetch=1, grid=(steps,),
            in_specs=[pl.BlockSpec((rows, SHARD_W), lambda i, jc: (i, jc[0])),
                      pl.BlockSpec((3, rows, SHARD_W), lambda i, jc: (0, i, 0)),
                      pl.BlockSpec((3, sq_rows, D // 2), lambda i, jc: (0, jc[0] * steps + i, 0)),
                      pl.BlockSpec((3, 3, sq_rows, D // 2), lambda i, jc: (0, 0, i, 0))],
            out_specs=[pl.BlockSpec((rows, SHARD_W), lambda i, jc: (jc[1] * steps + i, 0)),
                       pl.BlockSpec((3, sq_rows, D // 2), lambda i, jc: (0, i, jc[1]))]),
        out_shape=[jax.ShapeDtypeStruct((D, SHARD_W), F32), jax.ShapeDtypeStruct((3, SQ_ROWS, D), F32)],
        compiler_params=_params(("arbitrary",)),
    )(jc_arr, swin, win_got, ssq, sq_got)


def _join_halves(g_win, g_sq):
    def body(win_in, sq_in, win_out, sq_out, send_sems, recv_sems):
        del win_in, sq_in
        x, y, c = _place()
        sib = (x, y, 1 - c)

        def halves(h):
            return _win_half(win_out, h), sq_out.at[:, :, pl.ds(h * (D // 2), D // 2)]

        def copy(a, part):
            return pltpu.make_async_remote_copy(src_ref=part, dst_ref=part, send_sem=send_sems.at[a], recv_sem=recv_sems.at[a],
                                                device_id=sib, device_id_type=MESH)

        sent = [copy(a, part) for a, part in enumerate(halves(c))]
        for cp in sent:
            cp.start()
        for a, part in enumerate(halves(1 - c)):
            copy(a, part).wait_recv()
        for cp in sent:
            cp.wait_send()

    return pl.pallas_call(
        body, name="join_halves",
        in_specs=[HBM_SPEC, HBM_SPEC], out_specs=[HBM_SPEC, HBM_SPEC], input_output_aliases={0: 0, 1: 1},
        out_shape=[jax.ShapeDtypeStruct((D, SHARD_W), F32), jax.ShapeDtypeStruct((3, SQ_ROWS, D), F32)],
        scratch_shapes=[pltpu.SemaphoreType.DMA((2,)), pltpu.SemaphoreType.DMA((2,))],
    )(g_win, g_sq)


def _allreduce_small(vec):
    def body(vec_ref, out_ref, slots, send_sems, recv_sems):
        x, y, c = _place()
        me = 4 * x + 2 * y + c
        slots[me] = vec_ref[...]
        copies = []
        for k in range(1, 8):
            fx, fy, fc = (k >> 2) & 1, (k >> 1) & 1, k & 1
            copies.append(pltpu.make_async_remote_copy(
                src_ref=vec_ref, dst_ref=slots.at[me], send_sem=send_sems.at[k - 1], recv_sem=recv_sems.at[k - 1],
                device_id=(_flip(x, fx), _flip(y, fy), _flip(c, fc)), device_id_type=MESH))
        for cp in copies:
            cp.start()
        for k in range(1, 8):
            fx, fy, fc = (k >> 2) & 1, (k >> 1) & 1, k & 1
            src = 4 * _flip(x, fx) + 2 * _flip(y, fy) + _flip(c, fc)
            pltpu.make_async_remote_copy(src_ref=vec_ref, dst_ref=slots.at[src], send_sem=send_sems.at[k - 1],
                                         recv_sem=recv_sems.at[k - 1], device_id=(x, y, c), device_id_type=MESH).wait_recv()
        for cp in copies:
            cp.wait_send()
        total = slots[0]
        for s in range(1, 8):
            total = total + slots[s]
        out_ref[...] = total

    return pl.pallas_call(
        body, name="allreduce_small",
        in_specs=[pl.BlockSpec(memory_space=pltpu.VMEM)], out_specs=pl.BlockSpec(memory_space=pltpu.VMEM),
        out_shape=jax.ShapeDtypeStruct((8, D), F32),
        scratch_shapes=[pltpu.VMEM((8, 8, D), F32), pltpu.SemaphoreType.DMA((7,)), pltpu.SemaphoreType.DMA((7,))],
    )(vec)


def _adamw_math(w, g, m, v):
    m = ADAM_B1 * m + (1.0 - ADAM_B1) * g
    v = ADAM_B2 * v + (1.0 - ADAM_B2) * (g * g)
    m_hat = m / (1.0 - ADAM_B1 ** ADAM_STEP)
    v_hat = v / (1.0 - ADAM_B2 ** ADAM_STEP)
    delta = -ADAM_LR * (m_hat / (jnp.sqrt(v_hat) + ADAM_EPS) + ADAM_WD * w)
    return delta, m, v


def _adamw(name, w, g, m, v, rows):
    R, C = w.shape

    def body(w_ref, g_ref, m_ref, v_ref, d_out, m_out, v_out):
        d_out[...], m_out[...], v_out[...] = _adamw_math(w_ref[...], g_ref[...], m_ref[...], v_ref[...])

    spec = pl.BlockSpec((rows, C), lambda i: (i, 0))
    return pl.pallas_call(
        body, name=name, grid=(R // rows,), in_specs=[spec] * 4, out_specs=[spec] * 3,
        out_shape=[jax.ShapeDtypeStruct((R, C), F32)] * 3,
        compiler_params=_params(("parallel",)),
    )(w, g, m, v)


def _adamw_square(g_sq, ws, ms, vs):
    def body(g_ref, *refs):
        w_refs, m_refs, v_refs, outs = refs[0:3], refs[3:6], refs[6:9], refs[9:]
        for k in range(3):
            g = g_ref[k]
            outs[k][0] = g
            outs[3 + k][0], outs[6 + k][0], outs[9 + k][0] = _adamw_math(w_refs[k][0], g, m_refs[k][0], v_refs[k][0])

    out = pl.pallas_call(
        body, name="adamw_square", out_shape=[jax.ShapeDtypeStruct((1, SQ_ROWS, D), F32)] * 12,
        compiler_params=_params(),
    )(g_sq, *ws, *ms, *vs)
    return out[0:3], out[3:6], out[6:9], out[9:12]


def _small_update(total, lbw, w8, m8, v8):
    def body(t_ref, lbw_ref, w_ref, m_ref, v_ref, g_out, d_out, m_out, v_out):
        lb = 1.0 / (1.0 + jnp.exp(lbw_ref[1:2, :] - lbw_ref[0:1, :]))
        dlb = t_ref[2:3, :] * lb * (1.0 - lb)
        g_out[...] = jnp.zeros_like(g_out)
        g_out[0:1, :] = t_ref[3:4, :]
        g_out[1:2, :] = dlb
        g_out[2:3, :] = -dlb
        g_out[3:4, :] = t_ref[1:2, :]
        g_out[4:5, :] = t_ref[0:1, :]
        g_out[5:6, :] = t_ref[4:5, :]
        d_out[...], m_out[...], v_out[...] = _adamw_math(w_ref[...], g_out[...], m_ref[...], v_ref[...])

    return pl.pallas_call(
        body, name="small_update", out_shape=[jax.ShapeDtypeStruct((8, D), F32)] * 4,
        compiler_params=_params(),
    )(total, lbw, w8, m8, v8)


def _pack8(norm_w, lbw, hnw, fnw, sinks):
    pad = jnp.zeros((1, D - 16), F32)
    return jnp.concatenate([norm_w, lbw, hnw, fnw.reshape(1, D), jnp.concatenate([sinks, pad], axis=1),
                            jnp.zeros((2, D), F32)], axis=0)


def _unpack8(a):
    return a[0:1], a[1:3], a[3:4], a[5:6, 0:16], a[4]


def _local_step(order_arr, x, tgt, norm_w, lbw, hnw, sinks, fnw, win_mine, wsq_mine, exchange):
    proj, xnt_bf, win_bf = _fwd_proj(order_arr, x, norm_w, win_mine)
    oh, states, wsq_all = _hgrn_fwd(proj, lbw, wsq_mine)
    wsq_bf = wsq_all.reshape(SHARDS, 3, SQ_ROWS, D).transpose(1, 0, 2, 3).reshape(3, D, D)
    oa = _attn_fwd(proj, sinks)
    dx2, doh, doa, dhg, dtail, lhs, rhs, loss8, vec_mid = _mid(x, tgt, proj, oh, oa, hnw, fnw.reshape(1, D), wsq_bf)
    gsq, gsq_b = _wgrad_square(lhs, rhs)
    dhead, dlb = _hgrn_bwd(proj, lbw, states, doh)
    daq, dakv, dsink = _attn_bwd(proj, sinks, oa, doa)
    pieces = [dhead, dhg, daq, dakv, dtail]
    sums = exchange(*_wgrad_in(xnt_bf, pieces), gsq, gsq_b)
    wt_bf = win_bf.transpose(0, 2, 1).reshape(D_IN, D)
    grad_x, gnw, win_got, sq_got = _bwd_dx(pieces, wt_bf, x, norm_w, dx2, sums[2], sums[3])
    sink_row = jnp.concatenate([dsink[:, 0].reshape(1, 16), jnp.zeros((1, D - 16), F32)], axis=1)
    loss_row = jnp.broadcast_to(loss8[0:1, 0:1], (1, D))
    vec = jnp.concatenate([vec_mid[0:2], dlb, gnw, sink_row, loss_row, jnp.zeros((2, D), F32)], axis=0)
    return grad_x, sums, (win_got, sq_got), vec


def kernel(x, norm_w, w_in, hgrn_lower_bound, hgrn_norm_w, w_branch_hgrn, attn_sinks, w_branch_attn, w_out, final_norm_w, loss_target, m_norm_w, m_w_in, m_hgrn_lower_bound, m_hgrn_norm_w, m_w_branch_hgrn, m_attn_sinks, m_w_branch_attn, m_w_out, m_final_norm_w, v_norm_w, v_w_in, v_hgrn_lower_bound, v_hgrn_norm_w, v_w_branch_hgrn, v_attn_sinks, v_w_branch_attn, v_w_out, v_final_norm_w):
    c_arr = lax.axis_index("c").astype(jnp.int32).reshape(1)
    j_arr = (2 * lax.axis_index("x") + lax.axis_index("y")).astype(jnp.int32).reshape(1)
    jc_arr = jnp.concatenate([j_arr, c_arr])

    win_mine, wsq_mine = _cast_shards(j_arr, w_in[0], w_branch_hgrn[0], w_branch_attn[0], w_out[0])
    xi, yi = lax.axis_index("x"), lax.axis_index("y")
    order_arr = jnp.stack([2 * xi + yi] + [2 * _flip(xi, fx) + _flip(yi, fy) for fx, fy in CHIP_FLIPS]).astype(jnp.int32)

    def chip_sums(gwin, gwin_b, gsq, gsq_b):
        return _add_halves(c_arr, gwin, gsq, *_swap_halves(gwin_b, gsq_b))

    grad_x, (swin, ssq, _, _), arrived, vec = _local_step(
        order_arr, x[0], loss_target[0], norm_w, hgrn_lower_bound, hgrn_norm_w, attn_sinks, final_norm_w, win_mine, wsq_mine,
        chip_sums)
    g_win, g_sq = _join_halves(*_sum_chips(jc_arr, swin, ssq, *arrived))

    d_win, nm_win, nv_win = _adamw("adamw_w_in", w_in[0], g_win, m_w_in[0], v_w_in[0], 128)
    g_sqs, d_sqs, nm_sqs, nv_sqs = _adamw_square(
        g_sq, (w_branch_hgrn, w_branch_attn, w_out), (m_w_branch_hgrn, m_w_branch_attn, m_w_out),
        (v_w_branch_hgrn, v_w_branch_attn, v_w_out))

    total = _allreduce_small(vec)
    loss = total[5, 0]
    g8, d8, nm8, nv8 = _small_update(
        total, hgrn_lower_bound,
        _pack8(norm_w, hgrn_lower_bound, hgrn_norm_w, final_norm_w, attn_sinks),
        _pack8(m_norm_w, m_hgrn_lower_bound, m_hgrn_norm_w, m_final_norm_w, m_attn_sinks),
        _pack8(v_norm_w, v_hgrn_lower_bound, v_hgrn_norm_w, v_final_norm_w, v_attn_sinks))

    def assemble(win, sq, small):
        nw, lb, hn, sk, fn = _unpack8(small)
        return (nw, win.reshape(1, D, SHARD_W), lb, hn, sq[0], sk, sq[1], sq[2], fn)

    return (loss, grad_x.reshape(1, -1, D),
            *assemble(g_win, g_sqs, g8), *assemble(d_win, d_sqs, d8),
            *assemble(nm_win, nm_sqs, nm8), *assemble(nv_win, nv_sqs, nv8))
```

```python
import functools

import jax
import jax.numpy as jnp
from jax import lax
from jax.experimental import pallas as pl
from jax.experimental.pallas import tpu as pltpu

F32 = jnp.float32
BF16 = jnp.bfloat16

D = 1024
D_IN = 8704
SHARDS = 4
SHARD_W = D_IN // SHARDS
SQ_ROWS = D // SHARDS
HEADS = 8
HEAD_W = 128
CHUNK = 64
SUB = 4
ATT_BLOCK = 128
KV_HEADS = 4
HEAD_DIM = 64
EPS = 1e-6
NEG = -1e30
SCALE = HEAD_DIM ** -0.5
COL_HG, COL_AQ, COL_AK, COL_AV, COL_AG, COL_MH, COL_MA = 3072, 4096, 5120, 5376, 5632, 6656, 7680

ADAM_LR, ADAM_B1, ADAM_B2, ADAM_EPS, ADAM_WD, ADAM_STEP = 0.001, 0.9, 0.999, 1e-08, 0.01, 10

VMEM_LIMIT = 56 * 1024 * 1024
MESH = pl.DeviceIdType.MESH
HBM_SPEC = pl.BlockSpec(memory_space=pltpu.HBM)
CHIP_FLIPS = ((1, 0), (0, 1), (1, 1))


def _dot(a, b):
    return jnp.dot(a, b, preferred_element_type=F32)


def _dot_nt(a, b):
    return lax.dot_general(a, b, (((1,), (1,)), ((), ())), preferred_element_type=F32)


def _dot_tn(a, b):
    return lax.dot_general(a, b, (((0,), (0,)), ((), ())), preferred_element_type=F32)


def _sigmoid(v):
    return 1.0 / (1.0 + jnp.exp(-v))


def _bf(v):
    return v.astype(BF16)


def _tri_dot2(tri, v):
    a = _bf(v)
    return _dot(tri, a) + _dot(tri, _bf(v - a.astype(F32)))


def _params(sem=None):
    return pltpu.CompilerParams(dimension_semantics=sem, vmem_limit_bytes=VMEM_LIMIT)


def _cast_shards(j_arr, win_s, wbh_s, wba_s, wout_s):
    steps = 4
    rows = D // steps

    def body(j_ref, win_ref, a_ref, b_ref, c_ref, win_o, sq_o):
        win_o[...] = _bf(win_ref[...])

        @pl.when(pl.program_id(0) == 0)
        def _():
            sq_o[0:SQ_ROWS, :] = _bf(a_ref[...])
            sq_o[SQ_ROWS:2 * SQ_ROWS, :] = _bf(b_ref[...])
            sq_o[2 * SQ_ROWS:3 * SQ_ROWS, :] = _bf(c_ref[...])

    whole = pl.BlockSpec((SQ_ROWS, D), lambda i, j: (0, 0))
    return pl.pallas_call(
        body, name="cast_shards",
        grid_spec=pltpu.PrefetchScalarGridSpec(
            num_scalar_prefetch=1, grid=(steps,),
            in_specs=[pl.BlockSpec((rows, SHARD_W), lambda i, j: (i, 0)), whole, whole, whole],
            out_specs=[pl.BlockSpec((None, rows, SHARD_W), lambda i, j: (j[0], i, 0)),
                       pl.BlockSpec((None, 3 * SQ_ROWS, D), lambda i, j: (j[0], 0, 0))]),
        out_shape=[jax.ShapeDtypeStruct((SHARDS, D, SHARD_W), BF16), jax.ShapeDtypeStruct((SHARDS, 3 * SQ_ROWS, D), BF16)],
        compiler_params=_params(("arbitrary",)),
    )(j_arr, win_s, wbh_s, wba_s, wout_s)


def _fwd_proj(order_arr, x, norm_w, win_all):
    T = x.shape[0]
    tm = min(512, T)
    nt = T // tm

    def body(order_ref, x_ref, nw_ref, win_in, proj_ref, xn_ref, win_out, w_scr, xn_scr, sems, send_sems, recv_sems):
        del win_in
        p, i = pl.program_id(0), pl.program_id(1)

        def load(n):
            return pltpu.make_async_copy(win_out.at[order_ref[n]], w_scr.at[n % 2], sems.at[n % 2])

        @pl.when((p == 0) & (i == 0))
        def _():
            _gather_start(win_out, _win_half, send_sems, recv_sems)
            load(0).start()
            load(0).wait()

        @pl.when((p == 1) & (i == 0))
        def _():
            _gather_land(win_out, _win_half, 0, send_sems, recv_sems)
            load(1).start()
            load(1).wait()

        for k in range(1, SHARDS - 1):
            @pl.when((p == k) & (i == nt // 2))
            def _():
                _gather_land(win_out, _win_half, k, send_sems, recv_sems)
                load(k + 1).start()

            @pl.when((p == k + 1) & (i == 0))
            def _():
                load(k + 1).wait()

        @pl.when(p == 0)
        def _():
            xf = x_ref[...]
            rs = lax.rsqrt(jnp.mean(xf * xf, axis=1, keepdims=True) + EPS)
            xn = _bf((xf * rs) * nw_ref[...])
            xn_scr[i] = xn
            xn_ref[...] = xn.T

        proj_ref[...] = _dot(xn_scr[i], w_scr[p % 2])

        @pl.when((p == SHARDS - 1) & (i == nt - 1))
        def _():
            _gather_drain(win_out, _win_half, send_sems, recv_sems)

    first = lambda p, i: jnp.where(p == 0, i, nt - 1)
    return pl.pallas_call(
        body, name="fwd_proj",
        grid_spec=pltpu.PrefetchScalarGridSpec(
            num_scalar_prefetch=1, grid=(SHARDS, nt),
            in_specs=[pl.BlockSpec((tm, D), lambda p, i, order: (first(p, i), 0)),
                      pl.BlockSpec((1, D), lambda p, i, order: (0, 0)), HBM_SPEC],
            out_specs=[pl.BlockSpec((tm, SHARD_W), lambda p, i, order: (i, order[p])),
                       pl.BlockSpec((D, tm), lambda p, i, order: (0, first(p, i))),
                       HBM_SPEC],
            scratch_shapes=[pltpu.VMEM((2, D, SHARD_W), BF16), pltpu.VMEM((nt, tm, D), BF16), pltpu.SemaphoreType.DMA((2,)),
                            pltpu.SemaphoreType.DMA((6,)), pltpu.SemaphoreType.DMA((6,))]),
        out_shape=[jax.ShapeDtypeStruct((T, D_IN), F32), jax.ShapeDtypeStruct((D, T), BF16),
                   jax.ShapeDtypeStruct((SHARDS, D, SHARD_W), BF16)],
        input_output_aliases={3: 2},
        compiler_params=_params(("arbitrary", "arbitrary")),
    )(order_arr, x, norm_w, win_all)


def _hgrn_gates(hq_ref, hf_ref, lbw_ref, b_scr):
    lb = 1.0 / (1.0 + jnp.exp(lbw_ref[1:2, :] - lbw_ref[0:1, :]))
    hf = hf_ref[...]
    sig = _sigmoid(hf)
    f = lb + (1.0 - lb) * sig
    g = jnp.log(f)
    hq = hq_ref[...]
    sq = _sigmoid(hq)
    q = hq * sq
    row = lax.broadcasted_iota(jnp.int32, (CHUNK, CHUNK), 0)
    col = lax.broadcasted_iota(jnp.int32, (CHUNK, CHUNK), 1)
    causal = row >= col
    b = _tri_dot2(jnp.where(causal, 1.0, 0.0).astype(BF16), g)
    b_scr[...] = b
    bc = b_scr[CHUNK - 1:CHUNK, :]
    r = b_scr[CHUNK // 2 - 1:CHUNK // 2, :]
    return dict(lb=lb, sig=sig, f=f, k=1.0 - f, hq=hq, sq=sq, q=q, b=b, bc=bc, r=r, causal=causal)


def _hgrn_fwd(proj, lbw, wsq_all):
    T = proj.shape[0]
    n = T // CHUNK

    def body(hq_ref, hf_ref, hi_ref, lbw_ref, wsq_in, o_ref, st_ref, wsq_out, s_scr, b_scr, send_sems, recv_sems):
        del wsq_in

        @pl.when(pl.program_id(0) == 0)
        def _():
            _gather_start(wsq_out, _sq_half, send_sems, recv_sems)
            s_scr[...] = jnp.zeros_like(s_scr)

        for c in range(SUB):
            rows = pl.ds(c * CHUNK, CHUNK)
            gt = _hgrn_gates(hq_ref.at[rows, :], hf_ref.at[rows, :], lbw_ref, b_scr.at[rows, :])
            b, bc, r, q, k = gt["b"], gt["bc"], gt["r"], gt["q"], gt["k"]
            qe = _bf(q * jnp.exp(b))
            qr = _bf(q * jnp.exp(b - r))
            kr = _bf(k * jnp.exp(r - b))
            kl = _bf(k * jnp.exp(bc - b))
            ebc = jnp.exp(bc)
            v = _bf(hi_ref[rows, :])
            scores = [_bf(jnp.where(gt["causal"], _dot_nt(qr[:, h * HEAD_W:(h + 1) * HEAD_W], kr[:, h * HEAD_W:(h + 1) * HEAD_W]), 0.0))
                      for h in range(HEADS)]
            for h in range(HEADS):
                sl = slice(h * HEAD_W, (h + 1) * HEAD_W)
                st = s_scr[h]
                st_ref[c, h] = st
                o_ref[rows, sl] = _dot(scores[h], v[:, sl]) + _dot_nt(qe[:, sl], _bf(st))
                s_scr[h] = ebc[:, sl] * st + _dot_tn(v[:, sl], kl[:, sl])

        @pl.when(pl.program_id(0) == n // SUB - 1)
        def _():
            _gather_finish(wsq_out, _sq_half, send_sems, recv_sems)

    col = lambda j: pl.BlockSpec((SUB * CHUNK, D), lambda i: (i, j))
    return pl.pallas_call(
        body, name="hgrn_fwd", grid=(n // SUB,),
        in_specs=[col(0), col(1), col(2), pl.BlockSpec((2, D), lambda i: (0, 0)), HBM_SPEC],
        out_specs=[pl.BlockSpec((SUB * CHUNK, D), lambda i: (i, 0)),
                   pl.BlockSpec((SUB, HEADS, HEAD_W, HEAD_W), lambda i: (i, 0, 0, 0)), HBM_SPEC],
        out_shape=[jax.ShapeDtypeStruct((T, D), F32), jax.ShapeDtypeStruct((n, HEADS, HEAD_W, HEAD_W), F32),
                   jax.ShapeDtypeStruct((SHARDS, 3 * SQ_ROWS, D), BF16)],
        input_output_aliases={4: 2},
        scratch_shapes=[pltpu.VMEM((HEADS, HEAD_W, HEAD_W), F32), pltpu.VMEM((SUB * CHUNK, D), F32),
                        pltpu.SemaphoreType.DMA((6,)), pltpu.SemaphoreType.DMA((6,))],
        compiler_params=_params(("arbitrary",)),
    )(proj, proj, proj, lbw, wsq_all)


def _hgrn_bwd(proj, lbw, states, do):
    T = proj.shape[0]
    n = T // CHUNK

    def body(hq_ref, hf_ref, hi_ref, lbw_ref, st_ref, do_ref, dp_ref, dlb_ref,
             ds_scr, b_scr, dq_scr, dk_scr, dv_scr, late_scr, early_scr, ex_scr):
        @pl.when(pl.program_id(0) == 0)
        def _():
            ds_scr[...] = jnp.zeros_like(ds_scr)
            dlb_ref[...] = jnp.zeros_like(dlb_ref)

        for c in reversed(range(SUB)):
            rows = pl.ds(c * CHUNK, CHUNK)
            gt = _hgrn_gates(hq_ref.at[rows, :], hf_ref.at[rows, :], lbw_ref, b_scr.at[rows, :])
            b, bc, r, q, k = gt["b"], gt["bc"], gt["r"], gt["q"], gt["k"]
            eb = jnp.exp(b)
            er = jnp.exp(b - r)
            erk = jnp.exp(r - b)
            el = jnp.exp(bc - b)
            ebc = jnp.exp(bc)
            qe, qr, kr, kl = _bf(q * eb), _bf(q * er), _bf(k * erk), _bf(k * el)
            v = _bf(hi_ref[rows, :])
            do_b = do_ref[rows, :]
            do_t = do_b.T
            causal_t = lax.broadcasted_iota(jnp.int32, (CHUNK, CHUNK), 0) <= lax.broadcasted_iota(jnp.int32, (CHUNK, CHUNK), 1)
            firsts = []
            for h in range(HEADS):
                sl = slice(h * HEAD_W, (h + 1) * HEAD_W)
                firsts.append((_bf(jnp.where(causal_t, _dot_nt(kr[:, sl], qr[:, sl]), 0.0)),
                               _bf(jnp.where(gt["causal"], _dot_nt(do_b[:, sl], v[:, sl]), 0.0)),
                               _bf(jnp.where(causal_t, _dot_nt(v[:, sl], do_b[:, sl]), 0.0))))
            for h in range(HEADS):
                sl = slice(h * HEAD_W, (h + 1) * HEAD_W)
                st0 = st_ref[c, h]
                dst = ds_scr[h]
                dst_b = _bf(dst)
                a_t, da, da_t = firsts[h]
                mq = _dot(da, kr[:, sl])
                mk = _dot(da_t, qr[:, sl])
                dq_in = eb[:, sl] * _dot(do_b[:, sl], _bf(st0))
                dk_in = el[:, sl] * _dot(v[:, sl], dst_b)
                dq_scr[rows, sl] = er[:, sl] * mq + dq_in
                dk_scr[rows, sl] = erk[:, sl] * mk + dk_in
                dv_scr[rows, sl] = _dot(a_t, do_b[:, sl]) + _dot_nt(kl[:, sl], dst_b)
                late_scr[rows, sl] = q[:, sl] * dq_in + qr[:, sl].astype(F32) * mq - kr[:, sl].astype(F32) * mk
                early_scr[rows, sl] = k[:, sl] * dk_in
                ex_scr[:, sl] = jnp.sum(dst * st0, axis=0, keepdims=True)
                ds_scr[h] = ebc[:, sl] * dst + _dot(do_t[sl, :], qe[:, sl])

            dq, dk = dq_scr[rows, :], dk_scr[rows, :]
            row = lax.broadcasted_iota(jnp.int32, (CHUNK, CHUNK), 0)
            col = lax.broadcasted_iota(jnp.int32, (CHUNK, CHUNK), 1)
            at_or_after = jnp.where(col >= row, 1.0, 0.0).astype(BF16)
            before = jnp.where(col < row, 1.0, 0.0).astype(BF16)
            dg = _tri_dot2(jnp.concatenate([at_or_after, before], axis=1),
                           jnp.concatenate([late_scr[rows, :], early_scr[rows, :]], axis=0)) + ebc * ex_scr[...]
            df = dg / gt["f"] - dk
            sig, sq, hq, lb = gt["sig"], gt["sq"], gt["hq"], gt["lb"]
            dp_ref[rows, 0:D] = _bf(dq * (sq * (1.0 + hq * (1.0 - sq))))
            dp_ref[rows, D:2 * D] = _bf(df * (1.0 - lb) * sig * (1.0 - sig))
            dp_ref[rows, 2 * D:3 * D] = _bf(dv_scr[rows, :])
            dlb_ref[...] += jnp.sum(df * (1.0 - sig), axis=0, keepdims=True)

    ns = n // SUB
    col = lambda j: pl.BlockSpec((SUB * CHUNK, D), lambda i: (ns - 1 - i, j))
    return pl.pallas_call(
        body, name="hgrn_bwd", grid=(ns,),
        in_specs=[col(0), col(1), col(2), pl.BlockSpec((2, D), lambda i: (0, 0)),
                  pl.BlockSpec((SUB, HEADS, HEAD_W, HEAD_W), lambda i: (ns - 1 - i, 0, 0, 0)),
                  pl.BlockSpec((SUB * CHUNK, D), lambda i: (ns - 1 - i, 0))],
        out_specs=[pl.BlockSpec((SUB * CHUNK, 3 * D), lambda i: (ns - 1 - i, 0)),
                   pl.BlockSpec((1, D), lambda i: (0, 0))],
        out_shape=[jax.ShapeDtypeStruct((T, 3 * D), BF16), jax.ShapeDtypeStruct((1, D), F32)],
        scratch_shapes=[pltpu.VMEM((HEADS, HEAD_W, HEAD_W), F32)] + [pltpu.VMEM((SUB * CHUNK, D), F32)] * 6
                       + [pltpu.VMEM((1, D), F32)],
        compiler_params=_params(("arbitrary",)),
    )(proj, proj, proj, lbw, states, do)


def _attn_masks(blk):
    qi = lax.broadcasted_iota(jnp.int32, (ATT_BLOCK, 2 * ATT_BLOCK), 0)
    kj = lax.broadcasted_iota(jnp.int32, (ATT_BLOCK, 2 * ATT_BLOCK), 1)
    band = (kj > qi) & (kj <= qi + ATT_BLOCK)
    return band & ((blk > 0) | (kj >= ATT_BLOCK))


def _head_pair_operand(t, hp, low):
    mine = low if hp == 0 else jnp.logical_not(low)
    both = jnp.where(mine, t, pltpu.roll(t, HEAD_DIM, 1))
    return _bf(jnp.concatenate([jnp.where(low, both, 0.0), jnp.where(low, 0.0, both)], axis=0))


def _attn_probs(s, sink, valid):
    s = jnp.where(valid, s, NEG)
    m = jnp.maximum(jnp.max(s, axis=1, keepdims=True), sink)
    p = jnp.exp(s - m)
    es = jnp.exp(sink - m)
    inv = 1.0 / (jnp.sum(p, axis=1, keepdims=True) + es)
    return p * inv, es * inv


def _attn_fwd(proj, sinks):
    T = proj.shape[0]
    nb = T // ATT_BLOCK
    W2 = 2 * ATT_BLOCK

    def body(sink_ref, q_ref, kp_ref, kc_ref, vp_ref, vc_ref, o_ref):
        blk = pl.program_id(0)
        valid = _attn_masks(blk)
        low = lax.broadcasted_iota(jnp.int32, (1, 2 * HEAD_DIM), 1) < HEAD_DIM
        kcat = jnp.concatenate([kp_ref[...], kc_ref[...]], axis=0)
        vcat = jnp.concatenate([vp_ref[...], vc_ref[...]], axis=0)
        for h in range(KV_HEADS):
            tl = slice((h // 2) * 128, (h // 2) * 128 + 128)
            mine = low if h % 2 == 0 else jnp.logical_not(low)
            kh = _bf(jnp.where(mine, kcat[:, tl], pltpu.roll(kcat[:, tl], HEAD_DIM, 1)))
            vh = _bf(jnp.where(mine, vcat[:, tl], pltpu.roll(vcat[:, tl], HEAD_DIM, 1)))
            for t in range(2):
                ql = slice((2 * h + t) * 128, (2 * h + t) * 128 + 128)
                q2 = q_ref[:, ql] * SCALE
                outs = []
                for p in range(2):
                    qm = _bf(jnp.where(low if p == 0 else jnp.logical_not(low), q2, 0.0))
                    probs, _ = _attn_probs(_dot_nt(qm, kh), sink_ref[0, 4 * h + 2 * t + p], valid)
                    outs.append(_dot(_bf(probs), vh))
                o_ref[:, ql] = jnp.where(low, outs[0], outs[1])

    prev = lambda i: jnp.maximum(i - 1, 0)
    return pl.pallas_call(
        body, name="attn_fwd", grid=(nb,),
        in_specs=[pl.BlockSpec(memory_space=pltpu.SMEM),
                  pl.BlockSpec((ATT_BLOCK, D), lambda i: (i, COL_AQ // D)),
                  pl.BlockSpec((ATT_BLOCK, 256), lambda i: (prev(i), COL_AK // 256)),
                  pl.BlockSpec((ATT_BLOCK, 256), lambda i: (i, COL_AK // 256)),
                  pl.BlockSpec((ATT_BLOCK, 256), lambda i: (prev(i), COL_AV // 256)),
                  pl.BlockSpec((ATT_BLOCK, 256), lambda i: (i, COL_AV // 256))],
        out_specs=pl.BlockSpec((ATT_BLOCK, D), lambda i: (i, 0)),
        out_shape=jax.ShapeDtypeStruct((T, D), F32),
        compiler_params=_params(("arbitrary",)),
    )(sinks, proj, proj, proj, proj, proj)


def _attn_bwd(proj, sinks, o, do):
    T = proj.shape[0]
    nb = T // ATT_BLOCK
    W2 = 2 * ATT_BLOCK

    def body(sink_ref, q_ref, kp_ref, kc_ref, vp_ref, vc_ref, o_ref, do_ref,
             dq_ref, dkv_ref, dsink_ref, ck_scr, cv_scr, nk_scr, nv_scr):
        blk = pl.program_id(0)

        @pl.when(blk == 0)
        def _():
            ck_scr[...] = jnp.zeros_like(ck_scr)
            cv_scr[...] = jnp.zeros_like(cv_scr)
            dsink_ref[...] = jnp.zeros_like(dsink_ref)

        @pl.when(blk < nb)
        def _():
            valid = _attn_masks(blk)
            low = lax.broadcasted_iota(jnp.int32, (1, 2 * HEAD_DIM), 1) < HEAD_DIM
            kcat = jnp.concatenate([kp_ref[...], kc_ref[...]], axis=0)
            vcat = jnp.concatenate([vp_ref[...], vc_ref[...]], axis=0)
            for h in range(KV_HEADS):
                tl = slice((h // 2) * 128, (h // 2) * 128 + 128)
                kbd = _head_pair_operand(kcat[:, tl], h % 2, low)
                vbd = _head_pair_operand(vcat[:, tl], h % 2, low)
                dkbd = jnp.zeros((2 * W2, 128), F32)
                dvbd = jnp.zeros((2 * W2, 128), F32)
                tiles = []
                for t in range(2):
                    ql = slice((2 * h + t) * 128, (2 * h + t) * 128 + 128)
                    q2 = _bf(q_ref[:, ql] * SCALE)
                    do2_b = do_ref[:, ql]
                    doo = do2_b.astype(F32) * o_ref[:, ql]
                    dsum0 = jnp.sum(jnp.where(low, doo, 0.0), axis=1, keepdims=True)
                    dsum1 = jnp.sum(jnp.where(low, 0.0, doo), axis=1, keepdims=True)
                    tiles.append((ql, q2, do2_b, dsum0, dsum1, _dot_nt(q2, kbd), _dot_nt(do2_b, vbd)))
                grads = []
                for t, (ql, q2, do2_b, dsum0, dsum1, s2, dp2) in enumerate(tiles):
                    head = 4 * h + 2 * t
                    p0, ps0 = _attn_probs(s2[:, 0:W2], sink_ref[0, head], valid)
                    p1, ps1 = _attn_probs(s2[:, W2:2 * W2], sink_ref[0, head + 1], valid)
                    ds2 = _bf(jnp.concatenate([p0 * (dp2[:, 0:W2] - dsum0), p1 * (dp2[:, W2:2 * W2] - dsum1)], axis=1))
                    grads.append((ds2, _bf(jnp.concatenate([p0, p1], axis=1))))
                    dsink_ref[head:head + 1, :] += jnp.zeros((1, 128), F32) - jnp.sum(ps0 * dsum0, axis=0, keepdims=True)
                    dsink_ref[head + 1:head + 2, :] += jnp.zeros((1, 128), F32) - jnp.sum(ps1 * dsum1, axis=0, keepdims=True)
                for (ql, q2, do2_b, _, _, _, _), (ds2, p2) in zip(tiles, grads):
                    dq_ref[:, ql] = _bf(_dot(ds2, kbd) * SCALE)
                    dkbd = dkbd + _dot_tn(ds2, q2)
                    dvbd = dvbd + _dot_tn(p2, do2_b)
                dk2 = jnp.where(low, dkbd[0:W2], dkbd[W2:2 * W2])
                dv2 = jnp.where(low, dvbd[0:W2], dvbd[W2:2 * W2])
                dk2 = dk2 + pltpu.roll(dk2, HEAD_DIM, 1)
                dv2 = dv2 + pltpu.roll(dv2, HEAD_DIM, 1)
                if h % 2 == 0:
                    keep_k, keep_v = dk2, dv2
                else:
                    nk_scr[:, tl] = jnp.where(low, keep_k, dk2)
                    nv_scr[:, tl] = jnp.where(low, keep_v, dv2)
            dkv_ref[:, 0:256] = _bf(ck_scr[...] + nk_scr[0:ATT_BLOCK, :])
            dkv_ref[:, 256:512] = _bf(cv_scr[...] + nv_scr[0:ATT_BLOCK, :])
            ck_scr[...] = nk_scr[ATT_BLOCK:2 * ATT_BLOCK, :]
            cv_scr[...] = nv_scr[ATT_BLOCK:2 * ATT_BLOCK, :]

        @pl.when(blk == nb)
        def _():
            dkv_ref[:, 0:256] = _bf(ck_scr[...])
            dkv_ref[:, 256:512] = _bf(cv_scr[...])

    cur = lambda i: jnp.minimum(i, nb - 1)
    prev = lambda i: jnp.maximum(cur(i) - 1, 0)
    late = lambda i: jnp.maximum(i - 1, 0)
    return pl.pallas_call(
        body, name="attn_bwd", grid=(nb + 1,),
        in_specs=[pl.BlockSpec(memory_space=pltpu.SMEM),
                  pl.BlockSpec((ATT_BLOCK, D), lambda i: (cur(i), COL_AQ // D)),
                  pl.BlockSpec((ATT_BLOCK, 256), lambda i: (prev(i), COL_AK // 256)),
                  pl.BlockSpec((ATT_BLOCK, 256), lambda i: (cur(i), COL_AK // 256)),
                  pl.BlockSpec((ATT_BLOCK, 256), lambda i: (prev(i), COL_AV // 256)),
                  pl.BlockSpec((ATT_BLOCK, 256), lambda i: (cur(i), COL_AV // 256)),
                  pl.BlockSpec((ATT_BLOCK, D), lambda i: (cur(i), 0)),
                  pl.BlockSpec((ATT_BLOCK, D), lambda i: (cur(i), 0))],
        out_specs=[pl.BlockSpec((ATT_BLOCK, D), lambda i: (cur(i), 0)),
                   pl.BlockSpec((ATT_BLOCK, 512), lambda i: (late(i), 0)),
                   pl.BlockSpec((16, 128), lambda i: (0, 0))],
        out_shape=[jax.ShapeDtypeStruct((T, D), BF16), jax.ShapeDtypeStruct((T, 512), BF16),
                   jax.ShapeDtypeStruct((16, 128), F32)],
        scratch_shapes=[pltpu.VMEM((ATT_BLOCK, 256), F32), pltpu.VMEM((ATT_BLOCK, 256), F32),
                        pltpu.VMEM((2 * ATT_BLOCK, 256), F32), pltpu.VMEM((2 * ATT_BLOCK, 256), F32)],
        compiler_params=_params(("arbitrary",)),
    )(sinks, proj, proj, proj, proj, proj, o, do)


def _mid(x, tgt, proj, oh, oa, hnw, fnw, wsq_bf):
    T = x.shape[0]
    tm = min(256, T)
    nt = T // tm

    def body(x_ref, tgt_ref, oh_ref, oa_ref, hg_ref, ag0_ref, ag1_ref, mh0_ref, mh1_ref, ma0_ref, ma1_ref,
             hnw_ref, fnw_ref, w_hbm,
             dx2_ref, doh_ref, doa_ref, dhg_ref, dtail_ref, lhs_ref, rhs_ref, loss_ref, vec_ref,
             w_scr, xh_scr, rs_scr, sem):
        @pl.when(pl.program_id(0) == 0)
        def _():
            cp = pltpu.make_async_copy(w_hbm, w_scr, sem)
            cp.start()
            cp.wait()
            loss_ref[...] = jnp.zeros_like(loss_ref)
            vec_ref[...] = jnp.zeros_like(vec_ref)

        oh = oh_ref[...]
        for h in range(HEADS):
            sl = slice(h * HEAD_W, (h + 1) * HEAD_W)
            ohh = oh[:, sl]
            rs = lax.rsqrt(jnp.mean(ohh * ohh, axis=1, keepdims=True) + EPS)
            xh_scr[:, sl] = ohh * rs
            rs_scr[:, sl] = jnp.broadcast_to(rs, (tm, HEAD_W))
        xh = xh_scr[...]
        hnw = hnw_ref[...]
        on = xh * hnw
        hg = hg_ref[...]
        sg = _sigmoid(hg)
        silu_g = hg * sg
        gated_h = _bf(on * silu_g)
        oa = oa_ref[...]
        ag = jnp.concatenate([ag0_ref[...], ag1_ref[...]], axis=1)
        sa = _sigmoid(ag)
        silu_a = ag * sa
        gated_a = _bf(oa * silu_a)
        yh = _dot(gated_h, w_scr[0])
        ya = _dot(gated_a, w_scr[1])
        lhs_ref[0] = gated_h.T
        lhs_ref[1] = gated_a.T
        smh = _sigmoid(jnp.concatenate([mh0_ref[...], mh1_ref[...]], axis=1))
        sma = _sigmoid(jnp.concatenate([ma0_ref[...], ma1_ref[...]], axis=1))
        merged = _bf(smh * yh + sma * ya)
        lhs_ref[2] = merged.T
        x2 = x_ref[...] + _dot(merged, w_scr[2])
        rs2 = lax.rsqrt(jnp.mean(x2 * x2, axis=1, keepdims=True) + EPS)
        xh2 = x2 * rs2
        fnw = fnw_ref[...]
        diff = xh2 * fnw - tgt_ref[...]
        loss_ref[...] += jnp.zeros_like(loss_ref) + jnp.sum(diff * diff) * (0.5 / D)

        dy = diff * (1.0 / D)
        vec_ref[0:1, :] += jnp.sum(dy * xh2, axis=0, keepdims=True)
        gy = dy * fnw
        dx2 = rs2 * (gy - xh2 * jnp.mean(gy * xh2, axis=1, keepdims=True))
        dx2_ref[...] = dx2
        dx2_b = _bf(dx2)
        rhs_ref[2] = dx2_b
        dmerged = _dot_nt(dx2_b, w_scr[2])
        dyh = dmerged * smh
        dya = dmerged * sma
        dtail_ref[:, D:2 * D] = _bf(dyh * yh * (1.0 - smh))
        dtail_ref[:, 2 * D:3 * D] = _bf(dya * ya * (1.0 - sma))
        dyh_b, dya_b = _bf(dyh), _bf(dya)
        rhs_ref[0] = dyh_b
        rhs_ref[1] = dya_b
        dgh = _dot_nt(dyh_b, w_scr[0])
        dga = _dot_nt(dya_b, w_scr[1])
        don = dgh * silu_g
        dhg_ref[...] = _bf(dgh * on * (sg * (1.0 + hg * (1.0 - sg))))
        vec_ref[1:2, :] += jnp.sum(don * xh, axis=0, keepdims=True)
        gxh = don * hnw
        rsb = rs_scr[...]
        for h in range(HEADS):
            sl = slice(h * HEAD_W, (h + 1) * HEAD_W)
            gh, xhh = gxh[:, sl], xh[:, sl]
            doh_ref[:, sl] = _bf(rsb[:, sl] * (gh - xhh * jnp.mean(gh * xhh, axis=1, keepdims=True)))
        doa_ref[...] = _bf(dga * silu_a)
        dtail_ref[:, 0:D] = _bf(dga * oa * (sa * (1.0 + ag * (1.0 - sa))))

    row = lambda w, j: pl.BlockSpec((tm, w), lambda i: (i, j))
    const = lambda r, c: pl.BlockSpec((r, c), lambda i: (0, 0))
    stack = pl.BlockSpec((3, tm, D), lambda i: (0, i, 0))
    stack_t = pl.BlockSpec((3, D, tm), lambda i: (0, 0, i))
    return pl.pallas_call(
        body, name="mid", grid=(nt,),
        in_specs=[row(D, 0), row(D, 0), row(D, 0), row(D, 0), row(D, COL_HG // D),
                  row(512, COL_AG // 512), row(512, COL_AG // 512 + 1),
                  row(512, COL_MH // 512), row(512, COL_MH // 512 + 1),
                  row(512, COL_MA // 512), row(512, COL_MA // 512 + 1),
                  const(1, D), const(1, D), HBM_SPEC],
        out_specs=[row(D, 0), row(D, 0), row(D, 0), row(D, 0), row(3 * D, 0), stack_t, stack, const(8, 128), const(8, D)],
        out_shape=[jax.ShapeDtypeStruct((T, D), F32), jax.ShapeDtypeStruct((T, D), BF16), jax.ShapeDtypeStruct((T, D), BF16),
                   jax.ShapeDtypeStruct((T, D), BF16), jax.ShapeDtypeStruct((T, 3 * D), BF16),
                   jax.ShapeDtypeStruct((3, D, T), BF16), jax.ShapeDtypeStruct((3, T, D), BF16),
                   jax.ShapeDtypeStruct((8, 128), F32), jax.ShapeDtypeStruct((8, D), F32)],
        scratch_shapes=[pltpu.VMEM((3, D, D), BF16), pltpu.VMEM((tm, D), F32), pltpu.VMEM((tm, D), F32),
                        pltpu.SemaphoreType.DMA],
        compiler_params=_params(("arbitrary",)),
    )(x, tgt, oh, oa, proj, proj, proj, proj, proj, proj, proj, hnw, fnw, wsq_bf)


def _wgrad_square(lhs_t, rhs):
    T = rhs.shape[1]
    tk = min(2048, T)
    steps = T // tk

    def body(a_ref, b_ref, g_ref, gb_ref):
        part = _dot(a_ref[...], b_ref[...])

        @pl.when(pl.program_id(1) == 0)
        def _():
            g_ref[...] = part

        @pl.when(pl.program_id(1) > 0)
        def _():
            g_ref[...] += part

        @pl.when(pl.program_id(1) == steps - 1)
        def _():
            gb_ref[...] = _bf(g_ref[...])

    return pl.pallas_call(
        body, name="wgrad_square", grid=(3, steps),
        in_specs=[pl.BlockSpec((None, D, tk), lambda k, i: (k, 0, i)), pl.BlockSpec((None, tk, D), lambda k, i: (k, i, 0))],
        out_specs=[pl.BlockSpec((None, D, D), lambda k, i: (k, 0, 0))] * 2,
        out_shape=[jax.ShapeDtypeStruct((3, D, D), F32), jax.ShapeDtypeStruct((3, D, D), BF16)],
        compiler_params=_params(("parallel", "arbitrary")),
    )(lhs_t, rhs)


def _bwd_dx(pieces, wt_bf, x, norm_w, dx2, swin_b, ssq_b):
    T = x.shape[0]
    tm = min(512, T)
    nt = T // tm
    widths = [p.shape[1] for p in pieces]
    n_p = len(pieces)

    def body(*refs):
        piece_refs = refs[:n_p]
        (w_hbm, x_ref, nw_ref, dx2_ref, swin_ref, ssq_ref,
         gx_ref, gnw_ref, win_got, sq_got, w_scr, sem, send_sems, recv_sems) = refs[n_p:]

        def scatter_copies():
            x_, y_, c_ = _place()
            copies = []
            for k, (fx, fy) in enumerate(CHIP_FLIPS):
                px, py = _flip(x_, fx), _flip(y_, fy)
                jr = 2 * px + py
                for a, (src, dst) in enumerate(((swin_ref.at[:, pl.ds(jr * SHARD_W, SHARD_W)], win_got.at[k]),
                                                (ssq_ref.at[:, pl.ds(jr * SQ_ROWS, SQ_ROWS), :], sq_got.at[k]))):
                    copies.append(pltpu.make_async_remote_copy(
                        src_ref=src, dst_ref=dst, send_sem=send_sems.at[2 * k + a], recv_sem=recv_sems.at[2 * k + a],
                        device_id=(px, py, c_), device_id_type=MESH))
            return copies

        @pl.when(pl.program_id(0) == 0)
        def _():
            for cp in scatter_copies():
                cp.start()
            cp = pltpu.make_async_copy(w_hbm, w_scr, sem)
            cp.start()
            cp.wait()
            gnw_ref[...] = jnp.zeros_like(gnw_ref)

        dxn = None
        off = 0
        for ref, w in zip(piece_refs, widths):
            part = _dot(ref[...], w_scr[off:off + w, :])
            dxn = part if dxn is None else dxn + part
            off += w
        xf = x_ref[...]
        rs = lax.rsqrt(jnp.mean(xf * xf, axis=1, keepdims=True) + EPS)
        xh = xf * rs
        gnw_ref[...] += jnp.sum(dxn * xh, axis=0, keepdims=True)
        gx = dxn * nw_ref[...]
        gx_ref[...] = rs * (gx - xh * jnp.mean(gx * xh, axis=1, keepdims=True)) + dx2_ref[...]

        @pl.when(pl.program_id(0) == nt - 1)
        def _():
            for cp in scatter_copies():
                cp.wait()

    row = lambda w: pl.BlockSpec((tm, w), lambda i: (i, 0))
    return pl.pallas_call(
        body, name="bwd_dx", grid=(nt,),
        in_specs=[row(w) for w in widths] + [HBM_SPEC, row(D), pl.BlockSpec((1, D), lambda i: (0, 0)), row(D), HBM_SPEC, HBM_SPEC],
        out_specs=[row(D), pl.BlockSpec((1, D), lambda i: (0, 0)), HBM_SPEC, HBM_SPEC],
        out_shape=[jax.ShapeDtypeStruct((T, D), F32), jax.ShapeDtypeStruct((1, D), F32),
                   jax.ShapeDtypeStruct((3, D // 2, SHARD_W), BF16), jax.ShapeDtypeStruct((3, 3, SQ_ROWS, D // 2), BF16)],
        scratch_shapes=[pltpu.VMEM((D_IN, D), BF16), pltpu.SemaphoreType.DMA,
                        pltpu.SemaphoreType.DMA((6,)), pltpu.SemaphoreType.DMA((6,))],
        compiler_params=_params(("arbitrary",)),
    )(*pieces, wt_bf, x, norm_w, dx2, swin_b, ssq_b)


W_PIECES = ((0, 1024, 3), (COL_HG, 1024, 1), (COL_AQ, 1024, 1), (COL_AK, 512, 1), (COL_AG, 512, 6))


def _wgrad_in(xnt_bf, pieces):
    T = xnt_bf.shape[1]
    bufs = ()
    for n, (piece, (col, wb, blocks)) in enumerate(zip(pieces, W_PIECES)):
        tk = min(2048 if wb == 1024 else 4096, T)
        steps = T // tk

        def body(xnt_ref, p_ref, *rest):
            g_ref, gb_ref = rest[-2:]
            part = _dot(xnt_ref[...], p_ref[...])

            @pl.when(pl.program_id(1) == 0)
            def _():
                g_ref[...] = part

            @pl.when(pl.program_id(1) > 0)
            def _():
                g_ref[...] += part

            @pl.when(pl.program_id(1) == steps - 1)
            def _():
                gb_ref[...] = _bf(g_ref[...])

        out = pl.BlockSpec((D, wb), lambda jb, i, base=col // wb: (0, base + jb))
        bufs = pl.pallas_call(
            body, name=f"wgrad_in_{n}", grid=(blocks, steps),
            in_specs=[pl.BlockSpec((D, tk), lambda jb, i: (0, i)), pl.BlockSpec((tk, wb), lambda jb, i: (i, jb))]
                     + [HBM_SPEC] * len(bufs),
            out_specs=[out, out],
            out_shape=[jax.ShapeDtypeStruct((D, D_IN), F32), jax.ShapeDtypeStruct((D, D_IN), BF16)],
            input_output_aliases={2: 0, 3: 1} if bufs else {},
            compiler_params=_params(("parallel", "arbitrary")),
        )(xnt_bf, piece, *bufs)
    return bufs


def _place():
    return lax.axis_index("x"), lax.axis_index("y"), lax.axis_index("c")


def _flip(v, f):
    return 1 - v if f else v


def _win_half(ref, h):
    return ref.at[pl.ds(h * (D // 2), D // 2), :]


def _sq_half(ref, h):
    return ref.at[:, pl.ds(h * (D // 2), D // 2)]


def _gather_copy(part, k, to, send_sems, recv_sems):
    return pltpu.make_async_remote_copy(src_ref=part, dst_ref=part, send_sem=send_sems.at[k], recv_sem=recv_sems.at[k],
                                        device_id=to, device_id_type=MESH)


def _gather_start(out, half, send_sems, recv_sems):
    x, y, c = _place()
    for k, (fx, fy) in enumerate(CHIP_FLIPS):
        _gather_copy(half(out.at[2 * x + y], c), k, (_flip(x, fx), _flip(y, fy), c), send_sems, recv_sems).start()


def _gather_land(out, half, k, send_sems, recv_sems):
    x, y, c = _place()
    sib = (x, y, 1 - c)
    fx, fy = CHIP_FLIPS[k]
    slot = out.at[2 * _flip(x, fx) + _flip(y, fy)]
    _gather_copy(half(slot, c), k, sib, send_sems, recv_sems).wait_recv()
    _gather_copy(half(slot, c), 3 + k, sib, send_sems, recv_sems).start()
    _gather_copy(half(slot, 1 - c), 3 + k, sib, send_sems, recv_sems).wait_recv()


def _gather_drain(out, half, send_sems, recv_sems):
    x, y, c = _place()
    for k, (fx, fy) in enumerate(CHIP_FLIPS):
        _gather_copy(half(out.at[2 * x + y], c), k, (_flip(x, fx), _flip(y, fy), c), send_sems, recv_sems).wait_send()
        _gather_copy(half(out.at[2 * _flip(x, fx) + _flip(y, fy)], c), 3 + k, (x, y, 1 - c), send_sems, recv_sems).wait_send()


def _gather_finish(out, half, send_sems, recv_sems):
    for k in range(len(CHIP_FLIPS)):
        _gather_land(out, half, k, send_sems, recv_sems)
    _gather_drain(out, half, send_sems, recv_sems)


def _swap_halves(gwin, gsq):
    def body(gwin_ref, gsq_ref, win_got, sq_got, send_sems, recv_sems):
        x, y, c = _place()
        sib = (x, y, 1 - c)
        pairs = ((_win_half(gwin_ref, 1 - c), win_got),
                 (gsq_ref.at[:, :, pl.ds((1 - c) * (D // 2), D // 2)], sq_got))
        copies = [pltpu.make_async_remote_copy(src_ref=src, dst_ref=dst, send_sem=send_sems.at[a], recv_sem=recv_sems.at[a],
                                               device_id=sib, device_id_type=MESH) for a, (src, dst) in enumerate(pairs)]
        for cp in copies:
            cp.start()
        for cp in copies:
            cp.wait()

    return pl.pallas_call(
        body, name="swap_halves",
        in_specs=[HBM_SPEC, HBM_SPEC], out_specs=[HBM_SPEC, HBM_SPEC],
        out_shape=[jax.ShapeDtypeStruct((D // 2, D_IN), BF16), jax.ShapeDtypeStruct((3, D, D // 2), BF16)],
        scratch_shapes=[pltpu.SemaphoreType.DMA((2,)), pltpu.SemaphoreType.DMA((2,))],
    )(gwin, gsq)


def _add_halves(c_arr, gwin, gsq, win_got, sq_got):
    def body(c_ref, a_ref, b_ref, p_ref, q_ref, so_ref, sq_ref, sob_ref, sqb_ref):
        so = a_ref[...] + b_ref[...].astype(F32)
        sq = p_ref[...] + q_ref[...].astype(F32)
        so_ref[...] = so
        sq_ref[...] = sq
        sob_ref[...] = _bf(so)
        sqb_ref[...] = _bf(sq)

    steps = 8
    rows, sq_rows = (D // 2) // steps, D // steps
    win = lambda f: pl.BlockSpec((rows, D_IN), f)
    sq = lambda f: pl.BlockSpec((3, sq_rows, D // 2), f)
    return pl.pallas_call(
        body, name="add_halves",
        grid_spec=pltpu.PrefetchScalarGridSpec(
            num_scalar_prefetch=1, grid=(steps,),
            in_specs=[win(lambda i, c: (c[0] * steps + i, 0)), win(lambda i, c: (i, 0)),
                      sq(lambda i, c: (0, i, c[0])), sq(lambda i, c: (0, i, 0))],
            out_specs=[win(lambda i, c: (i, 0)), sq(lambda i, c: (0, i, 0))] * 2),
        out_shape=[jax.ShapeDtypeStruct((D // 2, D_IN), F32), jax.ShapeDtypeStruct((3, D, D // 2), F32),
                   jax.ShapeDtypeStruct((D // 2, D_IN), BF16), jax.ShapeDtypeStruct((3, D, D // 2), BF16)],
        compiler_params=_params(("arbitrary",)),
    )(c_arr, gwin, win_got, gsq, sq_got)


def _sum_chips(jc_arr, swin, ssq, win_got, sq_got):
    def body(jc_ref, a_ref, b_ref, p_ref, q_ref, so_ref, sq_ref):
        so_ref[...] = ((a_ref[...] + b_ref[0].astype(F32)) + b_ref[1].astype(F32)) + b_ref[2].astype(F32)
        sq_ref[...] = ((p_ref[...] + q_ref[0].astype(F32)) + q_ref[1].astype(F32)) + q_ref[2].astype(F32)

    rows = 128
    steps = (D // 2) // rows
    sq_rows = SQ_ROWS // steps
    return pl.pallas_call(
        body, name="sum_chips",
        grid_spec=pltpu.PrefetchScalarGridSpec(
            num_scalar_prefetch=1, grid=(steps,),
            in_specs=[pl.BlockSpec((rows, SHARD_W), lambda i, jc: (i, jc[0])),
                      pl.BlockSpec((3, rows, SHARD_W), lambda i, jc: (0, i, 0)),
                      pl.BlockSpec((3, sq_rows, D // 2), lambda i, jc: (0, jc[0] * steps + i, 0)),
                      pl.BlockSpec((3, 3, sq_rows, D // 2), lambda i, jc: (0, 0, i, 0))],
            out_specs=[pl.BlockSpec((rows, SHARD_W), lambda i, jc: (jc[1] * steps + i, 0)),
                       pl.BlockSpec((3, sq_rows, D // 2), lambda i, jc: (0, i, jc[1]))]),
        out_shape=[jax.ShapeDtypeStruct((D, SHARD_W), F32), jax.ShapeDtypeStruct((3, SQ_ROWS, D), F32)],
        compiler_params=_params(("arbitrary",)),
    )(jc_arr, swin, win_got, ssq, sq_got)


def _join_halves(g_win, g_sq):
    def body(win_in, sq_in, win_out, sq_out, send_sems, recv_sems):
        del win_in, sq_in
        x, y, c = _place()
        sib = (x, y, 1 - c)

        def halves(h):
            return _win_half(win_out, h), sq_out.at[:, :, pl.ds(h * (D // 2), D // 2)]

        def copy(a, part):
            return pltpu.make_async_remote_copy(src_ref=part, dst_ref=part, send_sem=send_sems.at[a], recv_sem=recv_sems.at[a],
                                                device_id=sib, device_id_type=MESH)

        sent = [copy(a, part) for a, part in enumerate(halves(c))]
        for cp in sent:
            cp.start()
        for a, part in enumerate(halves(1 - c)):
            copy(a, part).wait_recv()
        for cp in sent:
            cp.wait_send()

    return pl.pallas_call(
        body, name="join_halves",
        in_specs=[HBM_SPEC, HBM_SPEC], out_specs=[HBM_SPEC, HBM_SPEC], input_output_aliases={0: 0, 1: 1},
        out_shape=[jax.ShapeDtypeStruct((D, SHARD_W), F32), jax.ShapeDtypeStruct((3, SQ_ROWS, D), F32)],
        scratch_shapes=[pltpu.SemaphoreType.DMA((2,)), pltpu.SemaphoreType.DMA((2,))],
    )(g_win, g_sq)


def _allreduce_small(vec):
    def body(vec_ref, out_ref, slots, send_sems, recv_sems):
        x, y, c = _place()
        me = 4 * x + 2 * y + c
        slots[me] = vec_ref[...]
        copies = []
        for k in range(1, 8):
            fx, fy, fc = (k >> 2) & 1, (k >> 1) & 1, k & 1
            copies.append(pltpu.make_async_remote_copy(
                src_ref=vec_ref, dst_ref=slots.at[me], send_sem=send_sems.at[k - 1], recv_sem=recv_sems.at[k - 1],
                device_id=(_flip(x, fx), _flip(y, fy), _flip(c, fc)), device_id_type=MESH))
        for cp in copies:
            cp.start()
        for k in range(1, 8):
            fx, fy, fc = (k >> 2) & 1, (k >> 1) & 1, k & 1
            src = 4 * _flip(x, fx) + 2 * _flip(y, fy) + _flip(c, fc)
            pltpu.make_async_remote_copy(src_ref=vec_ref, dst_ref=slots.at[src], send_sem=send_sems.at[k - 1],
                                         recv_sem=recv_sems.at[k - 1], device_id=(x, y, c), device_id_type=MESH).wait_recv()
        for cp in copies:
            cp.wait_send()
        total = slots[0]
        for s in range(1, 8):
            total = total + slots[s]
        out_ref[...] = total

    return pl.pallas_call(
        body, name="allreduce_small",
        in_specs=[pl.BlockSpec(memory_space=pltpu.VMEM)], out_specs=pl.BlockSpec(memory_space=pltpu.VMEM),
        out_shape=jax.ShapeDtypeStruct((8, D), F32),
        scratch_shapes=[pltpu.VMEM((8, 8, D), F32), pltpu.SemaphoreType.DMA((7,)), pltpu.SemaphoreType.DMA((7,))],
    )(vec)


def _adamw_math(w, g, m, v):
    m = ADAM_B1 * m + (1.0 - ADAM_B1) * g
    v = ADAM_B2 * v + (1.0 - ADAM_B2) * (g * g)
    m_hat = m / (1.0 - ADAM_B1 ** ADAM_STEP)
    v_hat = v / (1.0 - ADAM_B2 ** ADAM_STEP)
    delta = -ADAM_LR * (m_hat / (jnp.sqrt(v_hat) + ADAM_EPS) + ADAM_WD * w)
    return delta, m, v


def _adamw(name, w, g, m, v, rows):
    R, C = w.shape

    def body(w_ref, g_ref, m_ref, v_ref, d_out, m_out, v_out):
        d_out[...], m_out[...], v_out[...] = _adamw_math(w_ref[...], g_ref[...], m_ref[...], v_ref[...])

    spec = pl.BlockSpec((rows, C), lambda i: (i, 0))
    return pl.pallas_call(
        body, name=name, grid=(R // rows,), in_specs=[spec] * 4, out_specs=[spec] * 3,
        out_shape=[jax.ShapeDtypeStruct((R, C), F32)] * 3,
        compiler_params=_params(("parallel",)),
    )(w, g, m, v)


def _adamw_square(g_sq, ws, ms, vs):
    def body(g_ref, *refs):
        w_refs, m_refs, v_refs, outs = refs[0:3], refs[3:6], refs[6:9], refs[9:]
        for k in range(3):
            g = g_ref[k]
            outs[k][0] = g
            outs[3 + k][0], outs[6 + k][0], outs[9 + k][0] = _adamw_math(w_refs[k][0], g, m_refs[k][0], v_refs[k][0])

    out = pl.pallas_call(
        body, name="adamw_square", out_shape=[jax.ShapeDtypeStruct((1, SQ_ROWS, D), F32)] * 12,
        compiler_params=_params(),
    )(g_sq, *ws, *ms, *vs)
    return out[0:3], out[3:6], out[6:9], out[9:12]


def _small_update(total, lbw, w8, m8, v8):
    def body(t_ref, lbw_ref, w_ref, m_ref, v_ref, g_out, d_out, m_out, v_out):
        lb = 1.0 / (1.0 + jnp.exp(lbw_ref[1:2, :] - lbw_ref[0:1, :]))
        dlb = t_ref[2:3, :] * lb * (1.0 - lb)
        g_out[...] = jnp.zeros_like(g_out)
        g_out[0:1, :] = t_ref[3:4, :]
        g_out[1:2, :] = dlb
        g_out[2:3, :] = -dlb
        g_out[3:4, :] = t_ref[1:2, :]
        g_out[4:5, :] = t_ref[0:1, :]
        g_out[5:6, :] = t_ref[4:5, :]
        d_out[...], m_out[...], v_out[...] = _adamw_math(w_ref[...], g_out[...], m_ref[...], v_ref[...])

    return pl.pallas_call(
        body, name="small_update", out_shape=[jax.ShapeDtypeStruct((8, D), F32)] * 4,
        compiler_params=_params(),
    )(total, lbw, w8, m8, v8)


def _pack8(norm_w, lbw, hnw, fnw, sinks):
    pad = jnp.zeros((1, D - 16), F32)
    return jnp.concatenate([norm_w, lbw, hnw, fnw.reshape(1, D), jnp.concatenate([sinks, pad], axis=1),
                            jnp.zeros((2, D), F32)], axis=0)


def _unpack8(a):
    return a[0:1], a[1:3], a[3:4], a[5:6, 0:16], a[4]


def _local_step(order_arr, x, tgt, norm_w, lbw, hnw, sinks, fnw, win_mine, wsq_mine, exchange):
    proj, xnt_bf, win_bf = _fwd_proj(order_arr, x, norm_w, win_mine)
    oh, states, wsq_all = _hgrn_fwd(proj, lbw, wsq_mine)
    wsq_bf = wsq_all.reshape(SHARDS, 3, SQ_ROWS, D).transpose(1, 0, 2, 3).reshape(3, D, D)
    oa = _attn_fwd(proj, sinks)
    dx2, doh, doa, dhg, dtail, lhs, rhs, loss8, vec_mid = _mid(x, tgt, proj, oh, oa, hnw, fnw.reshape(1, D), wsq_bf)
    gsq, gsq_b = _wgrad_square(lhs, rhs)
    dhead, dlb = _hgrn_bwd(proj, lbw, states, doh)
    daq, dakv, dsink = _attn_bwd(proj, sinks, oa, doa)
    pieces = [dhead, dhg, daq, dakv, dtail]
    sums = exchange(*_wgrad_in(xnt_bf, pieces), gsq, gsq_b)
    wt_bf = win_bf.transpose(0, 2, 1).reshape(D_IN, D)
    grad_x, gnw, win_got, sq_got = _bwd_dx(pieces, wt_bf, x, norm_w, dx2, sums[2], sums[3])
    sink_row = jnp.concatenate([dsink[:, 0].reshape(1, 16), jnp.zeros((1, D - 16), F32)], axis=1)
    loss_row = jnp.broadcast_to(loss8[0:1, 0:1], (1, D))
    vec = jnp.concatenate([vec_mid[0:2], dlb, gnw, sink_row, loss_row, jnp.zeros((2, D), F32)], axis=0)
    return grad_x, sums, (win_got, sq_got), vec


def kernel(x, norm_w, w_in, hgrn_lower_bound, hgrn_norm_w, w_branch_hgrn, attn_sinks, w_branch_attn, w_out, final_norm_w, loss_target, m_norm_w, m_w_in, m_hgrn_lower_bound, m_hgrn_norm_w, m_w_branch_hgrn, m_attn_sinks, m_w_branch_attn, m_w_out, m_final_norm_w, v_norm_w, v_w_in, v_hgrn_lower_bound, v_hgrn_norm_w, v_w_branch_hgrn, v_attn_sinks, v_w_branch_attn, v_w_out, v_final_norm_w):
    c_arr = lax.axis_index("c").astype(jnp.int32).reshape(1)
    j_arr = (2 * lax.axis_index("x") + lax.axis_index("y")).astype(jnp.int32).reshape(1)
    jc_arr = jnp.concatenate([j_arr, c_arr])

    win_mine, wsq_mine = _cast_shards(j_arr, w_in[0], w_branch_hgrn[0], w_branch_attn[0], w_out[0])
    xi, yi = lax.axis_index("x"), lax.axis_index("y")
    order_arr = jnp.stack([2 * xi + yi] + [2 * _flip(xi, fx) + _flip(yi, fy) for fx, fy in CHIP_FLIPS]).astype(jnp.int32)

    def chip_sums(gwin, gwin_b, gsq, gsq_b):
        return _add_halves(c_arr, gwin, gsq, *_swap_halves(gwin_b, gsq_b))

    grad_x, (swin, ssq, _, _), arrived, vec = _local_step(
        order_arr, x[0], loss_target[0], norm_w, hgrn_lower_bound, hgrn_norm_w, attn_sinks, final_norm_w, win_mine, wsq_mine,
        chip_sums)
    g_win, g_sq = _join_halves(*_sum_chips(jc_arr, swin, ssq, *arrived))

    d_win, nm_win, nv_win = _adamw("adamw_w_in", w_in[0], g_win, m_w_in[0], v_w_in[0], 128)
    g_sqs, d_sqs, nm_sqs, nv_sqs = _adamw_square(
        g_sq, (w_branch_hgrn, w_branch_attn, w_out), (m_w_branch_hgrn, m_w_branch_attn, m_w_out),
        (v_w_branch_hgrn, v_w_branch_attn, v_w_out))

    total = _allreduce_small(vec)
    loss = total[5, 0]
    g8, d8, nm8, nv8 = _small_update(
        total, hgrn_lower_bound,
        _pack8(norm_w, hgrn_lower_bound, hgrn_norm_w, final_norm_w, attn_sinks),
        _pack8(m_norm_w, m_hgrn_lower_bound, m_hgrn_norm_w, m_final_norm_w, m_attn_sinks),
        _pack8(v_norm_w, v_hgrn_lower_bound, v_hgrn_norm_w, v_final_norm_w, v_attn_sinks))

    def assemble(win, sq, small):
        nw, lb, hn, sk, fn = _unpack8(small)
        return (nw, win.reshape(1, D, SHARD_W), lb, hn, sq[0], sk, sq[1], sq[2], fn)

    return (loss, grad_x.reshape(1, -1, D),
            *assemble(g_win, g_sqs, g8), *assemble(d_win, d_sqs, d8),
            *assemble(nm_win, nm_sqs, nm8), *assemble(nv_win, nv_sqs, nv8))
```

```python
import functools

import jax
import jax.numpy as jnp
from jax import lax
from jax.experimental import pallas as pl
from jax.experimental.pallas import tpu as pltpu

F32 = jnp.float32
BF16 = jnp.bfloat16

D = 1024
D_IN = 8704
SHARDS = 4
SHARD_W = D_IN // SHARDS
SQ_ROWS = D // SHARDS
HEADS = 8
HEAD_W = 128
CHUNK = 64
SUB = 8
ATT_BLOCK = 128
ATT_STEP = 4
KV_HEADS = 4
HEAD_DIM = 64
EPS = 1e-6
NEG = -1e30
SCALE = HEAD_DIM ** -0.5
COL_HG, COL_AQ, COL_AK, COL_AV, COL_AG, COL_MH, COL_MA = 3072, 4096, 5120, 5376, 5632, 6656, 7680

ADAM_LR, ADAM_B1, ADAM_B2, ADAM_EPS, ADAM_WD, ADAM_STEP = 0.001, 0.9, 0.999, 1e-08, 0.01, 10

VMEM_LIMIT = 56 * 1024 * 1024
MESH = pl.DeviceIdType.MESH
HBM_SPEC = pl.BlockSpec(memory_space=pltpu.HBM)
CHIP_FLIPS = ((1, 0), (0, 1), (1, 1))


def _dot(a, b):
    return jnp.dot(a, b, preferred_element_type=F32)


def _dot_nt(a, b):
    return lax.dot_general(a, b, (((1,), (1,)), ((), ())), preferred_element_type=F32)


def _dot_tn(a, b):
    return lax.dot_general(a, b, (((0,), (0,)), ((), ())), preferred_element_type=F32)


def _sigmoid(v):
    return 1.0 / (1.0 + jnp.exp(-v))


def _bf(v):
    return v.astype(BF16)


def _tri_dot2(tri, v):
    a = _bf(v)
    return _dot(tri, a) + _dot(tri, _bf(v - a.astype(F32)))


def _params(sem=None):
    return pltpu.CompilerParams(dimension_semantics=sem, vmem_limit_bytes=VMEM_LIMIT)


def _cast_shards(j_arr, win_s, wbh_s, wba_s, wout_s):
    steps = 4
    rows = D // steps

    def body(j_ref, win_ref, a_ref, b_ref, c_ref, win_o, sq_o):
        win_o[...] = _bf(win_ref[...])

        @pl.when(pl.program_id(0) == 0)
        def _():
            sq_o[0:SQ_ROWS, :] = _bf(a_ref[...])
            sq_o[SQ_ROWS:2 * SQ_ROWS, :] = _bf(b_ref[...])
            sq_o[2 * SQ_ROWS:3 * SQ_ROWS, :] = _bf(c_ref[...])

    whole = pl.BlockSpec((SQ_ROWS, D), lambda i, j: (0, 0))
    return pl.pallas_call(
        body, name="cast_shards",
        grid_spec=pltpu.PrefetchScalarGridSpec(
            num_scalar_prefetch=1, grid=(steps,),
            in_specs=[pl.BlockSpec((rows, SHARD_W), lambda i, j: (i, 0)), whole, whole, whole],
            out_specs=[pl.BlockSpec((None, rows, SHARD_W), lambda i, j: (j[0], i, 0)),
                       pl.BlockSpec((None, 3 * SQ_ROWS, D), lambda i, j: (j[0], 0, 0))]),
        out_shape=[jax.ShapeDtypeStruct((SHARDS, D, SHARD_W), BF16), jax.ShapeDtypeStruct((SHARDS, 3 * SQ_ROWS, D), BF16)],
        compiler_params=_params(("arbitrary",)),
    )(j_arr, win_s, wbh_s, wba_s, wout_s)


def _fwd_proj(order_arr, x, norm_w, win_all):
    T = x.shape[0]
    tm = min(512, T)
    nt = T // tm

    def body(order_ref, x_ref, nw_ref, win_in, proj_ref, xn_ref, win_out, w_scr, xn_scr, sems, send_sems, recv_sems):
        del win_in
        p, i = pl.program_id(0), pl.program_id(1)

        def load(n):
            return pltpu.make_async_copy(win_out.at[order_ref[n]], w_scr.at[n % 2], sems.at[n % 2])

        @pl.when((p == 0) & (i == 0))
        def _():
            _gather_start(win_out, _win_half, send_sems, recv_sems)
            load(0).start()
            load(0).wait()

        @pl.when((p == 1) & (i == 0))
        def _():
            _gather_land(win_out, _win_half, 0, send_sems, recv_sems)
            load(1).start()
            load(1).wait()

        for k in range(1, SHARDS - 1):
            @pl.when((p == k) & (i == nt // 2))
            def _():
                _gather_land(win_out, _win_half, k, send_sems, recv_sems)
                load(k + 1).start()

            @pl.when((p == k + 1) & (i == 0))
            def _():
                load(k + 1).wait()

        @pl.when(p == 0)
        def _():
            xf = x_ref[...]
            rs = lax.rsqrt(jnp.mean(xf * xf, axis=1, keepdims=True) + EPS)
            xn = _bf((xf * rs) * nw_ref[...])
            xn_scr[i] = xn
            xn_ref[...] = xn.T

        proj_ref[...] = _dot(xn_scr[i], w_scr[p % 2])

        @pl.when((p == SHARDS - 1) & (i == nt - 1))
        def _():
            _gather_drain(win_out, _win_half, send_sems, recv_sems)

    first = lambda p, i: jnp.where(p == 0, i, nt - 1)
    return pl.pallas_call(
        body, name="fwd_proj",
        grid_spec=pltpu.PrefetchScalarGridSpec(
            num_scalar_prefetch=1, grid=(SHARDS, nt),
            in_specs=[pl.BlockSpec((tm, D), lambda p, i, order: (first(p, i), 0)),
                      pl.BlockSpec((1, D), lambda p, i, order: (0, 0)), HBM_SPEC],
            out_specs=[pl.BlockSpec((tm, SHARD_W), lambda p, i, order: (i, order[p])),
                       pl.BlockSpec((D, tm), lambda p, i, order: (0, first(p, i))),
                       HBM_SPEC],
            scratch_shapes=[pltpu.VMEM((2, D, SHARD_W), BF16), pltpu.VMEM((nt, tm, D), BF16), pltpu.SemaphoreType.DMA((2,)),
                            pltpu.SemaphoreType.DMA((6,)), pltpu.SemaphoreType.DMA((6,))]),
        out_shape=[jax.ShapeDtypeStruct((T, D_IN), F32), jax.ShapeDtypeStruct((D, T), BF16),
                   jax.ShapeDtypeStruct((SHARDS, D, SHARD_W), BF16)],
        input_output_aliases={3: 2},
        compiler_params=_params(("arbitrary", "arbitrary")),
    )(order_arr, x, norm_w, win_all)


def _hgrn_gates(hq_ref, hf_ref, lbw_ref, b_scr):
    lb = 1.0 / (1.0 + jnp.exp(lbw_ref[1:2, :] - lbw_ref[0:1, :]))
    hf = hf_ref[...]
    sig = _sigmoid(hf)
    f = lb + (1.0 - lb) * sig
    g = jnp.log(f)
    hq = hq_ref[...]
    sq = _sigmoid(hq)
    q = hq * sq
    row = lax.broadcasted_iota(jnp.int32, (CHUNK, CHUNK), 0)
    col = lax.broadcasted_iota(jnp.int32, (CHUNK, CHUNK), 1)
    causal = row >= col
    b = _tri_dot2(jnp.where(causal, 1.0, 0.0).astype(BF16), g)
    b_scr[...] = b
    bc = b_scr[CHUNK - 1:CHUNK, :]
    r = b_scr[CHUNK // 2 - 1:CHUNK // 2, :]
    return dict(lb=lb, sig=sig, f=f, k=1.0 - f, hq=hq, sq=sq, q=q, b=b, bc=bc, r=r, causal=causal)


def _hgrn_fwd(proj, lbw, wsq_all):
    T = proj.shape[0]
    n = T // CHUNK

    def body(hq_ref, hf_ref, hi_ref, lbw_ref, wsq_in, o_ref, st_ref, wsq_out, s_scr, b_scr, send_sems, recv_sems):
        del wsq_in

        @pl.when(pl.program_id(0) == 0)
        def _():
            _gather_start(wsq_out, _sq_half, send_sems, recv_sems)
            s_scr[...] = jnp.zeros_like(s_scr)

        for c in range(SUB):
            rows = pl.ds(c * CHUNK, CHUNK)
            gt = _hgrn_gates(hq_ref.at[rows, :], hf_ref.at[rows, :], lbw_ref, b_scr.at[rows, :])
            b, bc, r, q, k = gt["b"], gt["bc"], gt["r"], gt["q"], gt["k"]
            qe = _bf(q * jnp.exp(b))
            qr = _bf(q * jnp.exp(b - r))
            kr = _bf(k * jnp.exp(r - b))
            kl = _bf(k * jnp.exp(bc - b))
            ebc = jnp.exp(bc)
            v = _bf(hi_ref[rows, :])
            scores = [_bf(jnp.where(gt["causal"], _dot_nt(qr[:, h * HEAD_W:(h + 1) * HEAD_W], kr[:, h * HEAD_W:(h + 1) * HEAD_W]), 0.0))
                      for h in range(HEADS)]
            for h in range(HEADS):
                sl = slice(h * HEAD_W, (h + 1) * HEAD_W)
                st = s_scr[h]
                st_ref[c, h] = st
                o_ref[rows, sl] = _dot(scores[h], v[:, sl]) + _dot_nt(qe[:, sl], _bf(st))
                s_scr[h] = ebc[:, sl] * st + _dot_tn(v[:, sl], kl[:, sl])

        @pl.when(pl.program_id(0) == n // SUB - 1)
        def _():
            _gather_finish(wsq_out, _sq_half, send_sems, recv_sems)

    col = lambda j: pl.BlockSpec((SUB * CHUNK, D), lambda i: (i, j))
    return pl.pallas_call(
        body, name="hgrn_fwd", grid=(n // SUB,),
        in_specs=[col(0), col(1), col(2), pl.BlockSpec((2, D), lambda i: (0, 0)), HBM_SPEC],
        out_specs=[pl.BlockSpec((SUB * CHUNK, D), lambda i: (i, 0)),
                   pl.BlockSpec((SUB, HEADS, HEAD_W, HEAD_W), lambda i: (i, 0, 0, 0)), HBM_SPEC],
        out_shape=[jax.ShapeDtypeStruct((T, D), F32), jax.ShapeDtypeStruct((n, HEADS, HEAD_W, HEAD_W), F32),
                   jax.ShapeDtypeStruct((SHARDS, 3 * SQ_ROWS, D), BF16)],
        input_output_aliases={4: 2},
        scratch_shapes=[pltpu.VMEM((HEADS, HEAD_W, HEAD_W), F32), pltpu.VMEM((SUB * CHUNK, D), F32),
                        pltpu.SemaphoreType.DMA((6,)), pltpu.SemaphoreType.DMA((6,))],
        compiler_params=_params(("arbitrary",)),
    )(proj, proj, proj, lbw, wsq_all)


def _hgrn_bwd(proj, lbw, states, do):
    T = proj.shape[0]
    n = T // CHUNK

    def body(hq_ref, hf_ref, hi_ref, lbw_ref, st_ref, do_ref, dp_ref, dlb_ref,
             ds_scr, b_scr, dq_scr, dk_scr, dv_scr, late_scr, early_scr, ex_scr):
        @pl.when(pl.program_id(0) == 0)
        def _():
            ds_scr[...] = jnp.zeros_like(ds_scr)
            dlb_ref[...] = jnp.zeros_like(dlb_ref)

        for c in reversed(range(SUB)):
            rows = pl.ds(c * CHUNK, CHUNK)
            gt = _hgrn_gates(hq_ref.at[rows, :], hf_ref.at[rows, :], lbw_ref, b_scr.at[rows, :])
            b, bc, r, q, k = gt["b"], gt["bc"], gt["r"], gt["q"], gt["k"]
            eb = jnp.exp(b)
            er = jnp.exp(b - r)
            erk = jnp.exp(r - b)
            el = jnp.exp(bc - b)
            ebc = jnp.exp(bc)
            qe, qr, kr, kl = _bf(q * eb), _bf(q * er), _bf(k * erk), _bf(k * el)
            v = _bf(hi_ref[rows, :])
            do_b = do_ref[rows, :]
            do_t = do_b.T
            causal_t = lax.broadcasted_iota(jnp.int32, (CHUNK, CHUNK), 0) <= lax.broadcasted_iota(jnp.int32, (CHUNK, CHUNK), 1)
            firsts = []
            for h in range(HEADS):
                sl = slice(h * HEAD_W, (h + 1) * HEAD_W)
                firsts.append((_bf(jnp.where(causal_t, _dot_nt(kr[:, sl], qr[:, sl]), 0.0)),
                               _bf(jnp.where(gt["causal"], _dot_nt(do_b[:, sl], v[:, sl]), 0.0)),
                               _bf(jnp.where(causal_t, _dot_nt(v[:, sl], do_b[:, sl]), 0.0))))
            for h in range(HEADS):
                sl = slice(h * HEAD_W, (h + 1) * HEAD_W)
                st0 = st_ref[c, h]
                dst = ds_scr[h]
                dst_b = _bf(dst)
                a_t, da, da_t = firsts[h]
                mq = _dot(da, kr[:, sl])
                mk = _dot(da_t, qr[:, sl])
                dq_in = eb[:, sl] * _dot(do_b[:, sl], _bf(st0))
                dk_in = el[:, sl] * _dot(v[:, sl], dst_b)
                dq_scr[rows, sl] = er[:, sl] * mq + dq_in
                dk_scr[rows, sl] = erk[:, sl] * mk + dk_in
                dv_scr[rows, sl] = _dot(a_t, do_b[:, sl]) + _dot_nt(kl[:, sl], dst_b)
                late_scr[rows, sl] = q[:, sl] * dq_in + qr[:, sl].astype(F32) * mq - kr[:, sl].astype(F32) * mk
                early_scr[rows, sl] = k[:, sl] * dk_in
                ex_scr[:, sl] = jnp.sum(dst * st0, axis=0, keepdims=True)
                ds_scr[h] = ebc[:, sl] * dst + _dot(do_t[sl, :], qe[:, sl])

            dq, dk = dq_scr[rows, :], dk_scr[rows, :]
            row = lax.broadcasted_iota(jnp.int32, (CHUNK, CHUNK), 0)
            col = lax.broadcasted_iota(jnp.int32, (CHUNK, CHUNK), 1)
            at_or_after = jnp.where(col >= row, 1.0, 0.0).astype(BF16)
            before = jnp.where(col < row, 1.0, 0.0).astype(BF16)
            dg = _tri_dot2(jnp.concatenate([at_or_after, before], axis=1),
                           jnp.concatenate([late_scr[rows, :], early_scr[rows, :]], axis=0)) + ebc * ex_scr[...]
            df = dg / gt["f"] - dk
            sig, sq, hq, lb = gt["sig"], gt["sq"], gt["hq"], gt["lb"]
            dp_ref[rows, 0:D] = _bf(dq * (sq * (1.0 + hq * (1.0 - sq))))
            dp_ref[rows, D:2 * D] = _bf(df * (1.0 - lb) * sig * (1.0 - sig))
            dp_ref[rows, 2 * D:3 * D] = _bf(dv_scr[rows, :])
            dlb_ref[...] += jnp.sum(df * (1.0 - sig), axis=0, keepdims=True)

    ns = n // SUB
    col = lambda j: pl.BlockSpec((SUB * CHUNK, D), lambda i: (ns - 1 - i, j))
    return pl.pallas_call(
        body, name="hgrn_bwd", grid=(ns,),
        in_specs=[col(0), col(1), col(2), pl.BlockSpec((2, D), lambda i: (0, 0)),
                  pl.BlockSpec((SUB, HEADS, HEAD_W, HEAD_W), lambda i: (ns - 1 - i, 0, 0, 0)),
                  pl.BlockSpec((SUB * CHUNK, D), lambda i: (ns - 1 - i, 0))],
        out_specs=[pl.BlockSpec((SUB * CHUNK, 3 * D), lambda i: (ns - 1 - i, 0)),
                   pl.BlockSpec((1, D), lambda i: (0, 0))],
        out_shape=[jax.ShapeDtypeStruct((T, 3 * D), BF16), jax.ShapeDtypeStruct((1, D), F32)],
        scratch_shapes=[pltpu.VMEM((HEADS, HEAD_W, HEAD_W), F32)] + [pltpu.VMEM((SUB * CHUNK, D), F32)] * 6
                       + [pltpu.VMEM((1, D), F32)],
        compiler_params=_params(("arbitrary",)),
    )(proj, proj, proj, lbw, states, do)


def _attn_masks(blk):
    qi = lax.broadcasted_iota(jnp.int32, (ATT_BLOCK, 2 * ATT_BLOCK), 0)
    kj = lax.broadcasted_iota(jnp.int32, (ATT_BLOCK, 2 * ATT_BLOCK), 1)
    band = (kj > qi) & (kj <= qi + ATT_BLOCK)
    return band & ((blk > 0) | (kj >= ATT_BLOCK))


def _head_pair_operand(t, hp, low):
    mine = low if hp == 0 else jnp.logical_not(low)
    both = jnp.where(mine, t, pltpu.roll(t, HEAD_DIM, 1))
    return _bf(jnp.concatenate([jnp.where(low, both, 0.0), jnp.where(low, 0.0, both)], axis=0))


def _attn_probs(s, sink, valid):
    s = jnp.where(valid, s, NEG)
    m = jnp.maximum(jnp.max(s, axis=1, keepdims=True), sink)
    p = jnp.exp(s - m)
    es = jnp.exp(sink - m)
    inv = 1.0 / (jnp.sum(p, axis=1, keepdims=True) + es)
    return p * inv, es * inv


def _attn_fwd(proj, sinks):
    T = proj.shape[0]
    rows_step = ATT_STEP * ATT_BLOCK

    def body(sink_ref, q_ref, kp_ref, kc_ref, vp_ref, vc_ref, o_ref):
        low = lax.broadcasted_iota(jnp.int32, (1, 2 * HEAD_DIM), 1) < HEAD_DIM
        for sb in range(ATT_STEP):
            rows = slice(sb * ATT_BLOCK, (sb + 1) * ATT_BLOCK)
            before = slice((sb - 1) * ATT_BLOCK, sb * ATT_BLOCK)
            valid = _attn_masks(pl.program_id(0) * ATT_STEP + sb)
            kcat = jnp.concatenate([kp_ref[...] if sb == 0 else kc_ref[before, :], kc_ref[rows, :]], axis=0)
            vcat = jnp.concatenate([vp_ref[...] if sb == 0 else vc_ref[before, :], vc_ref[rows, :]], axis=0)
            for h in range(KV_HEADS):
                tl = slice((h // 2) * 128, (h // 2) * 128 + 128)
                mine = low if h % 2 == 0 else jnp.logical_not(low)
                kh = _bf(jnp.where(mine, kcat[:, tl], pltpu.roll(kcat[:, tl], HEAD_DIM, 1)))
                vh = _bf(jnp.where(mine, vcat[:, tl], pltpu.roll(vcat[:, tl], HEAD_DIM, 1)))
                for t in range(2):
                    ql = slice((2 * h + t) * 128, (2 * h + t) * 128 + 128)
                    q2 = q_ref[rows, ql] * SCALE
                    outs = []
                    for p in range(2):
                        qm = _bf(jnp.where(low if p == 0 else jnp.logical_not(low), q2, 0.0))
                        probs, _ = _attn_probs(_dot_nt(qm, kh), sink_ref[0, 4 * h + 2 * t + p], valid)
                        outs.append(_dot(_bf(probs), vh))
                    o_ref[rows, ql] = jnp.where(low, outs[0], outs[1])

    prev = lambda i: jnp.maximum(ATT_STEP * i - 1, 0)
    return pl.pallas_call(
        body, name="attn_fwd", grid=(T // rows_step,),
        in_specs=[pl.BlockSpec(memory_space=pltpu.SMEM),
                  pl.BlockSpec((rows_step, D), lambda i: (i, COL_AQ // D)),
                  pl.BlockSpec((ATT_BLOCK, 256), lambda i: (prev(i), COL_AK // 256)),
                  pl.BlockSpec((rows_step, 256), lambda i: (i, COL_AK // 256)),
                  pl.BlockSpec((ATT_BLOCK, 256), lambda i: (prev(i), COL_AV // 256)),
                  pl.BlockSpec((rows_step, 256), lambda i: (i, COL_AV // 256))],
        out_specs=pl.BlockSpec((rows_step, D), lambda i: (i, 0)),
        out_shape=jax.ShapeDtypeStruct((T, D), F32),
        compiler_params=_params(("arbitrary",)),
    )(sinks, proj, proj, proj, proj, proj)


def _attn_bwd(proj, sinks, o, do):
    T = proj.shape[0]
    nb = T // ATT_BLOCK
    W2 = 2 * ATT_BLOCK

    def body(sink_ref, q_ref, kp_ref, kc_ref, vp_ref, vc_ref, o_ref, do_ref,
             dq_ref, dkv_ref, dsink_ref, ck_scr, cv_scr, nk_scr, nv_scr):
        blk = pl.program_id(0)

        @pl.when(blk == 0)
        def _():
            ck_scr[...] = jnp.zeros_like(ck_scr)
            cv_scr[...] = jnp.zeros_like(cv_scr)
            dsink_ref[...] = jnp.zeros_like(dsink_ref)

        @pl.when(blk < nb)
        def _():
            valid = _attn_masks(blk)
            low = lax.broadcasted_iota(jnp.int32, (1, 2 * HEAD_DIM), 1) < HEAD_DIM
            kcat = jnp.concatenate([kp_ref[...], kc_ref[...]], axis=0)
            vcat = jnp.concatenate([vp_ref[...], vc_ref[...]], axis=0)
            for h in range(KV_HEADS):
                tl = slice((h // 2) * 128, (h // 2) * 128 + 128)
                kbd = _head_pair_operand(kcat[:, tl], h % 2, low)
                vbd = _head_pair_operand(vcat[:, tl], h % 2, low)
                dkbd = jnp.zeros((2 * W2, 128), F32)
                dvbd = jnp.zeros((2 * W2, 128), F32)
                tiles = []
                for t in range(2):
                    ql = slice((2 * h + t) * 128, (2 * h + t) * 128 + 128)
                    q2 = _bf(q_ref[:, ql] * SCALE)
                    do2_b = do_ref[:, ql]
                    doo = do2_b.astype(F32) * o_ref[:, ql]
                    dsum0 = jnp.sum(jnp.where(low, doo, 0.0), axis=1, keepdims=True)
                    dsum1 = jnp.sum(jnp.where(low, 0.0, doo), axis=1, keepdims=True)
                    tiles.append((ql, q2, do2_b, dsum0, dsum1, _dot_nt(q2, kbd), _dot_nt(do2_b, vbd)))
                grads = []
                for t, (ql, q2, do2_b, dsum0, dsum1, s2, dp2) in enumerate(tiles):
                    head = 4 * h + 2 * t
                    p0, ps0 = _attn_probs(s2[:, 0:W2], sink_ref[0, head], valid)
                    p1, ps1 = _attn_probs(s2[:, W2:2 * W2], sink_ref[0, head + 1], valid)
                    ds2 = _bf(jnp.concatenate([p0 * (dp2[:, 0:W2] - dsum0), p1 * (dp2[:, W2:2 * W2] - dsum1)], axis=1))
                    grads.append((ds2, _bf(jnp.concatenate([p0, p1], axis=1))))
                    dsink_ref[head:head + 1, :] += jnp.zeros((1, 128), F32) - jnp.sum(ps0 * dsum0, axis=0, keepdims=True)
                    dsink_ref[head + 1:head + 2, :] += jnp.zeros((1, 128), F32) - jnp.sum(ps1 * dsum1, axis=0, keepdims=True)
                for (ql, q2, do2_b, _, _, _, _), (ds2, p2) in zip(tiles, grads):
                    dq_ref[:, ql] = _bf(_dot(ds2, kbd) * SCALE)
                    dkbd = dkbd + _dot_tn(ds2, q2)
                    dvbd = dvbd + _dot_tn(p2, do2_b)
                dk2 = jnp.where(low, dkbd[0:W2], dkbd[W2:2 * W2])
                dv2 = jnp.where(low, dvbd[0:W2], dvbd[W2:2 * W2])
                dk2 = dk2 + pltpu.roll(dk2, HEAD_DIM, 1)
                dv2 = dv2 + pltpu.roll(dv2, HEAD_DIM, 1)
                if h % 2 == 0:
                    keep_k, keep_v = dk2, dv2
                else:
                    nk_scr[:, tl] = jnp.where(low, keep_k, dk2)
                    nv_scr[:, tl] = jnp.where(low, keep_v, dv2)
            dkv_ref[:, 0:256] = _bf(ck_scr[...] + nk_scr[0:ATT_BLOCK, :])
            dkv_ref[:, 256:512] = _bf(cv_scr[...] + nv_scr[0:ATT_BLOCK, :])
            ck_scr[...] = nk_scr[ATT_BLOCK:2 * ATT_BLOCK, :]
            cv_scr[...] = nv_scr[ATT_BLOCK:2 * ATT_BLOCK, :]

        @pl.when(blk == nb)
        def _():
            dkv_ref[:, 0:256] = _bf(ck_scr[...])
            dkv_ref[:, 256:512] = _bf(cv_scr[...])

    cur = lambda i: jnp.minimum(i, nb - 1)
    prev = lambda i: jnp.maximum(cur(i) - 1, 0)
    late = lambda i: jnp.maximum(i - 1, 0)
    return pl.pallas_call(
        body, name="attn_bwd", grid=(nb + 1,),
        in_specs=[pl.BlockSpec(memory_space=pltpu.SMEM),
                  pl.BlockSpec((ATT_BLOCK, D), lambda i: (cur(i), COL_AQ // D)),
                  pl.BlockSpec((ATT_BLOCK, 256), lambda i: (prev(i), COL_AK // 256)),
                  pl.BlockSpec((ATT_BLOCK, 256), lambda i: (cur(i), COL_AK // 256)),
                  pl.BlockSpec((ATT_BLOCK, 256), lambda i: (prev(i), COL_AV // 256)),
                  pl.BlockSpec((ATT_BLOCK, 256), lambda i: (cur(i), COL_AV // 256)),
                  pl.BlockSpec((ATT_BLOCK, D), lambda i: (cur(i), 0)),
                  pl.BlockSpec((ATT_BLOCK, D), lambda i: (cur(i), 0))],
        out_specs=[pl.BlockSpec((ATT_BLOCK, D), lambda i: (cur(i), 0)),
                   pl.BlockSpec((ATT_BLOCK, 512), lambda i: (late(i), 0)),
                   pl.BlockSpec((16, 128), lambda i: (0, 0))],
        out_shape=[jax.ShapeDtypeStruct((T, D), BF16), jax.ShapeDtypeStruct((T, 512), BF16),
                   jax.ShapeDtypeStruct((16, 128), F32)],
        scratch_shapes=[pltpu.VMEM((ATT_BLOCK, 256), F32), pltpu.VMEM((ATT_BLOCK, 256), F32),
                        pltpu.VMEM((2 * ATT_BLOCK, 256), F32), pltpu.VMEM((2 * ATT_BLOCK, 256), F32)],
        compiler_params=_params(("arbitrary",)),
    )(sinks, proj, proj, proj, proj, proj, o, do)


def _mid(x, tgt, proj, oh, oa, hnw, fnw, wsq_bf):
    T = x.shape[0]
    tm = min(256, T)
    nt = T // tm

    def body(x_ref, tgt_ref, oh_ref, oa_ref, hg_ref, ag0_ref, ag1_ref, mh0_ref, mh1_ref, ma0_ref, ma1_ref,
             hnw_ref, fnw_ref, w_hbm,
             dx2_ref, doh_ref, doa_ref, dhg_ref, dtail_ref, lhs_ref, rhs_ref, loss_ref, vec_ref,
             w_scr, xh_scr, rs_scr, sem):
        @pl.when(pl.program_id(0) == 0)
        def _():
            cp = pltpu.make_async_copy(w_hbm, w_scr, sem)
            cp.start()
            cp.wait()
            loss_ref[...] = jnp.zeros_like(loss_ref)
            vec_ref[...] = jnp.zeros_like(vec_ref)

        oh = oh_ref[...]
        for h in range(HEADS):
            sl = slice(h * HEAD_W, (h + 1) * HEAD_W)
            ohh = oh[:, sl]
            rs = lax.rsqrt(jnp.mean(ohh * ohh, axis=1, keepdims=True) + EPS)
            xh_scr[:, sl] = ohh * rs
            rs_scr[:, sl] = jnp.broadcast_to(rs, (tm, HEAD_W))
        xh = xh_scr[...]
        hnw = hnw_ref[...]
        on = xh * hnw
        hg = hg_ref[...]
        sg = _sigmoid(hg)
        silu_g = hg * sg
        gated_h = _bf(on * silu_g)
        oa = oa_ref[...]
        ag = jnp.concatenate([ag0_ref[...], ag1_ref[...]], axis=1)
        sa = _sigmoid(ag)
        silu_a = ag * sa
        gated_a = _bf(oa * silu_a)
        yh = _dot(gated_h, w_scr[0])
        ya = _dot(gated_a, w_scr[1])
        lhs_ref[0] = gated_h.T
        lhs_ref[1] = gated_a.T
        smh = _sigmoid(jnp.concatenate([mh0_ref[...], mh1_ref[...]], axis=1))
        sma = _sigmoid(jnp.concatenate([ma0_ref[...], ma1_ref[...]], axis=1))
        merged = _bf(smh * yh + sma * ya)
        lhs_ref[2] = merged.T
        x2 = x_ref[...] + _dot(merged, w_scr[2])
        rs2 = lax.rsqrt(jnp.mean(x2 * x2, axis=1, keepdims=True) + EPS)
        xh2 = x2 * rs2
        fnw = fnw_ref[...]
        diff = xh2 * fnw - tgt_ref[...]
        loss_ref[...] += jnp.zeros_like(loss_ref) + jnp.sum(diff * diff) * (0.5 / D)

        dy = diff * (1.0 / D)
        vec_ref[0:1, :] += jnp.sum(dy * xh2, axis=0, keepdims=True)
        gy = dy * fnw
        dx2 = rs2 * (gy - xh2 * jnp.mean(gy * xh2, axis=1, keepdims=True))
        dx2_ref[...] = dx2
        dx2_b = _bf(dx2)
        rhs_ref[2] = dx2_b
        dmerged = _dot_nt(dx2_b, w_scr[2])
        dyh = dmerged * smh
        dya = dmerged * sma
        dtail_ref[:, D:2 * D] = _bf(dyh * yh * (1.0 - smh))
        dtail_ref[:, 2 * D:3 * D] = _bf(dya * ya * (1.0 - sma))
        dyh_b, dya_b = _bf(dyh), _bf(dya)
        rhs_ref[0] = dyh_b
        rhs_ref[1] = dya_b
        dgh = _dot_nt(dyh_b, w_scr[0])
        dga = _dot_nt(dya_b, w_scr[1])
        don = dgh * silu_g
        dhg_ref[...] = _bf(dgh * on * (sg * (1.0 + hg * (1.0 - sg))))
        vec_ref[1:2, :] += jnp.sum(don * xh, axis=0, keepdims=True)
        gxh = don * hnw
        rsb = rs_scr[...]
        for h in range(HEADS):
            sl = slice(h * HEAD_W, (h + 1) * HEAD_W)
            gh, xhh = gxh[:, sl], xh[:, sl]
            doh_ref[:, sl] = _bf(rsb[:, sl] * (gh - xhh * jnp.mean(gh * xhh, axis=1, keepdims=True)))
        doa_ref[...] = _bf(dga * silu_a)
        dtail_ref[:, 0:D] = _bf(dga * oa * (sa * (1.0 + ag * (1.0 - sa))))

    row = lambda w, j: pl.BlockSpec((tm, w), lambda i: (i, j))
    const = lambda r, c: pl.BlockSpec((r, c), lambda i: (0, 0))
    stack = pl.BlockSpec((3, tm, D), lambda i: (0, i, 0))
    stack_t = pl.BlockSpec((3, D, tm), lambda i: (0, 0, i))
    return pl.pallas_call(
        body, name="mid", grid=(nt,),
        in_specs=[row(D, 0), row(D, 0), row(D, 0), row(D, 0), row(D, COL_HG // D),
                  row(512, COL_AG // 512), row(512, COL_AG // 512 + 1),
                  row(512, COL_MH // 512), row(512, COL_MH // 512 + 1),
                  row(512, COL_MA // 512), row(512, COL_MA // 512 + 1),
                  const(1, D), const(1, D), HBM_SPEC],
        out_specs=[row(D, 0), row(D, 0), row(D, 0), row(D, 0), row(3 * D, 0), stack_t, stack, const(8, 128), const(8, D)],
        out_shape=[jax.ShapeDtypeStruct((T, D), F32), jax.ShapeDtypeStruct((T, D), BF16), jax.ShapeDtypeStruct((T, D), BF16),
                   jax.ShapeDtypeStruct((T, D), BF16), jax.ShapeDtypeStruct((T, 3 * D), BF16),
                   jax.ShapeDtypeStruct((3, D, T), BF16), jax.ShapeDtypeStruct((3, T, D), BF16),
                   jax.ShapeDtypeStruct((8, 128), F32), jax.ShapeDtypeStruct((8, D), F32)],
        scratch_shapes=[pltpu.VMEM((3, D, D), BF16), pltpu.VMEM((tm, D), F32), pltpu.VMEM((tm, D), F32),
                        pltpu.SemaphoreType.DMA],
        compiler_params=_params(("arbitrary",)),
    )(x, tgt, oh, oa, proj, proj, proj, proj, proj, proj, proj, hnw, fnw, wsq_bf)


def _wgrad_square(lhs_t, rhs):
    T = rhs.shape[1]
    tk = min(2048, T)
    steps = T // tk

    def body(a_ref, b_ref, g_ref, gb_ref):
        part = _dot(a_ref[...], b_ref[...])

        @pl.when(pl.program_id(1) == 0)
        def _():
            g_ref[...] = part

        @pl.when(pl.program_id(1) > 0)
        def _():
            g_ref[...] += part

        @pl.when(pl.program_id(1) == steps - 1)
        def _():
            gb_ref[...] = _bf(g_ref[...])

    return pl.pallas_call(
        body, name="wgrad_square", grid=(3, steps),
        in_specs=[pl.BlockSpec((None, D, tk), lambda k, i: (k, 0, i)), pl.BlockSpec((None, tk, D), lambda k, i: (k, i, 0))],
        out_specs=[pl.BlockSpec((None, D, D), lambda k, i: (k, 0, 0))] * 2,
        out_shape=[jax.ShapeDtypeStruct((3, D, D), F32), jax.ShapeDtypeStruct((3, D, D), BF16)],
        compiler_params=_params(("parallel", "arbitrary")),
    )(lhs_t, rhs)


def _bwd_dx(pieces, wt_bf, x, norm_w, dx2, swin_b, ssq_b):
    T = x.shape[0]
    tm = min(512, T)
    nt = T // tm
    widths = [p.shape[1] for p in pieces]
    n_p = len(pieces)

    def body(*refs):
        piece_refs = refs[:n_p]
        (w_hbm, x_ref, nw_ref, dx2_ref, swin_ref, ssq_ref,
         gx_ref, gnw_ref, win_got, sq_got, w_scr, sem, send_sems, recv_sems) = refs[n_p:]

        def scatter_copies():
            x_, y_, c_ = _place()
            copies = []
            for k, (fx, fy) in enumerate(CHIP_FLIPS):
                px, py = _flip(x_, fx), _flip(y_, fy)
                jr = 2 * px + py
                for a, (src, dst) in enumerate(((swin_ref.at[:, pl.ds(jr * SHARD_W, SHARD_W)], win_got.at[k]),
                                                (ssq_ref.at[:, pl.ds(jr * SQ_ROWS, SQ_ROWS), :], sq_got.at[k]))):
                    copies.append(pltpu.make_async_remote_copy(
                        src_ref=src, dst_ref=dst, send_sem=send_sems.at[2 * k + a], recv_sem=recv_sems.at[2 * k + a],
                        device_id=(px, py, c_), device_id_type=MESH))
            return copies

        @pl.when(pl.program_id(0) == 0)
        def _():
            for cp in scatter_copies():
                cp.start()
            cp = pltpu.make_async_copy(w_hbm, w_scr, sem)
            cp.start()
            cp.wait()
            gnw_ref[...] = jnp.zeros_like(gnw_ref)

        dxn = None
        off = 0
        for ref, w in zip(piece_refs, widths):
            part = _dot(ref[...], w_scr[off:off + w, :])
            dxn = part if dxn is None else dxn + part
            off += w
        xf = x_ref[...]
        rs = lax.rsqrt(jnp.mean(xf * xf, axis=1, keepdims=True) + EPS)
        xh = xf * rs
        gnw_ref[...] += jnp.sum(dxn * xh, axis=0, keepdims=True)
        gx = dxn * nw_ref[...]
        gx_ref[...] = rs * (gx - xh * jnp.mean(gx * xh, axis=1, keepdims=True)) + dx2_ref[...]

        @pl.when(pl.program_id(0) == nt - 1)
        def _():
            for cp in scatter_copies():
                cp.wait()

    row = lambda w: pl.BlockSpec((tm, w), lambda i: (i, 0))
    return pl.pallas_call(
        body, name="bwd_dx", grid=(nt,),
        in_specs=[row(w) for w in widths] + [HBM_SPEC, row(D), pl.BlockSpec((1, D), lambda i: (0, 0)), row(D), HBM_SPEC, HBM_SPEC],
        out_specs=[row(D), pl.BlockSpec((1, D), lambda i: (0, 0)), HBM_SPEC, HBM_SPEC],
        out_shape=[jax.ShapeDtypeStruct((T, D), F32), jax.ShapeDtypeStruct((1, D), F32),
                   jax.ShapeDtypeStruct((3, D // 2, SHARD_W), BF16), jax.ShapeDtypeStruct((3, 3, SQ_ROWS, D // 2), BF16)],
        scratch_shapes=[pltpu.VMEM((D_IN, D), BF16), pltpu.SemaphoreType.DMA,
                        pltpu.SemaphoreType.DMA((6,)), pltpu.SemaphoreType.DMA((6,))],
        compiler_params=_params(("arbitrary",)),
    )(*pieces, wt_bf, x, norm_w, dx2, swin_b, ssq_b)


W_PIECES = ((0, 1024, 3), (COL_HG, 1024, 1), (COL_AQ, 1024, 1), (COL_AK, 512, 1), (COL_AG, 512, 6))


def _wgrad_in(xnt_bf, pieces):
    T = xnt_bf.shape[1]
    bufs = ()
    for n, (piece, (col, wb, blocks)) in enumerate(zip(pieces, W_PIECES)):
        tk = min(2048 if wb == 1024 else 4096, T)
        steps = T // tk

        def body(xnt_ref, p_ref, *rest):
            g_ref, gb_ref = rest[-2:]
            part = _dot(xnt_ref[...], p_ref[...])

            @pl.when(pl.program_id(1) == 0)
            def _():
                g_ref[...] = part

            @pl.when(pl.program_id(1) > 0)
            def _():
                g_ref[...] += part

            @pl.when(pl.program_id(1) == steps - 1)
            def _():
                gb_ref[...] = _bf(g_ref[...])

        out = pl.BlockSpec((D, wb), lambda jb, i, base=col // wb: (0, base + jb))
        bufs = pl.pallas_call(
            body, name=f"wgrad_in_{n}", grid=(blocks, steps),
            in_specs=[pl.BlockSpec((D, tk), lambda jb, i: (0, i)), pl.BlockSpec((tk, wb), lambda jb, i: (i, jb))]
                     + [HBM_SPEC] * len(bufs),
            out_specs=[out, out],
            out_shape=[jax.ShapeDtypeStruct((D, D_IN), F32), jax.ShapeDtypeStruct((D, D_IN), BF16)],
            input_output_aliases={2: 0, 3: 1} if bufs else {},
            compiler_params=_params(("parallel", "arbitrary")),
        )(xnt_bf, piece, *bufs)
    return bufs


def _place():
    return lax.axis_index("x"), lax.axis_index("y"), lax.axis_index("c")


def _flip(v, f):
    return 1 - v if f else v


def _win_half(ref, h):
    return ref.at[pl.ds(h * (D // 2), D // 2), :]


def _sq_half(ref, h):
    return ref.at[:, pl.ds(h * (D // 2), D // 2)]


def _gather_copy(part, k, to, send_sems, recv_sems):
    return pltpu.make_async_remote_copy(src_ref=part, dst_ref=part, send_sem=send_sems.at[k], recv_sem=recv_sems.at[k],
                                        device_id=to, device_id_type=MESH)


def _gather_start(out, half, send_sems, recv_sems):
    x, y, c = _place()
    for k, (fx, fy) in enumerate(CHIP_FLIPS):
        _gather_copy(half(out.at[2 * x + y], c), k, (_flip(x, fx), _flip(y, fy), c), send_sems, recv_sems).start()


def _gather_land(out, half, k, send_sems, recv_sems):
    x, y, c = _place()
    sib = (x, y, 1 - c)
    fx, fy = CHIP_FLIPS[k]
    slot = out.at[2 * _flip(x, fx) + _flip(y, fy)]
    _gather_copy(half(slot, c), k, sib, send_sems, recv_sems).wait_recv()
    _gather_copy(half(slot, c), 3 + k, sib, send_sems, recv_sems).start()
    _gather_copy(half(slot, 1 - c), 3 + k, sib, send_sems, recv_sems).wait_recv()


def _gather_drain(out, half, send_sems, recv_sems):
    x, y, c = _place()
    for k, (fx, fy) in enumerate(CHIP_FLIPS):
        _gather_copy(half(out.at[2 * x + y], c), k, (_flip(x, fx), _flip(y, fy), c), send_sems, recv_sems).wait_send()
        _gather_copy(half(out.at[2 * _flip(x, fx) + _flip(y, fy)], c), 3 + k, (x, y, 1 - c), send_sems, recv_sems).wait_send()


def _gather_finish(out, half, send_sems, recv_sems):
    for k in range(len(CHIP_FLIPS)):
        _gather_land(out, half, k, send_sems, recv_sems)
    _gather_drain(out, half, send_sems, recv_sems)


def _swap_halves(gwin, gsq):
    def body(gwin_ref, gsq_ref, win_got, sq_got, send_sems, recv_sems):
        x, y, c = _place()
        sib = (x, y, 1 - c)
        pairs = ((_win_half(gwin_ref, 1 - c), win_got),
                 (gsq_ref.at[:, :, pl.ds((1 - c) * (D // 2), D // 2)], sq_got))
        copies = [pltpu.make_async_remote_copy(src_ref=src, dst_ref=dst, send_sem=send_sems.at[a], recv_sem=recv_sems.at[a],
                                               device_id=sib, device_id_type=MESH) for a, (src, dst) in enumerate(pairs)]
        for cp in copies:
            cp.start()
        for cp in copies:
            cp.wait()

    return pl.pallas_call(
        body, name="swap_halves",
        in_specs=[HBM_SPEC, HBM_SPEC], out_specs=[HBM_SPEC, HBM_SPEC],
        out_shape=[jax.ShapeDtypeStruct((D // 2, D_IN), BF16), jax.ShapeDtypeStruct((3, D, D // 2), BF16)],
        scratch_shapes=[pltpu.SemaphoreType.DMA((2,)), pltpu.SemaphoreType.DMA((2,))],
    )(gwin, gsq)


def _add_halves(c_arr, gwin, gsq, win_got, sq_got):
    def body(c_ref, a_ref, b_ref, p_ref, q_ref, so_ref, sq_ref, sob_ref, sqb_ref):
        so = a_ref[...] + b_ref[...].astype(F32)
        sq = p_ref[...] + q_ref[...].astype(F32)
        so_ref[...] = so
        sq_ref[...] = sq
        sob_ref[...] = _bf(so)
        sqb_ref[...] = _bf(sq)

    steps = 8
    rows, sq_rows = (D // 2) // steps, D // steps
    win = lambda f: pl.BlockSpec((rows, D_IN), f)
    sq = lambda f: pl.BlockSpec((3, sq_rows, D // 2), f)
    return pl.pallas_call(
        body, name="add_halves",
        grid_spec=pltpu.PrefetchScalarGridSpec(
            num_scalar_prefetch=1, grid=(steps,),
            in_specs=[win(lambda i, c: (c[0] * steps + i, 0)), win(lambda i, c: (i, 0)),
                      sq(lambda i, c: (0, i, c[0])), sq(lambda i, c: (0, i, 0))],
            out_specs=[win(lambda i, c: (i, 0)), sq(lambda i, c: (0, i, 0))] * 2),
        out_shape=[jax.ShapeDtypeStruct((D // 2, D_IN), F32), jax.ShapeDtypeStruct((3, D, D // 2), F32),
                   jax.ShapeDtypeStruct((D // 2, D_IN), BF16), jax.ShapeDtypeStruct((3, D, D // 2), BF16)],
        compiler_params=_params(("arbitrary",)),
    )(c_arr, gwin, win_got, gsq, sq_got)


def _sum_chips(jc_arr, swin, ssq, win_got, sq_got):
    def body(jc_ref, a_ref, b_ref, p_ref, q_ref, so_ref, sq_ref):
        so_ref[...] = ((a_ref[...] + b_ref[0].astype(F32)) + b_ref[1].astype(F32)) + b_ref[2].astype(F32)
        sq_ref[...] = ((p_ref[...] + q_ref[0].astype(F32)) + q_ref[1].astype(F32)) + q_ref[2].astype(F32)

    rows = 128
    steps = (D // 2) // rows
    sq_rows = SQ_ROWS // steps
    return pl.pallas_call(
        body, name="sum_chips",
        grid_spec=pltpu.PrefetchScalarGridSpec(
            num_scalar_prefetch=1, grid=(steps,),
            in_specs=[pl.BlockSpec((rows, SHARD_W), lambda i, jc: (i, jc[0])),
                      pl.BlockSpec((3, rows, SHARD_W), lambda i, jc: (0, i, 0)),
                      pl.BlockSpec((3, sq_rows, D // 2), lambda i, jc: (0, jc[0] * steps + i, 0)),
                      pl.BlockSpec((3, 3, sq_rows, D // 2), lambda i, jc: (0, 0, i, 0))],
            out_specs=[pl.BlockSpec((rows, SHARD_W), lambda i, jc: (jc[1] * steps + i, 0)),
                       pl.BlockSpec((3, sq_rows, D // 2), lambda i, jc: (0, i, jc[1]))]),
        out_shape=[jax.ShapeDtypeStruct((D, SHARD_W), F32), jax.ShapeDtypeStruct((3, SQ_ROWS, D), F32)],
        compiler_params=_params(("arbitrary",)),
    )(jc_arr, swin, win_got, ssq, sq_got)


def _join_halves(g_win, g_sq):
    def body(win_in, sq_in, win_out, sq_out, send_sems, recv_sems):
        del win_in, sq_in
        x, y, c = _place()
        sib = (x, y, 1 - c)

        def halves(h):
            return _win_half(win_out, h), sq_out.at[:, :, pl.ds(h * (D // 2), D // 2)]

        def copy(a, part):
            return pltpu.make_async_remote_copy(src_ref=part, dst_ref=part, send_sem=send_sems.at[a], recv_sem=recv_sems.at[a],
                                                device_id=sib, device_id_type=MESH)

        sent = [copy(a, part) for a, part in enumerate(halves(c))]
        for cp in sent:
            cp.start()
        for a, part in enumerate(halves(1 - c)):
            copy(a, part).wait_recv()
        for cp in sent:
            cp.wait_send()

    return pl.pallas_call(
        body, name="join_halves",
        in_specs=[HBM_SPEC, HBM_SPEC], out_specs=[HBM_SPEC, HBM_SPEC], input_output_aliases={0: 0, 1: 1},
        out_shape=[jax.ShapeDtypeStruct((D, SHARD_W), F32), jax.ShapeDtypeStruct((3, SQ_ROWS, D), F32)],
        scratch_shapes=[pltpu.SemaphoreType.DMA((2,)), pltpu.SemaphoreType.DMA((2,))],
    )(g_win, g_sq)


def _allreduce_small(vec):
    def body(vec_ref, out_ref, slots, send_sems, recv_sems):
        x, y, c = _place()
        me = 4 * x + 2 * y + c
        slots[me] = vec_ref[...]
        copies = []
        for k in range(1, 8):
            fx, fy, fc = (k >> 2) & 1, (k >> 1) & 1, k & 1
            copies.append(pltpu.make_async_remote_copy(
                src_ref=vec_ref, dst_ref=slots.at[me], send_sem=send_sems.at[k - 1], recv_sem=recv_sems.at[k - 1],
                device_id=(_flip(x, fx), _flip(y, fy), _flip(c, fc)), device_id_type=MESH))
        for cp in copies:
            cp.start()
        for k in range(1, 8):
            fx, fy, fc = (k >> 2) & 1, (k >> 1) & 1, k & 1
            src = 4 * _flip(x, fx) + 2 * _flip(y, fy) + _flip(c, fc)
            pltpu.make_async_remote_copy(src_ref=vec_ref, dst_ref=slots.at[src], send_sem=send_sems.at[k - 1],
                                         recv_sem=recv_sems.at[k - 1], device_id=(x, y, c), device_id_type=MESH).wait_recv()
        for cp in copies:
            cp.wait_send()
        total = slots[0]
        for s in range(1, 8):
            total = total + slots[s]
        out_ref[...] = total

    return pl.pallas_call(
        body, name="allreduce_small",
        in_specs=[pl.BlockSpec(memory_space=pltpu.VMEM)], out_specs=pl.BlockSpec(memory_space=pltpu.VMEM),
        out_shape=jax.ShapeDtypeStruct((8, D), F32),
        scratch_shapes=[pltpu.VMEM((8, 8, D), F32), pltpu.SemaphoreType.DMA((7,)), pltpu.SemaphoreType.DMA((7,))],
    )(vec)


def _adamw_math(w, g, m, v):
    m = ADAM_B1 * m + (1.0 - ADAM_B1) * g
    v = ADAM_B2 * v + (1.0 - ADAM_B2) * (g * g)
    m_hat = m / (1.0 - ADAM_B1 ** ADAM_STEP)
    v_hat = v / (1.0 - ADAM_B2 ** ADAM_STEP)
    delta = -ADAM_LR * (m_hat / (jnp.sqrt(v_hat) + ADAM_EPS) + ADAM_WD * w)
    return delta, m, v


def _adamw(name, w, g, m, v, rows):
    R, C = w.shape

    def body(w_ref, g_ref, m_ref, v_ref, d_out, m_out, v_out):
        d_out[...], m_out[...], v_out[...] = _adamw_math(w_ref[...], g_ref[...], m_ref[...], v_ref[...])

    spec = pl.BlockSpec((rows, C), lambda i: (i, 0))
    return pl.pallas_call(
        body, name=name, grid=(R // rows,), in_specs=[spec] * 4, out_specs=[spec] * 3,
        out_shape=[jax.ShapeDtypeStruct((R, C), F32)] * 3,
        compiler_params=_params(("parallel",)),
    )(w, g, m, v)


def _adamw_square(g_sq, ws, ms, vs):
    def body(g_ref, *refs):
        w_refs, m_refs, v_refs, outs = refs[0:3], refs[3:6], refs[6:9], refs[9:]
        for k in range(3):
            g = g_ref[k]
            outs[k][0] = g
            outs[3 + k][0], outs[6 + k][0], outs[9 + k][0] = _adamw_math(w_refs[k][0], g, m_refs[k][0], v_refs[k][0])

    out = pl.pallas_call(
        body, name="adamw_square", out_shape=[jax.ShapeDtypeStruct((1, SQ_ROWS, D), F32)] * 12,
        compiler_params=_params(),
    )(g_sq, *ws, *ms, *vs)
    return out[0:3], out[3:6], out[6:9], out[9:12]


def _small_update(total, lbw, w8, m8, v8):
    def body(t_ref, lbw_ref, w_ref, m_ref, v_ref, g_out, d_out, m_out, v_out):
        lb = 1.0 / (1.0 + jnp.exp(lbw_ref[1:2, :] - lbw_ref[0:1, :]))
        dlb = t_ref[2:3, :] * lb * (1.0 - lb)
        g_out[...] = jnp.zeros_like(g_out)
        g_out[0:1, :] = t_ref[3:4, :]
        g_out[1:2, :] = dlb
        g_out[2:3, :] = -dlb
        g_out[3:4, :] = t_ref[1:2, :]
        g_out[4:5, :] = t_ref[0:1, :]
        g_out[5:6, :] = t_ref[4:5, :]
        d_out[...], m_out[...], v_out[...] = _adamw_math(w_ref[...], g_out[...], m_ref[...], v_ref[...])

    return pl.pallas_call(
        body, name="small_update", out_shape=[jax.ShapeDtypeStruct((8, D), F32)] * 4,
        compiler_params=_params(),
    )(total, lbw, w8, m8, v8)


def _pack8(norm_w, lbw, hnw, fnw, sinks):
    pad = jnp.zeros((1, D - 16), F32)
    return jnp.concatenate([norm_w, lbw, hnw, fnw.reshape(1, D), jnp.concatenate([sinks, pad], axis=1),
                            jnp.zeros((2, D), F32)], axis=0)


def _unpack8(a):
    return a[0:1], a[1:3], a[3:4], a[5:6, 0:16], a[4]


def _local_step(order_arr, x, tgt, norm_w, lbw, hnw, sinks, fnw, win_mine, wsq_mine, exchange):
    proj, xnt_bf, win_bf = _fwd_proj(order_arr, x, norm_w, win_mine)
    oh, states, wsq_all = _hgrn_fwd(proj, lbw, wsq_mine)
    wsq_bf = wsq_all.reshape(SHARDS, 3, SQ_ROWS, D).transpose(1, 0, 2, 3).reshape(3, D, D)
    oa = _attn_fwd(proj, sinks)
    dx2, doh, doa, dhg, dtail, lhs, rhs, loss8, vec_mid = _mid(x, tgt, proj, oh, oa, hnw, fnw.reshape(1, D), wsq_bf)
    gsq, gsq_b = _wgrad_square(lhs, rhs)
    dhead, dlb = _hgrn_bwd(proj, lbw, states, doh)
    daq, dakv, dsink = _attn_bwd(proj, sinks, oa, doa)
    pieces = [dhead, dhg, daq, dakv, dtail]
    sums = exchange(*_wgrad_in(xnt_bf, pieces), gsq, gsq_b)
    wt_bf = win_bf.transpose(0, 2, 1).reshape(D_IN, D)
    grad_x, gnw, win_got, sq_got = _bwd_dx(pieces, wt_bf, x, norm_w, dx2, sums[2], sums[3])
    sink_row = jnp.concatenate([dsink[:, 0].reshape(1, 16), jnp.zeros((1, D - 16), F32)], axis=1)
    loss_row = jnp.broadcast_to(loss8[0:1, 0:1], (1, D))
    vec = jnp.concatenate([vec_mid[0:2], dlb, gnw, sink_row, loss_row, jnp.zeros((2, D), F32)], axis=0)
    return grad_x, sums, (win_got, sq_got), vec


def kernel(x, norm_w, w_in, hgrn_lower_bound, hgrn_norm_w, w_branch_hgrn, attn_sinks, w_branch_attn, w_out, final_norm_w, loss_target, m_norm_w, m_w_in, m_hgrn_lower_bound, m_hgrn_norm_w, m_w_branch_hgrn, m_attn_sinks, m_w_branch_attn, m_w_out, m_final_norm_w, v_norm_w, v_w_in, v_hgrn_lower_bound, v_hgrn_norm_w, v_w_branch_hgrn, v_attn_sinks, v_w_branch_attn, v_w_out, v_final_norm_w):
    c_arr = lax.axis_index("c").astype(jnp.int32).reshape(1)
    j_arr = (2 * lax.axis_index("x") + lax.axis_index("y")).astype(jnp.int32).reshape(1)
    jc_arr = jnp.concatenate([j_arr, c_arr])

    win_mine, wsq_mine = _cast_shards(j_arr, w_in[0], w_branch_hgrn[0], w_branch_attn[0], w_out[0])
    xi, yi = lax.axis_index("x"), lax.axis_index("y")
    order_arr = jnp.stack([2 * xi + yi] + [2 * _flip(xi, fx) + _flip(yi, fy) for fx, fy in CHIP_FLIPS]).astype(jnp.int32)

    def chip_sums(gwin, gwin_b, gsq, gsq_b):
        return _add_halves(c_arr, gwin, gsq, *_swap_halves(gwin_b, gsq_b))

    grad_x, (swin, ssq, _, _), arrived, vec = _local_step(
        order_arr, x[0], loss_target[0], norm_w, hgrn_lower_bound, hgrn_norm_w, attn_sinks, final_norm_w, win_mine, wsq_mine,
        chip_sums)
    g_win, g_sq = _join_halves(*_sum_chips(jc_arr, swin, ssq, *arrived))

    d_win, nm_win, nv_win = _adamw("adamw_w_in", w_in[0], g_win, m_w_in[0], v_w_in[0], 128)
    g_sqs, d_sqs, nm_sqs, nv_sqs = _adamw_square(
        g_sq, (w_branch_hgrn, w_branch_attn, w_out), (m_w_branch_hgrn, m_w_branch_attn, m_w_out),
        (v_w_branch_hgrn, v_w_branch_attn, v_w_out))

    total = _allreduce_small(vec)
    loss = total[5, 0]
    g8, d8, nm8, nv8 = _small_update(
        total, hgrn_lower_bound,
        _pack8(norm_w, hgrn_lower_bound, hgrn_norm_w, final_norm_w, attn_sinks),
        _pack8(m_norm_w, m_hgrn_lower_bound, m_hgrn_norm_w, m_final_norm_w, m_attn_sinks),
        _pack8(v_norm_w, v_hgrn_lower_bound, v_hgrn_norm_w, v_final_norm_w, v_attn_sinks))

    def assemble(win, sq, small):
        nw, lb, hn, sk, fn = _unpack8(small)
        return (nw, win.reshape(1, D, SHARD_W), lb, hn, sq[0], sk, sq[1], sq[2], fn)

    return (loss, grad_x.reshape(1, -1, D),
            *assemble(g_win, g_sqs, g8), *assemble(d_win, d_sqs, d8),
            *assemble(nm_win, nm_sqs, nm8), *assemble(nv_win, nv_sqs, nv8))
```

```python
import functools

import jax
import jax.numpy as jnp
from jax import lax
from jax.experimental import pallas as pl
from jax.experimental.pallas import tpu as pltpu

F32 = jnp.float32
BF16 = jnp.bfloat16

D = 1024
D_IN = 8704
SHARDS = 4
SHARD_W = D_IN // SHARDS
SQ_ROWS = D // SHARDS
HEADS = 8
HEAD_W = 128
CHUNK = 64
SUB = 8
ATT_BLOCK = 128
ATT_STEP = 4
KV_HEADS = 4
HEAD_DIM = 64
EPS = 1e-6
NEG = -1e30
SCALE = HEAD_DIM ** -0.5
COL_HG, COL_AQ, COL_AK, COL_AV, COL_AG, COL_MH, COL_MA = 3072, 4096, 5120, 5376, 5632, 6656, 7680

ADAM_LR, ADAM_B1, ADAM_B2, ADAM_EPS, ADAM_WD, ADAM_STEP = 0.001, 0.9, 0.999, 1e-08, 0.01, 10

VMEM_LIMIT = 56 * 1024 * 1024
MESH = pl.DeviceIdType.MESH
HBM_SPEC = pl.BlockSpec(memory_space=pltpu.HBM)
CHIP_FLIPS = ((1, 0), (0, 1), (1, 1))


def _dot(a, b):
    return jnp.dot(a, b, preferred_element_type=F32)


def _dot_nt(a, b):
    return lax.dot_general(a, b, (((1,), (1,)), ((), ())), preferred_element_type=F32)


def _dot_tn(a, b):
    return lax.dot_general(a, b, (((0,), (0,)), ((), ())), preferred_element_type=F32)


def _sigmoid(v):
    return 1.0 / (1.0 + jnp.exp(-v))


def _bf(v):
    return v.astype(BF16)


def _tri_dot2(tri, v):
    a = _bf(v)
    return _dot(tri, a) + _dot(tri, _bf(v - a.astype(F32)))


def _params(sem=None):
    return pltpu.CompilerParams(dimension_semantics=sem, vmem_limit_bytes=VMEM_LIMIT)


def _cast_shards(j_arr, win_s, wbh_s, wba_s, wout_s):
    steps = 4
    rows = D // steps

    def body(j_ref, win_ref, a_ref, b_ref, c_ref, win_o, sq_o):
        win_o[...] = _bf(win_ref[...])

        @pl.when(pl.program_id(0) == 0)
        def _():
            sq_o[0:SQ_ROWS, :] = _bf(a_ref[...])
            sq_o[SQ_ROWS:2 * SQ_ROWS, :] = _bf(b_ref[...])
            sq_o[2 * SQ_ROWS:3 * SQ_ROWS, :] = _bf(c_ref[...])

    whole = pl.BlockSpec((SQ_ROWS, D), lambda i, j: (0, 0))
    return pl.pallas_call(
        body, name="cast_shards",
        grid_spec=pltpu.PrefetchScalarGridSpec(
            num_scalar_prefetch=1, grid=(steps,),
            in_specs=[pl.BlockSpec((rows, SHARD_W), lambda i, j: (i, 0)), whole, whole, whole],
            out_specs=[pl.BlockSpec((None, rows, SHARD_W), lambda i, j: (j[0], i, 0)),
                       pl.BlockSpec((None, 3 * SQ_ROWS, D), lambda i, j: (j[0], 0, 0))]),
        out_shape=[jax.ShapeDtypeStruct((SHARDS, D, SHARD_W), BF16), jax.ShapeDtypeStruct((SHARDS, 3 * SQ_ROWS, D), BF16)],
        compiler_params=_params(("arbitrary",)),
    )(j_arr, win_s, wbh_s, wba_s, wout_s)


def _fwd_proj(order_arr, x, norm_w, win_all):
    T = x.shape[0]
    tm = min(512, T)
    nt = T // tm

    def body(order_ref, x_ref, nw_ref, win_in, proj_ref, xn_ref, win_out, w_scr, xn_scr, sems, send_sems, recv_sems):
        del win_in
        p, i = pl.program_id(0), pl.program_id(1)

        def load(n):
            return pltpu.make_async_copy(win_out.at[order_ref[n]], w_scr.at[n % 2], sems.at[n % 2])

        @pl.when((p == 0) & (i == 0))
        def _():
            _gather_start(win_out, _win_half, send_sems, recv_sems)
            load(0).start()
            load(0).wait()

        @pl.when((p == 1) & (i == 0))
        def _():
            _gather_land(win_out, _win_half, 0, send_sems, recv_sems)
            load(1).start()
            load(1).wait()

        for k in range(1, SHARDS - 1):
            @pl.when((p == k) & (i == nt // 2))
            def _():
                _gather_land(win_out, _win_half, k, send_sems, recv_sems)
                load(k + 1).start()

            @pl.when((p == k + 1) & (i == 0))
            def _():
                load(k + 1).wait()

        @pl.when(p == 0)
        def _():
            xf = x_ref[...]
            rs = lax.rsqrt(jnp.mean(xf * xf, axis=1, keepdims=True) + EPS)
            xn = _bf((xf * rs) * nw_ref[...])
            xn_scr[i] = xn
            xn_ref[...] = xn.T

        proj_ref[...] = _dot(xn_scr[i], w_scr[p % 2])

        @pl.when((p == SHARDS - 1) & (i == nt - 1))
        def _():
            _gather_drain(win_out, _win_half, send_sems, recv_sems)

    first = lambda p, i: jnp.where(p == 0, i, nt - 1)
    return pl.pallas_call(
        body, name="fwd_proj",
        grid_spec=pltpu.PrefetchScalarGridSpec(
            num_scalar_prefetch=1, grid=(SHARDS, nt),
            in_specs=[pl.BlockSpec((tm, D), lambda p, i, order: (first(p, i), 0)),
                      pl.BlockSpec((1, D), lambda p, i, order: (0, 0)), HBM_SPEC],
            out_specs=[pl.BlockSpec((tm, SHARD_W), lambda p, i, order: (i, order[p])),
                       pl.BlockSpec((D, tm), lambda p, i, order: (0, first(p, i))),
                       HBM_SPEC],
            scratch_shapes=[pltpu.VMEM((2, D, SHARD_W), BF16), pltpu.VMEM((nt, tm, D), BF16), pltpu.SemaphoreType.DMA((2,)),
                            pltpu.SemaphoreType.DMA((6,)), pltpu.SemaphoreType.DMA((6,))]),
        out_shape=[jax.ShapeDtypeStruct((T, D_IN), F32), jax.ShapeDtypeStruct((D, T), BF16),
                   jax.ShapeDtypeStruct((SHARDS, D, SHARD_W), BF16)],
        input_output_aliases={3: 2},
        compiler_params=_params(("arbitrary", "arbitrary")),
    )(order_arr, x, norm_w, win_all)


def _hgrn_gates(hq_ref, hf_ref, lbw_ref, b_scr):
    lb = 1.0 / (1.0 + jnp.exp(lbw_ref[1:2, :] - lbw_ref[0:1, :]))
    hf = hf_ref[...]
    sig = _sigmoid(hf)
    f = lb + (1.0 - lb) * sig
    g = jnp.log(f)
    hq = hq_ref[...]
    sq = _sigmoid(hq)
    q = hq * sq
    row = lax.broadcasted_iota(jnp.int32, (CHUNK, CHUNK), 0)
    col = lax.broadcasted_iota(jnp.int32, (CHUNK, CHUNK), 1)
    causal = row >= col
    b = _tri_dot2(jnp.where(causal, 1.0, 0.0).astype(BF16), g)
    b_scr[...] = b
    bc = b_scr[CHUNK - 1:CHUNK, :]
    r = b_scr[CHUNK // 2 - 1:CHUNK // 2, :]
    return dict(lb=lb, sig=sig, f=f, k=1.0 - f, hq=hq, sq=sq, q=q, b=b, bc=bc, r=r, causal=causal)


def _hgrn_fwd(proj, lbw, wsq_all):
    T = proj.shape[0]
    n = T // CHUNK

    def body(hq_ref, hf_ref, hi_ref, lbw_ref, wsq_in, o_ref, st_ref, wsq_out, s_scr, b_scr, send_sems, recv_sems):
        del wsq_in

        @pl.when(pl.program_id(0) == 0)
        def _():
            _gather_start(wsq_out, _sq_half, send_sems, recv_sems)
            s_scr[...] = jnp.zeros_like(s_scr)

        for c in range(SUB):
            rows = pl.ds(c * CHUNK, CHUNK)
            gt = _hgrn_gates(hq_ref.at[rows, :], hf_ref.at[rows, :], lbw_ref, b_scr.at[rows, :])
            b, bc, r, q, k = gt["b"], gt["bc"], gt["r"], gt["q"], gt["k"]
            qe = _bf(q * jnp.exp(b))
            qr = _bf(q * jnp.exp(b - r))
            kr = _bf(k * jnp.exp(r - b))
            kl = _bf(k * jnp.exp(bc - b))
            ebc = jnp.exp(bc)
            v = _bf(hi_ref[rows, :])
            scores = [_bf(jnp.where(gt["causal"], _dot_nt(qr[:, h * HEAD_W:(h + 1) * HEAD_W], kr[:, h * HEAD_W:(h + 1) * HEAD_W]), 0.0))
                      for h in range(HEADS)]
            for h in range(HEADS):
                sl = slice(h * HEAD_W, (h + 1) * HEAD_W)
                st = s_scr[h]
                st_ref[c, h] = st
                o_ref[rows, sl] = _dot(scores[h], v[:, sl]) + _dot_nt(qe[:, sl], _bf(st))
                s_scr[h] = ebc[:, sl] * st + _dot_tn(v[:, sl], kl[:, sl])

        @pl.when(pl.program_id(0) == n // SUB - 1)
        def _():
            _gather_finish(wsq_out, _sq_half, send_sems, recv_sems)

    col = lambda j: pl.BlockSpec((SUB * CHUNK, D), lambda i: (i, j))
    return pl.pallas_call(
        body, name="hgrn_fwd", grid=(n // SUB,),
        in_specs=[col(0), col(1), col(2), pl.BlockSpec((2, D), lambda i: (0, 0)), HBM_SPEC],
        out_specs=[pl.BlockSpec((SUB * CHUNK, D), lambda i: (i, 0)),
                   pl.BlockSpec((SUB, HEADS, HEAD_W, HEAD_W), lambda i: (i, 0, 0, 0)), HBM_SPEC],
        out_shape=[jax.ShapeDtypeStruct((T, D), F32), jax.ShapeDtypeStruct((n, HEADS, HEAD_W, HEAD_W), F32),
                   jax.ShapeDtypeStruct((SHARDS, 3 * SQ_ROWS, D), BF16)],
        input_output_aliases={4: 2},
        scratch_shapes=[pltpu.VMEM((HEADS, HEAD_W, HEAD_W), F32), pltpu.VMEM((SUB * CHUNK, D), F32),
                        pltpu.SemaphoreType.DMA((6,)), pltpu.SemaphoreType.DMA((6,))],
        compiler_params=_params(("arbitrary",)),
    )(proj, proj, proj, lbw, wsq_all)


def _hgrn_bwd(proj, lbw, states, do):
    T = proj.shape[0]
    n = T // CHUNK

    def body(hq_ref, hf_ref, hi_ref, lbw_ref, st_ref, do_ref, dp_ref, dlb_ref,
             ds_scr, b_scr, dq_scr, dk_scr, dv_scr, late_scr, early_scr, ex_scr):
        @pl.when(pl.program_id(0) == 0)
        def _():
            ds_scr[...] = jnp.zeros_like(ds_scr)
            dlb_ref[...] = jnp.zeros_like(dlb_ref)

        for c in reversed(range(SUB)):
            rows = pl.ds(c * CHUNK, CHUNK)
            gt = _hgrn_gates(hq_ref.at[rows, :], hf_ref.at[rows, :], lbw_ref, b_scr.at[rows, :])
            b, bc, r, q, k = gt["b"], gt["bc"], gt["r"], gt["q"], gt["k"]
            eb = jnp.exp(b)
            er = jnp.exp(b - r)
            erk = jnp.exp(r - b)
            el = jnp.exp(bc - b)
            ebc = jnp.exp(bc)
            qe, qr, kr, kl = _bf(q * eb), _bf(q * er), _bf(k * erk), _bf(k * el)
            v = _bf(hi_ref[rows, :])
            do_b = do_ref[rows, :]
            do_t = do_b.T
            causal_t = lax.broadcasted_iota(jnp.int32, (CHUNK, CHUNK), 0) <= lax.broadcasted_iota(jnp.int32, (CHUNK, CHUNK), 1)
            firsts = []
            for h in range(HEADS):
                sl = slice(h * HEAD_W, (h + 1) * HEAD_W)
                firsts.append((_bf(jnp.where(causal_t, _dot_nt(kr[:, sl], qr[:, sl]), 0.0)),
                               _bf(jnp.where(gt["causal"], _dot_nt(do_b[:, sl], v[:, sl]), 0.0)),
                               _bf(jnp.where(causal_t, _dot_nt(v[:, sl], do_b[:, sl]), 0.0))))
            for h in range(HEADS):
                sl = slice(h * HEAD_W, (h + 1) * HEAD_W)
                st0 = st_ref[c, h]
                dst = ds_scr[h]
                dst_b = _bf(dst)
                a_t, da, da_t = firsts[h]
                mq = _dot(da, kr[:, sl])
                mk = _dot(da_t, qr[:, sl])
                dq_in = eb[:, sl] * _dot(do_b[:, sl], _bf(st0))
                dk_in = el[:, sl] * _dot(v[:, sl], dst_b)
                dq_scr[rows, sl] = er[:, sl] * mq + dq_in
                dk_scr[rows, sl] = erk[:, sl] * mk + dk_in
                dv_scr[rows, sl] = _dot(a_t, do_b[:, sl]) + _dot_nt(kl[:, sl], dst_b)
                late_scr[rows, sl] = q[:, sl] * dq_in + qr[:, sl].astype(F32) * mq - kr[:, sl].astype(F32) * mk
                early_scr[rows, sl] = k[:, sl] * dk_in
                ex_scr[:, sl] = jnp.sum(dst * st0, axis=0, keepdims=True)
                ds_scr[h] = ebc[:, sl] * dst + _dot(do_t[sl, :], qe[:, sl])

            dq, dk = dq_scr[rows, :], dk_scr[rows, :]
            row = lax.broadcasted_iota(jnp.int32, (CHUNK, CHUNK), 0)
            col = lax.broadcasted_iota(jnp.int32, (CHUNK, CHUNK), 1)
            at_or_after = jnp.where(col >= row, 1.0, 0.0).astype(BF16)
            before = jnp.where(col < row, 1.0, 0.0).astype(BF16)
            dg = _tri_dot2(jnp.concatenate([at_or_after, before], axis=1),
                           jnp.concatenate([late_scr[rows, :], early_scr[rows, :]], axis=0)) + ebc * ex_scr[...]
            df = dg / gt["f"] - dk
            sig, sq, hq, lb = gt["sig"], gt["sq"], gt["hq"], gt["lb"]
            dp_ref[rows, 0:D] = _bf(dq * (sq * (1.0 + hq * (1.0 - sq))))
            dp_ref[rows, D:2 * D] = _bf(df * (1.0 - lb) * sig * (1.0 - sig))
            dp_ref[rows, 2 * D:3 * D] = _bf(dv_scr[rows, :])
            dlb_ref[...] += jnp.sum(df * (1.0 - sig), axis=0, keepdims=True)

    ns = n // SUB
    col = lambda j: pl.BlockSpec((SUB * CHUNK, D), lambda i: (ns - 1 - i, j))
    return pl.pallas_call(
        body, name="hgrn_bwd", grid=(ns,),
        in_specs=[col(0), col(1), col(2), pl.BlockSpec((2, D), lambda i: (0, 0)),
                  pl.BlockSpec((SUB, HEADS, HEAD_W, HEAD_W), lambda i: (ns - 1 - i, 0, 0, 0)),
                  pl.BlockSpec((SUB * CHUNK, D), lambda i: (ns - 1 - i, 0))],
        out_specs=[pl.BlockSpec((SUB * CHUNK, 3 * D), lambda i: (ns - 1 - i, 0)),
                   pl.BlockSpec((1, D), lambda i: (0, 0))],
        out_shape=[jax.ShapeDtypeStruct((T, 3 * D), BF16), jax.ShapeDtypeStruct((1, D), F32)],
        scratch_shapes=[pltpu.VMEM((HEADS, HEAD_W, HEAD_W), F32)] + [pltpu.VMEM((SUB * CHUNK, D), F32)] * 6
                       + [pltpu.VMEM((1, D), F32)],
        compiler_params=_params(("arbitrary",)),
    )(proj, proj, proj, lbw, states, do)


def _attn_masks(blk):
    qi = lax.broadcasted_iota(jnp.int32, (ATT_BLOCK, 2 * ATT_BLOCK), 0)
    kj = lax.broadcasted_iota(jnp.int32, (ATT_BLOCK, 2 * ATT_BLOCK), 1)
    band = (kj > qi) & (kj <= qi + ATT_BLOCK)
    return band & ((blk > 0) | (kj >= ATT_BLOCK))


def _head_pair_operand(t, hp, low):
    mine = low if hp == 0 else jnp.logical_not(low)
    both = jnp.where(mine, t, pltpu.roll(t, HEAD_DIM, 1))
    return _bf(jnp.concatenate([jnp.where(low, both, 0.0), jnp.where(low, 0.0, both)], axis=0))


def _attn_probs(s, sink, valid):
    s = jnp.where(valid, s, NEG)
    m = jnp.maximum(jnp.max(s, axis=1, keepdims=True), sink)
    p = jnp.exp(s - m)
    es = jnp.exp(sink - m)
    inv = 1.0 / (jnp.sum(p, axis=1, keepdims=True) + es)
    return p * inv, es * inv


def _attn_fwd(proj, sinks):
    T = proj.shape[0]
    rows_step = ATT_STEP * ATT_BLOCK

    def body(sink_ref, q_ref, kp_ref, kc_ref, vp_ref, vc_ref, o_ref):
        low = lax.broadcasted_iota(jnp.int32, (1, 2 * HEAD_DIM), 1) < HEAD_DIM
        for sb in range(ATT_STEP):
            rows = slice(sb * ATT_BLOCK, (sb + 1) * ATT_BLOCK)
            before = slice((sb - 1) * ATT_BLOCK, sb * ATT_BLOCK)
            valid = _attn_masks(pl.program_id(0) * ATT_STEP + sb)
            kcat = jnp.concatenate([kp_ref[...] if sb == 0 else kc_ref[before, :], kc_ref[rows, :]], axis=0)
            vcat = jnp.concatenate([vp_ref[...] if sb == 0 else vc_ref[before, :], vc_ref[rows, :]], axis=0)
            for h in range(KV_HEADS):
                tl = slice((h // 2) * 128, (h // 2) * 128 + 128)
                mine = low if h % 2 == 0 else jnp.logical_not(low)
                kh = _bf(jnp.where(mine, kcat[:, tl], pltpu.roll(kcat[:, tl], HEAD_DIM, 1)))
                vh = _bf(jnp.where(mine, vcat[:, tl], pltpu.roll(vcat[:, tl], HEAD_DIM, 1)))
                for t in range(2):
                    ql = slice((2 * h + t) * 128, (2 * h + t) * 128 + 128)
                    q2 = q_ref[rows, ql] * SCALE
                    outs = []
                    for p in range(2):
                        qm = _bf(jnp.where(low if p == 0 else jnp.logical_not(low), q2, 0.0))
                        probs, _ = _attn_probs(_dot_nt(qm, kh), sink_ref[0, 4 * h + 2 * t + p], valid)
                        outs.append(_dot(_bf(probs), vh))
                    o_ref[rows, ql] = jnp.where(low, outs[0], outs[1])

    prev = lambda i: jnp.maximum(ATT_STEP * i - 1, 0)
    return pl.pallas_call(
        body, name="attn_fwd", grid=(T // rows_step,),
        in_specs=[pl.BlockSpec(memory_space=pltpu.SMEM),
                  pl.BlockSpec((rows_step, D), lambda i: (i, COL_AQ // D)),
                  pl.BlockSpec((ATT_BLOCK, 256), lambda i: (prev(i), COL_AK // 256)),
                  pl.BlockSpec((rows_step, 256), lambda i: (i, COL_AK // 256)),
                  pl.BlockSpec((ATT_BLOCK, 256), lambda i: (prev(i), COL_AV // 256)),
                  pl.BlockSpec((rows_step, 256), lambda i: (i, COL_AV // 256))],
        out_specs=pl.BlockSpec((rows_step, D), lambda i: (i, 0)),
        out_shape=jax.ShapeDtypeStruct((T, D), F32),
        compiler_params=_params(("arbitrary",)),
    )(sinks, proj, proj, proj, proj, proj)


def _attn_bwd(proj, sinks, o, do):
    T = proj.shape[0]
    rows_step = ATT_STEP * ATT_BLOCK
    ns = T // rows_step
    W2 = 2 * ATT_BLOCK
    last = slice(rows_step - ATT_BLOCK, rows_step)

    def body(sink_ref, q_ref, kp_ref, kc_ref, vp_ref, vc_ref, o_ref, do_ref,
             dq_ref, dkv_ref, dsink_ref, carry_scr, acc_scr, nk_scr, nv_scr):
        step = pl.program_id(0)

        @pl.when(step == 0)
        def _():
            carry_scr[...] = jnp.zeros_like(carry_scr)
            dsink_ref[...] = jnp.zeros_like(dsink_ref)

        @pl.when(step < ns)
        def _():
            acc_scr[...] = jnp.zeros_like(acc_scr)
            low = lax.broadcasted_iota(jnp.int32, (1, 2 * HEAD_DIM), 1) < HEAD_DIM
            for sb in range(ATT_STEP):
                rows = slice(sb * ATT_BLOCK, (sb + 1) * ATT_BLOCK)
                before = slice((sb - 1) * ATT_BLOCK, sb * ATT_BLOCK)
                valid = _attn_masks(step * ATT_STEP + sb)
                kcat = jnp.concatenate([kp_ref[...] if sb == 0 else kc_ref[before, :], kc_ref[rows, :]], axis=0)
                vcat = jnp.concatenate([vp_ref[...] if sb == 0 else vc_ref[before, :], vc_ref[rows, :]], axis=0)
                for h in range(KV_HEADS):
                    tl = slice((h // 2) * 128, (h // 2) * 128 + 128)
                    kbd = _head_pair_operand(kcat[:, tl], h % 2, low)
                    vbd = _head_pair_operand(vcat[:, tl], h % 2, low)
                    dkbd = jnp.zeros((2 * W2, 128), F32)
                    dvbd = jnp.zeros((2 * W2, 128), F32)
                    tiles = []
                    for t in range(2):
                        ql = slice((2 * h + t) * 128, (2 * h + t) * 128 + 128)
                        q2 = _bf(q_ref[rows, ql] * SCALE)
                        do2_b = do_ref[rows, ql]
                        doo = do2_b.astype(F32) * o_ref[rows, ql]
                        dsum0 = jnp.sum(jnp.where(low, doo, 0.0), axis=1, keepdims=True)
                        dsum1 = jnp.sum(jnp.where(low, 0.0, doo), axis=1, keepdims=True)
                        tiles.append((ql, q2, do2_b, dsum0, dsum1, _dot_nt(q2, kbd), _dot_nt(do2_b, vbd)))
                    grads = []
                    for t, (ql, q2, do2_b, dsum0, dsum1, s2, dp2) in enumerate(tiles):
                        head = 4 * h + 2 * t
                        p0, ps0 = _attn_probs(s2[:, 0:W2], sink_ref[0, head], valid)
                        p1, ps1 = _attn_probs(s2[:, W2:2 * W2], sink_ref[0, head + 1], valid)
                        ds2 = _bf(jnp.concatenate([p0 * (dp2[:, 0:W2] - dsum0), p1 * (dp2[:, W2:2 * W2] - dsum1)], axis=1))
                        grads.append((ds2, _bf(jnp.concatenate([p0, p1], axis=1))))
                        dsink_ref[head:head + 1, :] += jnp.zeros((1, 128), F32) - jnp.sum(ps0 * dsum0, axis=0, keepdims=True)
                        dsink_ref[head + 1:head + 2, :] += jnp.zeros((1, 128), F32) - jnp.sum(ps1 * dsum1, axis=0, keepdims=True)
                    for (ql, q2, do2_b, _, _, _, _), (ds2, p2) in zip(tiles, grads):
                        dq_ref[rows, ql] = _bf(_dot(ds2, kbd) * SCALE)
                        dkbd = dkbd + _dot_tn(ds2, q2)
                        dvbd = dvbd + _dot_tn(p2, do2_b)
                    dk2 = jnp.where(low, dkbd[0:W2], dkbd[W2:2 * W2])
                    dv2 = jnp.where(low, dvbd[0:W2], dvbd[W2:2 * W2])
                    dk2 = dk2 + pltpu.roll(dk2, HEAD_DIM, 1)
                    dv2 = dv2 + pltpu.roll(dv2, HEAD_DIM, 1)
                    if h % 2 == 0:
                        keep_k, keep_v = dk2, dv2
                    else:
                        nk_scr[:, tl] = jnp.where(low, keep_k, dk2)
                        nv_scr[:, tl] = jnp.where(low, keep_v, dv2)
                both = slice(sb * ATT_BLOCK, (sb + 2) * ATT_BLOCK)
                acc_scr[both, 0:256] += nk_scr[...]
                acc_scr[both, 256:512] += nv_scr[...]
            dkv_ref[...] = _bf(carry_scr[...])
            dkv_ref[last, :] = _bf(carry_scr[last, :] + acc_scr[0:ATT_BLOCK, :])
            carry_scr[...] = acc_scr[ATT_BLOCK:rows_step + ATT_BLOCK, :]

        @pl.when(step == ns)
        def _():
            dkv_ref[...] = _bf(carry_scr[...])

    cur = lambda i: jnp.minimum(i, ns - 1)
    prev = lambda i: jnp.maximum(ATT_STEP * cur(i) - 1, 0)
    late = lambda i: jnp.maximum(i - 1, 0)
    return pl.pallas_call(
        body, name="attn_bwd", grid=(ns + 1,),
        in_specs=[pl.BlockSpec(memory_space=pltpu.SMEM),
                  pl.BlockSpec((rows_step, D), lambda i: (cur(i), COL_AQ // D)),
                  pl.BlockSpec((ATT_BLOCK, 256), lambda i: (prev(i), COL_AK // 256)),
                  pl.BlockSpec((rows_step, 256), lambda i: (cur(i), COL_AK // 256)),
                  pl.BlockSpec((ATT_BLOCK, 256), lambda i: (prev(i), COL_AV // 256)),
                  pl.BlockSpec((rows_step, 256), lambda i: (cur(i), COL_AV // 256)),
                  pl.BlockSpec((rows_step, D), lambda i: (cur(i), 0)),
                  pl.BlockSpec((rows_step, D), lambda i: (cur(i), 0))],
        out_specs=[pl.BlockSpec((rows_step, D), lambda i: (cur(i), 0)),
                   pl.BlockSpec((rows_step, 512), lambda i: (late(i), 0)),
                   pl.BlockSpec((16, 128), lambda i: (0, 0))],
        out_shape=[jax.ShapeDtypeStruct((T, D), BF16), jax.ShapeDtypeStruct((T, 512), BF16),
                   jax.ShapeDtypeStruct((16, 128), F32)],
        scratch_shapes=[pltpu.VMEM((rows_step, 512), F32), pltpu.VMEM((rows_step + ATT_BLOCK, 512), F32),
                        pltpu.VMEM((2 * ATT_BLOCK, 256), F32), pltpu.VMEM((2 * ATT_BLOCK, 256), F32)],
        compiler_params=_params(("arbitrary",)),
    )(sinks, proj, proj, proj, proj, proj, o, do)


def _mid(x, tgt, proj, oh, oa, hnw, fnw, wsq_bf):
    T = x.shape[0]
    tm = min(256, T)
    nt = T // tm

    def body(x_ref, tgt_ref, oh_ref, oa_ref, hg_ref, ag0_ref, ag1_ref, mh0_ref, mh1_ref, ma0_ref, ma1_ref,
             hnw_ref, fnw_ref, w_hbm,
             dx2_ref, doh_ref, doa_ref, dhg_ref, dtail_ref, lhs_ref, rhs_ref, loss_ref, vec_ref,
             w_scr, xh_scr, rs_scr, sem):
        @pl.when(pl.program_id(0) == 0)
        def _():
            cp = pltpu.make_async_copy(w_hbm, w_scr, sem)
            cp.start()
            cp.wait()
            loss_ref[...] = jnp.zeros_like(loss_ref)
            vec_ref[...] = jnp.zeros_like(vec_ref)

        oh = oh_ref[...]
        for h in range(HEADS):
            sl = slice(h * HEAD_W, (h + 1) * HEAD_W)
            ohh = oh[:, sl]
            rs = lax.rsqrt(jnp.mean(ohh * ohh, axis=1, keepdims=True) + EPS)
            xh_scr[:, sl] = ohh * rs
            rs_scr[:, sl] = jnp.broadcast_to(rs, (tm, HEAD_W))
        xh = xh_scr[...]
        hnw = hnw_ref[...]
        on = xh * hnw
        hg = hg_ref[...]
        sg = _sigmoid(hg)
        silu_g = hg * sg
        gated_h = _bf(on * silu_g)
        oa = oa_ref[...]
        ag = jnp.concatenate([ag0_ref[...], ag1_ref[...]], axis=1)
        sa = _sigmoid(ag)
        silu_a = ag * sa
        gated_a = _bf(oa * silu_a)
        yh = _dot(gated_h, w_scr[0])
        ya = _dot(gated_a, w_scr[1])
        lhs_ref[0] = gated_h.T
        lhs_ref[1] = gated_a.T
        smh = _sigmoid(jnp.concatenate([mh0_ref[...], mh1_ref[...]], axis=1))
        sma = _sigmoid(jnp.concatenate([ma0_ref[...], ma1_ref[...]], axis=1))
        merged = _bf(smh * yh + sma * ya)
        lhs_ref[2] = merged.T
        x2 = x_ref[...] + _dot(merged, w_scr[2])
        rs2 = lax.rsqrt(jnp.mean(x2 * x2, axis=1, keepdims=True) + EPS)
        xh2 = x2 * rs2
        fnw = fnw_ref[...]
        diff = xh2 * fnw - tgt_ref[...]
        loss_ref[...] += jnp.zeros_like(loss_ref) + jnp.sum(diff * diff) * (0.5 / D)

        dy = diff * (1.0 / D)
        vec_ref[0:1, :] += jnp.sum(dy * xh2, axis=0, keepdims=True)
        gy = dy * fnw
        dx2 = rs2 * (gy - xh2 * jnp.mean(gy * xh2, axis=1, keepdims=True))
        dx2_ref[...] = dx2
        dx2_b = _bf(dx2)
        rhs_ref[2] = dx2_b
        dmerged = _dot_nt(dx2_b, w_scr[2])
        dyh = dmerged * smh
        dya = dmerged * sma
        dtail_ref[:, D:2 * D] = _bf(dyh * yh * (1.0 - smh))
        dtail_ref[:, 2 * D:3 * D] = _bf(dya * ya * (1.0 - sma))
        dyh_b, dya_b = _bf(dyh), _bf(dya)
        rhs_ref[0] = dyh_b
        rhs_ref[1] = dya_b
        dgh = _dot_nt(dyh_b, w_scr[0])
        dga = _dot_nt(dya_b, w_scr[1])
        don = dgh * silu_g
        dhg_ref[...] = _bf(dgh * on * (sg * (1.0 + hg * (1.0 - sg))))
        vec_ref[1:2, :] += jnp.sum(don * xh, axis=0, keepdims=True)
        gxh = don * hnw
        rsb = rs_scr[...]
        for h in range(HEADS):
            sl = slice(h * HEAD_W, (h + 1) * HEAD_W)
            gh, xhh = gxh[:, sl], xh[:, sl]
            doh_ref[:, sl] = _bf(rsb[:, sl] * (gh - xhh * jnp.mean(gh * xhh, axis=1, keepdims=True)))
        doa_ref[...] = _bf(dga * silu_a)
        dtail_ref[:, 0:D] = _bf(dga * oa * (sa * (1.0 + ag * (1.0 - sa))))

    row = lambda w, j: pl.BlockSpec((tm, w), lambda i: (i, j))
    const = lambda r, c: pl.BlockSpec((r, c), lambda i: (0, 0))
    stack = pl.BlockSpec((3, tm, D), lambda i: (0, i, 0))
    stack_t = pl.BlockSpec((3, D, tm), lambda i: (0, 0, i))
    return pl.pallas_call(
        body, name="mid", grid=(nt,),
        in_specs=[row(D, 0), row(D, 0), row(D, 0), row(D, 0), row(D, COL_HG // D),
                  row(512, COL_AG // 512), row(512, COL_AG // 512 + 1),
                  row(512, COL_MH // 512), row(512, COL_MH // 512 + 1),
                  row(512, COL_MA // 512), row(512, COL_MA // 512 + 1),
                  const(1, D), const(1, D), HBM_SPEC],
        out_specs=[row(D, 0), row(D, 0), row(D, 0), row(D, 0), row(3 * D, 0), stack_t, stack, const(8, 128), const(8, D)],
        out_shape=[jax.ShapeDtypeStruct((T, D), F32), jax.ShapeDtypeStruct((T, D), BF16), jax.ShapeDtypeStruct((T, D), BF16),
                   jax.ShapeDtypeStruct((T, D), BF16), jax.ShapeDtypeStruct((T, 3 * D), BF16),
                   jax.ShapeDtypeStruct((3, D, T), BF16), jax.ShapeDtypeStruct((3, T, D), BF16),
                   jax.ShapeDtypeStruct((8, 128), F32), jax.ShapeDtypeStruct((8, D), F32)],
        scratch_shapes=[pltpu.VMEM((3, D, D), BF16), pltpu.VMEM((tm, D), F32), pltpu.VMEM((tm, D), F32),
                        pltpu.SemaphoreType.DMA],
        compiler_params=_params(("arbitrary",)),
    )(x, tgt, oh, oa, proj, proj, proj, proj, proj, proj, proj, hnw, fnw, wsq_bf)


def _wgrad_square(lhs_t, rhs):
    T = rhs.shape[1]
    tk = min(2048, T)
    steps = T // tk

    def body(a_ref, b_ref, g_ref, gb_ref):
        part = _dot(a_ref[...], b_ref[...])

        @pl.when(pl.program_id(1) == 0)
        def _():
            g_ref[...] = part

        @pl.when(pl.program_id(1) > 0)
        def _():
            g_ref[...] += part

        @pl.when(pl.program_id(1) == steps - 1)
        def _():
            gb_ref[...] = _bf(g_ref[...])

    return pl.pallas_call(
        body, name="wgrad_square", grid=(3, steps),
        in_specs=[pl.BlockSpec((None, D, tk), lambda k, i: (k, 0, i)), pl.BlockSpec((None, tk, D), lambda k, i: (k, i, 0))],
        out_specs=[pl.BlockSpec((None, D, D), lambda k, i: (k, 0, 0))] * 2,
        out_shape=[jax.ShapeDtypeStruct((3, D, D), F32), jax.ShapeDtypeStruct((3, D, D), BF16)],
        compiler_params=_params(("parallel", "arbitrary")),
    )(lhs_t, rhs)


def _bwd_dx(pieces, wt_bf, x, norm_w, dx2, swin_b, ssq_b):
    T = x.shape[0]
    tm = min(512, T)
    nt = T // tm
    widths = [p.shape[1] for p in pieces]
    n_p = len(pieces)

    def body(*refs):
        piece_refs = refs[:n_p]
        (w_hbm, x_ref, nw_ref, dx2_ref, swin_ref, ssq_ref,
         gx_ref, gnw_ref, win_got, sq_got, w_scr, sem, send_sems, recv_sems) = refs[n_p:]

        def scatter_copies():
            x_, y_, c_ = _place()
            copies = []
            for k, (fx, fy) in enumerate(CHIP_FLIPS):
                px, py = _flip(x_, fx), _flip(y_, fy)
                jr = 2 * px + py
                for a, (src, dst) in enumerate(((swin_ref.at[:, pl.ds(jr * SHARD_W, SHARD_W)], win_got.at[k]),
                                                (ssq_ref.at[:, pl.ds(jr * SQ_ROWS, SQ_ROWS), :], sq_got.at[k]))):
                    copies.append(pltpu.make_async_remote_copy(
                        src_ref=src, dst_ref=dst, send_sem=send_sems.at[2 * k + a], recv_sem=recv_sems.at[2 * k + a],
                        device_id=(px, py, c_), device_id_type=MESH))
            return copies

        @pl.when(pl.program_id(0) == 0)
        def _():
            for cp in scatter_copies():
                cp.start()
            cp = pltpu.make_async_copy(w_hbm, w_scr, sem)
            cp.start()
            cp.wait()
            gnw_ref[...] = jnp.zeros_like(gnw_ref)

        dxn = None
        off = 0
        for ref, w in zip(piece_refs, widths):
            part = _dot(ref[...], w_scr[off:off + w, :])
            dxn = part if dxn is None else dxn + part
            off += w
        xf = x_ref[...]
        rs = lax.rsqrt(jnp.mean(xf * xf, axis=1, keepdims=True) + EPS)
        xh = xf * rs
        gnw_ref[...] += jnp.sum(dxn * xh, axis=0, keepdims=True)
        gx = dxn * nw_ref[...]
        gx_ref[...] = rs * (gx - xh * jnp.mean(gx * xh, axis=1, keepdims=True)) + dx2_ref[...]

        @pl.when(pl.program_id(0) == nt - 1)
        def _():
            for cp in scatter_copies():
                cp.wait()

    row = lambda w: pl.BlockSpec((tm, w), lambda i: (i, 0))
    return pl.pallas_call(
        body, name="bwd_dx", grid=(nt,),
        in_specs=[row(w) for w in widths] + [HBM_SPEC, row(D), pl.BlockSpec((1, D), lambda i: (0, 0)), row(D), HBM_SPEC, HBM_SPEC],
        out_specs=[row(D), pl.BlockSpec((1, D), lambda i: (0, 0)), HBM_SPEC, HBM_SPEC],
        out_shape=[jax.ShapeDtypeStruct((T, D), F32), jax.ShapeDtypeStruct((1, D), F32),
                   jax.ShapeDtypeStruct((3, D // 2, SHARD_W), BF16), jax.ShapeDtypeStruct((3, 3, SQ_ROWS, D // 2), BF16)],
        scratch_shapes=[pltpu.VMEM((D_IN, D), BF16), pltpu.SemaphoreType.DMA,
                        pltpu.SemaphoreType.DMA((6,)), pltpu.SemaphoreType.DMA((6,))],
        compiler_params=_params(("arbitrary",)),
    )(*pieces, wt_bf, x, norm_w, dx2, swin_b, ssq_b)


W_PIECES = ((0, 1024, 3), (COL_HG, 1024, 1), (COL_AQ, 1024, 1), (COL_AK, 512, 1), (COL_AG, 512, 6))


def _wgrad_in(xnt_bf, pieces):
    T = xnt_bf.shape[1]
    bufs = ()
    for n, (piece, (col, wb, blocks)) in enumerate(zip(pieces, W_PIECES)):
        tk = min(2048 if wb == 1024 else 4096, T)
        steps = T // tk

        def body(xnt_ref, p_ref, *rest):
            g_ref, gb_ref = rest[-2:]
            part = _dot(xnt_ref[...], p_ref[...])

            @pl.when(pl.program_id(1) == 0)
            def _():
                g_ref[...] = part

            @pl.when(pl.program_id(1) > 0)
            def _():
                g_ref[...] += part

            @pl.when(pl.program_id(1) == steps - 1)
            def _():
                gb_ref[...] = _bf(g_ref[...])

        out = pl.BlockSpec((D, wb), lambda jb, i, base=col // wb: (0, base + jb))
        bufs = pl.pallas_call(
            body, name=f"wgrad_in_{n}", grid=(blocks, steps),
            in_specs=[pl.BlockSpec((D, tk), lambda jb, i: (0, i)), pl.BlockSpec((tk, wb), lambda jb, i: (i, jb))]
                     + [HBM_SPEC] * len(bufs),
            out_specs=[out, out],
            out_shape=[jax.ShapeDtypeStruct((D, D_IN), F32), jax.ShapeDtypeStruct((D, D_IN), BF16)],
            input_output_aliases={2: 0, 3: 1} if bufs else {},
            compiler_params=_params(("parallel", "arbitrary")),
        )(xnt_bf, piece, *bufs)
    return bufs


def _place():
    return lax.axis_index("x"), lax.axis_index("y"), lax.axis_index("c")


def _flip(v, f):
    return 1 - v if f else v


def _win_half(ref, h):
    return ref.at[pl.ds(h * (D // 2), D // 2), :]


def _sq_half(ref, h):
    return ref.at[:, pl.ds(h * (D // 2), D // 2)]


def _gather_copy(part, k, to, send_sems, recv_sems):
    return pltpu.make_async_remote_copy(src_ref=part, dst_ref=part, send_sem=send_sems.at[k], recv_sem=recv_sems.at[k],
                                        device_id=to, device_id_type=MESH)


def _gather_start(out, half, send_sems, recv_sems):
    x, y, c = _place()
    for k, (fx, fy) in enumerate(CHIP_FLIPS):
        _gather_copy(half(out.at[2 * x + y], c), k, (_flip(x, fx), _flip(y, fy), c), send_sems, recv_sems).start()


def _gather_land(out, half, k, send_sems, recv_sems):
    x, y, c = _place()
    sib = (x, y, 1 - c)
    fx, fy = CHIP_FLIPS[k]
    slot = out.at[2 * _flip(x, fx) + _flip(y, fy)]
    _gather_copy(half(slot, c), k, sib, send_sems, recv_sems).wait_recv()
    _gather_copy(half(slot, c), 3 + k, sib, send_sems, recv_sems).start()
    _gather_copy(half(slot, 1 - c), 3 + k, sib, send_sems, recv_sems).wait_recv()


def _gather_drain(out, half, send_sems, recv_sems):
    x, y, c = _place()
    for k, (fx, fy) in enumerate(CHIP_FLIPS):
        _gather_copy(half(out.at[2 * x + y], c), k, (_flip(x, fx), _flip(y, fy), c), send_sems, recv_sems).wait_send()
        _gather_copy(half(out.at[2 * _flip(x, fx) + _flip(y, fy)], c), 3 + k, (x, y, 1 - c), send_sems, recv_sems).wait_send()


def _gather_finish(out, half, send_sems, recv_sems):
    for k in range(len(CHIP_FLIPS)):
        _gather_land(out, half, k, send_sems, recv_sems)
    _gather_drain(out, half, send_sems, recv_sems)


def _swap_halves(gwin, gsq):
    def body(gwin_ref, gsq_ref, win_got, sq_got, send_sems, recv_sems):
        x, y, c = _place()
        sib = (x, y, 1 - c)
        pairs = ((_win_half(gwin_ref, 1 - c), win_got),
                 (gsq_ref.at[:, :, pl.ds((1 - c) * (D // 2), D // 2)], sq_got))
        copies = [pltpu.make_async_remote_copy(src_ref=src, dst_ref=dst, send_sem=send_sems.at[a], recv_sem=recv_sems.at[a],
                                               device_id=sib, device_id_type=MESH) for a, (src, dst) in enumerate(pairs)]
        for cp in copies:
            cp.start()
        for cp in copies:
            cp.wait()

    return pl.pallas_call(
        body, name="swap_halves",
        in_specs=[HBM_SPEC, HBM_SPEC], out_specs=[HBM_SPEC, HBM_SPEC],
        out_shape=[jax.ShapeDtypeStruct((D // 2, D_IN), BF16), jax.ShapeDtypeStruct((3, D, D // 2), BF16)],
        scratch_shapes=[pltpu.SemaphoreType.DMA((2,)), pltpu.SemaphoreType.DMA((2,))],
    )(gwin, gsq)


def _add_halves(c_arr, gwin, gsq, win_got, sq_got):
    def body(c_ref, a_ref, b_ref, p_ref, q_ref, so_ref, sq_ref, sob_ref, sqb_ref):
        so = a_ref[...] + b_ref[...].astype(F32)
        sq = p_ref[...] + q_ref[...].astype(F32)
        so_ref[...] = so
        sq_ref[...] = sq
        sob_ref[...] = _bf(so)
        sqb_ref[...] = _bf(sq)

    steps = 8
    rows, sq_rows = (D // 2) // steps, D // steps
    win = lambda f: pl.BlockSpec((rows, D_IN), f)
    sq = lambda f: pl.BlockSpec((3, sq_rows, D // 2), f)
    return pl.pallas_call(
        body, name="add_halves",
        grid_spec=pltpu.PrefetchScalarGridSpec(
            num_scalar_prefetch=1, grid=(steps,),
            in_specs=[win(lambda i, c: (c[0] * steps + i, 0)), win(lambda i, c: (i, 0)),
                      sq(lambda i, c: (0, i, c[0])), sq(lambda i, c: (0, i, 0))],
            out_specs=[win(lambda i, c: (i, 0)), sq(lambda i, c: (0, i, 0))] * 2),
        out_shape=[jax.ShapeDtypeStruct((D // 2, D_IN), F32), jax.ShapeDtypeStruct((3, D, D // 2), F32),
                   jax.ShapeDtypeStruct((D // 2, D_IN), BF16), jax.ShapeDtypeStruct((3, D, D // 2), BF16)],
        compiler_params=_params(("arbitrary",)),
    )(c_arr, gwin, win_got, gsq, sq_got)


def _sum_chips(jc_arr, swin, ssq, win_got, sq_got):
    def body(jc_ref, a_ref, b_ref, p_ref, q_ref, so_ref, sq_ref):
        so_ref[...] = ((a_ref[...] + b_ref[0].astype(F32)) + b_ref[1].astype(F32)) + b_ref[2].astype(F32)
        sq_ref[...] = ((p_ref[...] + q_ref[0].astype(F32)) + q_ref[1].astype(F32)) + q_ref[2].astype(F32)

    rows = 128
    steps = (D // 2) // rows
    sq_rows = SQ_ROWS // steps
    return pl.pallas_call(
        body, name="sum_chips",
        grid_spec=pltpu.PrefetchScalarGridSpec(
            num_scalar_prefetch=1, grid=(steps,),
            in_specs=[pl.BlockSpec((rows, SHARD_W), lambda i, jc: (i, jc[0])),
                      pl.BlockSpec((3, rows, SHARD_W), lambda i, jc: (0, i, 0)),
                      pl.BlockSpec((3, sq_rows, D // 2), lambda i, jc: (0, jc[0] * steps + i, 0)),
                      pl.BlockSpec((3, 3, sq_rows, D // 2), lambda i, jc: (0, 0, i, 0))],
            out_specs=[pl.BlockSpec((rows, SHARD_W), lambda i, jc: (jc[1] * steps + i, 0)),
                       pl.BlockSpec((3, sq_rows, D // 2), lambda i, jc: (0, i, jc[1]))]),
        out_shape=[jax.ShapeDtypeStruct((D, SHARD_W), F32), jax.ShapeDtypeStruct((3, SQ_ROWS, D), F32)],
        compiler_params=_params(("arbitrary",)),
    )(jc_arr, swin, win_got, ssq, sq_got)


def _join_halves(g_win, g_sq):
    def body(win_in, sq_in, win_out, sq_out, send_sems, recv_sems):
        del win_in, sq_in
        x, y, c = _place()
        sib = (x, y, 1 - c)

        def halves(h):
            return _win_half(win_out, h), sq_out.at[:, :, pl.ds(h * (D // 2), D // 2)]

        def copy(a, part):
            return pltpu.make_async_remote_copy(src_ref=part, dst_ref=part, send_sem=send_sems.at[a], recv_sem=recv_sems.at[a],
                                                device_id=sib, device_id_type=MESH)

        sent = [copy(a, part) for a, part in enumerate(halves(c))]
        for cp in sent:
            cp.start()
        for a, part in enumerate(halves(1 - c)):
            copy(a, part).wait_recv()
        for cp in sent:
            cp.wait_send()

    return pl.pallas_call(
        body, name="join_halves",
        in_specs=[HBM_SPEC, HBM_SPEC], out_specs=[HBM_SPEC, HBM_SPEC], input_output_aliases={0: 0, 1: 1},
        out_shape=[jax.ShapeDtypeStruct((D, SHARD_W), F32), jax.ShapeDtypeStruct((3, SQ_ROWS, D), F32)],
        scratch_shapes=[pltpu.SemaphoreType.DMA((2,)), pltpu.SemaphoreType.DMA((2,))],
    )(g_win, g_sq)


def _allreduce_small(vec):
    def body(vec_ref, out_ref, slots, send_sems, recv_sems):
        x, y, c = _place()
        me = 4 * x + 2 * y + c
        slots[me] = vec_ref[...]
        copies = []
        for k in range(1, 8):
            fx, fy, fc = (k >> 2) & 1, (k >> 1) & 1, k & 1
            copies.append(pltpu.make_async_remote_copy(
                src_ref=vec_ref, dst_ref=slots.at[me], send_sem=send_sems.at[k - 1], recv_sem=recv_sems.at[k - 1],
                device_id=(_flip(x, fx), _flip(y, fy), _flip(c, fc)), device_id_type=MESH))
        for cp in copies:
            cp.start()
        for k in range(1, 8):
            fx, fy, fc = (k >> 2) & 1, (k >> 1) & 1, k & 1
            src = 4 * _flip(x, fx) + 2 * _flip(y, fy) + _flip(c, fc)
            pltpu.make_async_remote_copy(src_ref=vec_ref, dst_ref=slots.at[src], send_sem=send_sems.at[k - 1],
                                         recv_sem=recv_sems.at[k - 1], device_id=(x, y, c), device_id_type=MESH).wait_recv()
        for cp in copies:
            cp.wait_send()
        total = slots[0]
        for s in range(1, 8):
            total = total + slots[s]
        out_ref[...] = total

    return pl.pallas_call(
        body, name="allreduce_small",
        in_specs=[pl.BlockSpec(memory_space=pltpu.VMEM)], out_specs=pl.BlockSpec(memory_space=pltpu.VMEM),
        out_shape=jax.ShapeDtypeStruct((8, D), F32),
        scratch_shapes=[pltpu.VMEM((8, 8, D), F32), pltpu.SemaphoreType.DMA((7,)), pltpu.SemaphoreType.DMA((7,))],
    )(vec)


def _adamw_math(w, g, m, v):
    m = ADAM_B1 * m + (1.0 - ADAM_B1) * g
    v = ADAM_B2 * v + (1.0 - ADAM_B2) * (g * g)
    m_hat = m / (1.0 - ADAM_B1 ** ADAM_STEP)
    v_hat = v / (1.0 - ADAM_B2 ** ADAM_STEP)
    delta = -ADAM_LR * (m_hat / (jnp.sqrt(v_hat) + ADAM_EPS) + ADAM_WD * w)
    return delta, m, v


def _adamw(name, w, g, m, v, rows):
    R, C = w.shape

    def body(w_ref, g_ref, m_ref, v_ref, d_out, m_out, v_out):
        d_out[...], m_out[...], v_out[...] = _adamw_math(w_ref[...], g_ref[...], m_ref[...], v_ref[...])

    spec = pl.BlockSpec((rows, C), lambda i: (i, 0))
    return pl.pallas_call(
        body, name=name, grid=(R // rows,), in_specs=[spec] * 4, out_specs=[spec] * 3,
        out_shape=[jax.ShapeDtypeStruct((R, C), F32)] * 3,
        compiler_params=_params(("parallel",)),
    )(w, g, m, v)


def _adamw_square(g_sq, ws, ms, vs):
    def body(g_ref, *refs):
        w_refs, m_refs, v_refs, outs = refs[0:3], refs[3:6], refs[6:9], refs[9:]
        for k in range(3):
            g = g_ref[k]
            outs[k][0] = g
            outs[3 + k][0], outs[6 + k][0], outs[9 + k][0] = _adamw_math(w_refs[k][0], g, m_refs[k][0], v_refs[k][0])

    out = pl.pallas_call(
        body, name="adamw_square", out_shape=[jax.ShapeDtypeStruct((1, SQ_ROWS, D), F32)] * 12,
        compiler_params=_params(),
    )(g_sq, *ws, *ms, *vs)
    return out[0:3], out[3:6], out[6:9], out[9:12]


def _small_update(total, lbw, w8, m8, v8):
    def body(t_ref, lbw_ref, w_ref, m_ref, v_ref, g_out, d_out, m_out, v_out):
        lb = 1.0 / (1.0 + jnp.exp(lbw_ref[1:2, :] - lbw_ref[0:1, :]))
        dlb = t_ref[2:3, :] * lb * (1.0 - lb)
        g_out[...] = jnp.zeros_like(g_out)
        g_out[0:1, :] = t_ref[3:4, :]
        g_out[1:2, :] = dlb
        g_out[2:3, :] = -dlb
        g_out[3:4, :] = t_ref[1:2, :]
        g_out[4:5, :] = t_ref[0:1, :]
        g_out[5:6, :] = t_ref[4:5, :]
        d_out[...], m_out[...], v_out[...] = _adamw_math(w_ref[...], g_out[...], m_ref[...], v_ref[...])

    return pl.pallas_call(
        body, name="small_update", out_shape=[jax.ShapeDtypeStruct((8, D), F32)] * 4,
        compiler_params=_params(),
    )(total, lbw, w8, m8, v8)


def _pack8(norm_w, lbw, hnw, fnw, sinks):
    pad = jnp.zeros((1, D - 16), F32)
    return jnp.concatenate([norm_w, lbw, hnw, fnw.reshape(1, D), jnp.concatenate([sinks, pad], axis=1),
                            jnp.zeros((2, D), F32)], axis=0)


def _unpack8(a):
    return a[0:1], a[1:3], a[3:4], a[5:6, 0:16], a[4]


def _local_step(order_arr, x, tgt, norm_w, lbw, hnw, sinks, fnw, win_mine, wsq_mine, exchange):
    proj, xnt_bf, win_bf = _fwd_proj(order_arr, x, norm_w, win_mine)
    oh, states, wsq_all = _hgrn_fwd(proj, lbw, wsq_mine)
    wsq_bf = wsq_all.reshape(SHARDS, 3, SQ_ROWS, D).transpose(1, 0, 2, 3).reshape(3, D, D)
    oa = _attn_fwd(proj, sinks)
    dx2, doh, doa, dhg, dtail, lhs, rhs, loss8, vec_mid = _mid(x, tgt, proj, oh, oa, hnw, fnw.reshape(1, D), wsq_bf)
    gsq, gsq_b = _wgrad_square(lhs, rhs)
    dhead, dlb = _hgrn_bwd(proj, lbw, states, doh)
    daq, dakv, dsink = _attn_bwd(proj, sinks, oa, doa)
    pieces = [dhead, dhg, daq, dakv, dtail]
    sums = exchange(*_wgrad_in(xnt_bf, pieces), gsq, gsq_b)
    wt_bf = win_bf.transpose(0, 2, 1).reshape(D_IN, D)
    grad_x, gnw, win_got, sq_got = _bwd_dx(pieces, wt_bf, x, norm_w, dx2, sums[2], sums[3])
    sink_row = jnp.concatenate([dsink[:, 0].reshape(1, 16), jnp.zeros((1, D - 16), F32)], axis=1)
    loss_row = jnp.broadcast_to(loss8[0:1, 0:1], (1, D))
    vec = jnp.concatenate([vec_mid[0:2], dlb, gnw, sink_row, loss_row, jnp.zeros((2, D), F32)], axis=0)
    return grad_x, sums, (win_got, sq_got), vec


def kernel(x, norm_w, w_in, hgrn_lower_bound, hgrn_norm_w, w_branch_hgrn, attn_sinks, w_branch_attn, w_out, final_norm_w, loss_target, m_norm_w, m_w_in, m_hgrn_lower_bound, m_hgrn_norm_w, m_w_branch_hgrn, m_attn_sinks, m_w_branch_attn, m_w_out, m_final_norm_w, v_norm_w, v_w_in, v_hgrn_lower_bound, v_hgrn_norm_w, v_w_branch_hgrn, v_attn_sinks, v_w_branch_attn, v_w_out, v_final_norm_w):
    c_arr = lax.axis_index("c").astype(jnp.int32).reshape(1)
    j_arr = (2 * lax.axis_index("x") + lax.axis_index("y")).astype(jnp.int32).reshape(1)
    jc_arr = jnp.concatenate([j_arr, c_arr])

    win_mine, wsq_mine = _cast_shards(j_arr, w_in[0], w_branch_hgrn[0], w_branch_attn[0], w_out[0])
    xi, yi = lax.axis_index("x"), lax.axis_index("y")
    order_arr = jnp.stack([2 * xi + yi] + [2 * _flip(xi, fx) + _flip(yi, fy) for fx, fy in CHIP_FLIPS]).astype(jnp.int32)

    def chip_sums(gwin, gwin_b, gsq, gsq_b):
        return _add_halves(c_arr, gwin, gsq, *_swap_halves(gwin_b, gsq_b))

    grad_x, (swin, ssq, _, _), arrived, vec = _local_step(
        order_arr, x[0], loss_target[0], norm_w, hgrn_lower_bound, hgrn_norm_w, attn_sinks, final_norm_w, win_mine, wsq_mine,
        chip_sums)
    g_win, g_sq = _join_halves(*_sum_chips(jc_arr, swin, ssq, *arrived))

    d_win, nm_win, nv_win = _adamw("adamw_w_in", w_in[0], g_win, m_w_in[0], v_w_in[0], 128)
    g_sqs, d_sqs, nm_sqs, nv_sqs = _adamw_square(
        g_sq, (w_branch_hgrn, w_branch_attn, w_out), (m_w_branch_hgrn, m_w_branch_attn, m_w_out),
        (v_w_branch_hgrn, v_w_branch_attn, v_w_out))

    total = _allreduce_small(vec)
    loss = total[5, 0]
    g8, d8, nm8, nv8 = _small_update(
        total, hgrn_lower_bound,
        _pack8(norm_w, hgrn_lower_bound, hgrn_norm_w, final_norm_w, attn_sinks),
        _pack8(m_norm_w, m_hgrn_lower_bound, m_hgrn_norm_w, m_final_norm_w, m_attn_sinks),
        _pack8(v_norm_w, v_hgrn_lower_bound, v_hgrn_norm_w, v_final_norm_w, v_attn_sinks))

    def assemble(win, sq, small):
        nw, lb, hn, sk, fn = _unpack8(small)
        return (nw, win.reshape(1, D, SHARD_W), lb, hn, sq[0], sk, sq[1], sq[2], fn)

    return (loss, grad_x.reshape(1, -1, D),
            *assemble(g_win, g_sqs, g8), *assemble(d_win, d_sqs, d8),
            *assemble(nm_win, nm_sqs, nm8), *assemble(nv_win, nv_sqs, nv8))
```

```python
import functools

import jax
import jax.numpy as jnp
from jax import lax
from jax.experimental import pallas as pl
from jax.experimental.pallas import tpu as pltpu

F32 = jnp.float32
BF16 = jnp.bfloat16

D = 1024
D_IN = 8704
SHARDS = 4
SHARD_W = D_IN // SHARDS
SQ_ROWS = D // SHARDS
HEADS = 8
HEAD_W = 128
CHUNK = 64
SUB = 8
ATT_BLOCK = 128
ATT_STEP = 4
KV_HEADS = 4
HEAD_DIM = 64
EPS = 1e-6
NEG = -1e30
SCALE = HEAD_DIM ** -0.5
COL_HG, COL_AQ, COL_AK, COL_AV, COL_AG, COL_MH, COL_MA = 3072, 4096, 5120, 5376, 5632, 6656, 7680

ADAM_LR, ADAM_B1, ADAM_B2, ADAM_EPS, ADAM_WD, ADAM_STEP = 0.001, 0.9, 0.999, 1e-08, 0.01, 10

VMEM_LIMIT = 56 * 1024 * 1024
MESH = pl.DeviceIdType.MESH
HBM_SPEC = pl.BlockSpec(memory_space=pltpu.HBM)
CHIP_FLIPS = ((1, 0), (0, 1), (1, 1))


def _dot(a, b):
    return jnp.dot(a, b, preferred_element_type=F32)


def _dot_nt(a, b):
    return lax.dot_general(a, b, (((1,), (1,)), ((), ())), preferred_element_type=F32)


def _dot_tn(a, b):
    return lax.dot_general(a, b, (((0,), (0,)), ((), ())), preferred_element_type=F32)


def _sigmoid(v):
    return 1.0 / (1.0 + jnp.exp(-v))


def _bf(v):
    return v.astype(BF16)


def _tri_dot2(tri, v):
    a = _bf(v)
    return _dot(tri, a) + _dot(tri, _bf(v - a.astype(F32)))


def _params(sem=None):
    return pltpu.CompilerParams(dimension_semantics=sem, vmem_limit_bytes=VMEM_LIMIT)


def _cast_shards(j_arr, win_s, wbh_s, wba_s, wout_s):
    steps = 4
    rows = D // steps

    def body(j_ref, win_ref, a_ref, b_ref, c_ref, win_o, sq_o):
        win_o[...] = _bf(win_ref[...])

        @pl.when(pl.program_id(0) == 0)
        def _():
            sq_o[0:SQ_ROWS, :] = _bf(a_ref[...])
            sq_o[SQ_ROWS:2 * SQ_ROWS, :] = _bf(b_ref[...])
            sq_o[2 * SQ_ROWS:3 * SQ_ROWS, :] = _bf(c_ref[...])

    whole = pl.BlockSpec((SQ_ROWS, D), lambda i, j: (0, 0))
    return pl.pallas_call(
        body, name="cast_shards",
        grid_spec=pltpu.PrefetchScalarGridSpec(
            num_scalar_prefetch=1, grid=(steps,),
            in_specs=[pl.BlockSpec((rows, SHARD_W), lambda i, j: (i, 0)), whole, whole, whole],
            out_specs=[pl.BlockSpec((None, rows, SHARD_W), lambda i, j: (j[0], i, 0)),
                       pl.BlockSpec((None, 3 * SQ_ROWS, D), lambda i, j: (j[0], 0, 0))]),
        out_shape=[jax.ShapeDtypeStruct((SHARDS, D, SHARD_W), BF16), jax.ShapeDtypeStruct((SHARDS, 3 * SQ_ROWS, D), BF16)],
        compiler_params=_params(("arbitrary",)),
    )(j_arr, win_s, wbh_s, wba_s, wout_s)


def _fwd_proj(order_arr, x, norm_w, win_all):
    T = x.shape[0]
    tm = min(512, T)
    nt = T // tm

    def body(order_ref, x_ref, nw_ref, win_in, proj_ref, xn_ref, win_out, w_scr, xn_scr, sems, send_sems, recv_sems):
        del win_in
        p, i = pl.program_id(0), pl.program_id(1)

        def load(n):
            return pltpu.make_async_copy(win_out.at[order_ref[n]], w_scr.at[n % 2], sems.at[n % 2])

        @pl.when((p == 0) & (i == 0))
        def _():
            _gather_start(win_out, _win_half, send_sems, recv_sems)
            load(0).start()
            load(0).wait()

        @pl.when((p == 1) & (i == 0))
        def _():
            _gather_land(win_out, _win_half, 0, send_sems, recv_sems)
            load(1).start()
            load(1).wait()

        for k in range(1, SHARDS - 1):
            @pl.when((p == k) & (i == nt // 2))
            def _():
                _gather_land(win_out, _win_half, k, send_sems, recv_sems)
                load(k + 1).start()

            @pl.when((p == k + 1) & (i == 0))
            def _():
                load(k + 1).wait()

        @pl.when(p == 0)
        def _():
            xf = x_ref[...]
            rs = lax.rsqrt(jnp.mean(xf * xf, axis=1, keepdims=True) + EPS)
            xn = _bf((xf * rs) * nw_ref[...])
            xn_scr[i] = xn
            xn_ref[...] = xn.T

        proj_ref[...] = _dot(xn_scr[i], w_scr[p % 2])

        @pl.when((p == SHARDS - 1) & (i == nt - 1))
        def _():
            _gather_drain(win_out, _win_half, send_sems, recv_sems)

    first = lambda p, i: jnp.where(p == 0, i, nt - 1)
    return pl.pallas_call(
        body, name="fwd_proj",
        grid_spec=pltpu.PrefetchScalarGridSpec(
            num_scalar_prefetch=1, grid=(SHARDS, nt),
            in_specs=[pl.BlockSpec((tm, D), lambda p, i, order: (first(p, i), 0)),
                      pl.BlockSpec((1, D), lambda p, i, order: (0, 0)), HBM_SPEC],
            out_specs=[pl.BlockSpec((tm, SHARD_W), lambda p, i, order: (i, order[p])),
                       pl.BlockSpec((D, tm), lambda p, i, order: (0, first(p, i))),
                       HBM_SPEC],
            scratch_shapes=[pltpu.VMEM((2, D, SHARD_W), BF16), pltpu.VMEM((nt, tm, D), BF16), pltpu.SemaphoreType.DMA((2,)),
                            pltpu.SemaphoreType.DMA((6,)), pltpu.SemaphoreType.DMA((6,))]),
        out_shape=[jax.ShapeDtypeStruct((T, D_IN), F32), jax.ShapeDtypeStruct((D, T), BF16),
                   jax.ShapeDtypeStruct((SHARDS, D, SHARD_W), BF16)],
        input_output_aliases={3: 2},
        compiler_params=_params(("arbitrary", "arbitrary")),
    )(order_arr, x, norm_w, win_all)


def _hgrn_gates(hq_ref, hf_ref, lbw_ref, b_scr):
    lb = 1.0 / (1.0 + jnp.exp(lbw_ref[1:2, :] - lbw_ref[0:1, :]))
    hf = hf_ref[...]
    sig = _sigmoid(hf)
    f = lb + (1.0 - lb) * sig
    g = jnp.log(f)
    hq = hq_ref[...]
    sq = _sigmoid(hq)
    q = hq * sq
    row = lax.broadcasted_iota(jnp.int32, (CHUNK, CHUNK), 0)
    col = lax.broadcasted_iota(jnp.int32, (CHUNK, CHUNK), 1)
    causal = row >= col
    b = _tri_dot2(jnp.where(causal, 1.0, 0.0).astype(BF16), g)
    b_scr[...] = b
    bc = b_scr[CHUNK - 1:CHUNK, :]
    r = b_scr[CHUNK // 2 - 1:CHUNK // 2, :]
    return dict(lb=lb, sig=sig, f=f, k=1.0 - f, hq=hq, sq=sq, q=q, b=b, bc=bc, r=r, causal=causal)


def _hgrn_fwd(proj, lbw, wsq_all):
    T = proj.shape[0]
    n = T // CHUNK

    def body(hq_ref, hf_ref, hi_ref, lbw_ref, wsq_in, o_ref, st_ref, wsq_out, s_scr, b_scr, send_sems, recv_sems):
        del wsq_in

        @pl.when(pl.program_id(0) == 0)
        def _():
            _gather_start(wsq_out, _sq_half, send_sems, recv_sems)
            s_scr[...] = jnp.zeros_like(s_scr)

        for c in range(SUB):
            rows = pl.ds(c * CHUNK, CHUNK)
            gt = _hgrn_gates(hq_ref.at[rows, :], hf_ref.at[rows, :], lbw_ref, b_scr.at[rows, :])
            b, bc, r, q, k = gt["b"], gt["bc"], gt["r"], gt["q"], gt["k"]
            qe = _bf(q * jnp.exp(b))
            qr = _bf(q * jnp.exp(b - r))
            kr = _bf(k * jnp.exp(r - b))
            kl = _bf(k * jnp.exp(bc - b))
            ebc = jnp.exp(bc)
            v = _bf(hi_ref[rows, :])
            scores = [_bf(jnp.where(gt["causal"], _dot_nt(qr[:, h * HEAD_W:(h + 1) * HEAD_W], kr[:, h * HEAD_W:(h + 1) * HEAD_W]), 0.0))
                      for h in range(HEADS)]
            for h in range(HEADS):
                sl = slice(h * HEAD_W, (h + 1) * HEAD_W)
                st = s_scr[h]
                st_ref[c, h] = st
                o_ref[rows, sl] = _dot(scores[h], v[:, sl]) + _dot_nt(qe[:, sl], _bf(st))
                s_scr[h] = ebc[:, sl] * st + _dot_tn(v[:, sl], kl[:, sl])

        @pl.when(pl.program_id(0) == (n // SUB) // 2)
        def _():
            for k in range(len(CHIP_FLIPS)):
                _gather_land(wsq_out, _sq_half, k, send_sems, recv_sems)

        @pl.when(pl.program_id(0) == n // SUB - 1)
        def _():
            _gather_drain(wsq_out, _sq_half, send_sems, recv_sems)

    col = lambda j: pl.BlockSpec((SUB * CHUNK, D), lambda i: (i, j))
    return pl.pallas_call(
        body, name="hgrn_fwd", grid=(n // SUB,),
        in_specs=[col(0), col(1), col(2), pl.BlockSpec((2, D), lambda i: (0, 0)), HBM_SPEC],
        out_specs=[pl.BlockSpec((SUB * CHUNK, D), lambda i: (i, 0)),
                   pl.BlockSpec((SUB, HEADS, HEAD_W, HEAD_W), lambda i: (i, 0, 0, 0)), HBM_SPEC],
        out_shape=[jax.ShapeDtypeStruct((T, D), F32), jax.ShapeDtypeStruct((n, HEADS, HEAD_W, HEAD_W), F32),
                   jax.ShapeDtypeStruct((SHARDS, 3 * SQ_ROWS, D), BF16)],
        input_output_aliases={4: 2},
        scratch_shapes=[pltpu.VMEM((HEADS, HEAD_W, HEAD_W), F32), pltpu.VMEM((SUB * CHUNK, D), F32),
                        pltpu.SemaphoreType.DMA((6,)), pltpu.SemaphoreType.DMA((6,))],
        compiler_params=_params(("arbitrary",)),
    )(proj, proj, proj, lbw, wsq_all)


def _hgrn_bwd(proj, lbw, states, do):
    T = proj.shape[0]
    n = T // CHUNK

    def body(hq_ref, hf_ref, hi_ref, lbw_ref, st_ref, do_ref, dp_ref, dlb_ref,
             ds_scr, b_scr, dq_scr, dk_scr, dv_scr, late_scr, early_scr, ex_scr):
        @pl.when(pl.program_id(0) == 0)
        def _():
            ds_scr[...] = jnp.zeros_like(ds_scr)
            dlb_ref[...] = jnp.zeros_like(dlb_ref)

        for c in reversed(range(SUB)):
            rows = pl.ds(c * CHUNK, CHUNK)
            gt = _hgrn_gates(hq_ref.at[rows, :], hf_ref.at[rows, :], lbw_ref, b_scr.at[rows, :])
            b, bc, r, q, k = gt["b"], gt["bc"], gt["r"], gt["q"], gt["k"]
            eb = jnp.exp(b)
            er = jnp.exp(b - r)
            erk = jnp.exp(r - b)
            el = jnp.exp(bc - b)
            ebc = jnp.exp(bc)
            qe, qr, kr, kl = _bf(q * eb), _bf(q * er), _bf(k * erk), _bf(k * el)
            v = _bf(hi_ref[rows, :])
            do_b = do_ref[rows, :]
            do_t = do_b.T
            causal_t = lax.broadcasted_iota(jnp.int32, (CHUNK, CHUNK), 0) <= lax.broadcasted_iota(jnp.int32, (CHUNK, CHUNK), 1)
            firsts = []
            for h in range(HEADS):
                sl = slice(h * HEAD_W, (h + 1) * HEAD_W)
                firsts.append((_bf(jnp.where(causal_t, _dot_nt(kr[:, sl], qr[:, sl]), 0.0)),
                               _bf(jnp.where(gt["causal"], _dot_nt(do_b[:, sl], v[:, sl]), 0.0)),
                               _bf(jnp.where(causal_t, _dot_nt(v[:, sl], do_b[:, sl]), 0.0))))
            for h in range(HEADS):
                sl = slice(h * HEAD_W, (h + 1) * HEAD_W)
                st0 = st_ref[c, h]
                dst = ds_scr[h]
                dst_b = _bf(dst)
                a_t, da, da_t = firsts[h]
                mq = _dot(da, kr[:, sl])
                mk = _dot(da_t, qr[:, sl])
                dq_in = eb[:, sl] * _dot(do_b[:, sl], _bf(st0))
                dk_in = el[:, sl] * _dot(v[:, sl], dst_b)
                dq_scr[rows, sl] = er[:, sl] * mq + dq_in
                dk_scr[rows, sl] = erk[:, sl] * mk + dk_in
                dv_scr[rows, sl] = _dot(a_t, do_b[:, sl]) + _dot_nt(kl[:, sl], dst_b)
                late_scr[rows, sl] = q[:, sl] * dq_in + qr[:, sl].astype(F32) * mq - kr[:, sl].astype(F32) * mk
                early_scr[rows, sl] = k[:, sl] * dk_in
                ex_scr[:, sl] = jnp.sum(dst * st0, axis=0, keepdims=True)
                ds_scr[h] = ebc[:, sl] * dst + _dot(do_t[sl, :], qe[:, sl])

            dq, dk = dq_scr[rows, :], dk_scr[rows, :]
            row = lax.broadcasted_iota(jnp.int32, (CHUNK, CHUNK), 0)
            col = lax.broadcasted_iota(jnp.int32, (CHUNK, CHUNK), 1)
            at_or_after = jnp.where(col >= row, 1.0, 0.0).astype(BF16)
            before = jnp.where(col < row, 1.0, 0.0).astype(BF16)
            dg = _tri_dot2(jnp.concatenate([at_or_after, before], axis=1),
                           jnp.concatenate([late_scr[rows, :], early_scr[rows, :]], axis=0)) + ebc * ex_scr[...]
            df = dg / gt["f"] - dk
            sig, sq, hq, lb = gt["sig"], gt["sq"], gt["hq"], gt["lb"]
            dp_ref[rows, 0:D] = _bf(dq * (sq * (1.0 + hq * (1.0 - sq))))
            dp_ref[rows, D:2 * D] = _bf(df * (1.0 - lb) * sig * (1.0 - sig))
            dp_ref[rows, 2 * D:3 * D] = _bf(dv_scr[rows, :])
            dlb_ref[...] += jnp.sum(df * (1.0 - sig), axis=0, keepdims=True)

    ns = n // SUB
    col = lambda j: pl.BlockSpec((SUB * CHUNK, D), lambda i: (ns - 1 - i, j))
    return pl.pallas_call(
        body, name="hgrn_bwd", grid=(ns,),
        in_specs=[col(0), col(1), col(2), pl.BlockSpec((2, D), lambda i: (0, 0)),
                  pl.BlockSpec((SUB, HEADS, HEAD_W, HEAD_W), lambda i: (ns - 1 - i, 0, 0, 0)),
                  pl.BlockSpec((SUB * CHUNK, D), lambda i: (ns - 1 - i, 0))],
        out_specs=[pl.BlockSpec((SUB * CHUNK, 3 * D), lambda i: (ns - 1 - i, 0)),
                   pl.BlockSpec((1, D), lambda i: (0, 0))],
        out_shape=[jax.ShapeDtypeStruct((T, 3 * D), BF16), jax.ShapeDtypeStruct((1, D), F32)],
        scratch_shapes=[pltpu.VMEM((HEADS, HEAD_W, HEAD_W), F32)] + [pltpu.VMEM((SUB * CHUNK, D), F32)] * 6
                       + [pltpu.VMEM((1, D), F32)],
        compiler_params=_params(("arbitrary",)),
    )(proj, proj, proj, lbw, states, do)


def _attn_masks(blk):
    qi = lax.broadcasted_iota(jnp.int32, (ATT_BLOCK, 2 * ATT_BLOCK), 0)
    kj = lax.broadcasted_iota(jnp.int32, (ATT_BLOCK, 2 * ATT_BLOCK), 1)
    band = (kj > qi) & (kj <= qi + ATT_BLOCK)
    return band & ((blk > 0) | (kj >= ATT_BLOCK))


def _head_pair_operand(t, hp, low):
    mine = low if hp == 0 else jnp.logical_not(low)
    both = jnp.where(mine, t, pltpu.roll(t, HEAD_DIM, 1))
    return _bf(jnp.concatenate([jnp.where(low, both, 0.0), jnp.where(low, 0.0, both)], axis=0))


def _attn_probs(s, sink, valid):
    s = jnp.where(valid, s, NEG)
    m = jnp.maximum(jnp.max(s, axis=1, keepdims=True), sink)
    p = jnp.exp(s - m)
    es = jnp.exp(sink - m)
    inv = 1.0 / (jnp.sum(p, axis=1, keepdims=True) + es)
    return p * inv, es * inv


def _attn_fwd(proj, sinks):
    T = proj.shape[0]
    rows_step = ATT_STEP * ATT_BLOCK

    def body(sink_ref, q_ref, kp_ref, kc_ref, vp_ref, vc_ref, o_ref):
        low = lax.broadcasted_iota(jnp.int32, (1, 2 * HEAD_DIM), 1) < HEAD_DIM
        for sb in range(ATT_STEP):
            rows = slice(sb * ATT_BLOCK, (sb + 1) * ATT_BLOCK)
            before = slice((sb - 1) * ATT_BLOCK, sb * ATT_BLOCK)
            valid = _attn_masks(pl.program_id(0) * ATT_STEP + sb)
            kcat = jnp.concatenate([kp_ref[...] if sb == 0 else kc_ref[before, :], kc_ref[rows, :]], axis=0)
            vcat = jnp.concatenate([vp_ref[...] if sb == 0 else vc_ref[before, :], vc_ref[rows, :]], axis=0)
            for h in range(KV_HEADS):
                tl = slice((h // 2) * 128, (h // 2) * 128 + 128)
                mine = low if h % 2 == 0 else jnp.logical_not(low)
                kh = _bf(jnp.where(mine, kcat[:, tl], pltpu.roll(kcat[:, tl], HEAD_DIM, 1)))
                vh = _bf(jnp.where(mine, vcat[:, tl], pltpu.roll(vcat[:, tl], HEAD_DIM, 1)))
                for t in range(2):
                    ql = slice((2 * h + t) * 128, (2 * h + t) * 128 + 128)
                    q2 = q_ref[rows, ql] * SCALE
                    outs = []
                    for p in range(2):
                        qm = _bf(jnp.where(low if p == 0 else jnp.logical_not(low), q2, 0.0))
                        probs, _ = _attn_probs(_dot_nt(qm, kh), sink_ref[0, 4 * h + 2 * t + p], valid)
                        outs.append(_dot(_bf(probs), vh))
                    o_ref[rows, ql] = jnp.where(low, outs[0], outs[1])

    prev = lambda i: jnp.maximum(ATT_STEP * i - 1, 0)
    return pl.pallas_call(
        body, name="attn_fwd", grid=(T // rows_step,),
        in_specs=[pl.BlockSpec(memory_space=pltpu.SMEM),
                  pl.BlockSpec((rows_step, D), lambda i: (i, COL_AQ // D)),
                  pl.BlockSpec((ATT_BLOCK, 256), lambda i: (prev(i), COL_AK // 256)),
                  pl.BlockSpec((rows_step, 256), lambda i: (i, COL_AK // 256)),
                  pl.BlockSpec((ATT_BLOCK, 256), lambda i: (prev(i), COL_AV // 256)),
                  pl.BlockSpec((rows_step, 256), lambda i: (i, COL_AV // 256))],
        out_specs=pl.BlockSpec((rows_step, D), lambda i: (i, 0)),
        out_shape=jax.ShapeDtypeStruct((T, D), F32),
        compiler_params=_params(("arbitrary",)),
    )(sinks, proj, proj, proj, proj, proj)


def _attn_bwd(proj, sinks, o, do):
    T = proj.shape[0]
    nb = T // ATT_BLOCK
    W2 = 2 * ATT_BLOCK

    def body(sink_ref, q_ref, kp_ref, kc_ref, vp_ref, vc_ref, o_ref, do_ref,
             dq_ref, dkv_ref, dsink_ref, ck_scr, cv_scr, nk_scr, nv_scr):
        blk = pl.program_id(0)

        @pl.when(blk == 0)
        def _():
            ck_scr[...] = jnp.zeros_like(ck_scr)
            cv_scr[...] = jnp.zeros_like(cv_scr)
            dsink_ref[...] = jnp.zeros_like(dsink_ref)

        @pl.when(blk < nb)
        def _():
            valid = _attn_masks(blk)
            low = lax.broadcasted_iota(jnp.int32, (1, 2 * HEAD_DIM), 1) < HEAD_DIM
            kcat = jnp.concatenate([kp_ref[...], kc_ref[...]], axis=0)
            vcat = jnp.concatenate([vp_ref[...], vc_ref[...]], axis=0)
            for h in range(KV_HEADS):
                tl = slice((h // 2) * 128, (h // 2) * 128 + 128)
                kbd = _head_pair_operand(kcat[:, tl], h % 2, low)
                vbd = _head_pair_operand(vcat[:, tl], h % 2, low)
                dkbd = jnp.zeros((2 * W2, 128), F32)
                dvbd = jnp.zeros((2 * W2, 128), F32)
                tiles = []
                for t in range(2):
                    ql = slice((2 * h + t) * 128, (2 * h + t) * 128 + 128)
                    q2 = _bf(q_ref[:, ql] * SCALE)
                    do2_b = do_ref[:, ql]
                    doo = do2_b.astype(F32) * o_ref[:, ql]
                    dsum0 = jnp.sum(jnp.where(low, doo, 0.0), axis=1, keepdims=True)
                    dsum1 = jnp.sum(jnp.where(low, 0.0, doo), axis=1, keepdims=True)
                    tiles.append((ql, q2, do2_b, dsum0, dsum1, _dot_nt(q2, kbd), _dot_nt(do2_b, vbd)))
                grads = []
                for t, (ql, q2, do2_b, dsum0, dsum1, s2, dp2) in enumerate(tiles):
                    head = 4 * h + 2 * t
                    p0, ps0 = _attn_probs(s2[:, 0:W2], sink_ref[0, head], valid)
                    p1, ps1 = _attn_probs(s2[:, W2:2 * W2], sink_ref[0, head + 1], valid)
                    ds2 = _bf(jnp.concatenate([p0 * (dp2[:, 0:W2] - dsum0), p1 * (dp2[:, W2:2 * W2] - dsum1)], axis=1))
                    grads.append((ds2, _bf(jnp.concatenate([p0, p1], axis=1))))
                    dsink_ref[head:head + 1, :] += jnp.zeros((1, 128), F32) - jnp.sum(ps0 * dsum0, axis=0, keepdims=True)
                    dsink_ref[head + 1:head + 2, :] += jnp.zeros((1, 128), F32) - jnp.sum(ps1 * dsum1, axis=0, keepdims=True)
                for (ql, q2, do2_b, _, _, _, _), (ds2, p2) in zip(tiles, grads):
                    dq_ref[:, ql] = _bf(_dot(ds2, kbd) * SCALE)
                    dkbd = dkbd + _dot_tn(ds2, q2)
                    dvbd = dvbd + _dot_tn(p2, do2_b)
                dk2 = jnp.where(low, dkbd[0:W2], dkbd[W2:2 * W2])
                dv2 = jnp.where(low, dvbd[0:W2], dvbd[W2:2 * W2])
                dk2 = dk2 + pltpu.roll(dk2, HEAD_DIM, 1)
                dv2 = dv2 + pltpu.roll(dv2, HEAD_DIM, 1)
                if h % 2 == 0:
                    keep_k, keep_v = dk2, dv2
                else:
                    nk_scr[:, tl] = jnp.where(low, keep_k, dk2)
                    nv_scr[:, tl] = jnp.where(low, keep_v, dv2)
            dkv_ref[:, 0:256] = _bf(ck_scr[...] + nk_scr[0:ATT_BLOCK, :])
            dkv_ref[:, 256:512] = _bf(cv_scr[...] + nv_scr[0:ATT_BLOCK, :])
            ck_scr[...] = nk_scr[ATT_BLOCK:2 * ATT_BLOCK, :]
            cv_scr[...] = nv_scr[ATT_BLOCK:2 * ATT_BLOCK, :]

        @pl.when(blk == nb)
        def _():
            dkv_ref[:, 0:256] = _bf(ck_scr[...])
            dkv_ref[:, 256:512] = _bf(cv_scr[...])

    cur = lambda i: jnp.minimum(i, nb - 1)
    prev = lambda i: jnp.maximum(cur(i) - 1, 0)
    late = lambda i: jnp.maximum(i - 1, 0)
    return pl.pallas_call(
        body, name="attn_bwd", grid=(nb + 1,),
        in_specs=[pl.BlockSpec(memory_space=pltpu.SMEM),
                  pl.BlockSpec((ATT_BLOCK, D), lambda i: (cur(i), COL_AQ // D)),
                  pl.BlockSpec((ATT_BLOCK, 256), lambda i: (prev(i), COL_AK // 256)),
                  pl.BlockSpec((ATT_BLOCK, 256), lambda i: (cur(i), COL_AK // 256)),
                  pl.BlockSpec((ATT_BLOCK, 256), lambda i: (prev(i), COL_AV // 256)),
                  pl.BlockSpec((ATT_BLOCK, 256), lambda i: (cur(i), COL_AV // 256)),
                  pl.BlockSpec((ATT_BLOCK, D), lambda i: (cur(i), 0)),
                  pl.BlockSpec((ATT_BLOCK, D), lambda i: (cur(i), 0))],
        out_specs=[pl.BlockSpec((ATT_BLOCK, D), lambda i: (cur(i), 0)),
                   pl.BlockSpec((ATT_BLOCK, 512), lambda i: (late(i), 0)),
                   pl.BlockSpec((16, 128), lambda i: (0, 0))],
        out_shape=[jax.ShapeDtypeStruct((T, D), BF16), jax.ShapeDtypeStruct((T, 512), BF16),
                   jax.ShapeDtypeStruct((16, 128), F32)],
        scratch_shapes=[pltpu.VMEM((ATT_BLOCK, 256), F32), pltpu.VMEM((ATT_BLOCK, 256), F32),
                        pltpu.VMEM((2 * ATT_BLOCK, 256), F32), pltpu.VMEM((2 * ATT_BLOCK, 256), F32)],
        compiler_params=_params(("arbitrary",)),
    )(sinks, proj, proj, proj, proj, proj, o, do)


def _mid(x, tgt, proj, oh, oa, hnw, fnw, wsq_bf):
    T = x.shape[0]
    tm = min(256, T)
    nt = T // tm

    def body(x_ref, tgt_ref, oh_ref, oa_ref, hg_ref, ag0_ref, ag1_ref, mh0_ref, mh1_ref, ma0_ref, ma1_ref,
             hnw_ref, fnw_ref, w_hbm,
             dx2_ref, doh_ref, doa_ref, dhg_ref, dtail_ref, lhs_ref, rhs_ref, loss_ref, vec_ref,
             w_scr, xh_scr, rs_scr, sem):
        @pl.when(pl.program_id(0) == 0)
        def _():
            cp = pltpu.make_async_copy(w_hbm, w_scr, sem)
            cp.start()
            cp.wait()
            loss_ref[...] = jnp.zeros_like(loss_ref)
            vec_ref[...] = jnp.zeros_like(vec_ref)

        oh = oh_ref[...]
        for h in range(HEADS):
            sl = slice(h * HEAD_W, (h + 1) * HEAD_W)
            ohh = oh[:, sl]
            rs = lax.rsqrt(jnp.mean(ohh * ohh, axis=1, keepdims=True) + EPS)
            xh_scr[:, sl] = ohh * rs
            rs_scr[:, sl] = jnp.broadcast_to(rs, (tm, HEAD_W))
        xh = xh_scr[...]
        hnw = hnw_ref[...]
        on = xh * hnw
        hg = hg_ref[...]
        sg = _sigmoid(hg)
        silu_g = hg * sg
        gated_h = _bf(on * silu_g)
        oa = oa_ref[...]
        ag = jnp.concatenate([ag0_ref[...], ag1_ref[...]], axis=1)
        sa = _sigmoid(ag)
        silu_a = ag * sa
        gated_a = _bf(oa * silu_a)
        yh = _dot(gated_h, w_scr[0])
        ya = _dot(gated_a, w_scr[1])
        lhs_ref[0] = gated_h.T
        lhs_ref[1] = gated_a.T
        smh = _sigmoid(jnp.concatenate([mh0_ref[...], mh1_ref[...]], axis=1))
        sma = _sigmoid(jnp.concatenate([ma0_ref[...], ma1_ref[...]], axis=1))
        merged = _bf(smh * yh + sma * ya)
        lhs_ref[2] = merged.T
        x2 = x_ref[...] + _dot(merged, w_scr[2])
        rs2 = lax.rsqrt(jnp.mean(x2 * x2, axis=1, keepdims=True) + EPS)
        xh2 = x2 * rs2
        fnw = fnw_ref[...]
        diff = xh2 * fnw - tgt_ref[...]
        loss_ref[...] += jnp.zeros_like(loss_ref) + jnp.sum(diff * diff) * (0.5 / D)

        dy = diff * (1.0 / D)
        vec_ref[0:1, :] += jnp.sum(dy * xh2, axis=0, keepdims=True)
        gy = dy * fnw
        dx2 = rs2 * (gy - xh2 * jnp.mean(gy * xh2, axis=1, keepdims=True))
        dx2_ref[...] = dx2
        dx2_b = _bf(dx2)
        rhs_ref[2] = dx2_b
        dmerged = _dot_nt(dx2_b, w_scr[2])
        dyh = dmerged * smh
        dya = dmerged * sma
        dtail_ref[:, D:2 * D] = _bf(dyh * yh * (1.0 - smh))
        dtail_ref[:, 2 * D:3 * D] = _bf(dya * ya * (1.0 - sma))
        dyh_b, dya_b = _bf(dyh), _bf(dya)
        rhs_ref[0] = dyh_b
        rhs_ref[1] = dya_b
        dgh = _dot_nt(dyh_b, w_scr[0])
        dga = _dot_nt(dya_b, w_scr[1])
        don = dgh * silu_g
        dhg_ref[...] = _bf(dgh * on * (sg * (1.0 + hg * (1.0 - sg))))
        vec_ref[1:2, :] += jnp.sum(don * xh, axis=0, keepdims=True)
        gxh = don * hnw
        rsb = rs_scr[...]
        for h in range(HEADS):
            sl = slice(h * HEAD_W, (h + 1) * HEAD_W)
            gh, xhh = gxh[:, sl], xh[:, sl]
            doh_ref[:, sl] = _bf(rsb[:, sl] * (gh - xhh * jnp.mean(gh * xhh, axis=1, keepdims=True)))
        doa_ref[...] = _bf(dga * silu_a)
        dtail_ref[:, 0:D] = _bf(dga * oa * (sa * (1.0 + ag * (1.0 - sa))))

    row = lambda w, j: pl.BlockSpec((tm, w), lambda i: (i, j))
    const = lambda r, c: pl.BlockSpec((r, c), lambda i: (0, 0))
    stack = pl.BlockSpec((3, tm, D), lambda i: (0, i, 0))
    stack_t = pl.BlockSpec((3, D, tm), lambda i: (0, 0, i))
    return pl.pallas_call(
        body, name="mid", grid=(nt,),
        in_specs=[row(D, 0), row(D, 0), row(D, 0), row(D, 0), row(D, COL_HG // D),
                  row(512, COL_AG // 512), row(512, COL_AG // 512 + 1),
                  row(512, COL_MH // 512), row(512, COL_MH // 512 + 1),
                  row(512, COL_MA // 512), row(512, COL_MA // 512 + 1),
                  const(1, D), const(1, D), HBM_SPEC],
        out_specs=[row(D, 0), row(D, 0), row(D, 0), row(D, 0), row(3 * D, 0), stack_t, stack, const(8, 128), const(8, D)],
        out_shape=[jax.ShapeDtypeStruct((T, D), F32), jax.ShapeDtypeStruct((T, D), BF16), jax.ShapeDtypeStruct((T, D), BF16),
                   jax.ShapeDtypeStruct((T, D), BF16), jax.ShapeDtypeStruct((T, 3 * D), BF16),
                   jax.ShapeDtypeStruct((3, D, T), BF16), jax.ShapeDtypeStruct((3, T, D), BF16),
                   jax.ShapeDtypeStruct((8, 128), F32), jax.ShapeDtypeStruct((8, D), F32)],
        scratch_shapes=[pltpu.VMEM((3, D, D), BF16), pltpu.VMEM((tm, D), F32), pltpu.VMEM((tm, D), F32),
                        pltpu.SemaphoreType.DMA],
        compiler_params=_params(("arbitrary",)),
    )(x, tgt, oh, oa, proj, proj, proj, proj, proj, proj, proj, hnw, fnw, wsq_bf)


def _wgrad_square(lhs_t, rhs):
    T = rhs.shape[1]
    tk = min(2048, T)
    steps = T // tk

    def body(a_ref, b_ref, g_ref, gb_ref):
        part = _dot(a_ref[...], b_ref[...])

        @pl.when(pl.program_id(1) == 0)
        def _():
            g_ref[...] = part

        @pl.when(pl.program_id(1) > 0)
        def _():
            g_ref[...] += part

        @pl.when(pl.program_id(1) == steps - 1)
        def _():
            gb_ref[...] = _bf(g_ref[...])

    return pl.pallas_call(
        body, name="wgrad_square", grid=(3, steps),
        in_specs=[pl.BlockSpec((None, D, tk), lambda k, i: (k, 0, i)), pl.BlockSpec((None, tk, D), lambda k, i: (k, i, 0))],
        out_specs=[pl.BlockSpec((None, D, D), lambda k, i: (k, 0, 0))] * 2,
        out_shape=[jax.ShapeDtypeStruct((3, D, D), F32), jax.ShapeDtypeStruct((3, D, D), BF16)],
        compiler_params=_params(("parallel", "arbitrary")),
    )(lhs_t, rhs)


def _bwd_dx(pieces, wt_bf, x, norm_w, dx2, swin_b, ssq_b):
    T = x.shape[0]
    tm = min(512, T)
    nt = T // tm
    widths = [p.shape[1] for p in pieces]
    n_p = len(pieces)

    def body(*refs):
        piece_refs = refs[:n_p]
        (w_hbm, x_ref, nw_ref, dx2_ref, swin_ref, ssq_ref,
         gx_ref, gnw_ref, win_got, sq_got, w_scr, sem, send_sems, recv_sems) = refs[n_p:]

        def scatter_copies():
            x_, y_, c_ = _place()
            copies = []
            for k, (fx, fy) in enumerate(CHIP_FLIPS):
                px, py = _flip(x_, fx), _flip(y_, fy)
                jr = 2 * px + py
                for a, (src, dst) in enumerate(((swin_ref.at[:, pl.ds(jr * SHARD_W, SHARD_W)], win_got.at[k]),
                                                (ssq_ref.at[:, pl.ds(jr * SQ_ROWS, SQ_ROWS), :], sq_got.at[k]))):
                    copies.append(pltpu.make_async_remote_copy(
                        src_ref=src, dst_ref=dst, send_sem=send_sems.at[2 * k + a], recv_sem=recv_sems.at[2 * k + a],
                        device_id=(px, py, c_), device_id_type=MESH))
            return copies

        @pl.when(pl.program_id(0) == 0)
        def _():
            for cp in scatter_copies():
                cp.start()
            cp = pltpu.make_async_copy(w_hbm, w_scr, sem)
            cp.start()
            cp.wait()
            gnw_ref[...] = jnp.zeros_like(gnw_ref)

        dxn = None
        off = 0
        for ref, w in zip(piece_refs, widths):
            part = _dot(ref[...], w_scr[off:off + w, :])
            dxn = part if dxn is None else dxn + part
            off += w
        xf = x_ref[...]
        rs = lax.rsqrt(jnp.mean(xf * xf, axis=1, keepdims=True) + EPS)
        xh = xf * rs
        gnw_ref[...] += jnp.sum(dxn * xh, axis=0, keepdims=True)
        gx = dxn * nw_ref[...]
        gx_ref[...] = rs * (gx - xh * jnp.mean(gx * xh, axis=1, keepdims=True)) + dx2_ref[...]

        @pl.when(pl.program_id(0) == nt - 1)
        def _():
            for cp in scatter_copies():
                cp.wait()

    row = lambda w: pl.BlockSpec((tm, w), lambda i: (i, 0))
    return pl.pallas_call(
        body, name="bwd_dx", grid=(nt,),
        in_specs=[row(w) for w in widths] + [HBM_SPEC, row(D), pl.BlockSpec((1, D), lambda i: (0, 0)), row(D), HBM_SPEC, HBM_SPEC],
        out_specs=[row(D), pl.BlockSpec((1, D), lambda i: (0, 0)), HBM_SPEC, HBM_SPEC],
        out_shape=[jax.ShapeDtypeStruct((T, D), F32), jax.ShapeDtypeStruct((1, D), F32),
                   jax.ShapeDtypeStruct((3, D // 2, SHARD_W), BF16), jax.ShapeDtypeStruct((3, 3, SQ_ROWS, D // 2), BF16)],
        scratch_shapes=[pltpu.VMEM((D_IN, D), BF16), pltpu.SemaphoreType.DMA,
                        pltpu.SemaphoreType.DMA((6,)), pltpu.SemaphoreType.DMA((6,))],
        compiler_params=_params(("arbitrary",)),
    )(*pieces, wt_bf, x, norm_w, dx2, swin_b, ssq_b)


W_PIECES = ((0, 1024, 3), (COL_HG, 1024, 1), (COL_AQ, 1024, 1), (COL_AK, 512, 1), (COL_AG, 512, 6))


def _wgrad_in(xnt_bf, pieces):
    T = xnt_bf.shape[1]
    bufs = ()
    for n, (piece, (col, wb, blocks)) in enumerate(zip(pieces, W_PIECES)):
        tk = min(4096, T)
        steps = T // tk

        def body(xnt_ref, p_ref, *rest):
            g_ref, gb_ref = rest[-2:]
            part = _dot(xnt_ref[...], p_ref[...])

            @pl.when(pl.program_id(1) == 0)
            def _():
                g_ref[...] = part

            @pl.when(pl.program_id(1) > 0)
            def _():
                g_ref[...] += part

            @pl.when(pl.program_id(1) == steps - 1)
            def _():
                gb_ref[...] = _bf(g_ref[...])

        out = pl.BlockSpec((D, wb), lambda jb, i, base=col // wb: (0, base + jb))
        bufs = pl.pallas_call(
            body, name=f"wgrad_in_{n}", grid=(blocks, steps),
            in_specs=[pl.BlockSpec((D, tk), lambda jb, i: (0, i)), pl.BlockSpec((tk, wb), lambda jb, i: (i, jb))]
                     + [HBM_SPEC] * len(bufs),
            out_specs=[out, out],
            out_shape=[jax.ShapeDtypeStruct((D, D_IN), F32), jax.ShapeDtypeStruct((D, D_IN), BF16)],
            input_output_aliases={2: 0, 3: 1} if bufs else {},
            compiler_params=_params(("parallel", "arbitrary")),
        )(xnt_bf, piece, *bufs)
    return bufs


def _place():
    return lax.axis_index("x"), lax.axis_index("y"), lax.axis_index("c")


def _flip(v, f):
    return 1 - v if f else v


def _win_half(ref, h):
    return ref.at[pl.ds(h * (D // 2), D // 2), :]


def _sq_half(ref, h):
    return ref.at[:, pl.ds(h * (D // 2), D // 2)]


def _gather_copy(part, k, to, send_sems, recv_sems):
    return pltpu.make_async_remote_copy(src_ref=part, dst_ref=part, send_sem=send_sems.at[k], recv_sem=recv_sems.at[k],
                                        device_id=to, device_id_type=MESH)


def _gather_start(out, half, send_sems, recv_sems):
    x, y, c = _place()
    for k, (fx, fy) in enumerate(CHIP_FLIPS):
        _gather_copy(half(out.at[2 * x + y], c), k, (_flip(x, fx), _flip(y, fy), c), send_sems, recv_sems).start()


def _gather_land(out, half, k, send_sems, recv_sems):
    x, y, c = _place()
    sib = (x, y, 1 - c)
    fx, fy = CHIP_FLIPS[k]
    slot = out.at[2 * _flip(x, fx) + _flip(y, fy)]
    _gather_copy(half(slot, c), k, sib, send_sems, recv_sems).wait_recv()
    _gather_copy(half(slot, c), 3 + k, sib, send_sems, recv_sems).start()
    _gather_copy(half(slot, 1 - c), 3 + k, sib, send_sems, recv_sems).wait_recv()


def _gather_drain(out, half, send_sems, recv_sems):
    x, y, c = _place()
    for k, (fx, fy) in enumerate(CHIP_FLIPS):
        _gather_copy(half(out.at[2 * x + y], c), k, (_flip(x, fx), _flip(y, fy), c), send_sems, recv_sems).wait_send()
        _gather_copy(half(out.at[2 * _flip(x, fx) + _flip(y, fy)], c), 3 + k, (x, y, 1 - c), send_sems, recv_sems).wait_send()


def _swap_halves(gwin, gsq):
    def body(gwin_ref, gsq_ref, win_got, sq_got, send_sems, recv_sems):
        x, y, c = _place()
        sib = (x, y, 1 - c)
        pairs = ((_win_half(gwin_ref, 1 - c), win_got),
                 (gsq_ref.at[:, :, pl.ds((1 - c) * (D // 2), D // 2)], sq_got))
        copies = [pltpu.make_async_remote_copy(src_ref=src, dst_ref=dst, send_sem=send_sems.at[a], recv_sem=recv_sems.at[a],
                                               device_id=sib, device_id_type=MESH) for a, (src, dst) in enumerate(pairs)]
        for cp in copies:
            cp.start()
        for cp in copies:
            cp.wait()

    return pl.pallas_call(
        body, name="swap_halves",
        in_specs=[HBM_SPEC, HBM_SPEC], out_specs=[HBM_SPEC, HBM_SPEC],
        out_shape=[jax.ShapeDtypeStruct((D // 2, D_IN), BF16), jax.ShapeDtypeStruct((3, D, D // 2), BF16)],
        scratch_shapes=[pltpu.SemaphoreType.DMA((2,)), pltpu.SemaphoreType.DMA((2,))],
    )(gwin, gsq)


def _add_halves(c_arr, gwin, gsq, win_got, sq_got):
    def body(c_ref, a_ref, b_ref, p_ref, q_ref, so_ref, sq_ref, sob_ref, sqb_ref):
        so = a_ref[...] + b_ref[...].astype(F32)
        sq = p_ref[...] + q_ref[...].astype(F32)
        so_ref[...] = so
        sq_ref[...] = sq
        sob_ref[...] = _bf(so)
        sqb_ref[...] = _bf(sq)

    steps = 8
    rows, sq_rows = (D // 2) // steps, D // steps
    win = lambda f: pl.BlockSpec((rows, D_IN), f)
    sq = lambda f: pl.BlockSpec((3, sq_rows, D // 2), f)
    return pl.pallas_call(
        body, name="add_halves",
        grid_spec=pltpu.PrefetchScalarGridSpec(
            num_scalar_prefetch=1, grid=(steps,),
            in_specs=[win(lambda i, c: (c[0] * steps + i, 0)), win(lambda i, c: (i, 0)),
                      sq(lambda i, c: (0, i, c[0])), sq(lambda i, c: (0, i, 0))],
            out_specs=[win(lambda i, c: (i, 0)), sq(lambda i, c: (0, i, 0))] * 2),
        out_shape=[jax.ShapeDtypeStruct((D // 2, D_IN), F32), jax.ShapeDtypeStruct((3, D, D // 2), F32),
                   jax.ShapeDtypeStruct((D // 2, D_IN), BF16), jax.ShapeDtypeStruct((3, D, D // 2), BF16)],
        compiler_params=_params(("arbitrary",)),
    )(c_arr, gwin, win_got, gsq, sq_got)


def _sum_chips(jc_arr, swin, ssq, win_got, sq_got):
    def body(jc_ref, a_ref, b_ref, p_ref, q_ref, so_ref, sq_ref):
        so_ref[...] = ((a_ref[...] + b_ref[0].astype(F32)) + b_ref[1].astype(F32)) + b_ref[2].astype(F32)
        sq_ref[...] = ((p_ref[...] + q_ref[0].astype(F32)) + q_ref[1].astype(F32)) + q_ref[2].astype(F32)

    rows = 128
    steps = (D // 2) // rows
    sq_rows = SQ_ROWS // steps
    return pl.pallas_call(
        body, name="sum_chips",
        grid_spec=pltpu.PrefetchScalarGridSpec(
            num_scalar_prefetch=1, grid=(steps,),
            in_specs=[pl.BlockSpec((rows, SHARD_W), lambda i, jc: (i, jc[0])),
                      pl.BlockSpec((3, rows, SHARD_W), lambda i, jc: (0, i, 0)),
                      pl.BlockSpec((3, sq_rows, D // 2), lambda i, jc: (0, jc[0] * steps + i, 0)),
                      pl.BlockSpec((3, 3, sq_rows, D // 2), lambda i, jc: (0, 0, i, 0))],
            out_specs=[pl.BlockSpec((rows, SHARD_W), lambda i, jc: (jc[1] * steps + i, 0)),
                       pl.BlockSpec((3, sq_rows, D // 2), lambda i, jc: (0, i, jc[1]))]),
        out_shape=[jax.ShapeDtypeStruct((D, SHARD_W), F32), jax.ShapeDtypeStruct((3, SQ_ROWS, D), F32)],
        compiler_params=_params(("arbitrary",)),
    )(jc_arr, swin, win_got, ssq, sq_got)


def _join_halves(g_win, g_sq):
    def body(win_in, sq_in, win_out, sq_out, send_sems, recv_sems):
        del win_in, sq_in
        x, y, c = _place()
        sib = (x, y, 1 - c)

        def halves(h):
            return _win_half(win_out, h), sq_out.at[:, :, pl.ds(h * (D // 2), D // 2)]

        def copy(a, part):
            return pltpu.make_async_remote_copy(src_ref=part, dst_ref=part, send_sem=send_sems.at[a], recv_sem=recv_sems.at[a],
                                                device_id=sib, device_id_type=MESH)

        sent = [copy(a, part) for a, part in enumerate(halves(c))]
        for cp in sent:
            cp.start()
        for a, part in enumerate(halves(1 - c)):
            copy(a, part).wait_recv()
        for cp in sent:
            cp.wait_send()

    return pl.pallas_call(
        body, name="join_halves",
        in_specs=[HBM_SPEC, HBM_SPEC], out_specs=[HBM_SPEC, HBM_SPEC], input_output_aliases={0: 0, 1: 1},
        out_shape=[jax.ShapeDtypeStruct((D, SHARD_W), F32), jax.ShapeDtypeStruct((3, SQ_ROWS, D), F32)],
        scratch_shapes=[pltpu.SemaphoreType.DMA((2,)), pltpu.SemaphoreType.DMA((2,))],
    )(g_win, g_sq)


def _allreduce_small(vec):
    def body(vec_ref, out_ref, slots, send_sems, recv_sems):
        x, y, c = _place()
        me = 4 * x + 2 * y + c
        slots[me] = vec_ref[...]
        copies = []
        for k in range(1, 8):
            fx, fy, fc = (k >> 2) & 1, (k >> 1) & 1, k & 1
            copies.append(pltpu.make_async_remote_copy(
                src_ref=vec_ref, dst_ref=slots.at[me], send_sem=send_sems.at[k - 1], recv_sem=recv_sems.at[k - 1],
                device_id=(_flip(x, fx), _flip(y, fy), _flip(c, fc)), device_id_type=MESH))
        for cp in copies:
            cp.start()
        for k in range(1, 8):
            fx, fy, fc = (k >> 2) & 1, (k >> 1) & 1, k & 1
            src = 4 * _flip(x, fx) + 2 * _flip(y, fy) + _flip(c, fc)
            pltpu.make_async_remote_copy(src_ref=vec_ref, dst_ref=slots.at[src], send_sem=send_sems.at[k - 1],
                                         recv_sem=recv_sems.at[k - 1], device_id=(x, y, c), device_id_type=MESH).wait_recv()
        for cp in copies:
            cp.wait_send()
        total = slots[0]
        for s in range(1, 8):
            total = total + slots[s]
        out_ref[...] = total

    return pl.pallas_call(
        body, name="allreduce_small",
        in_specs=[pl.BlockSpec(memory_space=pltpu.VMEM)], out_specs=pl.BlockSpec(memory_space=pltpu.VMEM),
        out_shape=jax.ShapeDtypeStruct((8, D), F32),
        scratch_shapes=[pltpu.VMEM((8, 8, D), F32), pltpu.SemaphoreType.DMA((7,)), pltpu.SemaphoreType.DMA((7,))],
    )(vec)


def _adamw_math(w, g, m, v):
    m = ADAM_B1 * m + (1.0 - ADAM_B1) * g
    v = ADAM_B2 * v + (1.0 - ADAM_B2) * (g * g)
    m_hat = m / (1.0 - ADAM_B1 ** ADAM_STEP)
    v_hat = v / (1.0 - ADAM_B2 ** ADAM_STEP)
    delta = -ADAM_LR * (m_hat / (jnp.sqrt(v_hat) + ADAM_EPS) + ADAM_WD * w)
    return delta, m, v


def _adamw(name, w, g, m, v, rows):
    R, C = w.shape

    def body(w_ref, g_ref, m_ref, v_ref, d_out, m_out, v_out):
        d_out[...], m_out[...], v_out[...] = _adamw_math(w_ref[...], g_ref[...], m_ref[...], v_ref[...])

    spec = pl.BlockSpec((rows, C), lambda i: (i, 0))
    return pl.pallas_call(
        body, name=name, grid=(R // rows,), in_specs=[spec] * 4, out_specs=[spec] * 3,
        out_shape=[jax.ShapeDtypeStruct((R, C), F32)] * 3,
        compiler_params=_params(("parallel",)),
    )(w, g, m, v)


def _adamw_square(g_sq, ws, ms, vs):
    def body(g_ref, *refs):
        w_refs, m_refs, v_refs, outs = refs[0:3], refs[3:6], refs[6:9], refs[9:]
        for k in range(3):
            g = g_ref[k]
            outs[k][0] = g
            outs[3 + k][0], outs[6 + k][0], outs[9 + k][0] = _adamw_math(w_refs[k][0], g, m_refs[k][0], v_refs[k][0])

    out = pl.pallas_call(
        body, name="adamw_square", out_shape=[jax.ShapeDtypeStruct((1, SQ_ROWS, D), F32)] * 12,
        compiler_params=_params(),
    )(g_sq, *ws, *ms, *vs)
    return out[0:3], out[3:6], out[6:9], out[9:12]


def _small_update(total, lbw, w8, m8, v8):
    def body(t_ref, lbw_ref, w_ref, m_ref, v_ref, g_out, d_out, m_out, v_out):
        lb = 1.0 / (1.0 + jnp.exp(lbw_ref[1:2, :] - lbw_ref[0:1, :]))
        dlb = t_ref[2:3, :] * lb * (1.0 - lb)
        g_out[...] = jnp.zeros_like(g_out)
        g_out[0:1, :] = t_ref[3:4, :]
        g_out[1:2, :] = dlb
        g_out[2:3, :] = -dlb
        g_out[3:4, :] = t_ref[1:2, :]
        g_out[4:5, :] = t_ref[0:1, :]
        g_out[5:6, :] = t_ref[4:5, :]
        d_out[...], m_out[...], v_out[...] = _adamw_math(w_ref[...], g_out[...], m_ref[...], v_ref[...])

    return pl.pallas_call(
        body, name="small_update", out_shape=[jax.ShapeDtypeStruct((8, D), F32)] * 4,
        compiler_params=_params(),
    )(total, lbw, w8, m8, v8)


def _pack8(norm_w, lbw, hnw, fnw, sinks):
    pad = jnp.zeros((1, D - 16), F32)
    return jnp.concatenate([norm_w, lbw, hnw, fnw.reshape(1, D), jnp.concatenate([sinks, pad], axis=1),
                            jnp.zeros((2, D), F32)], axis=0)


def _unpack8(a):
    return a[0:1], a[1:3], a[3:4], a[5:6, 0:16], a[4]


def _local_step(order_arr, x, tgt, norm_w, lbw, hnw, sinks, fnw, win_mine, wsq_mine, exchange):
    proj, xnt_bf, win_bf = _fwd_proj(order_arr, x, norm_w, win_mine)
    oh, states, wsq_all = _hgrn_fwd(proj, lbw, wsq_mine)
    wsq_bf = wsq_all.reshape(SHARDS, 3, SQ_ROWS, D).transpose(1, 0, 2, 3).reshape(3, D, D)
    oa = _attn_fwd(proj, sinks)
    dx2, doh, doa, dhg, dtail, lhs, rhs, loss8, vec_mid = _mid(x, tgt, proj, oh, oa, hnw, fnw.reshape(1, D), wsq_bf)
    gsq, gsq_b = _wgrad_square(lhs, rhs)
    dhead, dlb = _hgrn_bwd(proj, lbw, states, doh)
    daq, dakv, dsink = _attn_bwd(proj, sinks, oa, doa)
    pieces = [dhead, dhg, daq, dakv, dtail]
    sums = exchange(*_wgrad_in(xnt_bf, pieces), gsq, gsq_b)
    wt_bf = win_bf.transpose(0, 2, 1).reshape(D_IN, D)
    grad_x, gnw, win_got, sq_got = _bwd_dx(pieces, wt_bf, x, norm_w, dx2, sums[2], sums[3])
    sink_row = jnp.concatenate([dsink[:, 0].reshape(1, 16), jnp.zeros((1, D - 16), F32)], axis=1)
    loss_row = jnp.broadcast_to(loss8[0:1, 0:1], (1, D))
    vec = jnp.concatenate([vec_mid[0:2], dlb, gnw, sink_row, loss_row, jnp.zeros((2, D), F32)], axis=0)
    return grad_x, sums, (win_got, sq_got), vec


def kernel(x, norm_w, w_in, hgrn_lower_bound, hgrn_norm_w, w_branch_hgrn, attn_sinks, w_branch_attn, w_out, final_norm_w, loss_target, m_norm_w, m_w_in, m_hgrn_lower_bound, m_hgrn_norm_w, m_w_branch_hgrn, m_attn_sinks, m_w_branch_attn, m_w_out, m_final_norm_w, v_norm_w, v_w_in, v_hgrn_lower_bound, v_hgrn_norm_w, v_w_branch_hgrn, v_attn_sinks, v_w_branch_attn, v_w_out, v_final_norm_w):
    c_arr = lax.axis_index("c").astype(jnp.int32).reshape(1)
    j_arr = (2 * lax.axis_index("x") + lax.axis_index("y")).astype(jnp.int32).reshape(1)
    jc_arr = jnp.concatenate([j_arr, c_arr])

    win_mine, wsq_mine = _cast_shards(j_arr, w_in[0], w_branch_hgrn[0], w_branch_attn[0], w_out[0])
    xi, yi = lax.axis_index("x"), lax.axis_index("y")
    order_arr = jnp.stack([2 * xi + yi] + [2 * _flip(xi, fx) + _flip(yi, fy) for fx, fy in CHIP_FLIPS]).astype(jnp.int32)

    def chip_sums(gwin, gwin_b, gsq, gsq_b):
        return _add_halves(c_arr, gwin, gsq, *_swap_halves(gwin_b, gsq_b))

    grad_x, (swin, ssq, _, _), arrived, vec = _local_step(
        order_arr, x[0], loss_target[0], norm_w, hgrn_lower_bound, hgrn_norm_w, attn_sinks, final_norm_w, win_mine, wsq_mine,
        chip_sums)
    g_win, g_sq = _join_halves(*_sum_chips(jc_arr, swin, ssq, *arrived))

    d_win, nm_win, nv_win = _adamw("adamw_w_in", w_in[0], g_win, m_w_in[0], v_w_in[0], 128)
    g_sqs, d_sqs, nm_sqs, nv_sqs = _adamw_square(
        g_sq, (w_branch_hgrn, w_branch_attn, w_out), (m_w_branch_hgrn, m_w_branch_attn, m_w_out),
        (v_w_branch_hgrn, v_w_branch_attn, v_w_out))

    total = _allreduce_small(vec)
    loss = total[5, 0]
    g8, d8, nm8, nv8 = _small_update(
        total, hgrn_lower_bound,
        _pack8(norm_w, hgrn_lower_bound, hgrn_norm_w, final_norm_w, attn_sinks),
        _pack8(m_norm_w, m_hgrn_lower_bound, m_hgrn_norm_w, m_final_norm_w, m_attn_sinks),
        _pack8(v_norm_w, v_hgrn_lower_bound, v_hgrn_norm_w, v_final_norm_w, v_attn_sinks))

    def assemble(win, sq, small):
        nw, lb, hn, sk, fn = _unpack8(small)
        return (nw, win.reshape(1, D, SHARD_W), lb, hn, sq[0], sk, sq[1], sq[2], fn)

    return (loss, grad_x.reshape(1, -1, D),
            *assemble(g_win, g_sqs, g8), *assemble(d_win, d_sqs, d8),
            *assemble(nm_win, nm_sqs, nm8), *assemble(nv_win, nv_sqs, nv8))
```

```python
import functools

import jax
import jax.numpy as jnp
from jax import lax
from jax.experimental import pallas as pl
from jax.experimental.pallas import tpu as pltpu

F32 = jnp.float32
BF16 = jnp.bfloat16

D = 1024
D_IN = 8704
SHARDS = 4
SHARD_W = D_IN // SHARDS
SQ_ROWS = D // SHARDS
HEADS = 8
HEAD_W = 128
CHUNK = 64
SUB = 8
ATT_BLOCK = 128
ATT_STEP = 4
KV_HEADS = 4
HEAD_DIM = 64
EPS = 1e-6
NEG = -1e30
SCALE = HEAD_DIM ** -0.5
COL_HG, COL_AQ, COL_AK, COL_AV, COL_AG, COL_MH, COL_MA = 3072, 4096, 5120, 5376, 5632, 6656, 7680

ADAM_LR, ADAM_B1, ADAM_B2, ADAM_EPS, ADAM_WD, ADAM_STEP = 0.001, 0.9, 0.999, 1e-08, 0.01, 10

VMEM_LIMIT = 56 * 1024 * 1024
MESH = pl.DeviceIdType.MESH
HBM_SPEC = pl.BlockSpec(memory_space=pltpu.HBM)
CHIP_FLIPS = ((1, 0), (0, 1), (1, 1))


def _dot(a, b):
    return jnp.dot(a, b, preferred_element_type=F32)


def _dot_nt(a, b):
    return lax.dot_general(a, b, (((1,), (1,)), ((), ())), preferred_element_type=F32)


def _dot_tn(a, b):
    return lax.dot_general(a, b, (((0,), (0,)), ((), ())), preferred_element_type=F32)


def _sigmoid(v):
    return 1.0 / (1.0 + jnp.exp(-v))


def _bf(v):
    return v.astype(BF16)


def _tri_dot2(tri, v):
    a = _bf(v)
    return _dot(tri, a) + _dot(tri, _bf(v - a.astype(F32)))


def _params(sem=None):
    return pltpu.CompilerParams(dimension_semantics=sem, vmem_limit_bytes=VMEM_LIMIT)


def _cast_shards(j_arr, win_s, wbh_s, wba_s, wout_s):
    steps = 4
    rows = D // steps

    def body(j_ref, win_ref, a_ref, b_ref, c_ref, win_o, sq_o):
        win_o[...] = _bf(win_ref[...])

        @pl.when(pl.program_id(0) == 0)
        def _():
            sq_o[0:SQ_ROWS, :] = _bf(a_ref[...])
            sq_o[SQ_ROWS:2 * SQ_ROWS, :] = _bf(b_ref[...])
            sq_o[2 * SQ_ROWS:3 * SQ_ROWS, :] = _bf(c_ref[...])

    whole = pl.BlockSpec((SQ_ROWS, D), lambda i, j: (0, 0))
    return pl.pallas_call(
        body, name="cast_shards",
        grid_spec=pltpu.PrefetchScalarGridSpec(
            num_scalar_prefetch=1, grid=(steps,),
            in_specs=[pl.BlockSpec((rows, SHARD_W), lambda i, j: (i, 0)), whole, whole, whole],
            out_specs=[pl.BlockSpec((None, rows, SHARD_W), lambda i, j: (j[0], i, 0)),
                       pl.BlockSpec((None, 3 * SQ_ROWS, D), lambda i, j: (j[0], 0, 0))]),
        out_shape=[jax.ShapeDtypeStruct((SHARDS, D, SHARD_W), BF16), jax.ShapeDtypeStruct((SHARDS, 3 * SQ_ROWS, D), BF16)],
        compiler_params=_params(("arbitrary",)),
    )(j_arr, win_s, wbh_s, wba_s, wout_s)


def _fwd_proj(order_arr, x, norm_w, win_all):
    T = x.shape[0]
    tm = min(512, T)
    nt = T // tm

    def body(order_ref, x_ref, nw_ref, win_in, proj_ref, xn_ref, win_out, w_scr, xn_scr, sems, send_sems, recv_sems):
        del win_in
        p, i = pl.program_id(0), pl.program_id(1)

        def load(n):
            return pltpu.make_async_copy(win_out.at[order_ref[n]], w_scr.at[n % 2], sems.at[n % 2])

        @pl.when((p == 0) & (i == 0))
        def _():
            _gather_start(win_out, _win_half, send_sems, recv_sems)
            load(0).start()
            load(0).wait()

        @pl.when((p == 1) & (i == 0))
        def _():
            _gather_land(win_out, _win_half, 0, send_sems, recv_sems)
            load(1).start()
            load(1).wait()

        for k in range(1, SHARDS - 1):
            @pl.when((p == k) & (i == nt // 2))
            def _():
                _gather_land(win_out, _win_half, k, send_sems, recv_sems)
                load(k + 1).start()

            @pl.when((p == k + 1) & (i == 0))
            def _():
                load(k + 1).wait()

        @pl.when(p == 0)
        def _():
            xf = x_ref[...]
            rs = lax.rsqrt(jnp.mean(xf * xf, axis=1, keepdims=True) + EPS)
            xn = _bf((xf * rs) * nw_ref[...])
            xn_scr[i] = xn
            xn_ref[...] = xn.T

        proj_ref[...] = _dot(xn_scr[i], w_scr[p % 2])

        @pl.when((p == SHARDS - 1) & (i == nt - 1))
        def _():
            _gather_drain(win_out, _win_half, send_sems, recv_sems)

    first = lambda p, i: jnp.where(p == 0, i, nt - 1)
    return pl.pallas_call(
        body, name="fwd_proj",
        grid_spec=pltpu.PrefetchScalarGridSpec(
            num_scalar_prefetch=1, grid=(SHARDS, nt),
            in_specs=[pl.BlockSpec((tm, D), lambda p, i, order: (first(p, i), 0)),
                      pl.BlockSpec((1, D), lambda p, i, order: (0, 0)), HBM_SPEC],
            out_specs=[pl.BlockSpec((tm, SHARD_W), lambda p, i, order: (i, order[p])),
                       pl.BlockSpec((D, tm), lambda p, i, order: (0, first(p, i))),
                       HBM_SPEC],
            scratch_shapes=[pltpu.VMEM((2, D, SHARD_W), BF16), pltpu.VMEM((nt, tm, D), BF16), pltpu.SemaphoreType.DMA((2,)),
                            pltpu.SemaphoreType.DMA((6,)), pltpu.SemaphoreType.DMA((6,))]),
        out_shape=[jax.ShapeDtypeStruct((T, D_IN), F32), jax.ShapeDtypeStruct((D, T), BF16),
                   jax.ShapeDtypeStruct((SHARDS, D, SHARD_W), BF16)],
        input_output_aliases={3: 2},
        compiler_params=_params(("arbitrary", "arbitrary")),
    )(order_arr, x, norm_w, win_all)


def _hgrn_gates(hq_ref, hf_ref, lbw_ref, b_scr):
    lb = 1.0 / (1.0 + jnp.exp(lbw_ref[1:2, :] - lbw_ref[0:1, :]))
    hf = hf_ref[...]
    sig = _sigmoid(hf)
    f = lb + (1.0 - lb) * sig
    g = jnp.log(f)
    hq = hq_ref[...]
    sq = _sigmoid(hq)
    q = hq * sq
    row = lax.broadcasted_iota(jnp.int32, (CHUNK, CHUNK), 0)
    col = lax.broadcasted_iota(jnp.int32, (CHUNK, CHUNK), 1)
    causal = row >= col
    b = _tri_dot2(jnp.where(causal, 1.0, 0.0).astype(BF16), g)
    b_scr[...] = b
    bc = b_scr[CHUNK - 1:CHUNK, :]
    r = b_scr[CHUNK // 2 - 1:CHUNK // 2, :]
    return dict(lb=lb, sig=sig, f=f, k=1.0 - f, hq=hq, sq=sq, q=q, b=b, bc=bc, r=r, causal=causal)


def _hgrn_fwd(proj, lbw, wsq_all):
    T = proj.shape[0]
    n = T // CHUNK

    def body(hq_ref, hf_ref, hi_ref, lbw_ref, wsq_in, o_ref, st_ref, wsq_out, s_scr, b_scr, send_sems, recv_sems):
        del wsq_in

        @pl.when(pl.program_id(0) == 0)
        def _():
            _gather_start(wsq_out, _sq_half, send_sems, recv_sems)
            s_scr[...] = jnp.zeros_like(s_scr)

        for c in range(SUB):
            rows = pl.ds(c * CHUNK, CHUNK)
            gt = _hgrn_gates(hq_ref.at[rows, :], hf_ref.at[rows, :], lbw_ref, b_scr.at[rows, :])
            b, bc, r, q, k = gt["b"], gt["bc"], gt["r"], gt["q"], gt["k"]
            qe = _bf(q * jnp.exp(b))
            qr = _bf(q * jnp.exp(b - r))
            kr = _bf(k * jnp.exp(r - b))
            kl = _bf(k * jnp.exp(bc - b))
            ebc = jnp.exp(bc)
            v = _bf(hi_ref[rows, :])
            scores = [_bf(jnp.where(gt["causal"], _dot_nt(qr[:, h * HEAD_W:(h + 1) * HEAD_W], kr[:, h * HEAD_W:(h + 1) * HEAD_W]), 0.0))
                      for h in range(HEADS)]
            for h in range(HEADS):
                sl = slice(h * HEAD_W, (h + 1) * HEAD_W)
                st = s_scr[h]
                st_ref[c, h] = st
                o_ref[rows, sl] = _dot(scores[h], v[:, sl]) + _dot_nt(qe[:, sl], _bf(st))
                s_scr[h] = ebc[:, sl] * st + _dot_tn(v[:, sl], kl[:, sl])

        @pl.when(pl.program_id(0) == n // SUB - 1)
        def _():
            _gather_finish(wsq_out, _sq_half, send_sems, recv_sems)

    col = lambda j: pl.BlockSpec((SUB * CHUNK, D), lambda i: (i, j))
    return pl.pallas_call(
        body, name="hgrn_fwd", grid=(n // SUB,),
        in_specs=[col(0), col(1), col(2), pl.BlockSpec((2, D), lambda i: (0, 0)), HBM_SPEC],
        out_specs=[pl.BlockSpec((SUB * CHUNK, D), lambda i: (i, 0)),
                   pl.BlockSpec((SUB, HEADS, HEAD_W, HEAD_W), lambda i: (i, 0, 0, 0)), HBM_SPEC],
        out_shape=[jax.ShapeDtypeStruct((T, D), F32), jax.ShapeDtypeStruct((n, HEADS, HEAD_W, HEAD_W), F32),
                   jax.ShapeDtypeStruct((SHARDS, 3 * SQ_ROWS, D), BF16)],
        input_output_aliases={4: 2},
        scratch_shapes=[pltpu.VMEM((HEADS, HEAD_W, HEAD_W), F32), pltpu.VMEM((SUB * CHUNK, D), F32),
                        pltpu.SemaphoreType.DMA((6,)), pltpu.SemaphoreType.DMA((6,))],
        compiler_params=_params(("arbitrary",)),
    )(proj, proj, proj, lbw, wsq_all)


def _hgrn_bwd(proj, lbw, states, do):
    T = proj.shape[0]
    n = T // CHUNK

    def body(hq_ref, hf_ref, hi_ref, lbw_ref, st_ref, do_ref, dp_ref, dlb_ref,
             ds_scr, b_scr, dq_scr, dk_scr, dv_scr, late_scr, early_scr, ex_scr):
        @pl.when(pl.program_id(0) == 0)
        def _():
            ds_scr[...] = jnp.zeros_like(ds_scr)
            dlb_ref[...] = jnp.zeros_like(dlb_ref)

        for c in reversed(range(SUB)):
            rows = pl.ds(c * CHUNK, CHUNK)
            gt = _hgrn_gates(hq_ref.at[rows, :], hf_ref.at[rows, :], lbw_ref, b_scr.at[rows, :])
            b, bc, r, q, k = gt["b"], gt["bc"], gt["r"], gt["q"], gt["k"]
            eb = jnp.exp(b)
            er = jnp.exp(b - r)
            erk = jnp.exp(r - b)
            el = jnp.exp(bc - b)
            ebc = jnp.exp(bc)
            qe, qr, kr, kl = _bf(q * eb), _bf(q * er), _bf(k * erk), _bf(k * el)
            v = _bf(hi_ref[rows, :])
            do_b = do_ref[rows, :]
            do_t = do_b.T
            causal_t = lax.broadcasted_iota(jnp.int32, (CHUNK, CHUNK), 0) <= lax.broadcasted_iota(jnp.int32, (CHUNK, CHUNK), 1)
            firsts = []
            for h in range(HEADS):
                sl = slice(h * HEAD_W, (h + 1) * HEAD_W)
                firsts.append((_bf(jnp.where(causal_t, _dot_nt(kr[:, sl], qr[:, sl]), 0.0)),
                               _bf(jnp.where(gt["causal"], _dot_nt(do_b[:, sl], v[:, sl]), 0.0)),
                               _bf(jnp.where(causal_t, _dot_nt(v[:, sl], do_b[:, sl]), 0.0))))
            for h in range(HEADS):
                sl = slice(h * HEAD_W, (h + 1) * HEAD_W)
                st0 = st_ref[c, h]
                dst = ds_scr[h]
                dst_b = _bf(dst)
                a_t, da, da_t = firsts[h]
                mq = _dot(da, kr[:, sl])
                mk = _dot(da_t, qr[:, sl])
                dq_in = eb[:, sl] * _dot(do_b[:, sl], _bf(st0))
                dk_in = el[:, sl] * _dot(v[:, sl], dst_b)
                dq_scr[rows, sl] = er[:, sl] * mq + dq_in
                dk_scr[rows, sl] = erk[:, sl] * mk + dk_in
                dv_scr[rows, sl] = _dot(a_t, do_b[:, sl]) + _dot_nt(kl[:, sl], dst_b)
                late_scr[rows, sl] = q[:, sl] * dq_in + qr[:, sl].astype(F32) * mq - kr[:, sl].astype(F32) * mk
                early_scr[rows, sl] = k[:, sl] * dk_in
                ex_scr[:, sl] = jnp.sum(dst * st0, axis=0, keepdims=True)
                ds_scr[h] = ebc[:, sl] * dst + _dot(do_t[sl, :], qe[:, sl])

            dq, dk = dq_scr[rows, :], dk_scr[rows, :]
            row = lax.broadcasted_iota(jnp.int32, (CHUNK, CHUNK), 0)
            col = lax.broadcasted_iota(jnp.int32, (CHUNK, CHUNK), 1)
            at_or_after = jnp.where(col >= row, 1.0, 0.0).astype(BF16)
            before = jnp.where(col < row, 1.0, 0.0).astype(BF16)
            dg = _tri_dot2(jnp.concatenate([at_or_after, before], axis=1),
                           jnp.concatenate([late_scr[rows, :], early_scr[rows, :]], axis=0)) + ebc * ex_scr[...]
            df = dg / gt["f"] - dk
            sig, sq, hq, lb = gt["sig"], gt["sq"], gt["hq"], gt["lb"]
            dp_ref[rows, 0:D] = _bf(dq * (sq * (1.0 + hq * (1.0 - sq))))
            dp_ref[rows, D:2 * D] = _bf(df * (1.0 - lb) * sig * (1.0 - sig))
            dp_ref[rows, 2 * D:3 * D] = _bf(dv_scr[rows, :])
            dlb_ref[...] += jnp.sum(df * (1.0 - sig), axis=0, keepdims=True)

    ns = n // SUB
    col = lambda j: pl.BlockSpec((SUB * CHUNK, D), lambda i: (ns - 1 - i, j))
    return pl.pallas_call(
        body, name="hgrn_bwd", grid=(ns,),
        in_specs=[col(0), col(1), col(2), pl.BlockSpec((2, D), lambda i: (0, 0)),
                  pl.BlockSpec((SUB, HEADS, HEAD_W, HEAD_W), lambda i: (ns - 1 - i, 0, 0, 0)),
                  pl.BlockSpec((SUB * CHUNK, D), lambda i: (ns - 1 - i, 0))],
        out_specs=[pl.BlockSpec((SUB * CHUNK, 3 * D), lambda i: (ns - 1 - i, 0)),
                   pl.BlockSpec((1, D), lambda i: (0, 0))],
        out_shape=[jax.ShapeDtypeStruct((T, 3 * D), BF16), jax.ShapeDtypeStruct((1, D), F32)],
        scratch_shapes=[pltpu.VMEM((HEADS, HEAD_W, HEAD_W), F32)] + [pltpu.VMEM((SUB * CHUNK, D), F32)] * 6
                       + [pltpu.VMEM((1, D), F32)],
        compiler_params=_params(("arbitrary",)),
    )(proj, proj, proj, lbw, states, do)


def _attn_masks(blk):
    qi = lax.broadcasted_iota(jnp.int32, (ATT_BLOCK, 2 * ATT_BLOCK), 0)
    kj = lax.broadcasted_iota(jnp.int32, (ATT_BLOCK, 2 * ATT_BLOCK), 1)
    band = (kj > qi) & (kj <= qi + ATT_BLOCK)
    return band & ((blk > 0) | (kj >= ATT_BLOCK))


def _head_pair_operand(t, hp, low):
    mine = low if hp == 0 else jnp.logical_not(low)
    both = jnp.where(mine, t, pltpu.roll(t, HEAD_DIM, 1))
    return _bf(jnp.concatenate([jnp.where(low, both, 0.0), jnp.where(low, 0.0, both)], axis=0))


def _attn_probs(s, sink, valid):
    s = jnp.where(valid, s, NEG)
    m = jnp.maximum(jnp.max(s, axis=1, keepdims=True), sink)
    p = jnp.exp(s - m)
    es = jnp.exp(sink - m)
    inv = 1.0 / (jnp.sum(p, axis=1, keepdims=True) + es)
    return p * inv, es * inv


def _attn_fwd(proj, sinks):
    T = proj.shape[0]
    rows_step = ATT_STEP * ATT_BLOCK

    def body(sink_ref, q_ref, kp_ref, kc_ref, vp_ref, vc_ref, o_ref):
        low = lax.broadcasted_iota(jnp.int32, (1, 2 * HEAD_DIM), 1) < HEAD_DIM
        for sb in range(ATT_STEP):
            rows = slice(sb * ATT_BLOCK, (sb + 1) * ATT_BLOCK)
            before = slice((sb - 1) * ATT_BLOCK, sb * ATT_BLOCK)
            valid = _attn_masks(pl.program_id(0) * ATT_STEP + sb)
            kcat = jnp.concatenate([kp_ref[...] if sb == 0 else kc_ref[before, :], kc_ref[rows, :]], axis=0)
            vcat = jnp.concatenate([vp_ref[...] if sb == 0 else vc_ref[before, :], vc_ref[rows, :]], axis=0)
            for h in range(KV_HEADS):
                tl = slice((h // 2) * 128, (h // 2) * 128 + 128)
                mine = low if h % 2 == 0 else jnp.logical_not(low)
                kh = _bf(jnp.where(mine, kcat[:, tl], pltpu.roll(kcat[:, tl], HEAD_DIM, 1)))
                vh = _bf(jnp.where(mine, vcat[:, tl], pltpu.roll(vcat[:, tl], HEAD_DIM, 1)))
                for t in range(2):
                    ql = slice((2 * h + t) * 128, (2 * h + t) * 128 + 128)
                    q2 = q_ref[rows, ql] * SCALE
                    outs = []
                    for p in range(2):
                        qm = _bf(jnp.where(low if p == 0 else jnp.logical_not(low), q2, 0.0))
                        probs, _ = _attn_probs(_dot_nt(qm, kh), sink_ref[0, 4 * h + 2 * t + p], valid)
                        outs.append(_dot(_bf(probs), vh))
                    o_ref[rows, ql] = jnp.where(low, outs[0], outs[1])

    prev = lambda i: jnp.maximum(ATT_STEP * i - 1, 0)
    return pl.pallas_call(
        body, name="attn_fwd", grid=(T // rows_step,),
        in_specs=[pl.BlockSpec(memory_space=pltpu.SMEM),
                  pl.BlockSpec((rows_step, D), lambda i: (i, COL_AQ // D)),
                  pl.BlockSpec((ATT_BLOCK, 256), lambda i: (prev(i), COL_AK // 256)),
                  pl.BlockSpec((rows_step, 256), lambda i: (i, COL_AK // 256)),
                  pl.BlockSpec((ATT_BLOCK, 256), lambda i: (prev(i), COL_AV // 256)),
                  pl.BlockSpec((rows_step, 256), lambda i: (i, COL_AV // 256))],
        out_specs=pl.BlockSpec((rows_step, D), lambda i: (i, 0)),
        out_shape=jax.ShapeDtypeStruct((T, D), F32),
        compiler_params=_params(("arbitrary",)),
    )(sinks, proj, proj, proj, proj, proj)


def _attn_bwd(proj, sinks, o, do):
    T = proj.shape[0]
    nb = T // ATT_BLOCK
    W2 = 2 * ATT_BLOCK

    def body(sink_ref, q_ref, kp_ref, kc_ref, vp_ref, vc_ref, o_ref, do_ref,
             dq_ref, dkv_ref, dsink_ref, ck_scr, cv_scr, nk_scr, nv_scr):
        blk = pl.program_id(0)

        @pl.when(blk == 0)
        def _():
            ck_scr[...] = jnp.zeros_like(ck_scr)
            cv_scr[...] = jnp.zeros_like(cv_scr)
            dsink_ref[...] = jnp.zeros_like(dsink_ref)

        @pl.when(blk < nb)
        def _():
            valid = _attn_masks(blk)
            low = lax.broadcasted_iota(jnp.int32, (1, 2 * HEAD_DIM), 1) < HEAD_DIM
            kcat = jnp.concatenate([kp_ref[...], kc_ref[...]], axis=0)
            vcat = jnp.concatenate([vp_ref[...], vc_ref[...]], axis=0)
            for h in range(KV_HEADS):
                tl = slice((h // 2) * 128, (h // 2) * 128 + 128)
                kbd = _head_pair_operand(kcat[:, tl], h % 2, low)
                vbd = _head_pair_operand(vcat[:, tl], h % 2, low)
                dkbd = jnp.zeros((2 * W2, 128), F32)
                dvbd = jnp.zeros((2 * W2, 128), F32)
                tiles = []
                for t in range(2):
                    ql = slice((2 * h + t) * 128, (2 * h + t) * 128 + 128)
                    q2 = _bf(q_ref[:, ql] * SCALE)
                    do2_b = do_ref[:, ql]
                    doo = do2_b.astype(F32) * o_ref[:, ql]
                    dsum0 = jnp.sum(jnp.where(low, doo, 0.0), axis=1, keepdims=True)
                    dsum1 = jnp.sum(jnp.where(low, 0.0, doo), axis=1, keepdims=True)
                    tiles.append((ql, q2, do2_b, dsum0, dsum1, _dot_nt(q2, kbd), _dot_nt(do2_b, vbd)))
                grads = []
                for t, (ql, q2, do2_b, dsum0, dsum1, s2, dp2) in enumerate(tiles):
                    head = 4 * h + 2 * t
                    p0, ps0 = _attn_probs(s2[:, 0:W2], sink_ref[0, head], valid)
                    p1, ps1 = _attn_probs(s2[:, W2:2 * W2], sink_ref[0, head + 1], valid)
                    ds2 = _bf(jnp.concatenate([p0 * (dp2[:, 0:W2] - dsum0), p1 * (dp2[:, W2:2 * W2] - dsum1)], axis=1))
                    grads.append((ds2, _bf(jnp.concatenate([p0, p1], axis=1))))
                    dsink_ref[head:head + 1, :] += jnp.zeros((1, 128), F32) - jnp.sum(ps0 * dsum0, axis=0, keepdims=True)
                    dsink_ref[head + 1:head + 2, :] += jnp.zeros((1, 128), F32) - jnp.sum(ps1 * dsum1, axis=0, keepdims=True)
                for (ql, q2, do2_b, _, _, _, _), (ds2, p2) in zip(tiles, grads):
                    dq_ref[:, ql] = _bf(_dot(ds2, kbd) * SCALE)
                    dkbd = dkbd + _dot_tn(ds2, q2)
                    dvbd = dvbd + _dot_tn(p2, do2_b)
                dk2 = jnp.where(low, dkbd[0:W2], dkbd[W2:2 * W2])
                dv2 = jnp.where(low, dvbd[0:W2], dvbd[W2:2 * W2])
                dk2 = dk2 + pltpu.roll(dk2, HEAD_DIM, 1)
                dv2 = dv2 + pltpu.roll(dv2, HEAD_DIM, 1)
                if h % 2 == 0:
                    keep_k, keep_v = dk2, dv2
                else:
                    nk_scr[:, tl] = jnp.where(low, keep_k, dk2)
                    nv_scr[:, tl] = jnp.where(low, keep_v, dv2)
            dkv_ref[:, 0:256] = _bf(ck_scr[...] + nk_scr[0:ATT_BLOCK, :])
            dkv_ref[:, 256:512] = _bf(cv_scr[...] + nv_scr[0:ATT_BLOCK, :])
            ck_scr[...] = nk_scr[ATT_BLOCK:2 * ATT_BLOCK, :]
            cv_scr[...] = nv_scr[ATT_BLOCK:2 * ATT_BLOCK, :]

        @pl.when(blk == nb)
        def _():
            dkv_ref[:, 0:256] = _bf(ck_scr[...])
            dkv_ref[:, 256:512] = _bf(cv_scr[...])

    cur = lambda i: jnp.minimum(i, nb - 1)
    prev = lambda i: jnp.maximum(cur(i) - 1, 0)
    late = lambda i: jnp.maximum(i - 1, 0)
    return pl.pallas_call(
        body, name="attn_bwd", grid=(nb + 1,),
        in_specs=[pl.BlockSpec(memory_space=pltpu.SMEM),
                  pl.BlockSpec((ATT_BLOCK, D), lambda i: (cur(i), COL_AQ // D)),
                  pl.BlockSpec((ATT_BLOCK, 256), lambda i: (prev(i), COL_AK // 256)),
                  pl.BlockSpec((ATT_BLOCK, 256), lambda i: (cur(i), COL_AK // 256)),
                  pl.BlockSpec((ATT_BLOCK, 256), lambda i: (prev(i), COL_AV // 256)),
                  pl.BlockSpec((ATT_BLOCK, 256), lambda i: (cur(i), COL_AV // 256)),
                  pl.BlockSpec((ATT_BLOCK, D), lambda i: (cur(i), 0)),
                  pl.BlockSpec((ATT_BLOCK, D), lambda i: (cur(i), 0))],
        out_specs=[pl.BlockSpec((ATT_BLOCK, D), lambda i: (cur(i), 0)),
                   pl.BlockSpec((ATT_BLOCK, 512), lambda i: (late(i), 0)),
                   pl.BlockSpec((16, 128), lambda i: (0, 0))],
        out_shape=[jax.ShapeDtypeStruct((T, D), BF16), jax.ShapeDtypeStruct((T, 512), BF16),
                   jax.ShapeDtypeStruct((16, 128), F32)],
        scratch_shapes=[pltpu.VMEM((ATT_BLOCK, 256), F32), pltpu.VMEM((ATT_BLOCK, 256), F32),
                        pltpu.VMEM((2 * ATT_BLOCK, 256), F32), pltpu.VMEM((2 * ATT_BLOCK, 256), F32)],
        compiler_params=_params(("arbitrary",)),
    )(sinks, proj, proj, proj, proj, proj, o, do)


def _mid(x, tgt, proj, oh, oa, hnw, fnw, wsq_bf):
    T = x.shape[0]
    tm = min(256, T)
    nt = T // tm

    def body(x_ref, tgt_ref, oh_ref, oa_ref, hg_ref, ag0_ref, ag1_ref, mh0_ref, mh1_ref, ma0_ref, ma1_ref,
             hnw_ref, fnw_ref, w_hbm,
             dx2_ref, doh_ref, doa_ref, dhg_ref, dtail_ref, lhs_ref, rhs_ref, loss_ref, vec_ref,
             w_scr, xh_scr, rs_scr, sem):
        @pl.when(pl.program_id(0) == 0)
        def _():
            cp = pltpu.make_async_copy(w_hbm, w_scr, sem)
            cp.start()
            cp.wait()
            loss_ref[...] = jnp.zeros_like(loss_ref)
            vec_ref[...] = jnp.zeros_like(vec_ref)

        oh = oh_ref[...]
        for h in range(HEADS):
            sl = slice(h * HEAD_W, (h + 1) * HEAD_W)
            ohh = oh[:, sl]
            rs = lax.rsqrt(jnp.mean(ohh * ohh, axis=1, keepdims=True) + EPS)
            xh_scr[:, sl] = ohh * rs
            rs_scr[:, sl] = jnp.broadcast_to(rs, (tm, HEAD_W))
        xh = xh_scr[...]
        hnw = hnw_ref[...]
        on = xh * hnw
        hg = hg_ref[...]
        sg = _sigmoid(hg)
        silu_g = hg * sg
        gated_h = _bf(on * silu_g)
        oa = oa_ref[...]
        ag = jnp.concatenate([ag0_ref[...], ag1_ref[...]], axis=1)
        sa = _sigmoid(ag)
        silu_a = ag * sa
        gated_a = _bf(oa * silu_a)
        yh = _dot(gated_h, w_scr[0])
        ya = _dot(gated_a, w_scr[1])
        lhs_ref[0] = gated_h.T
        lhs_ref[1] = gated_a.T
        smh = _sigmoid(jnp.concatenate([mh0_ref[...], mh1_ref[...]], axis=1))
        sma = _sigmoid(jnp.concatenate([ma0_ref[...], ma1_ref[...]], axis=1))
        merged = _bf(smh * yh + sma * ya)
        lhs_ref[2] = merged.T
        x2 = x_ref[...] + _dot(merged, w_scr[2])
        rs2 = lax.rsqrt(jnp.mean(x2 * x2, axis=1, keepdims=True) + EPS)
        xh2 = x2 * rs2
        fnw = fnw_ref[...]
        diff = xh2 * fnw - tgt_ref[...]
        loss_ref[...] += jnp.zeros_like(loss_ref) + jnp.sum(diff * diff) * (0.5 / D)

        dy = diff * (1.0 / D)
        vec_ref[0:1, :] += jnp.sum(dy * xh2, axis=0, keepdims=True)
        gy = dy * fnw
        dx2 = rs2 * (gy - xh2 * jnp.mean(gy * xh2, axis=1, keepdims=True))
        dx2_ref[...] = dx2
        dx2_b = _bf(dx2)
        rhs_ref[2] = dx2_b
        dmerged = _dot_nt(dx2_b, w_scr[2])
        dyh = dmerged * smh
        dya = dmerged * sma
        dtail_ref[:, D:2 * D] = _bf(dyh * yh * (1.0 - smh))
        dtail_ref[:, 2 * D:3 * D] = _bf(dya * ya * (1.0 - sma))
        dyh_b, dya_b = _bf(dyh), _bf(dya)
        rhs_ref[0] = dyh_b
        rhs_ref[1] = dya_b
        dgh = _dot_nt(dyh_b, w_scr[0])
        dga = _dot_nt(dya_b, w_scr[1])
        don = dgh * silu_g
        dhg_ref[...] = _bf(dgh * on * (sg * (1.0 + hg * (1.0 - sg))))
        vec_ref[1:2, :] += jnp.sum(don * xh, axis=0, keepdims=True)
        gxh = don * hnw
        rsb = rs_scr[...]
        for h in range(HEADS):
            sl = slice(h * HEAD_W, (h + 1) * HEAD_W)
            gh, xhh = gxh[:, sl], xh[:, sl]
            doh_ref[:, sl] = _bf(rsb[:, sl] * (gh - xhh * jnp.mean(gh * xhh, axis=1, keepdims=True)))
        doa_ref[...] = _bf(dga * silu_a)
        dtail_ref[:, 0:D] = _bf(dga * oa * (sa * (1.0 + ag * (1.0 - sa))))

    row = lambda w, j: pl.BlockSpec((tm, w), lambda i: (i, j))
    const = lambda r, c: pl.BlockSpec((r, c), lambda i: (0, 0))
    stack = pl.BlockSpec((3, tm, D), lambda i: (0, i, 0))
    stack_t = pl.BlockSpec((3, D, tm), lambda i: (0, 0, i))
    return pl.pallas_call(
        body, name="mid", grid=(nt,),
        in_specs=[row(D, 0), row(D, 0), row(D, 0), row(D, 0), row(D, COL_HG // D),
                  row(512, COL_AG // 512), row(512, COL_AG // 512 + 1),
                  row(512, COL_MH // 512), row(512, COL_MH // 512 + 1),
                  row(512, COL_MA // 512), row(512, COL_MA // 512 + 1),
                  const(1, D), const(1, D), HBM_SPEC],
        out_specs=[row(D, 0), row(D, 0), row(D, 0), row(D, 0), row(3 * D, 0), stack_t, stack, const(8, 128), const(8, D)],
        out_shape=[jax.ShapeDtypeStruct((T, D), F32), jax.ShapeDtypeStruct((T, D), BF16), jax.ShapeDtypeStruct((T, D), BF16),
                   jax.ShapeDtypeStruct((T, D), BF16), jax.ShapeDtypeStruct((T, 3 * D), BF16),
                   jax.ShapeDtypeStruct((3, D, T), BF16), jax.ShapeDtypeStruct((3, T, D), BF16),
                   jax.ShapeDtypeStruct((8, 128), F32), jax.ShapeDtypeStruct((8, D), F32)],
        scratch_shapes=[pltpu.VMEM((3, D, D), BF16), pltpu.VMEM((tm, D), F32), pltpu.VMEM((tm, D), F32),
                        pltpu.SemaphoreType.DMA],
        compiler_params=_params(("arbitrary",)),
    )(x, tgt, oh, oa, proj, proj, proj, proj, proj, proj, proj, hnw, fnw, wsq_bf)


def _wgrad_square(lhs_t, rhs):
    T = rhs.shape[1]
    tk = min(2048, T)
    steps = T // tk

    def body(a_ref, b_ref, g_ref, gb_ref):
        part = _dot(a_ref[...], b_ref[...])

        @pl.when(pl.program_id(1) == 0)
        def _():
            g_ref[...] = part

        @pl.when(pl.program_id(1) > 0)
        def _():
            g_ref[...] += part

        @pl.when(pl.program_id(1) == steps - 1)
        def _():
            gb_ref[...] = _bf(g_ref[...])

    return pl.pallas_call(
        body, name="wgrad_square", grid=(3, steps),
        in_specs=[pl.BlockSpec((None, D, tk), lambda k, i: (k, 0, i)), pl.BlockSpec((None, tk, D), lambda k, i: (k, i, 0))],
        out_specs=[pl.BlockSpec((None, D, D), lambda k, i: (k, 0, 0))] * 2,
        out_shape=[jax.ShapeDtypeStruct((3, D, D), F32), jax.ShapeDtypeStruct((3, D, D), BF16)],
        compiler_params=_params(("parallel", "arbitrary")),
    )(lhs_t, rhs)


def _bwd_dx(pieces, wt_bf, x, norm_w, dx2, swin_b, ssq_b):
    T = x.shape[0]
    tm = min(512, T)
    nt = T // tm
    widths = [p.shape[1] for p in pieces]
    n_p = len(pieces)

    def body(*refs):
        piece_refs = refs[:n_p]
        (w_hbm, x_ref, nw_ref, dx2_ref, swin_ref, ssq_ref,
         gx_ref, gnw_ref, win_got, sq_got, w_scr, sem, send_sems, recv_sems) = refs[n_p:]

        def scatter_copies():
            x_, y_, c_ = _place()
            copies = []
            for k, (fx, fy) in enumerate(CHIP_FLIPS):
                px, py = _flip(x_, fx), _flip(y_, fy)
                jr = 2 * px + py
                for a, (src, dst) in enumerate(((swin_ref.at[:, pl.ds(jr * SHARD_W, SHARD_W)], win_got.at[k]),
                                                (ssq_ref.at[:, pl.ds(jr * SQ_ROWS, SQ_ROWS), :], sq_got.at[k]))):
                    copies.append(pltpu.make_async_remote_copy(
                        src_ref=src, dst_ref=dst, send_sem=send_sems.at[2 * k + a], recv_sem=recv_sems.at[2 * k + a],
                        device_id=(px, py, c_), device_id_type=MESH))
            return copies

        @pl.when(pl.program_id(0) == 0)
        def _():
            for cp in scatter_copies():
                cp.start()
            cp = pltpu.make_async_copy(w_hbm, w_scr, sem)
            cp.start()
            cp.wait()
            gnw_ref[...] = jnp.zeros_like(gnw_ref)

        dxn = None
        off = 0
        for ref, w in zip(piece_refs, widths):
            part = _dot(ref[...], w_scr[off:off + w, :])
            dxn = part if dxn is None else dxn + part
            off += w
        xf = x_ref[...]
        rs = lax.rsqrt(jnp.mean(xf * xf, axis=1, keepdims=True) + EPS)
        xh = xf * rs
        gnw_ref[...] += jnp.sum(dxn * xh, axis=0, keepdims=True)
        gx = dxn * nw_ref[...]
        gx_ref[...] = rs * (gx - xh * jnp.mean(gx * xh, axis=1, keepdims=True)) + dx2_ref[...]

        @pl.when(pl.program_id(0) == nt - 1)
        def _():
            for cp in scatter_copies():
                cp.wait()

    row = lambda w: pl.BlockSpec((tm, w), lambda i: (i, 0))
    return pl.pallas_call(
        body, name="bwd_dx", grid=(nt,),
        in_specs=[row(w) for w in widths] + [HBM_SPEC, row(D), pl.BlockSpec((1, D), lambda i: (0, 0)), row(D), HBM_SPEC, HBM_SPEC],
        out_specs=[row(D), pl.BlockSpec((1, D), lambda i: (0, 0)), HBM_SPEC, HBM_SPEC],
        out_shape=[jax.ShapeDtypeStruct((T, D), F32), jax.ShapeDtypeStruct((1, D), F32),
                   jax.ShapeDtypeStruct((3, D // 2, SHARD_W), BF16), jax.ShapeDtypeStruct((3, 3, SQ_ROWS, D // 2), BF16)],
        scratch_shapes=[pltpu.VMEM((D_IN, D), BF16), pltpu.SemaphoreType.DMA,
                        pltpu.SemaphoreType.DMA((6,)), pltpu.SemaphoreType.DMA((6,))],
        compiler_params=_params(("arbitrary",)),
    )(*pieces, wt_bf, x, norm_w, dx2, swin_b, ssq_b)


W_PIECES = ((0, 1024, 3), (COL_HG, 1024, 1), (COL_AQ, 1024, 1), (COL_AK, 512, 1), (COL_AG, 512, 6))


def _wgrad_in(xnt_bf, pieces):
    T = xnt_bf.shape[1]
    bufs = ()
    for n, (piece, (col, wb, blocks)) in enumerate(zip(pieces, W_PIECES)):
        tk = min(2048 if wb == 1024 else 4096, T)
        steps = T // tk

        def body(xnt_ref, p_ref, *rest):
            g_ref, gb_ref = rest[-2:]
            part = _dot(xnt_ref[...], p_ref[...])

            @pl.when(pl.program_id(1) == 0)
            def _():
                g_ref[...] = part

            @pl.when(pl.program_id(1) > 0)
            def _():
                g_ref[...] += part

            @pl.when(pl.program_id(1) == steps - 1)
            def _():
                gb_ref[...] = _bf(g_ref[...])

        out = pl.BlockSpec((D, wb), lambda jb, i, base=col // wb: (0, base + jb))
        bufs = pl.pallas_call(
            body, name=f"wgrad_in_{n}", grid=(blocks, steps),
            in_specs=[pl.BlockSpec((D, tk), lambda jb, i: (0, i)), pl.BlockSpec((tk, wb), lambda jb, i: (i, jb))]
                     + [HBM_SPEC] * len(bufs),
            out_specs=[out, out],
            out_shape=[jax.ShapeDtypeStruct((D, D_IN), F32), jax.ShapeDtypeStruct((D, D_IN), BF16)],
            input_output_aliases={2: 0, 3: 1} if bufs else {},
            compiler_params=_params(("parallel", "arbitrary")),
        )(xnt_bf, piece, *bufs)
    return bufs


def _place():
    return lax.axis_index("x"), lax.axis_index("y"), lax.axis_index("c")


def _flip(v, f):
    return 1 - v if f else v


def _win_half(ref, h):
    return ref.at[pl.ds(h * (D // 2), D // 2), :]


def _sq_half(ref, h):
    return ref.at[:, pl.ds(h * (D // 2), D // 2)]


def _gather_copy(part, k, to, send_sems, recv_sems):
    return pltpu.make_async_remote_copy(src_ref=part, dst_ref=part, send_sem=send_sems.at[k], recv_sem=recv_sems.at[k],
                                        device_id=to, device_id_type=MESH)


def _gather_start(out, half, send_sems, recv_sems):
    x, y, c = _place()
    for k, (fx, fy) in enumerate(CHIP_FLIPS):
        _gather_copy(half(out.at[2 * x + y], c), k, (_flip(x, fx), _flip(y, fy), c), send_sems, recv_sems).start()


def _gather_land(out, half, k, send_sems, recv_sems):
    x, y, c = _place()
    sib = (x, y, 1 - c)
    fx, fy = CHIP_FLIPS[k]
    slot = out.at[2 * _flip(x, fx) + _flip(y, fy)]
    _gather_copy(half(slot, c), k, sib, send_sems, recv_sems).wait_recv()
    _gather_copy(half(slot, c), 3 + k, sib, send_sems, recv_sems).start()
    _gather_copy(half(slot, 1 - c), 3 + k, sib, send_sems, recv_sems).wait_recv()


def _gather_drain(out, half, send_sems, recv_sems):
    x, y, c = _place()
    for k, (fx, fy) in enumerate(CHIP_FLIPS):
        _gather_copy(half(out.at[2 * x + y], c), k, (_flip(x, fx), _flip(y, fy), c), send_sems, recv_sems).wait_send()
        _gather_copy(half(out.at[2 * _flip(x, fx) + _flip(y, fy)], c), 3 + k, (x, y, 1 - c), send_sems, recv_sems).wait_send()


def _gather_finish(out, half, send_sems, recv_sems):
    for k in range(len(CHIP_FLIPS)):
        _gather_land(out, half, k, send_sems, recv_sems)
    _gather_drain(out, half, send_sems, recv_sems)


def _chip_sums(c_arr, gwin, gwin_b, gsq, gsq_b):
    steps = 8
    rows, sq_rows = (D // 2) // steps, D // steps

    def body(c_ref, a_ref, p_ref, wb_hbm, sb_hbm, so_ref, sq_ref, sob_ref, sqb_ref, win_got, sq_got, send_sems, recv_sems):
        i = pl.program_id(0)
        x, y, c = _place()

        def copies(k):
            parts = ((wb_hbm.at[pl.ds((1 - c) * (D // 2) + k * rows, rows), :], win_got.at[k]),
                     (sb_hbm.at[:, pl.ds(k * sq_rows, sq_rows), pl.ds((1 - c) * (D // 2), D // 2)], sq_got.at[k]))
            return [pltpu.make_async_remote_copy(src_ref=src, dst_ref=dst, send_sem=send_sems.at[2 * k + a],
                                                 recv_sem=recv_sems.at[2 * k + a], device_id=(x, y, 1 - c), device_id_type=MESH)
                    for a, (src, dst) in enumerate(parts)]

        @pl.when(i == 0)
        def _():
            for k in range(steps):
                for cp in copies(k):
                    cp.start()

        for k in range(steps):
            @pl.when(i == k)
            def _():
                for cp in copies(k):
                    cp.wait_recv()
                so = a_ref[...] + win_got[k].astype(F32)
                sq = p_ref[...] + sq_got[k].astype(F32)
                so_ref[...] = so
                sq_ref[...] = sq
                sob_ref[...] = _bf(so)
                sqb_ref[...] = _bf(sq)

        @pl.when(i == steps - 1)
        def _():
            for k in range(steps):
                for cp in copies(k):
                    cp.wait_send()

    win = lambda f: pl.BlockSpec((rows, D_IN), f)
    sq = lambda f: pl.BlockSpec((3, sq_rows, D // 2), f)
    return pl.pallas_call(
        body, name="chip_sums",
        grid_spec=pltpu.PrefetchScalarGridSpec(
            num_scalar_prefetch=1, grid=(steps,),
            in_specs=[win(lambda i, c: (c[0] * steps + i, 0)), sq(lambda i, c: (0, i, c[0])), HBM_SPEC, HBM_SPEC],
            out_specs=[win(lambda i, c: (i, 0)), sq(lambda i, c: (0, i, 0))] * 2,
            scratch_shapes=[pltpu.VMEM((steps, rows, D_IN), BF16), pltpu.VMEM((steps, 3, sq_rows, D // 2), BF16),
                            pltpu.SemaphoreType.DMA((2 * steps,)), pltpu.SemaphoreType.DMA((2 * steps,))]),
        out_shape=[jax.ShapeDtypeStruct((D // 2, D_IN), F32), jax.ShapeDtypeStruct((3, D, D // 2), F32),
                   jax.ShapeDtypeStruct((D // 2, D_IN), BF16), jax.ShapeDtypeStruct((3, D, D // 2), BF16)],
        compiler_params=_params(("arbitrary",)),
    )(c_arr, gwin, gsq, gwin_b, gsq_b)


def _sum_chips(jc_arr, swin, ssq, win_got, sq_got):
    def body(jc_ref, a_ref, b_ref, p_ref, q_ref, so_ref, sq_ref):
        so_ref[...] = ((a_ref[...] + b_ref[0].astype(F32)) + b_ref[1].astype(F32)) + b_ref[2].astype(F32)
        sq_ref[...] = ((p_ref[...] + q_ref[0].astype(F32)) + q_ref[1].astype(F32)) + q_ref[2].astype(F32)

    rows = 128
    steps = (D // 2) // rows
    sq_rows = SQ_ROWS // steps
    return pl.pallas_call(
        body, name="sum_chips",
        grid_spec=pltpu.PrefetchScalarGridSpec(
            num_scalar_prefetch=1, grid=(steps,),
            in_specs=[pl.BlockSpec((rows, SHARD_W), lambda i, jc: (i, jc[0])),
                      pl.BlockSpec((3, rows, SHARD_W), lambda i, jc: (0, i, 0)),
                      pl.BlockSpec((3, sq_rows, D // 2), lambda i, jc: (0, jc[0] * steps + i, 0)),
                      pl.BlockSpec((3, 3, sq_rows, D // 2), lambda i, jc: (0, 0, i, 0))],
            out_specs=[pl.BlockSpec((rows, SHARD_W), lambda i, jc: (jc[1] * steps + i, 0)),
                       pl.BlockSpec((3, sq_rows, D // 2), lambda i, jc: (0, i, jc[1]))]),
        out_shape=[jax.ShapeDtypeStruct((D, SHARD_W), F32), jax.ShapeDtypeStruct((3, SQ_ROWS, D), F32)],
        compiler_params=_params(("arbitrary",)),
    )(jc_arr, swin, win_got, ssq, sq_got)


def _join_halves(g_win, g_sq):
    def body(win_in, sq_in, win_out, sq_out, send_sems, recv_sems):
        del win_in, sq_in
        x, y, c = _place()
        sib = (x, y, 1 - c)

        def halves(h):
            return _win_half(win_out, h), sq_out.at[:, :, pl.ds(h * (D // 2), D // 2)]

        def copy(a, part):
            return pltpu.make_async_remote_copy(src_ref=part, dst_ref=part, send_sem=send_sems.at[a], recv_sem=recv_sems.at[a],
                                                device_id=sib, device_id_type=MESH)

        sent = [copy(a, part) for a, part in enumerate(halves(c))]
        for cp in sent:
            cp.start()
        for a, part in enumerate(halves(1 - c)):
            copy(a, part).wait_recv()
        for cp in sent:
            cp.wait_send()

    return pl.pallas_call(
        body, name="join_halves",
        in_specs=[HBM_SPEC, HBM_SPEC], out_specs=[HBM_SPEC, HBM_SPEC], input_output_aliases={0: 0, 1: 1},
        out_shape=[jax.ShapeDtypeStruct((D, SHARD_W), F32), jax.ShapeDtypeStruct((3, SQ_ROWS, D), F32)],
        scratch_shapes=[pltpu.SemaphoreType.DMA((2,)), pltpu.SemaphoreType.DMA((2,))],
    )(g_win, g_sq)


def _allreduce_small(vec):
    def body(vec_ref, out_ref, slots, send_sems, recv_sems):
        x, y, c = _place()
        me = 4 * x + 2 * y + c
        slots[me] = vec_ref[...]
        copies = []
        for k in range(1, 8):
            fx, fy, fc = (k >> 2) & 1, (k >> 1) & 1, k & 1
            copies.append(pltpu.make_async_remote_copy(
                src_ref=vec_ref, dst_ref=slots.at[me], send_sem=send_sems.at[k - 1], recv_sem=recv_sems.at[k - 1],
                device_id=(_flip(x, fx), _flip(y, fy), _flip(c, fc)), device_id_type=MESH))
        for cp in copies:
            cp.start()
        for k in range(1, 8):
            fx, fy, fc = (k >> 2) & 1, (k >> 1) & 1, k & 1
            src = 4 * _flip(x, fx) + 2 * _flip(y, fy) + _flip(c, fc)
            pltpu.make_async_remote_copy(src_ref=vec_ref, dst_ref=slots.at[src], send_sem=send_sems.at[k - 1],
                                         recv_sem=recv_sems.at[k - 1], device_id=(x, y, c), device_id_type=MESH).wait_recv()
        for cp in copies:
            cp.wait_send()
        total = slots[0]
        for s in range(1, 8):
            total = total + slots[s]
        out_ref[...] = total

    return pl.pallas_call(
        body, name="allreduce_small",
        in_specs=[pl.BlockSpec(memory_space=pltpu.VMEM)], out_specs=pl.BlockSpec(memory_space=pltpu.VMEM),
        out_shape=jax.ShapeDtypeStruct((8, D), F32),
        scratch_shapes=[pltpu.VMEM((8, 8, D), F32), pltpu.SemaphoreType.DMA((7,)), pltpu.SemaphoreType.DMA((7,))],
    )(vec)


def _adamw_math(w, g, m, v):
    m = ADAM_B1 * m + (1.0 - ADAM_B1) * g
    v = ADAM_B2 * v + (1.0 - ADAM_B2) * (g * g)
    m_hat = m / (1.0 - ADAM_B1 ** ADAM_STEP)
    v_hat = v / (1.0 - ADAM_B2 ** ADAM_STEP)
    delta = -ADAM_LR * (m_hat / (jnp.sqrt(v_hat) + ADAM_EPS) + ADAM_WD * w)
    return delta, m, v


def _adamw(name, w, g, m, v, rows):
    R, C = w.shape

    def body(w_ref, g_ref, m_ref, v_ref, d_out, m_out, v_out):
        d_out[...], m_out[...], v_out[...] = _adamw_math(w_ref[...], g_ref[...], m_ref[...], v_ref[...])

    spec = pl.BlockSpec((rows, C), lambda i: (i, 0))
    return pl.pallas_call(
        body, name=name, grid=(R // rows,), in_specs=[spec] * 4, out_specs=[spec] * 3,
        out_shape=[jax.ShapeDtypeStruct((R, C), F32)] * 3,
        compiler_params=_params(("parallel",)),
    )(w, g, m, v)


def _adamw_square(g_sq, ws, ms, vs):
    def body(g_ref, *refs):
        w_refs, m_refs, v_refs, outs = refs[0:3], refs[3:6], refs[6:9], refs[9:]
        for k in range(3):
            g = g_ref[k]
            outs[k][0] = g
            outs[3 + k][0], outs[6 + k][0], outs[9 + k][0] = _adamw_math(w_refs[k][0], g, m_refs[k][0], v_refs[k][0])

    out = pl.pallas_call(
        body, name="adamw_square", out_shape=[jax.ShapeDtypeStruct((1, SQ_ROWS, D), F32)] * 12,
        compiler_params=_params(),
    )(g_sq, *ws, *ms, *vs)
    return out[0:3], out[3:6], out[6:9], out[9:12]


def _small_update(total, lbw, w8, m8, v8):
    def body(t_ref, lbw_ref, w_ref, m_ref, v_ref, g_out, d_out, m_out, v_out):
        lb = 1.0 / (1.0 + jnp.exp(lbw_ref[1:2, :] - lbw_ref[0:1, :]))
        dlb = t_ref[2:3, :] * lb * (1.0 - lb)
        g_out[...] = jnp.zeros_like(g_out)
        g_out[0:1, :] = t_ref[3:4, :]
        g_out[1:2, :] = dlb
        g_out[2:3, :] = -dlb
        g_out[3:4, :] = t_ref[1:2, :]
        g_out[4:5, :] = t_ref[0:1, :]
        g_out[5:6, :] = t_ref[4:5, :]
        d_out[...], m_out[...], v_out[...] = _adamw_math(w_ref[...], g_out[...], m_ref[...], v_ref[...])

    return pl.pallas_call(
        body, name="small_update", out_shape=[jax.ShapeDtypeStruct((8, D), F32)] * 4,
        compiler_params=_params(),
    )(total, lbw, w8, m8, v8)


def _pack8(norm_w, lbw, hnw, fnw, sinks):
    pad = jnp.zeros((1, D - 16), F32)
    return jnp.concatenate([norm_w, lbw, hnw, fnw.reshape(1, D), jnp.concatenate([sinks, pad], axis=1),
                            jnp.zeros((2, D), F32)], axis=0)


def _unpack8(a):
    return a[0:1], a[1:3], a[3:4], a[5:6, 0:16], a[4]


def _local_step(order_arr, x, tgt, norm_w, lbw, hnw, sinks, fnw, win_mine, wsq_mine, exchange):
    proj, xnt_bf, win_bf = _fwd_proj(order_arr, x, norm_w, win_mine)
    oh, states, wsq_all = _hgrn_fwd(proj, lbw, wsq_mine)
    wsq_bf = wsq_all.reshape(SHARDS, 3, SQ_ROWS, D).transpose(1, 0, 2, 3).reshape(3, D, D)
    oa = _attn_fwd(proj, sinks)
    dx2, doh, doa, dhg, dtail, lhs, rhs, loss8, vec_mid = _mid(x, tgt, proj, oh, oa, hnw, fnw.reshape(1, D), wsq_bf)
    gsq, gsq_b = _wgrad_square(lhs, rhs)
    dhead, dlb = _hgrn_bwd(proj, lbw, states, doh)
    daq, dakv, dsink = _attn_bwd(proj, sinks, oa, doa)
    pieces = [dhead, dhg, daq, dakv, dtail]
    sums = exchange(*_wgrad_in(xnt_bf, pieces), gsq, gsq_b)
    wt_bf = win_bf.transpose(0, 2, 1).reshape(D_IN, D)
    grad_x, gnw, win_got, sq_got = _bwd_dx(pieces, wt_bf, x, norm_w, dx2, sums[2], sums[3])
    sink_row = jnp.concatenate([dsink[:, 0].reshape(1, 16), jnp.zeros((1, D - 16), F32)], axis=1)
    loss_row = jnp.broadcast_to(loss8[0:1, 0:1], (1, D))
    vec = jnp.concatenate([vec_mid[0:2], dlb, gnw, sink_row, loss_row, jnp.zeros((2, D), F32)], axis=0)
    return grad_x, sums, (win_got, sq_got), vec


def kernel(x, norm_w, w_in, hgrn_lower_bound, hgrn_norm_w, w_branch_hgrn, attn_sinks, w_branch_attn, w_out, final_norm_w, loss_target, m_norm_w, m_w_in, m_hgrn_lower_bound, m_hgrn_norm_w, m_w_branch_hgrn, m_attn_sinks, m_w_branch_attn, m_w_out, m_final_norm_w, v_norm_w, v_w_in, v_hgrn_lower_bound, v_hgrn_norm_w, v_w_branch_hgrn, v_attn_sinks, v_w_branch_attn, v_w_out, v_final_norm_w):
    c_arr = lax.axis_index("c").astype(jnp.int32).reshape(1)
    j_arr = (2 * lax.axis_index("x") + lax.axis_index("y")).astype(jnp.int32).reshape(1)
    jc_arr = jnp.concatenate([j_arr, c_arr])

    win_mine, wsq_mine = _cast_shards(j_arr, w_in[0], w_branch_hgrn[0], w_branch_attn[0], w_out[0])
    xi, yi = lax.axis_index("x"), lax.axis_index("y")
    order_arr = jnp.stack([2 * xi + yi] + [2 * _flip(xi, fx) + _flip(yi, fy) for fx, fy in CHIP_FLIPS]).astype(jnp.int32)

    def chip_sums(gwin, gwin_b, gsq, gsq_b):
        return _chip_sums(c_arr, gwin, gwin_b, gsq, gsq_b)

    grad_x, (swin, ssq, _, _), arrived, vec = _local_step(
        order_arr, x[0], loss_target[0], norm_w, hgrn_lower_bound, hgrn_norm_w, attn_sinks, final_norm_w, win_mine, wsq_mine,
        chip_sums)
    g_win, g_sq = _join_halves(*_sum_chips(jc_arr, swin, ssq, *arrived))

    d_win, nm_win, nv_win = _adamw("adamw_w_in", w_in[0], g_win, m_w_in[0], v_w_in[0], 128)
    g_sqs, d_sqs, nm_sqs, nv_sqs = _adamw_square(
        g_sq, (w_branch_hgrn, w_branch_attn, w_out), (m_w_branch_hgrn, m_w_branch_attn, m_w_out),
        (v_w_branch_hgrn, v_w_branch_attn, v_w_out))

    total = _allreduce_small(vec)
    loss = total[5, 0]
    g8, d8, nm8, nv8 = _small_update(
        total, hgrn_lower_bound,
        _pack8(norm_w, hgrn_lower_bound, hgrn_norm_w, final_norm_w, attn_sinks),
        _pack8(m_norm_w, m_hgrn_lower_bound, m_hgrn_norm_w, m_final_norm_w, m_attn_sinks),
        _pack8(v_norm_w, v_hgrn_lower_bound, v_hgrn_norm_w, v_final_norm_w, v_attn_sinks))

    def assemble(win, sq, small):
        nw, lb, hn, sk, fn = _unpack8(small)
        return (nw, win.reshape(1, D, SHARD_W), lb, hn, sq[0], sk, sq[1], sq[2], fn)

    return (loss, grad_x.reshape(1, -1, D),
            *assemble(g_win, g_sqs, g8), *assemble(d_win, d_sqs, d8),
            *assemble(nm_win, nm_sqs, nm8), *assemble(nv_win, nv_sqs, nv8))
```

```python
import functools

import jax
import jax.numpy as jnp
from jax import lax
from jax.experimental import pallas as pl
from jax.experimental.pallas import tpu as pltpu

F32 = jnp.float32
BF16 = jnp.bfloat16

D = 1024
D_IN = 8704
SHARDS = 4
SHARD_W = D_IN // SHARDS
SQ_ROWS = D // SHARDS
HEADS = 8
HEAD_W = 128
CHUNK = 64
SUB = 8
ATT_BLOCK = 128
ATT_STEP = 4
KV_HEADS = 4
HEAD_DIM = 64
EPS = 1e-6
NEG = -1e30
SCALE = HEAD_DIM ** -0.5
COL_HG, COL_AQ, COL_AK, COL_AV, COL_AG, COL_MH, COL_MA = 3072, 4096, 5120, 5376, 5632, 6656, 7680

ADAM_LR, ADAM_B1, ADAM_B2, ADAM_EPS, ADAM_WD, ADAM_STEP = 0.001, 0.9, 0.999, 1e-08, 0.01, 10

VMEM_LIMIT = 56 * 1024 * 1024
MESH = pl.DeviceIdType.MESH
HBM_SPEC = pl.BlockSpec(memory_space=pltpu.HBM)
CHIP_FLIPS = ((1, 0), (0, 1), (1, 1))


def _dot(a, b):
    return jnp.dot(a, b, preferred_element_type=F32)


def _dot_nt(a, b):
    return lax.dot_general(a, b, (((1,), (1,)), ((), ())), preferred_element_type=F32)


def _dot_tn(a, b):
    return lax.dot_general(a, b, (((0,), (0,)), ((), ())), preferred_element_type=F32)


def _sigmoid(v):
    return 1.0 / (1.0 + jnp.exp(-v))


def _bf(v):
    return v.astype(BF16)


def _tri_dot2(tri, v):
    a = _bf(v)
    return _dot(tri, a) + _dot(tri, _bf(v - a.astype(F32)))


def _params(sem=None):
    return pltpu.CompilerParams(dimension_semantics=sem, vmem_limit_bytes=VMEM_LIMIT)


def _cast_shards(j_arr, win_s, wbh_s, wba_s, wout_s):
    steps = 4
    rows = D // steps

    def body(j_ref, win_ref, a_ref, b_ref, c_ref, win_o, sq_o):
        win_o[...] = _bf(win_ref[...])

        @pl.when(pl.program_id(0) == 0)
        def _():
            sq_o[0:SQ_ROWS, :] = _bf(a_ref[...])
            sq_o[SQ_ROWS:2 * SQ_ROWS, :] = _bf(b_ref[...])
            sq_o[2 * SQ_ROWS:3 * SQ_ROWS, :] = _bf(c_ref[...])

    whole = pl.BlockSpec((SQ_ROWS, D), lambda i, j: (0, 0))
    return pl.pallas_call(
        body, name="cast_shards",
        grid_spec=pltpu.PrefetchScalarGridSpec(
            num_scalar_prefetch=1, grid=(steps,),
            in_specs=[pl.BlockSpec((rows, SHARD_W), lambda i, j: (i, 0)), whole, whole, whole],
            out_specs=[pl.BlockSpec((None, rows, SHARD_W), lambda i, j: (j[0], i, 0)),
                       pl.BlockSpec((None, 3 * SQ_ROWS, D), lambda i, j: (j[0], 0, 0))]),
        out_shape=[jax.ShapeDtypeStruct((SHARDS, D, SHARD_W), BF16), jax.ShapeDtypeStruct((SHARDS, 3 * SQ_ROWS, D), BF16)],
        compiler_params=_params(("arbitrary",)),
    )(j_arr, win_s, wbh_s, wba_s, wout_s)


def _fwd_proj(order_arr, x, norm_w, win_all):
    T = x.shape[0]
    tm = min(512, T)
    nt = T // tm

    def body(order_ref, x_ref, nw_ref, win_in, proj_ref, xn_ref, win_out, w_scr, xn_scr, sems, send_sems, recv_sems):
        del win_in
        p, i = pl.program_id(0), pl.program_id(1)

        def load(n):
            return pltpu.make_async_copy(win_out.at[order_ref[n]], w_scr.at[n % 2], sems.at[n % 2])

        @pl.when((p == 0) & (i == 0))
        def _():
            _gather_start(win_out, _win_half, send_sems, recv_sems)
            load(0).start()
            load(0).wait()

        @pl.when((p == 1) & (i == 0))
        def _():
            _gather_land(win_out, _win_half, 0, send_sems, recv_sems)
            load(1).start()
            load(1).wait()

        for k in range(1, SHARDS - 1):
            @pl.when((p == k) & (i == nt // 2))
            def _():
                _gather_land(win_out, _win_half, k, send_sems, recv_sems)
                load(k + 1).start()

            @pl.when((p == k + 1) & (i == 0))
            def _():
                load(k + 1).wait()

        @pl.when(p == 0)
        def _():
            xf = x_ref[...]
            rs = lax.rsqrt(jnp.mean(xf * xf, axis=1, keepdims=True) + EPS)
            xn = _bf((xf * rs) * nw_ref[...])
            xn_scr[i] = xn
            xn_ref[...] = xn.T

        proj_ref[...] = _dot(xn_scr[i], w_scr[p % 2])

        @pl.when((p == SHARDS - 1) & (i == nt - 1))
        def _():
            _gather_drain(win_out, _win_half, send_sems, recv_sems)

    first = lambda p, i: jnp.where(p == 0, i, nt - 1)
    return pl.pallas_call(
        body, name="fwd_proj",
        grid_spec=pltpu.PrefetchScalarGridSpec(
            num_scalar_prefetch=1, grid=(SHARDS, nt),
            in_specs=[pl.BlockSpec((tm, D), lambda p, i, order: (first(p, i), 0)),
                      pl.BlockSpec((1, D), lambda p, i, order: (0, 0)), HBM_SPEC],
            out_specs=[pl.BlockSpec((tm, SHARD_W), lambda p, i, order: (i, order[p])),
                       pl.BlockSpec((D, tm), lambda p, i, order: (0, first(p, i))),
                       HBM_SPEC],
            scratch_shapes=[pltpu.VMEM((2, D, SHARD_W), BF16), pltpu.VMEM((nt, tm, D), BF16), pltpu.SemaphoreType.DMA((2,)),
                            pltpu.SemaphoreType.DMA((6,)), pltpu.SemaphoreType.DMA((6,))]),
        out_shape=[jax.ShapeDtypeStruct((T, D_IN), F32), jax.ShapeDtypeStruct((D, T), BF16),
                   jax.ShapeDtypeStruct((SHARDS, D, SHARD_W), BF16)],
        input_output_aliases={3: 2},
        compiler_params=_params(("arbitrary", "arbitrary")),
    )(order_arr, x, norm_w, win_all)


def _hgrn_gates(hq_ref, hf_ref, lbw_ref, b_scr):
    lb = 1.0 / (1.0 + jnp.exp(lbw_ref[1:2, :] - lbw_ref[0:1, :]))
    hf = hf_ref[...]
    sig = _sigmoid(hf)
    f = lb + (1.0 - lb) * sig
    g = jnp.log(f)
    hq = hq_ref[...]
    sq = _sigmoid(hq)
    q = hq * sq
    row = lax.broadcasted_iota(jnp.int32, (CHUNK, CHUNK), 0)
    col = lax.broadcasted_iota(jnp.int32, (CHUNK, CHUNK), 1)
    causal = row >= col
    b = _tri_dot2(jnp.where(causal, 1.0, 0.0).astype(BF16), g)
    b_scr[...] = b
    bc = b_scr[CHUNK - 1:CHUNK, :]
    r = b_scr[CHUNK // 2 - 1:CHUNK // 2, :]
    return dict(lb=lb, sig=sig, f=f, k=1.0 - f, hq=hq, sq=sq, q=q, b=b, bc=bc, r=r, causal=causal)


def _hgrn_fwd(proj, lbw, wsq_all):
    T = proj.shape[0]
    n = T // CHUNK

    def body(hq_ref, hf_ref, hi_ref, lbw_ref, wsq_in, o_ref, st_ref, wsq_out, s_scr, b_scr, send_sems, recv_sems):
        del wsq_in

        @pl.when(pl.program_id(0) == 0)
        def _():
            _gather_start(wsq_out, _sq_half, send_sems, recv_sems)
            s_scr[...] = jnp.zeros_like(s_scr)

        for c in range(SUB):
            rows = pl.ds(c * CHUNK, CHUNK)
            gt = _hgrn_gates(hq_ref.at[rows, :], hf_ref.at[rows, :], lbw_ref, b_scr.at[rows, :])
            b, bc, r, q, k = gt["b"], gt["bc"], gt["r"], gt["q"], gt["k"]
            qe = _bf(q * jnp.exp(b))
            qr = _bf(q * jnp.exp(b - r))
            kr = _bf(k * jnp.exp(r - b))
            kl = _bf(k * jnp.exp(bc - b))
            ebc = jnp.exp(bc)
            v = _bf(hi_ref[rows, :])
            scores = [_bf(jnp.where(gt["causal"], _dot_nt(qr[:, h * HEAD_W:(h + 1) * HEAD_W], kr[:, h * HEAD_W:(h + 1) * HEAD_W]), 0.0))
                      for h in range(HEADS)]
            for h in range(HEADS):
                sl = slice(h * HEAD_W, (h + 1) * HEAD_W)
                st = s_scr[h]
                st_ref[c, h] = st
                o_ref[rows, sl] = _dot(scores[h], v[:, sl]) + _dot_nt(qe[:, sl], _bf(st))
                s_scr[h] = ebc[:, sl] * st + _dot_tn(v[:, sl], kl[:, sl])

        @pl.when(pl.program_id(0) == n // SUB - 1)
        def _():
            _gather_finish(wsq_out, _sq_half, send_sems, recv_sems)

    col = lambda j: pl.BlockSpec((SUB * CHUNK, D), lambda i: (i, j))
    return pl.pallas_call(
        body, name="hgrn_fwd", grid=(n // SUB,),
        in_specs=[col(0), col(1), col(2), pl.BlockSpec((2, D), lambda i: (0, 0)), HBM_SPEC],
        out_specs=[pl.BlockSpec((SUB * CHUNK, D), lambda i: (i, 0)),
                   pl.BlockSpec((SUB, HEADS, HEAD_W, HEAD_W), lambda i: (i, 0, 0, 0)), HBM_SPEC],
        out_shape=[jax.ShapeDtypeStruct((T, D), F32), jax.ShapeDtypeStruct((n, HEADS, HEAD_W, HEAD_W), F32),
                   jax.ShapeDtypeStruct((SHARDS, 3 * SQ_ROWS, D), BF16)],
        input_output_aliases={4: 2},
        scratch_shapes=[pltpu.VMEM((HEADS, HEAD_W, HEAD_W), F32), pltpu.VMEM((SUB * CHUNK, D), F32),
                        pltpu.SemaphoreType.DMA((6,)), pltpu.SemaphoreType.DMA((6,))],
        compiler_params=_params(("arbitrary",)),
    )(proj, proj, proj, lbw, wsq_all)


def _hgrn_bwd(proj, lbw, states, do):
    T = proj.shape[0]
    n = T // CHUNK

    def body(hq_ref, hf_ref, hi_ref, lbw_ref, st_ref, do_ref, dp_ref, dlb_ref,
             ds_scr, b_scr, dq_scr, dk_scr, dv_scr, late_scr, early_scr, ex_scr):
        @pl.when(pl.program_id(0) == 0)
        def _():
            ds_scr[...] = jnp.zeros_like(ds_scr)
            dlb_ref[...] = jnp.zeros_like(dlb_ref)

        for c in reversed(range(SUB)):
            rows = pl.ds(c * CHUNK, CHUNK)
            gt = _hgrn_gates(hq_ref.at[rows, :], hf_ref.at[rows, :], lbw_ref, b_scr.at[rows, :])
            b, bc, r, q, k = gt["b"], gt["bc"], gt["r"], gt["q"], gt["k"]
            eb = jnp.exp(b)
            er = jnp.exp(b - r)
            erk = jnp.exp(r - b)
            el = jnp.exp(bc - b)
            ebc = jnp.exp(bc)
            qe, qr, kr, kl = _bf(q * eb), _bf(q * er), _bf(k * erk), _bf(k * el)
            v = _bf(hi_ref[rows, :])
            do_b = do_ref[rows, :]
            do_t = do_b.T
            causal_t = lax.broadcasted_iota(jnp.int32, (CHUNK, CHUNK), 0) <= lax.broadcasted_iota(jnp.int32, (CHUNK, CHUNK), 1)
            firsts = []
            for h in range(HEADS):
                sl = slice(h * HEAD_W, (h + 1) * HEAD_W)
                firsts.append((_bf(jnp.where(causal_t, _dot_nt(kr[:, sl], qr[:, sl]), 0.0)),
                               _bf(jnp.where(gt["causal"], _dot_nt(do_b[:, sl], v[:, sl]), 0.0)),
                               _bf(jnp.where(causal_t, _dot_nt(v[:, sl], do_b[:, sl]), 0.0))))
            for h in range(HEADS):
                sl = slice(h * HEAD_W, (h + 1) * HEAD_W)
                st0 = st_ref[c, h]
                dst = ds_scr[h]
                dst_b = _bf(dst)
                a_t, da, da_t = firsts[h]
                mq = _dot(da, kr[:, sl])
                mk = _dot(da_t, qr[:, sl])
                dq_in = eb[:, sl] * _dot(do_b[:, sl], _bf(st0))
                dk_in = el[:, sl] * _dot(v[:, sl], dst_b)
                dq_scr[rows, sl] = er[:, sl] * mq + dq_in
                dk_scr[rows, sl] = erk[:, sl] * mk + dk_in
                dv_scr[rows, sl] = _dot(a_t, do_b[:, sl]) + _dot_nt(kl[:, sl], dst_b)
                late_scr[rows, sl] = q[:, sl] * dq_in + qr[:, sl].astype(F32) * mq - kr[:, sl].astype(F32) * mk
                early_scr[rows, sl] = k[:, sl] * dk_in
                ex_scr[:, sl] = jnp.sum(dst * st0, axis=0, keepdims=True)
                ds_scr[h] = ebc[:, sl] * dst + _dot(do_t[sl, :], qe[:, sl])

            dq, dk = dq_scr[rows, :], dk_scr[rows, :]
            row = lax.broadcasted_iota(jnp.int32, (CHUNK, CHUNK), 0)
            col = lax.broadcasted_iota(jnp.int32, (CHUNK, CHUNK), 1)
            at_or_after = jnp.where(col >= row, 1.0, 0.0).astype(BF16)
            before = jnp.where(col < row, 1.0, 0.0).astype(BF16)
            dg = _tri_dot2(jnp.concatenate([at_or_after, before], axis=1),
                           jnp.concatenate([late_scr[rows, :], early_scr[rows, :]], axis=0)) + ebc * ex_scr[...]
            df = dg / gt["f"] - dk
            sig, sq, hq, lb = gt["sig"], gt["sq"], gt["hq"], gt["lb"]
            dp_ref[rows, 0:D] = _bf(dq * (sq * (1.0 + hq * (1.0 - sq))))
            dp_ref[rows, D:2 * D] = _bf(df * (1.0 - lb) * sig * (1.0 - sig))
            dp_ref[rows, 2 * D:3 * D] = _bf(dv_scr[rows, :])
            dlb_ref[...] += jnp.sum(df * (1.0 - sig), axis=0, keepdims=True)

    ns = n // SUB
    col = lambda j: pl.BlockSpec((SUB * CHUNK, D), lambda i: (ns - 1 - i, j))
    return pl.pallas_call(
        body, name="hgrn_bwd", grid=(ns,),
        in_specs=[col(0), col(1), col(2), pl.BlockSpec((2, D), lambda i: (0, 0)),
                  pl.BlockSpec((SUB, HEADS, HEAD_W, HEAD_W), lambda i: (ns - 1 - i, 0, 0, 0)),
                  pl.BlockSpec((SUB * CHUNK, D), lambda i: (ns - 1 - i, 0))],
        out_specs=[pl.BlockSpec((SUB * CHUNK, 3 * D), lambda i: (ns - 1 - i, 0)),
                   pl.BlockSpec((1, D), lambda i: (0, 0))],
        out_shape=[jax.ShapeDtypeStruct((T, 3 * D), BF16), jax.ShapeDtypeStruct((1, D), F32)],
        scratch_shapes=[pltpu.VMEM((HEADS, HEAD_W, HEAD_W), F32)] + [pltpu.VMEM((SUB * CHUNK, D), F32)] * 6
                       + [pltpu.VMEM((1, D), F32)],
        compiler_params=_params(("arbitrary",)),
    )(proj, proj, proj, lbw, states, do)


def _attn_masks(blk):
    qi = lax.broadcasted_iota(jnp.int32, (ATT_BLOCK, 2 * ATT_BLOCK), 0)
    kj = lax.broadcasted_iota(jnp.int32, (ATT_BLOCK, 2 * ATT_BLOCK), 1)
    band = (kj > qi) & (kj <= qi + ATT_BLOCK)
    return band & ((blk > 0) | (kj >= ATT_BLOCK))


def _head_pair_operand(t, hp, low):
    mine = low if hp == 0 else jnp.logical_not(low)
    both = jnp.where(mine, t, pltpu.roll(t, HEAD_DIM, 1))
    return _bf(jnp.concatenate([jnp.where(low, both, 0.0), jnp.where(low, 0.0, both)], axis=0))


def _attn_probs(s, sink, valid):
    s = jnp.where(valid, s, NEG)
    m = jnp.maximum(jnp.max(s, axis=1, keepdims=True), sink)
    p = jnp.exp(s - m)
    es = jnp.exp(sink - m)
    inv = 1.0 / (jnp.sum(p, axis=1, keepdims=True) + es)
    return p * inv, es * inv


def _attn_fwd(proj, sinks):
    T = proj.shape[0]
    rows_step = ATT_STEP * ATT_BLOCK

    def body(sink_ref, q_ref, kp_ref, kc_ref, vp_ref, vc_ref, o_ref):
        low = lax.broadcasted_iota(jnp.int32, (1, 2 * HEAD_DIM), 1) < HEAD_DIM
        for sb in range(ATT_STEP):
            rows = slice(sb * ATT_BLOCK, (sb + 1) * ATT_BLOCK)
            before = slice((sb - 1) * ATT_BLOCK, sb * ATT_BLOCK)
            valid = _attn_masks(pl.program_id(0) * ATT_STEP + sb)
            kcat = jnp.concatenate([kp_ref[...] if sb == 0 else kc_ref[before, :], kc_ref[rows, :]], axis=0)
            vcat = jnp.concatenate([vp_ref[...] if sb == 0 else vc_ref[before, :], vc_ref[rows, :]], axis=0)
            for h in range(KV_HEADS):
                tl = slice((h // 2) * 128, (h // 2) * 128 + 128)
                mine = low if h % 2 == 0 else jnp.logical_not(low)
                kh = _bf(jnp.where(mine, kcat[:, tl], pltpu.roll(kcat[:, tl], HEAD_DIM, 1)))
                vh = _bf(jnp.where(mine, vcat[:, tl], pltpu.roll(vcat[:, tl], HEAD_DIM, 1)))
                for t in range(2):
                    ql = slice((2 * h + t) * 128, (2 * h + t) * 128 + 128)
                    q2 = q_ref[rows, ql] * SCALE
                    outs = []
                    for p in range(2):
                        qm = _bf(jnp.where(low if p == 0 else jnp.logical_not(low), q2, 0.0))
                        probs, _ = _attn_probs(_dot_nt(qm, kh), sink_ref[0, 4 * h + 2 * t + p], valid)
                        outs.append(_dot(_bf(probs), vh))
                    o_ref[rows, ql] = jnp.where(low, outs[0], outs[1])

    prev = lambda i: jnp.maximum(ATT_STEP * i - 1, 0)
    return pl.pallas_call(
        body, name="attn_fwd", grid=(T // rows_step,),
        in_specs=[pl.BlockSpec(memory_space=pltpu.SMEM),
                  pl.BlockSpec((rows_step, D), lambda i: (i, COL_AQ // D)),
                  pl.BlockSpec((ATT_BLOCK, 256), lambda i: (prev(i), COL_AK // 256)),
                  pl.BlockSpec((rows_step, 256), lambda i: (i, COL_AK // 256)),
                  pl.BlockSpec((ATT_BLOCK, 256), lambda i: (prev(i), COL_AV // 256)),
                  pl.BlockSpec((rows_step, 256), lambda i: (i, COL_AV // 256))],
        out_specs=pl.BlockSpec((rows_step, D), lambda i: (i, 0)),
        out_shape=jax.ShapeDtypeStruct((T, D), F32),
        compiler_params=_params(("arbitrary",)),
    )(sinks, proj, proj, proj, proj, proj)


def _attn_bwd(proj, sinks, o, do):
    T = proj.shape[0]
    nb = T // ATT_BLOCK
    W2 = 2 * ATT_BLOCK

    def body(sink_ref, q_ref, kp_ref, kc_ref, vp_ref, vc_ref, o_ref, do_ref,
             dq_ref, dkv_ref, dsink_ref, ck_scr, cv_scr, nk_scr, nv_scr):
        blk = pl.program_id(0)

        @pl.when(blk == 0)
        def _():
            ck_scr[...] = jnp.zeros_like(ck_scr)
            cv_scr[...] = jnp.zeros_like(cv_scr)
            dsink_ref[...] = jnp.zeros_like(dsink_ref)

        @pl.when(blk < nb)
        def _():
            valid = _attn_masks(blk)
            low = lax.broadcasted_iota(jnp.int32, (1, 2 * HEAD_DIM), 1) < HEAD_DIM
            kcat = jnp.concatenate([kp_ref[...], kc_ref[...]], axis=0)
            vcat = jnp.concatenate([vp_ref[...], vc_ref[...]], axis=0)
            for h in range(KV_HEADS):
                tl = slice((h // 2) * 128, (h // 2) * 128 + 128)
                kbd = _head_pair_operand(kcat[:, tl], h % 2, low)
                vbd = _head_pair_operand(vcat[:, tl], h % 2, low)
                dkbd = jnp.zeros((2 * W2, 128), F32)
                dvbd = jnp.zeros((2 * W2, 128), F32)
                tiles = []
                for t in range(2):
                    ql = slice((2 * h + t) * 128, (2 * h + t) * 128 + 128)
                    q2 = _bf(q_ref[:, ql] * SCALE)
                    do2_b = do_ref[:, ql]
                    doo = do2_b.astype(F32) * o_ref[:, ql]
                    dsum0 = jnp.sum(jnp.where(low, doo, 0.0), axis=1, keepdims=True)
                    dsum1 = jnp.sum(jnp.where(low, 0.0, doo), axis=1, keepdims=True)
                    tiles.append((ql, q2, do2_b, dsum0, dsum1, _dot_nt(q2, kbd), _dot_nt(do2_b, vbd)))
                grads = []
                for t, (ql, q2, do2_b, dsum0, dsum1, s2, dp2) in enumerate(tiles):
                    head = 4 * h + 2 * t
                    p0, ps0 = _attn_probs(s2[:, 0:W2], sink_ref[0, head], valid)
                    p1, ps1 = _attn_probs(s2[:, W2:2 * W2], sink_ref[0, head + 1], valid)
                    ds2 = _bf(jnp.concatenate([p0 * (dp2[:, 0:W2] - dsum0), p1 * (dp2[:, W2:2 * W2] - dsum1)], axis=1))
                    grads.append((ds2, _bf(jnp.concatenate([p0, p1], axis=1))))
                    dsink_ref[head:head + 1, :] += jnp.zeros((1, 128), F32) - jnp.sum(ps0 * dsum0, axis=0, keepdims=True)
                    dsink_ref[head + 1:head + 2, :] += jnp.zeros((1, 128), F32) - jnp.sum(ps1 * dsum1, axis=0, keepdims=True)
                for (ql, q2, do2_b, _, _, _, _), (ds2, p2) in zip(tiles, grads):
                    dq_ref[:, ql] = _bf(_dot(ds2, kbd) * SCALE)
                    dkbd = dkbd + _dot_tn(ds2, q2)
                    dvbd = dvbd + _dot_tn(p2, do2_b)
                dk2 = jnp.where(low, dkbd[0:W2], dkbd[W2:2 * W2])
                dv2 = jnp.where(low, dvbd[0:W2], dvbd[W2:2 * W2])
                dk2 = dk2 + pltpu.roll(dk2, HEAD_DIM, 1)
                dv2 = dv2 + pltpu.roll(dv2, HEAD_DIM, 1)
                if h % 2 == 0:
                    keep_k, keep_v = dk2, dv2
                else:
                    nk_scr[:, tl] = jnp.where(low, keep_k, dk2)
                    nv_scr[:, tl] = jnp.where(low, keep_v, dv2)
            dkv_ref[:, 0:256] = _bf(ck_scr[...] + nk_scr[0:ATT_BLOCK, :])
            dkv_ref[:, 256:512] = _bf(cv_scr[...] + nv_scr[0:ATT_BLOCK, :])
            ck_scr[...] = nk_scr[ATT_BLOCK:2 * ATT_BLOCK, :]
            cv_scr[...] = nv_scr[ATT_BLOCK:2 * ATT_BLOCK, :]

        @pl.when(blk == nb)
        def _():
            dkv_ref[:, 0:256] = _bf(ck_scr[...])
            dkv_ref[:, 256:512] = _bf(cv_scr[...])

    cur = lambda i: jnp.minimum(i, nb - 1)
    prev = lambda i: jnp.maximum(cur(i) - 1, 0)
    late = lambda i: jnp.maximum(i - 1, 0)
    return pl.pallas_call(
        body, name="attn_bwd", grid=(nb + 1,),
        in_specs=[pl.BlockSpec(memory_space=pltpu.SMEM),
                  pl.BlockSpec((ATT_BLOCK, D), lambda i: (cur(i), COL_AQ // D)),
                  pl.BlockSpec((ATT_BLOCK, 256), lambda i: (prev(i), COL_AK // 256)),
                  pl.BlockSpec((ATT_BLOCK, 256), lambda i: (cur(i), COL_AK // 256)),
                  pl.BlockSpec((ATT_BLOCK, 256), lambda i: (prev(i), COL_AV // 256)),
                  pl.BlockSpec((ATT_BLOCK, 256), lambda i: (cur(i), COL_AV // 256)),
                  pl.BlockSpec((ATT_BLOCK, D), lambda i: (cur(i), 0)),
                  pl.BlockSpec((ATT_BLOCK, D), lambda i: (cur(i), 0))],
        out_specs=[pl.BlockSpec((ATT_BLOCK, D), lambda i: (cur(i), 0)),
                   pl.BlockSpec((ATT_BLOCK, 512), lambda i: (late(i), 0)),
                   pl.BlockSpec((16, 128), lambda i: (0, 0))],
        out_shape=[jax.ShapeDtypeStruct((T, D), BF16), jax.ShapeDtypeStruct((T, 512), BF16),
                   jax.ShapeDtypeStruct((16, 128), F32)],
        scratch_shapes=[pltpu.VMEM((ATT_BLOCK, 256), F32), pltpu.VMEM((ATT_BLOCK, 256), F32),
                        pltpu.VMEM((2 * ATT_BLOCK, 256), F32), pltpu.VMEM((2 * ATT_BLOCK, 256), F32)],
        compiler_params=_params(("arbitrary",)),
    )(sinks, proj, proj, proj, proj, proj, o, do)


def _mid(x, tgt, proj, oh, oa, hnw, fnw, wsq_bf):
    T = x.shape[0]
    tm = min(256, T)
    nt = T // tm

    def body(x_ref, tgt_ref, oh_ref, oa_ref, hg_ref, ag0_ref, ag1_ref, mh0_ref, mh1_ref, ma0_ref, ma1_ref,
             hnw_ref, fnw_ref, w_hbm,
             dx2_ref, doh_ref, doa_ref, dhg_ref, dtail_ref, lhs_ref, rhs_ref, loss_ref, vec_ref,
             w_scr, xh_scr, rs_scr, sem):
        @pl.when(pl.program_id(0) == 0)
        def _():
            cp = pltpu.make_async_copy(w_hbm, w_scr, sem)
            cp.start()
            cp.wait()
            loss_ref[...] = jnp.zeros_like(loss_ref)
            vec_ref[...] = jnp.zeros_like(vec_ref)

        oh = oh_ref[...]
        for h in range(HEADS):
            sl = slice(h * HEAD_W, (h + 1) * HEAD_W)
            ohh = oh[:, sl]
            rs = lax.rsqrt(jnp.mean(ohh * ohh, axis=1, keepdims=True) + EPS)
            xh_scr[:, sl] = ohh * rs
            rs_scr[:, sl] = jnp.broadcast_to(rs, (tm, HEAD_W))
        xh = xh_scr[...]
        hnw = hnw_ref[...]
        on = xh * hnw
        hg = hg_ref[...]
        sg = _sigmoid(hg)
        silu_g = hg * sg
        gated_h = _bf(on * silu_g)
        oa = oa_ref[...]
        ag = jnp.concatenate([ag0_ref[...], ag1_ref[...]], axis=1)
        sa = _sigmoid(ag)
        silu_a = ag * sa
        gated_a = _bf(oa * silu_a)
        yh = _dot(gated_h, w_scr[0])
        ya = _dot(gated_a, w_scr[1])
        lhs_ref[0] = gated_h.T
        lhs_ref[1] = gated_a.T
        smh = _sigmoid(jnp.concatenate([mh0_ref[...], mh1_ref[...]], axis=1))
        sma = _sigmoid(jnp.concatenate([ma0_ref[...], ma1_ref[...]], axis=1))
        merged = _bf(smh * yh + sma * ya)
        lhs_ref[2] = merged.T
        x2 = x_ref[...] + _dot(merged, w_scr[2])
        rs2 = lax.rsqrt(jnp.mean(x2 * x2, axis=1, keepdims=True) + EPS)
        xh2 = x2 * rs2
        fnw = fnw_ref[...]
        diff = xh2 * fnw - tgt_ref[...]
        loss_ref[...] += jnp.zeros_like(loss_ref) + jnp.sum(diff * diff) * (0.5 / D)

        dy = diff * (1.0 / D)
        vec_ref[0:1, :] += jnp.sum(dy * xh2, axis=0, keepdims=True)
        gy = dy * fnw
        dx2 = rs2 * (gy - xh2 * jnp.mean(gy * xh2, axis=1, keepdims=True))
        dx2_ref[...] = dx2
        dx2_b = _bf(dx2)
        rhs_ref[2] = dx2_b
        dmerged = _dot_nt(dx2_b, w_scr[2])
        dyh = dmerged * smh
        dya = dmerged * sma
        dtail_ref[:, D:2 * D] = _bf(dyh * yh * (1.0 - smh))
        dtail_ref[:, 2 * D:3 * D] = _bf(dya * ya * (1.0 - sma))
        dyh_b, dya_b = _bf(dyh), _bf(dya)
        rhs_ref[0] = dyh_b
        rhs_ref[1] = dya_b
        dgh = _dot_nt(dyh_b, w_scr[0])
        dga = _dot_nt(dya_b, w_scr[1])
        don = dgh * silu_g
        dhg_ref[...] = _bf(dgh * on * (sg * (1.0 + hg * (1.0 - sg))))
        vec_ref[1:2, :] += jnp.sum(don * xh, axis=0, keepdims=True)
        gxh = don * hnw
        rsb = rs_scr[...]
        for h in range(HEADS):
            sl = slice(h * HEAD_W, (h + 1) * HEAD_W)
            gh, xhh = gxh[:, sl], xh[:, sl]
            doh_ref[:, sl] = _bf(rsb[:, sl] * (gh - xhh * jnp.mean(gh * xhh, axis=1, keepdims=True)))
        doa_ref[...] = _bf(dga * silu_a)
        dtail_ref[:, 0:D] = _bf(dga * oa * (sa * (1.0 + ag * (1.0 - sa))))

    row = lambda w, j: pl.BlockSpec((tm, w), lambda i: (i, j))
    const = lambda r, c: pl.BlockSpec((r, c), lambda i: (0, 0))
    stack = pl.BlockSpec((3, tm, D), lambda i: (0, i, 0))
    stack_t = pl.BlockSpec((3, D, tm), lambda i: (0, 0, i))
    return pl.pallas_call(
        body, name="mid", grid=(nt,),
        in_specs=[row(D, 0), row(D, 0), row(D, 0), row(D, 0), row(D, COL_HG // D),
                  row(512, COL_AG // 512), row(512, COL_AG // 512 + 1),
                  row(512, COL_MH // 512), row(512, COL_MH // 512 + 1),
                  row(512, COL_MA // 512), row(512, COL_MA // 512 + 1),
                  const(1, D), const(1, D), HBM_SPEC],
        out_specs=[row(D, 0), row(D, 0), row(D, 0), row(D, 0), row(3 * D, 0), stack_t, stack, const(8, 128), const(8, D)],
        out_shape=[jax.ShapeDtypeStruct((T, D), F32), jax.ShapeDtypeStruct((T, D), BF16), jax.ShapeDtypeStruct((T, D), BF16),
                   jax.ShapeDtypeStruct((T, D), BF16), jax.ShapeDtypeStruct((T, 3 * D), BF16),
                   jax.ShapeDtypeStruct((3, D, T), BF16), jax.ShapeDtypeStruct((3, T, D), BF16),
                   jax.ShapeDtypeStruct((8, 128), F32), jax.ShapeDtypeStruct((8, D), F32)],
        scratch_shapes=[pltpu.VMEM((3, D, D), BF16), pltpu.VMEM((tm, D), F32), pltpu.VMEM((tm, D), F32),
                        pltpu.SemaphoreType.DMA],
        compiler_params=_params(("arbitrary",)),
    )(x, tgt, oh, oa, proj, proj, proj, proj, proj, proj, proj, hnw, fnw, wsq_bf)


def _wgrad_square(lhs_t, rhs):
    T = rhs.shape[1]
    tk = min(2048, T)
    steps = T // tk

    def body(a_ref, b_ref, g_ref, gb_ref):
        part = _dot(a_ref[...], b_ref[...])

        @pl.when(pl.program_id(1) == 0)
        def _():
            g_ref[...] = part

        @pl.when(pl.program_id(1) > 0)
        def _():
            g_ref[...] += part

        @pl.when(pl.program_id(1) == steps - 1)
        def _():
            gb_ref[...] = _bf(g_ref[...])

    return pl.pallas_call(
        body, name="wgrad_square", grid=(3, steps),
        in_specs=[pl.BlockSpec((None, D, tk), lambda k, i: (k, 0, i)), pl.BlockSpec((None, tk, D), lambda k, i: (k, i, 0))],
        out_specs=[pl.BlockSpec((None, D, D), lambda k, i: (k, 0, 0))] * 2,
        out_shape=[jax.ShapeDtypeStruct((3, D, D), F32), jax.ShapeDtypeStruct((3, D, D), BF16)],
        compiler_params=_params(("parallel", "arbitrary")),
    )(lhs_t, rhs)


def _bwd_dx(pieces, wt_bf, x, norm_w, dx2, swin_b, ssq_b):
    T = x.shape[0]
    tm = min(512, T)
    nt = T // tm
    widths = [p.shape[1] for p in pieces]
    n_p = len(pieces)

    def body(*refs):
        piece_refs = refs[:n_p]
        (w_hbm, x_ref, nw_ref, dx2_ref, swin_ref, ssq_ref,
         gx_ref, gnw_ref, win_got, sq_got, w_scr, sem, send_sems, recv_sems) = refs[n_p:]

        def scatter_copies():
            x_, y_, c_ = _place()
            copies = []
            for k, (fx, fy) in enumerate(CHIP_FLIPS):
                px, py = _flip(x_, fx), _flip(y_, fy)
                jr = 2 * px + py
                for a, (src, dst) in enumerate(((swin_ref.at[:, pl.ds(jr * SHARD_W, SHARD_W)], win_got.at[k]),
                                                (ssq_ref.at[:, pl.ds(jr * SQ_ROWS, SQ_ROWS), :], sq_got.at[k]))):
                    copies.append(pltpu.make_async_remote_copy(
                        src_ref=src, dst_ref=dst, send_sem=send_sems.at[2 * k + a], recv_sem=recv_sems.at[2 * k + a],
                        device_id=(px, py, c_), device_id_type=MESH))
            return copies

        @pl.when(pl.program_id(0) == 0)
        def _():
            for cp in scatter_copies():
                cp.start()
            cp = pltpu.make_async_copy(w_hbm, w_scr, sem)
            cp.start()
            cp.wait()
            gnw_ref[...] = jnp.zeros_like(gnw_ref)

        dxn = None
        off = 0
        for ref, w in zip(piece_refs, widths):
            part = _dot(ref[...], w_scr[off:off + w, :])
            dxn = part if dxn is None else dxn + part
            off += w
        xf = x_ref[...]
        rs = lax.rsqrt(jnp.mean(xf * xf, axis=1, keepdims=True) + EPS)
        xh = xf * rs
        gnw_ref[...] += jnp.sum(dxn * xh, axis=0, keepdims=True)
        gx = dxn * nw_ref[...]
        gx_ref[...] = rs * (gx - xh * jnp.mean(gx * xh, axis=1, keepdims=True)) + dx2_ref[...]

        @pl.when(pl.program_id(0) == nt - 1)
        def _():
            for cp in scatter_copies():
                cp.wait()

    row = lambda w: pl.BlockSpec((tm, w), lambda i: (i, 0))
    return pl.pallas_call(
        body, name="bwd_dx", grid=(nt,),
        in_specs=[row(w) for w in widths] + [HBM_SPEC, row(D), pl.BlockSpec((1, D), lambda i: (0, 0)), row(D), HBM_SPEC, HBM_SPEC],
        out_specs=[row(D), pl.BlockSpec((1, D), lambda i: (0, 0)), HBM_SPEC, HBM_SPEC],
        out_shape=[jax.ShapeDtypeStruct((T, D), F32), jax.ShapeDtypeStruct((1, D), F32),
                   jax.ShapeDtypeStruct((3, D // 2, SHARD_W), BF16), jax.ShapeDtypeStruct((3, 3, SQ_ROWS, D // 2), BF16)],
        scratch_shapes=[pltpu.VMEM((D_IN, D), BF16), pltpu.SemaphoreType.DMA,
                        pltpu.SemaphoreType.DMA((6,)), pltpu.SemaphoreType.DMA((6,))],
        compiler_params=_params(("arbitrary",)),
    )(*pieces, wt_bf, x, norm_w, dx2, swin_b, ssq_b)


W_PIECES = ((0, 1024, 3), (COL_HG, 1024, 1), (COL_AQ, 1024, 1), (COL_AK, 512, 1), (COL_AG, 512, 6))


def _wgrad_in(xnt_bf, pieces):
    T = xnt_bf.shape[1]
    bufs = ()
    for n, (piece, (col, wb, blocks)) in enumerate(zip(pieces, W_PIECES)):
        tk = min(2048 if wb == 1024 else 4096, T)
        steps = T // tk

        def body(xnt_ref, p_ref, *rest):
            g_ref, gb_ref = rest[-2:]
            part = _dot(xnt_ref[...], p_ref[...])

            @pl.when(pl.program_id(1) == 0)
            def _():
                g_ref[...] = part

            @pl.when(pl.program_id(1) > 0)
            def _():
                g_ref[...] += part

            @pl.when(pl.program_id(1) == steps - 1)
            def _():
                gb_ref[...] = _bf(g_ref[...])

        out = pl.BlockSpec((D, wb), lambda jb, i, base=col // wb: (0, base + jb))
        bufs = pl.pallas_call(
            body, name=f"wgrad_in_{n}", grid=(blocks, steps),
            in_specs=[pl.BlockSpec((D, tk), lambda jb, i: (0, i)), pl.BlockSpec((tk, wb), lambda jb, i: (i, jb))]
                     + [HBM_SPEC] * len(bufs),
            out_specs=[out, out],
            out_shape=[jax.ShapeDtypeStruct((D, D_IN), F32), jax.ShapeDtypeStruct((D, D_IN), BF16)],
            input_output_aliases={2: 0, 3: 1} if bufs else {},
            compiler_params=_params(("parallel", "arbitrary")),
        )(xnt_bf, piece, *bufs)
    return bufs


def _place():
    return lax.axis_index("x"), lax.axis_index("y"), lax.axis_index("c")


def _flip(v, f):
    return 1 - v if f else v


def _win_half(ref, h):
    return ref.at[pl.ds(h * (D // 2), D // 2), :]


def _sq_half(ref, h):
    return ref.at[:, pl.ds(h * (D // 2), D // 2)]


def _gather_copy(part, k, to, send_sems, recv_sems):
    return pltpu.make_async_remote_copy(src_ref=part, dst_ref=part, send_sem=send_sems.at[k], recv_sem=recv_sems.at[k],
                                        device_id=to, device_id_type=MESH)


def _gather_start(out, half, send_sems, recv_sems):
    x, y, c = _place()
    for k, (fx, fy) in enumerate(CHIP_FLIPS):
        _gather_copy(half(out.at[2 * x + y], c), k, (_flip(x, fx), _flip(y, fy), c), send_sems, recv_sems).start()


def _gather_land(out, half, k, send_sems, recv_sems):
    x, y, c = _place()
    sib = (x, y, 1 - c)
    fx, fy = CHIP_FLIPS[k]
    slot = out.at[2 * _flip(x, fx) + _flip(y, fy)]
    _gather_copy(half(slot, c), k, sib, send_sems, recv_sems).wait_recv()
    _gather_copy(half(slot, c), 3 + k, sib, send_sems, recv_sems).start()
    _gather_copy(half(slot, 1 - c), 3 + k, sib, send_sems, recv_sems).wait_recv()


def _gather_drain(out, half, send_sems, recv_sems):
    x, y, c = _place()
    for k, (fx, fy) in enumerate(CHIP_FLIPS):
        _gather_copy(half(out.at[2 * x + y], c), k, (_flip(x, fx), _flip(y, fy), c), send_sems, recv_sems).wait_send()
        _gather_copy(half(out.at[2 * _flip(x, fx) + _flip(y, fy)], c), 3 + k, (x, y, 1 - c), send_sems, recv_sems).wait_send()


def _gather_finish(out, half, send_sems, recv_sems):
    for k in range(len(CHIP_FLIPS)):
        _gather_land(out, half, k, send_sems, recv_sems)
    _gather_drain(out, half, send_sems, recv_sems)


def _chip_sums(c_arr, gwin, gwin_b, gsq, gsq_b):
    steps = 8
    rows, sq_rows = (D // 2) // steps, D // steps

    def body(c_ref, a_ref, p_ref, wb_hbm, sb_hbm, so_ref, sq_ref, sob_ref, sqb_ref, win_got, sq_got, send_sems, recv_sems):
        i = pl.program_id(0)
        x, y, c = _place()

        def copies(k):
            parts = ((wb_hbm.at[pl.ds((1 - c) * (D // 2) + k * rows, rows), :], win_got.at[k]),
                     (sb_hbm.at[:, pl.ds(k * sq_rows, sq_rows), pl.ds((1 - c) * (D // 2), D // 2)], sq_got.at[k]))
            return [pltpu.make_async_remote_copy(src_ref=src, dst_ref=dst, send_sem=send_sems.at[2 * k + a],
                                                 recv_sem=recv_sems.at[2 * k + a], device_id=(x, y, 1 - c), device_id_type=MESH)
                    for a, (src, dst) in enumerate(parts)]

        @pl.when(i == 0)
        def _():
            for k in range(steps):
                for cp in copies(k):
                    cp.start()

        for k in range(steps):
            @pl.when(i == k)
            def _():
                for cp in copies(k):
                    cp.wait_recv()
                so = a_ref[...] + win_got[k].astype(F32)
                sq = p_ref[...] + sq_got[k].astype(F32)
                so_ref[...] = so
                sq_ref[...] = sq
                sob_ref[...] = _bf(so)
                sqb_ref[...] = _bf(sq)

        @pl.when(i == steps - 1)
        def _():
            for k in range(steps):
                for cp in copies(k):
                    cp.wait_send()

    win = lambda f: pl.BlockSpec((rows, D_IN), f)
    sq = lambda f: pl.BlockSpec((3, sq_rows, D // 2), f)
    return pl.pallas_call(
        body, name="chip_sums",
        grid_spec=pltpu.PrefetchScalarGridSpec(
            num_scalar_prefetch=1, grid=(steps,),
            in_specs=[win(lambda i, c: (c[0] * steps + i, 0)), sq(lambda i, c: (0, i, c[0])), HBM_SPEC, HBM_SPEC],
            out_specs=[win(lambda i, c: (i, 0)), sq(lambda i, c: (0, i, 0))] * 2,
            scratch_shapes=[pltpu.VMEM((steps, rows, D_IN), BF16), pltpu.VMEM((steps, 3, sq_rows, D // 2), BF16),
                            pltpu.SemaphoreType.DMA((2 * steps,)), pltpu.SemaphoreType.DMA((2 * steps,))]),
        out_shape=[jax.ShapeDtypeStruct((D // 2, D_IN), F32), jax.ShapeDtypeStruct((3, D, D // 2), F32),
                   jax.ShapeDtypeStruct((D // 2, D_IN), BF16), jax.ShapeDtypeStruct((3, D, D // 2), BF16)],
        compiler_params=_params(("arbitrary",)),
    )(c_arr, gwin, gsq, gwin_b, gsq_b)


def _sum_chips(jc_arr, swin, ssq, win_got, sq_got):
    def body(jc_ref, a_ref, b_ref, p_ref, q_ref, so_ref, sq_ref):
        so_ref[...] = ((a_ref[...] + b_ref[0].astype(F32)) + b_ref[1].astype(F32)) + b_ref[2].astype(F32)
        sq_ref[...] = ((p_ref[...] + q_ref[0].astype(F32)) + q_ref[1].astype(F32)) + q_ref[2].astype(F32)

    rows = 128
    steps = (D // 2) // rows
    sq_rows = SQ_ROWS // steps
    return pl.pallas_call(
        body, name="sum_chips",
        grid_spec=pltpu.PrefetchScalarGridSpec(
            num_scalar_prefetch=1, grid=(steps,),
            in_specs=[pl.BlockSpec((rows, SHARD_W), lambda i, jc: (i, jc[0])),
                      pl.BlockSpec((3, rows, SHARD_W), lambda i, jc: (0, i, 0)),
                      pl.BlockSpec((3, sq_rows, D // 2), lambda i, jc: (0, jc[0] * steps + i, 0)),
                      pl.BlockSpec((3, 3, sq_rows, D // 2), lambda i, jc: (0, 0, i, 0))],
            out_specs=[pl.BlockSpec((rows, SHARD_W), lambda i, jc: (jc[1] * steps + i, 0)),
                       pl.BlockSpec((3, sq_rows, D // 2), lambda i, jc: (0, i, jc[1]))]),
        out_shape=[jax.ShapeDtypeStruct((D, SHARD_W), F32), jax.ShapeDtypeStruct((3, SQ_ROWS, D), F32)],
        compiler_params=_params(("arbitrary",)),
    )(jc_arr, swin, win_got, ssq, sq_got)


def _join_halves(g_win, g_sq):
    def body(win_in, sq_in, win_out, sq_out, send_sems, recv_sems):
        del win_in, sq_in
        x, y, c = _place()
        sib = (x, y, 1 - c)

        def halves(h):
            return _win_half(win_out, h), sq_out.at[:, :, pl.ds(h * (D // 2), D // 2)]

        def copy(a, part):
            return pltpu.make_async_remote_copy(src_ref=part, dst_ref=part, send_sem=send_sems.at[a], recv_sem=recv_sems.at[a],
                                                device_id=sib, device_id_type=MESH)

        sent = [copy(a, part) for a, part in enumerate(halves(c))]
        for cp in sent:
            cp.start()
        for a, part in enumerate(halves(1 - c)):
            copy(a, part).wait_recv()
        for cp in sent:
            cp.wait_send()

    return pl.pallas_call(
        body, name="join_halves",
        in_specs=[HBM_SPEC, HBM_SPEC], out_specs=[HBM_SPEC, HBM_SPEC], input_output_aliases={0: 0, 1: 1},
        out_shape=[jax.ShapeDtypeStruct((D, SHARD_W), F32), jax.ShapeDtypeStruct((3, SQ_ROWS, D), F32)],
        scratch_shapes=[pltpu.SemaphoreType.DMA((2,)), pltpu.SemaphoreType.DMA((2,))],
    )(g_win, g_sq)


def _allreduce_small(vec):
    def body(vec_ref, out_ref, slots, send_sems, recv_sems):
        x, y, c = _place()
        me = 4 * x + 2 * y + c
        slots[me] = vec_ref[...]
        copies = []
        for k in range(1, 8):
            fx, fy, fc = (k >> 2) & 1, (k >> 1) & 1, k & 1
            copies.append(pltpu.make_async_remote_copy(
                src_ref=vec_ref, dst_ref=slots.at[me], send_sem=send_sems.at[k - 1], recv_sem=recv_sems.at[k - 1],
                device_id=(_flip(x, fx), _flip(y, fy), _flip(c, fc)), device_id_type=MESH))
        for cp in copies:
            cp.start()
        for k in range(1, 8):
            fx, fy, fc = (k >> 2) & 1, (k >> 1) & 1, k & 1
            src = 4 * _flip(x, fx) + 2 * _flip(y, fy) + _flip(c, fc)
            pltpu.make_async_remote_copy(src_ref=vec_ref, dst_ref=slots.at[src], send_sem=send_sems.at[k - 1],
                                         recv_sem=recv_sems.at[k - 1], device_id=(x, y, c), device_id_type=MESH).wait_recv()
        for cp in copies:
            cp.wait_send()
        total = slots[0]
        for s in range(1, 8):
            total = total + slots[s]
        out_ref[...] = total

    return pl.pallas_call(
        body, name="allreduce_small",
        in_specs=[pl.BlockSpec(memory_space=pltpu.VMEM)], out_specs=pl.BlockSpec(memory_space=pltpu.VMEM),
        out_shape=jax.ShapeDtypeStruct((8, D), F32),
        scratch_shapes=[pltpu.VMEM((8, 8, D), F32), pltpu.SemaphoreType.DMA((7,)), pltpu.SemaphoreType.DMA((7,))],
    )(vec)


def _adamw_math(w, g, m, v):
    m = ADAM_B1 * m + (1.0 - ADAM_B1) * g
    v = ADAM_B2 * v + (1.0 - ADAM_B2) * (g * g)
    m_hat = m / (1.0 - ADAM_B1 ** ADAM_STEP)
    v_hat = v / (1.0 - ADAM_B2 ** ADAM_STEP)
    delta = -ADAM_LR * (m_hat / (jnp.sqrt(v_hat) + ADAM_EPS) + ADAM_WD * w)
    return delta, m, v


def _adamw(name, w, g, m, v, rows):
    R, C = w.shape

    def body(w_ref, g_ref, m_ref, v_ref, d_out, m_out, v_out):
        d_out[...], m_out[...], v_out[...] = _adamw_math(w_ref[...], g_ref[...], m_ref[...], v_ref[...])

    spec = pl.BlockSpec((rows, C), lambda i: (i, 0))
    return pl.pallas_call(
        body, name=name, grid=(R // rows,), in_specs=[spec] * 4, out_specs=[spec] * 3,
        out_shape=[jax.ShapeDtypeStruct((R, C), F32)] * 3,
        compiler_params=_params(("parallel",)),
    )(w, g, m, v)


def _adamw_square(g_sq, ws, ms, vs):
    def body(g_ref, *refs):
        w_refs, m_refs, v_refs, outs = refs[0:3], refs[3:6], refs[6:9], refs[9:]
        for k in range(3):
            g = g_ref[k]
            outs[k][0] = g
            outs[3 + k][0], outs[6 + k][0], outs[9 + k][0] = _adamw_math(w_refs[k][0], g, m_refs[k][0], v_refs[k][0])

    out = pl.pallas_call(
        body, name="adamw_square", out_shape=[jax.ShapeDtypeStruct((1, SQ_ROWS, D), F32)] * 12,
        compiler_params=_params(),
    )(g_sq, *ws, *ms, *vs)
    return out[0:3], out[3:6], out[6:9], out[9:12]


def _small_update(total, ws, ms, vs):
    def body(t_ref, *refs):
        w_refs, m_refs, v_refs, outs = refs[0:5], refs[5:10], refs[10:15], refs[15:]
        lb = 1.0 / (1.0 + jnp.exp(w_refs[1][1:2, :] - w_refs[1][0:1, :]))
        dlb = t_ref[2:3, :] * lb * (1.0 - lb)
        outs[0][...] = t_ref[3:4, :]
        outs[1][0:1, :] = dlb
        outs[1][1:2, :] = -dlb
        outs[2][...] = t_ref[1:2, :]
        outs[3][...] = t_ref[4:5, 0:16]
        outs[4][...] = t_ref[0:1, :]
        for k in range(5):
            outs[5 + k][...], outs[10 + k][...], outs[15 + k][...] = _adamw_math(
                w_refs[k][...], outs[k][...], m_refs[k][...], v_refs[k][...])

    shapes = [jax.ShapeDtypeStruct(w.shape, F32) for w in ws]
    out = pl.pallas_call(body, name="small_update", out_shape=shapes * 4, compiler_params=_params())(total, *ws, *ms, *vs)
    return out[0:5], out[5:10], out[10:15], out[15:20]


def _local_step(order_arr, x, tgt, norm_w, lbw, hnw, sinks, fnw, win_mine, wsq_mine, exchange):
    proj, xnt_bf, win_bf = _fwd_proj(order_arr, x, norm_w, win_mine)
    oh, states, wsq_all = _hgrn_fwd(proj, lbw, wsq_mine)
    wsq_bf = wsq_all.reshape(SHARDS, 3, SQ_ROWS, D).transpose(1, 0, 2, 3).reshape(3, D, D)
    oa = _attn_fwd(proj, sinks)
    dx2, doh, doa, dhg, dtail, lhs, rhs, loss8, vec_mid = _mid(x, tgt, proj, oh, oa, hnw, fnw.reshape(1, D), wsq_bf)
    gsq, gsq_b = _wgrad_square(lhs, rhs)
    dhead, dlb = _hgrn_bwd(proj, lbw, states, doh)
    daq, dakv, dsink = _attn_bwd(proj, sinks, oa, doa)
    pieces = [dhead, dhg, daq, dakv, dtail]
    sums = exchange(*_wgrad_in(xnt_bf, pieces), gsq, gsq_b)
    wt_bf = win_bf.transpose(0, 2, 1).reshape(D_IN, D)
    grad_x, gnw, win_got, sq_got = _bwd_dx(pieces, wt_bf, x, norm_w, dx2, sums[2], sums[3])
    sink_row = jnp.concatenate([dsink[:, 0].reshape(1, 16), jnp.zeros((1, D - 16), F32)], axis=1)
    loss_row = jnp.broadcast_to(loss8[0:1, 0:1], (1, D))
    vec = jnp.concatenate([vec_mid[0:2], dlb, gnw, sink_row, loss_row, jnp.zeros((2, D), F32)], axis=0)
    return grad_x, sums, (win_got, sq_got), vec


def kernel(x, norm_w, w_in, hgrn_lower_bound, hgrn_norm_w, w_branch_hgrn, attn_sinks, w_branch_attn, w_out, final_norm_w, loss_target, m_norm_w, m_w_in, m_hgrn_lower_bound, m_hgrn_norm_w, m_w_branch_hgrn, m_attn_sinks, m_w_branch_attn, m_w_out, m_final_norm_w, v_norm_w, v_w_in, v_hgrn_lower_bound, v_hgrn_norm_w, v_w_branch_hgrn, v_attn_sinks, v_w_branch_attn, v_w_out, v_final_norm_w):
    c_arr = lax.axis_index("c").astype(jnp.int32).reshape(1)
    j_arr = (2 * lax.axis_index("x") + lax.axis_index("y")).astype(jnp.int32).reshape(1)
    jc_arr = jnp.concatenate([j_arr, c_arr])

    win_mine, wsq_mine = _cast_shards(j_arr, w_in[0], w_branch_hgrn[0], w_branch_attn[0], w_out[0])
    xi, yi = lax.axis_index("x"), lax.axis_index("y")
    order_arr = jnp.stack([2 * xi + yi] + [2 * _flip(xi, fx) + _flip(yi, fy) for fx, fy in CHIP_FLIPS]).astype(jnp.int32)

    def chip_sums(gwin, gwin_b, gsq, gsq_b):
        return _chip_sums(c_arr, gwin, gwin_b, gsq, gsq_b)

    grad_x, (swin, ssq, _, _), arrived, vec = _local_step(
        order_arr, x[0], loss_target[0], norm_w, hgrn_lower_bound, hgrn_norm_w, attn_sinks, final_norm_w, win_mine, wsq_mine,
        chip_sums)
    g_win, g_sq = _join_halves(*_sum_chips(jc_arr, swin, ssq, *arrived))

    d_win, nm_win, nv_win = _adamw("adamw_w_in", w_in[0], g_win, m_w_in[0], v_w_in[0], 128)
    g_sqs, d_sqs, nm_sqs, nv_sqs = _adamw_square(
        g_sq, (w_branch_hgrn, w_branch_attn, w_out), (m_w_branch_hgrn, m_w_branch_attn, m_w_out),
        (v_w_branch_hgrn, v_w_branch_attn, v_w_out))

    total = _allreduce_small(vec)
    loss = total[5, 0]
    g8, d8, nm8, nv8 = _small_update(
        total,
        (norm_w, hgrn_lower_bound, hgrn_norm_w, attn_sinks, final_norm_w.reshape(1, D)),
        (m_norm_w, m_hgrn_lower_bound, m_hgrn_norm_w, m_attn_sinks, m_final_norm_w.reshape(1, D)),
        (v_norm_w, v_hgrn_lower_bound, v_hgrn_norm_w, v_attn_sinks, v_final_norm_w.reshape(1, D)))

    def assemble(win, sq, small):
        nw, lb, hn, sk, fn = small
        return (nw, win.reshape(1, D, SHARD_W), lb, hn, sq[0], sk, sq[1], sq[2], fn.reshape(D))

    return (loss, grad_x.reshape(1, -1, D),
            *assemble(g_win, g_sqs, g8), *assemble(d_win, d_sqs, d8),
            *assemble(nm_win, nm_sqs, nm8), *assemble(nv_win, nv_sqs, nv8))
```
